```python
import jax, jax.numpy as jnp
from jax import lax
import numpy as np

D_MODEL = 1024
BATCH = 8
SEQ = 4096
DEPTH = 2

N_EVEN = (DEPTH + 1) // 2
N_ODD = DEPTH // 2
NORM_EPS = 1e-5

GLA_HEADS = 4
GLA_DK = 64
GLA_DV = 128
GLA_KEY = GLA_HEADS * GLA_DK
GLA_VAL = GLA_HEADS * GLA_DV
GLA_GATE_RANK = 16
GLA_GATE_NORMALIZER = 16.0
GLA_CHUNK = 64

RWKV_HEADS = 8
RWKV_HEAD = 64
RWKV_W = RWKV_HEADS * RWKV_HEAD
RWKV_DECAY_RANK = 64
RWKV_A_RANK = 64
RWKV_LN_EPS = 64e-5
RWKV_SHIFT = 3 * RWKV_W + RWKV_DECAY_RANK + RWKV_A_RANK

MIX0 = GLA_VAL + RWKV_W
SPLIT0 = [GLA_KEY, 2 * GLA_KEY, 2 * GLA_KEY + GLA_VAL,
          2 * GLA_KEY + GLA_VAL + GLA_GATE_RANK,
          2 * GLA_KEY + GLA_VAL + GLA_GATE_RANK + RWKV_SHIFT]
IN0 = 2 * GLA_KEY + GLA_VAL + GLA_GATE_RANK + RWKV_SHIFT + MIX0
SPLIT_RWKV = [RWKV_W, 2 * RWKV_W, 3 * RWKV_W, 3 * RWKV_W + RWKV_DECAY_RANK]

SWA_Q_HEADS = 16
SWA_KV_HEADS = 4
SWA_GROUP = SWA_Q_HEADS // SWA_KV_HEADS
SWA_HEAD = 64
WINDOW = 128
ROPE_DIMS = SWA_HEAD // 4
ROPE_THETA = 500000.0
MIX1 = SWA_Q_HEADS * SWA_HEAD
SWA_KV = SWA_KV_HEADS * SWA_HEAD
SWA_QKV = MIX1 + 2 * SWA_KV
IN1 = SWA_QKV + MIX1

kernel_name = "hybrid_gla_rwkv7_swa_sink_gated"


def rmsnorm(x, w, eps=NORM_EPS):
    xf = x.astype(jnp.float32)
    y = xf * lax.rsqrt(jnp.mean(xf * xf, axis=-1, keepdims=True) + eps)
    return (y * w.astype(jnp.float32)).astype(x.dtype)


def token_shift(t):
    return jnp.concatenate([jnp.zeros_like(t[:, :1]), t[:, :-1]], axis=1)


def gla_chunked(q, k, v, g):
    f32 = jnp.float32
    B, T, H, dk = q.shape
    dv = v.shape[-1]
    C = GLA_CHUNK
    NC = T // C
    q = (q.astype(f32) * (dk ** -0.5)).reshape(B, NC, C, H, dk)
    k = k.astype(f32).reshape(B, NC, C, H, dk)
    v = v.astype(f32).reshape(B, NC, C, H, dv)
    b = jnp.cumsum(g.astype(f32).reshape(B, NC, C, H, dk), axis=2)
    ref = b[:, :, C // 2:C // 2 + 1]
    att = jnp.einsum('bnihd,bnjhd->bnhij', q * jnp.exp(b - ref), k * jnp.exp(ref - b))
    causal = jnp.tril(jnp.ones((C, C), dtype=bool))
    att = jnp.where(causal, att, 0.0)
    o_intra = jnp.einsum('bnhij,bnjhv->bnihv', att, v)
    b_last = b[:, :, -1:]
    chunk_kv = jnp.einsum('bnjhd,bnjhv->nbhdv', k * jnp.exp(b_last - b), v)
    chunk_decay = jnp.exp(b_last[:, :, 0]).transpose(1, 0, 2, 3)

    def step(S, inp):
        kv, dec = inp
        return S * dec[..., None] + kv, S

    _, S_prev = lax.scan(step, jnp.zeros((B, H, dk, dv), f32), (chunk_kv, chunk_decay))
    o_inter = jnp.einsum('bnihd,nbhdv->bnihv', q * jnp.exp(b), S_prev)
    return (o_intra + o_inter).reshape(B, T, H, dv)


def rwkv7_scan(r, decay, k, v, a_vec, b_vec):
    B, T, H, N = r.shape

    def step(S, inp):
        r_t, w_t, k_t, v_t, a_t, b_t = inp
        sa = jnp.einsum('bhvk,bhk->bhv', S, a_t)
        S = S * w_t[:, :, None, :] + sa[..., None] * b_t[:, :, None, :] + v_t[..., None] * k_t[:, :, None, :]
        return S, jnp.einsum('bhvk,bhk->bhv', S, r_t)

    xs = (jnp.moveaxis(r, 1, 0), jnp.moveaxis(decay, 1, 0), jnp.moveaxis(k, 1, 0),
          jnp.moveaxis(v, 1, 0), jnp.moveaxis(a_vec, 1, 0), jnp.moveaxis(b_vec, 1, 0))
    _, out = lax.scan(step, jnp.zeros((B, H, N, N), jnp.float32), xs)
    return jnp.moveaxis(out, 0, 1)


def gla_rwkv_mixer(h, w_in, gk_up, gk_bias, gla_norm_w, mu, w0, w_up, a0, a_up,
                   k_k, k_a, r_k, ln_w, ln_b, w_out):
    f32 = jnp.float32
    B, T, _ = h.shape
    proj = h @ w_in
    gq, gk, gv, g_low, rw, gate = jnp.split(proj, SPLIT0, axis=-1)

    log_decay = jax.nn.log_sigmoid((g_low @ gk_up + gk_bias).astype(f32)) / GLA_GATE_NORMALIZER
    o_a = gla_chunked(gq.reshape(B, T, GLA_HEADS, GLA_DK), gk.reshape(B, T, GLA_HEADS, GLA_DK),
                      gv.reshape(B, T, GLA_HEADS, GLA_DV), log_decay.reshape(B, T, GLA_HEADS, GLA_DK))
    o_a = o_a * lax.rsqrt(jnp.mean(o_a * o_a, axis=-1, keepdims=True) + NORM_EPS) * gla_norm_w.astype(f32)
    o_a = o_a.reshape(B, T, GLA_VAL)

    rw = rw.astype(f32)
    rw = rw + (token_shift(rw) - rw) * mu.astype(f32)
    r, k, v, xw, xa = jnp.split(rw, SPLIT_RWKV, axis=-1)
    w = -jax.nn.softplus(-(w0.astype(f32) + jnp.tanh(xw) @ w_up.astype(f32))) - 0.5
    decay = jnp.exp(-jnp.exp(w))
    a = jax.nn.sigmoid(a0.astype(f32) + xa @ a_up.astype(f32))
    hs = (B, T, RWKV_HEADS, RWKV_HEAD)
    kk = (k * k_k.astype(f32)).reshape(hs)
    kk = kk / jnp.maximum(jnp.sqrt(jnp.sum(kk * kk, axis=-1, keepdims=True)), 1e-12)
    k = k * (1.0 + (a - 1.0) * k_a.astype(f32))
    r, k, v, decay, a = r.reshape(hs), k.reshape(hs), v.reshape(hs), decay.reshape(hs), a.reshape(hs)
    o_b = rwkv7_scan(r, decay, k, v, -kk, kk * a)
    mean = jnp.mean(o_b, axis=-1, keepdims=True)
    var = jnp.mean((o_b - mean) ** 2, axis=-1, keepdims=True)
    o_b = (o_b - mean) * lax.rsqrt(var + RWKV_LN_EPS) * ln_w.astype(f32).reshape(RWKV_HEADS, RWKV_HEAD) \
        + ln_b.astype(f32).reshape(RWKV_HEADS, RWKV_HEAD)
    o_b = o_b + jnp.sum(r * k * r_k.astype(f32), axis=-1, keepdims=True) * v
    o_b = o_b.reshape(B, T, RWKV_W)

    o = jnp.concatenate([o_a, o_b], axis=-1) * jax.nn.silu(gate.astype(f32))
    return (o.astype(h.dtype) @ w_out).astype(h.dtype)


def partial_rope(x, pos):
    f32 = jnp.float32
    half = ROPE_DIMS // 2
    inv_freq = ROPE_THETA ** (-jnp.arange(half, dtype=f32) / half)
    ang = pos.astype(f32)[:, None] * inv_freq
    cos = jnp.cos(ang)[None, :, None, :]
    sin = jnp.sin(ang)[None, :, None, :]
    xr = x[..., :ROPE_DIMS].astype(f32)
    x1, x2 = xr[..., :half], xr[..., half:]
    rot = jnp.concatenate([x1 * cos - x2 * sin, x2 * cos + x1 * sin], axis=-1)
    return jnp.concatenate([rot.astype(x.dtype), x[..., ROPE_DIMS:]], axis=-1)


def banded_sink_attention(q, k, v, sinks):
    f32 = jnp.float32
    B, T, _, hd = q.shape
    NB = T // WINDOW
    qb = q.astype(f32).reshape(B, NB, WINDOW, SWA_KV_HEADS, SWA_GROUP, hd) * (hd ** -0.5)

    def two_blocks(t):
        tb = t.astype(f32).reshape(B, NB, WINDOW, SWA_KV_HEADS, hd)
        prev = jnp.pad(tb[:, :-1], ((0, 0), (1, 0), (0, 0), (0, 0), (0, 0)))
        return jnp.concatenate([prev, tb], axis=2)

    kb, vb = two_blocks(k), two_blocks(v)
    s = jnp.einsum('bnqhgd,bnkhd->bnhgqk', qb, kb)
    qi = jnp.arange(WINDOW)[:, None]
    kj = jnp.arange(2 * WINDOW)[None, :]
    diff = qi + WINDOW - kj
    band = (diff >= 0) & (diff < WINDOW)
    valid = band[None] & ((jnp.arange(NB)[:, None, None] > 0) | (kj[None] >= WINDOW))
    s = jnp.where(valid[None, :, None, None], s, -jnp.inf)
    sink = sinks.astype(f32).reshape(SWA_KV_HEADS, SWA_GROUP)[None, None, :, :, None, None]
    m = jnp.maximum(jnp.max(s, axis=-1, keepdims=True), sink)
    p = jnp.exp(s - m)
    denom = jnp.sum(p, axis=-1, keepdims=True) + jnp.exp(sink - m)
    o = jnp.einsum('bnhgqk,bnkhd->bnqhgd', p / denom, vb)
    return o.reshape(B, T, SWA_Q_HEADS, hd)


def swa_mixer(h, w_in, b_in, sinks, w_out, b_out):
    f32 = jnp.float32
    B, T, _ = h.shape
    proj = h @ w_in
    qkv = proj[..., :SWA_QKV] + b_in
    gate = proj[..., SWA_QKV:]
    q, k, v = jnp.split(qkv, [MIX1, MIX1 + SWA_KV], axis=-1)
    pos = jnp.arange(T)
    q = partial_rope(q.reshape(B, T, SWA_Q_HEADS, SWA_HEAD), pos)
    k = partial_rope(k.reshape(B, T, SWA_KV_HEADS, SWA_HEAD), pos)
    v = v.reshape(B, T, SWA_KV_HEADS, SWA_HEAD)
    o = banded_sink_attention(q, k, v, sinks).reshape(B, T, MIX1)
    o = o * jax.nn.silu(gate.astype(f32))
    return (o.astype(h.dtype) @ w_out + b_out).astype(h.dtype)


def _fwd_setup_inputs(seed: int = 0) -> dict:
    key = jax.random.key(seed)
    ks = jax.random.split(key, 24)
    f32 = jnp.float32
    nrm = lambda k, s: jax.random.normal(k, s, f32)
    return {
        "x": nrm(ks[0], (BATCH, SEQ, D_MODEL)),
        "norm_w": 1.0 + 0.02 * nrm(ks[1], (DEPTH, D_MODEL)),
        "w_in0": nrm(ks[2], (N_EVEN, D_MODEL, IN0)) * D_MODEL ** -0.5,
        "gla_gk_up": nrm(ks[3], (N_EVEN, GLA_GATE_RANK, GLA_KEY)) * GLA_GATE_RANK ** -0.5,
        "gla_gk_bias": 0.1 * nrm(ks[4], (N_EVEN, GLA_KEY)),
        "gla_norm_w": 1.0 + 0.02 * nrm(ks[5], (N_EVEN, GLA_DV)),
        "rwkv_mu": jax.random.uniform(ks[6], (N_EVEN, RWKV_SHIFT), f32),
        "rwkv_w0": jax.random.uniform(ks[7], (N_EVEN, RWKV_W), f32, -4.0, 1.0),
        "rwkv_w_up": nrm(ks[8], (N_EVEN, RWKV_DECAY_RANK, RWKV_W)) * 0.5 * RWKV_DECAY_RANK ** -0.5,
        "rwkv_a0": 0.2 * nrm(ks[9], (N_EVEN, RWKV_W)),
        "rwkv_a_up": nrm(ks[10], (N_EVEN, RWKV_A_RANK, RWKV_W)) * 0.5 * RWKV_A_RANK ** -0.5,
        "rwkv_k_k": 0.85 + 0.05 * nrm(ks[11], (N_EVEN, RWKV_W)),
        "rwkv_k_a": 1.0 + 0.05 * nrm(ks[12], (N_EVEN, RWKV_W)),
        "rwkv_r_k": 0.1 * nrm(ks[13], (N_EVEN, RWKV_HEADS, RWKV_HEAD)),
        "rwkv_ln_w": 1.0 + 0.02 * nrm(ks[14], (N_EVEN, RWKV_W)),
        "rwkv_ln_b": 0.02 * nrm(ks[15], (N_EVEN, RWKV_W)),
        "w_out0": nrm(ks[16], (N_EVEN, MIX0, D_MODEL)) * MIX0 ** -0.5,
        "w_in1": nrm(ks[17], (N_ODD, D_MODEL, IN1)) * D_MODEL ** -0.5,
        "b_in1": 0.02 * nrm(ks[18], (N_ODD, SWA_QKV)),
        "attn_sinks": 0.5 * nrm(ks[19], (N_ODD, SWA_Q_HEADS)),
        "w_out1": nrm(ks[20], (N_ODD, MIX1, D_MODEL)) * MIX1 ** -0.5,
        "b_out1": 0.02 * nrm(ks[21], (N_ODD, D_MODEL)),
        "final_norm_w": 1.0 + 0.02 * nrm(ks[22], (D_MODEL,)),
    }


def _fwd_reference(x, norm_w, w_in0, gla_gk_up, gla_gk_bias, gla_norm_w, rwkv_mu, rwkv_w0, rwkv_w_up,
              rwkv_a0, rwkv_a_up, rwkv_k_k, rwkv_k_a, rwkv_r_k, rwkv_ln_w, rwkv_ln_b, w_out0,
              w_in1, b_in1, attn_sinks, w_out1, b_out1, final_norm_w):
    h = x
    for layer in range(DEPTH):
        i = layer // 2
        hn = rmsnorm(h, norm_w[layer])
        if layer % 2 == 0:
            h = h + gla_rwkv_mixer(hn, w_in0[i], gla_gk_up[i], gla_gk_bias[i], gla_norm_w[i],
                                   rwkv_mu[i], rwkv_w0[i], rwkv_w_up[i], rwkv_a0[i], rwkv_a_up[i],
                                   rwkv_k_k[i], rwkv_k_a[i], rwkv_r_k[i], rwkv_ln_w[i], rwkv_ln_b[i],
                                   w_out0[i])
        else:
            h = h + swa_mixer(hn, w_in1[i], b_in1[i], attn_sinks[i], w_out1[i], b_out1[i])
    return rmsnorm(h, final_norm_w)


import jax as _jax
import jax.numpy as _jnp

TWIN_FORMAT = 'train_step'
FWD_PARAMS = ['x', 'norm_w', 'w_in0', 'gla_gk_up', 'gla_gk_bias', 'gla_norm_w', 'rwkv_mu', 'rwkv_w0', 'rwkv_w_up', 'rwkv_a0', 'rwkv_a_up', 'rwkv_k_k', 'rwkv_k_a', 'rwkv_r_k', 'rwkv_ln_w', 'rwkv_ln_b', 'w_out0', 'w_in1', 'b_in1', 'attn_sinks', 'w_out1', 'b_out1', 'final_norm_w']
TWIN_WEIGHTS = ['norm_w', 'w_in0', 'gla_gk_up', 'gla_gk_bias', 'gla_norm_w', 'rwkv_mu', 'rwkv_w0', 'rwkv_w_up', 'rwkv_a0', 'rwkv_a_up', 'rwkv_k_k', 'rwkv_k_a', 'rwkv_r_k', 'rwkv_ln_w', 'rwkv_ln_b', 'w_out0', 'w_in1', 'b_in1', 'attn_sinks', 'w_out1', 'b_out1', 'final_norm_w']
TWIN_DIFF_INPUT = 'x'
TWIN_INPUTS = ['x', 'norm_w', 'w_in0', 'gla_gk_up', 'gla_gk_bias', 'gla_norm_w', 'rwkv_mu', 'rwkv_w0', 'rwkv_w_up', 'rwkv_a0', 'rwkv_a_up', 'rwkv_k_k', 'rwkv_k_a', 'rwkv_r_k', 'rwkv_ln_w', 'rwkv_ln_b', 'w_out0', 'w_in1', 'b_in1', 'attn_sinks', 'w_out1', 'b_out1', 'final_norm_w', 'loss_target', 'm_norm_w', 'm_w_in0', 'm_gla_gk_up', 'm_gla_gk_bias', 'm_gla_norm_w', 'm_rwkv_mu', 'm_rwkv_w0', 'm_rwkv_w_up', 'm_rwkv_a0', 'm_rwkv_a_up', 'm_rwkv_k_k', 'm_rwkv_k_a', 'm_rwkv_r_k', 'm_rwkv_ln_w', 'm_rwkv_ln_b', 'm_w_out0', 'm_w_in1', 'm_b_in1', 'm_attn_sinks', 'm_w_out1', 'm_b_out1', 'm_final_norm_w', 'v_norm_w', 'v_w_in0', 'v_gla_gk_up', 'v_gla_gk_bias', 'v_gla_norm_w', 'v_rwkv_mu', 'v_rwkv_w0', 'v_rwkv_w_up', 'v_rwkv_a0', 'v_rwkv_a_up', 'v_rwkv_k_k', 'v_rwkv_k_a', 'v_rwkv_r_k', 'v_rwkv_ln_w', 'v_rwkv_ln_b', 'v_w_out0', 'v_w_in1', 'v_b_in1', 'v_attn_sinks', 'v_w_out1', 'v_b_out1', 'v_final_norm_w']
TWIN_OUTPUTS = ['loss', 'grad_x', 'grad_norm_w', 'grad_w_in0', 'grad_gla_gk_up', 'grad_gla_gk_bias', 'grad_gla_norm_w', 'grad_rwkv_mu', 'grad_rwkv_w0', 'grad_rwkv_w_up', 'grad_rwkv_a0', 'grad_rwkv_a_up', 'grad_rwkv_k_k', 'grad_rwkv_k_a', 'grad_rwkv_r_k', 'grad_rwkv_ln_w', 'grad_rwkv_ln_b', 'grad_w_out0', 'grad_w_in1', 'grad_b_in1', 'grad_attn_sinks', 'grad_w_out1', 'grad_b_out1', 'grad_final_norm_w', 'delta_norm_w', 'delta_w_in0', 'delta_gla_gk_up', 'delta_gla_gk_bias', 'delta_gla_norm_w', 'delta_rwkv_mu', 'delta_rwkv_w0', 'delta_rwkv_w_up', 'delta_rwkv_a0', 'delta_rwkv_a_up', 'delta_rwkv_k_k', 'delta_rwkv_k_a', 'delta_rwkv_r_k', 'delta_rwkv_ln_w', 'delta_rwkv_ln_b', 'delta_w_out0', 'delta_w_in1', 'delta_b_in1', 'delta_attn_sinks', 'delta_w_out1', 'delta_b_out1', 'delta_final_norm_w', 'new_m_norm_w', 'new_m_w_in0', 'new_m_gla_gk_up', 'new_m_gla_gk_bias', 'new_m_gla_norm_w', 'new_m_rwkv_mu', 'new_m_rwkv_w0', 'new_m_rwkv_w_up', 'new_m_rwkv_a0', 'new_m_rwkv_a_up', 'new_m_rwkv_k_k', 'new_m_rwkv_k_a', 'new_m_rwkv_r_k', 'new_m_rwkv_ln_w', 'new_m_rwkv_ln_b', 'new_m_w_out0', 'new_m_w_in1', 'new_m_b_in1', 'new_m_attn_sinks', 'new_m_w_out1', 'new_m_b_out1', 'new_m_final_norm_w', 'new_v_norm_w', 'new_v_w_in0', 'new_v_gla_gk_up', 'new_v_gla_gk_bias', 'new_v_gla_norm_w', 'new_v_rwkv_mu', 'new_v_rwkv_w0', 'new_v_rwkv_w_up', 'new_v_rwkv_a0', 'new_v_rwkv_a_up', 'new_v_rwkv_k_k', 'new_v_rwkv_k_a', 'new_v_rwkv_r_k', 'new_v_rwkv_ln_w', 'new_v_rwkv_ln_b', 'new_v_w_out0', 'new_v_w_in1', 'new_v_b_in1', 'new_v_attn_sinks', 'new_v_w_out1', 'new_v_b_out1', 'new_v_final_norm_w']
TWIN_LEAF_KINDS = {'loss': 'loss', 'grad_x': 'grad_x', 'grad_norm_w': 'grad_w', 'grad_w_in0': 'grad_w', 'grad_gla_gk_up': 'grad_w', 'grad_gla_gk_bias': 'grad_w', 'grad_gla_norm_w': 'grad_w', 'grad_rwkv_mu': 'grad_w', 'grad_rwkv_w0': 'grad_w', 'grad_rwkv_w_up': 'grad_w', 'grad_rwkv_a0': 'grad_w', 'grad_rwkv_a_up': 'grad_w', 'grad_rwkv_k_k': 'grad_w', 'grad_rwkv_k_a': 'grad_w', 'grad_rwkv_r_k': 'grad_w', 'grad_rwkv_ln_w': 'grad_w', 'grad_rwkv_ln_b': 'grad_w', 'grad_w_out0': 'grad_w', 'grad_w_in1': 'grad_w', 'grad_b_in1': 'grad_w', 'grad_attn_sinks': 'grad_w', 'grad_w_out1': 'grad_w', 'grad_b_out1': 'grad_w', 'grad_final_norm_w': 'grad_w', 'delta_norm_w': 'delta_w', 'delta_w_in0': 'delta_w', 'delta_gla_gk_up': 'delta_w', 'delta_gla_gk_bias': 'delta_w', 'delta_gla_norm_w': 'delta_w', 'delta_rwkv_mu': 'delta_w', 'delta_rwkv_w0': 'delta_w', 'delta_rwkv_w_up': 'delta_w', 'delta_rwkv_a0': 'delta_w', 'delta_rwkv_a_up': 'delta_w', 'delta_rwkv_k_k': 'delta_w', 'delta_rwkv_k_a': 'delta_w', 'delta_rwkv_r_k': 'delta_w', 'delta_rwkv_ln_w': 'delta_w', 'delta_rwkv_ln_b': 'delta_w', 'delta_w_out0': 'delta_w', 'delta_w_in1': 'delta_w', 'delta_b_in1': 'delta_w', 'delta_attn_sinks': 'delta_w', 'delta_w_out1': 'delta_w', 'delta_b_out1': 'delta_w', 'delta_final_norm_w': 'delta_w', 'new_m_norm_w': 'new_m', 'new_m_w_in0': 'new_m', 'new_m_gla_gk_up': 'new_m', 'new_m_gla_gk_bias': 'new_m', 'new_m_gla_norm_w': 'new_m', 'new_m_rwkv_mu': 'new_m', 'new_m_rwkv_w0': 'new_m', 'new_m_rwkv_w_up': 'new_m', 'new_m_rwkv_a0': 'new_m', 'new_m_rwkv_a_up': 'new_m', 'new_m_rwkv_k_k': 'new_m', 'new_m_rwkv_k_a': 'new_m', 'new_m_rwkv_r_k': 'new_m', 'new_m_rwkv_ln_w': 'new_m', 'new_m_rwkv_ln_b': 'new_m', 'new_m_w_out0': 'new_m', 'new_m_w_in1': 'new_m', 'new_m_b_in1': 'new_m', 'new_m_attn_sinks': 'new_m', 'new_m_w_out1': 'new_m', 'new_m_b_out1': 'new_m', 'new_m_final_norm_w': 'new_m', 'new_v_norm_w': 'new_v', 'new_v_w_in0': 'new_v', 'new_v_gla_gk_up': 'new_v', 'new_v_gla_gk_bias': 'new_v', 'new_v_gla_norm_w': 'new_v', 'new_v_rwkv_mu': 'new_v', 'new_v_rwkv_w0': 'new_v', 'new_v_rwkv_w_up': 'new_v', 'new_v_rwkv_a0': 'new_v', 'new_v_rwkv_a_up': 'new_v', 'new_v_rwkv_k_k': 'new_v', 'new_v_rwkv_k_a': 'new_v', 'new_v_rwkv_r_k': 'new_v', 'new_v_rwkv_ln_w': 'new_v', 'new_v_rwkv_ln_b': 'new_v', 'new_v_w_out0': 'new_v', 'new_v_w_in1': 'new_v', 'new_v_b_in1': 'new_v', 'new_v_attn_sinks': 'new_v', 'new_v_w_out1': 'new_v', 'new_v_b_out1': 'new_v', 'new_v_final_norm_w': 'new_v'}


def _forward(args):
    return _fwd_reference(*[args[k] for k in FWD_PARAMS])


def _output_shape():
    def fwd():
        inp = _fwd_setup_inputs(0)
        return _fwd_reference(*[inp[k] for k in FWD_PARAMS])
    out = _jax.eval_shape(fwd)
    return out.shape, out.dtype

N_MICROBATCH = 1
ADAM_LR = 0.001
ADAM_B1 = 0.9
ADAM_B2 = 0.999
ADAM_EPS = 1e-08
ADAM_WD = 0.01
ADAM_STEP = 10
PER_EXAMPLE_BATCH_AXIS = {'x': 0, 'loss_target': 0}
SHARED_INPUTS = []
_WEIGHT_DTYPES = {'norm_w': _jnp.float32, 'w_in0': _jnp.float32, 'gla_gk_up': _jnp.float32, 'gla_gk_bias': _jnp.float32, 'gla_norm_w': _jnp.float32, 'rwkv_mu': _jnp.float32, 'rwkv_w0': _jnp.float32, 'rwkv_w_up': _jnp.float32, 'rwkv_a0': _jnp.float32, 'rwkv_a_up': _jnp.float32, 'rwkv_k_k': _jnp.float32, 'rwkv_k_a': _jnp.float32, 'rwkv_r_k': _jnp.float32, 'rwkv_ln_w': _jnp.float32, 'rwkv_ln_b': _jnp.float32, 'w_out0': _jnp.float32, 'w_in1': _jnp.float32, 'b_in1': _jnp.float32, 'attn_sinks': _jnp.float32, 'w_out1': _jnp.float32, 'b_out1': _jnp.float32, 'final_norm_w': _jnp.float32}
MOMENT_SCALE = {'norm_w': 1.371850e-01, 'w_in0': 1.013970e-01, 'gla_gk_up': 1.413695e-02, 'gla_gk_bias': 5.886666e-02, 'gla_norm_w': 1.902619e-01, 'rwkv_mu': 1.565895e-01, 'rwkv_w0': 4.944019e-02, 'rwkv_w_up': 7.572205e-03, 'rwkv_a0': 4.119994e-02, 'rwkv_a_up': 3.745179e-02, 'rwkv_k_k': 5.045652e-02, 'rwkv_k_a': 1.021779e-01, 'rwkv_r_k': 2.171598e-01, 'rwkv_ln_w': 9.430812e-02, 'rwkv_ln_b': 9.895601e-02, 'w_out0': 9.260245e-02, 'w_in1': 2.210580e-02, 'b_in1': 7.276732e-02, 'attn_sinks': 2.102499e-02, 'w_out1': 1.852336e-02, 'b_out1': 1.594477e-01, 'final_norm_w': 3.201107e+01}


def _to_microbatches(a, axis):
    t = _jnp.moveaxis(a, axis, 0)
    t = t.reshape((N_MICROBATCH, t.shape[0] // N_MICROBATCH) + t.shape[1:])
    return _jnp.moveaxis(t, 1, axis + 1)


def setup_inputs(seed: int = 0) -> dict:
    inp = _fwd_setup_inputs(seed)
    key = _jax.random.fold_in(_jax.random.key(seed), 7919)
    shape, _ = _output_shape()
    out = dict(inp)
    out["loss_target"] = _jax.random.normal(_jax.random.fold_in(key, 0), shape, _jnp.float32)
    for i, name in enumerate(TWIN_WEIGHTS):
        w = inp[name].astype(_jnp.float32)
        if MOMENT_SCALE is None:
            s = _jnp.sqrt(_jnp.mean(_jnp.square(w)) + 1e-30)
        else:
            s = MOMENT_SCALE[name]
        km, kv = _jax.random.split(_jax.random.fold_in(key, i + 1))
        out[name] = w
        out["m_" + name] = s * _jax.random.normal(km, w.shape, _jnp.float32)
        out["v_" + name] = (s * s) * _jax.random.uniform(kv, w.shape, _jnp.float32, 0.5, 1.5)
    if N_MICROBATCH > 1:
        for name, axis in PER_EXAMPLE_BATCH_AXIS.items():
            out[name] = _to_microbatches(out[name], axis)
    return {'x': out['x'], 'norm_w': out['norm_w'], 'w_in0': out['w_in0'], 'gla_gk_up': out['gla_gk_up'], 'gla_gk_bias': out['gla_gk_bias'], 'gla_norm_w': out['gla_norm_w'], 'rwkv_mu': out['rwkv_mu'], 'rwkv_w0': out['rwkv_w0'], 'rwkv_w_up': out['rwkv_w_up'], 'rwkv_a0': out['rwkv_a0'], 'rwkv_a_up': out['rwkv_a_up'], 'rwkv_k_k': out['rwkv_k_k'], 'rwkv_k_a': out['rwkv_k_a'], 'rwkv_r_k': out['rwkv_r_k'], 'rwkv_ln_w': out['rwkv_ln_w'], 'rwkv_ln_b': out['rwkv_ln_b'], 'w_out0': out['w_out0'], 'w_in1': out['w_in1'], 'b_in1': out['b_in1'], 'attn_sinks': out['attn_sinks'], 'w_out1': out['w_out1'], 'b_out1': out['b_out1'], 'final_norm_w': out['final_norm_w'], 'loss_target': out['loss_target'], 'm_norm_w': out['m_norm_w'], 'm_w_in0': out['m_w_in0'], 'm_gla_gk_up': out['m_gla_gk_up'], 'm_gla_gk_bias': out['m_gla_gk_bias'], 'm_gla_norm_w': out['m_gla_norm_w'], 'm_rwkv_mu': out['m_rwkv_mu'], 'm_rwkv_w0': out['m_rwkv_w0'], 'm_rwkv_w_up': out['m_rwkv_w_up'], 'm_rwkv_a0': out['m_rwkv_a0'], 'm_rwkv_a_up': out['m_rwkv_a_up'], 'm_rwkv_k_k': out['m_rwkv_k_k'], 'm_rwkv_k_a': out['m_rwkv_k_a'], 'm_rwkv_r_k': out['m_rwkv_r_k'], 'm_rwkv_ln_w': out['m_rwkv_ln_w'], 'm_rwkv_ln_b': out['m_rwkv_ln_b'], 'm_w_out0': out['m_w_out0'], 'm_w_in1': out['m_w_in1'], 'm_b_in1': out['m_b_in1'], 'm_attn_sinks': out['m_attn_sinks'], 'm_w_out1': out['m_w_out1'], 'm_b_out1': out['m_b_out1'], 'm_final_norm_w': out['m_final_norm_w'], 'v_norm_w': out['v_norm_w'], 'v_w_in0': out['v_w_in0'], 'v_gla_gk_up': out['v_gla_gk_up'], 'v_gla_gk_bias': out['v_gla_gk_bias'], 'v_gla_norm_w': out['v_gla_norm_w'], 'v_rwkv_mu': out['v_rwkv_mu'], 'v_rwkv_w0': out['v_rwkv_w0'], 'v_rwkv_w_up': out['v_rwkv_w_up'], 'v_rwkv_a0': out['v_rwkv_a0'], 'v_rwkv_a_up': out['v_rwkv_a_up'], 'v_rwkv_k_k': out['v_rwkv_k_k'], 'v_rwkv_k_a': out['v_rwkv_k_a'], 'v_rwkv_r_k': out['v_rwkv_r_k'], 'v_rwkv_ln_w': out['v_rwkv_ln_w'], 'v_rwkv_ln_b': out['v_rwkv_ln_b'], 'v_w_out0': out['v_w_out0'], 'v_w_in1': out['v_w_in1'], 'v_b_in1': out['v_b_in1'], 'v_attn_sinks': out['v_attn_sinks'], 'v_w_out1': out['v_w_out1'], 'v_b_out1': out['v_b_out1'], 'v_final_norm_w': out['v_final_norm_w']}


def _loss(weights, diff, rest, loss_target):
    with _jax.named_scope("forward"):
        args = {**rest, TWIN_DIFF_INPUT: diff, **{k: w.astype(_WEIGHT_DTYPES[k]) for k, w in weights.items()}}
        y = _forward(args)
    with _jax.named_scope("loss_head"):
        err = _jnp.square(y.astype(_jnp.float32) - loss_target)
        return 0.5 * _jnp.sum(_jnp.mean(err, axis=-1)) if err.ndim else 0.5 * err


def _adamw(w, g, m, v):
    m = ADAM_B1 * m + (1.0 - ADAM_B1) * g
    v = ADAM_B2 * v + (1.0 - ADAM_B2) * _jnp.square(g)
    m_hat = m / (1.0 - ADAM_B1 ** ADAM_STEP)
    v_hat = v / (1.0 - ADAM_B2 ** ADAM_STEP)
    delta = -ADAM_LR * (m_hat / (_jnp.sqrt(v_hat) + ADAM_EPS) + ADAM_WD * w)
    return delta, m, v


def reference(x, norm_w, w_in0, gla_gk_up, gla_gk_bias, gla_norm_w, rwkv_mu, rwkv_w0, rwkv_w_up, rwkv_a0, rwkv_a_up, rwkv_k_k, rwkv_k_a, rwkv_r_k, rwkv_ln_w, rwkv_ln_b, w_out0, w_in1, b_in1, attn_sinks, w_out1, b_out1, final_norm_w, loss_target, m_norm_w, m_w_in0, m_gla_gk_up, m_gla_gk_bias, m_gla_norm_w, m_rwkv_mu, m_rwkv_w0, m_rwkv_w_up, m_rwkv_a0, m_rwkv_a_up, m_rwkv_k_k, m_rwkv_k_a, m_rwkv_r_k, m_rwkv_ln_w, m_rwkv_ln_b, m_w_out0, m_w_in1, m_b_in1, m_attn_sinks, m_w_out1, m_b_out1, m_final_norm_w, v_norm_w, v_w_in0, v_gla_gk_up, v_gla_gk_bias, v_gla_norm_w, v_rwkv_mu, v_rwkv_w0, v_rwkv_w_up, v_rwkv_a0, v_rwkv_a_up, v_rwkv_k_k, v_rwkv_k_a, v_rwkv_r_k, v_rwkv_ln_w, v_rwkv_ln_b, v_w_out0, v_w_in1, v_b_in1, v_attn_sinks, v_w_out1, v_b_out1, v_final_norm_w):
    given = dict(x=x, norm_w=norm_w, w_in0=w_in0, gla_gk_up=gla_gk_up, gla_gk_bias=gla_gk_bias, gla_norm_w=gla_norm_w, rwkv_mu=rwkv_mu, rwkv_w0=rwkv_w0, rwkv_w_up=rwkv_w_up, rwkv_a0=rwkv_a0, rwkv_a_up=rwkv_a_up, rwkv_k_k=rwkv_k_k, rwkv_k_a=rwkv_k_a, rwkv_r_k=rwkv_r_k, rwkv_ln_w=rwkv_ln_w, rwkv_ln_b=rwkv_ln_b, w_out0=w_out0, w_in1=w_in1, b_in1=b_in1, attn_sinks=attn_sinks, w_out1=w_out1, b_out1=b_out1, final_norm_w=final_norm_w, loss_target=loss_target, m_norm_w=m_norm_w, m_w_in0=m_w_in0, m_gla_gk_up=m_gla_gk_up, m_gla_gk_bias=m_gla_gk_bias, m_gla_norm_w=m_gla_norm_w, m_rwkv_mu=m_rwkv_mu, m_rwkv_w0=m_rwkv_w0, m_rwkv_w_up=m_rwkv_w_up, m_rwkv_a0=m_rwkv_a0, m_rwkv_a_up=m_rwkv_a_up, m_rwkv_k_k=m_rwkv_k_k, m_rwkv_k_a=m_rwkv_k_a, m_rwkv_r_k=m_rwkv_r_k, m_rwkv_ln_w=m_rwkv_ln_w, m_rwkv_ln_b=m_rwkv_ln_b, m_w_out0=m_w_out0, m_w_in1=m_w_in1, m_b_in1=m_b_in1, m_attn_sinks=m_attn_sinks, m_w_out1=m_w_out1, m_b_out1=m_b_out1, m_final_norm_w=m_final_norm_w, v_norm_w=v_norm_w, v_w_in0=v_w_in0, v_gla_gk_up=v_gla_gk_up, v_gla_gk_bias=v_gla_gk_bias, v_gla_norm_w=v_gla_norm_w, v_rwkv_mu=v_rwkv_mu, v_rwkv_w0=v_rwkv_w0, v_rwkv_w_up=v_rwkv_w_up, v_rwkv_a0=v_rwkv_a0, v_rwkv_a_up=v_rwkv_a_up, v_rwkv_k_k=v_rwkv_k_k, v_rwkv_k_a=v_rwkv_k_a, v_rwkv_r_k=v_rwkv_r_k, v_rwkv_ln_w=v_rwkv_ln_w, v_rwkv_ln_b=v_rwkv_ln_b, v_w_out0=v_w_out0, v_w_in1=v_w_in1, v_b_in1=v_b_in1, v_attn_sinks=v_attn_sinks, v_w_out1=v_w_out1, v_b_out1=v_b_out1, v_final_norm_w=v_final_norm_w)
    weights = {n: given[n] for n in TWIN_WEIGHTS}
    shared = {n: given[n] for n in SHARED_INPUTS}
    per_example = {n: given[n] for n in ['x']}
    grad_fn = _jax.value_and_grad(_loss, argnums=(0, 1))

    def one_microbatch(ex, loss_target):
        ex = dict(ex)
        diff = ex.pop(TWIN_DIFF_INPUT)
        return grad_fn(weights, diff, {**shared, **ex}, loss_target)

    if N_MICROBATCH == 1:
        loss, (grad_w, grad_x) = one_microbatch(per_example, given["loss_target"])
    else:
        def body(carry, xs):
            loss_sum, grad_sum = carry
            l_k, (gw_k, gx_k) = one_microbatch(xs[0], xs[1])
            with _jax.named_scope("update"):
                return (loss_sum + l_k, _jax.tree.map(_jnp.add, grad_sum, gw_k)), gx_k

        init = (_jnp.zeros((), _jnp.float32), _jax.tree.map(_jnp.zeros_like, weights))
        (loss, grad_w), grad_x = _jax.lax.scan(body, init, (per_example, given["loss_target"]))
    with _jax.named_scope("update"):
        delta_w, new_m, new_v = {}, {}, {}
        for n in TWIN_WEIGHTS:
            delta_w[n], new_m[n], new_v[n] = _adamw(weights[n], grad_w[n], given["m_" + n], given["v_" + n])
    return (loss, grad_x, *[grad_w[n] for n in TWIN_WEIGHTS], *[delta_w[n] for n in TWIN_WEIGHTS],
            *[new_m[n] for n in TWIN_WEIGHTS], *[new_v[n] for n in TWIN_WEIGHTS])
```

```python
import functools

import jax
import jax.numpy as jnp
from jax import lax
from jax.experimental import pallas as pl
from jax.experimental.pallas import tpu as pltpu

F32 = jnp.float32
BF16 = jnp.bfloat16
HI = lax.Precision.HIGHEST

D_MODEL = 1024
NORM_EPS = 1e-5
GLA_HEADS, GLA_DK, GLA_DV = 4, 64, 128
GLA_NORMALIZER = 16.0
GLA_CHUNK = 64
RWKV_HEADS, RWKV_N = 8, 64
RWKV_LN_EPS = 64e-5
RWKV_CHUNK = 64
SWA_Q_HEADS, SWA_KV_HEADS, SWA_GROUP, SWA_HD = 16, 4, 4, 64
WINDOW = 128
ROPE_THETA = 500000.0
NEG = -1e30
N_DEV = 8
LANES = 128

ADAM_LR, ADAM_B1, ADAM_B2, ADAM_EPS, ADAM_WD, ADAM_STEP = 0.001, 0.9, 0.999, 1e-08, 0.01, 10

N0P = 4096
C0 = dict(gate=(0, 1024), gv=(1024, 512), r=(1536, 512), k=(2048, 512), v=(2560, 512), gq=(3072, 256), gk=(3328, 256),
          glow=(3584, 128), xw=(3712, 128), xa=(3840, 128))
N1P = 2560
C1 = dict(gate=(0, 1024), q=(1024, 1024), k=(2048, 256), v=(2304, 256))

VMEM_LIMIT = 56 * 1024 * 1024


def _cparams(sem=None):
    return pltpu.CompilerParams(dimension_semantics=sem, vmem_limit_bytes=VMEM_LIMIT)


def mm(a, b):
    return jnp.dot(a, b, precision=HI, preferred_element_type=F32)


def mm_nt(a, b):
    return lax.dot_general(a, b, (((1,), (1,)), ((), ())), precision=HI, preferred_element_type=F32)


def mm_tn(a, b):
    return lax.dot_general(a, b, (((0,), (0,)), ((), ())), precision=HI, preferred_element_type=F32)


def _iota2(n, m):
    return lax.broadcasted_iota(jnp.int32, (n, m), 0), lax.broadcasted_iota(jnp.int32, (n, m), 1)


def tril_ones(c, strict=False):
    i, j = _iota2(c, c)
    return jnp.where((i > j) if strict else (i >= j), 1.0, 0.0).astype(F32)


def row_of(x, r):
    i = lax.broadcasted_iota(jnp.int32, x.shape, 0)
    return jnp.sum(jnp.where(i == r, x, 0.0), axis=0, keepdims=True)


def shift_rows(x, prev):
    c = x.shape[0]
    i, j = _iota2(c, c)
    sh = jnp.where(i == j + 1, 1.0, 0.0).astype(F32)
    r = lax.broadcasted_iota(jnp.int32, x.shape, 0)
    return mm(sh, x) + jnp.where(r == 0, prev, 0.0)


def log_sigmoid(x):
    return jnp.minimum(x, 0.0) - jnp.log(1.0 + jnp.exp(-jnp.abs(x)))


def softplus(x):
    return jnp.maximum(x, 0.0) + jnp.log(1.0 + jnp.exp(-jnp.abs(x)))


def sigmoid(x):
    return 1.0 / (1.0 + jnp.exp(-x))


def rms(x, w, eps=NORM_EPS):
    return x * lax.rsqrt(jnp.mean(x * x, axis=-1, keepdims=True) + eps) * w


def gla_chunk(state, toks, params):
    q, k, v, glow = toks
    gk_up, bias, norm_w = params
    c = glow.shape[0]
    ltri = tril_ones(c)
    outs, new_state = [], []
    for h in range(GLA_HEADS):
        g = log_sigmoid(mm(glow, gk_up[h]) + bias[h]) / GLA_NORMALIZER
        b = mm(ltri, g)
        ref = lax.stop_gradient(row_of(b, c // 2))
        last = row_of(b, c - 1)
        qs = q[h] * (GLA_DK ** -0.5)
        att = mm_nt(qs * jnp.exp(b - ref), k[h] * jnp.exp(ref - b)) * ltri
        o = mm(att, v[h]) + mm_nt(qs * jnp.exp(b), state[h])
        s1 = state[h] * jnp.exp(last) + mm_tn(v[h], k[h] * jnp.exp(last - b))
        o = o * lax.rsqrt(jnp.mean(o * o, axis=-1, keepdims=True) + NORM_EPS) * norm_w
        outs.append(o)
        new_state.append(s1)
    return outs, new_state


def rwkv_chunk(state, toks, params):
    S, pr, pk, pv, pxw, pxa = state
    r_, k_, v_, xw_, xa_ = toks
    mu_r, mu_k, mu_v, mu_xw, mu_xa, w0, w_up, a0, a_up, k_k, k_a, r_k, ln_w, ln_b = params
    c = xw_.shape[0]
    ltri = tril_ones(c)
    stri = tril_ones(c, strict=True)

    def lerp(x, prev, mu):
        return x + (shift_rows(x, prev) - x) * mu

    xw = jnp.tanh(lerp(xw_, pxw, mu_xw))
    xa = lerp(xa_, pxa, mu_xa)
    outs, S1 = [], []
    for h in range(RWKV_HEADS):
        r = lerp(r_[h], pr[h], mu_r[h])
        k = lerp(k_[h], pk[h], mu_k[h])
        v = lerp(v_[h], pv[h], mu_v[h])
        w = -softplus(-(w0[h] + mm(xw, w_up[h]))) - 0.5
        lw = -jnp.exp(w)
        asig = sigmoid(a0[h] + mm(xa, a_up[h]))
        kk = k * k_k[h]
        kk = kk / jnp.maximum(jnp.sqrt(jnp.sum(kk * kk, axis=-1, keepdims=True)), 1e-12)
        k2 = k * (1.0 + (asig - 1.0) * k_a[h])
        a = -kk
        b = kk * asig
        cum = mm(ltri, lw)
        cumx = cum - lw
        ref = lax.stop_gradient(row_of(cum, c // 2))
        last = row_of(cum, c - 1)
        at = a * jnp.exp(cumx - ref)
        rt = r * jnp.exp(cum - ref)
        bt = b * jnp.exp(ref - cum)
        kt = k2 * jnp.exp(ref - cum)
        s0e = S[h] * jnp.exp(ref)
        aab = mm_nt(at, bt) * stri
        aak = mm_nt(at, kt) * stri
        arb = mm_nt(rt, bt) * ltri
        ark = mm_nt(rt, kt) * ltri
        u = mm_nt(at, s0e) + mm(aak, v)
        p = aab
        n_double = max(1, (c - 1).bit_length())
        for it in range(n_double):
            u = u + mm(p, u)
            if it + 1 < n_double:
                p = mm(p, p)
        o = mm_nt(rt, s0e) + mm(arb, u) + mm(ark, v)
        s1 = S[h] * jnp.exp(last) + mm_tn(u, b * jnp.exp(last - cum)) + mm_tn(v, k2 * jnp.exp(last - cum))
        mean = jnp.mean(o, axis=-1, keepdims=True)
        d = o - mean
        var = jnp.mean(d * d, axis=-1, keepdims=True)
        o = d * lax.rsqrt(var + RWKV_LN_EPS) * ln_w[h] + ln_b[h]
        o = o + jnp.sum(r * k2 * r_k[h], axis=-1, keepdims=True) * v
        outs.append(o)
        S1.append(s1)
    last_rows = lambda xs: [row_of(x, c - 1) for x in xs]
    new_state = (S1, last_rows(r_), last_rows(k_), last_rows(v_), row_of(xw_, c - 1), row_of(xa_, c - 1))
    return outs, new_state


def rope_mat():
    i, j = _iota2(SWA_HD, SWA_HD)
    plus = (j >= 8) & (j < 16) & (i == j - 8)
    minus = (j < 8) & (i == j + 8)
    return jnp.where(plus, 1.0, 0.0).astype(F32) - jnp.where(minus, 1.0, 0.0).astype(F32)


def swa_chunk(state, toks, params, first):
    kprev, vprev = state
    q_, k_, v_, cos, sin = toks
    bq, bk, bv, sinks = params
    c = cos.shape[0]
    rm = rope_mat()
    qi, kj = _iota2(c, c)
    cur_ok = qi >= kj
    prev_ok = (kj > qi) & jnp.logical_not(first)

    def rope(x):
        return x * cos + mm(x, rm) * sin

    outs, kn, vn = [None] * SWA_Q_HEADS, [], []
    for g in range(SWA_KV_HEADS):
        k = rope(k_[g] + bk[g])
        v = v_[g] + bv[g]
        kn.append(k)
        vn.append(v)
        for j in range(SWA_GROUP):
            h = g * SWA_GROUP + j
            q = rope(q_[h] + bq[h]) * (SWA_HD ** -0.5)
            sc = jnp.where(cur_ok, mm_nt(q, k), NEG)
            sp = jnp.where(prev_ok, mm_nt(q, kprev[g]), NEG)
            m = jnp.maximum(jnp.maximum(jnp.max(sc, axis=-1, keepdims=True), jnp.max(sp, axis=-1, keepdims=True)), sinks[h])
            m = lax.stop_gradient(m)
            pc = jnp.exp(sc - m)
            pp = jnp.exp(sp - m)
            den = jnp.sum(pc, axis=-1, keepdims=True) + jnp.sum(pp, axis=-1, keepdims=True) + jnp.exp(sinks[h] - m)
            outs[h] = (mm(pc, v) + mm(pp, vprev[g])) / den
    return outs, (kn, vn)


def _heads(ref, n, w, rows=slice(None)):
    return [ref[rows, h * w:(h + 1) * w] for h in range(n)]


def _put_heads(ref, vals, w, rows=slice(None), add=False):
    for h, val in enumerate(vals):
        if add:
            ref[rows, h * w:(h + 1) * w] += val
        else:
            ref[rows, h * w:(h + 1) * w] = val


def _col(block_w, name, table):
    off, w = table[name]
    assert off % block_w == 0 and w % block_w == 0
    return off // block_w


def _tok_spec(c, w, colblock, n=None):
    if n is None:
        return pl.BlockSpec((c, w), lambda i: (i, colblock))
    return pl.BlockSpec((c, w), lambda i: (n - 1 - i, colblock))


def _full_spec(shape):
    return pl.BlockSpec(shape, lambda i: (0,) * len(shape))


def _matmul(name, a, b, mode, tm, tn, tk, out_dtype=F32):
    if mode == "nn":
        (m, kd), n = a.shape, b.shape[1]
        a_spec = pl.BlockSpec((tm, tk), lambda j, i, k: (i, k))
        b_spec = pl.BlockSpec((tk, tn), lambda j, i, k: (k, j))
        dims = (((1,), (0,)), ((), ()))
    elif mode == "nt":
        (m, kd), n = a.shape, b.shape[0]
        a_spec = pl.BlockSpec((tm, tk), lambda j, i, k: (i, k))
        b_spec = pl.BlockSpec((tn, tk), lambda j, i, k: (j, k))
        dims = (((1,), (1,)), ((), ()))
    else:
        (kd, m), n = a.shape, b.shape[1]
        a_spec = pl.BlockSpec((tk, tm), lambda j, i, k: (k, i))
        b_spec = pl.BlockSpec((tk, tn), lambda j, i, k: (k, j))
        dims = (((0,), (0,)), ((), ()))
    assert m % tm == 0 and n % tn == 0 and kd % tk == 0
    nk = kd // tk

    def body(a_ref, b_ref, o_ref, acc_ref):
        k = pl.program_id(2)

        @pl.when(k == 0)
        def _():
            acc_ref[...] = jnp.zeros_like(acc_ref)

        acc_ref[...] += lax.dot_general(a_ref[...].astype(BF16), b_ref[...].astype(BF16), dims, preferred_element_type=F32)

        @pl.when(k == nk - 1)
        def _():
            o_ref[...] = acc_ref[...].astype(out_dtype)

    return pl.pallas_call(
        body, name=name, grid=(n // tn, m // tm, nk), in_specs=[a_spec, b_spec],
        out_specs=pl.BlockSpec((tm, tn), lambda j, i, k: (i, j)),
        out_shape=jax.ShapeDtypeStruct((m, n), out_dtype), scratch_shapes=[pltpu.VMEM((tm, tn), F32)],
        compiler_params=_cparams(("arbitrary", "arbitrary", "arbitrary")))(a, b)


TOK_TILE = 512


def _norm_fwd(name, x, w, y=None):
    t, d = x.shape
    tile = pl.BlockSpec((TOK_TILE, d), lambda i: (i, 0))

    def body(*refs):
        if y is None:
            x_ref, w_ref, hn_ref = refs
            h = x_ref[...]
        else:
            x_ref, y_ref, w_ref, h_ref, hn_ref = refs
            h = x_ref[...] + y_ref[...]
            h_ref[...] = h
        hn_ref[...] = rms(h, w_ref[...]).astype(BF16)

    ins = [x, w] if y is None else [x, y, w]
    in_specs = [tile, _full_spec((1, d))] if y is None else [tile, tile, _full_spec((1, d))]
    hn_shape = jax.ShapeDtypeStruct((t, d), BF16)
    out_shape = hn_shape if y is None else (jax.ShapeDtypeStruct((t, d), F32), hn_shape)
    out_specs = tile if y is None else (tile, tile)
    return pl.pallas_call(body, name=name, grid=(t // TOK_TILE,), in_specs=in_specs, out_specs=out_specs, out_shape=out_shape,
                          compiler_params=_cparams(("arbitrary",)))(*ins)


def _norm_bwd(name, h, w, dhn, dres):
    t, d = h.shape
    tile = pl.BlockSpec((TOK_TILE, d), lambda i: (i, 0))

    def body(h_ref, w_ref, dhn_ref, dres_ref, dx_ref, dw_ref):
        @pl.when(pl.program_id(0) == 0)
        def _():
            dw_ref[...] = jnp.zeros_like(dw_ref)

        _, vjp = jax.vjp(rms, h_ref[...], w_ref[...])
        dh, dw = vjp(dhn_ref[...])
        dx_ref[...] = dh + dres_ref[...]
        dw_ref[...] += dw

    return pl.pallas_call(body, name=name, grid=(t // TOK_TILE,), in_specs=[tile, _full_spec((1, d)), tile, tile],
                          out_specs=(tile, _full_spec((1, d))),
                          out_shape=(jax.ShapeDtypeStruct((t, d), F32), jax.ShapeDtypeStruct((1, d), F32)),
                          compiler_params=_cparams(("arbitrary",)))(h, w, dhn, dres)


def _gate_fwd(name, outs, proj):
    t = proj.shape[0]
    widths = [o.shape[1] for o in outs]
    n = len(outs)

    def body(*refs):
        o_refs, g_ref, og_ref = refs[:n], refs[n], refs[n + 1]
        c = 0
        for o_ref, w in zip(o_refs, widths):
            g = g_ref[:, c:c + w]
            og_ref[:, c:c + w] = (o_ref[...] * (g * sigmoid(g))).astype(BF16)
            c += w

    in_specs = [pl.BlockSpec((TOK_TILE, w), lambda i: (i, 0)) for w in widths] + [pl.BlockSpec((TOK_TILE, 1024), lambda i: (i, 0))]
    return pl.pallas_call(body, name=name, grid=(t // TOK_TILE,), in_specs=in_specs,
                          out_specs=pl.BlockSpec((TOK_TILE, 1024), lambda i: (i, 0)),
                          out_shape=jax.ShapeDtypeStruct((t, 1024), BF16), compiler_params=_cparams(("arbitrary",)))(*outs, proj)


def _gate_bwd(name, outs, proj, dog):
    t = proj.shape[0]
    widths = [o.shape[1] for o in outs]
    n = len(outs)

    def body(*refs):
        o_refs, g_ref, dog_ref = refs[:n], refs[n], refs[n + 1]
        do_refs, dg_ref = refs[n + 2:2 * n + 2], refs[2 * n + 2]
        c = 0
        for o_ref, do_ref, w in zip(o_refs, do_refs, widths):
            g = g_ref[:, c:c + w]
            dog_ = dog_ref[:, c:c + w]
            s = sigmoid(g)
            do_ref[...] = dog_ * (g * s)
            dg_ref[:, c:c + w] = dog_ * o_ref[...] * (s * (1.0 + g * (1.0 - s)))
            c += w

    o_specs = [pl.BlockSpec((TOK_TILE, w), lambda i: (i, 0)) for w in widths]
    wide = pl.BlockSpec((TOK_TILE, 1024), lambda i: (i, 0))
    return pl.pallas_call(body, name=name, grid=(t // TOK_TILE,), in_specs=o_specs + [wide, wide], out_specs=tuple(o_specs) + (wide,),
                          out_shape=tuple(jax.ShapeDtypeStruct((t, w), F32) for w in widths) + (jax.ShapeDtypeStruct((t, 1024), F32),),
                          compiler_params=_cparams(("arbitrary",)))(*outs, proj, dog)


def _top(h1, y1, b_out1, fw, target):
    t, d = h1.shape
    tile = pl.BlockSpec((TOK_TILE, d), lambda i: (i, 0))
    vec = _full_spec((1, d))

    def body(h1_ref, y1_ref, b_ref, fw_ref, tgt_ref, dh2_ref, loss_ref, db_ref, dfw_ref):
        @pl.when(pl.program_id(0) == 0)
        def _():
            loss_ref[...] = jnp.zeros_like(loss_ref)
            db_ref[...] = jnp.zeros_like(db_ref)
            dfw_ref[...] = jnp.zeros_like(dfw_ref)

        tgt = tgt_ref[...]

        def f(h2, w):
            err = rms(h2, w) - tgt
            per_tok = jnp.mean(err * err, axis=-1, keepdims=True)
            return 0.5 * jnp.sum(per_tok, axis=0, keepdims=True)

        h2 = h1_ref[...] + y1_ref[...] + b_ref[...]
        loss, vjp = jax.vjp(f, h2, fw_ref[...])
        dh2, dfw = vjp(jnp.ones((1, 1), F32))
        dh2_ref[...] = dh2
        loss_ref[...] += jnp.broadcast_to(loss, loss_ref.shape)
        db_ref[...] += jnp.sum(dh2, axis=0, keepdims=True)
        dfw_ref[...] += dfw

    return pl.pallas_call(body, name="top_loss", grid=(t // TOK_TILE,), in_specs=[tile, tile, vec, vec, tile],
                          out_specs=(tile, _full_spec((1, LANES)), vec, vec),
                          out_shape=(jax.ShapeDtypeStruct((t, d), F32), jax.ShapeDtypeStruct((1, LANES), F32),
                                     jax.ShapeDtypeStruct((1, d), F32), jax.ShapeDtypeStruct((1, d), F32)),
                          compiler_params=_cparams(("arbitrary",)))(h1, y1, b_out1, fw, target)


def _gla_load(q_ref, k_ref, v_ref, gl_ref, up_ref, bias_ref, nw_ref):
    toks = (_heads(q_ref, 4, GLA_DK), _heads(k_ref, 4, GLA_DK), _heads(v_ref, 4, GLA_DV), gl_ref[...])
    params = (_heads(up_ref, 4, GLA_DK), _heads(bias_ref, 4, GLA_DK), nw_ref[...])
    return toks, params


def _gla_specs(c, n=None):
    toks = [_tok_spec(c, 256, _col(256, "gq", C0), n), _tok_spec(c, 256, _col(256, "gk", C0), n),
            _tok_spec(c, 512, _col(512, "gv", C0), n), _tok_spec(c, 128, _col(128, "glow", C0), n)]
    params = [_full_spec((128, 256)), _full_spec((1, 256)), _full_spec((1, 128))]
    return toks, params


def _gla_fwd(proj0, gk_up, gk_bias, norm_w):
    t = proj0.shape[0]
    c = GLA_CHUNK
    nc = t // c
    toks_s, params_s = _gla_specs(c)

    def body(q_ref, k_ref, v_ref, gl_ref, up_ref, bias_ref, nw_ref, o_ref, st_ref, s_scr):
        @pl.when(pl.program_id(0) == 0)
        def _():
            s_scr[...] = jnp.zeros_like(s_scr)

        st_ref[...] = s_scr[...]
        toks, params = _gla_load(q_ref, k_ref, v_ref, gl_ref, up_ref, bias_ref, nw_ref)
        state = [s_scr[h * GLA_DV:(h + 1) * GLA_DV, :] for h in range(GLA_HEADS)]
        outs, new = gla_chunk(state, toks, params)
        _put_heads(o_ref, outs, GLA_DV)
        for h in range(GLA_HEADS):
            s_scr[h * GLA_DV:(h + 1) * GLA_DV, :] = new[h]

    return pl.pallas_call(
        body, name="gla_fwd", grid=(nc,), in_specs=toks_s + params_s,
        out_specs=(_tok_spec(c, 512, 0), pl.BlockSpec((512, GLA_DK), lambda i: (i, 0))),
        out_shape=(jax.ShapeDtypeStruct((t, 512), F32), jax.ShapeDtypeStruct((nc * 512, GLA_DK), F32)),
        scratch_shapes=[pltpu.VMEM((512, GLA_DK), F32)], compiler_params=_cparams(("arbitrary",)))(
            proj0, proj0, proj0, proj0, gk_up, gk_bias, norm_w)


def _gla_bwd(proj0, gk_up, gk_bias, norm_w, states, do):
    t = proj0.shape[0]
    c = GLA_CHUNK
    nc = t // c
    toks_s, params_s = _gla_specs(c, nc)

    def body(q_ref, k_ref, v_ref, gl_ref, up_ref, bias_ref, nw_ref, st_ref, do_ref,
             dq_ref, dk_ref, dv_ref, dgl_ref, dup_ref, dbias_ref, dnw_ref, ds_scr):
        @pl.when(pl.program_id(0) == 0)
        def _():
            ds_scr[...] = jnp.zeros_like(ds_scr)
            dup_ref[...] = jnp.zeros_like(dup_ref)
            dbias_ref[...] = jnp.zeros_like(dbias_ref)
            dnw_ref[...] = jnp.zeros_like(dnw_ref)

        toks, params = _gla_load(q_ref, k_ref, v_ref, gl_ref, up_ref, bias_ref, nw_ref)
        rows = lambda h: slice(h * GLA_DV, (h + 1) * GLA_DV)
        state = [st_ref[rows(h), :] for h in range(GLA_HEADS)]
        _, vjp = jax.vjp(gla_chunk, state, toks, params)
        douts = _heads(do_ref, 4, GLA_DV)
        dstate_in = [ds_scr[rows(h), :] for h in range(GLA_HEADS)]
        dstate, (dq, dk, dv, dgl), (dup, dbias, dnw) = vjp((douts, dstate_in))
        _put_heads(dq_ref, dq, GLA_DK)
        _put_heads(dk_ref, dk, GLA_DK)
        _put_heads(dv_ref, dv, GLA_DV)
        dgl_ref[...] = dgl
        _put_heads(dup_ref, dup, GLA_DK, add=True)
        _put_heads(dbias_ref, dbias, GLA_DK, add=True)
        dnw_ref[...] += dnw
        for h in range(GLA_HEADS):
            ds_scr[rows(h), :] = dstate[h]

    rev = lambda w: pl.BlockSpec((c, w), lambda i: (nc - 1 - i, 0))
    return pl.pallas_call(
        body, name="gla_bwd", grid=(nc,),
        in_specs=toks_s + params_s + [pl.BlockSpec((512, GLA_DK), lambda i: (nc - 1 - i, 0)), rev(512)],
        out_specs=(rev(256), rev(256), rev(512), rev(128), _full_spec((128, 256)), _full_spec((1, 256)), _full_spec((1, 128))),
        out_shape=(jax.ShapeDtypeStruct((t, 256), F32), jax.ShapeDtypeStruct((t, 256), F32), jax.ShapeDtypeStruct((t, 512), F32),
                   jax.ShapeDtypeStruct((t, 128), F32), jax.ShapeDtypeStruct((128, 256), F32), jax.ShapeDtypeStruct((1, 256), F32),
                   jax.ShapeDtypeStruct((1, 128), F32)),
        scratch_shapes=[pltpu.VMEM((512, GLA_DK), F32)], compiler_params=_cparams(("arbitrary",)))(
            proj0, proj0, proj0, proj0, gk_up, gk_bias, norm_w, states, do)


RWKV_PARAM_SHAPES = [(1, 512), (1, 512), (1, 512), (1, 128), (1, 128), (1, 512), (128, 512), (1, 512), (128, 512),
                     (1, 512), (1, 512), (1, 512), (1, 512), (1, 512)]
RWKV_PER_HEAD = [True, True, True, False, False, True, True, True, True, True, True, True, True, True]
PREV_W = 1792
PREV_OFF = dict(r=0, k=512, v=1024, xw=1536, xa=1664)


def _rwkv_load(r_ref, k_ref, v_ref, xw_ref, xa_ref, p_refs):
    n = RWKV_N
    toks = (_heads(r_ref, 8, n), _heads(k_ref, 8, n), _heads(v_ref, 8, n), xw_ref[...], xa_ref[...])
    params = tuple(_heads(p, 8, n) if per_head else p[...] for p, per_head in zip(p_refs, RWKV_PER_HEAD))
    return toks, params


def _rwkv_state(s_ref, prev_ref):
    n = RWKV_N
    S = [s_ref[h * n:(h + 1) * n, :] for h in range(RWKV_HEADS)]
    row = slice(0, 1)
    pr = [prev_ref[row, PREV_OFF["r"] + h * n:PREV_OFF["r"] + (h + 1) * n] for h in range(RWKV_HEADS)]
    pk = [prev_ref[row, PREV_OFF["k"] + h * n:PREV_OFF["k"] + (h + 1) * n] for h in range(RWKV_HEADS)]
    pv = [prev_ref[row, PREV_OFF["v"] + h * n:PREV_OFF["v"] + (h + 1) * n] for h in range(RWKV_HEADS)]
    pxw = prev_ref[row, PREV_OFF["xw"]:PREV_OFF["xw"] + 128]
    pxa = prev_ref[row, PREV_OFF["xa"]:PREV_OFF["xa"] + 128]
    return (S, pr, pk, pv, pxw, pxa)


def _rwkv_put_state(s_ref, prev_ref, state):
    n = RWKV_N
    S, pr, pk, pv, pxw, pxa = state
    row = slice(0, 1)
    for h in range(RWKV_HEADS):
        s_ref[h * n:(h + 1) * n, :] = S[h]
        prev_ref[row, PREV_OFF["r"] + h * n:PREV_OFF["r"] + (h + 1) * n] = pr[h]
        prev_ref[row, PREV_OFF["k"] + h * n:PREV_OFF["k"] + (h + 1) * n] = pk[h]
        prev_ref[row, PREV_OFF["v"] + h * n:PREV_OFF["v"] + (h + 1) * n] = pv[h]
    prev_ref[row, PREV_OFF["xw"]:PREV_OFF["xw"] + 128] = pxw
    prev_ref[row, PREV_OFF["xa"]:PREV_OFF["xa"] + 128] = pxa


def _rwkv_specs(c, n=None):
    toks = [_tok_spec(c, 512, _col(512, "r", C0), n), _tok_spec(c, 512, _col(512, "k", C0), n),
            _tok_spec(c, 512, _col(512, "v", C0), n), _tok_spec(c, 128, _col(128, "xw", C0), n),
            _tok_spec(c, 128, _col(128, "xa", C0), n)]
    return toks, [_full_spec(s) for s in RWKV_PARAM_SHAPES]


def _rwkv_fwd(proj0, params):
    t = proj0.shape[0]
    c = RWKV_CHUNK
    nc = t // c
    toks_s, params_s = _rwkv_specs(c)
    npar = len(params)

    def body(*refs):
        tok_refs, p_refs = refs[:5], refs[5:5 + npar]
        o_ref, st_ref, pst_ref, s_scr, prev_scr = refs[5 + npar:]

        @pl.when(pl.program_id(0) == 0)
        def _():
            s_scr[...] = jnp.zeros_like(s_scr)
            prev_scr[...] = jnp.zeros_like(prev_scr)

        st_ref[...] = s_scr[...]
        pst_ref[...] = prev_scr[...]
        toks, prm = _rwkv_load(*tok_refs, p_refs)
        outs, new = rwkv_chunk(_rwkv_state(s_scr, prev_scr), toks, prm)
        _put_heads(o_ref, outs, RWKV_N)
        _rwkv_put_state(s_scr, prev_scr, new)

    return pl.pallas_call(
        body, name="rwkv_fwd", grid=(nc,), in_specs=toks_s + params_s,
        out_specs=(_tok_spec(c, 512, 0), pl.BlockSpec((512, RWKV_N), lambda i: (i, 0)), pl.BlockSpec((8, PREV_W), lambda i: (i, 0))),
        out_shape=(jax.ShapeDtypeStruct((t, 512), F32), jax.ShapeDtypeStruct((nc * 512, RWKV_N), F32),
                   jax.ShapeDtypeStruct((nc * 8, PREV_W), F32)),
        scratch_shapes=[pltpu.VMEM((512, RWKV_N), F32), pltpu.VMEM((8, PREV_W), F32)],
        compiler_params=_cparams(("arbitrary",)))(proj0, proj0, proj0, proj0, proj0, *params)


def _rwkv_bwd(proj0, params, states, prevs, do):
    t = proj0.shape[0]
    c = RWKV_CHUNK
    nc = t // c
    toks_s, params_s = _rwkv_specs(c, nc)
    npar = len(params)

    def body(*refs):
        tok_refs, p_refs = refs[:5], refs[5:5 + npar]
        st_ref, pst_ref, do_ref = refs[5 + npar:8 + npar]
        dtok_refs = refs[8 + npar:13 + npar]
        dp_refs = refs[13 + npar:13 + 2 * npar]
        ds_scr, dprev_scr = refs[13 + 2 * npar:]

        @pl.when(pl.program_id(0) == 0)
        def _():
            ds_scr[...] = jnp.zeros_like(ds_scr)
            dprev_scr[...] = jnp.zeros_like(dprev_scr)
            for dp in dp_refs:
                dp[...] = jnp.zeros_like(dp)

        toks, prm = _rwkv_load(*tok_refs, p_refs)
        _, vjp = jax.vjp(rwkv_chunk, _rwkv_state(st_ref, pst_ref), toks, prm)
        dstate, dtoks, dprm = vjp((_heads(do_ref, 8, RWKV_N), _rwkv_state(ds_scr, dprev_scr)))
        for ref, val in zip(dtok_refs[:3], dtoks[:3]):
            _put_heads(ref, val, RWKV_N)
        dtok_refs[3][...] = dtoks[3]
        dtok_refs[4][...] = dtoks[4]
        for ref, val, per_head in zip(dp_refs, dprm, RWKV_PER_HEAD):
            if per_head:
                _put_heads(ref, val, RWKV_N, add=True)
            else:
                ref[...] += val
        _rwkv_put_state(ds_scr, dprev_scr, dstate)

    rev = lambda w: pl.BlockSpec((c, w), lambda i: (nc - 1 - i, 0))
    return pl.pallas_call(
        body, name="rwkv_bwd", grid=(nc,),
        in_specs=toks_s + params_s + [pl.BlockSpec((512, RWKV_N), lambda i: (nc - 1 - i, 0)),
                                      pl.BlockSpec((8, PREV_W), lambda i: (nc - 1 - i, 0)), rev(512)],
        out_specs=tuple([rev(512), rev(512), rev(512), rev(128), rev(128)] + params_s),
        out_shape=tuple([jax.ShapeDtypeStruct((t, w), F32) for w in (512, 512, 512, 128, 128)]
                        + [jax.ShapeDtypeStruct(s, F32) for s in RWKV_PARAM_SHAPES]),
        scratch_shapes=[pltpu.VMEM((512, RWKV_N), F32), pltpu.VMEM((8, PREV_W), F32)],
        compiler_params=_cparams(("arbitrary",)))(proj0, proj0, proj0, proj0, proj0, *params, states, prevs, do)


def _swa_load(q_ref, k_ref, v_ref, cos_ref, sin_ref, bq_ref, bk_ref, bv_ref, sk_ref):
    toks = (_heads(q_ref, 16, SWA_HD), _heads(k_ref, 4, SWA_HD), _heads(v_ref, 4, SWA_HD), cos_ref[...], sin_ref[...])
    params = (_heads(bq_ref, 16, SWA_HD), _heads(bk_ref, 4, SWA_HD), _heads(bv_ref, 4, SWA_HD), _heads(sk_ref, 16, 1))
    return toks, params


def _swa_specs(c, n=None):
    toks = [_tok_spec(c, 1024, _col(1024, "q", C1), n), _tok_spec(c, 256, _col(256, "k", C1), n),
            _tok_spec(c, 256, _col(256, "v", C1), n), _tok_spec(c, SWA_HD, 0, n), _tok_spec(c, SWA_HD, 0, n)]
    params = [_full_spec((1, 1024)), _full_spec((1, 256)), _full_spec((1, 256)), _full_spec((1, 16))]
    return toks, params


def _swa_fwd(proj1, cos, sin, bq, bk, bv, sinks):
    t = proj1.shape[0]
    c = WINDOW
    nb = t // c
    toks_s, params_s = _swa_specs(c)

    def body(q_ref, k_ref, v_ref, cos_ref, sin_ref, bq_ref, bk_ref, bv_ref, sk_ref, o_ref, kst_ref, vst_ref, k_scr, v_scr):
        first = pl.program_id(0) == 0

        @pl.when(first)
        def _():
            k_scr[...] = jnp.zeros_like(k_scr)
            v_scr[...] = jnp.zeros_like(v_scr)

        kst_ref[...] = k_scr[...]
        vst_ref[...] = v_scr[...]
        toks, params = _swa_load(q_ref, k_ref, v_ref, cos_ref, sin_ref, bq_ref, bk_ref, bv_ref, sk_ref)
        outs, (kn, vn) = swa_chunk((_heads(k_scr, 4, SWA_HD), _heads(v_scr, 4, SWA_HD)), toks, params, first)
        _put_heads(o_ref, outs, SWA_HD)
        _put_heads(k_scr, kn, SWA_HD)
        _put_heads(v_scr, vn, SWA_HD)

    return pl.pallas_call(
        body, name="swa_fwd", grid=(nb,), in_specs=toks_s + params_s,
        out_specs=(_tok_spec(c, 1024, 0), _tok_spec(c, 256, 0), _tok_spec(c, 256, 0)),
        out_shape=(jax.ShapeDtypeStruct((t, 1024), F32), jax.ShapeDtypeStruct((t, 256), F32), jax.ShapeDtypeStruct((t, 256), F32)),
        scratch_shapes=[pltpu.VMEM((c, 256), F32), pltpu.VMEM((c, 256), F32)],
        compiler_params=_cparams(("arbitrary",)))(proj1, proj1, proj1, cos, sin, bq, bk, bv, sinks)


def _swa_bwd(proj1, cos, sin, bq, bk, bv, sinks, kst, vst, do):
    t = proj1.shape[0]
    c = WINDOW
    nb = t // c
    toks_s, params_s = _swa_specs(c, nb)

    def body(q_ref, k_ref, v_ref, cos_ref, sin_ref, bq_ref, bk_ref, bv_ref, sk_ref, kst_ref, vst_ref, do_ref,
             dq_ref, dk_ref, dv_ref, dbq_ref, dbk_ref, dbv_ref, dsk_ref, dk_scr, dv_scr):
        i = pl.program_id(0)

        @pl.when(i == 0)
        def _():
            dk_scr[...] = jnp.zeros_like(dk_scr)
            dv_scr[...] = jnp.zeros_like(dv_scr)
            for ref in (dbq_ref, dbk_ref, dbv_ref, dsk_ref):
                ref[...] = jnp.zeros_like(ref)

        first = i == nb - 1
        toks, params = _swa_load(q_ref, k_ref, v_ref, cos_ref, sin_ref, bq_ref, bk_ref, bv_ref, sk_ref)
        f = functools.partial(swa_chunk, first=first)
        _, vjp = jax.vjp(f, (_heads(kst_ref, 4, SWA_HD), _heads(vst_ref, 4, SWA_HD)), toks, params)
        dstate_in = (_heads(dk_scr, 4, SWA_HD), _heads(dv_scr, 4, SWA_HD))
        (dkp, dvp), (dq, dk, dv, _, _), (dbq, dbk, dbv, dsk) = vjp((_heads(do_ref, 16, SWA_HD), dstate_in))
        _put_heads(dq_ref, dq, SWA_HD)
        _put_heads(dk_ref, dk, SWA_HD)
        _put_heads(dv_ref, dv, SWA_HD)
        _put_heads(dbq_ref, dbq, SWA_HD, add=True)
        _put_heads(dbk_ref, dbk, SWA_HD, add=True)
        _put_heads(dbv_ref, dbv, SWA_HD, add=True)
        _put_heads(dsk_ref, dsk, 1, add=True)
        _put_heads(dk_scr, dkp, SWA_HD)
        _put_heads(dv_scr, dvp, SWA_HD)

    rev = lambda w: pl.BlockSpec((c, w), lambda i: (nb - 1 - i, 0))
    return pl.pallas_call(
        body, name="swa_bwd", grid=(nb,), in_specs=toks_s + params_s + [rev(256), rev(256), rev(1024)],
        out_specs=(rev(1024), rev(256), rev(256), _full_spec((1, 1024)), _full_spec((1, 256)), _full_spec((1, 256)), _full_spec((1, 16))),
        out_shape=(jax.ShapeDtypeStruct((t, 1024), F32), jax.ShapeDtypeStruct((t, 256), F32), jax.ShapeDtypeStruct((t, 256), F32),
                   jax.ShapeDtypeStruct((1, 1024), F32), jax.ShapeDtypeStruct((1, 256), F32), jax.ShapeDtypeStruct((1, 256), F32),
                   jax.ShapeDtypeStruct((1, 16), F32)),
        scratch_shapes=[pltpu.VMEM((c, 256), F32), pltpu.VMEM((c, 256), F32)],
        compiler_params=_cparams(("arbitrary",)))(proj1, proj1, proj1, cos, sin, bq, bk, bv, sinks, kst, vst, do)


MESH = pl.DeviceIdType.MESH
ANY = pl.BlockSpec(memory_space=pl.ANY)


def _my_place():
    return lax.axis_index("x"), lax.axis_index("y"), lax.axis_index("c")


def _all_gather(shards):
    n = len(shards)

    def body(*refs):
        in_refs, out_refs = refs[:n], refs[n:2 * n]
        send_sems, recv_sems, local_sems = refs[2 * n:]
        x, y, c = _my_place()
        me, sibling = (x, y, c), (x, y, 1 - c)
        chips = [(1 - x, y), (x, 1 - y), (1 - x, 1 - y)]

        def slot(out_ref, place):
            px, py, pc = place
            return out_ref.at[4 * px + 2 * py + pc]

        def copy(a, k, block, to, src=None):
            return pltpu.make_async_remote_copy(
                src_ref=slot(out_refs[a], block) if src is None else src, dst_ref=slot(out_refs[a], block),
                send_sem=send_sems.at[a, k], recv_sem=recv_sems.at[a, k], device_id=to, device_id_type=MESH)

        mine = [pltpu.make_async_copy(in_refs[a], slot(out_refs[a], me), local_sems.at[a]) for a in range(n)]
        for cp in mine:
            cp.start()
        first = []
        for a in range(n):
            first.append(copy(a, 0, me, sibling, src=in_refs[a]))
            first += [copy(a, 1 + j, me, (*chip, c), src=in_refs[a]) for j, chip in enumerate(chips)]
        for cp in first:
            cp.start()
        passed = []
        for j, chip in enumerate(chips):
            for a in range(n):
                copy(a, 1 + j, (*chip, c), me).wait_recv()
                fwd = copy(a, 4 + j, (*chip, c), sibling)
                fwd.start()
                passed.append(fwd)
        for a in range(n):
            copy(a, 0, sibling, me).wait_recv()
            for j, chip in enumerate(chips):
                copy(a, 4 + j, (*chip, 1 - c), me).wait_recv()
        for cp in first + passed:
            cp.wait_send()
        for cp in mine:
            cp.wait()

    return pl.pallas_call(
        body, name="all_gather_weights", in_specs=[ANY] * n, out_specs=[ANY] * n,
        out_shape=[jax.ShapeDtypeStruct((N_DEV,) + s.shape, s.dtype) for s in shards],
        scratch_shapes=[pltpu.SemaphoreType.DMA((n, 7)), pltpu.SemaphoreType.DMA((n, 7)), pltpu.SemaphoreType.DMA((n,))],
        compiler_params=pltpu.CompilerParams(has_side_effects=True))(*shards)


def _exchange(parts, rep):
    n = len(parts)

    def body(*refs):
        in_refs, rep_ref = refs[:n], refs[n]
        out_refs, rep_out = refs[n + 1:2 * n + 1], refs[2 * n + 1]
        send_sems, recv_sems, local_sems = refs[2 * n + 2:]
        x, y, c = _my_place()
        my_idx = 4 * x + 2 * y + c
        local = [pltpu.make_async_copy(in_refs[a].at[my_idx], out_refs[a].at[my_idx], local_sems.at[a]) for a in range(n)]
        local.append(pltpu.make_async_copy(rep_ref, rep_out.at[my_idx], local_sems.at[n]))
        for cp in local:
            cp.start()
        copies = []
        for rel in range(1, N_DEV):
            dx, dy, dc = (rel >> 2) & 1, (rel >> 1) & 1, rel & 1
            px, py, pc = x ^ dx, y ^ dy, c ^ dc
            peer_idx = 4 * px + 2 * py + pc
            for a in range(n):
                copies.append(pltpu.make_async_remote_copy(
                    src_ref=in_refs[a].at[peer_idx], dst_ref=out_refs[a].at[my_idx], send_sem=send_sems.at[a, rel - 1],
                    recv_sem=recv_sems.at[a, rel - 1], device_id=(px, py, pc), device_id_type=MESH))
            copies.append(pltpu.make_async_remote_copy(
                src_ref=rep_ref, dst_ref=rep_out.at[my_idx], send_sem=send_sems.at[n, rel - 1],
                recv_sem=recv_sems.at[n, rel - 1], device_id=(px, py, pc), device_id_type=MESH))
        for cp in copies:
            cp.start()
        for cp in copies:
            cp.wait_recv()
        for cp in copies:
            cp.wait_send()
        for cp in local:
            cp.wait()

    outs = pl.pallas_call(
        body, name="exchange_grads", in_specs=[ANY] * (n + 1), out_specs=[ANY] * (n + 1),
        out_shape=[jax.ShapeDtypeStruct(p.shape, p.dtype) for p in parts] + [jax.ShapeDtypeStruct((N_DEV,) + rep.shape, rep.dtype)],
        scratch_shapes=[pltpu.SemaphoreType.DMA((n + 1, 7)), pltpu.SemaphoreType.DMA((n + 1, 7)), pltpu.SemaphoreType.DMA((n + 1,))],
        compiler_params=pltpu.CompilerParams(has_side_effects=True))(*parts, rep)
    return outs[:n], outs[n]


def _adam_math(w, g, m, v):
    m = ADAM_B1 * m + (1.0 - ADAM_B1) * g
    v = ADAM_B2 * v + (1.0 - ADAM_B2) * (g * g)
    m_hat = m / (1.0 - ADAM_B1 ** ADAM_STEP)
    v_hat = v / (1.0 - ADAM_B2 ** ADAM_STEP)
    delta = -ADAM_LR * (m_hat / (jnp.sqrt(v_hat) + ADAM_EPS) + ADAM_WD * w)
    return delta, m, v


def _adamw(name, w, gslots, m, v, tr):
    r, cc = w.shape
    assert r % tr == 0
    tile = pl.BlockSpec((tr, cc), lambda i: (i, 0))

    def body(w_ref, g_ref, m_ref, v_ref, go_ref, d_ref, mo_ref, vo_ref):
        g = g_ref[0]
        for s in range(1, N_DEV):
            g = g + g_ref[s]
        d, mn, vn = _adam_math(w_ref[...], g, m_ref[...], v_ref[...])
        go_ref[...] = g
        d_ref[...] = d
        mo_ref[...] = mn
        vo_ref[...] = vn

    shp = jax.ShapeDtypeStruct((r, cc), F32)
    return pl.pallas_call(body, name=name, grid=(r // tr,),
                          in_specs=[tile, pl.BlockSpec((N_DEV, tr, cc), lambda i: (0, i, 0)), tile, tile],
                          out_specs=(tile,) * 4, out_shape=(shp,) * 4, compiler_params=_cparams(("arbitrary",)))(w, gslots, m, v)


def _pack(arrays):
    rows = []
    for a in arrays:
        flat = a.reshape(-1).astype(F32)
        pad = (-flat.shape[0]) % LANES
        rows.append(jnp.pad(flat, (0, pad)).reshape(-1, LANES))
    out = jnp.concatenate(rows, axis=0)
    return jnp.pad(out, ((0, (-out.shape[0]) % 8), (0, 0)))


def _unpack(packed, shapes):
    outs, r = [], 0
    for s in shapes:
        n = 1
        for d in s:
            n *= d
        nr = -(-n // LANES)
        outs.append(packed[r:r + nr].reshape(-1)[:n].reshape(s))
        r += nr
    return outs


def _rope_tables(t):
    half = 8
    inv_freq = ROPE_THETA ** (-jnp.arange(half, dtype=F32) / half)
    ang = jnp.arange(t, dtype=F32)[:, None] * inv_freq
    cos = jnp.concatenate([jnp.cos(ang), jnp.cos(ang), jnp.ones((t, SWA_HD - 16), F32)], axis=1)
    sin = jnp.concatenate([jnp.sin(ang), jnp.sin(ang), jnp.zeros((t, SWA_HD - 16), F32)], axis=1)
    return cos, sin


def _pad_to(a, rows=None, cols=None):
    r = 0 if rows is None else rows - a.shape[0]
    c = 0 if cols is None else cols - a.shape[1]
    return jnp.pad(a, ((0, r), (0, c)))


ORIG0 = dict(gq=(0, 256), gk=(256, 256), gv=(512, 512), glow=(1024, 16), r=(1040, 512), k=(1552, 512), v=(2064, 512),
             xw=(2576, 64), xa=(2640, 64), gate=(2704, 1024))
ORIG0_ORDER = ["gq", "gk", "gv", "glow", "r", "k", "v", "xw", "xa", "gate"]


def _w0_to_padded(w):
    out = jnp.zeros((w.shape[0], N0P), w.dtype)
    for name, (off, width) in ORIG0.items():
        out = lax.dynamic_update_slice(out, w[:, off:off + width], (0, C0[name][0]))
    return out


def _w0_from_padded(wp):
    return jnp.concatenate([wp[:, C0[n][0]:C0[n][0] + ORIG0[n][1]] for n in ORIG0_ORDER], axis=1)


def _w1_to_mine(w):
    return jnp.concatenate([w[:, 1536:2560], w[:, :1536]], axis=1)


def _w1_from_mine(w):
    return jnp.concatenate([w[:, 1024:2560], w[:, :1024]], axis=1)


def kernel(x, norm_w, w_in0, gla_gk_up, gla_gk_bias, gla_norm_w, rwkv_mu, rwkv_w0, rwkv_w_up, rwkv_a0, rwkv_a_up, rwkv_k_k, rwkv_k_a, rwkv_r_k, rwkv_ln_w, rwkv_ln_b, w_out0, w_in1, b_in1, attn_sinks, w_out1, b_out1, final_norm_w, loss_target, m_norm_w, m_w_in0, m_gla_gk_up, m_gla_gk_bias, m_gla_norm_w, m_rwkv_mu, m_rwkv_w0, m_rwkv_w_up, m_rwkv_a0, m_rwkv_a_up, m_rwkv_k_k, m_rwkv_k_a, m_rwkv_r_k, m_rwkv_ln_w, m_rwkv_ln_b, m_w_out0, m_w_in1, m_b_in1, m_attn_sinks, m_w_out1, m_b_out1, m_final_norm_w, v_norm_w, v_w_in0, v_gla_gk_up, v_gla_gk_bias, v_gla_norm_w, v_rwkv_mu, v_rwkv_w0, v_rwkv_w_up, v_rwkv_a0, v_rwkv_a_up, v_rwkv_k_k, v_rwkv_k_a, v_rwkv_r_k, v_rwkv_ln_w, v_rwkv_ln_b, v_w_out0, v_w_in1, v_b_in1, v_attn_sinks, v_w_out1, v_b_out1, v_final_norm_w):
    weights = dict(norm_w=norm_w, w_in0=w_in0, gla_gk_up=gla_gk_up, gla_gk_bias=gla_gk_bias, gla_norm_w=gla_norm_w, rwkv_mu=rwkv_mu,
                   rwkv_w0=rwkv_w0, rwkv_w_up=rwkv_w_up, rwkv_a0=rwkv_a0, rwkv_a_up=rwkv_a_up, rwkv_k_k=rwkv_k_k, rwkv_k_a=rwkv_k_a,
                   rwkv_r_k=rwkv_r_k, rwkv_ln_w=rwkv_ln_w, rwkv_ln_b=rwkv_ln_b, w_out0=w_out0, w_in1=w_in1, b_in1=b_in1,
                   attn_sinks=attn_sinks, w_out1=w_out1, b_out1=b_out1, final_norm_w=final_norm_w)
    moms = dict(norm_w=m_norm_w, w_in0=m_w_in0, gla_gk_up=m_gla_gk_up, gla_gk_bias=m_gla_gk_bias, gla_norm_w=m_gla_norm_w,
                rwkv_mu=m_rwkv_mu, rwkv_w0=m_rwkv_w0, rwkv_w_up=m_rwkv_w_up, rwkv_a0=m_rwkv_a0, rwkv_a_up=m_rwkv_a_up,
                rwkv_k_k=m_rwkv_k_k, rwkv_k_a=m_rwkv_k_a, rwkv_r_k=m_rwkv_r_k, rwkv_ln_w=m_rwkv_ln_w, rwkv_ln_b=m_rwkv_ln_b,
                w_out0=m_w_out0, w_in1=m_w_in1, b_in1=m_b_in1, attn_sinks=m_attn_sinks, w_out1=m_w_out1, b_out1=m_b_out1,
                final_norm_w=m_final_norm_w)
    vars_ = dict(norm_w=v_norm_w, w_in0=v_w_in0, gla_gk_up=v_gla_gk_up, gla_gk_bias=v_gla_gk_bias, gla_norm_w=v_gla_norm_w,
                 rwkv_mu=v_rwkv_mu, rwkv_w0=v_rwkv_w0, rwkv_w_up=v_rwkv_w_up, rwkv_a0=v_rwkv_a0, rwkv_a_up=v_rwkv_a_up,
                 rwkv_k_k=v_rwkv_k_k, rwkv_k_a=v_rwkv_k_a, rwkv_r_k=v_rwkv_r_k, rwkv_ln_w=v_rwkv_ln_w, rwkv_ln_b=v_rwkv_ln_b,
                 w_out0=v_w_out0, w_in1=v_w_in1, b_in1=v_b_in1, attn_sinks=v_attn_sinks, w_out1=v_w_out1, b_out1=v_b_out1,
                 final_norm_w=v_final_norm_w)
    names = list(weights)
    big = ["w_in0", "w_out0", "w_in1", "w_out1"]
    small_sharded = ["gla_gk_up", "rwkv_w_up", "rwkv_a_up", "b_in1", "b_out1"]
    replicated = [n for n in names if n not in big and n not in small_sharded]

    xs = x[0]
    tgt = loss_target[0]
    t = xs.shape[0]

    small_shard_pack = _pack([weights[n] for n in small_sharded])
    g_in0, g_out0, g_in1, g_out1, g_small = _all_gather(
        [w_in0[0].astype(BF16), w_out0[0].astype(BF16), w_in1[0].astype(BF16), w_out1[0].astype(BF16), small_shard_pack])
    w0p = _w0_to_padded(jnp.transpose(g_in0, (1, 0, 2)).reshape(D_MODEL, -1))
    wo0 = g_out0.reshape(1024, D_MODEL)
    w1p = _w1_to_mine(jnp.transpose(g_in1, (1, 0, 2)).reshape(D_MODEL, -1))
    wo1 = g_out1.reshape(1024, D_MODEL)
    small_shapes = [weights[n].shape for n in small_sharded]
    per_dev = [_unpack(g_small[d], small_shapes) for d in range(N_DEV)]
    gk_up = jnp.concatenate([p[0][0] for p in per_dev], axis=1)
    w_up = jnp.concatenate([p[1][0] for p in per_dev], axis=1)
    a_up = jnp.concatenate([p[2][0] for p in per_dev], axis=1)
    b_in = jnp.concatenate([p[3] for p in per_dev], axis=1)
    b_out = jnp.concatenate([p[4] for p in per_dev], axis=1)

    gk_up_p = _pad_to(gk_up, rows=128)
    mu = rwkv_mu
    rwkv_params = [mu[:, 0:512], mu[:, 512:1024], mu[:, 1024:1536], _pad_to(mu[:, 1536:1600], cols=128), _pad_to(mu[:, 1600:1664], cols=128),
                   rwkv_w0, _pad_to(w_up, rows=128), rwkv_a0, _pad_to(a_up, rows=128), rwkv_k_k, rwkv_k_a, rwkv_r_k.reshape(1, 512),
                   rwkv_ln_w, rwkv_ln_b]
    bq, bk, bv = b_in[:, :1024], b_in[:, 1024:1280], b_in[:, 1280:1536]
    cos, sin = _rope_tables(t)
    nw0, nw1, fw = norm_w[0:1], norm_w[1:2], final_norm_w.reshape(1, D_MODEL)

    hn0 = _norm_fwd("norm0_fwd", xs, nw0)
    proj0 = _matmul("proj0", hn0, w0p, "nn", 512, 1024, 1024)
    o_a, gla_states = _gla_fwd(proj0, gk_up_p, gla_gk_bias, gla_norm_w)
    o_b, rwkv_states, rwkv_prevs = _rwkv_fwd(proj0, rwkv_params)
    og0 = _gate_fwd("gate0_fwd", [o_a, o_b], proj0)
    y0 = _matmul("out0", og0, wo0, "nn", 512, 1024, 1024)
    h1, hn1 = _norm_fwd("norm1_fwd", xs, nw1, y0)
    proj1 = _matmul("proj1", hn1, w1p, "nn", 512, 512, 1024)
    o_c, kst, vst = _swa_fwd(proj1, cos, sin, bq, bk, bv, attn_sinks)
    og1 = _gate_fwd("gate1_fwd", [o_c], proj1)
    y1 = _matmul("out1", og1, wo1, "nn", 512, 1024, 1024)
    dh2, loss_part, d_b_out, d_fw = _top(h1, y1, b_out, fw, tgt)

    dog1 = _matmul("out1_dx", dh2, wo1, "nt", 512, 1024, 1024)
    d_wo1 = _matmul("out1_dw", og1, dh2, "tn", 1024, 512, 512)
    d_oc, d_gate1 = _gate_bwd("gate1_bwd", [o_c], proj1, dog1)
    dq, dk, dv, d_bq, d_bk, d_bv, d_sinks = _swa_bwd(proj1, cos, sin, bq, bk, bv, attn_sinks, kst, vst, d_oc)
    dproj1 = jnp.concatenate([d_gate1, dq, dk, dv], axis=1)
    dhn1 = _matmul("proj1_dx", dproj1, w1p, "nt", 512, 1024, 512)
    d_w1p = _matmul("proj1_dw", hn1, dproj1, "tn", 1024, 512, 512)
    dh1, d_nw1 = _norm_bwd("norm1_bwd", h1, nw1, dhn1, dh2)
    dog0 = _matmul("out0_dx", dh1, wo0, "nt", 512, 1024, 1024)
    d_wo0 = _matmul("out0_dw", og0, dh1, "tn", 1024, 512, 512)
    d_oa, d_ob, d_gate0 = _gate_bwd("gate0_bwd", [o_a, o_b], proj0, dog0)
    dgq, dgk, dgv, dglow, d_gk_up, d_gk_bias, d_gla_nw = _gla_bwd(proj0, gk_up_p, gla_gk_bias, gla_norm_w, gla_states, d_oa)
    rb = _rwkv_bwd(proj0, rwkv_params, rwkv_states, rwkv_prevs, d_ob)
    dr, dkk, dvv, dxw, dxa = rb[:5]
    d_rp = rb[5:]
    dproj0 = jnp.concatenate([d_gate0, dgv, dr, dkk, dvv, dgq, dgk, dglow, dxw, dxa, jnp.zeros((t, 128), F32)], axis=1)
    dhn0 = _matmul("proj0_dx", dproj0, w0p, "nt", 512, 1024, 1024)
    d_w0p = _matmul("proj0_dw", hn0, dproj0, "tn", 1024, 512, 512)
    grad_x, d_nw0 = _norm_bwd("norm0_bwd", xs, nw0, dhn0, dh1)

    contrib = dict(
        norm_w=jnp.concatenate([d_nw0, d_nw1], axis=0), gla_gk_bias=d_gk_bias, gla_norm_w=d_gla_nw,
        rwkv_mu=jnp.concatenate([d_rp[0], d_rp[1], d_rp[2], d_rp[3][:, :64], d_rp[4][:, :64]], axis=1),
        rwkv_w0=d_rp[5], rwkv_a0=d_rp[7], rwkv_k_k=d_rp[9], rwkv_k_a=d_rp[10], rwkv_r_k=d_rp[11].reshape(1, 8, 64),
        rwkv_ln_w=d_rp[12], rwkv_ln_b=d_rp[13], attn_sinks=d_sinks, final_norm_w=d_fw.reshape(D_MODEL))
    rep_pack = _pack([contrib[n] for n in replicated] + [loss_part[:, :1]])

    d_w0 = _w0_from_padded(d_w0p)
    d_w1 = _w1_from_mine(d_w1p)
    d_b_in = jnp.concatenate([d_bq, d_bk, d_bv], axis=1)
    full_small = [d_gk_up[:16], d_rp[6][:64], d_rp[8][:64], d_b_in, d_b_out]
    split_cols = lambda a: jnp.transpose(a.reshape(a.shape[0], N_DEV, -1), (1, 0, 2))
    small_parts = [split_cols(a) for a in full_small]
    small_pack = jnp.stack([_pack([sp[d] for sp in small_parts]) for d in range(N_DEV)])
    parts = [split_cols(d_w0), d_wo0.reshape(N_DEV, 128, D_MODEL), split_cols(d_w1), d_wo1.reshape(N_DEV, 128, D_MODEL), small_pack]
    (r_in0, r_out0, r_in1, r_out1, r_small), r_rep = _exchange(parts, rep_pack)

    res = {}
    res["w_in0"] = _adamw("adamw_w_in0", w_in0[0], r_in0, m_w_in0[0], v_w_in0[0], 256)
    res["w_out0"] = _adamw("adamw_w_out0", w_out0[0], r_out0, m_w_out0[0], v_w_out0[0], 128)
    res["w_in1"] = _adamw("adamw_w_in1", w_in1[0], r_in1, m_w_in1[0], v_w_in1[0], 256)
    res["w_out1"] = _adamw("adamw_w_out1", w_out1[0], r_out1, m_w_out1[0], v_w_out1[0], 128)
    for n in big:
        res[n] = tuple(a[None] for a in res[n])
    small_names = small_sharded + replicated
    slots = jnp.concatenate([r_small, r_rep], axis=1)
    n_shard_rows = r_small.shape[1]
    pk = lambda d: jnp.concatenate([_pack([d[n] for n in small_sharded]), _pack([d[n] for n in replicated] + [jnp.zeros((1, 1), F32)])], axis=0)
    g_p, d_p, m_p, v_p = _adamw("adamw_small", pk(weights), slots, pk(moms), pk(vars_), slots.shape[1])
    sh_shapes = [weights[n].shape for n in small_sharded]
    rep_shapes = [weights[n].shape for n in replicated] + [(1, 1)]
    for i, packed in enumerate((g_p, d_p, m_p, v_p)):
        vals = _unpack(packed[:n_shard_rows], sh_shapes) + _unpack(packed[n_shard_rows:], rep_shapes)
        for n, val in zip(small_names, vals):
            res.setdefault(n, [None] * 4)[i] = val
        if i == 0:
            loss = vals[-1].reshape(())
    return (loss, grad_x[None], *[res[n][0] for n in names], *[res[n][1] for n in names],
            *[res[n][2] for n in names], *[res[n][3] for n in names])
```

```python
import functools

import jax
import jax.numpy as jnp
from jax import lax
from jax.experimental import pallas as pl
from jax.experimental.pallas import tpu as pltpu

F32 = jnp.float32
BF16 = jnp.bfloat16
HI = lax.Precision.HIGHEST

D_MODEL = 1024
NORM_EPS = 1e-5
GLA_HEADS, GLA_DK, GLA_DV = 4, 64, 128
GLA_NORMALIZER = 16.0
GLA_CHUNK = 64
RWKV_HEADS, RWKV_N = 8, 64
RWKV_LN_EPS = 64e-5
RWKV_CHUNK = 64
SWA_Q_HEADS, SWA_KV_HEADS, SWA_GROUP, SWA_HD = 16, 4, 4, 64
WINDOW = 128
ROPE_THETA = 500000.0
NEG = -1e30
N_DEV = 8
LANES = 128

ADAM_LR, ADAM_B1, ADAM_B2, ADAM_EPS, ADAM_WD, ADAM_STEP = 0.001, 0.9, 0.999, 1e-08, 0.01, 10

N0P = 4096
C0 = dict(gate=(0, 1024), gv=(1024, 512), r=(1536, 512), k=(2048, 512), v=(2560, 512), gq=(3072, 256), gk=(3328, 256),
          glow=(3584, 128), xw=(3712, 128), xa=(3840, 128))
N1P = 2560
C1 = dict(gate=(0, 1024), q=(1024, 1024), k=(2048, 256), v=(2304, 256))

VMEM_LIMIT = 56 * 1024 * 1024

P_LORA = 1
P_GLA = 1
P_RWKV_G = 3
P_RWKV = 1
P_SWA = 1
P_ROPE = 3


def _cparams(sem=None):
    return pltpu.CompilerParams(dimension_semantics=sem, vmem_limit_bytes=VMEM_LIMIT)


DIMS = dict(nn=(((1,), (0,)), ((), ())), nt=(((1,), (1,)), ((), ())), tn=(((0,), (0,)), ((), ())))


def _split_bf16(a):
    hi = a.astype(BF16)
    return hi, (a - hi.astype(F32)).astype(BF16)


def _dot(a, b, mode, passes):
    dg = lambda p, q: lax.dot_general(p, q, DIMS[mode], preferred_element_type=F32)
    if passes == 1:
        return dg(a.astype(BF16), b.astype(BF16))
    if passes == 3:
        (ah, al), (bh, bl) = _split_bf16(a), _split_bf16(b)
        return dg(ah, bh) + dg(al, bh) + dg(ah, bl)
    return lax.dot_general(a, b, DIMS[mode], precision=HI, preferred_element_type=F32)


@functools.partial(jax.custom_vjp, nondiff_argnums=(2, 3))
def mmx(a, b, mode, passes):
    return _dot(a, b, mode, passes)


def _mmx_fwd(a, b, mode, passes):
    return _dot(a, b, mode, passes), (a, b)


def _mmx_bwd(mode, passes, res, g):
    a, b = res
    if mode == "nn":
        return _dot(g, b, "nt", passes), _dot(a, g, "tn", passes)
    if mode == "nt":
        return _dot(g, b, "nn", passes), _dot(g, a, "tn", passes)
    return _dot(b, g, "nt", passes), _dot(a, g, "nn", passes)


mmx.defvjp(_mmx_fwd, _mmx_bwd)


def _tri_dot(tri, x):
    t = tri.astype(BF16)
    x1 = x.astype(BF16)
    r1 = x - x1.astype(F32)
    x2 = r1.astype(BF16)
    x3 = (r1 - x2.astype(F32)).astype(BF16)
    dg = lambda q: jnp.dot(t, q, preferred_element_type=F32)
    return dg(x1) + dg(x2) + dg(x3)


@jax.custom_vjp
def cumsum_rows(x):
    return _tri_dot(tril_ones(x.shape[0]), x)


def _cumsum_fwd(x):
    return cumsum_rows(x), None


def _cumsum_bwd(_, g):
    i, j = _iota2(g.shape[0], g.shape[0])
    return (_tri_dot(jnp.where(i <= j, 1.0, 0.0).astype(F32), g),)


cumsum_rows.defvjp(_cumsum_fwd, _cumsum_bwd)


def cat_rows(*xs):
    return jnp.concatenate(xs, axis=0)


def _iota2(n, m):
    return lax.broadcasted_iota(jnp.int32, (n, m), 0), lax.broadcasted_iota(jnp.int32, (n, m), 1)


def tril_ones(c, strict=False):
    i, j = _iota2(c, c)
    return jnp.where((i > j) if strict else (i >= j), 1.0, 0.0).astype(F32)


def row_of(x, r):
    i = lax.broadcasted_iota(jnp.int32, x.shape, 0)
    return jnp.sum(jnp.where(i == r, x, 0.0), axis=0, keepdims=True)


@jax.custom_vjp
def shift_rows(x, prev):
    r = lax.broadcasted_iota(jnp.int32, x.shape, 0)
    return jnp.where(r == 0, prev, pltpu.roll(x, 1, 0))


def _shift_fwd(x, prev):
    return shift_rows(x, prev), None


def _shift_bwd(_, g):
    c = g.shape[0]
    r = lax.broadcasted_iota(jnp.int32, g.shape, 0)
    return jnp.where(r == c - 1, 0.0, pltpu.roll(g, c - 1, 0)), row_of(g, 0)


shift_rows.defvjp(_shift_fwd, _shift_bwd)


def log_sigmoid(x):
    return jnp.minimum(x, 0.0) - jnp.log(1.0 + jnp.exp(-jnp.abs(x)))


def softplus(x):
    return jnp.maximum(x, 0.0) + jnp.log(1.0 + jnp.exp(-jnp.abs(x)))


def sigmoid(x):
    return 1.0 / (1.0 + jnp.exp(-x))


def rms(x, w, eps=NORM_EPS):
    return x * lax.rsqrt(jnp.mean(x * x, axis=-1, keepdims=True) + eps) * w


def gla_chunk(state, toks, params):
    q, k, v, glow = toks
    gk_up, bias, norm_w = params
    c = glow.shape[0]
    ltri = tril_ones(c)
    outs, new_state = [], []
    for h in range(GLA_HEADS):
        g = log_sigmoid(mmx(glow, gk_up[h], "nn", P_LORA) + bias[h]) / GLA_NORMALIZER
        b = cumsum_rows(g)
        ref = lax.stop_gradient(row_of(b, c // 2))
        last = row_of(b, c - 1)
        qs = q[h] * (GLA_DK ** -0.5)
        sc = mmx(qs * jnp.exp(b - ref), cat_rows(k[h] * jnp.exp(ref - b), state[h] * jnp.exp(ref)), "nt", P_GLA)
        o = mmx(sc[:, :c] * ltri, v[h], "nn", P_GLA) + sc[:, c:]
        s1 = state[h] * jnp.exp(last) + mmx(v[h], k[h] * jnp.exp(last - b), "tn", P_GLA)
        o = o * lax.rsqrt(jnp.mean(o * o, axis=-1, keepdims=True) + NORM_EPS) * norm_w
        outs.append(o)
        new_state.append(s1)
    return outs, new_state


def rwkv_chunk(state, toks, params):
    S, pr, pk, pv, pxw, pxa = state
    r_, k_, v_, xw_, xa_ = toks
    mu_r, mu_k, mu_v, mu_xw, mu_xa, w0, w_up, a0, a_up, k_k, k_a, r_k, ln_w, ln_b = params
    c = xw_.shape[0]
    ltri = tril_ones(c)
    stri = tril_ones(c, strict=True)

    def lerp(x, prev, mu):
        return x + (shift_rows(x, prev) - x) * mu

    xw = jnp.tanh(lerp(xw_, pxw, mu_xw))
    xa = lerp(xa_, pxa, mu_xa)
    outs, S1 = [], []
    for h in range(RWKV_HEADS):
        r = lerp(r_[h], pr[h], mu_r[h])
        k = lerp(k_[h], pk[h], mu_k[h])
        v = lerp(v_[h], pv[h], mu_v[h])
        w = -softplus(-(w0[h] + mmx(xw, w_up[h], "nn", P_LORA))) - 0.5
        lw = -jnp.exp(w)
        asig = sigmoid(a0[h] + mmx(xa, a_up[h], "nn", P_LORA))
        kk = k * k_k[h]
        kk = kk / jnp.maximum(jnp.sqrt(jnp.sum(kk * kk, axis=-1, keepdims=True)), 1e-12)
        k2 = k * (1.0 + (asig - 1.0) * k_a[h])
        a = -kk
        b = kk * asig
        cum = cumsum_rows(lw)
        cumx = cum - lw
        ref = lax.stop_gradient(row_of(cum, c // 2))
        last = row_of(cum, c - 1)
        at = a * jnp.exp(cumx - ref)
        rt = r * jnp.exp(cum - ref)
        bt = b * jnp.exp(ref - cum)
        kt = k2 * jnp.exp(ref - cum)
        s0e = S[h] * jnp.exp(ref)
        g = mmx(cat_rows(at, rt), cat_rows(bt, kt, s0e), "nt", P_RWKV_G)
        aab, aak, as0 = g[:c, :c] * stri, g[:c, c:2 * c] * stri, g[:c, 2 * c:]
        arb, ark, rs0 = g[c:, :c] * ltri, g[c:, c:2 * c] * ltri, g[c:, 2 * c:]
        av = mmx(cat_rows(aak, ark), v, "nn", P_RWKV)
        u = as0 + av[:c]
        p = aab
        n_double = max(1, (c - 1).bit_length())
        for it in range(n_double):
            if it + 1 < n_double:
                y = mmx(p, jnp.concatenate([p, u], axis=1), "nn", P_RWKV)
                u = u + y[:, c:]
                p = y[:, :c]
            else:
                u = u + mmx(p, u, "nn", P_RWKV)
        o = rs0 + av[c:] + mmx(arb, u, "nn", P_RWKV)
        e_last = jnp.exp(last - cum)
        s1 = S[h] * jnp.exp(last) + mmx(cat_rows(u, v), cat_rows(b * e_last, k2 * e_last), "tn", P_RWKV)
        mean = jnp.mean(o, axis=-1, keepdims=True)
        d = o - mean
        var = jnp.mean(d * d, axis=-1, keepdims=True)
        o = d * lax.rsqrt(var + RWKV_LN_EPS) * ln_w[h] + ln_b[h]
        o = o + jnp.sum(r * k2 * r_k[h], axis=-1, keepdims=True) * v
        outs.append(o)
        S1.append(s1)
    last_rows = lambda xs: [row_of(x, c - 1) for x in xs]
    new_state = (S1, last_rows(r_), last_rows(k_), last_rows(v_), row_of(xw_, c - 1), row_of(xa_, c - 1))
    return outs, new_state


def rope_mat():
    i, j = _iota2(SWA_HD, SWA_HD)
    plus = (j >= 8) & (j < 16) & (i == j - 8)
    minus = (j < 8) & (i == j + 8)
    return jnp.where(plus, 1.0, 0.0).astype(F32) - jnp.where(minus, 1.0, 0.0).astype(F32)


def swa_chunk(state, toks, params, first):
    kprev, vprev = state
    q_, k_, v_, cos, sin = toks
    bq, bk, bv, sinks = params
    c = cos.shape[0]
    ng = SWA_GROUP
    rm = rope_mat()
    qi, kj = _iota2(ng * c, 2 * c)
    qpos = qi & (c - 1)
    ok = ((kj < c) & (kj > qpos) & jnp.logical_not(first)) | ((kj >= c) & (qpos >= kj - c))
    cos_g, sin_g = cat_rows(*[cos] * ng), cat_rows(*[sin] * ng)

    def rope(x, cs, sn):
        return x * cs + mmx(x, rm, "nn", P_ROPE) * sn

    outs, kn, vn = [None] * SWA_Q_HEADS, [], []
    for g in range(SWA_KV_HEADS):
        k = rope(k_[g] + bk[g], cos, sin)
        v = v_[g] + bv[g]
        kn.append(k)
        vn.append(v)
        hs = range(g * ng, (g + 1) * ng)
        q = rope(cat_rows(*[q_[h] + bq[h] for h in hs]), cos_g, sin_g) * (SWA_HD ** -0.5)
        s = jnp.where(ok, mmx(q, cat_rows(kprev[g], k), "nt", P_SWA), NEG)
        sink = cat_rows(*[jnp.broadcast_to(sinks[h], (c, 1)) for h in hs])
        m = lax.stop_gradient(jnp.maximum(jnp.max(s, axis=-1, keepdims=True), sink))
        p = jnp.exp(s - m)
        den = jnp.sum(p, axis=-1, keepdims=True) + jnp.exp(sink - m)
        o = mmx(p, cat_rows(vprev[g], v), "nn", P_SWA) / den
        for j, h in enumerate(hs):
            outs[h] = o[j * c:(j + 1) * c]
    return outs, (kn, vn)


def _heads(ref, n, w, rows=slice(None)):
    return [ref[rows, h * w:(h + 1) * w] for h in range(n)]


def _put_heads(ref, vals, w, rows=slice(None), add=False):
    for h, val in enumerate(vals):
        if add:
            ref[rows, h * w:(h + 1) * w] += val
        else:
            ref[rows, h * w:(h + 1) * w] = val


def _col(block_w, name, table):
    off, w = table[name]
    assert off % block_w == 0 and w % block_w == 0
    return off // block_w


def _tok_spec(c, w, colblock, n=None):
    if n is None:
        return pl.BlockSpec((c, w), lambda i: (i, colblock))
    return pl.BlockSpec((c, w), lambda i: (n - 1 - i, colblock))


def _full_spec(shape):
    return pl.BlockSpec(shape, lambda i: (0,) * len(shape))


def _matmul(name, a, b, mode, tm, tn, tk, out_dtype=F32):
    if mode == "nn":
        (m, kd), n = a.shape, b.shape[1]
        a_spec = pl.BlockSpec((tm, tk), lambda j, i, k: (i, k))
        b_spec = pl.BlockSpec((tk, tn), lambda j, i, k: (k, j))
        dims = (((1,), (0,)), ((), ()))
    elif mode == "nt":
        (m, kd), n = a.shape, b.shape[0]
        a_spec = pl.BlockSpec((tm, tk), lambda j, i, k: (i, k))
        b_spec = pl.BlockSpec((tn, tk), lambda j, i, k: (j, k))
        dims = (((1,), (1,)), ((), ()))
    else:
        (kd, m), n = a.shape, b.shape[1]
        a_spec = pl.BlockSpec((tk, tm), lambda j, i, k: (k, i))
        b_spec = pl.BlockSpec((tk, tn), lambda j, i, k: (k, j))
        dims = (((0,), (0,)), ((), ()))
    assert m % tm == 0 and n % tn == 0 and kd % tk == 0
    nk = kd // tk

    def body(a_ref, b_ref, o_ref, acc_ref):
        k = pl.program_id(2)

        @pl.when(k == 0)
        def _():
            acc_ref[...] = jnp.zeros_like(acc_ref)

        acc_ref[...] += lax.dot_general(a_ref[...].astype(BF16), b_ref[...].astype(BF16), dims, preferred_element_type=F32)

        @pl.when(k == nk - 1)
        def _():
            o_ref[...] = acc_ref[...].astype(out_dtype)

    return pl.pallas_call(
        body, name=name, grid=(n // tn, m // tm, nk), in_specs=[a_spec, b_spec],
        out_specs=pl.BlockSpec((tm, tn), lambda j, i, k: (i, j)),
        out_shape=jax.ShapeDtypeStruct((m, n), out_dtype), scratch_shapes=[pltpu.VMEM((tm, tn), F32)],
        compiler_params=_cparams(("arbitrary", "arbitrary", "arbitrary")))(a, b)


TOK_TILE = 512


def _norm_fwd(name, x, w, y=None):
    t, d = x.shape
    tile = pl.BlockSpec((TOK_TILE, d), lambda i: (i, 0))

    def body(*refs):
        if y is None:
            x_ref, w_ref, hn_ref = refs
            h = x_ref[...]
        else:
            x_ref, y_ref, w_ref, h_ref, hn_ref = refs
            h = x_ref[...] + y_ref[...]
            h_ref[...] = h
        hn_ref[...] = rms(h, w_ref[...]).astype(BF16)

    ins = [x, w] if y is None else [x, y, w]
    in_specs = [tile, _full_spec((1, d))] if y is None else [tile, tile, _full_spec((1, d))]
    hn_shape = jax.ShapeDtypeStruct((t, d), BF16)
    out_shape = hn_shape if y is None else (jax.ShapeDtypeStruct((t, d), F32), hn_shape)
    out_specs = tile if y is None else (tile, tile)
    return pl.pallas_call(body, name=name, grid=(t // TOK_TILE,), in_specs=in_specs, out_specs=out_specs, out_shape=out_shape,
                          compiler_params=_cparams(("arbitrary",)))(*ins)


def _norm_bwd(name, h, w, dhn, dres):
    t, d = h.shape
    tile = pl.BlockSpec((TOK_TILE, d), lambda i: (i, 0))

    def body(h_ref, w_ref, dhn_ref, dres_ref, dx_ref, dw_ref):
        @pl.when(pl.program_id(0) == 0)
        def _():
            dw_ref[...] = jnp.zeros_like(dw_ref)

        _, vjp = jax.vjp(rms, h_ref[...], w_ref[...])
        dh, dw = vjp(dhn_ref[...])
        dx_ref[...] = dh + dres_ref[...]
        dw_ref[...] += dw

    return pl.pallas_call(body, name=name, grid=(t // TOK_TILE,), in_specs=[tile, _full_spec((1, d)), tile, tile],
                          out_specs=(tile, _full_spec((1, d))),
                          out_shape=(jax.ShapeDtypeStruct((t, d), F32), jax.ShapeDtypeStruct((1, d), F32)),
                          compiler_params=_cparams(("arbitrary",)))(h, w, dhn, dres)


def _gate_fwd(name, outs, proj):
    t = proj.shape[0]
    widths = [o.shape[1] for o in outs]
    n = len(outs)

    def body(*refs):
        o_refs, g_ref, og_ref = refs[:n], refs[n], refs[n + 1]
        c = 0
        for o_ref, w in zip(o_refs, widths):
            g = g_ref[:, c:c + w]
            og_ref[:, c:c + w] = (o_ref[...] * (g * sigmoid(g))).astype(BF16)
            c += w

    in_specs = [pl.BlockSpec((TOK_TILE, w), lambda i: (i, 0)) for w in widths] + [pl.BlockSpec((TOK_TILE, 1024), lambda i: (i, 0))]
    return pl.pallas_call(body, name=name, grid=(t // TOK_TILE,), in_specs=in_specs,
                          out_specs=pl.BlockSpec((TOK_TILE, 1024), lambda i: (i, 0)),
                          out_shape=jax.ShapeDtypeStruct((t, 1024), BF16), compiler_params=_cparams(("arbitrary",)))(*outs, proj)


def _gate_bwd(name, outs, proj, dog):
    t = proj.shape[0]
    widths = [o.shape[1] for o in outs]
    n = len(outs)

    def body(*refs):
        o_refs, g_ref, dog_ref = refs[:n], refs[n], refs[n + 1]
        do_refs, dg_ref = refs[n + 2:2 * n + 2], refs[2 * n + 2]
        c = 0
        for o_ref, do_ref, w in zip(o_refs, do_refs, widths):
            g = g_ref[:, c:c + w]
            dog_ = dog_ref[:, c:c + w]
            s = sigmoid(g)
            do_ref[...] = dog_ * (g * s)
            dg_ref[:, c:c + w] = dog_ * o_ref[...] * (s * (1.0 + g * (1.0 - s)))
            c += w

    o_specs = [pl.BlockSpec((TOK_TILE, w), lambda i: (i, 0)) for w in widths]
    wide = pl.BlockSpec((TOK_TILE, 1024), lambda i: (i, 0))
    return pl.pallas_call(body, name=name, grid=(t // TOK_TILE,), in_specs=o_specs + [wide, wide], out_specs=tuple(o_specs) + (wide,),
                          out_shape=tuple(jax.ShapeDtypeStruct((t, w), F32) for w in widths) + (jax.ShapeDtypeStruct((t, 1024), F32),),
                          compiler_params=_cparams(("arbitrary",)))(*outs, proj, dog)


def _top(h1, y1, b_out1, fw, target):
    t, d = h1.shape
    tile = pl.BlockSpec((TOK_TILE, d), lambda i: (i, 0))
    vec = _full_spec((1, d))

    def body(h1_ref, y1_ref, b_ref, fw_ref, tgt_ref, dh2_ref, loss_ref, db_ref, dfw_ref):
        @pl.when(pl.program_id(0) == 0)
        def _():
            loss_ref[...] = jnp.zeros_like(loss_ref)
            db_ref[...] = jnp.zeros_like(db_ref)
            dfw_ref[...] = jnp.zeros_like(dfw_ref)

        tgt = tgt_ref[...]

        def f(h2, w):
            err = rms(h2, w) - tgt
            per_tok = jnp.mean(err * err, axis=-1, keepdims=True)
            return 0.5 * jnp.sum(per_tok, axis=0, keepdims=True)

        h2 = h1_ref[...] + y1_ref[...] + b_ref[...]
        loss, vjp = jax.vjp(f, h2, fw_ref[...])
        dh2, dfw = vjp(jnp.ones((1, 1), F32))
        dh2_ref[...] = dh2
        loss_ref[...] += jnp.broadcast_to(loss, loss_ref.shape)
        db_ref[...] += jnp.sum(dh2, axis=0, keepdims=True)
        dfw_ref[...] += dfw

    return pl.pallas_call(body, name="top_loss", grid=(t // TOK_TILE,), in_specs=[tile, tile, vec, vec, tile],
                          out_specs=(tile, _full_spec((1, LANES)), vec, vec),
                          out_shape=(jax.ShapeDtypeStruct((t, d), F32), jax.ShapeDtypeStruct((1, LANES), F32),
                                     jax.ShapeDtypeStruct((1, d), F32), jax.ShapeDtypeStruct((1, d), F32)),
                          compiler_params=_cparams(("arbitrary",)))(h1, y1, b_out1, fw, target)


def _gla_load(q_ref, k_ref, v_ref, gl_ref, up_ref, bias_ref, nw_ref):
    toks = (_heads(q_ref, 4, GLA_DK), _heads(k_ref, 4, GLA_DK), _heads(v_ref, 4, GLA_DV), gl_ref[...])
    params = (_heads(up_ref, 4, GLA_DK), _heads(bias_ref, 4, GLA_DK), nw_ref[...])
    return toks, params


def _gla_specs(c, n=None):
    toks = [_tok_spec(c, 256, _col(256, "gq", C0), n), _tok_spec(c, 256, _col(256, "gk", C0), n),
            _tok_spec(c, 512, _col(512, "gv", C0), n), _tok_spec(c, 128, _col(128, "glow", C0), n)]
    params = [_full_spec((128, 256)), _full_spec((1, 256)), _full_spec((1, 128))]
    return toks, params


def _gla_fwd(proj0, gk_up, gk_bias, norm_w):
    t = proj0.shape[0]
    c = GLA_CHUNK
    nc = t // c
    toks_s, params_s = _gla_specs(c)

    def body(q_ref, k_ref, v_ref, gl_ref, up_ref, bias_ref, nw_ref, o_ref, st_ref, s_scr):
        @pl.when(pl.program_id(0) == 0)
        def _():
            s_scr[...] = jnp.zeros_like(s_scr)

        st_ref[...] = s_scr[...]
        toks, params = _gla_load(q_ref, k_ref, v_ref, gl_ref, up_ref, bias_ref, nw_ref)
        state = [s_scr[h * GLA_DV:(h + 1) * GLA_DV, :] for h in range(GLA_HEADS)]
        outs, new = gla_chunk(state, toks, params)
        _put_heads(o_ref, outs, GLA_DV)
        for h in range(GLA_HEADS):
            s_scr[h * GLA_DV:(h + 1) * GLA_DV, :] = new[h]

    return pl.pallas_call(
        body, name="gla_fwd", grid=(nc,), in_specs=toks_s + params_s,
        out_specs=(_tok_spec(c, 512, 0), pl.BlockSpec((512, GLA_DK), lambda i: (i, 0))),
        out_shape=(jax.ShapeDtypeStruct((t, 512), F32), jax.ShapeDtypeStruct((nc * 512, GLA_DK), F32)),
        scratch_shapes=[pltpu.VMEM((512, GLA_DK), F32)], compiler_params=_cparams(("arbitrary",)))(
            proj0, proj0, proj0, proj0, gk_up, gk_bias, norm_w)


def _gla_bwd(proj0, gk_up, gk_bias, norm_w, states, do):
    t = proj0.shape[0]
    c = GLA_CHUNK
    nc = t // c
    toks_s, params_s = _gla_specs(c, nc)

    def body(q_ref, k_ref, v_ref, gl_ref, up_ref, bias_ref, nw_ref, st_ref, do_ref,
             dq_ref, dk_ref, dv_ref, dgl_ref, dup_ref, dbias_ref, dnw_ref, ds_scr):
        @pl.when(pl.program_id(0) == 0)
        def _():
            ds_scr[...] = jnp.zeros_like(ds_scr)
            dup_ref[...] = jnp.zeros_like(dup_ref)
            dbias_ref[...] = jnp.zeros_like(dbias_ref)
            dnw_ref[...] = jnp.zeros_like(dnw_ref)

        toks, params = _gla_load(q_ref, k_ref, v_ref, gl_ref, up_ref, bias_ref, nw_ref)
        rows = lambda h: slice(h * GLA_DV, (h + 1) * GLA_DV)
        state = [st_ref[rows(h), :] for h in range(GLA_HEADS)]
        _, vjp = jax.vjp(gla_chunk, state, toks, params)
        douts = _heads(do_ref, 4, GLA_DV)
        dstate_in = [ds_scr[rows(h), :] for h in range(GLA_HEADS)]
        dstate, (dq, dk, dv, dgl), (dup, dbias, dnw) = vjp((douts, dstate_in))
        _put_heads(dq_ref, dq, GLA_DK)
        _put_heads(dk_ref, dk, GLA_DK)
        _put_heads(dv_ref, dv, GLA_DV)
        dgl_ref[...] = dgl
        _put_heads(dup_ref, dup, GLA_DK, add=True)
        _put_heads(dbias_ref, dbias, GLA_DK, add=True)
        dnw_ref[...] += dnw
        for h in range(GLA_HEADS):
            ds_scr[rows(h), :] = dstate[h]

    rev = lambda w: pl.BlockSpec((c, w), lambda i: (nc - 1 - i, 0))
    return pl.pallas_call(
        body, name="gla_bwd", grid=(nc,),
        in_specs=toks_s + params_s + [pl.BlockSpec((512, GLA_DK), lambda i: (nc - 1 - i, 0)), rev(512)],
        out_specs=(rev(256), rev(256), rev(512), rev(128), _full_spec((128, 256)), _full_spec((1, 256)), _full_spec((1, 128))),
        out_shape=(jax.ShapeDtypeStruct((t, 256), F32), jax.ShapeDtypeStruct((t, 256), F32), jax.ShapeDtypeStruct((t, 512), F32),
                   jax.ShapeDtypeStruct((t, 128), F32), jax.ShapeDtypeStruct((128, 256), F32), jax.ShapeDtypeStruct((1, 256), F32),
                   jax.ShapeDtypeStruct((1, 128), F32)),
        scratch_shapes=[pltpu.VMEM((512, GLA_DK), F32)], compiler_params=_cparams(("arbitrary",)))(
            proj0, proj0, proj0, proj0, gk_up, gk_bias, norm_w, states, do)


RWKV_PARAM_SHAPES = [(1, 512), (1, 512), (1, 512), (1, 128), (1, 128), (1, 512), (128, 512), (1, 512), (128, 512),
                     (1, 512), (1, 512), (1, 512), (1, 512), (1, 512)]
RWKV_PER_HEAD = [True, True, True, False, False, True, True, True, True, True, True, True, True, True]
PREV_W = 1792
PREV_OFF = dict(r=0, k=512, v=1024, xw=1536, xa=1664)


def _rwkv_load(r_ref, k_ref, v_ref, xw_ref, xa_ref, p_refs):
    n = RWKV_N
    toks = (_heads(r_ref, 8, n), _heads(k_ref, 8, n), _heads(v_ref, 8, n), xw_ref[...], xa_ref[...])
    params = tuple(_heads(p, 8, n) if per_head else p[...] for p, per_head in zip(p_refs, RWKV_PER_HEAD))
    return toks, params


def _rwkv_state(s_ref, prev_ref):
    n = RWKV_N
    S = [s_ref[h * n:(h + 1) * n, :] for h in range(RWKV_HEADS)]
    row = slice(0, 1)
    pr = [prev_ref[row, PREV_OFF["r"] + h * n:PREV_OFF["r"] + (h + 1) * n] for h in range(RWKV_HEADS)]
    pk = [prev_ref[row, PREV_OFF["k"] + h * n:PREV_OFF["k"] + (h + 1) * n] for h in range(RWKV_HEADS)]
    pv = [prev_ref[row, PREV_OFF["v"] + h * n:PREV_OFF["v"] + (h + 1) * n] for h in range(RWKV_HEADS)]
    pxw = prev_ref[row, PREV_OFF["xw"]:PREV_OFF["xw"] + 128]
    pxa = prev_ref[row, PREV_OFF["xa"]:PREV_OFF["xa"] + 128]
    return (S, pr, pk, pv, pxw, pxa)


def _rwkv_put_state(s_ref, prev_ref, state):
    n = RWKV_N
    S, pr, pk, pv, pxw, pxa = state
    row = slice(0, 1)
    for h in range(RWKV_HEADS):
        s_ref[h * n:(h + 1) * n, :] = S[h]
        prev_ref[row, PREV_OFF["r"] + h * n:PREV_OFF["r"] + (h + 1) * n] = pr[h]
        prev_ref[row, PREV_OFF["k"] + h * n:PREV_OFF["k"] + (h + 1) * n] = pk[h]
        prev_ref[row, PREV_OFF["v"] + h * n:PREV_OFF["v"] + (h + 1) * n] = pv[h]
    prev_ref[row, PREV_OFF["xw"]:PREV_OFF["xw"] + 128] = pxw
    prev_ref[row, PREV_OFF["xa"]:PREV_OFF["xa"] + 128] = pxa


def _rwkv_specs(c, n=None):
    toks = [_tok_spec(c, 512, _col(512, "r", C0), n), _tok_spec(c, 512, _col(512, "k", C0), n),
            _tok_spec(c, 512, _col(512, "v", C0), n), _tok_spec(c, 128, _col(128, "xw", C0), n),
            _tok_spec(c, 128, _col(128, "xa", C0), n)]
    return toks, [_full_spec(s) for s in RWKV_PARAM_SHAPES]


def _rwkv_fwd(proj0, params):
    t = proj0.shape[0]
    c = RWKV_CHUNK
    nc = t // c
    toks_s, params_s = _rwkv_specs(c)
    npar = len(params)

    def body(*refs):
        tok_refs, p_refs = refs[:5], refs[5:5 + npar]
        o_ref, st_ref, pst_ref, s_scr, prev_scr = refs[5 + npar:]

        @pl.when(pl.program_id(0) == 0)
        def _():
            s_scr[...] = jnp.zeros_like(s_scr)
            prev_scr[...] = jnp.zeros_like(prev_scr)

        st_ref[...] = s_scr[...]
        pst_ref[...] = prev_scr[...]
        toks, prm = _rwkv_load(*tok_refs, p_refs)
        outs, new = rwkv_chunk(_rwkv_state(s_scr, prev_scr), toks, prm)
        _put_heads(o_ref, outs, RWKV_N)
        _rwkv_put_state(s_scr, prev_scr, new)

    return pl.pallas_call(
        body, name="rwkv_fwd", grid=(nc,), in_specs=toks_s + params_s,
        out_specs=(_tok_spec(c, 512, 0), pl.BlockSpec((512, RWKV_N), lambda i: (i, 0)), pl.BlockSpec((8, PREV_W), lambda i: (i, 0))),
        out_shape=(jax.ShapeDtypeStruct((t, 512), F32), jax.ShapeDtypeStruct((nc * 512, RWKV_N), F32),
                   jax.ShapeDtypeStruct((nc * 8, PREV_W), F32)),
        scratch_shapes=[pltpu.VMEM((512, RWKV_N), F32), pltpu.VMEM((8, PREV_W), F32)],
        compiler_params=_cparams(("arbitrary",)))(proj0, proj0, proj0, proj0, proj0, *params)


def _rwkv_bwd(proj0, params, states, prevs, do):
    t = proj0.shape[0]
    c = RWKV_CHUNK
    nc = t // c
    toks_s, params_s = _rwkv_specs(c, nc)
    npar = len(params)

    def body(*refs):
        tok_refs, p_refs = refs[:5], refs[5:5 + npar]
        st_ref, pst_ref, do_ref = refs[5 + npar:8 + npar]
        dtok_refs = refs[8 + npar:13 + npar]
        dp_refs = refs[13 + npar:13 + 2 * npar]
        ds_scr, dprev_scr = refs[13 + 2 * npar:]

        @pl.when(pl.program_id(0) == 0)
        def _():
            ds_scr[...] = jnp.zeros_like(ds_scr)
            dprev_scr[...] = jnp.zeros_like(dprev_scr)
            for dp in dp_refs:
                dp[...] = jnp.zeros_like(dp)

        toks, prm = _rwkv_load(*tok_refs, p_refs)
        _, vjp = jax.vjp(rwkv_chunk, _rwkv_state(st_ref, pst_ref), toks, prm)
        dstate, dtoks, dprm = vjp((_heads(do_ref, 8, RWKV_N), _rwkv_state(ds_scr, dprev_scr)))
        for ref, val in zip(dtok_refs[:3], dtoks[:3]):
            _put_heads(ref, val, RWKV_N)
        dtok_refs[3][...] = dtoks[3]
        dtok_refs[4][...] = dtoks[4]
        for ref, val, per_head in zip(dp_refs, dprm, RWKV_PER_HEAD):
            if per_head:
                _put_heads(ref, val, RWKV_N, add=True)
            else:
                ref[...] += val
        _rwkv_put_state(ds_scr, dprev_scr, dstate)

    rev = lambda w: pl.BlockSpec((c, w), lambda i: (nc - 1 - i, 0))
    return pl.pallas_call(
        body, name="rwkv_bwd", grid=(nc,),
        in_specs=toks_s + params_s + [pl.BlockSpec((512, RWKV_N), lambda i: (nc - 1 - i, 0)),
                                      pl.BlockSpec((8, PREV_W), lambda i: (nc - 1 - i, 0)), rev(512)],
        out_specs=tuple([rev(512), rev(512), rev(512), rev(128), rev(128)] + params_s),
        out_shape=tuple([jax.ShapeDtypeStruct((t, w), F32) for w in (512, 512, 512, 128, 128)]
                        + [jax.ShapeDtypeStruct(s, F32) for s in RWKV_PARAM_SHAPES]),
        scratch_shapes=[pltpu.VMEM((512, RWKV_N), F32), pltpu.VMEM((8, PREV_W), F32)],
        compiler_params=_cparams(("arbitrary",)))(proj0, proj0, proj0, proj0, proj0, *params, states, prevs, do)


def _swa_load(q_ref, k_ref, v_ref, cos_ref, sin_ref, bq_ref, bk_ref, bv_ref, sk_ref):
    toks = (_heads(q_ref, 16, SWA_HD), _heads(k_ref, 4, SWA_HD), _heads(v_ref, 4, SWA_HD), cos_ref[...], sin_ref[...])
    params = (_heads(bq_ref, 16, SWA_HD), _heads(bk_ref, 4, SWA_HD), _heads(bv_ref, 4, SWA_HD), _heads(sk_ref, 16, 1))
    return toks, params


def _swa_specs(c, n=None):
    toks = [_tok_spec(c, 1024, _col(1024, "q", C1), n), _tok_spec(c, 256, _col(256, "k", C1), n),
            _tok_spec(c, 256, _col(256, "v", C1), n), _tok_spec(c, SWA_HD, 0, n), _tok_spec(c, SWA_HD, 0, n)]
    params = [_full_spec((1, 1024)), _full_spec((1, 256)), _full_spec((1, 256)), _full_spec((1, 16))]
    return toks, params


def _swa_fwd(proj1, cos, sin, bq, bk, bv, sinks):
    t = proj1.shape[0]
    c = WINDOW
    nb = t // c
    toks_s, params_s = _swa_specs(c)

    def body(q_ref, k_ref, v_ref, cos_ref, sin_ref, bq_ref, bk_ref, bv_ref, sk_ref, o_ref, kst_ref, vst_ref, k_scr, v_scr):
        first = pl.program_id(0) == 0

        @pl.when(first)
        def _():
            k_scr[...] = jnp.zeros_like(k_scr)
            v_scr[...] = jnp.zeros_like(v_scr)

        kst_ref[...] = k_scr[...]
        vst_ref[...] = v_scr[...]
        toks, params = _swa_load(q_ref, k_ref, v_ref, cos_ref, sin_ref, bq_ref, bk_ref, bv_ref, sk_ref)
        outs, (kn, vn) = swa_chunk((_heads(k_scr, 4, SWA_HD), _heads(v_scr, 4, SWA_HD)), toks, params, first)
        _put_heads(o_ref, outs, SWA_HD)
        _put_heads(k_scr, kn, SWA_HD)
        _put_heads(v_scr, vn, SWA_HD)

    return pl.pallas_call(
        body, name="swa_fwd", grid=(nb,), in_specs=toks_s + params_s,
        out_specs=(_tok_spec(c, 1024, 0), _tok_spec(c, 256, 0), _tok_spec(c, 256, 0)),
        out_shape=(jax.ShapeDtypeStruct((t, 1024), F32), jax.ShapeDtypeStruct((t, 256), F32), jax.ShapeDtypeStruct((t, 256), F32)),
        scratch_shapes=[pltpu.VMEM((c, 256), F32), pltpu.VMEM((c, 256), F32)],
        compiler_params=_cparams(("arbitrary",)))(proj1, proj1, proj1, cos, sin, bq, bk, bv, sinks)


def _swa_bwd(proj1, cos, sin, bq, bk, bv, sinks, kst, vst, do):
    t = proj1.shape[0]
    c = WINDOW
    nb = t // c
    toks_s, params_s = _swa_specs(c, nb)

    def body(q_ref, k_ref, v_ref, cos_ref, sin_ref, bq_ref, bk_ref, bv_ref, sk_ref, kst_ref, vst_ref, do_ref,
             dq_ref, dk_ref, dv_ref, dbq_ref, dbk_ref, dbv_ref, dsk_ref, dk_scr, dv_scr):
        i = pl.program_id(0)

        @pl.when(i == 0)
        def _():
            dk_scr[...] = jnp.zeros_like(dk_scr)
            dv_scr[...] = jnp.zeros_like(dv_scr)
            for ref in (dbq_ref, dbk_ref, dbv_ref, dsk_ref):
                ref[...] = jnp.zeros_like(ref)

        first = i == nb - 1
        toks, params = _swa_load(q_ref, k_ref, v_ref, cos_ref, sin_ref, bq_ref, bk_ref, bv_ref, sk_ref)
        f = functools.partial(swa_chunk, first=first)
        _, vjp = jax.vjp(f, (_heads(kst_ref, 4, SWA_HD), _heads(vst_ref, 4, SWA_HD)), toks, params)
        dstate_in = (_heads(dk_scr, 4, SWA_HD), _heads(dv_scr, 4, SWA_HD))
        (dkp, dvp), (dq, dk, dv, _, _), (dbq, dbk, dbv, dsk) = vjp((_heads(do_ref, 16, SWA_HD), dstate_in))
        _put_heads(dq_ref, dq, SWA_HD)
        _put_heads(dk_ref, dk, SWA_HD)
        _put_heads(dv_ref, dv, SWA_HD)
        _put_heads(dbq_ref, dbq, SWA_HD, add=True)
        _put_heads(dbk_ref, dbk, SWA_HD, add=True)
        _put_heads(dbv_ref, dbv, SWA_HD, add=True)
        _put_heads(dsk_ref, dsk, 1, add=True)
        _put_heads(dk_scr, dkp, SWA_HD)
        _put_heads(dv_scr, dvp, SWA_HD)

    rev = lambda w: pl.BlockSpec((c, w), lambda i: (nb - 1 - i, 0))
    return pl.pallas_call(
        body, name="swa_bwd", grid=(nb,), in_specs=toks_s + params_s + [rev(256), rev(256), rev(1024)],
        out_specs=(rev(1024), rev(256), rev(256), _full_spec((1, 1024)), _full_spec((1, 256)), _full_spec((1, 256)), _full_spec((1, 16))),
        out_shape=(jax.ShapeDtypeStruct((t, 1024), F32), jax.ShapeDtypeStruct((t, 256), F32), jax.ShapeDtypeStruct((t, 256), F32),
                   jax.ShapeDtypeStruct((1, 1024), F32), jax.ShapeDtypeStruct((1, 256), F32), jax.ShapeDtypeStruct((1, 256), F32),
                   jax.ShapeDtypeStruct((1, 16), F32)),
        scratch_shapes=[pltpu.VMEM((c, 256), F32), pltpu.VMEM((c, 256), F32)],
        compiler_params=_cparams(("arbitrary",)))(proj1, proj1, proj1, cos, sin, bq, bk, bv, sinks, kst, vst, do)


MESH = pl.DeviceIdType.MESH
ANY = pl.BlockSpec(memory_space=pl.ANY)


def _my_place():
    return lax.axis_index("x"), lax.axis_index("y"), lax.axis_index("c")


def _all_gather(shards):
    n = len(shards)

    def body(*refs):
        in_refs, out_refs = refs[:n], refs[n:2 * n]
        send_sems, recv_sems, local_sems = refs[2 * n:]
        x, y, c = _my_place()
        me, sibling = (x, y, c), (x, y, 1 - c)
        chips = [(1 - x, y), (x, 1 - y), (1 - x, 1 - y)]

        def slot(out_ref, place):
            px, py, pc = place
            return out_ref.at[4 * px + 2 * py + pc]

        def copy(a, k, block, to, src=None):
            return pltpu.make_async_remote_copy(
                src_ref=slot(out_refs[a], block) if src is None else src, dst_ref=slot(out_refs[a], block),
                send_sem=send_sems.at[a, k], recv_sem=recv_sems.at[a, k], device_id=to, device_id_type=MESH)

        mine = [pltpu.make_async_copy(in_refs[a], slot(out_refs[a], me), local_sems.at[a]) for a in range(n)]
        for cp in mine:
            cp.start()
        first = []
        for a in range(n):
            first.append(copy(a, 0, me, sibling, src=in_refs[a]))
            first += [copy(a, 1 + j, me, (*chip, c), src=in_refs[a]) for j, chip in enumerate(chips)]
        for cp in first:
            cp.start()
        passed = []
        for j, chip in enumerate(chips):
            for a in range(n):
                copy(a, 1 + j, (*chip, c), me).wait_recv()
                fwd = copy(a, 4 + j, (*chip, c), sibling)
                fwd.start()
                passed.append(fwd)
        for a in range(n):
            copy(a, 0, sibling, me).wait_recv()
            for j, chip in enumerate(chips):
                copy(a, 4 + j, (*chip, 1 - c), me).wait_recv()
        for cp in first + passed:
            cp.wait_send()
        for cp in mine:
            cp.wait()

    return pl.pallas_call(
        body, name="all_gather_weights", in_specs=[ANY] * n, out_specs=[ANY] * n,
        out_shape=[jax.ShapeDtypeStruct((N_DEV,) + s.shape, s.dtype) for s in shards],
        scratch_shapes=[pltpu.SemaphoreType.DMA((n, 7)), pltpu.SemaphoreType.DMA((n, 7)), pltpu.SemaphoreType.DMA((n,))],
        compiler_params=pltpu.CompilerParams(has_side_effects=True))(*shards)


def _exchange(parts, rep):
    n = len(parts)

    def body(*refs):
        in_refs, rep_ref = refs[:n], refs[n]
        out_refs, rep_out = refs[n + 1:2 * n + 1], refs[2 * n + 1]
        send_sems, recv_sems, local_sems = refs[2 * n + 2:]
        x, y, c = _my_place()
        my_idx = 4 * x + 2 * y + c
        local = [pltpu.make_async_copy(in_refs[a].at[my_idx], out_refs[a].at[my_idx], local_sems.at[a]) for a in range(n)]
        local.append(pltpu.make_async_copy(rep_ref, rep_out.at[my_idx], local_sems.at[n]))
        for cp in local:
            cp.start()
        copies = []
        for rel in range(1, N_DEV):
            dx, dy, dc = (rel >> 2) & 1, (rel >> 1) & 1, rel & 1
            px, py, pc = x ^ dx, y ^ dy, c ^ dc
            peer_idx = 4 * px + 2 * py + pc
            for a in range(n):
                copies.append(pltpu.make_async_remote_copy(
                    src_ref=in_refs[a].at[peer_idx], dst_ref=out_refs[a].at[my_idx], send_sem=send_sems.at[a, rel - 1],
                    recv_sem=recv_sems.at[a, rel - 1], device_id=(px, py, pc), device_id_type=MESH))
            copies.append(pltpu.make_async_remote_copy(
                src_ref=rep_ref, dst_ref=rep_out.at[my_idx], send_sem=send_sems.at[n, rel - 1],
                recv_sem=recv_sems.at[n, rel - 1], device_id=(px, py, pc), device_id_type=MESH))
        for cp in copies:
            cp.start()
        for cp in copies:
            cp.wait_recv()
        for cp in copies:
            cp.wait_send()
        for cp in local:
            cp.wait()

    outs = pl.pallas_call(
        body, name="exchange_grads", in_specs=[ANY] * (n + 1), out_specs=[ANY] * (n + 1),
        out_shape=[jax.ShapeDtypeStruct(p.shape, p.dtype) for p in parts] + [jax.ShapeDtypeStruct((N_DEV,) + rep.shape, rep.dtype)],
        scratch_shapes=[pltpu.SemaphoreType.DMA((n + 1, 7)), pltpu.SemaphoreType.DMA((n + 1, 7)), pltpu.SemaphoreType.DMA((n + 1,))],
        compiler_params=pltpu.CompilerParams(has_side_effects=True))(*parts, rep)
    return outs[:n], outs[n]


def _adam_math(w, g, m, v):
    m = ADAM_B1 * m + (1.0 - ADAM_B1) * g
    v = ADAM_B2 * v + (1.0 - ADAM_B2) * (g * g)
    m_hat = m / (1.0 - ADAM_B1 ** ADAM_STEP)
    v_hat = v / (1.0 - ADAM_B2 ** ADAM_STEP)
    delta = -ADAM_LR * (m_hat / (jnp.sqrt(v_hat) + ADAM_EPS) + ADAM_WD * w)
    return delta, m, v


def _adamw(name, w, gslots, m, v, tr):
    r, cc = w.shape
    assert r % tr == 0
    tile = pl.BlockSpec((tr, cc), lambda i: (i, 0))

    def body(w_ref, g_ref, m_ref, v_ref, go_ref, d_ref, mo_ref, vo_ref):
        g = g_ref[0].astype(F32)
        for s in range(1, N_DEV):
            g = g + g_ref[s].astype(F32)
        d, mn, vn = _adam_math(w_ref[...], g, m_ref[...], v_ref[...])
        go_ref[...] = g
        d_ref[...] = d
        mo_ref[...] = mn
        vo_ref[...] = vn

    shp = jax.ShapeDtypeStruct((r, cc), F32)
    return pl.pallas_call(body, name=name, grid=(r // tr,),
                          in_specs=[tile, pl.BlockSpec((N_DEV, tr, cc), lambda i: (0, i, 0)), tile, tile],
                          out_specs=(tile,) * 4, out_shape=(shp,) * 4, compiler_params=_cparams(("arbitrary",)))(w, gslots, m, v)


def _pack(arrays):
    rows = []
    for a in arrays:
        flat = a.reshape(-1).astype(F32)
        pad = (-flat.shape[0]) % LANES
        rows.append(jnp.pad(flat, (0, pad)).reshape(-1, LANES))
    out = jnp.concatenate(rows, axis=0)
    return jnp.pad(out, ((0, (-out.shape[0]) % 8), (0, 0)))


def _unpack(packed, shapes):
    outs, r = [], 0
    for s in shapes:
        n = 1
        for d in s:
            n *= d
        nr = -(-n // LANES)
        outs.append(packed[r:r + nr].reshape(-1)[:n].reshape(s))
        r += nr
    return outs


def _rope_tables(t):
    half = 8
    inv_freq = ROPE_THETA ** (-jnp.arange(half, dtype=F32) / half)
    ang = jnp.arange(t, dtype=F32)[:, None] * inv_freq
    cos = jnp.concatenate([jnp.cos(ang), jnp.cos(ang), jnp.ones((t, SWA_HD - 16), F32)], axis=1)
    sin = jnp.concatenate([jnp.sin(ang), jnp.sin(ang), jnp.zeros((t, SWA_HD - 16), F32)], axis=1)
    return cos, sin


def _pad_to(a, rows=None, cols=None):
    r = 0 if rows is None else rows - a.shape[0]
    c = 0 if cols is None else cols - a.shape[1]
    return jnp.pad(a, ((0, r), (0, c)))


ORIG0 = dict(gq=(0, 256), gk=(256, 256), gv=(512, 512), glow=(1024, 16), r=(1040, 512), k=(1552, 512), v=(2064, 512),
             xw=(2576, 64), xa=(2640, 64), gate=(2704, 1024))
ORIG0_ORDER = ["gq", "gk", "gv", "glow", "r", "k", "v", "xw", "xa", "gate"]


def _w0_to_padded(w):
    out = jnp.zeros((w.shape[0], N0P), w.dtype)
    for name, (off, width) in ORIG0.items():
        out = lax.dynamic_update_slice(out, w[:, off:off + width], (0, C0[name][0]))
    return out


def _w0_from_padded(wp):
    return jnp.concatenate([wp[:, C0[n][0]:C0[n][0] + ORIG0[n][1]] for n in ORIG0_ORDER], axis=1)


def _w1_to_mine(w):
    return jnp.concatenate([w[:, 1536:2560], w[:, :1536]], axis=1)


def _w1_from_mine(w):
    return jnp.concatenate([w[:, 1024:2560], w[:, :1024]], axis=1)


def kernel(x, norm_w, w_in0, gla_gk_up, gla_gk_bias, gla_norm_w, rwkv_mu, rwkv_w0, rwkv_w_up, rwkv_a0, rwkv_a_up, rwkv_k_k, rwkv_k_a, rwkv_r_k, rwkv_ln_w, rwkv_ln_b, w_out0, w_in1, b_in1, attn_sinks, w_out1, b_out1, final_norm_w, loss_target, m_norm_w, m_w_in0, m_gla_gk_up, m_gla_gk_bias, m_gla_norm_w, m_rwkv_mu, m_rwkv_w0, m_rwkv_w_up, m_rwkv_a0, m_rwkv_a_up, m_rwkv_k_k, m_rwkv_k_a, m_rwkv_r_k, m_rwkv_ln_w, m_rwkv_ln_b, m_w_out0, m_w_in1, m_b_in1, m_attn_sinks, m_w_out1, m_b_out1, m_final_norm_w, v_norm_w, v_w_in0, v_gla_gk_up, v_gla_gk_bias, v_gla_norm_w, v_rwkv_mu, v_rwkv_w0, v_rwkv_w_up, v_rwkv_a0, v_rwkv_a_up, v_rwkv_k_k, v_rwkv_k_a, v_rwkv_r_k, v_rwkv_ln_w, v_rwkv_ln_b, v_w_out0, v_w_in1, v_b_in1, v_attn_sinks, v_w_out1, v_b_out1, v_final_norm_w):
    weights = dict(norm_w=norm_w, w_in0=w_in0, gla_gk_up=gla_gk_up, gla_gk_bias=gla_gk_bias, gla_norm_w=gla_norm_w, rwkv_mu=rwkv_mu,
                   rwkv_w0=rwkv_w0, rwkv_w_up=rwkv_w_up, rwkv_a0=rwkv_a0, rwkv_a_up=rwkv_a_up, rwkv_k_k=rwkv_k_k, rwkv_k_a=rwkv_k_a,
                   rwkv_r_k=rwkv_r_k, rwkv_ln_w=rwkv_ln_w, rwkv_ln_b=rwkv_ln_b, w_out0=w_out0, w_in1=w_in1, b_in1=b_in1,
                   attn_sinks=attn_sinks, w_out1=w_out1, b_out1=b_out1, final_norm_w=final_norm_w)
    moms = dict(norm_w=m_norm_w, w_in0=m_w_in0, gla_gk_up=m_gla_gk_up, gla_gk_bias=m_gla_gk_bias, gla_norm_w=m_gla_norm_w,
                rwkv_mu=m_rwkv_mu, rwkv_w0=m_rwkv_w0, rwkv_w_up=m_rwkv_w_up, rwkv_a0=m_rwkv_a0, rwkv_a_up=m_rwkv_a_up,
                rwkv_k_k=m_rwkv_k_k, rwkv_k_a=m_rwkv_k_a, rwkv_r_k=m_rwkv_r_k, rwkv_ln_w=m_rwkv_ln_w, rwkv_ln_b=m_rwkv_ln_b,
                w_out0=m_w_out0, w_in1=m_w_in1, b_in1=m_b_in1, attn_sinks=m_attn_sinks, w_out1=m_w_out1, b_out1=m_b_out1,
                final_norm_w=m_final_norm_w)
    vars_ = dict(norm_w=v_norm_w, w_in0=v_w_in0, gla_gk_up=v_gla_gk_up, gla_gk_bias=v_gla_gk_bias, gla_norm_w=v_gla_norm_w,
                 rwkv_mu=v_rwkv_mu, rwkv_w0=v_rwkv_w0, rwkv_w_up=v_rwkv_w_up, rwkv_a0=v_rwkv_a0, rwkv_a_up=v_rwkv_a_up,
                 rwkv_k_k=v_rwkv_k_k, rwkv_k_a=v_rwkv_k_a, rwkv_r_k=v_rwkv_r_k, rwkv_ln_w=v_rwkv_ln_w, rwkv_ln_b=v_rwkv_ln_b,
                 w_out0=v_w_out0, w_in1=v_w_in1, b_in1=v_b_in1, attn_sinks=v_attn_sinks, w_out1=v_w_out1, b_out1=v_b_out1,
                 final_norm_w=v_final_norm_w)
    names = list(weights)
    big = ["w_in0", "w_out0", "w_in1", "w_out1"]
    small_sharded = ["gla_gk_up", "rwkv_w_up", "rwkv_a_up", "b_in1", "b_out1"]
    replicated = [n for n in names if n not in big and n not in small_sharded]

    xs = x[0]
    tgt = loss_target[0]
    t = xs.shape[0]

    small_shard_pack = _pack([weights[n] for n in small_sharded])
    g_in0, g_out0, g_in1, g_out1, g_small = _all_gather(
        [w_in0[0].astype(BF16), w_out0[0].astype(BF16), w_in1[0].astype(BF16), w_out1[0].astype(BF16), small_shard_pack])
    w0p = _w0_to_padded(jnp.transpose(g_in0, (1, 0, 2)).reshape(D_MODEL, -1))
    wo0 = g_out0.reshape(1024, D_MODEL)
    w1p = _w1_to_mine(jnp.transpose(g_in1, (1, 0, 2)).reshape(D_MODEL, -1))
    wo1 = g_out1.reshape(1024, D_MODEL)
    small_shapes = [weights[n].shape for n in small_sharded]
    per_dev = [_unpack(g_small[d], small_shapes) for d in range(N_DEV)]
    gk_up = jnp.concatenate([p[0][0] for p in per_dev], axis=1)
    w_up = jnp.concatenate([p[1][0] for p in per_dev], axis=1)
    a_up = jnp.concatenate([p[2][0] for p in per_dev], axis=1)
    b_in = jnp.concatenate([p[3] for p in per_dev], axis=1)
    b_out = jnp.concatenate([p[4] for p in per_dev], axis=1)

    gk_up_p = _pad_to(gk_up, rows=128)
    mu = rwkv_mu
    rwkv_params = [mu[:, 0:512], mu[:, 512:1024], mu[:, 1024:1536], _pad_to(mu[:, 1536:1600], cols=128), _pad_to(mu[:, 1600:1664], cols=128),
                   rwkv_w0, _pad_to(w_up, rows=128), rwkv_a0, _pad_to(a_up, rows=128), rwkv_k_k, rwkv_k_a, rwkv_r_k.reshape(1, 512),
                   rwkv_ln_w, rwkv_ln_b]
    bq, bk, bv = b_in[:, :1024], b_in[:, 1024:1280], b_in[:, 1280:1536]
    cos, sin = _rope_tables(t)
    nw0, nw1, fw = norm_w[0:1], norm_w[1:2], final_norm_w.reshape(1, D_MODEL)

    hn0 = _norm_fwd("norm0_fwd", xs, nw0)
    proj0 = _matmul("proj0", hn0, w0p, "nn", 512, 1024, 1024)
    o_a, gla_states = _gla_fwd(proj0, gk_up_p, gla_gk_bias, gla_norm_w)
    o_b, rwkv_states, rwkv_prevs = _rwkv_fwd(proj0, rwkv_params)
    og0 = _gate_fwd("gate0_fwd", [o_a, o_b], proj0)
    y0 = _matmul("out0", og0, wo0, "nn", 512, 1024, 1024)
    h1, hn1 = _norm_fwd("norm1_fwd", xs, nw1, y0)
    proj1 = _matmul("proj1", hn1, w1p, "nn", 512, 512, 1024)
    o_c, kst, vst = _swa_fwd(proj1, cos, sin, bq, bk, bv, attn_sinks)
    og1 = _gate_fwd("gate1_fwd", [o_c], proj1)
    y1 = _matmul("out1", og1, wo1, "nn", 512, 1024, 1024)
    dh2, loss_part, d_b_out, d_fw = _top(h1, y1, b_out, fw, tgt)

    dog1 = _matmul("out1_dx", dh2, wo1, "nt", 512, 1024, 1024)
    d_wo1 = _matmul("out1_dw", og1, dh2, "tn", 1024, 512, 512)
    d_oc, d_gate1 = _gate_bwd("gate1_bwd", [o_c], proj1, dog1)
    dq, dk, dv, d_bq, d_bk, d_bv, d_sinks = _swa_bwd(proj1, cos, sin, bq, bk, bv, attn_sinks, kst, vst, d_oc)
    dproj1 = jnp.concatenate([d_gate1, dq, dk, dv], axis=1)
    dhn1 = _matmul("proj1_dx", dproj1, w1p, "nt", 512, 1024, 512)
    d_w1p = _matmul("proj1_dw", hn1, dproj1, "tn", 1024, 512, 512)
    dh1, d_nw1 = _norm_bwd("norm1_bwd", h1, nw1, dhn1, dh2)
    dog0 = _matmul("out0_dx", dh1, wo0, "nt", 512, 1024, 1024)
    d_wo0 = _matmul("out0_dw", og0, dh1, "tn", 1024, 512, 512)
    d_oa, d_ob, d_gate0 = _gate_bwd("gate0_bwd", [o_a, o_b], proj0, dog0)
    dgq, dgk, dgv, dglow, d_gk_up, d_gk_bias, d_gla_nw = _gla_bwd(proj0, gk_up_p, gla_gk_bias, gla_norm_w, gla_states, d_oa)
    rb = _rwkv_bwd(proj0, rwkv_params, rwkv_states, rwkv_prevs, d_ob)
    dr, dkk, dvv, dxw, dxa = rb[:5]
    d_rp = rb[5:]
    dproj0 = jnp.concatenate([d_gate0, dgv, dr, dkk, dvv, dgq, dgk, dglow, dxw, dxa, jnp.zeros((t, 128), F32)], axis=1)
    dhn0 = _matmul("proj0_dx", dproj0, w0p, "nt", 512, 1024, 1024)
    d_w0p = _matmul("proj0_dw", hn0, dproj0, "tn", 1024, 512, 512)
    grad_x, d_nw0 = _norm_bwd("norm0_bwd", xs, nw0, dhn0, dh1)

    contrib = dict(
        norm_w=jnp.concatenate([d_nw0, d_nw1], axis=0), gla_gk_bias=d_gk_bias, gla_norm_w=d_gla_nw,
        rwkv_mu=jnp.concatenate([d_rp[0], d_rp[1], d_rp[2], d_rp[3][:, :64], d_rp[4][:, :64]], axis=1),
        rwkv_w0=d_rp[5], rwkv_a0=d_rp[7], rwkv_k_k=d_rp[9], rwkv_k_a=d_rp[10], rwkv_r_k=d_rp[11].reshape(1, 8, 64),
        rwkv_ln_w=d_rp[12], rwkv_ln_b=d_rp[13], attn_sinks=d_sinks, final_norm_w=d_fw.reshape(D_MODEL))
    rep_pack = _pack([contrib[n] for n in replicated] + [loss_part[:, :1]])

    d_w0 = _w0_from_padded(d_w0p)
    d_w1 = _w1_from_mine(d_w1p)
    d_b_in = jnp.concatenate([d_bq, d_bk, d_bv], axis=1)
    full_small = [d_gk_up[:16], d_rp[6][:64], d_rp[8][:64], d_b_in, d_b_out]
    split_cols = lambda a: jnp.transpose(a.reshape(a.shape[0], N_DEV, -1), (1, 0, 2))
    small_parts = [split_cols(a) for a in full_small]
    small_pack = jnp.stack([_pack([sp[d] for sp in small_parts]) for d in range(N_DEV)])
    parts = [split_cols(d_w0).astype(BF16), d_wo0.reshape(N_DEV, 128, D_MODEL).astype(BF16), split_cols(d_w1).astype(BF16),
             d_wo1.reshape(N_DEV, 128, D_MODEL).astype(BF16), small_pack]
    (r_in0, r_out0, r_in1, r_out1, r_small), r_rep = _exchange(parts, rep_pack)

    res = {}
    res["w_in0"] = _adamw("adamw_w_in0", w_in0[0], r_in0, m_w_in0[0], v_w_in0[0], 256)
    res["w_out0"] = _adamw("adamw_w_out0", w_out0[0], r_out0, m_w_out0[0], v_w_out0[0], 128)
    res["w_in1"] = _adamw("adamw_w_in1", w_in1[0], r_in1, m_w_in1[0], v_w_in1[0], 256)
    res["w_out1"] = _adamw("adamw_w_out1", w_out1[0], r_out1, m_w_out1[0], v_w_out1[0], 128)
    for n in big:
        res[n] = tuple(a[None] for a in res[n])
    small_names = small_sharded + replicated
    slots = jnp.concatenate([r_small, r_rep], axis=1)
    n_shard_rows = r_small.shape[1]
    pk = lambda d: jnp.concatenate([_pack([d[n] for n in small_sharded]), _pack([d[n] for n in replicated] + [jnp.zeros((1, 1), F32)])], axis=0)
    g_p, d_p, m_p, v_p = _adamw("adamw_small", pk(weights), slots, pk(moms), pk(vars_), slots.shape[1])
    sh_shapes = [weights[n].shape for n in small_sharded]
    rep_shapes = [weights[n].shape for n in replicated] + [(1, 1)]
    for i, packed in enumerate((g_p, d_p, m_p, v_p)):
        vals = _unpack(packed[:n_shard_rows], sh_shapes) + _unpack(packed[n_shard_rows:], rep_shapes)
        for n, val in zip(small_names, vals):
            res.setdefault(n, [None] * 4)[i] = val
        if i == 0:
            loss = vals[-1].reshape(())
    return (loss, grad_x[None], *[res[n][0] for n in names], *[res[n][1] for n in names],
            *[res[n][2] for n in names], *[res[n][3] for n in names])
```

```python
import functools

import jax
import jax.numpy as jnp
from jax import lax
from jax.experimental import pallas as pl
from jax.experimental.pallas import tpu as pltpu

F32 = jnp.float32
BF16 = jnp.bfloat16
HI = lax.Precision.HIGHEST

D_MODEL = 1024
NORM_EPS = 1e-5
GLA_HEADS, GLA_DK, GLA_DV = 4, 64, 128
GLA_NORMALIZER = 16.0
GLA_CHUNK = 64
RWKV_HEADS, RWKV_N = 8, 64
RWKV_LN_EPS = 64e-5
RWKV_CHUNK = 64
SWA_Q_HEADS, SWA_KV_HEADS, SWA_GROUP, SWA_HD = 16, 4, 4, 64
WINDOW = 128
ROPE_THETA = 500000.0
NEG = -1e30
N_DEV = 8
LANES = 128

ADAM_LR, ADAM_B1, ADAM_B2, ADAM_EPS, ADAM_WD, ADAM_STEP = 0.001, 0.9, 0.999, 1e-08, 0.01, 10

N0P = 4096
C0 = dict(gate=(0, 1024), gv=(1024, 512), r=(1536, 512), k=(2048, 512), v=(2560, 512), gq=(3072, 256), gk=(3328, 256),
          glow=(3584, 128), xw=(3712, 128), xa=(3840, 128))
N1P = 2560
C1 = dict(gate=(0, 1024), q=(1024, 1024), k=(2048, 256), v=(2304, 256))

VMEM_LIMIT = 56 * 1024 * 1024

P_LORA = 1
P_GLA = 1
P_RWKV_G = 3
P_RWKV = 1
P_SWA = 1
P_ROPE = 3


def _cparams(sem=None):
    return pltpu.CompilerParams(dimension_semantics=sem, vmem_limit_bytes=VMEM_LIMIT)


DIMS = dict(nn=(((1,), (0,)), ((), ())), nt=(((1,), (1,)), ((), ())), tn=(((0,), (0,)), ((), ())))


def _split_bf16(a):
    hi = a.astype(BF16)
    return hi, (a - hi.astype(F32)).astype(BF16)


def _dot(a, b, mode, passes):
    dg = lambda p, q: lax.dot_general(p, q, DIMS[mode], preferred_element_type=F32)
    if passes == 1:
        return dg(a.astype(BF16), b.astype(BF16))
    if passes == 3:
        (ah, al), (bh, bl) = _split_bf16(a), _split_bf16(b)
        return dg(ah, bh) + dg(al, bh) + dg(ah, bl)
    return lax.dot_general(a, b, DIMS[mode], precision=HI, preferred_element_type=F32)


@functools.partial(jax.custom_vjp, nondiff_argnums=(2, 3))
def mmx(a, b, mode, passes):
    return _dot(a, b, mode, passes)


def _mmx_fwd(a, b, mode, passes):
    return _dot(a, b, mode, passes), (a, b)


def _mmx_bwd(mode, passes, res, g):
    a, b = res
    if mode == "nn":
        return _dot(g, b, "nt", passes), _dot(a, g, "tn", passes)
    if mode == "nt":
        return _dot(g, b, "nn", passes), _dot(g, a, "tn", passes)
    return _dot(b, g, "nt", passes), _dot(a, g, "nn", passes)


mmx.defvjp(_mmx_fwd, _mmx_bwd)


def _tri_dot(tri, x):
    t = tri.astype(BF16)
    x1 = x.astype(BF16)
    r1 = x - x1.astype(F32)
    x2 = r1.astype(BF16)
    x3 = (r1 - x2.astype(F32)).astype(BF16)
    dg = lambda q: jnp.dot(t, q, preferred_element_type=F32)
    return dg(x1) + dg(x2) + dg(x3)


@jax.custom_vjp
def cumsum_rows(x):
    return _tri_dot(tril_ones(x.shape[0]), x)


def _cumsum_fwd(x):
    return cumsum_rows(x), None


def _cumsum_bwd(_, g):
    i, j = _iota2(g.shape[0], g.shape[0])
    return (_tri_dot(jnp.where(i <= j, 1.0, 0.0).astype(F32), g),)


cumsum_rows.defvjp(_cumsum_fwd, _cumsum_bwd)


def _head_dot(x):
    i, j = _iota2(x.shape[1], x.shape[1])
    shift = RWKV_N.bit_length() - 1
    same = jnp.where(jnp.right_shift(i, shift) == jnp.right_shift(j, shift), 1.0, 0.0).astype(F32)
    return _ones_right(x, same)


def _ones_right(x, ones):
    t = ones.astype(BF16)
    x1 = x.astype(BF16)
    r1 = x - x1.astype(F32)
    x2 = r1.astype(BF16)
    x3 = (r1 - x2.astype(F32)).astype(BF16)
    dg = lambda q: jnp.dot(q, t, preferred_element_type=F32)
    return dg(x1) + dg(x2) + dg(x3)


@jax.custom_vjp
def head_sum(x):
    return _head_dot(x)


def _head_sum_fwd(x):
    return head_sum(x), None


def _head_sum_bwd(_, g):
    return (_head_dot(g),)


head_sum.defvjp(_head_sum_fwd, _head_sum_bwd)


def cat_rows(*xs):
    return jnp.concatenate(xs, axis=0)


def _iota2(n, m):
    return lax.broadcasted_iota(jnp.int32, (n, m), 0), lax.broadcasted_iota(jnp.int32, (n, m), 1)


def tril_ones(c, strict=False):
    i, j = _iota2(c, c)
    return jnp.where((i > j) if strict else (i >= j), 1.0, 0.0).astype(F32)


def row_of(x, r):
    i = lax.broadcasted_iota(jnp.int32, x.shape, 0)
    return jnp.sum(jnp.where(i == r, x, 0.0), axis=0, keepdims=True)


@jax.custom_vjp
def shift_rows(x, prev):
    r = lax.broadcasted_iota(jnp.int32, x.shape, 0)
    return jnp.where(r == 0, prev, pltpu.roll(x, 1, 0))


def _shift_fwd(x, prev):
    return shift_rows(x, prev), None


def _shift_bwd(_, g):
    c = g.shape[0]
    r = lax.broadcasted_iota(jnp.int32, g.shape, 0)
    return jnp.where(r == c - 1, 0.0, pltpu.roll(g, c - 1, 0)), row_of(g, 0)


shift_rows.defvjp(_shift_fwd, _shift_bwd)


def log_sigmoid(x):
    return jnp.minimum(x, 0.0) - jnp.log(1.0 + jnp.exp(-jnp.abs(x)))


def softplus(x):
    return jnp.maximum(x, 0.0) + jnp.log(1.0 + jnp.exp(-jnp.abs(x)))


def sigmoid(x):
    return 1.0 / (1.0 + jnp.exp(-x))


def rms(x, w, eps=NORM_EPS):
    return x * lax.rsqrt(jnp.mean(x * x, axis=-1, keepdims=True) + eps) * w


def gla_chunk(state, toks, params):
    q, k, v, glow = toks
    gk_up, bias, norm_w = params
    c = glow.shape[0]
    heads = range(GLA_HEADS)
    hk = lambda x, h: x[:, h * GLA_DK:(h + 1) * GLA_DK]
    hv = lambda x, h: x[:, h * GLA_DV:(h + 1) * GLA_DV]
    ltri = tril_ones(c)
    g = log_sigmoid(mmx(glow, gk_up, "nn", P_LORA) + bias) / GLA_NORMALIZER
    b = cumsum_rows(g)
    ref = lax.stop_gradient(row_of(b, c // 2))
    last = row_of(b, c - 1)
    ql = q * (GLA_DK ** -0.5) * jnp.exp(b - ref)
    kr = k * jnp.exp(ref - b)
    kl = k * jnp.exp(last - b)
    e_ref, e_last = jnp.exp(ref), jnp.exp(last)
    sc = [mmx(hk(ql, h), cat_rows(hk(kr, h), state[h] * hk(e_ref, h)), "nt", P_GLA) for h in heads]
    o = [mmx(sc[h][:, :c] * ltri, hv(v, h), "nn", P_GLA) + sc[h][:, c:] for h in heads]
    s1 = [state[h] * hk(e_last, h) + mmx(hv(v, h), hk(kl, h), "tn", P_GLA) for h in heads]
    o = [x * lax.rsqrt(jnp.mean(x * x, axis=-1, keepdims=True) + NORM_EPS) * norm_w for x in o]
    return jnp.concatenate(o, axis=1), s1


def rwkv_chunk(state, toks, params):
    S, pr, pk, pv, pxw, pxa = state
    r_, k_, v_, xw_, xa_ = toks
    mu_r, mu_k, mu_v, mu_xw, mu_xa, w0, w_up, a0, a_up, k_k, k_a, r_k, ln_w, ln_b = params
    c, n = xw_.shape[0], RWKV_N
    heads = range(RWKV_HEADS)
    hs = lambda x, h: x[:, h * n:(h + 1) * n]
    ltri = tril_ones(c)
    stri = tril_ones(c, strict=True)

    def lerp(x, prev, mu):
        return x + (shift_rows(x, prev) - x) * mu

    xw = jnp.tanh(lerp(xw_, pxw, mu_xw))
    xa = lerp(xa_, pxa, mu_xa)
    r = lerp(r_, pr, mu_r)
    k = lerp(k_, pk, mu_k)
    v = lerp(v_, pv, mu_v)
    w = -softplus(-(w0 + mmx(xw, w_up, "nn", P_LORA))) - 0.5
    lw = -jnp.exp(w)
    asig = sigmoid(a0 + mmx(xa, a_up, "nn", P_LORA))
    kk = k * k_k
    kk = kk / jnp.maximum(jnp.sqrt(head_sum(kk * kk)), 1e-12)
    k2 = k * (1.0 + (asig - 1.0) * k_a)
    b = kk * asig
    cum = cumsum_rows(lw)
    ref = lax.stop_gradient(row_of(cum, c // 2))
    last = row_of(cum, c - 1)
    at = -kk * jnp.exp(cum - lw - ref)
    rt = r * jnp.exp(cum - ref)
    e_out = jnp.exp(ref - cum)
    bt, kt = b * e_out, k2 * e_out
    e_tail = jnp.exp(last - cum)
    bl, kl = b * e_tail, k2 * e_tail
    e_ref, e_last = jnp.exp(ref), jnp.exp(last)
    g = [mmx(cat_rows(hs(at, h), hs(rt, h)), cat_rows(hs(bt, h), hs(kt, h), S[h] * hs(e_ref, h)), "nt", P_RWKV_G) for h in heads]
    aab = [x[:c, :c] * stri for x in g]
    aak = [x[:c, c:2 * c] * stri for x in g]
    arb = [x[c:, :c] * ltri for x in g]
    ark = [x[c:, c:2 * c] * ltri for x in g]
    av = [mmx(cat_rows(aak[h], ark[h]), hs(v, h), "nn", P_RWKV) for h in heads]
    u = [g[h][:c, 2 * c:] + av[h][:c] for h in heads]
    p = aab
    n_double = max(1, (c - 1).bit_length())
    for it in range(n_double):
        if it + 1 < n_double:
            y = [mmx(p[h], jnp.concatenate([p[h], u[h]], axis=1), "nn", P_RWKV) for h in heads]
            u = [u[h] + y[h][:, c:] for h in heads]
            p = [y[h][:, :c] for h in heads]
        else:
            u = [u[h] + mmx(p[h], u[h], "nn", P_RWKV) for h in heads]
    o = [g[h][c:, 2 * c:] + av[h][c:] + mmx(arb[h], u[h], "nn", P_RWKV) for h in heads]
    s1 = [S[h] * hs(e_last, h) + mmx(cat_rows(u[h], hs(v, h)), cat_rows(hs(bl, h), hs(kl, h)), "tn", P_RWKV) for h in heads]
    o = jnp.concatenate(o, axis=1)
    d = o - head_sum(o) * (1.0 / n)
    var = head_sum(d * d) * (1.0 / n)
    o = d * lax.rsqrt(var + RWKV_LN_EPS) * ln_w + ln_b + head_sum(r * k2 * r_k) * v
    new_state = (s1, row_of(r_, c - 1), row_of(k_, c - 1), row_of(v_, c - 1), row_of(xw_, c - 1), row_of(xa_, c - 1))
    return o, new_state


def rope_mat():
    i, j = _iota2(SWA_HD, SWA_HD)
    plus = (j >= 8) & (j < 16) & (i == j - 8)
    minus = (j < 8) & (i == j + 8)
    return jnp.where(plus, 1.0, 0.0).astype(F32) - jnp.where(minus, 1.0, 0.0).astype(F32)


def swa_chunk(state, toks, params, first):
    kprev, vprev = state
    q_, k_, v_, cos, sin = toks
    bq, bk, bv, sinks = params
    c = cos.shape[0]
    ng = SWA_GROUP
    rm = rope_mat()
    qi, kj = _iota2(ng * c, 2 * c)
    qpos = qi & (c - 1)
    ok = ((kj < c) & (kj > qpos) & jnp.logical_not(first)) | ((kj >= c) & (qpos >= kj - c))
    cos_g, sin_g = cat_rows(*[cos] * ng), cat_rows(*[sin] * ng)

    def rope(x, cs, sn):
        return x * cs + mmx(x, rm, "nn", P_ROPE) * sn

    groups = range(SWA_KV_HEADS)
    hs = lambda g: range(g * ng, (g + 1) * ng)
    k = [rope(k_[g] + bk[g], cos, sin) for g in groups]
    v = [v_[g] + bv[g] for g in groups]
    q = [rope(cat_rows(*[q_[h] + bq[h] for h in hs(g)]), cos_g, sin_g) * (SWA_HD ** -0.5) for g in groups]
    s = [jnp.where(ok, mmx(q[g], cat_rows(kprev[g], k[g]), "nt", P_SWA), NEG) for g in groups]
    sink = [cat_rows(*[jnp.broadcast_to(sinks[h], (c, 1)) for h in hs(g)]) for g in groups]
    m = [lax.stop_gradient(jnp.maximum(jnp.max(s[g], axis=-1, keepdims=True), sink[g])) for g in groups]
    p = [jnp.exp(s[g] - m[g]) for g in groups]
    den = [jnp.sum(p[g], axis=-1, keepdims=True) + jnp.exp(sink[g] - m[g]) for g in groups]
    o = [mmx(p[g], cat_rows(vprev[g], v[g]), "nn", P_SWA) / den[g] for g in groups]
    outs = [o[g][j * c:(j + 1) * c] for g in groups for j in range(ng)]
    return outs, (k, v)


def _heads(ref, n, w, rows=slice(None)):
    return [ref[rows, h * w:(h + 1) * w] for h in range(n)]


def _put_heads(ref, vals, w, rows=slice(None), add=False):
    for h, val in enumerate(vals):
        if add:
            ref[rows, h * w:(h + 1) * w] += val
        else:
            ref[rows, h * w:(h + 1) * w] = val


def _col(block_w, name, table):
    off, w = table[name]
    assert off % block_w == 0 and w % block_w == 0
    return off // block_w


def _tok_spec(c, w, colblock, n=None):
    if n is None:
        return pl.BlockSpec((c, w), lambda i: (i, colblock))
    return pl.BlockSpec((c, w), lambda i: (n - 1 - i, colblock))


def _full_spec(shape):
    return pl.BlockSpec(shape, lambda i: (0,) * len(shape))


def _matmul(name, a, b, mode, tm, tn, tk, out_dtype=F32):
    if mode == "nn":
        (m, kd), n = a.shape, b.shape[1]
        a_spec = pl.BlockSpec((tm, tk), lambda j, i, k: (i, k))
        b_spec = pl.BlockSpec((tk, tn), lambda j, i, k: (k, j))
        dims = (((1,), (0,)), ((), ()))
    elif mode == "nt":
        (m, kd), n = a.shape, b.shape[0]
        a_spec = pl.BlockSpec((tm, tk), lambda j, i, k: (i, k))
        b_spec = pl.BlockSpec((tn, tk), lambda j, i, k: (j, k))
        dims = (((1,), (1,)), ((), ()))
    else:
        (kd, m), n = a.shape, b.shape[1]
        a_spec = pl.BlockSpec((tk, tm), lambda j, i, k: (k, i))
        b_spec = pl.BlockSpec((tk, tn), lambda j, i, k: (k, j))
        dims = (((0,), (0,)), ((), ()))
    assert m % tm == 0 and n % tn == 0 and kd % tk == 0
    nk = kd // tk

    def body(a_ref, b_ref, o_ref, acc_ref):
        k = pl.program_id(2)

        @pl.when(k == 0)
        def _():
            acc_ref[...] = jnp.zeros_like(acc_ref)

        acc_ref[...] += lax.dot_general(a_ref[...].astype(BF16), b_ref[...].astype(BF16), dims, preferred_element_type=F32)

        @pl.when(k == nk - 1)
        def _():
            o_ref[...] = acc_ref[...].astype(out_dtype)

    return pl.pallas_call(
        body, name=name, grid=(n // tn, m // tm, nk), in_specs=[a_spec, b_spec],
        out_specs=pl.BlockSpec((tm, tn), lambda j, i, k: (i, j)),
        out_shape=jax.ShapeDtypeStruct((m, n), out_dtype), scratch_shapes=[pltpu.VMEM((tm, tn), F32)],
        compiler_params=_cparams(("arbitrary", "arbitrary", "arbitrary")))(a, b)


TOK_TILE = 512


def _norm_fwd(name, x, w, y=None):
    t, d = x.shape
    tile = pl.BlockSpec((TOK_TILE, d), lambda i: (i, 0))

    def body(*refs):
        if y is None:
            x_ref, w_ref, hn_ref = refs
            h = x_ref[...]
        else:
            x_ref, y_ref, w_ref, h_ref, hn_ref = refs
            h = x_ref[...] + y_ref[...]
            h_ref[...] = h
        hn_ref[...] = rms(h, w_ref[...]).astype(BF16)

    ins = [x, w] if y is None else [x, y, w]
    in_specs = [tile, _full_spec((1, d))] if y is None else [tile, tile, _full_spec((1, d))]
    hn_shape = jax.ShapeDtypeStruct((t, d), BF16)
    out_shape = hn_shape if y is None else (jax.ShapeDtypeStruct((t, d), F32), hn_shape)
    out_specs = tile if y is None else (tile, tile)
    return pl.pallas_call(body, name=name, grid=(t // TOK_TILE,), in_specs=in_specs, out_specs=out_specs, out_shape=out_shape,
                          compiler_params=_cparams(("arbitrary",)))(*ins)


def _norm_bwd(name, h, w, dhn, dres):
    t, d = h.shape
    tile = pl.BlockSpec((TOK_TILE, d), lambda i: (i, 0))

    def body(h_ref, w_ref, dhn_ref, dres_ref, dx_ref, dw_ref):
        @pl.when(pl.program_id(0) == 0)
        def _():
            dw_ref[...] = jnp.zeros_like(dw_ref)

        _, vjp = jax.vjp(rms, h_ref[...], w_ref[...])
        dh, dw = vjp(dhn_ref[...])
        dx_ref[...] = dh + dres_ref[...]
        dw_ref[...] += dw

    return pl.pallas_call(body, name=name, grid=(t // TOK_TILE,), in_specs=[tile, _full_spec((1, d)), tile, tile],
                          out_specs=(tile, _full_spec((1, d))),
                          out_shape=(jax.ShapeDtypeStruct((t, d), F32), jax.ShapeDtypeStruct((1, d), F32)),
                          compiler_params=_cparams(("arbitrary",)))(h, w, dhn, dres)


def _gate_fwd(name, outs, proj):
    t = proj.shape[0]
    widths = [o.shape[1] for o in outs]
    n = len(outs)

    def body(*refs):
        o_refs, g_ref, og_ref = refs[:n], refs[n], refs[n + 1]
        c = 0
        for o_ref, w in zip(o_refs, widths):
            g = g_ref[:, c:c + w]
            og_ref[:, c:c + w] = (o_ref[...] * (g * sigmoid(g))).astype(BF16)
            c += w

    in_specs = [pl.BlockSpec((TOK_TILE, w), lambda i: (i, 0)) for w in widths] + [pl.BlockSpec((TOK_TILE, 1024), lambda i: (i, 0))]
    return pl.pallas_call(body, name=name, grid=(t // TOK_TILE,), in_specs=in_specs,
                          out_specs=pl.BlockSpec((TOK_TILE, 1024), lambda i: (i, 0)),
                          out_shape=jax.ShapeDtypeStruct((t, 1024), BF16), compiler_params=_cparams(("arbitrary",)))(*outs, proj)


def _gate_bwd(name, outs, proj, dog):
    t = proj.shape[0]
    widths = [o.shape[1] for o in outs]
    n = len(outs)

    def body(*refs):
        o_refs, g_ref, dog_ref = refs[:n], refs[n], refs[n + 1]
        do_refs, dg_ref = refs[n + 2:2 * n + 2], refs[2 * n + 2]
        c = 0
        for o_ref, do_ref, w in zip(o_refs, do_refs, widths):
            g = g_ref[:, c:c + w]
            dog_ = dog_ref[:, c:c + w]
            s = sigmoid(g)
            do_ref[...] = dog_ * (g * s)
            dg_ref[:, c:c + w] = dog_ * o_ref[...] * (s * (1.0 + g * (1.0 - s)))
            c += w

    o_specs = [pl.BlockSpec((TOK_TILE, w), lambda i: (i, 0)) for w in widths]
    wide = pl.BlockSpec((TOK_TILE, 1024), lambda i: (i, 0))
    return pl.pallas_call(body, name=name, grid=(t // TOK_TILE,), in_specs=o_specs + [wide, wide], out_specs=tuple(o_specs) + (wide,),
                          out_shape=tuple(jax.ShapeDtypeStruct((t, w), F32) for w in widths) + (jax.ShapeDtypeStruct((t, 1024), F32),),
                          compiler_params=_cparams(("arbitrary",)))(*outs, proj, dog)


def _top(h1, y1, b_out1, fw, target):
    t, d = h1.shape
    tile = pl.BlockSpec((TOK_TILE, d), lambda i: (i, 0))
    vec = _full_spec((1, d))

    def body(h1_ref, y1_ref, b_ref, fw_ref, tgt_ref, dh2_ref, loss_ref, db_ref, dfw_ref):
        @pl.when(pl.program_id(0) == 0)
        def _():
            loss_ref[...] = jnp.zeros_like(loss_ref)
            db_ref[...] = jnp.zeros_like(db_ref)
            dfw_ref[...] = jnp.zeros_like(dfw_ref)

        tgt = tgt_ref[...]

        def f(h2, w):
            err = rms(h2, w) - tgt
            per_tok = jnp.mean(err * err, axis=-1, keepdims=True)
            return 0.5 * jnp.sum(per_tok, axis=0, keepdims=True)

        h2 = h1_ref[...] + y1_ref[...] + b_ref[...]
        loss, vjp = jax.vjp(f, h2, fw_ref[...])
        dh2, dfw = vjp(jnp.ones((1, 1), F32))
        dh2_ref[...] = dh2
        loss_ref[...] += jnp.broadcast_to(loss, loss_ref.shape)
        db_ref[...] += jnp.sum(dh2, axis=0, keepdims=True)
        dfw_ref[...] += dfw

    return pl.pallas_call(body, name="top_loss", grid=(t // TOK_TILE,), in_specs=[tile, tile, vec, vec, tile],
                          out_specs=(tile, _full_spec((1, LANES)), vec, vec),
                          out_shape=(jax.ShapeDtypeStruct((t, d), F32), jax.ShapeDtypeStruct((1, LANES), F32),
                                     jax.ShapeDtypeStruct((1, d), F32), jax.ShapeDtypeStruct((1, d), F32)),
                          compiler_params=_cparams(("arbitrary",)))(h1, y1, b_out1, fw, target)


def _gla_load(q_ref, k_ref, v_ref, gl_ref, up_ref, bias_ref, nw_ref):
    toks = (q_ref[...], k_ref[...], v_ref[...], gl_ref[...])
    params = (up_ref[...], bias_ref[...], nw_ref[...])
    return toks, params


def _gla_specs(c, n=None):
    toks = [_tok_spec(c, 256, _col(256, "gq", C0), n), _tok_spec(c, 256, _col(256, "gk", C0), n),
            _tok_spec(c, 512, _col(512, "gv", C0), n), _tok_spec(c, 128, _col(128, "glow", C0), n)]
    params = [_full_spec((128, 256)), _full_spec((1, 256)), _full_spec((1, 128))]
    return toks, params


def _gla_fwd(proj0, gk_up, gk_bias, norm_w):
    t = proj0.shape[0]
    c = GLA_CHUNK
    nc = t // c
    toks_s, params_s = _gla_specs(c)

    def body(q_ref, k_ref, v_ref, gl_ref, up_ref, bias_ref, nw_ref, o_ref, st_ref, s_scr):
        @pl.when(pl.program_id(0) == 0)
        def _():
            s_scr[...] = jnp.zeros_like(s_scr)

        st_ref[...] = s_scr[...]
        toks, params = _gla_load(q_ref, k_ref, v_ref, gl_ref, up_ref, bias_ref, nw_ref)
        state = [s_scr[h * GLA_DV:(h + 1) * GLA_DV, :] for h in range(GLA_HEADS)]
        o_ref[...], new = gla_chunk(state, toks, params)
        for h in range(GLA_HEADS):
            s_scr[h * GLA_DV:(h + 1) * GLA_DV, :] = new[h]

    return pl.pallas_call(
        body, name="gla_fwd", grid=(nc,), in_specs=toks_s + params_s,
        out_specs=(_tok_spec(c, 512, 0), pl.BlockSpec((512, GLA_DK), lambda i: (i, 0))),
        out_shape=(jax.ShapeDtypeStruct((t, 512), F32), jax.ShapeDtypeStruct((nc * 512, GLA_DK), F32)),
        scratch_shapes=[pltpu.VMEM((512, GLA_DK), F32)], compiler_params=_cparams(("arbitrary",)))(
            proj0, proj0, proj0, proj0, gk_up, gk_bias, norm_w)


def _gla_bwd(proj0, gk_up, gk_bias, norm_w, states, do):
    t = proj0.shape[0]
    c = GLA_CHUNK
    nc = t // c
    toks_s, params_s = _gla_specs(c, nc)

    def body(q_ref, k_ref, v_ref, gl_ref, up_ref, bias_ref, nw_ref, st_ref, do_ref,
             dq_ref, dk_ref, dv_ref, dgl_ref, dup_ref, dbias_ref, dnw_ref, ds_scr):
        @pl.when(pl.program_id(0) == 0)
        def _():
            ds_scr[...] = jnp.zeros_like(ds_scr)
            dup_ref[...] = jnp.zeros_like(dup_ref)
            dbias_ref[...] = jnp.zeros_like(dbias_ref)
            dnw_ref[...] = jnp.zeros_like(dnw_ref)

        toks, params = _gla_load(q_ref, k_ref, v_ref, gl_ref, up_ref, bias_ref, nw_ref)
        rows = lambda h: slice(h * GLA_DV, (h + 1) * GLA_DV)
        state = [st_ref[rows(h), :] for h in range(GLA_HEADS)]
        _, vjp = jax.vjp(gla_chunk, state, toks, params)
        dstate_in = [ds_scr[rows(h), :] for h in range(GLA_HEADS)]
        dstate, (dq_ref[...], dk_ref[...], dv_ref[...], dgl_ref[...]), (dup, dbias, dnw) = vjp((do_ref[...], dstate_in))
        dup_ref[...] += dup
        dbias_ref[...] += dbias
        dnw_ref[...] += dnw
        for h in range(GLA_HEADS):
            ds_scr[rows(h), :] = dstate[h]

    rev = lambda w: pl.BlockSpec((c, w), lambda i: (nc - 1 - i, 0))
    return pl.pallas_call(
        body, name="gla_bwd", grid=(nc,),
        in_specs=toks_s + params_s + [pl.BlockSpec((512, GLA_DK), lambda i: (nc - 1 - i, 0)), rev(512)],
        out_specs=(rev(256), rev(256), rev(512), rev(128), _full_spec((128, 256)), _full_spec((1, 256)), _full_spec((1, 128))),
        out_shape=(jax.ShapeDtypeStruct((t, 256), F32), jax.ShapeDtypeStruct((t, 256), F32), jax.ShapeDtypeStruct((t, 512), F32),
                   jax.ShapeDtypeStruct((t, 128), F32), jax.ShapeDtypeStruct((128, 256), F32), jax.ShapeDtypeStruct((1, 256), F32),
                   jax.ShapeDtypeStruct((1, 128), F32)),
        scratch_shapes=[pltpu.VMEM((512, GLA_DK), F32)], compiler_params=_cparams(("arbitrary",)))(
            proj0, proj0, proj0, proj0, gk_up, gk_bias, norm_w, states, do)


RWKV_PARAM_SHAPES = [(1, 512), (1, 512), (1, 512), (1, 128), (1, 128), (1, 512), (128, 512), (1, 512), (128, 512),
                     (1, 512), (1, 512), (1, 512), (1, 512), (1, 512)]
PREV_W = 1792
PREV_COLS = [slice(0, 512), slice(512, 1024), slice(1024, 1536), slice(1536, 1664), slice(1664, 1792)]


def _rwkv_load(r_ref, k_ref, v_ref, xw_ref, xa_ref, p_refs):
    toks = (r_ref[...], k_ref[...], v_ref[...], xw_ref[...], xa_ref[...])
    return toks, tuple(p[...] for p in p_refs)


def _rwkv_state(s_ref, prev_ref):
    n = RWKV_N
    S = [s_ref[h * n:(h + 1) * n, :] for h in range(RWKV_HEADS)]
    return (S,) + tuple(prev_ref[0:1, cols] for cols in PREV_COLS)


def _rwkv_put_state(s_ref, prev_ref, state):
    n = RWKV_N
    for h in range(RWKV_HEADS):
        s_ref[h * n:(h + 1) * n, :] = state[0][h]
    for cols, val in zip(PREV_COLS, state[1:]):
        prev_ref[0:1, cols] = val


def _rwkv_specs(c, n=None):
    toks = [_tok_spec(c, 512, _col(512, "r", C0), n), _tok_spec(c, 512, _col(512, "k", C0), n),
            _tok_spec(c, 512, _col(512, "v", C0), n), _tok_spec(c, 128, _col(128, "xw", C0), n),
            _tok_spec(c, 128, _col(128, "xa", C0), n)]
    return toks, [_full_spec(s) for s in RWKV_PARAM_SHAPES]


def _rwkv_fwd(proj0, params):
    t = proj0.shape[0]
    c = RWKV_CHUNK
    nc = t // c
    toks_s, params_s = _rwkv_specs(c)
    npar = len(params)

    def body(*refs):
        tok_refs, p_refs = refs[:5], refs[5:5 + npar]
        o_ref, st_ref, pst_ref, s_scr, prev_scr = refs[5 + npar:]

        @pl.when(pl.program_id(0) == 0)
        def _():
            s_scr[...] = jnp.zeros_like(s_scr)
            prev_scr[...] = jnp.zeros_like(prev_scr)

        st_ref[...] = s_scr[...]
        pst_ref[...] = prev_scr[...]
        toks, prm = _rwkv_load(*tok_refs, p_refs)
        o_ref[...], new = rwkv_chunk(_rwkv_state(s_scr, prev_scr), toks, prm)
        _rwkv_put_state(s_scr, prev_scr, new)

    return pl.pallas_call(
        body, name="rwkv_fwd", grid=(nc,), in_specs=toks_s + params_s,
        out_specs=(_tok_spec(c, 512, 0), pl.BlockSpec((512, RWKV_N), lambda i: (i, 0)), pl.BlockSpec((8, PREV_W), lambda i: (i, 0))),
        out_shape=(jax.ShapeDtypeStruct((t, 512), F32), jax.ShapeDtypeStruct((nc * 512, RWKV_N), F32),
                   jax.ShapeDtypeStruct((nc * 8, PREV_W), F32)),
        scratch_shapes=[pltpu.VMEM((512, RWKV_N), F32), pltpu.VMEM((8, PREV_W), F32)],
        compiler_params=_cparams(("arbitrary",)))(proj0, proj0, proj0, proj0, proj0, *params)


def _rwkv_bwd(proj0, params, states, prevs, do):
    t = proj0.shape[0]
    c = RWKV_CHUNK
    nc = t // c
    toks_s, params_s = _rwkv_specs(c, nc)
    npar = len(params)

    def body(*refs):
        tok_refs, p_refs = refs[:5], refs[5:5 + npar]
        st_ref, pst_ref, do_ref = refs[5 + npar:8 + npar]
        dtok_refs = refs[8 + npar:13 + npar]
        dp_refs = refs[13 + npar:13 + 2 * npar]
        ds_scr, dprev_scr = refs[13 + 2 * npar:]

        @pl.when(pl.program_id(0) == 0)
        def _():
            ds_scr[...] = jnp.zeros_like(ds_scr)
            dprev_scr[...] = jnp.zeros_like(dprev_scr)
            for dp in dp_refs:
                dp[...] = jnp.zeros_like(dp)

        toks, prm = _rwkv_load(*tok_refs, p_refs)
        _, vjp = jax.vjp(rwkv_chunk, _rwkv_state(st_ref, pst_ref), toks, prm)
        dstate, dtoks, dprm = vjp((do_ref[...], _rwkv_state(ds_scr, dprev_scr)))
        for ref, val in zip(dtok_refs, dtoks):
            ref[...] = val
        for ref, val in zip(dp_refs, dprm):
            ref[...] += val
        _rwkv_put_state(ds_scr, dprev_scr, dstate)

    rev = lambda w: pl.BlockSpec((c, w), lambda i: (nc - 1 - i, 0))
    return pl.pallas_call(
        body, name="rwkv_bwd", grid=(nc,),
        in_specs=toks_s + params_s + [pl.BlockSpec((512, RWKV_N), lambda i: (nc - 1 - i, 0)),
                                      pl.BlockSpec((8, PREV_W), lambda i: (nc - 1 - i, 0)), rev(512)],
        out_specs=tuple([rev(512), rev(512), rev(512), rev(128), rev(128)] + params_s),
        out_shape=tuple([jax.ShapeDtypeStruct((t, w), F32) for w in (512, 512, 512, 128, 128)]
                        + [jax.ShapeDtypeStruct(s, F32) for s in RWKV_PARAM_SHAPES]),
        scratch_shapes=[pltpu.VMEM((512, RWKV_N), F32), pltpu.VMEM((8, PREV_W), F32)],
        compiler_params=_cparams(("arbitrary",)))(proj0, proj0, proj0, proj0, proj0, *params, states, prevs, do)


def _swa_load(q_ref, k_ref, v_ref, cos_ref, sin_ref, bq_ref, bk_ref, bv_ref, sk_ref):
    toks = (_heads(q_ref, 16, SWA_HD), _heads(k_ref, 4, SWA_HD), _heads(v_ref, 4, SWA_HD), cos_ref[...], sin_ref[...])
    params = (_heads(bq_ref, 16, SWA_HD), _heads(bk_ref, 4, SWA_HD), _heads(bv_ref, 4, SWA_HD), _heads(sk_ref, 16, 1))
    return toks, params


def _swa_specs(c, n=None):
    toks = [_tok_spec(c, 1024, _col(1024, "q", C1), n), _tok_spec(c, 256, _col(256, "k", C1), n),
            _tok_spec(c, 256, _col(256, "v", C1), n), _tok_spec(c, SWA_HD, 0, n), _tok_spec(c, SWA_HD, 0, n)]
    params = [_full_spec((1, 1024)), _full_spec((1, 256)), _full_spec((1, 256)), _full_spec((1, 16))]
    return toks, params


def _swa_fwd(proj1, cos, sin, bq, bk, bv, sinks):
    t = proj1.shape[0]
    c = WINDOW
    nb = t // c
    toks_s, params_s = _swa_specs(c)

    def body(q_ref, k_ref, v_ref, cos_ref, sin_ref, bq_ref, bk_ref, bv_ref, sk_ref, o_ref, kst_ref, vst_ref, k_scr, v_scr):
        first = pl.program_id(0) == 0

        @pl.when(first)
        def _():
            k_scr[...] = jnp.zeros_like(k_scr)
            v_scr[...] = jnp.zeros_like(v_scr)

        kst_ref[...] = k_scr[...]
        vst_ref[...] = v_scr[...]
        toks, params = _swa_load(q_ref, k_ref, v_ref, cos_ref, sin_ref, bq_ref, bk_ref, bv_ref, sk_ref)
        outs, (kn, vn) = swa_chunk((_heads(k_scr, 4, SWA_HD), _heads(v_scr, 4, SWA_HD)), toks, params, first)
        _put_heads(o_ref, outs, SWA_HD)
        _put_heads(k_scr, kn, SWA_HD)
        _put_heads(v_scr, vn, SWA_HD)

    return pl.pallas_call(
        body, name="swa_fwd", grid=(nb,), in_specs=toks_s + params_s,
        out_specs=(_tok_spec(c, 1024, 0), _tok_spec(c, 256, 0), _tok_spec(c, 256, 0)),
        out_shape=(jax.ShapeDtypeStruct((t, 1024), F32), jax.ShapeDtypeStruct((t, 256), F32), jax.ShapeDtypeStruct((t, 256), F32)),
        scratch_shapes=[pltpu.VMEM((c, 256), F32), pltpu.VMEM((c, 256), F32)],
        compiler_params=_cparams(("arbitrary",)))(proj1, proj1, proj1, cos, sin, bq, bk, bv, sinks)


def _swa_bwd(proj1, cos, sin, bq, bk, bv, sinks, kst, vst, do):
    t = proj1.shape[0]
    c = WINDOW
    nb = t // c
    toks_s, params_s = _swa_specs(c, nb)

    def body(q_ref, k_ref, v_ref, cos_ref, sin_ref, bq_ref, bk_ref, bv_ref, sk_ref, kst_ref, vst_ref, do_ref,
             dq_ref, dk_ref, dv_ref, dbq_ref, dbk_ref, dbv_ref, dsk_ref, dk_scr, dv_scr):
        i = pl.program_id(0)

        @pl.when(i == 0)
        def _():
            dk_scr[...] = jnp.zeros_like(dk_scr)
            dv_scr[...] = jnp.zeros_like(dv_scr)
            for ref in (dbq_ref, dbk_ref, dbv_ref, dsk_ref):
                ref[...] = jnp.zeros_like(ref)

        first = i == nb - 1
        toks, params = _swa_load(q_ref, k_ref, v_ref, cos_ref, sin_ref, bq_ref, bk_ref, bv_ref, sk_ref)
        f = functools.partial(swa_chunk, first=first)
        _, vjp = jax.vjp(f, (_heads(kst_ref, 4, SWA_HD), _heads(vst_ref, 4, SWA_HD)), toks, params)
        dstate_in = (_heads(dk_scr, 4, SWA_HD), _heads(dv_scr, 4, SWA_HD))
        (dkp, dvp), (dq, dk, dv, _, _), (dbq, dbk, dbv, dsk) = vjp((_heads(do_ref, 16, SWA_HD), dstate_in))
        _put_heads(dq_ref, dq, SWA_HD)
        _put_heads(dk_ref, dk, SWA_HD)
        _put_heads(dv_ref, dv, SWA_HD)
        _put_heads(dbq_ref, dbq, SWA_HD, add=True)
        _put_heads(dbk_ref, dbk, SWA_HD, add=True)
        _put_heads(dbv_ref, dbv, SWA_HD, add=True)
        _put_heads(dsk_ref, dsk, 1, add=True)
        _put_heads(dk_scr, dkp, SWA_HD)
        _put_heads(dv_scr, dvp, SWA_HD)

    rev = lambda w: pl.BlockSpec((c, w), lambda i: (nb - 1 - i, 0))
    return pl.pallas_call(
        body, name="swa_bwd", grid=(nb,), in_specs=toks_s + params_s + [rev(256), rev(256), rev(1024)],
        out_specs=(rev(1024), rev(256), rev(256), _full_spec((1, 1024)), _full_spec((1, 256)), _full_spec((1, 256)), _full_spec((1, 16))),
        out_shape=(jax.ShapeDtypeStruct((t, 1024), F32), jax.ShapeDtypeStruct((t, 256), F32), jax.ShapeDtypeStruct((t, 256), F32),
                   jax.ShapeDtypeStruct((1, 1024), F32), jax.ShapeDtypeStruct((1, 256), F32), jax.ShapeDtypeStruct((1, 256), F32),
                   jax.ShapeDtypeStruct((1, 16), F32)),
        scratch_shapes=[pltpu.VMEM((c, 256), F32), pltpu.VMEM((c, 256), F32)],
        compiler_params=_cparams(("arbitrary",)))(proj1, proj1, proj1, cos, sin, bq, bk, bv, sinks, kst, vst, do)


MESH = pl.DeviceIdType.MESH
ANY = pl.BlockSpec(memory_space=pl.ANY)


def _my_place():
    return lax.axis_index("x"), lax.axis_index("y"), lax.axis_index("c")


def _all_gather(shards):
    n = len(shards)

    def body(*refs):
        in_refs, out_refs = refs[:n], refs[n:2 * n]
        send_sems, recv_sems, local_sems = refs[2 * n:]
        x, y, c = _my_place()
        me, sibling = (x, y, c), (x, y, 1 - c)
        chips = [(1 - x, y), (x, 1 - y), (1 - x, 1 - y)]

        def slot(out_ref, place):
            px, py, pc = place
            return out_ref.at[4 * px + 2 * py + pc]

        def copy(a, k, block, to, src=None):
            return pltpu.make_async_remote_copy(
                src_ref=slot(out_refs[a], block) if src is None else src, dst_ref=slot(out_refs[a], block),
                send_sem=send_sems.at[a, k], recv_sem=recv_sems.at[a, k], device_id=to, device_id_type=MESH)

        mine = [pltpu.make_async_copy(in_refs[a], slot(out_refs[a], me), local_sems.at[a]) for a in range(n)]
        for cp in mine:
            cp.start()
        first = []
        for a in range(n):
            first.append(copy(a, 0, me, sibling, src=in_refs[a]))
            first += [copy(a, 1 + j, me, (*chip, c), src=in_refs[a]) for j, chip in enumerate(chips)]
        for cp in first:
            cp.start()
        passed = []
        for j, chip in enumerate(chips):
            for a in range(n):
                copy(a, 1 + j, (*chip, c), me).wait_recv()
                fwd = copy(a, 4 + j, (*chip, c), sibling)
                fwd.start()
                passed.append(fwd)
        for a in range(n):
            copy(a, 0, sibling, me).wait_recv()
            for j, chip in enumerate(chips):
                copy(a, 4 + j, (*chip, 1 - c), me).wait_recv()
        for cp in first + passed:
            cp.wait_send()
        for cp in mine:
            cp.wait()

    return pl.pallas_call(
        body, name="all_gather_weights", in_specs=[ANY] * n, out_specs=[ANY] * n,
        out_shape=[jax.ShapeDtypeStruct((N_DEV,) + s.shape, s.dtype) for s in shards],
        scratch_shapes=[pltpu.SemaphoreType.DMA((n, 7)), pltpu.SemaphoreType.DMA((n, 7)), pltpu.SemaphoreType.DMA((n,))],
        compiler_params=pltpu.CompilerParams(has_side_effects=True))(*shards)


def _exchange(parts, rep):
    n = len(parts)

    def body(*refs):
        in_refs, rep_ref = refs[:n], refs[n]
        out_refs, rep_out = refs[n + 1:2 * n + 1], refs[2 * n + 1]
        send_sems, recv_sems, local_sems = refs[2 * n + 2:]
        x, y, c = _my_place()
        my_idx = 4 * x + 2 * y + c
        local = [pltpu.make_async_copy(in_refs[a].at[my_idx], out_refs[a].at[my_idx], local_sems.at[a]) for a in range(n)]
        local.append(pltpu.make_async_copy(rep_ref, rep_out.at[my_idx], local_sems.at[n]))
        for cp in local:
            cp.start()
        copies = []
        for rel in range(1, N_DEV):
            dx, dy, dc = (rel >> 2) & 1, (rel >> 1) & 1, rel & 1
            px, py, pc = x ^ dx, y ^ dy, c ^ dc
            peer_idx = 4 * px + 2 * py + pc
            for a in range(n):
                copies.append(pltpu.make_async_remote_copy(
                    src_ref=in_refs[a].at[peer_idx], dst_ref=out_refs[a].at[my_idx], send_sem=send_sems.at[a, rel - 1],
                    recv_sem=recv_sems.at[a, rel - 1], device_id=(px, py, pc), device_id_type=MESH))
            copies.append(pltpu.make_async_remote_copy(
                src_ref=rep_ref, dst_ref=rep_out.at[my_idx], send_sem=send_sems.at[n, rel - 1],
                recv_sem=recv_sems.at[n, rel - 1], device_id=(px, py, pc), device_id_type=MESH))
        for cp in copies:
            cp.start()
        for cp in copies:
            cp.wait_recv()
        for cp in copies:
            cp.wait_send()
        for cp in local:
            cp.wait()

    outs = pl.pallas_call(
        body, name="exchange_grads", in_specs=[ANY] * (n + 1), out_specs=[ANY] * (n + 1),
        out_shape=[jax.ShapeDtypeStruct(p.shape, p.dtype) for p in parts] + [jax.ShapeDtypeStruct((N_DEV,) + rep.shape, rep.dtype)],
        scratch_shapes=[pltpu.SemaphoreType.DMA((n + 1, 7)), pltpu.SemaphoreType.DMA((n + 1, 7)), pltpu.SemaphoreType.DMA((n + 1,))],
        compiler_params=pltpu.CompilerParams(has_side_effects=True))(*parts, rep)
    return outs[:n], outs[n]


def _adam_math(w, g, m, v):
    m = ADAM_B1 * m + (1.0 - ADAM_B1) * g
    v = ADAM_B2 * v + (1.0 - ADAM_B2) * (g * g)
    m_hat = m / (1.0 - ADAM_B1 ** ADAM_STEP)
    v_hat = v / (1.0 - ADAM_B2 ** ADAM_STEP)
    delta = -ADAM_LR * (m_hat / (jnp.sqrt(v_hat) + ADAM_EPS) + ADAM_WD * w)
    return delta, m, v


def _adamw(name, w, gslots, m, v, tr):
    r, cc = w.shape
    assert r % tr == 0
    tile = pl.BlockSpec((tr, cc), lambda i: (i, 0))

    def body(w_ref, g_ref, m_ref, v_ref, go_ref, d_ref, mo_ref, vo_ref):
        g = g_ref[0].astype(F32)
        for s in range(1, N_DEV):
            g = g + g_ref[s].astype(F32)
        d, mn, vn = _adam_math(w_ref[...], g, m_ref[...], v_ref[...])
        go_ref[...] = g
        d_ref[...] = d
        mo_ref[...] = mn
        vo_ref[...] = vn

    shp = jax.ShapeDtypeStruct((r, cc), F32)
    return pl.pallas_call(body, name=name, grid=(r // tr,),
                          in_specs=[tile, pl.BlockSpec((N_DEV, tr, cc), lambda i: (0, i, 0)), tile, tile],
                          out_specs=(tile,) * 4, out_shape=(shp,) * 4, compiler_params=_cparams(("arbitrary",)))(w, gslots, m, v)


def _pack(arrays):
    rows = []
    for a in arrays:
        flat = a.reshape(-1).astype(F32)
        pad = (-flat.shape[0]) % LANES
        rows.append(jnp.pad(flat, (0, pad)).reshape(-1, LANES))
    out = jnp.concatenate(rows, axis=0)
    return jnp.pad(out, ((0, (-out.shape[0]) % 8), (0, 0)))


def _unpack(packed, shapes):
    outs, r = [], 0
    for s in shapes:
        n = 1
        for d in s:
            n *= d
        nr = -(-n // LANES)
        outs.append(packed[r:r + nr].reshape(-1)[:n].reshape(s))
        r += nr
    return outs


def _rope_tables(t):
    half = 8
    inv_freq = ROPE_THETA ** (-jnp.arange(half, dtype=F32) / half)
    ang = jnp.arange(t, dtype=F32)[:, None] * inv_freq
    cos = jnp.concatenate([jnp.cos(ang), jnp.cos(ang), jnp.ones((t, SWA_HD - 16), F32)], axis=1)
    sin = jnp.concatenate([jnp.sin(ang), jnp.sin(ang), jnp.zeros((t, SWA_HD - 16), F32)], axis=1)
    return cos, sin


def _pad_to(a, rows=None, cols=None):
    r = 0 if rows is None else rows - a.shape[0]
    c = 0 if cols is None else cols - a.shape[1]
    return jnp.pad(a, ((0, r), (0, c)))


ORIG0 = dict(gq=(0, 256), gk=(256, 256), gv=(512, 512), glow=(1024, 16), r=(1040, 512), k=(1552, 512), v=(2064, 512),
             xw=(2576, 64), xa=(2640, 64), gate=(2704, 1024))
ORIG0_ORDER = ["gq", "gk", "gv", "glow", "r", "k", "v", "xw", "xa", "gate"]


def _w0_to_padded(w):
    out = jnp.zeros((w.shape[0], N0P), w.dtype)
    for name, (off, width) in ORIG0.items():
        out = lax.dynamic_update_slice(out, w[:, off:off + width], (0, C0[name][0]))
    return out


def _w0_from_padded(wp):
    return jnp.concatenate([wp[:, C0[n][0]:C0[n][0] + ORIG0[n][1]] for n in ORIG0_ORDER], axis=1)


def _w1_to_mine(w):
    return jnp.concatenate([w[:, 1536:2560], w[:, :1536]], axis=1)


def _w1_from_mine(w):
    return jnp.concatenate([w[:, 1024:2560], w[:, :1024]], axis=1)


def kernel(x, norm_w, w_in0, gla_gk_up, gla_gk_bias, gla_norm_w, rwkv_mu, rwkv_w0, rwkv_w_up, rwkv_a0, rwkv_a_up, rwkv_k_k, rwkv_k_a, rwkv_r_k, rwkv_ln_w, rwkv_ln_b, w_out0, w_in1, b_in1, attn_sinks, w_out1, b_out1, final_norm_w, loss_target, m_norm_w, m_w_in0, m_gla_gk_up, m_gla_gk_bias, m_gla_norm_w, m_rwkv_mu, m_rwkv_w0, m_rwkv_w_up, m_rwkv_a0, m_rwkv_a_up, m_rwkv_k_k, m_rwkv_k_a, m_rwkv_r_k, m_rwkv_ln_w, m_rwkv_ln_b, m_w_out0, m_w_in1, m_b_in1, m_attn_sinks, m_w_out1, m_b_out1, m_final_norm_w, v_norm_w, v_w_in0, v_gla_gk_up, v_gla_gk_bias, v_gla_norm_w, v_rwkv_mu, v_rwkv_w0, v_rwkv_w_up, v_rwkv_a0, v_rwkv_a_up, v_rwkv_k_k, v_rwkv_k_a, v_rwkv_r_k, v_rwkv_ln_w, v_rwkv_ln_b, v_w_out0, v_w_in1, v_b_in1, v_attn_sinks, v_w_out1, v_b_out1, v_final_norm_w):
    weights = dict(norm_w=norm_w, w_in0=w_in0, gla_gk_up=gla_gk_up, gla_gk_bias=gla_gk_bias, gla_norm_w=gla_norm_w, rwkv_mu=rwkv_mu,
                   rwkv_w0=rwkv_w0, rwkv_w_up=rwkv_w_up, rwkv_a0=rwkv_a0, rwkv_a_up=rwkv_a_up, rwkv_k_k=rwkv_k_k, rwkv_k_a=rwkv_k_a,
                   rwkv_r_k=rwkv_r_k, rwkv_ln_w=rwkv_ln_w, rwkv_ln_b=rwkv_ln_b, w_out0=w_out0, w_in1=w_in1, b_in1=b_in1,
                   attn_sinks=attn_sinks, w_out1=w_out1, b_out1=b_out1, final_norm_w=final_norm_w)
    moms = dict(norm_w=m_norm_w, w_in0=m_w_in0, gla_gk_up=m_gla_gk_up, gla_gk_bias=m_gla_gk_bias, gla_norm_w=m_gla_norm_w,
                rwkv_mu=m_rwkv_mu, rwkv_w0=m_rwkv_w0, rwkv_w_up=m_rwkv_w_up, rwkv_a0=m_rwkv_a0, rwkv_a_up=m_rwkv_a_up,
                rwkv_k_k=m_rwkv_k_k, rwkv_k_a=m_rwkv_k_a, rwkv_r_k=m_rwkv_r_k, rwkv_ln_w=m_rwkv_ln_w, rwkv_ln_b=m_rwkv_ln_b,
                w_out0=m_w_out0, w_in1=m_w_in1, b_in1=m_b_in1, attn_sinks=m_attn_sinks, w_out1=m_w_out1, b_out1=m_b_out1,
                final_norm_w=m_final_norm_w)
    vars_ = dict(norm_w=v_norm_w, w_in0=v_w_in0, gla_gk_up=v_gla_gk_up, gla_gk_bias=v_gla_gk_bias, gla_norm_w=v_gla_norm_w,
                 rwkv_mu=v_rwkv_mu, rwkv_w0=v_rwkv_w0, rwkv_w_up=v_rwkv_w_up, rwkv_a0=v_rwkv_a0, rwkv_a_up=v_rwkv_a_up,
                 rwkv_k_k=v_rwkv_k_k, rwkv_k_a=v_rwkv_k_a, rwkv_r_k=v_rwkv_r_k, rwkv_ln_w=v_rwkv_ln_w, rwkv_ln_b=v_rwkv_ln_b,
                 w_out0=v_w_out0, w_in1=v_w_in1, b_in1=v_b_in1, attn_sinks=v_attn_sinks, w_out1=v_w_out1, b_out1=v_b_out1,
                 final_norm_w=v_final_norm_w)
    names = list(weights)
    big = ["w_in0", "w_out0", "w_in1", "w_out1"]
    small_sharded = ["gla_gk_up", "rwkv_w_up", "rwkv_a_up", "b_in1", "b_out1"]
    replicated = [n for n in names if n not in big and n not in small_sharded]

    xs = x[0]
    tgt = loss_target[0]
    t = xs.shape[0]

    small_shard_pack = _pack([weights[n] for n in small_sharded])
    g_in0, g_out0, g_in1, g_out1, g_small = _all_gather(
        [w_in0[0].astype(BF16), w_out0[0].astype(BF16), w_in1[0].astype(BF16), w_out1[0].astype(BF16), small_shard_pack])
    w0p = _w0_to_padded(jnp.transpose(g_in0, (1, 0, 2)).reshape(D_MODEL, -1))
    wo0 = g_out0.reshape(1024, D_MODEL)
    w1p = _w1_to_mine(jnp.transpose(g_in1, (1, 0, 2)).reshape(D_MODEL, -1))
    wo1 = g_out1.reshape(1024, D_MODEL)
    small_shapes = [weights[n].shape for n in small_sharded]
    per_dev = [_unpack(g_small[d], small_shapes) for d in range(N_DEV)]
    gk_up = jnp.concatenate([p[0][0] for p in per_dev], axis=1)
    w_up = jnp.concatenate([p[1][0] for p in per_dev], axis=1)
    a_up = jnp.concatenate([p[2][0] for p in per_dev], axis=1)
    b_in = jnp.concatenate([p[3] for p in per_dev], axis=1)
    b_out = jnp.concatenate([p[4] for p in per_dev], axis=1)

    gk_up_p = _pad_to(gk_up, rows=128)
    mu = rwkv_mu
    rwkv_params = [mu[:, 0:512], mu[:, 512:1024], mu[:, 1024:1536], _pad_to(mu[:, 1536:1600], cols=128), _pad_to(mu[:, 1600:1664], cols=128),
                   rwkv_w0, _pad_to(w_up, rows=128), rwkv_a0, _pad_to(a_up, rows=128), rwkv_k_k, rwkv_k_a, rwkv_r_k.reshape(1, 512),
                   rwkv_ln_w, rwkv_ln_b]
    bq, bk, bv = b_in[:, :1024], b_in[:, 1024:1280], b_in[:, 1280:1536]
    cos, sin = _rope_tables(t)
    nw0, nw1, fw = norm_w[0:1], norm_w[1:2], final_norm_w.reshape(1, D_MODEL)

    hn0 = _norm_fwd("norm0_fwd", xs, nw0)
    proj0 = _matmul("proj0", hn0, w0p, "nn", 512, 1024, 1024)
    o_a, gla_states = _gla_fwd(proj0, gk_up_p, gla_gk_bias, gla_norm_w)
    o_b, rwkv_states, rwkv_prevs = _rwkv_fwd(proj0, rwkv_params)
    og0 = _gate_fwd("gate0_fwd", [o_a, o_b], proj0)
    y0 = _matmul("out0", og0, wo0, "nn", 512, 1024, 1024)
    h1, hn1 = _norm_fwd("norm1_fwd", xs, nw1, y0)
    proj1 = _matmul("proj1", hn1, w1p, "nn", 512, 512, 1024)
    o_c, kst, vst = _swa_fwd(proj1, cos, sin, bq, bk, bv, attn_sinks)
    og1 = _gate_fwd("gate1_fwd", [o_c], proj1)
    y1 = _matmul("out1", og1, wo1, "nn", 512, 1024, 1024)
    dh2, loss_part, d_b_out, d_fw = _top(h1, y1, b_out, fw, tgt)

    dog1 = _matmul("out1_dx", dh2, wo1, "nt", 512, 1024, 1024)
    d_wo1 = _matmul("out1_dw", og1, dh2, "tn", 1024, 512, 512)
    d_oc, d_gate1 = _gate_bwd("gate1_bwd", [o_c], proj1, dog1)
    dq, dk, dv, d_bq, d_bk, d_bv, d_sinks = _swa_bwd(proj1, cos, sin, bq, bk, bv, attn_sinks, kst, vst, d_oc)
    dproj1 = jnp.concatenate([d_gate1, dq, dk, dv], axis=1)
    dhn1 = _matmul("proj1_dx", dproj1, w1p, "nt", 512, 1024, 512)
    d_w1p = _matmul("proj1_dw", hn1, dproj1, "tn", 1024, 512, 512)
    dh1, d_nw1 = _norm_bwd("norm1_bwd", h1, nw1, dhn1, dh2)
    dog0 = _matmul("out0_dx", dh1, wo0, "nt", 512, 1024, 1024)
    d_wo0 = _matmul("out0_dw", og0, dh1, "tn", 1024, 512, 512)
    d_oa, d_ob, d_gate0 = _gate_bwd("gate0_bwd", [o_a, o_b], proj0, dog0)
    dgq, dgk, dgv, dglow, d_gk_up, d_gk_bias, d_gla_nw = _gla_bwd(proj0, gk_up_p, gla_gk_bias, gla_norm_w, gla_states, d_oa)
    rb = _rwkv_bwd(proj0, rwkv_params, rwkv_states, rwkv_prevs, d_ob)
    dr, dkk, dvv, dxw, dxa = rb[:5]
    d_rp = rb[5:]
    dproj0 = jnp.concatenate([d_gate0, dgv, dr, dkk, dvv, dgq, dgk, dglow, dxw, dxa, jnp.zeros((t, 128), F32)], axis=1)
    dhn0 = _matmul("proj0_dx", dproj0, w0p, "nt", 512, 1024, 1024)
    d_w0p = _matmul("proj0_dw", hn0, dproj0, "tn", 1024, 512, 512)
    grad_x, d_nw0 = _norm_bwd("norm0_bwd", xs, nw0, dhn0, dh1)

    contrib = dict(
        norm_w=jnp.concatenate([d_nw0, d_nw1], axis=0), gla_gk_bias=d_gk_bias, gla_norm_w=d_gla_nw,
        rwkv_mu=jnp.concatenate([d_rp[0], d_rp[1], d_rp[2], d_rp[3][:, :64], d_rp[4][:, :64]], axis=1),
        rwkv_w0=d_rp[5], rwkv_a0=d_rp[7], rwkv_k_k=d_rp[9], rwkv_k_a=d_rp[10], rwkv_r_k=d_rp[11].reshape(1, 8, 64),
        rwkv_ln_w=d_rp[12], rwkv_ln_b=d_rp[13], attn_sinks=d_sinks, final_norm_w=d_fw.reshape(D_MODEL))
    rep_pack = _pack([contrib[n] for n in replicated] + [loss_part[:, :1]])

    d_w0 = _w0_from_padded(d_w0p)
    d_w1 = _w1_from_mine(d_w1p)
    d_b_in = jnp.concatenate([d_bq, d_bk, d_bv], axis=1)
    full_small = [d_gk_up[:16], d_rp[6][:64], d_rp[8][:64], d_b_in, d_b_out]
    split_cols = lambda a: jnp.transpose(a.reshape(a.shape[0], N_DEV, -1), (1, 0, 2))
    small_parts = [split_cols(a) for a in full_small]
    small_pack = jnp.stack([_pack([sp[d] for sp in small_parts]) for d in range(N_DEV)])
    parts = [split_cols(d_w0).astype(BF16), d_wo0.reshape(N_DEV, 128, D_MODEL).astype(BF16), split_cols(d_w1).astype(BF16),
             d_wo1.reshape(N_DEV, 128, D_MODEL).astype(BF16), small_pack]
    (r_in0, r_out0, r_in1, r_out1, r_small), r_rep = _exchange(parts, rep_pack)

    res = {}
    res["w_in0"] = _adamw("adamw_w_in0", w_in0[0], r_in0, m_w_in0[0], v_w_in0[0], 256)
    res["w_out0"] = _adamw("adamw_w_out0", w_out0[0], r_out0, m_w_out0[0], v_w_out0[0], 128)
    res["w_in1"] = _adamw("adamw_w_in1", w_in1[0], r_in1, m_w_in1[0], v_w_in1[0], 256)
    res["w_out1"] = _adamw("adamw_w_out1", w_out1[0], r_out1, m_w_out1[0], v_w_out1[0], 128)
    for n in big:
        res[n] = tuple(a[None] for a in res[n])
    small_names = small_sharded + replicated
    slots = jnp.concatenate([r_small, r_rep], axis=1)
    n_shard_rows = r_small.shape[1]
    pk = lambda d: jnp.concatenate([_pack([d[n] for n in small_sharded]), _pack([d[n] for n in replicated] + [jnp.zeros((1, 1), F32)])], axis=0)
    g_p, d_p, m_p, v_p = _adamw("adamw_small", pk(weights), slots, pk(moms), pk(vars_), slots.shape[1])
    sh_shapes = [weights[n].shape for n in small_sharded]
    rep_shapes = [weights[n].shape for n in replicated] + [(1, 1)]
    for i, packed in enumerate((g_p, d_p, m_p, v_p)):
        vals = _unpack(packed[:n_shard_rows], sh_shapes) + _unpack(packed[n_shard_rows:], rep_shapes)
        for n, val in zip(small_names, vals):
            res.setdefault(n, [None] * 4)[i] = val
        if i == 0:
            loss = vals[-1].reshape(())
    return (loss, grad_x[None], *[res[n][0] for n in names], *[res[n][1] for n in names],
            *[res[n][2] for n in names], *[res[n][3] for n in names])
```

```python
import functools

import jax
import jax.numpy as jnp
from jax import lax
from jax.experimental import pallas as pl
from jax.experimental.pallas import tpu as pltpu

F32 = jnp.float32
BF16 = jnp.bfloat16
HI = lax.Precision.HIGHEST

D_MODEL = 1024
NORM_EPS = 1e-5
GLA_HEADS, GLA_DK, GLA_DV = 4, 64, 128
GLA_NORMALIZER = 16.0
GLA_CHUNK = 64
RWKV_HEADS, RWKV_N = 8, 64
RWKV_LN_EPS = 64e-5
RWKV_CHUNK = 128
SWA_Q_HEADS, SWA_KV_HEADS, SWA_GROUP, SWA_HD = 16, 4, 4, 64
WINDOW = 128
ROPE_THETA = 500000.0
NEG = -1e30
N_DEV = 8
LANES = 128

ADAM_LR, ADAM_B1, ADAM_B2, ADAM_EPS, ADAM_WD, ADAM_STEP = 0.001, 0.9, 0.999, 1e-08, 0.01, 10

N0P = 4096
C0 = dict(gate=(0, 1024), gv=(1024, 512), r=(1536, 512), k=(2048, 512), v=(2560, 512), gq=(3072, 256), gk=(3328, 256),
          glow=(3584, 128), xw=(3712, 128), xa=(3840, 128))
N1P = 2560
C1 = dict(gate=(0, 1024), q=(1024, 1024), k=(2048, 256), v=(2304, 256))

VMEM_LIMIT = 56 * 1024 * 1024

P_LORA = 1
P_GLA = 1
P_RWKV_G = 3
P_RWKV = 1
P_SWA = 1
P_ROPE = 3


def _cparams(sem=None):
    return pltpu.CompilerParams(dimension_semantics=sem, vmem_limit_bytes=VMEM_LIMIT)


DIMS = dict(nn=(((1,), (0,)), ((), ())), nt=(((1,), (1,)), ((), ())), tn=(((0,), (0,)), ((), ())))


def _split_bf16(a):
    hi = a.astype(BF16)
    return hi, (a - hi.astype(F32)).astype(BF16)


def _dot(a, b, mode, passes):
    dg = lambda p, q: lax.dot_general(p, q, DIMS[mode], preferred_element_type=F32)
    if passes == 1:
        return dg(a.astype(BF16), b.astype(BF16))
    if passes == 3:
        (ah, al), (bh, bl) = _split_bf16(a), _split_bf16(b)
        return dg(ah, bh) + dg(al, bh) + dg(ah, bl)
    return lax.dot_general(a, b, DIMS[mode], precision=HI, preferred_element_type=F32)


@functools.partial(jax.custom_vjp, nondiff_argnums=(2, 3))
def mmx(a, b, mode, passes):
    return _dot(a, b, mode, passes)


def _mmx_fwd(a, b, mode, passes):
    return _dot(a, b, mode, passes), (a, b)


def _mmx_bwd(mode, passes, res, g):
    a, b = res
    if mode == "nn":
        return _dot(g, b, "nt", passes), _dot(a, g, "tn", passes)
    if mode == "nt":
        return _dot(g, b, "nn", passes), _dot(g, a, "tn", passes)
    return _dot(b, g, "nt", passes), _dot(a, g, "nn", passes)


mmx.defvjp(_mmx_fwd, _mmx_bwd)


def _tri_dot(tri, x):
    t = tri.astype(BF16)
    x1 = x.astype(BF16)
    r1 = x - x1.astype(F32)
    x2 = r1.astype(BF16)
    x3 = (r1 - x2.astype(F32)).astype(BF16)
    dg = lambda q: jnp.dot(t, q, preferred_element_type=F32)
    return dg(x1) + dg(x2) + dg(x3)


@jax.custom_vjp
def cumsum_rows(x):
    return _tri_dot(tril_ones(x.shape[0]), x)


def _cumsum_fwd(x):
    return cumsum_rows(x), None


def _cumsum_bwd(_, g):
    i, j = _iota2(g.shape[0], g.shape[0])
    return (_tri_dot(jnp.where(i <= j, 1.0, 0.0).astype(F32), g),)


cumsum_rows.defvjp(_cumsum_fwd, _cumsum_bwd)


def _head_dot(x):
    i, j = _iota2(LANES, LANES)
    shift = RWKV_N.bit_length() - 1
    same = jnp.where(jnp.right_shift(i, shift) == jnp.right_shift(j, shift), 1.0, 0.0).astype(F32)
    return jnp.concatenate([_ones_right(x[:, g * LANES:(g + 1) * LANES], same) for g in range(x.shape[1] // LANES)], axis=1)


def _ones_right(x, ones):
    t = ones.astype(BF16)
    x1 = x.astype(BF16)
    r1 = x - x1.astype(F32)
    x2 = r1.astype(BF16)
    x3 = (r1 - x2.astype(F32)).astype(BF16)
    dg = lambda q: jnp.dot(q, t, preferred_element_type=F32)
    return dg(x1) + dg(x2) + dg(x3)


@jax.custom_vjp
def head_sum(x):
    return _head_dot(x)


def _head_sum_fwd(x):
    return head_sum(x), None


def _head_sum_bwd(_, g):
    return (_head_dot(g),)


head_sum.defvjp(_head_sum_fwd, _head_sum_bwd)


def cat_rows(*xs):
    return jnp.concatenate(xs, axis=0)


def _iota2(n, m):
    return lax.broadcasted_iota(jnp.int32, (n, m), 0), lax.broadcasted_iota(jnp.int32, (n, m), 1)


def tril_ones(c, strict=False):
    i, j = _iota2(c, c)
    return jnp.where((i > j) if strict else (i >= j), 1.0, 0.0).astype(F32)


def row_of(x, r):
    i = lax.broadcasted_iota(jnp.int32, x.shape, 0)
    return jnp.sum(jnp.where(i == r, x, 0.0), axis=0, keepdims=True)


@jax.custom_vjp
def shift_rows(x, prev):
    r = lax.broadcasted_iota(jnp.int32, x.shape, 0)
    return jnp.where(r == 0, prev, pltpu.roll(x, 1, 0))


def _shift_fwd(x, prev):
    return shift_rows(x, prev), None


def _shift_bwd(_, g):
    c = g.shape[0]
    r = lax.broadcasted_iota(jnp.int32, g.shape, 0)
    return jnp.where(r == c - 1, 0.0, pltpu.roll(g, c - 1, 0)), row_of(g, 0)


shift_rows.defvjp(_shift_fwd, _shift_bwd)


def log_sigmoid(x):
    return jnp.minimum(x, 0.0) - jnp.log(1.0 + jnp.exp(-jnp.abs(x)))


def softplus(x):
    return jnp.maximum(x, 0.0) + jnp.log(1.0 + jnp.exp(-jnp.abs(x)))


def sigmoid(x):
    return 1.0 / (1.0 + jnp.exp(-x))


def rms(x, w, eps=NORM_EPS):
    return x * lax.rsqrt(jnp.mean(x * x, axis=-1, keepdims=True) + eps) * w


def gla_chunk(state, toks, params):
    q, k, v, glow = toks
    gk_up, bias, norm_w = params
    c = glow.shape[0]
    heads = range(GLA_HEADS)
    hk = lambda x, h: x[:, h * GLA_DK:(h + 1) * GLA_DK]
    hv = lambda x, h: x[:, h * GLA_DV:(h + 1) * GLA_DV]
    ltri = tril_ones(c)
    g = log_sigmoid(mmx(glow, gk_up, "nn", P_LORA) + bias) / GLA_NORMALIZER
    b = cumsum_rows(g)
    ref = lax.stop_gradient(row_of(b, c // 2))
    last = row_of(b, c - 1)
    ql = q * (GLA_DK ** -0.5) * jnp.exp(b - ref)
    kr = k * jnp.exp(ref - b)
    kl = k * jnp.exp(last - b)
    e_ref, e_last = jnp.exp(ref), jnp.exp(last)
    sc = [mmx(hk(ql, h), cat_rows(hk(kr, h), state[h] * hk(e_ref, h)), "nt", P_GLA) for h in heads]
    o = [mmx(sc[h][:, :c] * ltri, hv(v, h), "nn", P_GLA) + sc[h][:, c:] for h in heads]
    s1 = [state[h] * hk(e_last, h) + mmx(hv(v, h), hk(kl, h), "tn", P_GLA) for h in heads]
    o = [x * lax.rsqrt(jnp.mean(x * x, axis=-1, keepdims=True) + NORM_EPS) * norm_w for x in o]
    return jnp.concatenate(o, axis=1), s1


def rwkv_chunk(state, toks, params):
    S, pr, pk, pv, pxw, pxa = state
    r_, k_, v_, xw_, xa_ = toks
    mu_r, mu_k, mu_v, mu_xw, mu_xa, w0, w_up, a0, a_up, k_k, k_a, r_k, ln_w, ln_b = params
    c, n = xw_.shape[0], RWKV_N
    heads = range(RWKV_HEADS)
    hs = lambda x, h: x[:, h * n:(h + 1) * n]
    ltri = tril_ones(c)
    stri = tril_ones(c, strict=True)

    def lerp(x, prev, mu):
        return x + (shift_rows(x, prev) - x) * mu

    xw = jnp.tanh(lerp(xw_, pxw, mu_xw))
    xa = lerp(xa_, pxa, mu_xa)
    r = lerp(r_, pr, mu_r)
    k = lerp(k_, pk, mu_k)
    v = lerp(v_, pv, mu_v)
    w = -softplus(-(w0 + mmx(xw, w_up, "nn", P_LORA))) - 0.5
    lw = -jnp.exp(w)
    asig = sigmoid(a0 + mmx(xa, a_up, "nn", P_LORA))
    kk = k * k_k
    kk = kk / jnp.maximum(jnp.sqrt(head_sum(kk * kk)), 1e-12)
    k2 = k * (1.0 + (asig - 1.0) * k_a)
    b = kk * asig
    cum = cumsum_rows(lw)
    ref = lax.stop_gradient(row_of(cum, c // 2))
    last = row_of(cum, c - 1)
    at = -kk * jnp.exp(cum - lw - ref)
    rt = r * jnp.exp(cum - ref)
    e_out = jnp.exp(ref - cum)
    bt, kt = b * e_out, k2 * e_out
    e_tail = jnp.exp(last - cum)
    bl, kl = b * e_tail, k2 * e_tail
    e_ref, e_last = jnp.exp(ref), jnp.exp(last)
    g = [mmx(cat_rows(hs(at, h), hs(rt, h)), cat_rows(hs(bt, h), hs(kt, h), S[h] * hs(e_ref, h)), "nt", P_RWKV_G) for h in heads]
    aab = [x[:c, :c] * stri for x in g]
    aak = [x[:c, c:2 * c] * stri for x in g]
    arb = [x[c:, :c] * ltri for x in g]
    ark = [x[c:, c:2 * c] * ltri for x in g]
    av = [mmx(cat_rows(aak[h], ark[h]), hs(v, h), "nn", P_RWKV) for h in heads]
    u = [g[h][:c, 2 * c:] + av[h][:c] for h in heads]
    p = aab
    n_double = max(1, (c - 1).bit_length())
    for it in range(n_double):
        if it + 1 < n_double:
            y = [mmx(p[h], jnp.concatenate([p[h], u[h]], axis=1), "nn", P_RWKV) for h in heads]
            u = [u[h] + y[h][:, c:] for h in heads]
            p = [y[h][:, :c] for h in heads]
        else:
            u = [u[h] + mmx(p[h], u[h], "nn", P_RWKV) for h in heads]
    o = [g[h][c:, 2 * c:] + av[h][c:] + mmx(arb[h], u[h], "nn", P_RWKV) for h in heads]
    s1 = [S[h] * hs(e_last, h) + mmx(cat_rows(u[h], hs(v, h)), cat_rows(hs(bl, h), hs(kl, h)), "tn", P_RWKV) for h in heads]
    o = jnp.concatenate(o, axis=1)
    d = o - head_sum(o) * (1.0 / n)
    var = head_sum(d * d) * (1.0 / n)
    o = d * lax.rsqrt(var + RWKV_LN_EPS) * ln_w + ln_b + head_sum(r * k2 * r_k) * v
    new_state = (s1, row_of(r_, c - 1), row_of(k_, c - 1), row_of(v_, c - 1), row_of(xw_, c - 1), row_of(xa_, c - 1))
    return o, new_state


def rope_mat():
    i, j = _iota2(SWA_HD, SWA_HD)
    plus = (j >= 8) & (j < 16) & (i == j - 8)
    minus = (j < 8) & (i == j + 8)
    return jnp.where(plus, 1.0, 0.0).astype(F32) - jnp.where(minus, 1.0, 0.0).astype(F32)


def swa_chunk(state, toks, params, first):
    kprev, vprev = state
    q_, k_, v_, cos, sin = toks
    bq, bk, bv, sinks = params
    c = cos.shape[0]
    ng = SWA_GROUP
    rm = rope_mat()
    qi, kj = _iota2(ng * c, 2 * c)
    qpos = qi & (c - 1)
    ok = ((kj < c) & (kj > qpos) & jnp.logical_not(first)) | ((kj >= c) & (qpos >= kj - c))
    cos_g, sin_g = cat_rows(*[cos] * ng), cat_rows(*[sin] * ng)

    def rope(x, cs, sn):
        return x * cs + mmx(x, rm, "nn", P_ROPE) * sn

    groups = range(SWA_KV_HEADS)
    hs = lambda g: range(g * ng, (g + 1) * ng)
    k = [rope(k_[g] + bk[g], cos, sin) for g in groups]
    v = [v_[g] + bv[g] for g in groups]
    q = [rope(cat_rows(*[q_[h] + bq[h] for h in hs(g)]), cos_g, sin_g) * (SWA_HD ** -0.5) for g in groups]
    s = [jnp.where(ok, mmx(q[g], cat_rows(kprev[g], k[g]), "nt", P_SWA), NEG) for g in groups]
    sink = [cat_rows(*[jnp.broadcast_to(sinks[h], (c, 1)) for h in hs(g)]) for g in groups]
    m = [lax.stop_gradient(jnp.maximum(jnp.max(s[g], axis=-1, keepdims=True), sink[g])) for g in groups]
    p = [jnp.exp(s[g] - m[g]) for g in groups]
    den = [jnp.sum(p[g], axis=-1, keepdims=True) + jnp.exp(sink[g] - m[g]) for g in groups]
    o = [mmx(p[g], cat_rows(vprev[g], v[g]), "nn", P_SWA) / den[g] for g in groups]
    outs = [o[g][j * c:(j + 1) * c] for g in groups for j in range(ng)]
    return outs, (k, v)


def _heads(ref, n, w, rows=slice(None)):
    return [ref[rows, h * w:(h + 1) * w] for h in range(n)]


def _put_heads(ref, vals, w, rows=slice(None), add=False):
    for h, val in enumerate(vals):
        if add:
            ref[rows, h * w:(h + 1) * w] += val
        else:
            ref[rows, h * w:(h + 1) * w] = val


def _col(block_w, name, table):
    off, w = table[name]
    assert off % block_w == 0 and w % block_w == 0
    return off // block_w


def _tok_spec(c, w, colblock, n=None):
    if n is None:
        return pl.BlockSpec((c, w), lambda i: (i, colblock))
    return pl.BlockSpec((c, w), lambda i: (n - 1 - i, colblock))


def _full_spec(shape):
    return pl.BlockSpec(shape, lambda i: (0,) * len(shape))


def _matmul(name, a, b, mode, tm, tn, out_dtype=F32):
    (m, kd) = a.shape
    n = b.shape[1] if mode == "nn" else b.shape[0]
    assert m % tm == 0 and n % tn == 0
    a_spec = pl.BlockSpec((tm, kd), lambda j, i: (i, 0))
    b_spec = pl.BlockSpec((kd, tn), lambda j, i: (0, j)) if mode == "nn" else pl.BlockSpec((tn, kd), lambda j, i: (j, 0))

    def body(a_ref, b_ref, o_ref):
        o_ref[...] = lax.dot_general(a_ref[...].astype(BF16), b_ref[...].astype(BF16), DIMS[mode],
                                     preferred_element_type=F32).astype(out_dtype)

    return pl.pallas_call(
        body, name=name, grid=(n // tn, m // tm), in_specs=[a_spec, b_spec],
        out_specs=pl.BlockSpec((tm, tn), lambda j, i: (i, j)), out_shape=jax.ShapeDtypeStruct((m, n), out_dtype),
        compiler_params=_cparams(("arbitrary", "arbitrary")))(a, b)


TOK_TILE = 512


def _norm_fwd(name, x, w, y=None):
    t, d = x.shape
    tile = pl.BlockSpec((TOK_TILE, d), lambda i: (i, 0))
    tile_t = pl.BlockSpec((d, TOK_TILE), lambda i: (0, i))

    def body(*refs):
        if y is None:
            x_ref, w_ref, hn_ref, hnt_ref = refs
            h = x_ref[...]
        else:
            x_ref, y_ref, w_ref, h_ref, hn_ref, hnt_ref = refs
            h = x_ref[...] + y_ref[...]
            h_ref[...] = h
        hn = rms(h, w_ref[...])
        hn_ref[...] = hn.astype(BF16)
        hnt_ref[...] = hn.T.astype(BF16)

    ins = [x, w] if y is None else [x, y, w]
    in_specs = [tile, _full_spec((1, d))] if y is None else [tile, tile, _full_spec((1, d))]
    hn_shapes = (jax.ShapeDtypeStruct((t, d), BF16), jax.ShapeDtypeStruct((d, t), BF16))
    out_shape = hn_shapes if y is None else (jax.ShapeDtypeStruct((t, d), F32),) + hn_shapes
    out_specs = (tile, tile_t) if y is None else (tile, tile, tile_t)
    return pl.pallas_call(body, name=name, grid=(t // TOK_TILE,), in_specs=in_specs, out_specs=out_specs, out_shape=out_shape,
                          compiler_params=_cparams(("arbitrary",)))(*ins)


def _norm_bwd(name, h, w, dhn, dres):
    t, d = h.shape
    tile = pl.BlockSpec((TOK_TILE, d), lambda i: (i, 0))

    def body(h_ref, w_ref, dhn_ref, dres_ref, dx_ref, dw_ref):
        @pl.when(pl.program_id(0) == 0)
        def _():
            dw_ref[...] = jnp.zeros_like(dw_ref)

        _, vjp = jax.vjp(rms, h_ref[...], w_ref[...])
        dh, dw = vjp(dhn_ref[...])
        dx_ref[...] = dh + dres_ref[...]
        dw_ref[...] += dw

    return pl.pallas_call(body, name=name, grid=(t // TOK_TILE,), in_specs=[tile, _full_spec((1, d)), tile, tile],
                          out_specs=(tile, _full_spec((1, d))),
                          out_shape=(jax.ShapeDtypeStruct((t, d), F32), jax.ShapeDtypeStruct((1, d), F32)),
                          compiler_params=_cparams(("arbitrary",)))(h, w, dhn, dres)


def _gate_fwd(name, outs, proj):
    t = proj.shape[0]
    widths = [o.shape[1] for o in outs]
    n = len(outs)

    def body(*refs):
        o_refs, g_ref, og_ref, ogt_ref = refs[:n], refs[n], refs[n + 1], refs[n + 2]
        c = 0
        for o_ref, w in zip(o_refs, widths):
            g = g_ref[:, c:c + w]
            og = o_ref[...] * (g * sigmoid(g))
            og_ref[:, c:c + w] = og.astype(BF16)
            ogt_ref[c:c + w, :] = og.T.astype(BF16)
            c += w

    in_specs = [pl.BlockSpec((TOK_TILE, w), lambda i: (i, 0)) for w in widths] + [pl.BlockSpec((TOK_TILE, 1024), lambda i: (i, 0))]
    return pl.pallas_call(body, name=name, grid=(t // TOK_TILE,), in_specs=in_specs,
                          out_specs=(pl.BlockSpec((TOK_TILE, 1024), lambda i: (i, 0)), pl.BlockSpec((1024, TOK_TILE), lambda i: (0, i))),
                          out_shape=(jax.ShapeDtypeStruct((t, 1024), BF16), jax.ShapeDtypeStruct((1024, t), BF16)),
                          compiler_params=_cparams(("arbitrary",)))(*outs, proj)


def _gate_bwd(name, outs, proj, dog):
    t = proj.shape[0]
    widths = [o.shape[1] for o in outs]
    n = len(outs)

    def body(*refs):
        o_refs, g_ref, dog_ref = refs[:n], refs[n], refs[n + 1]
        do_refs, dg_ref = refs[n + 2:2 * n + 2], refs[2 * n + 2]
        c = 0
        for o_ref, do_ref, w in zip(o_refs, do_refs, widths):
            g = g_ref[:, c:c + w]
            dog_ = dog_ref[:, c:c + w]
            s = sigmoid(g)
            do_ref[...] = dog_ * (g * s)
            dg_ref[:, c:c + w] = dog_ * o_ref[...] * (s * (1.0 + g * (1.0 - s)))
            c += w

    o_specs = [pl.BlockSpec((TOK_TILE, w), lambda i: (i, 0)) for w in widths]
    wide = pl.BlockSpec((TOK_TILE, 1024), lambda i: (i, 0))
    return pl.pallas_call(body, name=name, grid=(t // TOK_TILE,), in_specs=o_specs + [wide, wide], out_specs=tuple(o_specs) + (wide,),
                          out_shape=tuple(jax.ShapeDtypeStruct((t, w), F32) for w in widths) + (jax.ShapeDtypeStruct((t, 1024), F32),),
                          compiler_params=_cparams(("arbitrary",)))(*outs, proj, dog)


def _top(h1, y1, b_out1, fw, target):
    t, d = h1.shape
    tile = pl.BlockSpec((TOK_TILE, d), lambda i: (i, 0))
    vec = _full_spec((1, d))

    def body(h1_ref, y1_ref, b_ref, fw_ref, tgt_ref, dh2_ref, loss_ref, db_ref, dfw_ref):
        @pl.when(pl.program_id(0) == 0)
        def _():
            loss_ref[...] = jnp.zeros_like(loss_ref)
            db_ref[...] = jnp.zeros_like(db_ref)
            dfw_ref[...] = jnp.zeros_like(dfw_ref)

        tgt = tgt_ref[...]

        def f(h2, w):
            err = rms(h2, w) - tgt
            per_tok = jnp.mean(err * err, axis=-1, keepdims=True)
            return 0.5 * jnp.sum(per_tok, axis=0, keepdims=True)

        h2 = h1_ref[...] + y1_ref[...] + b_ref[...]
        loss, vjp = jax.vjp(f, h2, fw_ref[...])
        dh2, dfw = vjp(jnp.ones((1, 1), F32))
        dh2_ref[...] = dh2
        loss_ref[...] += jnp.broadcast_to(loss, loss_ref.shape)
        db_ref[...] += jnp.sum(dh2, axis=0, keepdims=True)
        dfw_ref[...] += dfw

    return pl.pallas_call(body, name="top_loss", grid=(t // TOK_TILE,), in_specs=[tile, tile, vec, vec, tile],
                          out_specs=(tile, _full_spec((1, LANES)), vec, vec),
                          out_shape=(jax.ShapeDtypeStruct((t, d), F32), jax.ShapeDtypeStruct((1, LANES), F32),
                                     jax.ShapeDtypeStruct((1, d), F32), jax.ShapeDtypeStruct((1, d), F32)),
                          compiler_params=_cparams(("arbitrary",)))(h1, y1, b_out1, fw, target)


def _gla_load(q_ref, k_ref, v_ref, gl_ref, up_ref, bias_ref, nw_ref):
    toks = (q_ref[...], k_ref[...], v_ref[...], gl_ref[...])
    params = (up_ref[...], bias_ref[...], nw_ref[...])
    return toks, params


def _gla_specs(c, n=None):
    toks = [_tok_spec(c, 256, _col(256, "gq", C0), n), _tok_spec(c, 256, _col(256, "gk", C0), n),
            _tok_spec(c, 512, _col(512, "gv", C0), n), _tok_spec(c, 128, _col(128, "glow", C0), n)]
    params = [_full_spec((128, 256)), _full_spec((1, 256)), _full_spec((1, 128))]
    return toks, params


def _gla_fwd(proj0, gk_up, gk_bias, norm_w):
    t = proj0.shape[0]
    c = GLA_CHUNK
    nc = t // c
    toks_s, params_s = _gla_specs(c)

    def body(q_ref, k_ref, v_ref, gl_ref, up_ref, bias_ref, nw_ref, o_ref, st_ref, s_scr):
        @pl.when(pl.program_id(0) == 0)
        def _():
            s_scr[...] = jnp.zeros_like(s_scr)

        st_ref[...] = s_scr[...]
        toks, params = _gla_load(q_ref, k_ref, v_ref, gl_ref, up_ref, bias_ref, nw_ref)
        state = [s_scr[h * GLA_DV:(h + 1) * GLA_DV, :] for h in range(GLA_HEADS)]
        o_ref[...], new = gla_chunk(state, toks, params)
        for h in range(GLA_HEADS):
            s_scr[h * GLA_DV:(h + 1) * GLA_DV, :] = new[h]

    return pl.pallas_call(
        body, name="gla_fwd", grid=(nc,), in_specs=toks_s + params_s,
        out_specs=(_tok_spec(c, 512, 0), pl.BlockSpec((512, GLA_DK), lambda i: (i, 0))),
        out_shape=(jax.ShapeDtypeStruct((t, 512), F32), jax.ShapeDtypeStruct((nc * 512, GLA_DK), F32)),
        scratch_shapes=[pltpu.VMEM((512, GLA_DK), F32)], compiler_params=_cparams(("arbitrary",)))(
            proj0, proj0, proj0, proj0, gk_up, gk_bias, norm_w)


def _gla_bwd(proj0, gk_up, gk_bias, norm_w, states, do):
    t = proj0.shape[0]
    c = GLA_CHUNK
    nc = t // c
    toks_s, params_s = _gla_specs(c, nc)

    def body(q_ref, k_ref, v_ref, gl_ref, up_ref, bias_ref, nw_ref, st_ref, do_ref,
             dq_ref, dk_ref, dv_ref, dgl_ref, dup_ref, dbias_ref, dnw_ref, ds_scr):
        @pl.when(pl.program_id(0) == 0)
        def _():
            ds_scr[...] = jnp.zeros_like(ds_scr)
            dup_ref[...] = jnp.zeros_like(dup_ref)
            dbias_ref[...] = jnp.zeros_like(dbias_ref)
            dnw_ref[...] = jnp.zeros_like(dnw_ref)

        toks, params = _gla_load(q_ref, k_ref, v_ref, gl_ref, up_ref, bias_ref, nw_ref)
        rows = lambda h: slice(h * GLA_DV, (h + 1) * GLA_DV)
        state = [st_ref[rows(h), :] for h in range(GLA_HEADS)]
        _, vjp = jax.vjp(gla_chunk, state, toks, params)
        dstate_in = [ds_scr[rows(h), :] for h in range(GLA_HEADS)]
        dstate, (dq_ref[...], dk_ref[...], dv_ref[...], dgl_ref[...]), (dup, dbias, dnw) = vjp((do_ref[...], dstate_in))
        dup_ref[...] += dup
        dbias_ref[...] += dbias
        dnw_ref[...] += dnw
        for h in range(GLA_HEADS):
            ds_scr[rows(h), :] = dstate[h]

    rev = lambda w: pl.BlockSpec((c, w), lambda i: (nc - 1 - i, 0))
    return pl.pallas_call(
        body, name="gla_bwd", grid=(nc,),
        in_specs=toks_s + params_s + [pl.BlockSpec((512, GLA_DK), lambda i: (nc - 1 - i, 0)), rev(512)],
        out_specs=(rev(256), rev(256), rev(512), rev(128), _full_spec((128, 256)), _full_spec((1, 256)), _full_spec((1, 128))),
        out_shape=(jax.ShapeDtypeStruct((t, 256), F32), jax.ShapeDtypeStruct((t, 256), F32), jax.ShapeDtypeStruct((t, 512), F32),
                   jax.ShapeDtypeStruct((t, 128), F32), jax.ShapeDtypeStruct((128, 256), F32), jax.ShapeDtypeStruct((1, 256), F32),
                   jax.ShapeDtypeStruct((1, 128), F32)),
        scratch_shapes=[pltpu.VMEM((512, GLA_DK), F32)], compiler_params=_cparams(("arbitrary",)))(
            proj0, proj0, proj0, proj0, gk_up, gk_bias, norm_w, states, do)


RWKV_PARAM_SHAPES = [(1, 512), (1, 512), (1, 512), (1, 128), (1, 128), (1, 512), (128, 512), (1, 512), (128, 512),
                     (1, 512), (1, 512), (1, 512), (1, 512), (1, 512)]
PREV_W = 1792
PREV_COLS = [slice(0, 512), slice(512, 1024), slice(1024, 1536), slice(1536, 1664), slice(1664, 1792)]


def _rwkv_load(r_ref, k_ref, v_ref, xw_ref, xa_ref, p_refs):
    toks = (r_ref[...], k_ref[...], v_ref[...], xw_ref[...], xa_ref[...])
    return toks, tuple(p[...] for p in p_refs)


def _rwkv_state(s_ref, prev_ref):
    n = RWKV_N
    S = [s_ref[h * n:(h + 1) * n, :] for h in range(RWKV_HEADS)]
    return (S,) + tuple(prev_ref[0:1, cols] for cols in PREV_COLS)


def _rwkv_put_state(s_ref, prev_ref, state):
    n = RWKV_N
    for h in range(RWKV_HEADS):
        s_ref[h * n:(h + 1) * n, :] = state[0][h]
    for cols, val in zip(PREV_COLS, state[1:]):
        prev_ref[0:1, cols] = val


def _rwkv_specs(c, n=None):
    toks = [_tok_spec(c, 512, _col(512, "r", C0), n), _tok_spec(c, 512, _col(512, "k", C0), n),
            _tok_spec(c, 512, _col(512, "v", C0), n), _tok_spec(c, 128, _col(128, "xw", C0), n),
            _tok_spec(c, 128, _col(128, "xa", C0), n)]
    return toks, [_full_spec(s) for s in RWKV_PARAM_SHAPES]


def _rwkv_fwd(proj0, params):
    t = proj0.shape[0]
    c = RWKV_CHUNK
    nc = t // c
    toks_s, params_s = _rwkv_specs(c)
    npar = len(params)

    def body(*refs):
        tok_refs, p_refs = refs[:5], refs[5:5 + npar]
        o_ref, st_ref, pst_ref, s_scr, prev_scr = refs[5 + npar:]

        @pl.when(pl.program_id(0) == 0)
        def _():
            s_scr[...] = jnp.zeros_like(s_scr)
            prev_scr[...] = jnp.zeros_like(prev_scr)

        st_ref[...] = s_scr[...]
        pst_ref[...] = prev_scr[...]
        toks, prm = _rwkv_load(*tok_refs, p_refs)
        o_ref[...], new = rwkv_chunk(_rwkv_state(s_scr, prev_scr), toks, prm)
        _rwkv_put_state(s_scr, prev_scr, new)

    return pl.pallas_call(
        body, name="rwkv_fwd", grid=(nc,), in_specs=toks_s + params_s,
        out_specs=(_tok_spec(c, 512, 0), pl.BlockSpec((512, RWKV_N), lambda i: (i, 0)), pl.BlockSpec((8, PREV_W), lambda i: (i, 0))),
        out_shape=(jax.ShapeDtypeStruct((t, 512), F32), jax.ShapeDtypeStruct((nc * 512, RWKV_N), F32),
                   jax.ShapeDtypeStruct((nc * 8, PREV_W), F32)),
        scratch_shapes=[pltpu.VMEM((512, RWKV_N), F32), pltpu.VMEM((8, PREV_W), F32)],
        compiler_params=_cparams(("arbitrary",)))(proj0, proj0, proj0, proj0, proj0, *params)


def _rwkv_bwd(proj0, params, states, prevs, do):
    t = proj0.shape[0]
    c = RWKV_CHUNK
    nc = t // c
    toks_s, params_s = _rwkv_specs(c, nc)
    npar = len(params)

    def body(*refs):
        tok_refs, p_refs = refs[:5], refs[5:5 + npar]
        st_ref, pst_ref, do_ref = refs[5 + npar:8 + npar]
        dtok_refs = refs[8 + npar:13 + npar]
        dp_refs = refs[13 + npar:13 + 2 * npar]
        ds_scr, dprev_scr = refs[13 + 2 * npar:]

        @pl.when(pl.program_id(0) == 0)
        def _():
            ds_scr[...] = jnp.zeros_like(ds_scr)
            dprev_scr[...] = jnp.zeros_like(dprev_scr)
            for dp in dp_refs:
                dp[...] = jnp.zeros_like(dp)

        toks, prm = _rwkv_load(*tok_refs, p_refs)
        _, vjp = jax.vjp(rwkv_chunk, _rwkv_state(st_ref, pst_ref), toks, prm)
        dstate, dtoks, dprm = vjp((do_ref[...], _rwkv_state(ds_scr, dprev_scr)))
        for ref, val in zip(dtok_refs, dtoks):
            ref[...] = val
        for ref, val in zip(dp_refs, dprm):
            ref[...] += val
        _rwkv_put_state(ds_scr, dprev_scr, dstate)

    rev = lambda w: pl.BlockSpec((c, w), lambda i: (nc - 1 - i, 0))
    return pl.pallas_call(
        body, name="rwkv_bwd", grid=(nc,),
        in_specs=toks_s + params_s + [pl.BlockSpec((512, RWKV_N), lambda i: (nc - 1 - i, 0)),
                                      pl.BlockSpec((8, PREV_W), lambda i: (nc - 1 - i, 0)), rev(512)],
        out_specs=tuple([rev(512), rev(512), rev(512), rev(128), rev(128)] + params_s),
        out_shape=tuple([jax.ShapeDtypeStruct((t, w), F32) for w in (512, 512, 512, 128, 128)]
                        + [jax.ShapeDtypeStruct(s, F32) for s in RWKV_PARAM_SHAPES]),
        scratch_shapes=[pltpu.VMEM((512, RWKV_N), F32), pltpu.VMEM((8, PREV_W), F32)],
        compiler_params=_cparams(("arbitrary",)))(proj0, proj0, proj0, proj0, proj0, *params, states, prevs, do)


def _swa_load(q_ref, k_ref, v_ref, cos_ref, sin_ref, bq_ref, bk_ref, bv_ref, sk_ref):
    toks = (_heads(q_ref, 16, SWA_HD), _heads(k_ref, 4, SWA_HD), _heads(v_ref, 4, SWA_HD), cos_ref[...], sin_ref[...])
    params = (_heads(bq_ref, 16, SWA_HD), _heads(bk_ref, 4, SWA_HD), _heads(bv_ref, 4, SWA_HD), _heads(sk_ref, 16, 1))
    return toks, params


def _swa_specs(c, n=None):
    toks = [_tok_spec(c, 1024, _col(1024, "q", C1), n), _tok_spec(c, 256, _col(256, "k", C1), n),
            _tok_spec(c, 256, _col(256, "v", C1), n), _tok_spec(c, SWA_HD, 0, n), _tok_spec(c, SWA_HD, 0, n)]
    params = [_full_spec((1, 1024)), _full_spec((1, 256)), _full_spec((1, 256)), _full_spec((1, 16))]
    return toks, params


def _swa_fwd(proj1, cos, sin, bq, bk, bv, sinks):
    t = proj1.shape[0]
    c = WINDOW
    nb = t // c
    toks_s, params_s = _swa_specs(c)

    def body(q_ref, k_ref, v_ref, cos_ref, sin_ref, bq_ref, bk_ref, bv_ref, sk_ref, o_ref, kst_ref, vst_ref, k_scr, v_scr):
        first = pl.program_id(0) == 0

        @pl.when(first)
        def _():
            k_scr[...] = jnp.zeros_like(k_scr)
            v_scr[...] = jnp.zeros_like(v_scr)

        kst_ref[...] = k_scr[...]
        vst_ref[...] = v_scr[...]
        toks, params = _swa_load(q_ref, k_ref, v_ref, cos_ref, sin_ref, bq_ref, bk_ref, bv_ref, sk_ref)
        outs, (kn, vn) = swa_chunk((_heads(k_scr, 4, SWA_HD), _heads(v_scr, 4, SWA_HD)), toks, params, first)
        _put_heads(o_ref, outs, SWA_HD)
        _put_heads(k_scr, kn, SWA_HD)
        _put_heads(v_scr, vn, SWA_HD)

    return pl.pallas_call(
        body, name="swa_fwd", grid=(nb,), in_specs=toks_s + params_s,
        out_specs=(_tok_spec(c, 1024, 0), _tok_spec(c, 256, 0), _tok_spec(c, 256, 0)),
        out_shape=(jax.ShapeDtypeStruct((t, 1024), F32), jax.ShapeDtypeStruct((t, 256), F32), jax.ShapeDtypeStruct((t, 256), F32)),
        scratch_shapes=[pltpu.VMEM((c, 256), F32), pltpu.VMEM((c, 256), F32)],
        compiler_params=_cparams(("arbitrary",)))(proj1, proj1, proj1, cos, sin, bq, bk, bv, sinks)


def _swa_bwd(proj1, cos, sin, bq, bk, bv, sinks, kst, vst, do):
    t = proj1.shape[0]
    c = WINDOW
    nb = t // c
    toks_s, params_s = _swa_specs(c, nb)

    def body(q_ref, k_ref, v_ref, cos_ref, sin_ref, bq_ref, bk_ref, bv_ref, sk_ref, kst_ref, vst_ref, do_ref,
             dq_ref, dk_ref, dv_ref, dbq_ref, dbk_ref, dbv_ref, dsk_ref, dk_scr, dv_scr):
        i = pl.program_id(0)

        @pl.when(i == 0)
        def _():
            dk_scr[...] = jnp.zeros_like(dk_scr)
            dv_scr[...] = jnp.zeros_like(dv_scr)
            for ref in (dbq_ref, dbk_ref, dbv_ref, dsk_ref):
                ref[...] = jnp.zeros_like(ref)

        first = i == nb - 1
        toks, params = _swa_load(q_ref, k_ref, v_ref, cos_ref, sin_ref, bq_ref, bk_ref, bv_ref, sk_ref)
        f = functools.partial(swa_chunk, first=first)
        _, vjp = jax.vjp(f, (_heads(kst_ref, 4, SWA_HD), _heads(vst_ref, 4, SWA_HD)), toks, params)
        dstate_in = (_heads(dk_scr, 4, SWA_HD), _heads(dv_scr, 4, SWA_HD))
        (dkp, dvp), (dq, dk, dv, _, _), (dbq, dbk, dbv, dsk) = vjp((_heads(do_ref, 16, SWA_HD), dstate_in))
        _put_heads(dq_ref, dq, SWA_HD)
        _put_heads(dk_ref, dk, SWA_HD)
        _put_heads(dv_ref, dv, SWA_HD)
        _put_heads(dbq_ref, dbq, SWA_HD, add=True)
        _put_heads(dbk_ref, dbk, SWA_HD, add=True)
        _put_heads(dbv_ref, dbv, SWA_HD, add=True)
        _put_heads(dsk_ref, dsk, 1, add=True)
        _put_heads(dk_scr, dkp, SWA_HD)
        _put_heads(dv_scr, dvp, SWA_HD)

    rev = lambda w: pl.BlockSpec((c, w), lambda i: (nb - 1 - i, 0))
    return pl.pallas_call(
        body, name="swa_bwd", grid=(nb,), in_specs=toks_s + params_s + [rev(256), rev(256), rev(1024)],
        out_specs=(rev(1024), rev(256), rev(256), _full_spec((1, 1024)), _full_spec((1, 256)), _full_spec((1, 256)), _full_spec((1, 16))),
        out_shape=(jax.ShapeDtypeStruct((t, 1024), F32), jax.ShapeDtypeStruct((t, 256), F32), jax.ShapeDtypeStruct((t, 256), F32),
                   jax.ShapeDtypeStruct((1, 1024), F32), jax.ShapeDtypeStruct((1, 256), F32), jax.ShapeDtypeStruct((1, 256), F32),
                   jax.ShapeDtypeStruct((1, 16), F32)),
        scratch_shapes=[pltpu.VMEM((c, 256), F32), pltpu.VMEM((c, 256), F32)],
        compiler_params=_cparams(("arbitrary",)))(proj1, proj1, proj1, cos, sin, bq, bk, bv, sinks, kst, vst, do)


MESH = pl.DeviceIdType.MESH
ANY = pl.BlockSpec(memory_space=pl.ANY)


def _my_place():
    return lax.axis_index("x"), lax.axis_index("y"), lax.axis_index("c")


def _all_gather(shards):
    n = len(shards)

    def body(*refs):
        in_refs, out_refs = refs[:n], refs[n:2 * n]
        send_sems, recv_sems, local_sems = refs[2 * n:]
        x, y, c = _my_place()
        me, sibling = (x, y, c), (x, y, 1 - c)
        chips = [(1 - x, y), (x, 1 - y), (1 - x, 1 - y)]

        def slot(out_ref, place):
            px, py, pc = place
            return out_ref.at[4 * px + 2 * py + pc]

        def copy(a, k, block, to, src=None):
            return pltpu.make_async_remote_copy(
                src_ref=slot(out_refs[a], block) if src is None else src, dst_ref=slot(out_refs[a], block),
                send_sem=send_sems.at[a, k], recv_sem=recv_sems.at[a, k], device_id=to, device_id_type=MESH)

        mine = [pltpu.make_async_copy(in_refs[a], slot(out_refs[a], me), local_sems.at[a]) for a in range(n)]
        for cp in mine:
            cp.start()
        first = []
        for a in range(n):
            first.append(copy(a, 0, me, sibling, src=in_refs[a]))
            first += [copy(a, 1 + j, me, (*chip, c), src=in_refs[a]) for j, chip in enumerate(chips)]
        for cp in first:
            cp.start()
        passed = []
        for j, chip in enumerate(chips):
            for a in range(n):
                copy(a, 1 + j, (*chip, c), me).wait_recv()
                fwd = copy(a, 4 + j, (*chip, c), sibling)
                fwd.start()
                passed.append(fwd)
        for a in range(n):
            copy(a, 0, sibling, me).wait_recv()
            for j, chip in enumerate(chips):
                copy(a, 4 + j, (*chip, 1 - c), me).wait_recv()
        for cp in first + passed:
            cp.wait_send()
        for cp in mine:
            cp.wait()

    return pl.pallas_call(
        body, name="all_gather_weights", in_specs=[ANY] * n, out_specs=[ANY] * n,
        out_shape=[jax.ShapeDtypeStruct((N_DEV,) + s.shape, s.dtype) for s in shards],
        scratch_shapes=[pltpu.SemaphoreType.DMA((n, 7)), pltpu.SemaphoreType.DMA((n, 7)), pltpu.SemaphoreType.DMA((n,))],
        compiler_params=pltpu.CompilerParams(has_side_effects=True))(*shards)


def _exchange(parts, rep):
    n = len(parts)

    def body(*refs):
        in_refs, rep_ref = refs[:n], refs[n]
        out_refs, rep_out = refs[n + 1:2 * n + 1], refs[2 * n + 1]
        send_sems, recv_sems, local_sems = refs[2 * n + 2:]
        x, y, c = _my_place()
        my_idx = 4 * x + 2 * y + c
        local = [pltpu.make_async_copy(in_refs[a].at[my_idx], out_refs[a].at[my_idx], local_sems.at[a]) for a in range(n)]
        local.append(pltpu.make_async_copy(rep_ref, rep_out.at[my_idx], local_sems.at[n]))
        for cp in local:
            cp.start()
        copies = []
        for rel in range(1, N_DEV):
            dx, dy, dc = (rel >> 2) & 1, (rel >> 1) & 1, rel & 1
            px, py, pc = x ^ dx, y ^ dy, c ^ dc
            peer_idx = 4 * px + 2 * py + pc
            for a in range(n):
                copies.append(pltpu.make_async_remote_copy(
                    src_ref=in_refs[a].at[peer_idx], dst_ref=out_refs[a].at[my_idx], send_sem=send_sems.at[a, rel - 1],
                    recv_sem=recv_sems.at[a, rel - 1], device_id=(px, py, pc), device_id_type=MESH))
            copies.append(pltpu.make_async_remote_copy(
                src_ref=rep_ref, dst_ref=rep_out.at[my_idx], send_sem=send_sems.at[n, rel - 1],
                recv_sem=recv_sems.at[n, rel - 1], device_id=(px, py, pc), device_id_type=MESH))
        for cp in copies:
            cp.start()
        for cp in copies:
            cp.wait_recv()
        for cp in copies:
            cp.wait_send()
        for cp in local:
            cp.wait()

    outs = pl.pallas_call(
        body, name="exchange_grads", in_specs=[ANY] * (n + 1), out_specs=[ANY] * (n + 1),
        out_shape=[jax.ShapeDtypeStruct(p.shape, p.dtype) for p in parts] + [jax.ShapeDtypeStruct((N_DEV,) + rep.shape, rep.dtype)],
        scratch_shapes=[pltpu.SemaphoreType.DMA((n + 1, 7)), pltpu.SemaphoreType.DMA((n + 1, 7)), pltpu.SemaphoreType.DMA((n + 1,))],
        compiler_params=pltpu.CompilerParams(has_side_effects=True))(*parts, rep)
    return outs[:n], outs[n]


def _adam_math(w, g, m, v):
    m = ADAM_B1 * m + (1.0 - ADAM_B1) * g
    v = ADAM_B2 * v + (1.0 - ADAM_B2) * (g * g)
    m_hat = m / (1.0 - ADAM_B1 ** ADAM_STEP)
    v_hat = v / (1.0 - ADAM_B2 ** ADAM_STEP)
    delta = -ADAM_LR * (m_hat / (jnp.sqrt(v_hat) + ADAM_EPS) + ADAM_WD * w)
    return delta, m, v


def _adamw(name, w, gslots, m, v, tr):
    r, cc = w.shape
    assert r % tr == 0
    tile = pl.BlockSpec((tr, cc), lambda i: (i, 0))

    def body(w_ref, g_ref, m_ref, v_ref, go_ref, d_ref, mo_ref, vo_ref):
        g = g_ref[0].astype(F32)
        for s in range(1, N_DEV):
            g = g + g_ref[s].astype(F32)
        d, mn, vn = _adam_math(w_ref[...], g, m_ref[...], v_ref[...])
        go_ref[...] = g
        d_ref[...] = d
        mo_ref[...] = mn
        vo_ref[...] = vn

    shp = jax.ShapeDtypeStruct((r, cc), F32)
    return pl.pallas_call(body, name=name, grid=(r // tr,),
                          in_specs=[tile, pl.BlockSpec((N_DEV, tr, cc), lambda i: (0, i, 0)), tile, tile],
                          out_specs=(tile,) * 4, out_shape=(shp,) * 4, compiler_params=_cparams(("arbitrary",)))(w, gslots, m, v)


PACK_TILE = 8 * LANES


def _size(shape):
    n = 1
    for d in shape:
        n *= d
    return n


def _pack(arrays):
    rows = []
    for a in arrays:
        flat = a.reshape(-1).astype(F32)
        rows.append(jnp.pad(flat, (0, (-flat.shape[0]) % PACK_TILE)).reshape(-1, LANES))
    return jnp.concatenate(rows, axis=0)


def _unpack(packed, shapes):
    outs, r = [], 0
    for s in shapes:
        n = _size(s)
        nr = -(-n // PACK_TILE) * 8
        outs.append(packed[r:r + nr].reshape(-1)[:n].reshape(s))
        r += nr
    return outs


def _pack_dev(arrays):
    rows = []
    for a in arrays:
        flat = a.reshape(N_DEV, -1).astype(F32)
        rows.append(jnp.pad(flat, ((0, 0), (0, (-flat.shape[1]) % PACK_TILE))).reshape(N_DEV, -1, LANES))
    return jnp.concatenate(rows, axis=1)


def _unpack_dev(packed, shapes):
    outs, r = [], 0
    for s in shapes:
        n = _size(s)
        nr = -(-n // PACK_TILE) * 8
        outs.append(packed[:, r:r + nr].reshape(N_DEV, -1)[:, :n].reshape((N_DEV,) + tuple(s)))
        r += nr
    return outs


def _rope_tables(t):
    half = 8
    inv_freq = ROPE_THETA ** (-jnp.arange(half, dtype=F32) / half)
    ang = jnp.arange(t, dtype=F32)[:, None] * inv_freq
    cos = jnp.concatenate([jnp.cos(ang), jnp.cos(ang), jnp.ones((t, SWA_HD - 16), F32)], axis=1)
    sin = jnp.concatenate([jnp.sin(ang), jnp.sin(ang), jnp.zeros((t, SWA_HD - 16), F32)], axis=1)
    return cos, sin


def _pad_to(a, rows=None, cols=None):
    r = 0 if rows is None else rows - a.shape[0]
    c = 0 if cols is None else cols - a.shape[1]
    return jnp.pad(a, ((0, r), (0, c)))


ORIG0 = dict(gq=(0, 256), gk=(256, 256), gv=(512, 512), glow=(1024, 16), r=(1040, 512), k=(1552, 512), v=(2064, 512),
             xw=(2576, 64), xa=(2640, 64), gate=(2704, 1024))
ORIG0_ORDER = ["gq", "gk", "gv", "glow", "r", "k", "v", "xw", "xa", "gate"]


def _w0_to_padded(w):
    cols, at = [], 0
    for name, (off, width) in sorted(C0.items(), key=lambda kv: kv[1][0]):
        assert off == at
        src, src_w = ORIG0[name]
        cols.append(_pad_to(w[:, src:src + src_w], cols=width))
        at += width
    cols.append(jnp.zeros((w.shape[0], N0P - at), w.dtype))
    return jnp.concatenate(cols, axis=1)


def _w0_from_padded(wp):
    return jnp.concatenate([wp[:, C0[n][0]:C0[n][0] + ORIG0[n][1]] for n in ORIG0_ORDER], axis=1)


def _w1_to_mine(w):
    return jnp.concatenate([w[:, 1536:2560], w[:, :1536]], axis=1)


def _w1_from_mine(w):
    return jnp.concatenate([w[:, 1024:2560], w[:, :1024]], axis=1)


def kernel(x, norm_w, w_in0, gla_gk_up, gla_gk_bias, gla_norm_w, rwkv_mu, rwkv_w0, rwkv_w_up, rwkv_a0, rwkv_a_up, rwkv_k_k, rwkv_k_a, rwkv_r_k, rwkv_ln_w, rwkv_ln_b, w_out0, w_in1, b_in1, attn_sinks, w_out1, b_out1, final_norm_w, loss_target, m_norm_w, m_w_in0, m_gla_gk_up, m_gla_gk_bias, m_gla_norm_w, m_rwkv_mu, m_rwkv_w0, m_rwkv_w_up, m_rwkv_a0, m_rwkv_a_up, m_rwkv_k_k, m_rwkv_k_a, m_rwkv_r_k, m_rwkv_ln_w, m_rwkv_ln_b, m_w_out0, m_w_in1, m_b_in1, m_attn_sinks, m_w_out1, m_b_out1, m_final_norm_w, v_norm_w, v_w_in0, v_gla_gk_up, v_gla_gk_bias, v_gla_norm_w, v_rwkv_mu, v_rwkv_w0, v_rwkv_w_up, v_rwkv_a0, v_rwkv_a_up, v_rwkv_k_k, v_rwkv_k_a, v_rwkv_r_k, v_rwkv_ln_w, v_rwkv_ln_b, v_w_out0, v_w_in1, v_b_in1, v_attn_sinks, v_w_out1, v_b_out1, v_final_norm_w):
    weights = dict(norm_w=norm_w, w_in0=w_in0, gla_gk_up=gla_gk_up, gla_gk_bias=gla_gk_bias, gla_norm_w=gla_norm_w, rwkv_mu=rwkv_mu,
                   rwkv_w0=rwkv_w0, rwkv_w_up=rwkv_w_up, rwkv_a0=rwkv_a0, rwkv_a_up=rwkv_a_up, rwkv_k_k=rwkv_k_k, rwkv_k_a=rwkv_k_a,
                   rwkv_r_k=rwkv_r_k, rwkv_ln_w=rwkv_ln_w, rwkv_ln_b=rwkv_ln_b, w_out0=w_out0, w_in1=w_in1, b_in1=b_in1,
                   attn_sinks=attn_sinks, w_out1=w_out1, b_out1=b_out1, final_norm_w=final_norm_w)
    moms = dict(norm_w=m_norm_w, w_in0=m_w_in0, gla_gk_up=m_gla_gk_up, gla_gk_bias=m_gla_gk_bias, gla_norm_w=m_gla_norm_w,
                rwkv_mu=m_rwkv_mu, rwkv_w0=m_rwkv_w0, rwkv_w_up=m_rwkv_w_up, rwkv_a0=m_rwkv_a0, rwkv_a_up=m_rwkv_a_up,
                rwkv_k_k=m_rwkv_k_k, rwkv_k_a=m_rwkv_k_a, rwkv_r_k=m_rwkv_r_k, rwkv_ln_w=m_rwkv_ln_w, rwkv_ln_b=m_rwkv_ln_b,
                w_out0=m_w_out0, w_in1=m_w_in1, b_in1=m_b_in1, attn_sinks=m_attn_sinks, w_out1=m_w_out1, b_out1=m_b_out1,
                final_norm_w=m_final_norm_w)
    vars_ = dict(norm_w=v_norm_w, w_in0=v_w_in0, gla_gk_up=v_gla_gk_up, gla_gk_bias=v_gla_gk_bias, gla_norm_w=v_gla_norm_w,
                 rwkv_mu=v_rwkv_mu, rwkv_w0=v_rwkv_w0, rwkv_w_up=v_rwkv_w_up, rwkv_a0=v_rwkv_a0, rwkv_a_up=v_rwkv_a_up,
                 rwkv_k_k=v_rwkv_k_k, rwkv_k_a=v_rwkv_k_a, rwkv_r_k=v_rwkv_r_k, rwkv_ln_w=v_rwkv_ln_w, rwkv_ln_b=v_rwkv_ln_b,
                 w_out0=v_w_out0, w_in1=v_w_in1, b_in1=v_b_in1, attn_sinks=v_attn_sinks, w_out1=v_w_out1, b_out1=v_b_out1,
                 final_norm_w=v_final_norm_w)
    names = list(weights)
    big = ["w_in0", "w_out0", "w_in1", "w_out1"]
    small_sharded = ["gla_gk_up", "rwkv_w_up", "rwkv_a_up", "b_in1", "b_out1"]
    replicated = [n for n in names if n not in big and n not in small_sharded]

    xs = x[0]
    tgt = loss_target[0]
    t = xs.shape[0]

    small_shard_pack = _pack([weights[n] for n in small_sharded])
    g_in0, g_out0, g_in1, g_out1, g_small = _all_gather(
        [w_in0[0].astype(BF16), w_out0[0].astype(BF16), w_in1[0].astype(BF16), w_out1[0].astype(BF16), small_shard_pack])
    w0p = _w0_to_padded(jnp.transpose(g_in0, (1, 0, 2)).reshape(D_MODEL, -1))
    wo0 = g_out0.reshape(1024, D_MODEL)
    w1p = _w1_to_mine(jnp.transpose(g_in1, (1, 0, 2)).reshape(D_MODEL, -1))
    wo1 = g_out1.reshape(1024, D_MODEL)
    small_shapes = [weights[n].shape for n in small_sharded]
    gs = _unpack_dev(g_small, small_shapes)
    join_cols = lambda a: jnp.transpose(a[:, 0], (1, 0, 2)).reshape(a.shape[2], -1)
    gk_up, w_up, a_up = join_cols(gs[0]), join_cols(gs[1]), join_cols(gs[2])
    b_in, b_out = gs[3].reshape(1, -1), gs[4].reshape(1, -1)

    gk_up_p = _pad_to(gk_up, rows=128)
    mu = rwkv_mu
    rwkv_params = [mu[:, 0:512], mu[:, 512:1024], mu[:, 1024:1536], _pad_to(mu[:, 1536:1600], cols=128), _pad_to(mu[:, 1600:1664], cols=128),
                   rwkv_w0, _pad_to(w_up, rows=128), rwkv_a0, _pad_to(a_up, rows=128), rwkv_k_k, rwkv_k_a, rwkv_r_k.reshape(1, 512),
                   rwkv_ln_w, rwkv_ln_b]
    bq, bk, bv = b_in[:, :1024], b_in[:, 1024:1280], b_in[:, 1280:1536]
    cos, sin = _rope_tables(t)
    nw0, nw1, fw = norm_w[0:1], norm_w[1:2], final_norm_w.reshape(1, D_MODEL)

    hn0, hn0_t = _norm_fwd("norm0_fwd", xs, nw0)
    proj0 = _matmul("proj0", hn0, w0p, "nn", 512, 1024)
    o_a, gla_states = _gla_fwd(proj0, gk_up_p, gla_gk_bias, gla_norm_w)
    o_b, rwkv_states, rwkv_prevs = _rwkv_fwd(proj0, rwkv_params)
    og0, og0_t = _gate_fwd("gate0_fwd", [o_a, o_b], proj0)
    y0 = _matmul("out0", og0, wo0, "nn", 512, 1024)
    h1, hn1, hn1_t = _norm_fwd("norm1_fwd", xs, nw1, y0)
    proj1 = _matmul("proj1", hn1, w1p, "nn", 512, 1280)
    o_c, kst, vst = _swa_fwd(proj1, cos, sin, bq, bk, bv, attn_sinks)
    og1, og1_t = _gate_fwd("gate1_fwd", [o_c], proj1)
    y1 = _matmul("out1", og1, wo1, "nn", 512, 1024)
    dh2, loss_part, d_b_out, d_fw = _top(h1, y1, b_out, fw, tgt)

    dog1 = _matmul("out1_dx", dh2, wo1, "nt", 512, 1024)
    d_wo1 = _matmul("out1_dw", og1_t, dh2, "nn", 1024, 512)
    d_oc, d_gate1 = _gate_bwd("gate1_bwd", [o_c], proj1, dog1)
    dq, dk, dv, d_bq, d_bk, d_bv, d_sinks = _swa_bwd(proj1, cos, sin, bq, bk, bv, attn_sinks, kst, vst, d_oc)
    dproj1 = jnp.concatenate([d_gate1, dq, dk, dv], axis=1).astype(BF16)
    dhn1 = _matmul("proj1_dx", dproj1, w1p, "nt", 512, 1024)
    d_w1p = _matmul("proj1_dw", hn1_t, dproj1, "nn", 1024, 512)
    dh1, d_nw1 = _norm_bwd("norm1_bwd", h1, nw1, dhn1, dh2)
    dog0 = _matmul("out0_dx", dh1, wo0, "nt", 512, 1024)
    d_wo0 = _matmul("out0_dw", og0_t, dh1, "nn", 1024, 512)
    d_oa, d_ob, d_gate0 = _gate_bwd("gate0_bwd", [o_a, o_b], proj0, dog0)
    dgq, dgk, dgv, dglow, d_gk_up, d_gk_bias, d_gla_nw = _gla_bwd(proj0, gk_up_p, gla_gk_bias, gla_norm_w, gla_states, d_oa)
    rb = _rwkv_bwd(proj0, rwkv_params, rwkv_states, rwkv_prevs, d_ob)
    dr, dkk, dvv, dxw, dxa = rb[:5]
    d_rp = rb[5:]
    dproj0 = jnp.concatenate([d_gate0, dgv, dr, dkk, dvv, dgq, dgk, dglow, dxw, dxa, jnp.zeros((t, 128), F32)], axis=1).astype(BF16)
    dhn0 = _matmul("proj0_dx", dproj0, w0p, "nt", 512, 1024)
    d_w0p = _matmul("proj0_dw", hn0_t, dproj0, "nn", 1024, 512)
    grad_x, d_nw0 = _norm_bwd("norm0_bwd", xs, nw0, dhn0, dh1)

    contrib = dict(
        norm_w=jnp.concatenate([d_nw0, d_nw1], axis=0), gla_gk_bias=d_gk_bias, gla_norm_w=d_gla_nw,
        rwkv_mu=jnp.concatenate([d_rp[0], d_rp[1], d_rp[2], d_rp[3][:, :64], d_rp[4][:, :64]], axis=1),
        rwkv_w0=d_rp[5], rwkv_a0=d_rp[7], rwkv_k_k=d_rp[9], rwkv_k_a=d_rp[10], rwkv_r_k=d_rp[11].reshape(1, 8, 64),
        rwkv_ln_w=d_rp[12], rwkv_ln_b=d_rp[13], attn_sinks=d_sinks, final_norm_w=d_fw.reshape(D_MODEL))
    rep_pack = _pack([contrib[n] for n in replicated] + [loss_part[:, :1]])

    d_w0 = _w0_from_padded(d_w0p)
    d_w1 = _w1_from_mine(d_w1p)
    d_b_in = jnp.concatenate([d_bq, d_bk, d_bv], axis=1)
    full_small = [d_gk_up[:16], d_rp[6][:64], d_rp[8][:64], d_b_in, d_b_out]
    split_cols = lambda a: jnp.transpose(a.reshape(a.shape[0], N_DEV, -1), (1, 0, 2))
    small_parts = [split_cols(a) for a in full_small]
    small_pack = _pack_dev(small_parts)
    parts = [split_cols(d_w0).astype(BF16), d_wo0.reshape(N_DEV, 128, D_MODEL).astype(BF16), split_cols(d_w1).astype(BF16),
             d_wo1.reshape(N_DEV, 128, D_MODEL).astype(BF16), small_pack]
    (r_in0, r_out0, r_in1, r_out1, r_small), r_rep = _exchange(parts, rep_pack)

    res = {}
    res["w_in0"] = _adamw("adamw_w_in0", w_in0[0], r_in0, m_w_in0[0], v_w_in0[0], 256)
    res["w_out0"] = _adamw("adamw_w_out0", w_out0[0], r_out0, m_w_out0[0], v_w_out0[0], 128)
    res["w_in1"] = _adamw("adamw_w_in1", w_in1[0], r_in1, m_w_in1[0], v_w_in1[0], 256)
    res["w_out1"] = _adamw("adamw_w_out1", w_out1[0], r_out1, m_w_out1[0], v_w_out1[0], 128)
    for n in big:
        res[n] = tuple(a[None] for a in res[n])
    small_names = small_sharded + replicated
    slots = jnp.concatenate([r_small, r_rep], axis=1)
    n_shard_rows = r_small.shape[1]
    pk = lambda d: jnp.concatenate([_pack([d[n] for n in small_sharded]), _pack([d[n] for n in replicated] + [jnp.zeros((1, 1), F32)])], axis=0)
    g_p, d_p, m_p, v_p = _adamw("adamw_small", pk(weights), slots, pk(moms), pk(vars_), slots.shape[1])
    sh_shapes = [weights[n].shape for n in small_sharded]
    rep_shapes = [weights[n].shape for n in replicated] + [(1, 1)]
    for i, packed in enumerate((g_p, d_p, m_p, v_p)):
        vals = _unpack(packed[:n_shard_rows], sh_shapes) + _unpack(packed[n_shard_rows:], rep_shapes)
        for n, val in zip(small_names, vals):
            res.setdefault(n, [None] * 4)[i] = val
        if i == 0:
            loss = vals[-1].reshape(())
    return (loss, grad_x[None], *[res[n][0] for n in names], *[res[n][1] for n in names],
            *[res[n][2] for n in names], *[res[n][3] for n in names])
```

```python
import functools

import jax
import jax.numpy as jnp
from jax import lax
from jax.experimental import pallas as pl
from jax.experimental.pallas import tpu as pltpu

F32 = jnp.float32
BF16 = jnp.bfloat16
HI = lax.Precision.HIGHEST

D_MODEL = 1024
NORM_EPS = 1e-5
GLA_HEADS, GLA_DK, GLA_DV = 4, 64, 128
GLA_NORMALIZER = 16.0
GLA_CHUNK = 64
RWKV_HEADS, RWKV_N = 8, 64
RWKV_LN_EPS = 64e-5
RWKV_CHUNK = 128
SWA_Q_HEADS, SWA_KV_HEADS, SWA_GROUP, SWA_HD = 16, 4, 4, 64
WINDOW = 128
ROPE_THETA = 500000.0
NEG = -1e30
N_DEV = 8
LANES = 128

ADAM_LR, ADAM_B1, ADAM_B2, ADAM_EPS, ADAM_WD, ADAM_STEP = 0.001, 0.9, 0.999, 1e-08, 0.01, 10

N0P = 4096
C0 = dict(gate=(0, 1024), gv=(1024, 512), r=(1536, 512), k=(2048, 512), v=(2560, 512), gq=(3072, 256), gk=(3328, 256),
          glow=(3584, 128), xw=(3712, 128), xa=(3840, 128))
N1P = 2560
C1 = dict(gate=(0, 1024), q=(1024, 1024), k=(2048, 256), v=(2304, 256))

VMEM_LIMIT = 56 * 1024 * 1024

P_LORA = 1
P_GLA = 1
P_RWKV_G = 3
P_RWKV = 1
P_SWA = 1
P_ROPE = 3


def _cparams(sem=None):
    return pltpu.CompilerParams(dimension_semantics=sem, vmem_limit_bytes=VMEM_LIMIT)


DIMS = dict(nn=(((1,), (0,)), ((), ())), nt=(((1,), (1,)), ((), ())), tn=(((0,), (0,)), ((), ())))


def _split_bf16(a):
    hi = a.astype(BF16)
    return hi, (a - hi.astype(F32)).astype(BF16)


def _dot(a, b, mode, passes):
    dg = lambda p, q: lax.dot_general(p, q, DIMS[mode], preferred_element_type=F32)
    if passes == 1:
        return dg(a.astype(BF16), b.astype(BF16))
    if passes == 3:
        (ah, al), (bh, bl) = _split_bf16(a), _split_bf16(b)
        return dg(ah, bh) + dg(al, bh) + dg(ah, bl)
    return lax.dot_general(a, b, DIMS[mode], precision=HI, preferred_element_type=F32)


@functools.partial(jax.custom_vjp, nondiff_argnums=(2, 3))
def mmx(a, b, mode, passes):
    return _dot(a, b, mode, passes)


def _mmx_fwd(a, b, mode, passes):
    return _dot(a, b, mode, passes), (a, b)


def _mmx_bwd(mode, passes, res, g):
    a, b = res
    if mode == "nn":
        return _dot(g, b, "nt", passes), _dot(a, g, "tn", passes)
    if mode == "nt":
        return _dot(g, b, "nn", passes), _dot(g, a, "tn", passes)
    return _dot(b, g, "nt", passes), _dot(a, g, "nn", passes)


mmx.defvjp(_mmx_fwd, _mmx_bwd)


def _tri_dot(tri, x):
    t = tri.astype(BF16)
    x1 = x.astype(BF16)
    r1 = x - x1.astype(F32)
    x2 = r1.astype(BF16)
    x3 = (r1 - x2.astype(F32)).astype(BF16)
    dg = lambda q: jnp.dot(t, q, preferred_element_type=F32)
    return dg(x1) + dg(x2) + dg(x3)


@jax.custom_vjp
def cumsum_rows(x):
    return _tri_dot(tril_ones(x.shape[0]), x)


def _cumsum_fwd(x):
    return cumsum_rows(x), None


def _cumsum_bwd(_, g):
    i, j = _iota2(g.shape[0], g.shape[0])
    return (_tri_dot(jnp.where(i <= j, 1.0, 0.0).astype(F32), g),)


cumsum_rows.defvjp(_cumsum_fwd, _cumsum_bwd)


def _head_dot(x):
    i, j = _iota2(LANES, LANES)
    shift = RWKV_N.bit_length() - 1
    same = jnp.where(jnp.right_shift(i, shift) == jnp.right_shift(j, shift), 1.0, 0.0).astype(F32)
    return jnp.concatenate([_ones_right(x[:, g * LANES:(g + 1) * LANES], same) for g in range(x.shape[1] // LANES)], axis=1)


def _ones_right(x, ones):
    t = ones.astype(BF16)
    x1 = x.astype(BF16)
    r1 = x - x1.astype(F32)
    x2 = r1.astype(BF16)
    x3 = (r1 - x2.astype(F32)).astype(BF16)
    dg = lambda q: jnp.dot(q, t, preferred_element_type=F32)
    return dg(x1) + dg(x2) + dg(x3)


@jax.custom_vjp
def head_sum(x):
    return _head_dot(x)


def _head_sum_fwd(x):
    return head_sum(x), None


def _head_sum_bwd(_, g):
    return (_head_dot(g),)


head_sum.defvjp(_head_sum_fwd, _head_sum_bwd)


def cat_rows(*xs):
    return jnp.concatenate(xs, axis=0)


def _iota2(n, m):
    return lax.broadcasted_iota(jnp.int32, (n, m), 0), lax.broadcasted_iota(jnp.int32, (n, m), 1)


def tril_ones(c, strict=False):
    i, j = _iota2(c, c)
    return jnp.where((i > j) if strict else (i >= j), 1.0, 0.0).astype(F32)


def row_of(x, r):
    i = lax.broadcasted_iota(jnp.int32, x.shape, 0)
    return jnp.sum(jnp.where(i == r, x, 0.0), axis=0, keepdims=True)


@jax.custom_vjp
def shift_rows(x, prev):
    r = lax.broadcasted_iota(jnp.int32, x.shape, 0)
    return jnp.where(r == 0, prev, pltpu.roll(x, 1, 0))


def _shift_fwd(x, prev):
    return shift_rows(x, prev), None


def _shift_bwd(_, g):
    c = g.shape[0]
    r = lax.broadcasted_iota(jnp.int32, g.shape, 0)
    return jnp.where(r == c - 1, 0.0, pltpu.roll(g, c - 1, 0)), row_of(g, 0)


shift_rows.defvjp(_shift_fwd, _shift_bwd)


def log_sigmoid(x):
    return jnp.minimum(x, 0.0) - jnp.log(1.0 + jnp.exp(-jnp.abs(x)))


def softplus(x):
    return jnp.maximum(x, 0.0) + jnp.log(1.0 + jnp.exp(-jnp.abs(x)))


def sigmoid(x):
    return 1.0 / (1.0 + jnp.exp(-x))


def rms(x, w, eps=NORM_EPS):
    return x * lax.rsqrt(jnp.mean(x * x, axis=-1, keepdims=True) + eps) * w


def gla_chunk(state, toks, params):
    q, k, v, glow = toks
    gk_up, bias, norm_w = params
    c = glow.shape[0]
    heads = range(GLA_HEADS)
    hk = lambda x, h: x[:, h * GLA_DK:(h + 1) * GLA_DK]
    hv = lambda x, h: x[:, h * GLA_DV:(h + 1) * GLA_DV]
    ltri = tril_ones(c)
    g = log_sigmoid(mmx(glow, gk_up, "nn", P_LORA) + bias) / GLA_NORMALIZER
    b = cumsum_rows(g)
    ref = lax.stop_gradient(row_of(b, c // 2))
    last = row_of(b, c - 1)
    ql = q * (GLA_DK ** -0.5) * jnp.exp(b - ref)
    kr = k * jnp.exp(ref - b)
    kl = k * jnp.exp(last - b)
    e_ref, e_last = jnp.exp(ref), jnp.exp(last)
    sc = [mmx(hk(ql, h), cat_rows(hk(kr, h), state[h] * hk(e_ref, h)), "nt", P_GLA) for h in heads]
    o = [mmx(sc[h][:, :c] * ltri, hv(v, h), "nn", P_GLA) + sc[h][:, c:] for h in heads]
    s1 = [state[h] * hk(e_last, h) + mmx(hv(v, h), hk(kl, h), "tn", P_GLA) for h in heads]
    o = [x * lax.rsqrt(jnp.mean(x * x, axis=-1, keepdims=True) + NORM_EPS) * norm_w for x in o]
    return jnp.concatenate(o, axis=1), s1


def rwkv_chunk(state, toks, params):
    S, pr, pk, pv, pxw, pxa = state
    r_, k_, v_, xw_, xa_ = toks
    mu_r, mu_k, mu_v, mu_xw, mu_xa, w0, w_up, a0, a_up, k_k, k_a, r_k, ln_w, ln_b = params
    c, n = xw_.shape[0], RWKV_N
    heads = range(RWKV_HEADS)
    hs = lambda x, h: x[:, h * n:(h + 1) * n]
    ltri = tril_ones(c)
    stri = tril_ones(c, strict=True)

    def lerp(x, prev, mu):
        return x + (shift_rows(x, prev) - x) * mu

    xw = jnp.tanh(lerp(xw_, pxw, mu_xw))
    xa = lerp(xa_, pxa, mu_xa)
    r = lerp(r_, pr, mu_r)
    k = lerp(k_, pk, mu_k)
    v = lerp(v_, pv, mu_v)
    w = -softplus(-(w0 + mmx(xw, w_up, "nn", P_LORA))) - 0.5
    lw = -jnp.exp(w)
    asig = sigmoid(a0 + mmx(xa, a_up, "nn", P_LORA))
    kk = k * k_k
    kk = kk / jnp.maximum(jnp.sqrt(head_sum(kk * kk)), 1e-12)
    k2 = k * (1.0 + (asig - 1.0) * k_a)
    b = kk * asig
    cum = cumsum_rows(lw)
    ref = lax.stop_gradient(row_of(cum, c // 2))
    last = row_of(cum, c - 1)
    at = -kk * jnp.exp(cum - lw - ref)
    rt = r * jnp.exp(cum - ref)
    e_out = jnp.exp(ref - cum)
    bt, kt = b * e_out, k2 * e_out
    e_tail = jnp.exp(last - cum)
    bl, kl = b * e_tail, k2 * e_tail
    e_ref, e_last = jnp.exp(ref), jnp.exp(last)
    g = [mmx(cat_rows(hs(at, h), hs(rt, h)), cat_rows(hs(bt, h), hs(kt, h), S[h] * hs(e_ref, h)), "nt", P_RWKV_G) for h in heads]
    aab = [x[:c, :c] * stri for x in g]
    aak = [x[:c, c:2 * c] * stri for x in g]
    arb = [x[c:, :c] * ltri for x in g]
    ark = [x[c:, c:2 * c] * ltri for x in g]
    av = [mmx(cat_rows(aak[h], ark[h]), hs(v, h), "nn", P_RWKV) for h in heads]
    u = [g[h][:c, 2 * c:] + av[h][:c] for h in heads]
    p = aab
    n_double = max(1, (c - 1).bit_length())
    for it in range(n_double):
        if it + 1 < n_double:
            y = [mmx(p[h], jnp.concatenate([p[h], u[h]], axis=1), "nn", P_RWKV) for h in heads]
            u = [u[h] + y[h][:, c:] for h in heads]
            p = [y[h][:, :c] for h in heads]
        else:
            u = [u[h] + mmx(p[h], u[h], "nn", P_RWKV) for h in heads]
    o = [g[h][c:, 2 * c:] + av[h][c:] + mmx(arb[h], u[h], "nn", P_RWKV) for h in heads]
    s1 = [S[h] * hs(e_last, h) + mmx(cat_rows(u[h], hs(v, h)), cat_rows(hs(bl, h), hs(kl, h)), "tn", P_RWKV) for h in heads]
    o = jnp.concatenate(o, axis=1)
    d = o - head_sum(o) * (1.0 / n)
    var = head_sum(d * d) * (1.0 / n)
    o = d * lax.rsqrt(var + RWKV_LN_EPS) * ln_w + ln_b + head_sum(r * k2 * r_k) * v
    new_state = (s1, row_of(r_, c - 1), row_of(k_, c - 1), row_of(v_, c - 1), row_of(xw_, c - 1), row_of(xa_, c - 1))
    return o, new_state


def rope_mat():
    i, j = _iota2(SWA_HD, SWA_HD)
    plus = (j >= 8) & (j < 16) & (i == j - 8)
    minus = (j < 8) & (i == j + 8)
    return jnp.where(plus, 1.0, 0.0).astype(F32) - jnp.where(minus, 1.0, 0.0).astype(F32)


def swa_chunk(state, toks, params, first):
    kprev, vprev = state
    q_, k_, v_, cos, sin = toks
    bq, bk, bv, sinks = params
    c = cos.shape[0]
    ng = SWA_GROUP
    rm = rope_mat()
    qi, kj = _iota2(ng * c, 2 * c)
    qpos = qi & (c - 1)
    ok = ((kj < c) & (kj > qpos) & jnp.logical_not(first)) | ((kj >= c) & (qpos >= kj - c))
    cos_g, sin_g = cat_rows(*[cos] * ng), cat_rows(*[sin] * ng)

    def rope(x, cs, sn):
        return x * cs + mmx(x, rm, "nn", P_ROPE) * sn

    groups = range(SWA_KV_HEADS)
    hs = lambda g: range(g * ng, (g + 1) * ng)
    k = [rope(k_[g] + bk[g], cos, sin) for g in groups]
    v = [v_[g] + bv[g] for g in groups]
    q = [rope(cat_rows(*[q_[h] + bq[h] for h in hs(g)]), cos_g, sin_g) * (SWA_HD ** -0.5) for g in groups]
    s = [jnp.where(ok, mmx(q[g], cat_rows(kprev[g], k[g]), "nt", P_SWA), NEG) for g in groups]
    sink = [cat_rows(*[jnp.broadcast_to(sinks[h], (c, 1)) for h in hs(g)]) for g in groups]
    m = [lax.stop_gradient(jnp.maximum(jnp.max(s[g], axis=-1, keepdims=True), sink[g])) for g in groups]
    p = [jnp.exp(s[g] - m[g]) for g in groups]
    ones = jnp.ones((2 * c, SWA_HD), F32)
    pv = [mmx(p[g], jnp.concatenate([cat_rows(vprev[g], v[g]), ones], axis=1), "nn", P_SWA) for g in groups]
    o = [pv[g][:, :SWA_HD] / (pv[g][:, SWA_HD:] + jnp.exp(sink[g] - m[g])) for g in groups]
    outs = [o[g][j * c:(j + 1) * c] for g in groups for j in range(ng)]
    return outs, (k, v)


def _heads(ref, n, w, rows=slice(None)):
    return [ref[rows, h * w:(h + 1) * w] for h in range(n)]


def _put_heads(ref, vals, w, rows=slice(None), add=False):
    for h, val in enumerate(vals):
        if add:
            ref[rows, h * w:(h + 1) * w] += val
        else:
            ref[rows, h * w:(h + 1) * w] = val


def _col(block_w, name, table):
    off, w = table[name]
    assert off % block_w == 0 and w % block_w == 0
    return off // block_w


def _tok_spec(c, w, colblock, n=None):
    if n is None:
        return pl.BlockSpec((c, w), lambda i: (i, colblock))
    return pl.BlockSpec((c, w), lambda i: (n - 1 - i, colblock))


def _full_spec(shape):
    return pl.BlockSpec(shape, lambda i: (0,) * len(shape))


def _matmul(name, a, b, mode, tm, tn, out_dtype=F32):
    (m, kd) = (a.shape[1], a.shape[0]) if mode == "tn" else a.shape
    n = b.shape[0] if mode == "nt" else b.shape[1]
    assert m % tm == 0 and n % tn == 0
    a_spec = pl.BlockSpec((kd, tm), lambda j, i: (0, i)) if mode == "tn" else pl.BlockSpec((tm, kd), lambda j, i: (i, 0))
    b_spec = pl.BlockSpec((tn, kd), lambda j, i: (j, 0)) if mode == "nt" else pl.BlockSpec((kd, tn), lambda j, i: (0, j))

    def body(a_ref, b_ref, o_ref):
        o_ref[...] = lax.dot_general(a_ref[...].astype(BF16), b_ref[...].astype(BF16), DIMS[mode],
                                     preferred_element_type=F32).astype(out_dtype)

    return pl.pallas_call(
        body, name=name, grid=(n // tn, m // tm), in_specs=[a_spec, b_spec],
        out_specs=pl.BlockSpec((tm, tn), lambda j, i: (i, j)), out_shape=jax.ShapeDtypeStruct((m, n), out_dtype),
        compiler_params=_cparams(("arbitrary", "arbitrary")))(a, b)


TOK_TILE = 512


def _norm_fwd(name, x, w, y=None):
    t, d = x.shape
    tile = pl.BlockSpec((TOK_TILE, d), lambda i: (i, 0))

    def body(*refs):
        if y is None:
            x_ref, w_ref, hn_ref = refs
            h = x_ref[...]
        else:
            x_ref, y_ref, w_ref, h_ref, hn_ref = refs
            h = x_ref[...] + y_ref[...]
            h_ref[...] = h
        hn_ref[...] = rms(h, w_ref[...]).astype(BF16)

    ins = [x, w] if y is None else [x, y, w]
    in_specs = [tile, _full_spec((1, d))] if y is None else [tile, tile, _full_spec((1, d))]
    hn_shape = jax.ShapeDtypeStruct((t, d), BF16)
    out_shape = hn_shape if y is None else (jax.ShapeDtypeStruct((t, d), F32), hn_shape)
    out_specs = tile if y is None else (tile, tile)
    return pl.pallas_call(body, name=name, grid=(t // TOK_TILE,), in_specs=in_specs, out_specs=out_specs, out_shape=out_shape,
                          compiler_params=_cparams(("arbitrary",)))(*ins)


def _norm_bwd(name, h, w, dhn, dres):
    t, d = h.shape
    tile = pl.BlockSpec((TOK_TILE, d), lambda i: (i, 0))

    def body(h_ref, w_ref, dhn_ref, dres_ref, dx_ref, dw_ref):
        @pl.when(pl.program_id(0) == 0)
        def _():
            dw_ref[...] = jnp.zeros_like(dw_ref)

        _, vjp = jax.vjp(rms, h_ref[...], w_ref[...])
        dh, dw = vjp(dhn_ref[...])
        dx_ref[...] = dh + dres_ref[...]
        dw_ref[...] += dw

    return pl.pallas_call(body, name=name, grid=(t // TOK_TILE,), in_specs=[tile, _full_spec((1, d)), tile, tile],
                          out_specs=(tile, _full_spec((1, d))),
                          out_shape=(jax.ShapeDtypeStruct((t, d), F32), jax.ShapeDtypeStruct((1, d), F32)),
                          compiler_params=_cparams(("arbitrary",)))(h, w, dhn, dres)


def _gate_fwd(name, outs, proj):
    t = proj.shape[0]
    widths = [o.shape[1] for o in outs]
    n = len(outs)

    def body(*refs):
        o_refs, g_ref, og_ref = refs[:n], refs[n], refs[n + 1]
        c = 0
        for o_ref, w in zip(o_refs, widths):
            g = g_ref[:, c:c + w]
            og_ref[:, c:c + w] = (o_ref[...] * (g * sigmoid(g))).astype(BF16)
            c += w

    in_specs = [pl.BlockSpec((TOK_TILE, w), lambda i: (i, 0)) for w in widths] + [pl.BlockSpec((TOK_TILE, 1024), lambda i: (i, 0))]
    return pl.pallas_call(body, name=name, grid=(t // TOK_TILE,), in_specs=in_specs,
                          out_specs=pl.BlockSpec((TOK_TILE, 1024), lambda i: (i, 0)),
                          out_shape=jax.ShapeDtypeStruct((t, 1024), BF16), compiler_params=_cparams(("arbitrary",)))(*outs, proj)


def _gate_bwd(name, outs, proj, dog):
    t = proj.shape[0]
    widths = [o.shape[1] for o in outs]
    n = len(outs)

    def body(*refs):
        o_refs, g_ref, dog_ref = refs[:n], refs[n], refs[n + 1]
        do_refs, dg_ref = refs[n + 2:2 * n + 2], refs[2 * n + 2]
        c = 0
        for o_ref, do_ref, w in zip(o_refs, do_refs, widths):
            g = g_ref[:, c:c + w]
            dog_ = dog_ref[:, c:c + w]
            s = sigmoid(g)
            do_ref[...] = dog_ * (g * s)
            dg_ref[:, c:c + w] = dog_ * o_ref[...] * (s * (1.0 + g * (1.0 - s)))
            c += w

    o_specs = [pl.BlockSpec((TOK_TILE, w), lambda i: (i, 0)) for w in widths]
    wide = pl.BlockSpec((TOK_TILE, 1024), lambda i: (i, 0))
    return pl.pallas_call(body, name=name, grid=(t // TOK_TILE,), in_specs=o_specs + [wide, wide], out_specs=tuple(o_specs) + (wide,),
                          out_shape=tuple(jax.ShapeDtypeStruct((t, w), F32) for w in widths) + (jax.ShapeDtypeStruct((t, 1024), F32),),
                          compiler_params=_cparams(("arbitrary",)))(*outs, proj, dog)


def _top(h1, y1, b_out1, fw, target):
    t, d = h1.shape
    tile = pl.BlockSpec((TOK_TILE, d), lambda i: (i, 0))
    vec = _full_spec((1, d))

    def body(h1_ref, y1_ref, b_ref, fw_ref, tgt_ref, dh2_ref, loss_ref, db_ref, dfw_ref):
        @pl.when(pl.program_id(0) == 0)
        def _():
            loss_ref[...] = jnp.zeros_like(loss_ref)
            db_ref[...] = jnp.zeros_like(db_ref)
            dfw_ref[...] = jnp.zeros_like(dfw_ref)

        tgt = tgt_ref[...]

        def f(h2, w):
            err = rms(h2, w) - tgt
            per_tok = jnp.mean(err * err, axis=-1, keepdims=True)
            return 0.5 * jnp.sum(per_tok, axis=0, keepdims=True)

        h2 = h1_ref[...] + y1_ref[...] + b_ref[...]
        loss, vjp = jax.vjp(f, h2, fw_ref[...])
        dh2, dfw = vjp(jnp.ones((1, 1), F32))
        dh2_ref[...] = dh2
        loss_ref[...] += jnp.broadcast_to(loss, loss_ref.shape)
        db_ref[...] += jnp.sum(dh2, axis=0, keepdims=True)
        dfw_ref[...] += dfw

    return pl.pallas_call(body, name="top_loss", grid=(t // TOK_TILE,), in_specs=[tile, tile, vec, vec, tile],
                          out_specs=(tile, _full_spec((1, LANES)), vec, vec),
                          out_shape=(jax.ShapeDtypeStruct((t, d), F32), jax.ShapeDtypeStruct((1, LANES), F32),
                                     jax.ShapeDtypeStruct((1, d), F32), jax.ShapeDtypeStruct((1, d), F32)),
                          compiler_params=_cparams(("arbitrary",)))(h1, y1, b_out1, fw, target)


def _gla_load(q_ref, k_ref, v_ref, gl_ref, up_ref, bias_ref, nw_ref):
    toks = (q_ref[...], k_ref[...], v_ref[...], gl_ref[...])
    params = (up_ref[...], bias_ref[...], nw_ref[...])
    return toks, params


def _gla_specs(c, n=None):
    toks = [_tok_spec(c, 256, _col(256, "gq", C0), n), _tok_spec(c, 256, _col(256, "gk", C0), n),
            _tok_spec(c, 512, _col(512, "gv", C0), n), _tok_spec(c, 128, _col(128, "glow", C0), n)]
    params = [_full_spec((128, 256)), _full_spec((1, 256)), _full_spec((1, 128))]
    return toks, params


def _gla_fwd(proj0, gk_up, gk_bias, norm_w):
    t = proj0.shape[0]
    c = GLA_CHUNK
    nc = t // c
    toks_s, params_s = _gla_specs(c)

    def body(q_ref, k_ref, v_ref, gl_ref, up_ref, bias_ref, nw_ref, o_ref, st_ref, s_scr):
        @pl.when(pl.program_id(0) == 0)
        def _():
            s_scr[...] = jnp.zeros_like(s_scr)

        st_ref[...] = s_scr[...]
        toks, params = _gla_load(q_ref, k_ref, v_ref, gl_ref, up_ref, bias_ref, nw_ref)
        state = [s_scr[h * GLA_DV:(h + 1) * GLA_DV, :] for h in range(GLA_HEADS)]
        o_ref[...], new = gla_chunk(state, toks, params)
        for h in range(GLA_HEADS):
            s_scr[h * GLA_DV:(h + 1) * GLA_DV, :] = new[h]

    return pl.pallas_call(
        body, name="gla_fwd", grid=(nc,), in_specs=toks_s + params_s,
        out_specs=(_tok_spec(c, 512, 0), pl.BlockSpec((512, GLA_DK), lambda i: (i, 0))),
        out_shape=(jax.ShapeDtypeStruct((t, 512), F32), jax.ShapeDtypeStruct((nc * 512, GLA_DK), F32)),
        scratch_shapes=[pltpu.VMEM((512, GLA_DK), F32)], compiler_params=_cparams(("arbitrary",)))(
            proj0, proj0, proj0, proj0, gk_up, gk_bias, norm_w)


def _gla_bwd(proj0, gk_up, gk_bias, norm_w, states, do):
    t = proj0.shape[0]
    c = GLA_CHUNK
    nc = t // c
    toks_s, params_s = _gla_specs(c, nc)

    def body(q_ref, k_ref, v_ref, gl_ref, up_ref, bias_ref, nw_ref, st_ref, do_ref,
             dq_ref, dk_ref, dv_ref, dgl_ref, dup_ref, dbias_ref, dnw_ref, ds_scr):
        @pl.when(pl.program_id(0) == 0)
        def _():
            ds_scr[...] = jnp.zeros_like(ds_scr)
            dup_ref[...] = jnp.zeros_like(dup_ref)
            dbias_ref[...] = jnp.zeros_like(dbias_ref)
            dnw_ref[...] = jnp.zeros_like(dnw_ref)

        toks, params = _gla_load(q_ref, k_ref, v_ref, gl_ref, up_ref, bias_ref, nw_ref)
        rows = lambda h: slice(h * GLA_DV, (h + 1) * GLA_DV)
        state = [st_ref[rows(h), :] for h in range(GLA_HEADS)]
        _, vjp = jax.vjp(gla_chunk, state, toks, params)
        dstate_in = [ds_scr[rows(h), :] for h in range(GLA_HEADS)]
        dstate, (dq_ref[...], dk_ref[...], dv_ref[...], dgl_ref[...]), (dup, dbias, dnw) = vjp((do_ref[...], dstate_in))
        dup_ref[...] += dup
        dbias_ref[...] += dbias
        dnw_ref[...] += dnw
        for h in range(GLA_HEADS):
            ds_scr[rows(h), :] = dstate[h]

    rev = lambda w: pl.BlockSpec((c, w), lambda i: (nc - 1 - i, 0))
    return pl.pallas_call(
        body, name="gla_bwd", grid=(nc,),
        in_specs=toks_s + params_s + [pl.BlockSpec((512, GLA_DK), lambda i: (nc - 1 - i, 0)), rev(512)],
        out_specs=(rev(256), rev(256), rev(512), rev(128), _full_spec((128, 256)), _full_spec((1, 256)), _full_spec((1, 128))),
        out_shape=(jax.ShapeDtypeStruct((t, 256), F32), jax.ShapeDtypeStruct((t, 256), F32), jax.ShapeDtypeStruct((t, 512), F32),
                   jax.ShapeDtypeStruct((t, 128), F32), jax.ShapeDtypeStruct((128, 256), F32), jax.ShapeDtypeStruct((1, 256), F32),
                   jax.ShapeDtypeStruct((1, 128), F32)),
        scratch_shapes=[pltpu.VMEM((512, GLA_DK), F32)], compiler_params=_cparams(("arbitrary",)))(
            proj0, proj0, proj0, proj0, gk_up, gk_bias, norm_w, states, do)


RWKV_PARAM_SHAPES = [(1, 512), (1, 512), (1, 512), (1, 128), (1, 128), (1, 512), (128, 512), (1, 512), (128, 512),
                     (1, 512), (1, 512), (1, 512), (1, 512), (1, 512)]
PREV_W = 1792
PREV_COLS = [slice(0, 512), slice(512, 1024), slice(1024, 1536), slice(1536, 1664), slice(1664, 1792)]


def _rwkv_load(r_ref, k_ref, v_ref, xw_ref, xa_ref, p_refs):
    toks = (r_ref[...], k_ref[...], v_ref[...], xw_ref[...], xa_ref[...])
    return toks, tuple(p[...] for p in p_refs)


def _rwkv_state(s_ref, prev_ref):
    n = RWKV_N
    S = [s_ref[h * n:(h + 1) * n, :] for h in range(RWKV_HEADS)]
    return (S,) + tuple(prev_ref[0:1, cols] for cols in PREV_COLS)


def _rwkv_put_state(s_ref, prev_ref, state):
    n = RWKV_N
    for h in range(RWKV_HEADS):
        s_ref[h * n:(h + 1) * n, :] = state[0][h]
    for cols, val in zip(PREV_COLS, state[1:]):
        prev_ref[0:1, cols] = val


def _rwkv_specs(c, n=None):
    toks = [_tok_spec(c, 512, _col(512, "r", C0), n), _tok_spec(c, 512, _col(512, "k", C0), n),
            _tok_spec(c, 512, _col(512, "v", C0), n), _tok_spec(c, 128, _col(128, "xw", C0), n),
            _tok_spec(c, 128, _col(128, "xa", C0), n)]
    return toks, [_full_spec(s) for s in RWKV_PARAM_SHAPES]


def _rwkv_fwd(proj0, params):
    t = proj0.shape[0]
    c = RWKV_CHUNK
    nc = t // c
    toks_s, params_s = _rwkv_specs(c)
    npar = len(params)

    def body(*refs):
        tok_refs, p_refs = refs[:5], refs[5:5 + npar]
        o_ref, st_ref, pst_ref, s_scr, prev_scr = refs[5 + npar:]

        @pl.when(pl.program_id(0) == 0)
        def _():
            s_scr[...] = jnp.zeros_like(s_scr)
            prev_scr[...] = jnp.zeros_like(prev_scr)

        st_ref[...] = s_scr[...]
        pst_ref[...] = prev_scr[...]
        toks, prm = _rwkv_load(*tok_refs, p_refs)
        o_ref[...], new = rwkv_chunk(_rwkv_state(s_scr, prev_scr), toks, prm)
        _rwkv_put_state(s_scr, prev_scr, new)

    return pl.pallas_call(
        body, name="rwkv_fwd", grid=(nc,), in_specs=toks_s + params_s,
        out_specs=(_tok_spec(c, 512, 0), pl.BlockSpec((512, RWKV_N), lambda i: (i, 0)), pl.BlockSpec((8, PREV_W), lambda i: (i, 0))),
        out_shape=(jax.ShapeDtypeStruct((t, 512), F32), jax.ShapeDtypeStruct((nc * 512, RWKV_N), F32),
                   jax.ShapeDtypeStruct((nc * 8, PREV_W), F32)),
        scratch_shapes=[pltpu.VMEM((512, RWKV_N), F32), pltpu.VMEM((8, PREV_W), F32)],
        compiler_params=_cparams(("arbitrary",)))(proj0, proj0, proj0, proj0, proj0, *params)


def _rwkv_bwd(proj0, params, states, prevs, do):
    t = proj0.shape[0]
    c = RWKV_CHUNK
    nc = t // c
    toks_s, params_s = _rwkv_specs(c, nc)
    npar = len(params)

    def body(*refs):
        tok_refs, p_refs = refs[:5], refs[5:5 + npar]
        st_ref, pst_ref, do_ref = refs[5 + npar:8 + npar]
        dtok_refs = refs[8 + npar:13 + npar]
        dp_refs = refs[13 + npar:13 + 2 * npar]
        ds_scr, dprev_scr = refs[13 + 2 * npar:]

        @pl.when(pl.program_id(0) == 0)
        def _():
            ds_scr[...] = jnp.zeros_like(ds_scr)
            dprev_scr[...] = jnp.zeros_like(dprev_scr)
            for dp in dp_refs:
                dp[...] = jnp.zeros_like(dp)

        toks, prm = _rwkv_load(*tok_refs, p_refs)
        _, vjp = jax.vjp(rwkv_chunk, _rwkv_state(st_ref, pst_ref), toks, prm)
        dstate, dtoks, dprm = vjp((do_ref[...], _rwkv_state(ds_scr, dprev_scr)))
        for ref, val in zip(dtok_refs, dtoks):
            ref[...] = val
        for ref, val in zip(dp_refs, dprm):
            ref[...] += val
        _rwkv_put_state(ds_scr, dprev_scr, dstate)

    rev = lambda w: pl.BlockSpec((c, w), lambda i: (nc - 1 - i, 0))
    return pl.pallas_call(
        body, name="rwkv_bwd", grid=(nc,),
        in_specs=toks_s + params_s + [pl.BlockSpec((512, RWKV_N), lambda i: (nc - 1 - i, 0)),
                                      pl.BlockSpec((8, PREV_W), lambda i: (nc - 1 - i, 0)), rev(512)],
        out_specs=tuple([rev(512), rev(512), rev(512), rev(128), rev(128)] + params_s),
        out_shape=tuple([jax.ShapeDtypeStruct((t, w), F32) for w in (512, 512, 512, 128, 128)]
                        + [jax.ShapeDtypeStruct(s, F32) for s in RWKV_PARAM_SHAPES]),
        scratch_shapes=[pltpu.VMEM((512, RWKV_N), F32), pltpu.VMEM((8, PREV_W), F32)],
        compiler_params=_cparams(("arbitrary",)))(proj0, proj0, proj0, proj0, proj0, *params, states, prevs, do)


def _swa_load(q_ref, k_ref, v_ref, cos_ref, sin_ref, bq_ref, bk_ref, bv_ref, sk_ref):
    toks = (_heads(q_ref, 16, SWA_HD), _heads(k_ref, 4, SWA_HD), _heads(v_ref, 4, SWA_HD), cos_ref[...], sin_ref[...])
    params = (_heads(bq_ref, 16, SWA_HD), _heads(bk_ref, 4, SWA_HD), _heads(bv_ref, 4, SWA_HD), _heads(sk_ref, 16, 1))
    return toks, params


def _swa_specs(c, n=None):
    toks = [_tok_spec(c, 1024, _col(1024, "q", C1), n), _tok_spec(c, 256, _col(256, "k", C1), n),
            _tok_spec(c, 256, _col(256, "v", C1), n), _tok_spec(c, SWA_HD, 0, n), _tok_spec(c, SWA_HD, 0, n)]
    params = [_full_spec((1, 1024)), _full_spec((1, 256)), _full_spec((1, 256)), _full_spec((1, 16))]
    return toks, params


def _swa_fwd(proj1, cos, sin, bq, bk, bv, sinks):
    t = proj1.shape[0]
    c = WINDOW
    nb = t // c
    toks_s, params_s = _swa_specs(c)

    def body(q_ref, k_ref, v_ref, cos_ref, sin_ref, bq_ref, bk_ref, bv_ref, sk_ref, o_ref, kst_ref, vst_ref, k_scr, v_scr):
        first = pl.program_id(0) == 0

        @pl.when(first)
        def _():
            k_scr[...] = jnp.zeros_like(k_scr)
            v_scr[...] = jnp.zeros_like(v_scr)

        kst_ref[...] = k_scr[...]
        vst_ref[...] = v_scr[...]
        toks, params = _swa_load(q_ref, k_ref, v_ref, cos_ref, sin_ref, bq_ref, bk_ref, bv_ref, sk_ref)
        outs, (kn, vn) = swa_chunk((_heads(k_scr, 4, SWA_HD), _heads(v_scr, 4, SWA_HD)), toks, params, first)
        _put_heads(o_ref, outs, SWA_HD)
        _put_heads(k_scr, kn, SWA_HD)
        _put_heads(v_scr, vn, SWA_HD)

    return pl.pallas_call(
        body, name="swa_fwd", grid=(nb,), in_specs=toks_s + params_s,
        out_specs=(_tok_spec(c, 1024, 0), _tok_spec(c, 256, 0), _tok_spec(c, 256, 0)),
        out_shape=(jax.ShapeDtypeStruct((t, 1024), F32), jax.ShapeDtypeStruct((t, 256), F32), jax.ShapeDtypeStruct((t, 256), F32)),
        scratch_shapes=[pltpu.VMEM((c, 256), F32), pltpu.VMEM((c, 256), F32)],
        compiler_params=_cparams(("arbitrary",)))(proj1, proj1, proj1, cos, sin, bq, bk, bv, sinks)


def _swa_bwd(proj1, cos, sin, bq, bk, bv, sinks, kst, vst, do):
    t = proj1.shape[0]
    c = WINDOW
    nb = t // c
    toks_s, params_s = _swa_specs(c, nb)

    def body(q_ref, k_ref, v_ref, cos_ref, sin_ref, bq_ref, bk_ref, bv_ref, sk_ref, kst_ref, vst_ref, do_ref,
             dq_ref, dk_ref, dv_ref, dbq_ref, dbk_ref, dbv_ref, dsk_ref, dk_scr, dv_scr):
        i = pl.program_id(0)

        @pl.when(i == 0)
        def _():
            dk_scr[...] = jnp.zeros_like(dk_scr)
            dv_scr[...] = jnp.zeros_like(dv_scr)
            for ref in (dbq_ref, dbk_ref, dbv_ref, dsk_ref):
                ref[...] = jnp.zeros_like(ref)

        first = i == nb - 1
        toks, params = _swa_load(q_ref, k_ref, v_ref, cos_ref, sin_ref, bq_ref, bk_ref, bv_ref, sk_ref)
        f = functools.partial(swa_chunk, first=first)
        _, vjp = jax.vjp(f, (_heads(kst_ref, 4, SWA_HD), _heads(vst_ref, 4, SWA_HD)), toks, params)
        dstate_in = (_heads(dk_scr, 4, SWA_HD), _heads(dv_scr, 4, SWA_HD))
        (dkp, dvp), (dq, dk, dv, _, _), (dbq, dbk, dbv, dsk) = vjp((_heads(do_ref, 16, SWA_HD), dstate_in))
        _put_heads(dq_ref, dq, SWA_HD)
        _put_heads(dk_ref, dk, SWA_HD)
        _put_heads(dv_ref, dv, SWA_HD)
        _put_heads(dbq_ref, dbq, SWA_HD, add=True)
        _put_heads(dbk_ref, dbk, SWA_HD, add=True)
        _put_heads(dbv_ref, dbv, SWA_HD, add=True)
        _put_heads(dsk_ref, dsk, 1, add=True)
        _put_heads(dk_scr, dkp, SWA_HD)
        _put_heads(dv_scr, dvp, SWA_HD)

    rev = lambda w: pl.BlockSpec((c, w), lambda i: (nb - 1 - i, 0))
    return pl.pallas_call(
        body, name="swa_bwd", grid=(nb,), in_specs=toks_s + params_s + [rev(256), rev(256), rev(1024)],
        out_specs=(rev(1024), rev(256), rev(256), _full_spec((1, 1024)), _full_spec((1, 256)), _full_spec((1, 256)), _full_spec((1, 16))),
        out_shape=(jax.ShapeDtypeStruct((t, 1024), F32), jax.ShapeDtypeStruct((t, 256), F32), jax.ShapeDtypeStruct((t, 256), F32),
                   jax.ShapeDtypeStruct((1, 1024), F32), jax.ShapeDtypeStruct((1, 256), F32), jax.ShapeDtypeStruct((1, 256), F32),
                   jax.ShapeDtypeStruct((1, 16), F32)),
        scratch_shapes=[pltpu.VMEM((c, 256), F32), pltpu.VMEM((c, 256), F32)],
        compiler_params=_cparams(("arbitrary",)))(proj1, proj1, proj1, cos, sin, bq, bk, bv, sinks, kst, vst, do)


MESH = pl.DeviceIdType.MESH
ANY = pl.BlockSpec(memory_space=pl.ANY)


def _my_place():
    return lax.axis_index("x"), lax.axis_index("y"), lax.axis_index("c")


def _all_gather(shards):
    n = len(shards)

    def body(*refs):
        in_refs, out_refs = refs[:n], refs[n:2 * n]
        send_sems, recv_sems, local_sems = refs[2 * n:]
        x, y, c = _my_place()
        me, sibling = (x, y, c), (x, y, 1 - c)
        chips = [(1 - x, y), (x, 1 - y), (1 - x, 1 - y)]

        def slot(out_ref, place):
            px, py, pc = place
            return out_ref.at[4 * px + 2 * py + pc]

        def copy(a, k, block, to, src=None):
            return pltpu.make_async_remote_copy(
                src_ref=slot(out_refs[a], block) if src is None else src, dst_ref=slot(out_refs[a], block),
                send_sem=send_sems.at[a, k], recv_sem=recv_sems.at[a, k], device_id=to, device_id_type=MESH)

        mine = [pltpu.make_async_copy(in_refs[a], slot(out_refs[a], me), local_sems.at[a]) for a in range(n)]
        for cp in mine:
            cp.start()
        first = []
        for a in range(n):
            first.append(copy(a, 0, me, sibling, src=in_refs[a]))
            first += [copy(a, 1 + j, me, (*chip, c), src=in_refs[a]) for j, chip in enumerate(chips)]
        for cp in first:
            cp.start()
        passed = []
        for j, chip in enumerate(chips):
            for a in range(n):
                copy(a, 1 + j, (*chip, c), me).wait_recv()
                fwd = copy(a, 4 + j, (*chip, c), sibling)
                fwd.start()
                passed.append(fwd)
        for a in range(n):
            copy(a, 0, sibling, me).wait_recv()
            for j, chip in enumerate(chips):
                copy(a, 4 + j, (*chip, 1 - c), me).wait_recv()
        for cp in first + passed:
            cp.wait_send()
        for cp in mine:
            cp.wait()

    return pl.pallas_call(
        body, name="all_gather_weights", in_specs=[ANY] * n, out_specs=[ANY] * n,
        out_shape=[jax.ShapeDtypeStruct((N_DEV,) + s.shape, s.dtype) for s in shards],
        scratch_shapes=[pltpu.SemaphoreType.DMA((n, 7)), pltpu.SemaphoreType.DMA((n, 7)), pltpu.SemaphoreType.DMA((n,))],
        compiler_params=pltpu.CompilerParams(has_side_effects=True))(*shards)


def _exchange(parts, rep):
    n = len(parts)

    def body(*refs):
        in_refs, rep_ref = refs[:n], refs[n]
        out_refs, rep_out = refs[n + 1:2 * n + 1], refs[2 * n + 1]
        send_sems, recv_sems, local_sems = refs[2 * n + 2:]
        x, y, c = _my_place()
        my_idx = 4 * x + 2 * y + c
        local = [pltpu.make_async_copy(in_refs[a].at[my_idx], out_refs[a].at[my_idx], local_sems.at[a]) for a in range(n)]
        local.append(pltpu.make_async_copy(rep_ref, rep_out.at[my_idx], local_sems.at[n]))
        for cp in local:
            cp.start()
        copies = []
        for rel in range(1, N_DEV):
            dx, dy, dc = (rel >> 2) & 1, (rel >> 1) & 1, rel & 1
            px, py, pc = x ^ dx, y ^ dy, c ^ dc
            peer_idx = 4 * px + 2 * py + pc
            for a in range(n):
                copies.append(pltpu.make_async_remote_copy(
                    src_ref=in_refs[a].at[peer_idx], dst_ref=out_refs[a].at[my_idx], send_sem=send_sems.at[a, rel - 1],
                    recv_sem=recv_sems.at[a, rel - 1], device_id=(px, py, pc), device_id_type=MESH))
            copies.append(pltpu.make_async_remote_copy(
                src_ref=rep_ref, dst_ref=rep_out.at[my_idx], send_sem=send_sems.at[n, rel - 1],
                recv_sem=recv_sems.at[n, rel - 1], device_id=(px, py, pc), device_id_type=MESH))
        for cp in copies:
            cp.start()
        for cp in copies:
            cp.wait_recv()
        for cp in copies:
            cp.wait_send()
        for cp in local:
            cp.wait()

    outs = pl.pallas_call(
        body, name="exchange_grads", in_specs=[ANY] * (n + 1), out_specs=[ANY] * (n + 1),
        out_shape=[jax.ShapeDtypeStruct(p.shape, p.dtype) for p in parts] + [jax.ShapeDtypeStruct((N_DEV,) + rep.shape, rep.dtype)],
        scratch_shapes=[pltpu.SemaphoreType.DMA((n + 1, 7)), pltpu.SemaphoreType.DMA((n + 1, 7)), pltpu.SemaphoreType.DMA((n + 1,))],
        compiler_params=pltpu.CompilerParams(has_side_effects=True))(*parts, rep)
    return outs[:n], outs[n]


def _adam_math(w, g, m, v):
    m = ADAM_B1 * m + (1.0 - ADAM_B1) * g
    v = ADAM_B2 * v + (1.0 - ADAM_B2) * (g * g)
    m_hat = m / (1.0 - ADAM_B1 ** ADAM_STEP)
    v_hat = v / (1.0 - ADAM_B2 ** ADAM_STEP)
    delta = -ADAM_LR * (m_hat / (jnp.sqrt(v_hat) + ADAM_EPS) + ADAM_WD * w)
    return delta, m, v


def _adamw(name, w, gslots, m, v, tc):
    r, cc = w.shape
    assert cc % tc == 0
    tile = pl.BlockSpec((r, tc), lambda i: (0, i))

    def body(w_ref, g_ref, m_ref, v_ref, go_ref, d_ref, mo_ref, vo_ref):
        g = g_ref[0].astype(F32)
        for s in range(1, N_DEV):
            g = g + g_ref[s].astype(F32)
        d, mn, vn = _adam_math(w_ref[...], g, m_ref[...], v_ref[...])
        go_ref[...] = g
        d_ref[...] = d
        mo_ref[...] = mn
        vo_ref[...] = vn

    shp = jax.ShapeDtypeStruct((r, cc), F32)
    return pl.pallas_call(body, name=name, grid=(cc // tc,),
                          in_specs=[tile, pl.BlockSpec((N_DEV, r, tc), lambda i: (0, 0, i)), tile, tile],
                          out_specs=(tile,) * 4, out_shape=(shp,) * 4, compiler_params=_cparams(("arbitrary",)))(w, gslots, m, v)


PACK_TILE = 8 * LANES


def _size(shape):
    n = 1
    for d in shape:
        n *= d
    return n


def _pack(arrays):
    rows = []
    for a in arrays:
        flat = a.reshape(-1).astype(F32)
        rows.append(jnp.pad(flat, (0, (-flat.shape[0]) % PACK_TILE)).reshape(-1, LANES))
    return jnp.concatenate(rows, axis=0)


def _unpack(packed, shapes):
    outs, r = [], 0
    for s in shapes:
        n = _size(s)
        nr = -(-n // PACK_TILE) * 8
        outs.append(packed[r:r + nr].reshape(-1)[:n].reshape(s))
        r += nr
    return outs


def _pack_dev(arrays):
    rows = []
    for a in arrays:
        flat = a.reshape(N_DEV, -1).astype(F32)
        rows.append(jnp.pad(flat, ((0, 0), (0, (-flat.shape[1]) % PACK_TILE))).reshape(N_DEV, -1, LANES))
    return jnp.concatenate(rows, axis=1)


def _unpack_dev(packed, shapes):
    outs, r = [], 0
    for s in shapes:
        n = _size(s)
        nr = -(-n // PACK_TILE) * 8
        outs.append(packed[:, r:r + nr].reshape(N_DEV, -1)[:, :n].reshape((N_DEV,) + tuple(s)))
        r += nr
    return outs


def _rope_tables(t):
    half = 8
    inv_freq = ROPE_THETA ** (-jnp.arange(half, dtype=F32) / half)
    ang = jnp.arange(t, dtype=F32)[:, None] * inv_freq
    cos = jnp.concatenate([jnp.cos(ang), jnp.cos(ang), jnp.ones((t, SWA_HD - 16), F32)], axis=1)
    sin = jnp.concatenate([jnp.sin(ang), jnp.sin(ang), jnp.zeros((t, SWA_HD - 16), F32)], axis=1)
    return cos, sin


def _pad_to(a, rows=None, cols=None):
    r = 0 if rows is None else rows - a.shape[0]
    c = 0 if cols is None else cols - a.shape[1]
    return jnp.pad(a, ((0, r), (0, c)))


ORIG0 = dict(gq=(0, 256), gk=(256, 256), gv=(512, 512), glow=(1024, 16), r=(1040, 512), k=(1552, 512), v=(2064, 512),
             xw=(2576, 64), xa=(2640, 64), gate=(2704, 1024))
ORIG0_ORDER = ["gq", "gk", "gv", "glow", "r", "k", "v", "xw", "xa", "gate"]


def _w0t_to_padded(wt):
    rows, at = [], 0
    for name, (off, width) in sorted(C0.items(), key=lambda kv: kv[1][0]):
        assert off == at
        src, src_w = ORIG0[name]
        rows.append(_pad_to(wt[src:src + src_w], rows=width))
        at += width
    rows.append(jnp.zeros((N0P - at, wt.shape[1]), wt.dtype))
    return jnp.concatenate(rows, axis=0)


def _w0t_from_padded(wpt):
    return jnp.concatenate([wpt[C0[n][0]:C0[n][0] + ORIG0[n][1]] for n in ORIG0_ORDER], axis=0)


def _w1t_to_mine(wt):
    return jnp.concatenate([wt[1536:2560], wt[:1536]], axis=0)


def _w1t_from_mine(wt):
    return jnp.concatenate([wt[1024:2560], wt[:1024]], axis=0)


def kernel(x, norm_w, w_in0, gla_gk_up, gla_gk_bias, gla_norm_w, rwkv_mu, rwkv_w0, rwkv_w_up, rwkv_a0, rwkv_a_up, rwkv_k_k, rwkv_k_a, rwkv_r_k, rwkv_ln_w, rwkv_ln_b, w_out0, w_in1, b_in1, attn_sinks, w_out1, b_out1, final_norm_w, loss_target, m_norm_w, m_w_in0, m_gla_gk_up, m_gla_gk_bias, m_gla_norm_w, m_rwkv_mu, m_rwkv_w0, m_rwkv_w_up, m_rwkv_a0, m_rwkv_a_up, m_rwkv_k_k, m_rwkv_k_a, m_rwkv_r_k, m_rwkv_ln_w, m_rwkv_ln_b, m_w_out0, m_w_in1, m_b_in1, m_attn_sinks, m_w_out1, m_b_out1, m_final_norm_w, v_norm_w, v_w_in0, v_gla_gk_up, v_gla_gk_bias, v_gla_norm_w, v_rwkv_mu, v_rwkv_w0, v_rwkv_w_up, v_rwkv_a0, v_rwkv_a_up, v_rwkv_k_k, v_rwkv_k_a, v_rwkv_r_k, v_rwkv_ln_w, v_rwkv_ln_b, v_w_out0, v_w_in1, v_b_in1, v_attn_sinks, v_w_out1, v_b_out1, v_final_norm_w):
    weights = dict(norm_w=norm_w, w_in0=w_in0, gla_gk_up=gla_gk_up, gla_gk_bias=gla_gk_bias, gla_norm_w=gla_norm_w, rwkv_mu=rwkv_mu,
                   rwkv_w0=rwkv_w0, rwkv_w_up=rwkv_w_up, rwkv_a0=rwkv_a0, rwkv_a_up=rwkv_a_up, rwkv_k_k=rwkv_k_k, rwkv_k_a=rwkv_k_a,
                   rwkv_r_k=rwkv_r_k, rwkv_ln_w=rwkv_ln_w, rwkv_ln_b=rwkv_ln_b, w_out0=w_out0, w_in1=w_in1, b_in1=b_in1,
                   attn_sinks=attn_sinks, w_out1=w_out1, b_out1=b_out1, final_norm_w=final_norm_w)
    moms = dict(norm_w=m_norm_w, w_in0=m_w_in0, gla_gk_up=m_gla_gk_up, gla_gk_bias=m_gla_gk_bias, gla_norm_w=m_gla_norm_w,
                rwkv_mu=m_rwkv_mu, rwkv_w0=m_rwkv_w0, rwkv_w_up=m_rwkv_w_up, rwkv_a0=m_rwkv_a0, rwkv_a_up=m_rwkv_a_up,
                rwkv_k_k=m_rwkv_k_k, rwkv_k_a=m_rwkv_k_a, rwkv_r_k=m_rwkv_r_k, rwkv_ln_w=m_rwkv_ln_w, rwkv_ln_b=m_rwkv_ln_b,
                w_out0=m_w_out0, w_in1=m_w_in1, b_in1=m_b_in1, attn_sinks=m_attn_sinks, w_out1=m_w_out1, b_out1=m_b_out1,
                final_norm_w=m_final_norm_w)
    vars_ = dict(norm_w=v_norm_w, w_in0=v_w_in0, gla_gk_up=v_gla_gk_up, gla_gk_bias=v_gla_gk_bias, gla_norm_w=v_gla_norm_w,
                 rwkv_mu=v_rwkv_mu, rwkv_w0=v_rwkv_w0, rwkv_w_up=v_rwkv_w_up, rwkv_a0=v_rwkv_a0, rwkv_a_up=v_rwkv_a_up,
                 rwkv_k_k=v_rwkv_k_k, rwkv_k_a=v_rwkv_k_a, rwkv_r_k=v_rwkv_r_k, rwkv_ln_w=v_rwkv_ln_w, rwkv_ln_b=v_rwkv_ln_b,
                 w_out0=v_w_out0, w_in1=v_w_in1, b_in1=v_b_in1, attn_sinks=v_attn_sinks, w_out1=v_w_out1, b_out1=v_b_out1,
                 final_norm_w=v_final_norm_w)
    names = list(weights)
    big = ["w_in0", "w_out0", "w_in1", "w_out1"]
    small_sharded = ["gla_gk_up", "rwkv_w_up", "rwkv_a_up", "b_in1", "b_out1"]
    replicated = [n for n in names if n not in big and n not in small_sharded]

    xs = x[0]
    tgt = loss_target[0]
    t = xs.shape[0]

    small_shard_pack = _pack([weights[n] for n in small_sharded])
    g_in0, g_out0, g_in1, g_out1, g_small = _all_gather(
        [w_in0[0].T.astype(BF16), w_out0[0].astype(BF16), w_in1[0].T.astype(BF16), w_out1[0].astype(BF16), small_shard_pack])
    w0t = _w0t_to_padded(g_in0.reshape(-1, D_MODEL))
    wo0 = g_out0.reshape(1024, D_MODEL)
    w1t = _w1t_to_mine(g_in1.reshape(-1, D_MODEL))
    wo1 = g_out1.reshape(1024, D_MODEL)
    small_shapes = [weights[n].shape for n in small_sharded]
    gs = _unpack_dev(g_small, small_shapes)
    join_cols = lambda a: jnp.transpose(a[:, 0], (1, 0, 2)).reshape(a.shape[2], -1)
    gk_up, w_up, a_up = join_cols(gs[0]), join_cols(gs[1]), join_cols(gs[2])
    b_in, b_out = gs[3].reshape(1, -1), gs[4].reshape(1, -1)

    gk_up_p = _pad_to(gk_up, rows=128)
    mu = rwkv_mu
    rwkv_params = [mu[:, 0:512], mu[:, 512:1024], mu[:, 1024:1536], _pad_to(mu[:, 1536:1600], cols=128), _pad_to(mu[:, 1600:1664], cols=128),
                   rwkv_w0, _pad_to(w_up, rows=128), rwkv_a0, _pad_to(a_up, rows=128), rwkv_k_k, rwkv_k_a, rwkv_r_k.reshape(1, 512),
                   rwkv_ln_w, rwkv_ln_b]
    bq, bk, bv = b_in[:, :1024], b_in[:, 1024:1280], b_in[:, 1280:1536]
    cos, sin = _rope_tables(t)
    nw0, nw1, fw = norm_w[0:1], norm_w[1:2], final_norm_w.reshape(1, D_MODEL)

    hn0 = _norm_fwd("norm0_fwd", xs, nw0)
    proj0 = _matmul("proj0", hn0, w0t, "nt", 512, 1024)
    o_a, gla_states = _gla_fwd(proj0, gk_up_p, gla_gk_bias, gla_norm_w)
    o_b, rwkv_states, rwkv_prevs = _rwkv_fwd(proj0, rwkv_params)
    og0 = _gate_fwd("gate0_fwd", [o_a, o_b], proj0)
    y0 = _matmul("out0", og0, wo0, "nn", 512, 1024)
    h1, hn1 = _norm_fwd("norm1_fwd", xs, nw1, y0)
    proj1 = _matmul("proj1", hn1, w1t, "nt", 512, 1280)
    o_c, kst, vst = _swa_fwd(proj1, cos, sin, bq, bk, bv, attn_sinks)
    og1 = _gate_fwd("gate1_fwd", [o_c], proj1)
    y1 = _matmul("out1", og1, wo1, "nn", 512, 1024)
    dh2, loss_part, d_b_out, d_fw = _top(h1, y1, b_out, fw, tgt)

    dog1 = _matmul("out1_dx", dh2, wo1, "nt", 512, 1024)
    d_wo1 = _matmul("out1_dw", og1, dh2, "tn", 512, 512)
    d_oc, d_gate1 = _gate_bwd("gate1_bwd", [o_c], proj1, dog1)
    dq, dk, dv, d_bq, d_bk, d_bv, d_sinks = _swa_bwd(proj1, cos, sin, bq, bk, bv, attn_sinks, kst, vst, d_oc)
    dproj1 = jnp.concatenate([d_gate1, dq, dk, dv], axis=1).astype(BF16)
    dhn1 = _matmul("proj1_dx", dproj1, w1t, "nn", 512, 1024)
    d_w1t = _matmul("proj1_dw", dproj1, hn1, "tn", 512, 1024)
    dh1, d_nw1 = _norm_bwd("norm1_bwd", h1, nw1, dhn1, dh2)
    dog0 = _matmul("out0_dx", dh1, wo0, "nt", 512, 1024)
    d_wo0 = _matmul("out0_dw", og0, dh1, "tn", 512, 512)
    d_oa, d_ob, d_gate0 = _gate_bwd("gate0_bwd", [o_a, o_b], proj0, dog0)
    dgq, dgk, dgv, dglow, d_gk_up, d_gk_bias, d_gla_nw = _gla_bwd(proj0, gk_up_p, gla_gk_bias, gla_norm_w, gla_states, d_oa)
    rb = _rwkv_bwd(proj0, rwkv_params, rwkv_states, rwkv_prevs, d_ob)
    dr, dkk, dvv, dxw, dxa = rb[:5]
    d_rp = rb[5:]
    dproj0 = jnp.concatenate([d_gate0, dgv, dr, dkk, dvv, dgq, dgk, dglow, dxw, dxa, jnp.zeros((t, 128), F32)], axis=1).astype(BF16)
    dhn0 = _matmul("proj0_dx", dproj0, w0t, "nn", 512, 1024)
    d_w0t = _matmul("proj0_dw", dproj0, hn0, "tn", 512, 1024)
    grad_x, d_nw0 = _norm_bwd("norm0_bwd", xs, nw0, dhn0, dh1)

    contrib = dict(
        norm_w=jnp.concatenate([d_nw0, d_nw1], axis=0), gla_gk_bias=d_gk_bias, gla_norm_w=d_gla_nw,
        rwkv_mu=jnp.concatenate([d_rp[0], d_rp[1], d_rp[2], d_rp[3][:, :64], d_rp[4][:, :64]], axis=1),
        rwkv_w0=d_rp[5], rwkv_a0=d_rp[7], rwkv_k_k=d_rp[9], rwkv_k_a=d_rp[10], rwkv_r_k=d_rp[11].reshape(1, 8, 64),
        rwkv_ln_w=d_rp[12], rwkv_ln_b=d_rp[13], attn_sinks=d_sinks, final_norm_w=d_fw.reshape(D_MODEL))
    rep_pack = _pack([contrib[n] for n in replicated] + [loss_part[:, :1]])

    d_w0 = _w0t_from_padded(d_w0t)
    d_w1 = _w1t_from_mine(d_w1t)
    d_b_in = jnp.concatenate([d_bq, d_bk, d_bv], axis=1)
    full_small = [d_gk_up[:16], d_rp[6][:64], d_rp[8][:64], d_b_in, d_b_out]
    split_cols = lambda a: jnp.transpose(a.reshape(a.shape[0], N_DEV, -1), (1, 0, 2))
    small_parts = [split_cols(a) for a in full_small]
    small_pack = _pack_dev(small_parts)
    row_blocks = lambda a: a.astype(BF16).reshape(N_DEV, -1, D_MODEL)
    parts = [row_blocks(d_w0), row_blocks(d_wo0), row_blocks(d_w1), row_blocks(d_wo1), small_pack]
    (r_in0, r_out0, r_in1, r_out1, r_small), r_rep = _exchange(parts, rep_pack)

    res = {}
    res["w_in0"] = tuple(a.T[None] for a in _adamw("adamw_w_in0", w_in0[0].T, r_in0, m_w_in0[0].T, v_w_in0[0].T, 256))
    res["w_out0"] = tuple(a[None] for a in _adamw("adamw_w_out0", w_out0[0], r_out0, m_w_out0[0], v_w_out0[0], 256))
    res["w_in1"] = tuple(a.T[None] for a in _adamw("adamw_w_in1", w_in1[0].T, r_in1, m_w_in1[0].T, v_w_in1[0].T, 256))
    res["w_out1"] = tuple(a[None] for a in _adamw("adamw_w_out1", w_out1[0], r_out1, m_w_out1[0], v_w_out1[0], 256))
    small_names = small_sharded + replicated
    slots = jnp.concatenate([r_small, r_rep], axis=1)
    n_shard_rows = r_small.shape[1]
    pk = lambda d: jnp.concatenate([_pack([d[n] for n in small_sharded]), _pack([d[n] for n in replicated] + [jnp.zeros((1, 1), F32)])], axis=0)
    g_p, d_p, m_p, v_p = _adamw("adamw_small", pk(weights), slots, pk(moms), pk(vars_), LANES)
    sh_shapes = [weights[n].shape for n in small_sharded]
    rep_shapes = [weights[n].shape for n in replicated] + [(1, 1)]
    for i, packed in enumerate((g_p, d_p, m_p, v_p)):
        vals = _unpack(packed[:n_shard_rows], sh_shapes) + _unpack(packed[n_shard_rows:], rep_shapes)
        for n, val in zip(small_names, vals):
            res.setdefault(n, [None] * 4)[i] = val
        if i == 0:
            loss = vals[-1].reshape(())
    return (loss, grad_x[None], *[res[n][0] for n in names], *[res[n][1] for n in names],
            *[res[n][2] for n in names], *[res[n][3] for n in names])
```

```python
import functools

import jax
import jax.numpy as jnp
from jax import lax
from jax.experimental import pallas as pl
from jax.experimental.pallas import tpu as pltpu

F32 = jnp.float32
BF16 = jnp.bfloat16
HI = lax.Precision.HIGHEST

D_MODEL = 1024
NORM_EPS = 1e-5
GLA_HEADS, GLA_DK, GLA_DV = 4, 64, 128
GLA_NORMALIZER = 16.0
GLA_CHUNK = 64
RWKV_HEADS, RWKV_N = 8, 64
RWKV_LN_EPS = 64e-5
RWKV_CHUNK = 128
SWA_Q_HEADS, SWA_KV_HEADS, SWA_GROUP, SWA_HD = 16, 4, 4, 64
WINDOW = 128
ROPE_THETA = 500000.0
NEG = -1e30
N_DEV = 8
LANES = 128

ADAM_LR, ADAM_B1, ADAM_B2, ADAM_EPS, ADAM_WD, ADAM_STEP = 0.001, 0.9, 0.999, 1e-08, 0.01, 10

N0P = 4096
C0 = dict(gate=(0, 1024), gv=(1024, 512), r=(1536, 512), k=(2048, 512), v=(2560, 512), gq=(3072, 256), gk=(3328, 256),
          glow=(3584, 128), xw=(3712, 128), xa=(3840, 128))
N1P = 2560
C1 = dict(gate=(0, 1024), q=(1024, 1024), k=(2048, 256), v=(2304, 256))

VMEM_LIMIT = 56 * 1024 * 1024

P_LORA = 1
P_GLA = 1
P_RWKV_G = 3
P_RWKV = 1
P_SWA = 1
P_ROPE = 3


def _cparams(sem=None):
    return pltpu.CompilerParams(dimension_semantics=sem, vmem_limit_bytes=VMEM_LIMIT)


DIMS = dict(nn=(((1,), (0,)), ((), ())), nt=(((1,), (1,)), ((), ())), tn=(((0,), (0,)), ((), ())))


def _split_bf16(a):
    hi = a.astype(BF16)
    return hi, (a - hi.astype(F32)).astype(BF16)


def _dot(a, b, mode, passes):
    dg = lambda p, q: lax.dot_general(p, q, DIMS[mode], preferred_element_type=F32)
    if passes == 1:
        return dg(a.astype(BF16), b.astype(BF16))
    if passes == 3:
        (ah, al), (bh, bl) = _split_bf16(a), _split_bf16(b)
        return dg(ah, bh) + dg(al, bh) + dg(ah, bl)
    return lax.dot_general(a, b, DIMS[mode], precision=HI, preferred_element_type=F32)


@functools.partial(jax.custom_vjp, nondiff_argnums=(2, 3))
def mmx(a, b, mode, passes):
    return _dot(a, b, mode, passes)


def _mmx_fwd(a, b, mode, passes):
    return _dot(a, b, mode, passes), (a, b)


def _mmx_bwd(mode, passes, res, g):
    a, b = res
    if mode == "nn":
        return _dot(g, b, "nt", passes), _dot(a, g, "tn", passes)
    if mode == "nt":
        return _dot(g, b, "nn", passes), _dot(g, a, "tn", passes)
    return _dot(b, g, "nt", passes), _dot(a, g, "nn", passes)


mmx.defvjp(_mmx_fwd, _mmx_bwd)


def _tri_dot(tri, x):
    t = tri.astype(BF16)
    x1 = x.astype(BF16)
    r1 = x - x1.astype(F32)
    x2 = r1.astype(BF16)
    x3 = (r1 - x2.astype(F32)).astype(BF16)
    dg = lambda q: jnp.dot(t, q, preferred_element_type=F32)
    return dg(x1) + dg(x2) + dg(x3)


@jax.custom_vjp
def cumsum_rows(x):
    return _tri_dot(tril_ones(x.shape[0]), x)


def _cumsum_fwd(x):
    return cumsum_rows(x), None


def _cumsum_bwd(_, g):
    i, j = _iota2(g.shape[0], g.shape[0])
    return (_tri_dot(jnp.where(i <= j, 1.0, 0.0).astype(F32), g),)


cumsum_rows.defvjp(_cumsum_fwd, _cumsum_bwd)


def _head_dot(x):
    i, j = _iota2(LANES, LANES)
    shift = RWKV_N.bit_length() - 1
    same = jnp.where(jnp.right_shift(i, shift) == jnp.right_shift(j, shift), 1.0, 0.0).astype(F32)
    return jnp.concatenate([_ones_right(x[:, g * LANES:(g + 1) * LANES], same) for g in range(x.shape[1] // LANES)], axis=1)


def _ones_right(x, ones):
    t = ones.astype(BF16)
    x1 = x.astype(BF16)
    r1 = x - x1.astype(F32)
    x2 = r1.astype(BF16)
    x3 = (r1 - x2.astype(F32)).astype(BF16)
    dg = lambda q: jnp.dot(q, t, preferred_element_type=F32)
    return dg(x1) + dg(x2) + dg(x3)


@jax.custom_vjp
def head_sum(x):
    return _head_dot(x)


def _head_sum_fwd(x):
    return head_sum(x), None


def _head_sum_bwd(_, g):
    return (_head_dot(g),)


head_sum.defvjp(_head_sum_fwd, _head_sum_bwd)


def cat_rows(*xs):
    return jnp.concatenate(xs, axis=0)


def _iota2(n, m):
    return lax.broadcasted_iota(jnp.int32, (n, m), 0), lax.broadcasted_iota(jnp.int32, (n, m), 1)


def tril_ones(c, strict=False):
    i, j = _iota2(c, c)
    return jnp.where((i > j) if strict else (i >= j), 1.0, 0.0).astype(F32)


def row_of(x, r):
    i = lax.broadcasted_iota(jnp.int32, x.shape, 0)
    return jnp.sum(jnp.where(i == r, x, 0.0), axis=0, keepdims=True)


@jax.custom_vjp
def shift_rows(x, prev):
    r = lax.broadcasted_iota(jnp.int32, x.shape, 0)
    return jnp.where(r == 0, prev, pltpu.roll(x, 1, 0))


def _shift_fwd(x, prev):
    return shift_rows(x, prev), None


def _shift_bwd(_, g):
    c = g.shape[0]
    r = lax.broadcasted_iota(jnp.int32, g.shape, 0)
    return jnp.where(r == c - 1, 0.0, pltpu.roll(g, c - 1, 0)), row_of(g, 0)


shift_rows.defvjp(_shift_fwd, _shift_bwd)


def log_sigmoid(x):
    return jnp.minimum(x, 0.0) - jnp.log(1.0 + jnp.exp(-jnp.abs(x)))


def softplus(x):
    return jnp.maximum(x, 0.0) + jnp.log(1.0 + jnp.exp(-jnp.abs(x)))


def sigmoid(x):
    return 1.0 / (1.0 + jnp.exp(-x))


def rms(x, w, eps=NORM_EPS):
    return x * lax.rsqrt(jnp.mean(x * x, axis=-1, keepdims=True) + eps) * w


def gla_chunk(state, toks, params):
    q, k, v, glow = toks
    gk_up, bias, norm_w = params
    c = glow.shape[0]
    heads = range(GLA_HEADS)
    hk = lambda x, h: x[:, h * GLA_DK:(h + 1) * GLA_DK]
    hv = lambda x, h: x[:, h * GLA_DV:(h + 1) * GLA_DV]
    ltri = tril_ones(c)
    g = log_sigmoid(mmx(glow, gk_up, "nn", P_LORA) + bias) / GLA_NORMALIZER
    b = cumsum_rows(g)
    ref = lax.stop_gradient(row_of(b, c // 2))
    last = row_of(b, c - 1)
    ql = q * (GLA_DK ** -0.5) * jnp.exp(b - ref)
    kr = k * jnp.exp(ref - b)
    kl = k * jnp.exp(last - b)
    e_ref, e_last = jnp.exp(ref), jnp.exp(last)
    sc = [mmx(hk(ql, h), cat_rows(hk(kr, h), state[h] * hk(e_ref, h)), "nt", P_GLA) for h in heads]
    o = [mmx(sc[h][:, :c] * ltri, hv(v, h), "nn", P_GLA) + sc[h][:, c:] for h in heads]
    s1 = [state[h] * hk(e_last, h) + mmx(hv(v, h), hk(kl, h), "tn", P_GLA) for h in heads]
    o = [x * lax.rsqrt(jnp.mean(x * x, axis=-1, keepdims=True) + NORM_EPS) * norm_w for x in o]
    return jnp.concatenate(o, axis=1), s1


def rwkv_chunk(state, toks, params):
    S, pr, pk, pv, pxw, pxa = state
    r_, k_, v_, xw_, xa_ = toks
    mu_r, mu_k, mu_v, mu_xw, mu_xa, w0, w_up, a0, a_up, k_k, k_a, r_k, ln_w, ln_b = params
    c, n = xw_.shape[0], RWKV_N
    heads = range(RWKV_HEADS)
    hs = lambda x, h: x[:, h * n:(h + 1) * n]
    ltri = tril_ones(c)
    stri = tril_ones(c, strict=True)

    def lerp(x, prev, mu):
        return x + (shift_rows(x, prev) - x) * mu

    xw = jnp.tanh(lerp(xw_, pxw, mu_xw))
    xa = lerp(xa_, pxa, mu_xa)
    r = lerp(r_, pr, mu_r)
    k = lerp(k_, pk, mu_k)
    v = lerp(v_, pv, mu_v)
    w = -softplus(-(w0 + mmx(xw, w_up, "nn", P_LORA))) - 0.5
    lw = -jnp.exp(w)
    asig = sigmoid(a0 + mmx(xa, a_up, "nn", P_LORA))
    kk = k * k_k
    kk = kk / jnp.maximum(jnp.sqrt(head_sum(kk * kk)), 1e-12)
    k2 = k * (1.0 + (asig - 1.0) * k_a)
    b = kk * asig
    cum = cumsum_rows(lw)
    ref = lax.stop_gradient(row_of(cum, c // 2))
    last = row_of(cum, c - 1)
    at = -kk * jnp.exp(cum - lw - ref)
    rt = r * jnp.exp(cum - ref)
    e_out = jnp.exp(ref - cum)
    bt, kt = b * e_out, k2 * e_out
    e_tail = jnp.exp(last - cum)
    bl, kl = b * e_tail, k2 * e_tail
    e_ref, e_last = jnp.exp(ref), jnp.exp(last)
    g = [mmx(cat_rows(hs(at, h), hs(rt, h)), cat_rows(hs(bt, h), hs(kt, h), S[h] * hs(e_ref, h)), "nt", P_RWKV_G) for h in heads]
    aab = [x[:c, :c] * stri for x in g]
    aak = [x[:c, c:2 * c] * stri for x in g]
    arb = [x[c:, :c] * ltri for x in g]
    ark = [x[c:, c:2 * c] * ltri for x in g]
    av = [mmx(cat_rows(aak[h], ark[h]), hs(v, h), "nn", P_RWKV) for h in heads]
    u = [g[h][:c, 2 * c:] + av[h][:c] for h in heads]
    p = aab
    n_double = max(1, (c - 1).bit_length())
    for it in range(n_double):
        if it + 1 < n_double:
            y = [mmx(p[h], jnp.concatenate([p[h], u[h]], axis=1), "nn", P_RWKV) for h in heads]
            u = [u[h] + y[h][:, c:] for h in heads]
            p = [y[h][:, :c] for h in heads]
        else:
            u = [u[h] + mmx(p[h], u[h], "nn", P_RWKV) for h in heads]
    o = [g[h][c:, 2 * c:] + av[h][c:] + mmx(arb[h], u[h], "nn", P_RWKV) for h in heads]
    s1 = [S[h] * hs(e_last, h) + mmx(cat_rows(u[h], hs(v, h)), cat_rows(hs(bl, h), hs(kl, h)), "tn", P_RWKV) for h in heads]
    o = jnp.concatenate(o, axis=1)
    d = o - head_sum(o) * (1.0 / n)
    var = head_sum(d * d) * (1.0 / n)
    o = d * lax.rsqrt(var + RWKV_LN_EPS) * ln_w + ln_b + head_sum(r * k2 * r_k) * v
    new_state = (s1, row_of(r_, c - 1), row_of(k_, c - 1), row_of(v_, c - 1), row_of(xw_, c - 1), row_of(xa_, c - 1))
    return o, new_state


def rope_mat():
    i, j = _iota2(SWA_HD, SWA_HD)
    plus = (j >= 8) & (j < 16) & (i == j - 8)
    minus = (j < 8) & (i == j + 8)
    return jnp.where(plus, 1.0, 0.0).astype(F32) - jnp.where(minus, 1.0, 0.0).astype(F32)


def swa_chunk(state, toks, params, first):
    kprev, vprev = state
    q_, k_, v_, cos, sin = toks
    bq, bk, bv, sinks = params
    c = cos.shape[0]
    ng = SWA_GROUP
    rm = rope_mat()
    qi, kj = _iota2(ng * c, 2 * c)
    qpos = qi & (c - 1)
    ok = ((kj < c) & (kj > qpos) & jnp.logical_not(first)) | ((kj >= c) & (qpos >= kj - c))
    cos_g, sin_g = cat_rows(*[cos] * ng), cat_rows(*[sin] * ng)

    def rope(x, cs, sn):
        return x * cs + mmx(x, rm, "nn", P_ROPE) * sn

    groups = range(SWA_KV_HEADS)
    hs = lambda g: range(g * ng, (g + 1) * ng)
    k = [rope(k_[g] + bk[g], cos, sin) for g in groups]
    v = [v_[g] + bv[g] for g in groups]
    q = [rope(cat_rows(*[q_[h] + bq[h] for h in hs(g)]), cos_g, sin_g) * (SWA_HD ** -0.5) for g in groups]
    s = [jnp.where(ok, mmx(q[g], cat_rows(kprev[g], k[g]), "nt", P_SWA), NEG) for g in groups]
    sink = [cat_rows(*[jnp.broadcast_to(sinks[h], (c, 1)) for h in hs(g)]) for g in groups]
    m = [lax.stop_gradient(jnp.maximum(jnp.max(s[g], axis=-1, keepdims=True), sink[g])) for g in groups]
    p = [jnp.exp(s[g] - m[g]) for g in groups]
    ones = jnp.ones((2 * c, SWA_HD), F32)
    pv = [mmx(p[g], jnp.concatenate([cat_rows(vprev[g], v[g]), ones], axis=1), "nn", P_SWA) for g in groups]
    o = [pv[g][:, :SWA_HD] / (pv[g][:, SWA_HD:] + jnp.exp(sink[g] - m[g])) for g in groups]
    outs = [o[g][j * c:(j + 1) * c] for g in groups for j in range(ng)]
    return outs, (k, v)


def _heads(ref, n, w, rows=slice(None)):
    return [ref[rows, h * w:(h + 1) * w] for h in range(n)]


def _put_heads(ref, vals, w, rows=slice(None), add=False):
    for h, val in enumerate(vals):
        if add:
            ref[rows, h * w:(h + 1) * w] += val
        else:
            ref[rows, h * w:(h + 1) * w] = val


def _col(block_w, name, table):
    off, w = table[name]
    assert off % block_w == 0 and w % block_w == 0
    return off // block_w


def _tok_spec(c, w, colblock, n=None):
    if n is None:
        return pl.BlockSpec((c, w), lambda i: (i, colblock))
    return pl.BlockSpec((c, w), lambda i: (n - 1 - i, colblock))


def _full_spec(shape):
    return pl.BlockSpec(shape, lambda i: (0,) * len(shape))


def _matmul(name, a, b, mode, tm, tn, out_dtype=F32):
    (m, kd) = (a.shape[1], a.shape[0]) if mode == "tn" else a.shape
    n = b.shape[0] if mode == "nt" else b.shape[1]
    assert m % tm == 0 and n % tn == 0
    a_spec = pl.BlockSpec((kd, tm), lambda j, i: (0, i)) if mode == "tn" else pl.BlockSpec((tm, kd), lambda j, i: (i, 0))
    b_spec = pl.BlockSpec((tn, kd), lambda j, i: (j, 0)) if mode == "nt" else pl.BlockSpec((kd, tn), lambda j, i: (0, j))

    def body(a_ref, b_ref, o_ref):
        o_ref[...] = lax.dot_general(a_ref[...].astype(BF16), b_ref[...].astype(BF16), DIMS[mode],
                                     preferred_element_type=F32).astype(out_dtype)

    return pl.pallas_call(
        body, name=name, grid=(n // tn, m // tm), in_specs=[a_spec, b_spec],
        out_specs=pl.BlockSpec((tm, tn), lambda j, i: (i, j)), out_shape=jax.ShapeDtypeStruct((m, n), out_dtype),
        compiler_params=_cparams(("arbitrary", "arbitrary")))(a, b)


TOK_TILE = 512


def _norm_fwd(name, x, w, y=None):
    t, d = x.shape
    tile = pl.BlockSpec((TOK_TILE, d), lambda i: (i, 0))

    def body(*refs):
        if y is None:
            x_ref, w_ref, hn_ref = refs
            h = x_ref[...]
        else:
            x_ref, y_ref, w_ref, h_ref, hn_ref = refs
            h = x_ref[...] + y_ref[...]
            h_ref[...] = h
        hn_ref[...] = rms(h, w_ref[...]).astype(BF16)

    ins = [x, w] if y is None else [x, y, w]
    in_specs = [tile, _full_spec((1, d))] if y is None else [tile, tile, _full_spec((1, d))]
    hn_shape = jax.ShapeDtypeStruct((t, d), BF16)
    out_shape = hn_shape if y is None else (jax.ShapeDtypeStruct((t, d), F32), hn_shape)
    out_specs = tile if y is None else (tile, tile)
    return pl.pallas_call(body, name=name, grid=(t // TOK_TILE,), in_specs=in_specs, out_specs=out_specs, out_shape=out_shape,
                          compiler_params=_cparams(("arbitrary",)))(*ins)


def _norm_bwd(name, h, w, dhn, dres):
    t, d = h.shape
    tile = pl.BlockSpec((TOK_TILE, d), lambda i: (i, 0))

    def body(h_ref, w_ref, dhn_ref, dres_ref, dx_ref, dw_ref):
        @pl.when(pl.program_id(0) == 0)
        def _():
            dw_ref[...] = jnp.zeros_like(dw_ref)

        _, vjp = jax.vjp(rms, h_ref[...], w_ref[...])
        dh, dw = vjp(dhn_ref[...])
        dx_ref[...] = dh + dres_ref[...]
        dw_ref[...] += dw

    return pl.pallas_call(body, name=name, grid=(t // TOK_TILE,), in_specs=[tile, _full_spec((1, d)), tile, tile],
                          out_specs=(tile, _full_spec((1, d))),
                          out_shape=(jax.ShapeDtypeStruct((t, d), F32), jax.ShapeDtypeStruct((1, d), F32)),
                          compiler_params=_cparams(("arbitrary",)))(h, w, dhn, dres)


def _gate_fwd(name, outs, proj):
    t = proj.shape[0]
    widths = [o.shape[1] for o in outs]
    n = len(outs)

    def body(*refs):
        o_refs, g_ref, og_ref = refs[:n], refs[n], refs[n + 1]
        c = 0
        for o_ref, w in zip(o_refs, widths):
            g = g_ref[:, c:c + w]
            og_ref[:, c:c + w] = (o_ref[...] * (g * sigmoid(g))).astype(BF16)
            c += w

    in_specs = [pl.BlockSpec((TOK_TILE, w), lambda i: (i, 0)) for w in widths] + [pl.BlockSpec((TOK_TILE, 1024), lambda i: (i, 0))]
    return pl.pallas_call(body, name=name, grid=(t // TOK_TILE,), in_specs=in_specs,
                          out_specs=pl.BlockSpec((TOK_TILE, 1024), lambda i: (i, 0)),
                          out_shape=jax.ShapeDtypeStruct((t, 1024), BF16), compiler_params=_cparams(("arbitrary",)))(*outs, proj)


def _gate_bwd(name, outs, proj, dog):
    t = proj.shape[0]
    widths = [o.shape[1] for o in outs]
    n = len(outs)

    def body(*refs):
        o_refs, g_ref, dog_ref = refs[:n], refs[n], refs[n + 1]
        do_refs, dg_ref = refs[n + 2:2 * n + 2], refs[2 * n + 2]
        c = 0
        for o_ref, do_ref, w in zip(o_refs, do_refs, widths):
            g = g_ref[:, c:c + w]
            dog_ = dog_ref[:, c:c + w]
            s = sigmoid(g)
            do_ref[...] = dog_ * (g * s)
            dg_ref[:, c:c + w] = dog_ * o_ref[...] * (s * (1.0 + g * (1.0 - s)))
            c += w

    o_specs = [pl.BlockSpec((TOK_TILE, w), lambda i: (i, 0)) for w in widths]
    wide = pl.BlockSpec((TOK_TILE, 1024), lambda i: (i, 0))
    return pl.pallas_call(body, name=name, grid=(t // TOK_TILE,), in_specs=o_specs + [wide, wide], out_specs=tuple(o_specs) + (wide,),
                          out_shape=tuple(jax.ShapeDtypeStruct((t, w), F32) for w in widths) + (jax.ShapeDtypeStruct((t, 1024), F32),),
                          compiler_params=_cparams(("arbitrary",)))(*outs, proj, dog)


def _top(h1, y1, b_out1, fw, target):
    t, d = h1.shape
    tile = pl.BlockSpec((TOK_TILE, d), lambda i: (i, 0))
    vec = _full_spec((1, d))

    def body(h1_ref, y1_ref, b_ref, fw_ref, tgt_ref, dh2_ref, loss_ref, db_ref, dfw_ref):
        @pl.when(pl.program_id(0) == 0)
        def _():
            loss_ref[...] = jnp.zeros_like(loss_ref)
            db_ref[...] = jnp.zeros_like(db_ref)
            dfw_ref[...] = jnp.zeros_like(dfw_ref)

        tgt = tgt_ref[...]

        def f(h2, w):
            err = rms(h2, w) - tgt
            per_tok = jnp.mean(err * err, axis=-1, keepdims=True)
            return 0.5 * jnp.sum(per_tok, axis=0, keepdims=True)

        h2 = h1_ref[...] + y1_ref[...] + b_ref[...]
        loss, vjp = jax.vjp(f, h2, fw_ref[...])
        dh2, dfw = vjp(jnp.ones((1, 1), F32))
        dh2_ref[...] = dh2
        loss_ref[...] += jnp.broadcast_to(loss, loss_ref.shape)
        db_ref[...] += jnp.sum(dh2, axis=0, keepdims=True)
        dfw_ref[...] += dfw

    return pl.pallas_call(body, name="top_loss", grid=(t // TOK_TILE,), in_specs=[tile, tile, vec, vec, tile],
                          out_specs=(tile, _full_spec((1, LANES)), vec, vec),
                          out_shape=(jax.ShapeDtypeStruct((t, d), F32), jax.ShapeDtypeStruct((1, LANES), F32),
                                     jax.ShapeDtypeStruct((1, d), F32), jax.ShapeDtypeStruct((1, d), F32)),
                          compiler_params=_cparams(("arbitrary",)))(h1, y1, b_out1, fw, target)


def _gla_load(q_ref, k_ref, v_ref, gl_ref, up_ref, bias_ref, nw_ref):
    toks = (q_ref[...], k_ref[...], v_ref[...], gl_ref[...])
    params = (up_ref[...], bias_ref[...], nw_ref[...])
    return toks, params


def _gla_specs(c, n=None):
    toks = [_tok_spec(c, 256, _col(256, "gq", C0), n), _tok_spec(c, 256, _col(256, "gk", C0), n),
            _tok_spec(c, 512, _col(512, "gv", C0), n), _tok_spec(c, 128, _col(128, "glow", C0), n)]
    params = [_full_spec((128, 256)), _full_spec((1, 256)), _full_spec((1, 128))]
    return toks, params


def _gla_fwd(proj0, gk_up, gk_bias, norm_w):
    t = proj0.shape[0]
    c = GLA_CHUNK
    nc = t // c
    toks_s, params_s = _gla_specs(c)

    def body(q_ref, k_ref, v_ref, gl_ref, up_ref, bias_ref, nw_ref, o_ref, st_ref, s_scr):
        @pl.when(pl.program_id(0) == 0)
        def _():
            s_scr[...] = jnp.zeros_like(s_scr)

        st_ref[...] = s_scr[...]
        toks, params = _gla_load(q_ref, k_ref, v_ref, gl_ref, up_ref, bias_ref, nw_ref)
        state = [s_scr[h * GLA_DV:(h + 1) * GLA_DV, :] for h in range(GLA_HEADS)]
        o_ref[...], new = gla_chunk(state, toks, params)
        for h in range(GLA_HEADS):
            s_scr[h * GLA_DV:(h + 1) * GLA_DV, :] = new[h]

    return pl.pallas_call(
        body, name="gla_fwd", grid=(nc,), in_specs=toks_s + params_s,
        out_specs=(_tok_spec(c, 512, 0), pl.BlockSpec((512, GLA_DK), lambda i: (i, 0))),
        out_shape=(jax.ShapeDtypeStruct((t, 512), F32), jax.ShapeDtypeStruct((nc * 512, GLA_DK), F32)),
        scratch_shapes=[pltpu.VMEM((512, GLA_DK), F32)], compiler_params=_cparams(("arbitrary",)))(
            proj0, proj0, proj0, proj0, gk_up, gk_bias, norm_w)


def _gla_bwd(proj0, gk_up, gk_bias, norm_w, states, do):
    t = proj0.shape[0]
    c = GLA_CHUNK
    nc = t // c
    toks_s, params_s = _gla_specs(c, nc)

    def body(q_ref, k_ref, v_ref, gl_ref, up_ref, bias_ref, nw_ref, st_ref, do_ref,
             dq_ref, dk_ref, dv_ref, dgl_ref, dup_ref, dbias_ref, dnw_ref, ds_scr):
        @pl.when(pl.program_id(0) == 0)
        def _():
            ds_scr[...] = jnp.zeros_like(ds_scr)
            dup_ref[...] = jnp.zeros_like(dup_ref)
            dbias_ref[...] = jnp.zeros_like(dbias_ref)
            dnw_ref[...] = jnp.zeros_like(dnw_ref)

        toks, params = _gla_load(q_ref, k_ref, v_ref, gl_ref, up_ref, bias_ref, nw_ref)
        rows = lambda h: slice(h * GLA_DV, (h + 1) * GLA_DV)
        state = [st_ref[rows(h), :] for h in range(GLA_HEADS)]
        _, vjp = jax.vjp(gla_chunk, state, toks, params)
        dstate_in = [ds_scr[rows(h), :] for h in range(GLA_HEADS)]
        dstate, (dq_ref[...], dk_ref[...], dv_ref[...], dgl_ref[...]), (dup, dbias, dnw) = vjp((do_ref[...], dstate_in))
        dup_ref[...] += dup
        dbias_ref[...] += dbias
        dnw_ref[...] += dnw
        for h in range(GLA_HEADS):
            ds_scr[rows(h), :] = dstate[h]

    rev = lambda w: pl.BlockSpec((c, w), lambda i: (nc - 1 - i, 0))
    return pl.pallas_call(
        body, name="gla_bwd", grid=(nc,),
        in_specs=toks_s + params_s + [pl.BlockSpec((512, GLA_DK), lambda i: (nc - 1 - i, 0)), rev(512)],
        out_specs=(rev(256), rev(256), rev(512), rev(128), _full_spec((128, 256)), _full_spec((1, 256)), _full_spec((1, 128))),
        out_shape=(jax.ShapeDtypeStruct((t, 256), F32), jax.ShapeDtypeStruct((t, 256), F32), jax.ShapeDtypeStruct((t, 512), F32),
                   jax.ShapeDtypeStruct((t, 128), F32), jax.ShapeDtypeStruct((128, 256), F32), jax.ShapeDtypeStruct((1, 256), F32),
                   jax.ShapeDtypeStruct((1, 128), F32)),
        scratch_shapes=[pltpu.VMEM((512, GLA_DK), F32)], compiler_params=_cparams(("arbitrary",)))(
            proj0, proj0, proj0, proj0, gk_up, gk_bias, norm_w, states, do)


RWKV_PARAM_SHAPES = [(1, 512), (1, 512), (1, 512), (1, 128), (1, 128), (1, 512), (128, 512), (1, 512), (128, 512),
                     (1, 512), (1, 512), (1, 512), (1, 512), (1, 512)]
PREV_W = 1792
PREV_COLS = [slice(0, 512), slice(512, 1024), slice(1024, 1536), slice(1536, 1664), slice(1664, 1792)]


def _rwkv_load(r_ref, k_ref, v_ref, xw_ref, xa_ref, p_refs):
    toks = (r_ref[...], k_ref[...], v_ref[...], xw_ref[...], xa_ref[...])
    return toks, tuple(p[...] for p in p_refs)


def _rwkv_state(s_ref, prev_ref):
    n = RWKV_N
    S = [s_ref[h * n:(h + 1) * n, :] for h in range(RWKV_HEADS)]
    return (S,) + tuple(prev_ref[0:1, cols] for cols in PREV_COLS)


def _rwkv_put_state(s_ref, prev_ref, state):
    n = RWKV_N
    for h in range(RWKV_HEADS):
        s_ref[h * n:(h + 1) * n, :] = state[0][h]
    for cols, val in zip(PREV_COLS, state[1:]):
        prev_ref[0:1, cols] = val


def _rwkv_specs(c, n=None):
    toks = [_tok_spec(c, 512, _col(512, "r", C0), n), _tok_spec(c, 512, _col(512, "k", C0), n),
            _tok_spec(c, 512, _col(512, "v", C0), n), _tok_spec(c, 128, _col(128, "xw", C0), n),
            _tok_spec(c, 128, _col(128, "xa", C0), n)]
    return toks, [_full_spec(s) for s in RWKV_PARAM_SHAPES]


def _rwkv_fwd(proj0, params, comm, kinds):
    t = proj0.shape[0]
    c = RWKV_CHUNK
    nc = t // c
    toks_s, params_s = _rwkv_specs(c)
    npar, ncomm = len(params), len(comm)

    def body(*refs):
        tok_refs, p_refs = refs[:5], refs[5:5 + npar]
        comm_in = refs[5 + npar:5 + npar + ncomm]
        o_ref, st_ref, pst_ref = refs[5 + npar + ncomm:8 + npar + ncomm]
        comm_out = refs[8 + npar + ncomm:8 + npar + 2 * ncomm]
        s_scr, prev_scr = refs[8 + npar + 2 * ncomm:10 + npar + 2 * ncomm]
        sems = refs[10 + npar + 2 * ncomm:]
        i = pl.program_id(0)

        @pl.when(i == 0)
        def _():
            _comm_start(*_comm_copies(comm_in, comm_out, kinds, *sems))
            s_scr[...] = jnp.zeros_like(s_scr)
            prev_scr[...] = jnp.zeros_like(prev_scr)

        st_ref[...] = s_scr[...]
        pst_ref[...] = prev_scr[...]
        toks, prm = _rwkv_load(*tok_refs, p_refs)
        o_ref[...], new = rwkv_chunk(_rwkv_state(s_scr, prev_scr), toks, prm)
        _rwkv_put_state(s_scr, prev_scr, new)

        @pl.when(i == nc - 1)
        def _():
            _comm_wait(*_comm_copies(comm_in, comm_out, kinds, *sems))

    outs = pl.pallas_call(
        body, name="rwkv_fwd", grid=(nc,), in_specs=toks_s + params_s + [ANY] * ncomm,
        out_specs=[_tok_spec(c, 512, 0), pl.BlockSpec((512, RWKV_N), lambda i: (i, 0)), pl.BlockSpec((8, PREV_W), lambda i: (i, 0))]
        + [ANY] * ncomm,
        out_shape=[jax.ShapeDtypeStruct((t, 512), F32), jax.ShapeDtypeStruct((nc * 512, RWKV_N), F32),
                   jax.ShapeDtypeStruct((nc * 8, PREV_W), F32)] + _comm_out_shapes(comm, kinds),
        scratch_shapes=[pltpu.VMEM((512, RWKV_N), F32), pltpu.VMEM((8, PREV_W), F32)] + _comm_scratch(ncomm),
        compiler_params=_cparams(("arbitrary",)))(proj0, proj0, proj0, proj0, proj0, *params, *comm)
    return outs[0], outs[1], outs[2], outs[3:]


def _rwkv_bwd(proj0, params, states, prevs, do, comm, kinds):
    t = proj0.shape[0]
    c = RWKV_CHUNK
    nc = t // c
    toks_s, params_s = _rwkv_specs(c, nc)
    npar, ncomm = len(params), len(comm)

    def body(*refs):
        tok_refs, p_refs = refs[:5], refs[5:5 + npar]
        st_ref, pst_ref, do_ref = refs[5 + npar:8 + npar]
        comm_in = refs[8 + npar:8 + npar + ncomm]
        outs = refs[8 + npar + ncomm:]
        dtok_refs, dp_refs, comm_out = outs[:5], outs[5:5 + npar], outs[5 + npar:5 + npar + ncomm]
        ds_scr, dprev_scr = outs[5 + npar + ncomm:7 + npar + ncomm]
        sems = outs[7 + npar + ncomm:]
        i = pl.program_id(0)

        @pl.when(i == 0)
        def _():
            _comm_start(*_comm_copies(comm_in, comm_out, kinds, *sems))
            ds_scr[...] = jnp.zeros_like(ds_scr)
            dprev_scr[...] = jnp.zeros_like(dprev_scr)
            for dp in dp_refs:
                dp[...] = jnp.zeros_like(dp)

        toks, prm = _rwkv_load(*tok_refs, p_refs)
        _, vjp = jax.vjp(rwkv_chunk, _rwkv_state(st_ref, pst_ref), toks, prm)
        dstate, dtoks, dprm = vjp((do_ref[...], _rwkv_state(ds_scr, dprev_scr)))
        for ref, val in zip(dtok_refs, dtoks):
            ref[...] = val
        for ref, val in zip(dp_refs, dprm):
            ref[...] += val
        _rwkv_put_state(ds_scr, dprev_scr, dstate)

        @pl.when(i == nc - 1)
        def _():
            _comm_wait(*_comm_copies(comm_in, comm_out, kinds, *sems))

    rev = lambda w: pl.BlockSpec((c, w), lambda i: (nc - 1 - i, 0))
    outs = pl.pallas_call(
        body, name="rwkv_bwd", grid=(nc,),
        in_specs=toks_s + params_s + [pl.BlockSpec((512, RWKV_N), lambda i: (nc - 1 - i, 0)),
                                      pl.BlockSpec((8, PREV_W), lambda i: (nc - 1 - i, 0)), rev(512)] + [ANY] * ncomm,
        out_specs=[rev(512), rev(512), rev(512), rev(128), rev(128)] + params_s + [ANY] * ncomm,
        out_shape=[jax.ShapeDtypeStruct((t, w), F32) for w in (512, 512, 512, 128, 128)]
        + [jax.ShapeDtypeStruct(s, F32) for s in RWKV_PARAM_SHAPES] + _comm_out_shapes(comm, kinds),
        scratch_shapes=[pltpu.VMEM((512, RWKV_N), F32), pltpu.VMEM((8, PREV_W), F32)] + _comm_scratch(ncomm),
        compiler_params=_cparams(("arbitrary",)))(proj0, proj0, proj0, proj0, proj0, *params, states, prevs, do, *comm)
    return outs[:5], outs[5:5 + npar], outs[5 + npar:]


def _swa_load(q_ref, k_ref, v_ref, cos_ref, sin_ref, bq_ref, bk_ref, bv_ref, sk_ref):
    toks = (_heads(q_ref, 16, SWA_HD), _heads(k_ref, 4, SWA_HD), _heads(v_ref, 4, SWA_HD), cos_ref[...], sin_ref[...])
    params = (_heads(bq_ref, 16, SWA_HD), _heads(bk_ref, 4, SWA_HD), _heads(bv_ref, 4, SWA_HD), _heads(sk_ref, 16, 1))
    return toks, params


def _swa_specs(c, n=None):
    toks = [_tok_spec(c, 1024, _col(1024, "q", C1), n), _tok_spec(c, 256, _col(256, "k", C1), n),
            _tok_spec(c, 256, _col(256, "v", C1), n), _tok_spec(c, SWA_HD, 0, n), _tok_spec(c, SWA_HD, 0, n)]
    params = [_full_spec((1, 1024)), _full_spec((1, 256)), _full_spec((1, 256)), _full_spec((1, 16))]
    return toks, params


def _swa_fwd(proj1, cos, sin, bq, bk, bv, sinks):
    t = proj1.shape[0]
    c = WINDOW
    nb = t // c
    toks_s, params_s = _swa_specs(c)

    def body(q_ref, k_ref, v_ref, cos_ref, sin_ref, bq_ref, bk_ref, bv_ref, sk_ref, o_ref, kst_ref, vst_ref, k_scr, v_scr):
        first = pl.program_id(0) == 0

        @pl.when(first)
        def _():
            k_scr[...] = jnp.zeros_like(k_scr)
            v_scr[...] = jnp.zeros_like(v_scr)

        kst_ref[...] = k_scr[...]
        vst_ref[...] = v_scr[...]
        toks, params = _swa_load(q_ref, k_ref, v_ref, cos_ref, sin_ref, bq_ref, bk_ref, bv_ref, sk_ref)
        outs, (kn, vn) = swa_chunk((_heads(k_scr, 4, SWA_HD), _heads(v_scr, 4, SWA_HD)), toks, params, first)
        _put_heads(o_ref, outs, SWA_HD)
        _put_heads(k_scr, kn, SWA_HD)
        _put_heads(v_scr, vn, SWA_HD)

    return pl.pallas_call(
        body, name="swa_fwd", grid=(nb,), in_specs=toks_s + params_s,
        out_specs=(_tok_spec(c, 1024, 0), _tok_spec(c, 256, 0), _tok_spec(c, 256, 0)),
        out_shape=(jax.ShapeDtypeStruct((t, 1024), F32), jax.ShapeDtypeStruct((t, 256), F32), jax.ShapeDtypeStruct((t, 256), F32)),
        scratch_shapes=[pltpu.VMEM((c, 256), F32), pltpu.VMEM((c, 256), F32)],
        compiler_params=_cparams(("arbitrary",)))(proj1, proj1, proj1, cos, sin, bq, bk, bv, sinks)


def _swa_bwd(proj1, cos, sin, bq, bk, bv, sinks, kst, vst, do):
    t = proj1.shape[0]
    c = WINDOW
    nb = t // c
    toks_s, params_s = _swa_specs(c, nb)

    def body(q_ref, k_ref, v_ref, cos_ref, sin_ref, bq_ref, bk_ref, bv_ref, sk_ref, kst_ref, vst_ref, do_ref,
             dq_ref, dk_ref, dv_ref, dbq_ref, dbk_ref, dbv_ref, dsk_ref, dk_scr, dv_scr):
        i = pl.program_id(0)

        @pl.when(i == 0)
        def _():
            dk_scr[...] = jnp.zeros_like(dk_scr)
            dv_scr[...] = jnp.zeros_like(dv_scr)
            for ref in (dbq_ref, dbk_ref, dbv_ref, dsk_ref):
                ref[...] = jnp.zeros_like(ref)

        first = i == nb - 1
        toks, params = _swa_load(q_ref, k_ref, v_ref, cos_ref, sin_ref, bq_ref, bk_ref, bv_ref, sk_ref)
        f = functools.partial(swa_chunk, first=first)
        _, vjp = jax.vjp(f, (_heads(kst_ref, 4, SWA_HD), _heads(vst_ref, 4, SWA_HD)), toks, params)
        dstate_in = (_heads(dk_scr, 4, SWA_HD), _heads(dv_scr, 4, SWA_HD))
        (dkp, dvp), (dq, dk, dv, _, _), (dbq, dbk, dbv, dsk) = vjp((_heads(do_ref, 16, SWA_HD), dstate_in))
        _put_heads(dq_ref, dq, SWA_HD)
        _put_heads(dk_ref, dk, SWA_HD)
        _put_heads(dv_ref, dv, SWA_HD)
        _put_heads(dbq_ref, dbq, SWA_HD, add=True)
        _put_heads(dbk_ref, dbk, SWA_HD, add=True)
        _put_heads(dbv_ref, dbv, SWA_HD, add=True)
        _put_heads(dsk_ref, dsk, 1, add=True)
        _put_heads(dk_scr, dkp, SWA_HD)
        _put_heads(dv_scr, dvp, SWA_HD)

    rev = lambda w: pl.BlockSpec((c, w), lambda i: (nb - 1 - i, 0))
    return pl.pallas_call(
        body, name="swa_bwd", grid=(nb,), in_specs=toks_s + params_s + [rev(256), rev(256), rev(1024)],
        out_specs=(rev(1024), rev(256), rev(256), _full_spec((1, 1024)), _full_spec((1, 256)), _full_spec((1, 256)), _full_spec((1, 16))),
        out_shape=(jax.ShapeDtypeStruct((t, 1024), F32), jax.ShapeDtypeStruct((t, 256), F32), jax.ShapeDtypeStruct((t, 256), F32),
                   jax.ShapeDtypeStruct((1, 1024), F32), jax.ShapeDtypeStruct((1, 256), F32), jax.ShapeDtypeStruct((1, 256), F32),
                   jax.ShapeDtypeStruct((1, 16), F32)),
        scratch_shapes=[pltpu.VMEM((c, 256), F32), pltpu.VMEM((c, 256), F32)],
        compiler_params=_cparams(("arbitrary",)))(proj1, proj1, proj1, cos, sin, bq, bk, bv, sinks, kst, vst, do)


MESH = pl.DeviceIdType.MESH
ANY = pl.BlockSpec(memory_space=pl.ANY)


def _my_place():
    return lax.axis_index("x"), lax.axis_index("y"), lax.axis_index("c")


def _all_gather(shards):
    n = len(shards)

    def body(*refs):
        in_refs, out_refs = refs[:n], refs[n:2 * n]
        send_sems, recv_sems, local_sems = refs[2 * n:]
        x, y, c = _my_place()
        me, sibling = (x, y, c), (x, y, 1 - c)
        chips = [(1 - x, y), (x, 1 - y), (1 - x, 1 - y)]

        def slot(out_ref, place):
            px, py, pc = place
            return out_ref.at[4 * px + 2 * py + pc]

        def copy(a, k, block, to, src=None):
            return pltpu.make_async_remote_copy(
                src_ref=slot(out_refs[a], block) if src is None else src, dst_ref=slot(out_refs[a], block),
                send_sem=send_sems.at[a, k], recv_sem=recv_sems.at[a, k], device_id=to, device_id_type=MESH)

        mine = [pltpu.make_async_copy(in_refs[a], slot(out_refs[a], me), local_sems.at[a]) for a in range(n)]
        for cp in mine:
            cp.start()
        first = []
        for a in range(n):
            first.append(copy(a, 0, me, sibling, src=in_refs[a]))
            first += [copy(a, 1 + j, me, (*chip, c), src=in_refs[a]) for j, chip in enumerate(chips)]
        for cp in first:
            cp.start()
        passed = []
        for j, chip in enumerate(chips):
            for a in range(n):
                copy(a, 1 + j, (*chip, c), me).wait_recv()
                fwd = copy(a, 4 + j, (*chip, c), sibling)
                fwd.start()
                passed.append(fwd)
        for a in range(n):
            copy(a, 0, sibling, me).wait_recv()
            for j, chip in enumerate(chips):
                copy(a, 4 + j, (*chip, 1 - c), me).wait_recv()
        for cp in first + passed:
            cp.wait_send()
        for cp in mine:
            cp.wait()

    return pl.pallas_call(
        body, name="all_gather_weights", in_specs=[ANY] * n, out_specs=[ANY] * n,
        out_shape=[jax.ShapeDtypeStruct((N_DEV,) + s.shape, s.dtype) for s in shards],
        scratch_shapes=_comm_scratch(n))(*shards)


def _comm_copies(in_refs, out_refs, kinds, send_sems, recv_sems, local_sems):
    x, y, c = _my_place()
    my_idx = 4 * x + 2 * y + c
    src = lambda a, idx: in_refs[a] if kinds[a] == "gather" else in_refs[a].at[idx]
    local = [pltpu.make_async_copy(src(a, my_idx), out_refs[a].at[my_idx], local_sems.at[a]) for a in range(len(kinds))]
    remote = []
    for rel in range(1, N_DEV):
        px, py, pc = x ^ ((rel >> 2) & 1), y ^ ((rel >> 1) & 1), c ^ (rel & 1)
        for a in range(len(kinds)):
            remote.append(pltpu.make_async_remote_copy(
                src_ref=src(a, 4 * px + 2 * py + pc), dst_ref=out_refs[a].at[my_idx], send_sem=send_sems.at[a, rel - 1],
                recv_sem=recv_sems.at[a, rel - 1], device_id=(px, py, pc), device_id_type=MESH))
    return local, remote


def _comm_start(local, remote):
    for cp in local + remote:
        cp.start()


def _comm_wait(local, remote):
    for cp in remote:
        cp.wait_recv()
    for cp in remote:
        cp.wait_send()
    for cp in local:
        cp.wait()


def _comm_out_shapes(arrays, kinds):
    return [jax.ShapeDtypeStruct(((N_DEV,) + a.shape) if k == "gather" else a.shape, a.dtype) for a, k in zip(arrays, kinds)]


def _comm_scratch(n):
    return [pltpu.SemaphoreType.DMA((n, N_DEV - 1)), pltpu.SemaphoreType.DMA((n, N_DEV - 1)), pltpu.SemaphoreType.DMA((n,))]


def _exchange(arrays, kinds):
    n = len(arrays)

    def body(*refs):
        copies = _comm_copies(refs[:n], refs[n:2 * n], kinds, *refs[2 * n:])
        _comm_start(*copies)
        _comm_wait(*copies)

    return pl.pallas_call(body, name="exchange_grads", in_specs=[ANY] * n, out_specs=[ANY] * n,
                          out_shape=_comm_out_shapes(arrays, kinds), scratch_shapes=_comm_scratch(n))(*arrays)


def _adam_math(w, g, m, v):
    m = ADAM_B1 * m + (1.0 - ADAM_B1) * g
    v = ADAM_B2 * v + (1.0 - ADAM_B2) * (g * g)
    m_hat = m / (1.0 - ADAM_B1 ** ADAM_STEP)
    v_hat = v / (1.0 - ADAM_B2 ** ADAM_STEP)
    delta = -ADAM_LR * (m_hat / (jnp.sqrt(v_hat) + ADAM_EPS) + ADAM_WD * w)
    return delta, m, v


def _adamw(name, w, gslots, m, v, tc):
    r, cc = w.shape
    assert cc % tc == 0
    tile = pl.BlockSpec((r, tc), lambda i: (0, i))

    def body(w_ref, g_ref, m_ref, v_ref, go_ref, d_ref, mo_ref, vo_ref):
        g = g_ref[0].astype(F32)
        for s in range(1, N_DEV):
            g = g + g_ref[s].astype(F32)
        d, mn, vn = _adam_math(w_ref[...], g, m_ref[...], v_ref[...])
        go_ref[...] = g
        d_ref[...] = d
        mo_ref[...] = mn
        vo_ref[...] = vn

    shp = jax.ShapeDtypeStruct((r, cc), F32)
    return pl.pallas_call(body, name=name, grid=(cc // tc,),
                          in_specs=[tile, pl.BlockSpec((N_DEV, r, tc), lambda i: (0, 0, i)), tile, tile],
                          out_specs=(tile,) * 4, out_shape=(shp,) * 4, compiler_params=_cparams(("arbitrary",)))(w, gslots, m, v)


PACK_TILE = 8 * LANES


def _packed_rows(shape, mode):
    r, w = shape
    return -(-r // 8) * 8 if mode == "rows" else -(-(r * w) // PACK_TILE) * 8


def _pack_small(arrays, modes, lead=False):
    out = []
    for a, mode in zip(arrays, modes):
        a = a.astype(F32) if lead else a.astype(F32)[None]
        if mode == "rows":
            out.append(jnp.pad(a, ((0, 0), (0, (-a.shape[1]) % 8), (0, LANES - a.shape[2]))))
        else:
            flat = a.reshape(a.shape[0], -1)
            out.append(jnp.pad(flat, ((0, 0), (0, (-flat.shape[1]) % PACK_TILE))).reshape(a.shape[0], -1, LANES))
    out = jnp.concatenate(out, axis=1)
    return out if lead else out[0]


def _take_small(packed, row0, shape, mode):
    r, w = shape
    lead = packed.ndim == 3
    if mode == "rows":
        return packed[:, row0:row0 + r, :w] if lead else packed[row0:row0 + r, :w]
    per_row = -(-w // LANES)
    if lead:
        return packed[:, row0:row0 + r * per_row].reshape(packed.shape[0], r, per_row * LANES)[:, :, :w]
    rows = []
    for i in range(r):
        pieces = [packed[row0 + i * per_row + j:row0 + i * per_row + j + 1, :] for j in range(per_row)]
        rows.append((pieces[0] if per_row == 1 else jnp.concatenate(pieces, axis=1))[:, :w])
    return rows[0] if r == 1 else jnp.concatenate(rows, axis=0)


def _adamw_small(slots, specs, ws, ms, vs, loss_row):
    n = len(specs)

    def body(*refs):
        slots_ref, w_refs, m_refs, v_refs = refs[0], refs[1:1 + n], refs[1 + n:1 + 2 * n], refs[1 + 2 * n:1 + 3 * n]
        out_refs, loss_ref = refs[1 + 3 * n:1 + 7 * n], refs[1 + 7 * n]
        gp = slots_ref[0]
        for s in range(1, N_DEV):
            gp = gp + slots_ref[s]
        read = lambda ref: ref[0] if len(ref.shape) == 3 else ref[...]
        for k, (shape, mode, row0) in enumerate(specs):
            g = _take_small(gp, row0, shape, mode)
            d, mn, vn = _adam_math(read(w_refs[k]), g, read(m_refs[k]), read(v_refs[k]))
            for ref, val in zip(out_refs[4 * k:4 * k + 4], (g, d, mn, vn)):
                if len(ref.shape) == 3:
                    ref[0] = val
                else:
                    ref[...] = val
        loss_ref[...] = gp[loss_row:loss_row + 1, :]

    vmem = pl.BlockSpec(memory_space=pltpu.VMEM)
    out_shape = [jax.ShapeDtypeStruct(w.shape, F32) for w in ws for _ in range(4)] + [jax.ShapeDtypeStruct((1, LANES), F32)]
    outs = pl.pallas_call(body, name="adamw_small", in_specs=[vmem] * (1 + 3 * n), out_specs=[vmem] * (4 * n + 1),
                          out_shape=out_shape)(slots, *ws, *ms, *vs)
    return [outs[4 * k:4 * k + 4] for k in range(n)], outs[4 * n]


def _rope_tables(t):
    half = 8
    inv_freq = ROPE_THETA ** (-jnp.arange(half, dtype=F32) / half)
    ang = jnp.arange(t, dtype=F32)[:, None] * inv_freq
    cos = jnp.concatenate([jnp.cos(ang), jnp.cos(ang), jnp.ones((t, SWA_HD - 16), F32)], axis=1)
    sin = jnp.concatenate([jnp.sin(ang), jnp.sin(ang), jnp.zeros((t, SWA_HD - 16), F32)], axis=1)
    return cos, sin


def _pad_to(a, rows=None, cols=None):
    r = 0 if rows is None else rows - a.shape[0]
    c = 0 if cols is None else cols - a.shape[1]
    return jnp.pad(a, ((0, r), (0, c)))


ORIG0 = dict(gq=(0, 256), gk=(256, 256), gv=(512, 512), glow=(1024, 16), r=(1040, 512), k=(1552, 512), v=(2064, 512),
             xw=(2576, 64), xa=(2640, 64), gate=(2704, 1024))
ORIG0_ORDER = ["gq", "gk", "gv", "glow", "r", "k", "v", "xw", "xa", "gate"]


def _w0t_to_padded(wt):
    rows, at = [], 0
    for name, (off, width) in sorted(C0.items(), key=lambda kv: kv[1][0]):
        assert off == at
        src, src_w = ORIG0[name]
        rows.append(_pad_to(wt[src:src + src_w], rows=width))
        at += width
    rows.append(jnp.zeros((N0P - at, wt.shape[1]), wt.dtype))
    return jnp.concatenate(rows, axis=0)


def _w0t_from_padded(wpt):
    return jnp.concatenate([wpt[C0[n][0]:C0[n][0] + ORIG0[n][1]] for n in ORIG0_ORDER], axis=0)


def _w1t_to_mine(wt):
    return jnp.concatenate([wt[1536:2560], wt[:1536]], axis=0)


def _w1t_from_mine(wt):
    return jnp.concatenate([wt[1024:2560], wt[:1024]], axis=0)


def kernel(x, norm_w, w_in0, gla_gk_up, gla_gk_bias, gla_norm_w, rwkv_mu, rwkv_w0, rwkv_w_up, rwkv_a0, rwkv_a_up, rwkv_k_k, rwkv_k_a, rwkv_r_k, rwkv_ln_w, rwkv_ln_b, w_out0, w_in1, b_in1, attn_sinks, w_out1, b_out1, final_norm_w, loss_target, m_norm_w, m_w_in0, m_gla_gk_up, m_gla_gk_bias, m_gla_norm_w, m_rwkv_mu, m_rwkv_w0, m_rwkv_w_up, m_rwkv_a0, m_rwkv_a_up, m_rwkv_k_k, m_rwkv_k_a, m_rwkv_r_k, m_rwkv_ln_w, m_rwkv_ln_b, m_w_out0, m_w_in1, m_b_in1, m_attn_sinks, m_w_out1, m_b_out1, m_final_norm_w, v_norm_w, v_w_in0, v_gla_gk_up, v_gla_gk_bias, v_gla_norm_w, v_rwkv_mu, v_rwkv_w0, v_rwkv_w_up, v_rwkv_a0, v_rwkv_a_up, v_rwkv_k_k, v_rwkv_k_a, v_rwkv_r_k, v_rwkv_ln_w, v_rwkv_ln_b, v_w_out0, v_w_in1, v_b_in1, v_attn_sinks, v_w_out1, v_b_out1, v_final_norm_w):
    weights = dict(norm_w=norm_w, w_in0=w_in0, gla_gk_up=gla_gk_up, gla_gk_bias=gla_gk_bias, gla_norm_w=gla_norm_w, rwkv_mu=rwkv_mu,
                   rwkv_w0=rwkv_w0, rwkv_w_up=rwkv_w_up, rwkv_a0=rwkv_a0, rwkv_a_up=rwkv_a_up, rwkv_k_k=rwkv_k_k, rwkv_k_a=rwkv_k_a,
                   rwkv_r_k=rwkv_r_k, rwkv_ln_w=rwkv_ln_w, rwkv_ln_b=rwkv_ln_b, w_out0=w_out0, w_in1=w_in1, b_in1=b_in1,
                   attn_sinks=attn_sinks, w_out1=w_out1, b_out1=b_out1, final_norm_w=final_norm_w)
    moms = dict(norm_w=m_norm_w, w_in0=m_w_in0, gla_gk_up=m_gla_gk_up, gla_gk_bias=m_gla_gk_bias, gla_norm_w=m_gla_norm_w,
                rwkv_mu=m_rwkv_mu, rwkv_w0=m_rwkv_w0, rwkv_w_up=m_rwkv_w_up, rwkv_a0=m_rwkv_a0, rwkv_a_up=m_rwkv_a_up,
                rwkv_k_k=m_rwkv_k_k, rwkv_k_a=m_rwkv_k_a, rwkv_r_k=m_rwkv_r_k, rwkv_ln_w=m_rwkv_ln_w, rwkv_ln_b=m_rwkv_ln_b,
                w_out0=m_w_out0, w_in1=m_w_in1, b_in1=m_b_in1, attn_sinks=m_attn_sinks, w_out1=m_w_out1, b_out1=m_b_out1,
                final_norm_w=m_final_norm_w)
    vars_ = dict(norm_w=v_norm_w, w_in0=v_w_in0, gla_gk_up=v_gla_gk_up, gla_gk_bias=v_gla_gk_bias, gla_norm_w=v_gla_norm_w,
                 rwkv_mu=v_rwkv_mu, rwkv_w0=v_rwkv_w0, rwkv_w_up=v_rwkv_w_up, rwkv_a0=v_rwkv_a0, rwkv_a_up=v_rwkv_a_up,
                 rwkv_k_k=v_rwkv_k_k, rwkv_k_a=v_rwkv_k_a, rwkv_r_k=v_rwkv_r_k, rwkv_ln_w=v_rwkv_ln_w, rwkv_ln_b=v_rwkv_ln_b,
                 w_out0=v_w_out0, w_in1=v_w_in1, b_in1=v_b_in1, attn_sinks=v_attn_sinks, w_out1=v_w_out1, b_out1=v_b_out1,
                 final_norm_w=v_final_norm_w)
    names = list(weights)
    big = ["w_in0", "w_out0", "w_in1", "w_out1"]
    small_sharded = ["gla_gk_up", "rwkv_w_up", "rwkv_a_up", "b_in1", "b_out1"]
    replicated = [n for n in names if n not in big and n not in small_sharded]

    xs = x[0]
    tgt = loss_target[0]
    t = xs.shape[0]

    def view(w):
        shape = tuple(w.shape[-2:]) if w.ndim >= 2 else (1, w.shape[0])
        return shape, ("rows" if shape[0] > 1 and shape[1] <= LANES else "flat")

    def layout(ns, row0=0):
        specs = []
        for n in ns:
            shape, mode = view(weights[n])
            specs.append((shape, mode, row0))
            row0 += _packed_rows(shape, mode)
        return specs, row0

    sh_specs, n_shard_rows = layout(small_sharded)
    rep_specs, loss_row = layout(replicated, n_shard_rows)
    sh_modes, rep_modes = [s[1] for s in sh_specs], [s[1] for s in rep_specs]

    small_shard_pack = _pack_small([weights[n].reshape(view(weights[n])[0]) for n in small_sharded], sh_modes)
    g_in0, g_small = _all_gather([w_in0[0].T.astype(BF16), small_shard_pack])
    w0t = _w0t_to_padded(g_in0.reshape(-1, D_MODEL))
    later_shards = [w_out0[0].astype(BF16), w_in1[0].T.astype(BF16), w_out1[0].astype(BF16)]
    gs = [_take_small(g_small, row0, shape, mode) for shape, mode, row0 in sh_specs]
    join_cols = lambda a: jnp.transpose(a, (1, 0, 2)).reshape(a.shape[1], -1)
    gk_up, w_up, a_up = join_cols(gs[0]), join_cols(gs[1]), join_cols(gs[2])
    b_in, b_out = gs[3].reshape(1, -1), gs[4].reshape(1, -1)

    gk_up_p = _pad_to(gk_up, rows=128)
    mu = rwkv_mu
    rwkv_params = [mu[:, 0:512], mu[:, 512:1024], mu[:, 1024:1536], _pad_to(mu[:, 1536:1600], cols=128), _pad_to(mu[:, 1600:1664], cols=128),
                   rwkv_w0, _pad_to(w_up, rows=128), rwkv_a0, _pad_to(a_up, rows=128), rwkv_k_k, rwkv_k_a, rwkv_r_k.reshape(1, 512),
                   rwkv_ln_w, rwkv_ln_b]
    bq, bk, bv = b_in[:, :1024], b_in[:, 1024:1280], b_in[:, 1280:1536]
    cos, sin = _rope_tables(t)
    nw0, nw1, fw = norm_w[0:1], norm_w[1:2], final_norm_w.reshape(1, D_MODEL)

    hn0 = _norm_fwd("norm0_fwd", xs, nw0)
    proj0 = _matmul("proj0", hn0, w0t, "nt", 512, 1024)
    o_a, gla_states = _gla_fwd(proj0, gk_up_p, gla_gk_bias, gla_norm_w)
    o_b, rwkv_states, rwkv_prevs, (g_out0, g_in1, g_out1) = _rwkv_fwd(proj0, rwkv_params, later_shards, ["gather"] * 3)
    wo0 = g_out0.reshape(1024, D_MODEL)
    w1t = _w1t_to_mine(g_in1.reshape(-1, D_MODEL))
    wo1 = g_out1.reshape(1024, D_MODEL)
    og0 = _gate_fwd("gate0_fwd", [o_a, o_b], proj0)
    y0 = _matmul("out0", og0, wo0, "nn", 512, 1024)
    h1, hn1 = _norm_fwd("norm1_fwd", xs, nw1, y0)
    proj1 = _matmul("proj1", hn1, w1t, "nt", 512, 1280)
    o_c, kst, vst = _swa_fwd(proj1, cos, sin, bq, bk, bv, attn_sinks)
    og1 = _gate_fwd("gate1_fwd", [o_c], proj1)
    y1 = _matmul("out1", og1, wo1, "nn", 512, 1024)
    dh2, loss_part, d_b_out, d_fw = _top(h1, y1, b_out, fw, tgt)

    dog1 = _matmul("out1_dx", dh2, wo1, "nt", 512, 1024)
    d_wo1 = _matmul("out1_dw", og1, dh2, "tn", 512, 512)
    d_oc, d_gate1 = _gate_bwd("gate1_bwd", [o_c], proj1, dog1)
    dq, dk, dv, d_bq, d_bk, d_bv, d_sinks = _swa_bwd(proj1, cos, sin, bq, bk, bv, attn_sinks, kst, vst, d_oc)
    dproj1 = jnp.concatenate([d_gate1, dq, dk, dv], axis=1).astype(BF16)
    dhn1 = _matmul("proj1_dx", dproj1, w1t, "nn", 512, 1024)
    d_w1t = _matmul("proj1_dw", dproj1, hn1, "tn", 512, 1024)
    dh1, d_nw1 = _norm_bwd("norm1_bwd", h1, nw1, dhn1, dh2)
    dog0 = _matmul("out0_dx", dh1, wo0, "nt", 512, 1024)
    d_wo0 = _matmul("out0_dw", og0, dh1, "tn", 512, 512)
    d_oa, d_ob, d_gate0 = _gate_bwd("gate0_bwd", [o_a, o_b], proj0, dog0)
    dgq, dgk, dgv, dglow, d_gk_up, d_gk_bias, d_gla_nw = _gla_bwd(proj0, gk_up_p, gla_gk_bias, gla_norm_w, gla_states, d_oa)
    row_blocks = lambda a: a.astype(BF16).reshape(N_DEV, -1, D_MODEL)
    early = [row_blocks(_w1t_from_mine(d_w1t)), row_blocks(d_wo1), row_blocks(d_wo0)]
    (dr, dkk, dvv, dxw, dxa), d_rp, (r_in1, r_out1, r_out0) = _rwkv_bwd(
        proj0, rwkv_params, rwkv_states, rwkv_prevs, d_ob, early, ["scatter"] * 3)
    dproj0 = jnp.concatenate([d_gate0, dgv, dr, dkk, dvv, dgq, dgk, dglow, dxw, dxa, jnp.zeros((t, 128), F32)], axis=1).astype(BF16)
    dhn0 = _matmul("proj0_dx", dproj0, w0t, "nn", 512, 1024)
    d_w0t = _matmul("proj0_dw", dproj0, hn0, "tn", 512, 1024)
    grad_x, d_nw0 = _norm_bwd("norm0_bwd", xs, nw0, dhn0, dh1)

    contrib = dict(
        norm_w=jnp.concatenate([d_nw0, d_nw1], axis=0), gla_gk_bias=d_gk_bias, gla_norm_w=d_gla_nw,
        rwkv_mu=jnp.concatenate([d_rp[0], d_rp[1], d_rp[2], d_rp[3][:, :64], d_rp[4][:, :64]], axis=1),
        rwkv_w0=d_rp[5], rwkv_a0=d_rp[7], rwkv_k_k=d_rp[9], rwkv_k_a=d_rp[10], rwkv_r_k=d_rp[11].reshape(RWKV_HEADS, RWKV_N),
        rwkv_ln_w=d_rp[12], rwkv_ln_b=d_rp[13], attn_sinks=d_sinks, final_norm_w=d_fw)
    rep_pack = _pack_small([contrib[n] for n in replicated] + [loss_part[:, :1]], rep_modes + ["flat"])

    d_w0 = _w0t_from_padded(d_w0t)
    d_b_in = jnp.concatenate([d_bq, d_bk, d_bv], axis=1)
    full_small = [d_gk_up[:16], d_rp[6][:64], d_rp[8][:64], d_b_in, d_b_out]
    split_cols = lambda a: jnp.transpose(a.reshape(a.shape[0], N_DEV, -1), (1, 0, 2))
    small_parts = [split_cols(a) for a in full_small]
    small_pack = _pack_small(small_parts, sh_modes, lead=True)
    r_in0, r_small, r_rep = _exchange([row_blocks(d_w0), small_pack, rep_pack], ["scatter", "scatter", "gather"])

    res = {}
    res["w_in0"] = tuple(a.T[None] for a in _adamw("adamw_w_in0", w_in0[0].T, r_in0, m_w_in0[0].T, v_w_in0[0].T, 256))
    res["w_out0"] = tuple(a[None] for a in _adamw("adamw_w_out0", w_out0[0], r_out0, m_w_out0[0], v_w_out0[0], 256))
    res["w_in1"] = tuple(a.T[None] for a in _adamw("adamw_w_in1", w_in1[0].T, r_in1, m_w_in1[0].T, v_w_in1[0].T, 256))
    res["w_out1"] = tuple(a[None] for a in _adamw("adamw_w_out1", w_out1[0], r_out1, m_w_out1[0], v_w_out1[0], 256))
    small_names = small_sharded + replicated
    slots = jnp.concatenate([r_small, r_rep], axis=1)
    as_2d = lambda a: a.reshape(1, -1) if a.ndim == 1 else a
    small_res, loss_row_out = _adamw_small(slots, sh_specs + rep_specs, [as_2d(weights[n]) for n in small_names],
                                           [as_2d(moms[n]) for n in small_names], [as_2d(vars_[n]) for n in small_names], loss_row)
    for n, vals in zip(small_names, small_res):
        res[n] = tuple(val.reshape(weights[n].shape) for val in vals)
    loss = loss_row_out[0, 0]
    return (loss, grad_x[None], *[res[n][0] for n in names], *[res[n][1] for n in names],
            *[res[n][2] for n in names], *[res[n][3] for n in names])
```

```python
import functools

import jax
import jax.numpy as jnp
from jax import lax
from jax.experimental import pallas as pl
from jax.experimental.pallas import tpu as pltpu

F32 = jnp.float32
BF16 = jnp.bfloat16
HI = lax.Precision.HIGHEST

D_MODEL = 1024
NORM_EPS = 1e-5
GLA_HEADS, GLA_DK, GLA_DV = 4, 64, 128
GLA_NORMALIZER = 16.0
GLA_CHUNK = 64
GLA_STEP = 256
RWKV_HEADS, RWKV_N = 8, 64
RWKV_LN_EPS = 64e-5
RWKV_CHUNK = 128
SWA_Q_HEADS, SWA_KV_HEADS, SWA_GROUP, SWA_HD = 16, 4, 4, 64
WINDOW = 128
SWA_STEP = 256
ROPE_THETA = 500000.0
NEG = -1e30
N_DEV = 8
LANES = 128

ADAM_LR, ADAM_B1, ADAM_B2, ADAM_EPS, ADAM_WD, ADAM_STEP = 0.001, 0.9, 0.999, 1e-08, 0.01, 10

N0P = 4096
C0 = dict(gate=(0, 1024), gv=(1024, 512), r=(1536, 512), k=(2048, 512), v=(2560, 512), gq=(3072, 256), gk=(3328, 256),
          glow=(3584, 128), xw=(3712, 128), xa=(3840, 128))
N1P = 2560
C1 = dict(gate=(0, 1024), q=(1024, 1024), k=(2048, 256), v=(2304, 256))

VMEM_LIMIT = 56 * 1024 * 1024

P_LORA = 1
P_GLA = 1
P_RWKV_G = 2
P_RWKV = 1
P_SWA = 1
P_ROPE = 3


def _cparams(sem=None):
    return pltpu.CompilerParams(dimension_semantics=sem, vmem_limit_bytes=VMEM_LIMIT)


DIMS = dict(nn=(((1,), (0,)), ((), ())), nt=(((1,), (1,)), ((), ())), tn=(((0,), (0,)), ((), ())))


def _split_bf16(a):
    hi = a.astype(BF16)
    return hi, (a - hi.astype(F32)).astype(BF16)


def _dot(a, b, mode, passes):
    dg = lambda p, q: lax.dot_general(p, q, DIMS[mode], preferred_element_type=F32)
    if passes == 1:
        return dg(a.astype(BF16), b.astype(BF16))
    if passes == 2:
        ah, (bh, bl) = a.astype(BF16), _split_bf16(b)
        return dg(ah, bh) + dg(ah, bl)
    if passes == 3:
        (ah, al), (bh, bl) = _split_bf16(a), _split_bf16(b)
        return dg(ah, bh) + dg(al, bh) + dg(ah, bl)
    return lax.dot_general(a, b, DIMS[mode], precision=HI, preferred_element_type=F32)


@functools.partial(jax.custom_vjp, nondiff_argnums=(2, 3))
def mmx(a, b, mode, passes):
    return _dot(a, b, mode, passes)


def _mmx_fwd(a, b, mode, passes):
    return _dot(a, b, mode, passes), (a, b)


def _mmx_bwd(mode, passes, res, g):
    a, b = res
    if mode == "nn":
        return _dot(g, b, "nt", passes), _dot(a, g, "tn", passes)
    if mode == "nt":
        return _dot(g, b, "nn", passes), _dot(g, a, "tn", passes)
    return _dot(b, g, "nt", passes), _dot(a, g, "nn", passes)


mmx.defvjp(_mmx_fwd, _mmx_bwd)


def _tri_dot(tri, x):
    t = tri.astype(BF16)
    x1 = x.astype(BF16)
    r1 = x - x1.astype(F32)
    x2 = r1.astype(BF16)
    x3 = (r1 - x2.astype(F32)).astype(BF16)
    dg = lambda q: jnp.dot(t, q, preferred_element_type=F32)
    return dg(x1) + dg(x2) + dg(x3)


@jax.custom_vjp
def cumsum_rows(x):
    return _tri_dot(tril_ones(x.shape[0]), x)


def _cumsum_fwd(x):
    return cumsum_rows(x), None


def _cumsum_bwd(_, g):
    i, j = _iota2(g.shape[0], g.shape[0])
    return (_tri_dot(jnp.where(i <= j, 1.0, 0.0).astype(F32), g),)


cumsum_rows.defvjp(_cumsum_fwd, _cumsum_bwd)


def _head_dot(x):
    i, j = _iota2(LANES, LANES)
    shift = RWKV_N.bit_length() - 1
    same = jnp.where(jnp.right_shift(i, shift) == jnp.right_shift(j, shift), 1.0, 0.0).astype(F32)
    return jnp.concatenate([_ones_right(x[:, g * LANES:(g + 1) * LANES], same) for g in range(x.shape[1] // LANES)], axis=1)


def _ones_right(x, ones):
    t = ones.astype(BF16)
    x1 = x.astype(BF16)
    r1 = x - x1.astype(F32)
    x2 = r1.astype(BF16)
    x3 = (r1 - x2.astype(F32)).astype(BF16)
    dg = lambda q: jnp.dot(q, t, preferred_element_type=F32)
    return dg(x1) + dg(x2) + dg(x3)


@jax.custom_vjp
def head_sum(x):
    return _head_dot(x)


def _head_sum_fwd(x):
    return head_sum(x), None


def _head_sum_bwd(_, g):
    return (_head_dot(g),)


head_sum.defvjp(_head_sum_fwd, _head_sum_bwd)


def cat_rows(*xs):
    return jnp.concatenate(xs, axis=0)


def _iota2(n, m):
    return lax.broadcasted_iota(jnp.int32, (n, m), 0), lax.broadcasted_iota(jnp.int32, (n, m), 1)


def tril_ones(c, strict=False):
    i, j = _iota2(c, c)
    return jnp.where((i > j) if strict else (i >= j), 1.0, 0.0).astype(F32)


def row_of(x, r):
    i = lax.broadcasted_iota(jnp.int32, x.shape, 0)
    return jnp.sum(jnp.where(i == r, x, 0.0), axis=0, keepdims=True)


@jax.custom_vjp
def shift_rows(x, prev):
    r = lax.broadcasted_iota(jnp.int32, x.shape, 0)
    return jnp.where(r == 0, prev, pltpu.roll(x, 1, 0))


def _shift_fwd(x, prev):
    return shift_rows(x, prev), None


def _shift_bwd(_, g):
    c = g.shape[0]
    r = lax.broadcasted_iota(jnp.int32, g.shape, 0)
    return jnp.where(r == c - 1, 0.0, pltpu.roll(g, c - 1, 0)), row_of(g, 0)


shift_rows.defvjp(_shift_fwd, _shift_bwd)


def log_sigmoid(x):
    return jnp.minimum(x, 0.0) - jnp.log(1.0 + jnp.exp(-jnp.abs(x)))


def softplus(x):
    return jnp.maximum(x, 0.0) + jnp.log(1.0 + jnp.exp(-jnp.abs(x)))


def sigmoid(x):
    return 1.0 / (1.0 + jnp.exp(-x))


def rms(x, w, eps=NORM_EPS):
    return x * lax.rsqrt(jnp.mean(x * x, axis=-1, keepdims=True) + eps) * w


def gla_chunk(state, toks, params):
    q, k, v, glow = toks
    gk_up, bias, norm_w = params
    c = GLA_CHUNK
    subs, heads = range(glow.shape[0] // c), range(GLA_HEADS)
    rows = lambda x, j: x[j * c:(j + 1) * c]
    hk = lambda x, h: x[:, h * GLA_DK:(h + 1) * GLA_DK]
    hv = lambda x, h: x[:, h * GLA_DV:(h + 1) * GLA_DV]
    ltri = tril_ones(c)
    g = log_sigmoid(mmx(glow, gk_up, "nn", P_LORA) + bias) / GLA_NORMALIZER
    b = [cumsum_rows(rows(g, j)) for j in subs]
    ref = [lax.stop_gradient(row_of(b[j], c // 2)) for j in subs]
    last = [row_of(b[j], c - 1) for j in subs]
    ql = [rows(q, j) * (GLA_DK ** -0.5) * jnp.exp(b[j] - ref[j]) for j in subs]
    kr = [rows(k, j) * jnp.exp(ref[j] - b[j]) for j in subs]
    kl = [rows(k, j) * jnp.exp(last[j] - b[j]) for j in subs]
    vj = [rows(v, j) for j in subs]
    e_ref, e_last = [jnp.exp(x) for x in ref], [jnp.exp(x) for x in last]
    att = [[mmx(hk(ql[j], h), hk(kr[j], h), "nt", P_GLA) * ltri for h in heads] for j in subs]
    o_in = [[mmx(att[j][h], hv(vj[j], h), "nn", P_GLA) for h in heads] for j in subs]
    kv = [[mmx(hv(vj[j], h), hk(kl[j], h), "tn", P_GLA) for h in heads] for j in subs]
    o = []
    for j in subs:
        o.append([o_in[j][h] + mmx(hk(ql[j], h), state[h] * hk(e_ref[j], h), "nt", P_GLA) for h in heads])
        state = [state[h] * hk(e_last[j], h) + kv[j][h] for h in heads]
    o = [[x * lax.rsqrt(jnp.mean(x * x, axis=-1, keepdims=True) + NORM_EPS) * norm_w for x in oj] for oj in o]
    return cat_rows(*[jnp.concatenate(oj, axis=1) for oj in o]), state


def rwkv_chunk(state, toks, params):
    S, pr, pk, pv, pxw, pxa = state
    r_, k_, v_, xw_, xa_ = toks
    mu_r, mu_k, mu_v, mu_xw, mu_xa, w0, w_up, a0, a_up, k_k, k_a, r_k, ln_w, ln_b = params
    c, n = xw_.shape[0], RWKV_N
    heads = range(RWKV_HEADS)
    hs = lambda x, h: x[:, h * n:(h + 1) * n]
    ltri = tril_ones(c)
    stri = tril_ones(c, strict=True)

    def lerp(x, prev, mu):
        return x + (shift_rows(x, prev) - x) * mu

    xw = jnp.tanh(lerp(xw_, pxw, mu_xw))
    xa = lerp(xa_, pxa, mu_xa)
    r = lerp(r_, pr, mu_r)
    k = lerp(k_, pk, mu_k)
    v = lerp(v_, pv, mu_v)
    w = -softplus(-(w0 + mmx(xw, w_up, "nn", P_LORA))) - 0.5
    lw = -jnp.exp(w)
    asig = sigmoid(a0 + mmx(xa, a_up, "nn", P_LORA))
    kk = k * k_k
    kk = kk / jnp.maximum(jnp.sqrt(head_sum(kk * kk)), 1e-12)
    k2 = k * (1.0 + (asig - 1.0) * k_a)
    b = kk * asig
    cum = cumsum_rows(lw)
    ref = lax.stop_gradient(row_of(cum, c // 2))
    last = row_of(cum, c - 1)
    at = -kk * jnp.exp(cum - lw - ref)
    rt = r * jnp.exp(cum - ref)
    e_out = jnp.exp(ref - cum)
    bt, kt = b * e_out, k2 * e_out
    e_tail = jnp.exp(last - cum)
    bl, kl = b * e_tail, k2 * e_tail
    e_ref, e_last = jnp.exp(ref), jnp.exp(last)
    g = [mmx(cat_rows(hs(at, h), hs(rt, h)), cat_rows(hs(bt, h), hs(kt, h), S[h] * hs(e_ref, h)), "nt", P_RWKV_G) for h in heads]
    aab = [x[:c, :c] * stri for x in g]
    aak = [x[:c, c:2 * c] * stri for x in g]
    arb = [x[c:, :c] * ltri for x in g]
    ark = [x[c:, c:2 * c] * ltri for x in g]
    av = [mmx(cat_rows(aak[h], ark[h]), hs(v, h), "nn", P_RWKV) for h in heads]
    u = [g[h][:c, 2 * c:] + av[h][:c] for h in heads]
    p = aab
    n_double = max(1, (c - 1).bit_length())
    for it in range(n_double):
        if it + 1 < n_double:
            y = [mmx(p[h], jnp.concatenate([p[h], u[h]], axis=1), "nn", P_RWKV) for h in heads]
            u = [u[h] + y[h][:, c:] for h in heads]
            p = [y[h][:, :c] for h in heads]
        else:
            u = [u[h] + mmx(p[h], u[h], "nn", P_RWKV) for h in heads]
    o = [g[h][c:, 2 * c:] + av[h][c:] + mmx(arb[h], u[h], "nn", P_RWKV) for h in heads]
    s1 = [S[h] * hs(e_last, h) + mmx(cat_rows(u[h], hs(v, h)), cat_rows(hs(bl, h), hs(kl, h)), "tn", P_RWKV) for h in heads]
    o = jnp.concatenate(o, axis=1)
    d = o - head_sum(o) * (1.0 / n)
    var = head_sum(d * d) * (1.0 / n)
    o = d * lax.rsqrt(var + RWKV_LN_EPS) * ln_w + ln_b + head_sum(r * k2 * r_k) * v
    new_state = (s1, row_of(r_, c - 1), row_of(k_, c - 1), row_of(v_, c - 1), row_of(xw_, c - 1), row_of(xa_, c - 1))
    return o, new_state


def rope_mat():
    i, j = _iota2(SWA_HD, SWA_HD)
    plus = (j >= 8) & (j < 16) & (i == j - 8)
    minus = (j < 8) & (i == j + 8)
    return jnp.where(plus, 1.0, 0.0).astype(F32) - jnp.where(minus, 1.0, 0.0).astype(F32)


def swa_chunk(state, toks, params, first):
    kprev, vprev = state
    q_, k_, v_, cos, sin = toks
    bq, bk, bv, sinks = params
    c, ng = WINDOW, SWA_GROUP
    n_sub = cos.shape[0] // c
    units = [(j, g) for j in range(n_sub) for g in range(SWA_KV_HEADS)]
    rows = lambda x, j: x[j * c:(j + 1) * c]
    hs = lambda g: range(g * ng, (g + 1) * ng)
    rm = rope_mat()
    qi, kj = _iota2(ng * c, 2 * c)
    qpos = qi & (c - 1)
    cur_ok = (kj >= c) & (qpos >= kj - c)
    prev_ok = (kj < c) & (kj > qpos)
    ok = [cur_ok | (prev_ok & jnp.logical_not(first))] + [cur_ok | prev_ok] * (n_sub - 1)
    cs, sn = [rows(cos, j) for j in range(n_sub)], [rows(sin, j) for j in range(n_sub)]
    cs_g, sn_g = [cat_rows(*[x] * ng) for x in cs], [cat_rows(*[x] * ng) for x in sn]

    def rope(x, cos_, sin_):
        return x * cos_ + mmx(x, rm, "nn", P_ROPE) * sin_

    k = {(j, g): rope(rows(k_[g], j) + bk[g], cs[j], sn[j]) for j, g in units}
    v = {(j, g): rows(v_[g], j) + bv[g] for j, g in units}
    q = {(j, g): rope(cat_rows(*[rows(q_[h], j) + bq[h] for h in hs(g)]), cs_g[j], sn_g[j]) * (SWA_HD ** -0.5) for j, g in units}
    kp = lambda j, g: kprev[g] if j == 0 else k[(j - 1, g)]
    vp = lambda j, g: vprev[g] if j == 0 else v[(j - 1, g)]
    s = {(j, g): jnp.where(ok[j], mmx(q[(j, g)], cat_rows(kp(j, g), k[(j, g)]), "nt", P_SWA), NEG) for j, g in units}
    sink = [cat_rows(*[jnp.broadcast_to(sinks[h], (c, 1)) for h in hs(g)]) for g in range(SWA_KV_HEADS)]
    m = {(j, g): lax.stop_gradient(jnp.maximum(jnp.max(s[(j, g)], axis=-1, keepdims=True), sink[g])) for j, g in units}
    p = {u: jnp.exp(s[u] - m[u]) for u in units}
    ones = jnp.ones((2 * c, SWA_HD), F32)
    pv = {(j, g): mmx(p[(j, g)], jnp.concatenate([cat_rows(vp(j, g), v[(j, g)]), ones], axis=1), "nn", P_SWA) for j, g in units}
    o = {(j, g): pv[(j, g)][:, :SWA_HD] / (pv[(j, g)][:, SWA_HD:] + jnp.exp(sink[g] - m[(j, g)])) for j, g in units}
    outs = [cat_rows(*[o[(j, g)][i * c:(i + 1) * c] for j in range(n_sub)]) for g in range(SWA_KV_HEADS) for i in range(ng)]
    last = n_sub - 1
    return outs, ([k[(last, g)] for g in range(SWA_KV_HEADS)], [v[(last, g)] for g in range(SWA_KV_HEADS)])


def _heads(ref, n, w, rows=slice(None)):
    return [ref[rows, h * w:(h + 1) * w] for h in range(n)]


def _put_heads(ref, vals, w, rows=slice(None), add=False):
    for h, val in enumerate(vals):
        if add:
            ref[rows, h * w:(h + 1) * w] += val
        else:
            ref[rows, h * w:(h + 1) * w] = val


def _col(block_w, name, table):
    off, w = table[name]
    assert off % block_w == 0 and w % block_w == 0
    return off // block_w


def _tok_spec(c, w, colblock, n=None):
    if n is None:
        return pl.BlockSpec((c, w), lambda i: (i, colblock))
    return pl.BlockSpec((c, w), lambda i: (n - 1 - i, colblock))


def _full_spec(shape):
    return pl.BlockSpec(shape, lambda i: (0,) * len(shape))


def _matmul(name, a, b, mode, tm, tn, out_dtype=F32):
    (m, kd) = (a.shape[1], a.shape[0]) if mode == "tn" else a.shape
    n = b.shape[0] if mode == "nt" else b.shape[1]
    assert m % tm == 0 and n % tn == 0
    a_spec = pl.BlockSpec((kd, tm), lambda j, i: (0, i)) if mode == "tn" else pl.BlockSpec((tm, kd), lambda j, i: (i, 0))
    b_spec = pl.BlockSpec((tn, kd), lambda j, i: (j, 0)) if mode == "nt" else pl.BlockSpec((kd, tn), lambda j, i: (0, j))

    def body(a_ref, b_ref, o_ref):
        o_ref[...] = lax.dot_general(a_ref[...].astype(BF16), b_ref[...].astype(BF16), DIMS[mode],
                                     preferred_element_type=F32).astype(out_dtype)

    return pl.pallas_call(
        body, name=name, grid=(n // tn, m // tm), in_specs=[a_spec, b_spec],
        out_specs=pl.BlockSpec((tm, tn), lambda j, i: (i, j)), out_shape=jax.ShapeDtypeStruct((m, n), out_dtype),
        compiler_params=_cparams(("arbitrary", "arbitrary")))(a, b)


TOK_TILE = 512


def _norm_fwd(name, x, w, y=None):
    t, d = x.shape
    tile = pl.BlockSpec((TOK_TILE, d), lambda i: (i, 0))

    def body(*refs):
        if y is None:
            x_ref, w_ref, hn_ref = refs
            h = x_ref[...]
        else:
            x_ref, y_ref, w_ref, h_ref, hn_ref = refs
            h = x_ref[...] + y_ref[...]
            h_ref[...] = h
        hn_ref[...] = rms(h, w_ref[...]).astype(BF16)

    ins = [x, w] if y is None else [x, y, w]
    in_specs = [tile, _full_spec((1, d))] if y is None else [tile, tile, _full_spec((1, d))]
    hn_shape = jax.ShapeDtypeStruct((t, d), BF16)
    out_shape = hn_shape if y is None else (jax.ShapeDtypeStruct((t, d), F32), hn_shape)
    out_specs = tile if y is None else (tile, tile)
    return pl.pallas_call(body, name=name, grid=(t // TOK_TILE,), in_specs=in_specs, out_specs=out_specs, out_shape=out_shape,
                          compiler_params=_cparams(("arbitrary",)))(*ins)


def _norm_bwd(name, h, w, dhn, dres):
    t, d = h.shape
    tile = pl.BlockSpec((TOK_TILE, d), lambda i: (i, 0))

    def body(h_ref, w_ref, dhn_ref, dres_ref, dx_ref, dw_ref):
        @pl.when(pl.program_id(0) == 0)
        def _():
            dw_ref[...] = jnp.zeros_like(dw_ref)

        _, vjp = jax.vjp(rms, h_ref[...], w_ref[...])
        dh, dw = vjp(dhn_ref[...])
        dx_ref[...] = dh + dres_ref[...]
        dw_ref[...] += dw

    return pl.pallas_call(body, name=name, grid=(t // TOK_TILE,), in_specs=[tile, _full_spec((1, d)), tile, tile],
                          out_specs=(tile, _full_spec((1, d))),
                          out_shape=(jax.ShapeDtypeStruct((t, d), F32), jax.ShapeDtypeStruct((1, d), F32)),
                          compiler_params=_cparams(("arbitrary",)))(h, w, dhn, dres)


def _gate_fwd(name, outs, proj):
    t = proj.shape[0]
    widths = [o.shape[1] for o in outs]
    n = len(outs)

    def body(*refs):
        o_refs, g_ref, og_ref = refs[:n], refs[n], refs[n + 1]
        c = 0
        for o_ref, w in zip(o_refs, widths):
            g = g_ref[:, c:c + w]
            og_ref[:, c:c + w] = (o_ref[...] * (g * sigmoid(g))).astype(BF16)
            c += w

    in_specs = [pl.BlockSpec((TOK_TILE, w), lambda i: (i, 0)) for w in widths] + [pl.BlockSpec((TOK_TILE, 1024), lambda i: (i, 0))]
    return pl.pallas_call(body, name=name, grid=(t // TOK_TILE,), in_specs=in_specs,
                          out_specs=pl.BlockSpec((TOK_TILE, 1024), lambda i: (i, 0)),
                          out_shape=jax.ShapeDtypeStruct((t, 1024), BF16), compiler_params=_cparams(("arbitrary",)))(*outs, proj)


def _gate_bwd(name, outs, proj, dog):
    t = proj.shape[0]
    widths = [o.shape[1] for o in outs]
    n = len(outs)

    def body(*refs):
        o_refs, g_ref, dog_ref = refs[:n], refs[n], refs[n + 1]
        do_refs, dg_ref = refs[n + 2:2 * n + 2], refs[2 * n + 2]
        c = 0
        for o_ref, do_ref, w in zip(o_refs, do_refs, widths):
            g = g_ref[:, c:c + w]
            dog_ = dog_ref[:, c:c + w]
            s = sigmoid(g)
            do_ref[...] = dog_ * (g * s)
            dg_ref[:, c:c + w] = dog_ * o_ref[...] * (s * (1.0 + g * (1.0 - s)))
            c += w

    o_specs = [pl.BlockSpec((TOK_TILE, w), lambda i: (i, 0)) for w in widths]
    wide = pl.BlockSpec((TOK_TILE, 1024), lambda i: (i, 0))
    return pl.pallas_call(body, name=name, grid=(t // TOK_TILE,), in_specs=o_specs + [wide, wide], out_specs=tuple(o_specs) + (wide,),
                          out_shape=tuple(jax.ShapeDtypeStruct((t, w), F32) for w in widths) + (jax.ShapeDtypeStruct((t, 1024), F32),),
                          compiler_params=_cparams(("arbitrary",)))(*outs, proj, dog)


def _top(h1, y1, b_out1, fw, target):
    t, d = h1.shape
    tile = pl.BlockSpec((TOK_TILE, d), lambda i: (i, 0))
    vec = _full_spec((1, d))

    def body(h1_ref, y1_ref, b_ref, fw_ref, tgt_ref, dh2_ref, loss_ref, db_ref, dfw_ref):
        @pl.when(pl.program_id(0) == 0)
        def _():
            loss_ref[...] = jnp.zeros_like(loss_ref)
            db_ref[...] = jnp.zeros_like(db_ref)
            dfw_ref[...] = jnp.zeros_like(dfw_ref)

        tgt = tgt_ref[...]

        def f(h2, w):
            err = rms(h2, w) - tgt
            per_tok = jnp.mean(err * err, axis=-1, keepdims=True)
            return 0.5 * jnp.sum(per_tok, axis=0, keepdims=True)

        h2 = h1_ref[...] + y1_ref[...] + b_ref[...]
        loss, vjp = jax.vjp(f, h2, fw_ref[...])
        dh2, dfw = vjp(jnp.ones((1, 1), F32))
        dh2_ref[...] = dh2
        loss_ref[...] += jnp.broadcast_to(loss, loss_ref.shape)
        db_ref[...] += jnp.sum(dh2, axis=0, keepdims=True)
        dfw_ref[...] += dfw

    return pl.pallas_call(body, name="top_loss", grid=(t // TOK_TILE,), in_specs=[tile, tile, vec, vec, tile],
                          out_specs=(tile, _full_spec((1, LANES)), vec, vec),
                          out_shape=(jax.ShapeDtypeStruct((t, d), F32), jax.ShapeDtypeStruct((1, LANES), F32),
                                     jax.ShapeDtypeStruct((1, d), F32), jax.ShapeDtypeStruct((1, d), F32)),
                          compiler_params=_cparams(("arbitrary",)))(h1, y1, b_out1, fw, target)


def _gla_load(q_ref, k_ref, v_ref, gl_ref, up_ref, bias_ref, nw_ref):
    toks = (q_ref[...], k_ref[...], v_ref[...], gl_ref[...])
    params = (up_ref[...], bias_ref[...], nw_ref[...])
    return toks, params


def _gla_specs(c, n=None):
    toks = [_tok_spec(c, 256, _col(256, "gq", C0), n), _tok_spec(c, 256, _col(256, "gk", C0), n),
            _tok_spec(c, 512, _col(512, "gv", C0), n), _tok_spec(c, 128, _col(128, "glow", C0), n)]
    params = [_full_spec((128, 256)), _full_spec((1, 256)), _full_spec((1, 128))]
    return toks, params


def _gla_fwd(proj0, gk_up, gk_bias, norm_w):
    t = proj0.shape[0]
    c = GLA_STEP
    nc = t // c
    toks_s, params_s = _gla_specs(c)

    def body(q_ref, k_ref, v_ref, gl_ref, up_ref, bias_ref, nw_ref, o_ref, st_ref, s_scr):
        @pl.when(pl.program_id(0) == 0)
        def _():
            s_scr[...] = jnp.zeros_like(s_scr)

        st_ref[...] = s_scr[...]
        toks, params = _gla_load(q_ref, k_ref, v_ref, gl_ref, up_ref, bias_ref, nw_ref)
        state = [s_scr[h * GLA_DV:(h + 1) * GLA_DV, :] for h in range(GLA_HEADS)]
        o_ref[...], new = gla_chunk(state, toks, params)
        for h in range(GLA_HEADS):
            s_scr[h * GLA_DV:(h + 1) * GLA_DV, :] = new[h]

    return pl.pallas_call(
        body, name="gla_fwd", grid=(nc,), in_specs=toks_s + params_s,
        out_specs=(_tok_spec(c, 512, 0), pl.BlockSpec((512, GLA_DK), lambda i: (i, 0))),
        out_shape=(jax.ShapeDtypeStruct((t, 512), F32), jax.ShapeDtypeStruct((nc * 512, GLA_DK), F32)),
        scratch_shapes=[pltpu.VMEM((512, GLA_DK), F32)], compiler_params=_cparams(("arbitrary",)))(
            proj0, proj0, proj0, proj0, gk_up, gk_bias, norm_w)


def _gla_bwd(proj0, gk_up, gk_bias, norm_w, states, do):
    t = proj0.shape[0]
    c = GLA_STEP
    nc = t // c
    toks_s, params_s = _gla_specs(c, nc)

    def body(q_ref, k_ref, v_ref, gl_ref, up_ref, bias_ref, nw_ref, st_ref, do_ref,
             dq_ref, dk_ref, dv_ref, dgl_ref, dup_ref, dbias_ref, dnw_ref, ds_scr):
        @pl.when(pl.program_id(0) == 0)
        def _():
            ds_scr[...] = jnp.zeros_like(ds_scr)
            dup_ref[...] = jnp.zeros_like(dup_ref)
            dbias_ref[...] = jnp.zeros_like(dbias_ref)
            dnw_ref[...] = jnp.zeros_like(dnw_ref)

        toks, params = _gla_load(q_ref, k_ref, v_ref, gl_ref, up_ref, bias_ref, nw_ref)
        rows = lambda h: slice(h * GLA_DV, (h + 1) * GLA_DV)
        state = [st_ref[rows(h), :] for h in range(GLA_HEADS)]
        _, vjp = jax.vjp(gla_chunk, state, toks, params)
        dstate_in = [ds_scr[rows(h), :] for h in range(GLA_HEADS)]
        dstate, (dq_ref[...], dk_ref[...], dv_ref[...], dgl_ref[...]), (dup, dbias, dnw) = vjp((do_ref[...], dstate_in))
        dup_ref[...] += dup
        dbias_ref[...] += dbias
        dnw_ref[...] += dnw
        for h in range(GLA_HEADS):
            ds_scr[rows(h), :] = dstate[h]

    rev = lambda w: pl.BlockSpec((c, w), lambda i: (nc - 1 - i, 0))
    return pl.pallas_call(
        body, name="gla_bwd", grid=(nc,),
        in_specs=toks_s + params_s + [pl.BlockSpec((512, GLA_DK), lambda i: (nc - 1 - i, 0)), rev(512)],
        out_specs=(rev(256), rev(256), rev(512), rev(128), _full_spec((128, 256)), _full_spec((1, 256)), _full_spec((1, 128))),
        out_shape=(jax.ShapeDtypeStruct((t, 256), F32), jax.ShapeDtypeStruct((t, 256), F32), jax.ShapeDtypeStruct((t, 512), F32),
                   jax.ShapeDtypeStruct((t, 128), F32), jax.ShapeDtypeStruct((128, 256), F32), jax.ShapeDtypeStruct((1, 256), F32),
                   jax.ShapeDtypeStruct((1, 128), F32)),
        scratch_shapes=[pltpu.VMEM((512, GLA_DK), F32)], compiler_params=_cparams(("arbitrary",)))(
            proj0, proj0, proj0, proj0, gk_up, gk_bias, norm_w, states, do)


RWKV_PARAM_SHAPES = [(1, 512), (1, 512), (1, 512), (1, 128), (1, 128), (1, 512), (128, 512), (1, 512), (128, 512),
                     (1, 512), (1, 512), (1, 512), (1, 512), (1, 512)]
PREV_W = 1792
PREV_COLS = [slice(0, 512), slice(512, 1024), slice(1024, 1536), slice(1536, 1664), slice(1664, 1792)]


def _rwkv_load(r_ref, k_ref, v_ref, xw_ref, xa_ref, p_refs):
    toks = (r_ref[...], k_ref[...], v_ref[...], xw_ref[...], xa_ref[...])
    return toks, tuple(p[...] for p in p_refs)


def _rwkv_state(s_ref, prev_ref):
    n = RWKV_N
    S = [s_ref[h * n:(h + 1) * n, :] for h in range(RWKV_HEADS)]
    return (S,) + tuple(prev_ref[0:1, cols] for cols in PREV_COLS)


def _rwkv_put_state(s_ref, prev_ref, state):
    n = RWKV_N
    for h in range(RWKV_HEADS):
        s_ref[h * n:(h + 1) * n, :] = state[0][h]
    for cols, val in zip(PREV_COLS, state[1:]):
        prev_ref[0:1, cols] = val


def _rwkv_specs(c, n=None):
    toks = [_tok_spec(c, 512, _col(512, "r", C0), n), _tok_spec(c, 512, _col(512, "k", C0), n),
            _tok_spec(c, 512, _col(512, "v", C0), n), _tok_spec(c, 128, _col(128, "xw", C0), n),
            _tok_spec(c, 128, _col(128, "xa", C0), n)]
    return toks, [_full_spec(s) for s in RWKV_PARAM_SHAPES]


def _rwkv_fwd(proj0, params, comm, kinds):
    t = proj0.shape[0]
    c = RWKV_CHUNK
    nc = t // c
    toks_s, params_s = _rwkv_specs(c)
    npar, ncomm = len(params), len(comm)

    def body(*refs):
        tok_refs, p_refs = refs[:5], refs[5:5 + npar]
        comm_in = refs[5 + npar:5 + npar + ncomm]
        o_ref, st_ref, pst_ref = refs[5 + npar + ncomm:8 + npar + ncomm]
        comm_out = refs[8 + npar + ncomm:8 + npar + 2 * ncomm]
        s_scr, prev_scr = refs[8 + npar + 2 * ncomm:10 + npar + 2 * ncomm]
        sems = refs[10 + npar + 2 * ncomm:]
        i = pl.program_id(0)

        @pl.when(i == 0)
        def _():
            _comm_start(*_comm_copies(comm_in, comm_out, kinds, *sems))
            s_scr[...] = jnp.zeros_like(s_scr)
            prev_scr[...] = jnp.zeros_like(prev_scr)

        st_ref[...] = s_scr[...]
        pst_ref[...] = prev_scr[...]
        toks, prm = _rwkv_load(*tok_refs, p_refs)
        o_ref[...], new = rwkv_chunk(_rwkv_state(s_scr, prev_scr), toks, prm)
        _rwkv_put_state(s_scr, prev_scr, new)

        @pl.when(i == nc - 1)
        def _():
            _comm_wait(*_comm_copies(comm_in, comm_out, kinds, *sems))

    outs = pl.pallas_call(
        body, name="rwkv_fwd", grid=(nc,), in_specs=toks_s + params_s + [ANY] * ncomm,
        out_specs=[_tok_spec(c, 512, 0), pl.BlockSpec((512, RWKV_N), lambda i: (i, 0)), pl.BlockSpec((8, PREV_W), lambda i: (i, 0))]
        + [ANY] * ncomm,
        out_shape=[jax.ShapeDtypeStruct((t, 512), F32), jax.ShapeDtypeStruct((nc * 512, RWKV_N), F32),
                   jax.ShapeDtypeStruct((nc * 8, PREV_W), F32)] + _comm_out_shapes(comm, kinds),
        scratch_shapes=[pltpu.VMEM((512, RWKV_N), F32), pltpu.VMEM((8, PREV_W), F32)] + _comm_scratch(ncomm),
        compiler_params=_cparams(("arbitrary",)))(proj0, proj0, proj0, proj0, proj0, *params, *comm)
    return outs[0], outs[1], outs[2], outs[3:]


def _rwkv_bwd(proj0, params, states, prevs, do, comm, kinds):
    t = proj0.shape[0]
    c = RWKV_CHUNK
    nc = t // c
    toks_s, params_s = _rwkv_specs(c, nc)
    npar, ncomm = len(params), len(comm)

    def body(*refs):
        tok_refs, p_refs = refs[:5], refs[5:5 + npar]
        st_ref, pst_ref, do_ref = refs[5 + npar:8 + npar]
        comm_in = refs[8 + npar:8 + npar + ncomm]
        outs = refs[8 + npar + ncomm:]
        dtok_refs, dp_refs, comm_out = outs[:5], outs[5:5 + npar], outs[5 + npar:5 + npar + ncomm]
        ds_scr, dprev_scr = outs[5 + npar + ncomm:7 + npar + ncomm]
        sems = outs[7 + npar + ncomm:]
        i = pl.program_id(0)

        @pl.when(i == 0)
        def _():
            _comm_start(*_comm_copies(comm_in, comm_out, kinds, *sems))
            ds_scr[...] = jnp.zeros_like(ds_scr)
            dprev_scr[...] = jnp.zeros_like(dprev_scr)
            for dp in dp_refs:
                dp[...] = jnp.zeros_like(dp)

        toks, prm = _rwkv_load(*tok_refs, p_refs)
        _, vjp = jax.vjp(rwkv_chunk, _rwkv_state(st_ref, pst_ref), toks, prm)
        dstate, dtoks, dprm = vjp((do_ref[...], _rwkv_state(ds_scr, dprev_scr)))
        for ref, val in zip(dtok_refs, dtoks):
            ref[...] = val
        for ref, val in zip(dp_refs, dprm):
            ref[...] += val
        _rwkv_put_state(ds_scr, dprev_scr, dstate)

        @pl.when(i == nc - 1)
        def _():
            _comm_wait(*_comm_copies(comm_in, comm_out, kinds, *sems))

    rev = lambda w: pl.BlockSpec((c, w), lambda i: (nc - 1 - i, 0))
    outs = pl.pallas_call(
        body, name="rwkv_bwd", grid=(nc,),
        in_specs=toks_s + params_s + [pl.BlockSpec((512, RWKV_N), lambda i: (nc - 1 - i, 0)),
                                      pl.BlockSpec((8, PREV_W), lambda i: (nc - 1 - i, 0)), rev(512)] + [ANY] * ncomm,
        out_specs=[rev(512), rev(512), rev(512), rev(128), rev(128)] + params_s + [ANY] * ncomm,
        out_shape=[jax.ShapeDtypeStruct((t, w), F32) for w in (512, 512, 512, 128, 128)]
        + [jax.ShapeDtypeStruct(s, F32) for s in RWKV_PARAM_SHAPES] + _comm_out_shapes(comm, kinds),
        scratch_shapes=[pltpu.VMEM((512, RWKV_N), F32), pltpu.VMEM((8, PREV_W), F32)] + _comm_scratch(ncomm),
        compiler_params=_cparams(("arbitrary",)))(proj0, proj0, proj0, proj0, proj0, *params, states, prevs, do, *comm)
    return outs[:5], outs[5:5 + npar], outs[5 + npar:]


def _swa_load(q_ref, k_ref, v_ref, cos_ref, sin_ref, bq_ref, bk_ref, bv_ref, sk_ref):
    toks = (_heads(q_ref, 16, SWA_HD), _heads(k_ref, 4, SWA_HD), _heads(v_ref, 4, SWA_HD), cos_ref[...], sin_ref[...])
    params = (_heads(bq_ref, 16, SWA_HD), _heads(bk_ref, 4, SWA_HD), _heads(bv_ref, 4, SWA_HD), _heads(sk_ref, 16, 1))
    return toks, params


def _swa_specs(c, n=None):
    toks = [_tok_spec(c, 1024, _col(1024, "q", C1), n), _tok_spec(c, 256, _col(256, "k", C1), n),
            _tok_spec(c, 256, _col(256, "v", C1), n), _tok_spec(c, SWA_HD, 0, n), _tok_spec(c, SWA_HD, 0, n)]
    params = [_full_spec((1, 1024)), _full_spec((1, 256)), _full_spec((1, 256)), _full_spec((1, 16))]
    return toks, params


def _swa_fwd(proj1, cos, sin, bq, bk, bv, sinks):
    t = proj1.shape[0]
    c = SWA_STEP
    nb = t // c
    toks_s, params_s = _swa_specs(c)
    state_spec = pl.BlockSpec((WINDOW, 256), lambda i: (i, 0))

    def body(q_ref, k_ref, v_ref, cos_ref, sin_ref, bq_ref, bk_ref, bv_ref, sk_ref, o_ref, kst_ref, vst_ref, k_scr, v_scr):
        first = pl.program_id(0) == 0

        @pl.when(first)
        def _():
            k_scr[...] = jnp.zeros_like(k_scr)
            v_scr[...] = jnp.zeros_like(v_scr)

        kst_ref[...] = k_scr[...]
        vst_ref[...] = v_scr[...]
        toks, params = _swa_load(q_ref, k_ref, v_ref, cos_ref, sin_ref, bq_ref, bk_ref, bv_ref, sk_ref)
        outs, (kn, vn) = swa_chunk((_heads(k_scr, 4, SWA_HD), _heads(v_scr, 4, SWA_HD)), toks, params, first)
        _put_heads(o_ref, outs, SWA_HD)
        _put_heads(k_scr, kn, SWA_HD)
        _put_heads(v_scr, vn, SWA_HD)

    return pl.pallas_call(
        body, name="swa_fwd", grid=(nb,), in_specs=toks_s + params_s,
        out_specs=(_tok_spec(c, 1024, 0), state_spec, state_spec),
        out_shape=(jax.ShapeDtypeStruct((t, 1024), F32), jax.ShapeDtypeStruct((nb * WINDOW, 256), F32),
                   jax.ShapeDtypeStruct((nb * WINDOW, 256), F32)),
        scratch_shapes=[pltpu.VMEM((WINDOW, 256), F32), pltpu.VMEM((WINDOW, 256), F32)],
        compiler_params=_cparams(("arbitrary",)))(proj1, proj1, proj1, cos, sin, bq, bk, bv, sinks)


def _swa_bwd(proj1, cos, sin, bq, bk, bv, sinks, kst, vst, do):
    t = proj1.shape[0]
    c = SWA_STEP
    nb = t // c
    toks_s, params_s = _swa_specs(c, nb)
    state_spec = pl.BlockSpec((WINDOW, 256), lambda i: (nb - 1 - i, 0))

    def body(q_ref, k_ref, v_ref, cos_ref, sin_ref, bq_ref, bk_ref, bv_ref, sk_ref, kst_ref, vst_ref, do_ref,
             dq_ref, dk_ref, dv_ref, dbq_ref, dbk_ref, dbv_ref, dsk_ref, dk_scr, dv_scr):
        i = pl.program_id(0)

        @pl.when(i == 0)
        def _():
            dk_scr[...] = jnp.zeros_like(dk_scr)
            dv_scr[...] = jnp.zeros_like(dv_scr)
            for ref in (dbq_ref, dbk_ref, dbv_ref, dsk_ref):
                ref[...] = jnp.zeros_like(ref)

        first = i == nb - 1
        toks, params = _swa_load(q_ref, k_ref, v_ref, cos_ref, sin_ref, bq_ref, bk_ref, bv_ref, sk_ref)
        f = functools.partial(swa_chunk, first=first)
        _, vjp = jax.vjp(f, (_heads(kst_ref, 4, SWA_HD), _heads(vst_ref, 4, SWA_HD)), toks, params)
        dstate_in = (_heads(dk_scr, 4, SWA_HD), _heads(dv_scr, 4, SWA_HD))
        (dkp, dvp), (dq, dk, dv, _, _), (dbq, dbk, dbv, dsk) = vjp((_heads(do_ref, 16, SWA_HD), dstate_in))
        _put_heads(dq_ref, dq, SWA_HD)
        _put_heads(dk_ref, dk, SWA_HD)
        _put_heads(dv_ref, dv, SWA_HD)
        _put_heads(dbq_ref, dbq, SWA_HD, add=True)
        _put_heads(dbk_ref, dbk, SWA_HD, add=True)
        _put_heads(dbv_ref, dbv, SWA_HD, add=True)
        _put_heads(dsk_ref, dsk, 1, add=True)
        _put_heads(dk_scr, dkp, SWA_HD)
        _put_heads(dv_scr, dvp, SWA_HD)

    rev = lambda w: pl.BlockSpec((c, w), lambda i: (nb - 1 - i, 0))
    return pl.pallas_call(
        body, name="swa_bwd", grid=(nb,), in_specs=toks_s + params_s + [state_spec, state_spec, rev(1024)],
        out_specs=(rev(1024), rev(256), rev(256), _full_spec((1, 1024)), _full_spec((1, 256)), _full_spec((1, 256)), _full_spec((1, 16))),
        out_shape=(jax.ShapeDtypeStruct((t, 1024), F32), jax.ShapeDtypeStruct((t, 256), F32), jax.ShapeDtypeStruct((t, 256), F32),
                   jax.ShapeDtypeStruct((1, 1024), F32), jax.ShapeDtypeStruct((1, 256), F32), jax.ShapeDtypeStruct((1, 256), F32),
                   jax.ShapeDtypeStruct((1, 16), F32)),
        scratch_shapes=[pltpu.VMEM((WINDOW, 256), F32), pltpu.VMEM((WINDOW, 256), F32)],
        compiler_params=_cparams(("arbitrary",)))(proj1, proj1, proj1, cos, sin, bq, bk, bv, sinks, kst, vst, do)


MESH = pl.DeviceIdType.MESH
ANY = pl.BlockSpec(memory_space=pl.ANY)


def _my_place():
    return lax.axis_index("x"), lax.axis_index("y"), lax.axis_index("c")


def _all_gather(shards):
    n = len(shards)

    def body(*refs):
        in_refs, out_refs = refs[:n], refs[n:2 * n]
        send_sems, recv_sems, local_sems = refs[2 * n:]
        x, y, c = _my_place()
        me, sibling = (x, y, c), (x, y, 1 - c)
        chips = [(1 - x, y), (x, 1 - y), (1 - x, 1 - y)]

        def slot(out_ref, place):
            px, py, pc = place
            return out_ref.at[4 * px + 2 * py + pc]

        def copy(a, k, block, to, src=None):
            return pltpu.make_async_remote_copy(
                src_ref=slot(out_refs[a], block) if src is None else src, dst_ref=slot(out_refs[a], block),
                send_sem=send_sems.at[a, k], recv_sem=recv_sems.at[a, k], device_id=to, device_id_type=MESH)

        mine = [pltpu.make_async_copy(in_refs[a], slot(out_refs[a], me), local_sems.at[a]) for a in range(n)]
        for cp in mine:
            cp.start()
        first = []
        for a in range(n):
            first.append(copy(a, 0, me, sibling, src=in_refs[a]))
            first += [copy(a, 1 + j, me, (*chip, c), src=in_refs[a]) for j, chip in enumerate(chips)]
        for cp in first:
            cp.start()
        passed = []
        for j, chip in enumerate(chips):
            for a in range(n):
                copy(a, 1 + j, (*chip, c), me).wait_recv()
                fwd = copy(a, 4 + j, (*chip, c), sibling)
                fwd.start()
                passed.append(fwd)
        for a in range(n):
            copy(a, 0, sibling, me).wait_recv()
            for j, chip in enumerate(chips):
                copy(a, 4 + j, (*chip, 1 - c), me).wait_recv()
        for cp in first + passed:
            cp.wait_send()
        for cp in mine:
            cp.wait()

    return pl.pallas_call(
        body, name="all_gather_weights", in_specs=[ANY] * n, out_specs=[ANY] * n,
        out_shape=[jax.ShapeDtypeStruct((N_DEV,) + s.shape, s.dtype) for s in shards],
        scratch_shapes=_comm_scratch(n))(*shards)


def _comm_copies(in_refs, out_refs, kinds, send_sems, recv_sems, local_sems):
    x, y, c = _my_place()
    my_idx = 4 * x + 2 * y + c
    src = lambda a, idx: in_refs[a] if kinds[a] == "gather" else in_refs[a].at[idx]
    local = [pltpu.make_async_copy(src(a, my_idx), out_refs[a].at[my_idx], local_sems.at[a]) for a in range(len(kinds))]
    remote = []
    for rel in range(1, N_DEV):
        px, py, pc = x ^ ((rel >> 2) & 1), y ^ ((rel >> 1) & 1), c ^ (rel & 1)
        for a in range(len(kinds)):
            remote.append(pltpu.make_async_remote_copy(
                src_ref=src(a, 4 * px + 2 * py + pc), dst_ref=out_refs[a].at[my_idx], send_sem=send_sems.at[a, rel - 1],
                recv_sem=recv_sems.at[a, rel - 1], device_id=(px, py, pc), device_id_type=MESH))
    return local, remote


def _comm_start(local, remote):
    for cp in local + remote:
        cp.start()


def _comm_wait(local, remote):
    for cp in remote:
        cp.wait_recv()
    for cp in remote:
        cp.wait_send()
    for cp in local:
        cp.wait()


def _comm_out_shapes(arrays, kinds):
    return [jax.ShapeDtypeStruct(((N_DEV,) + a.shape) if k == "gather" else a.shape, a.dtype) for a, k in zip(arrays, kinds)]


def _comm_scratch(n):
    return [pltpu.SemaphoreType.DMA((n, N_DEV - 1)), pltpu.SemaphoreType.DMA((n, N_DEV - 1)), pltpu.SemaphoreType.DMA((n,))]


def _exchange(arrays, kinds):
    n = len(arrays)

    def body(*refs):
        copies = _comm_copies(refs[:n], refs[n:2 * n], kinds, *refs[2 * n:])
        _comm_start(*copies)
        _comm_wait(*copies)

    return pl.pallas_call(body, name="exchange_grads", in_specs=[ANY] * n, out_specs=[ANY] * n,
                          out_shape=_comm_out_shapes(arrays, kinds), scratch_shapes=_comm_scratch(n))(*arrays)


def _adam_math(w, g, m, v):
    m = ADAM_B1 * m + (1.0 - ADAM_B1) * g
    v = ADAM_B2 * v + (1.0 - ADAM_B2) * (g * g)
    m_hat = m / (1.0 - ADAM_B1 ** ADAM_STEP)
    v_hat = v / (1.0 - ADAM_B2 ** ADAM_STEP)
    delta = -ADAM_LR * (m_hat / (jnp.sqrt(v_hat) + ADAM_EPS) + ADAM_WD * w)
    return delta, m, v


def _adamw(name, w, gslots, m, v, tc):
    r, cc = w.shape
    assert cc % tc == 0
    tile = pl.BlockSpec((r, tc), lambda i: (0, i))

    def body(w_ref, g_ref, m_ref, v_ref, go_ref, d_ref, mo_ref, vo_ref):
        g = g_ref[0].astype(F32)
        for s in range(1, N_DEV):
            g = g + g_ref[s].astype(F32)
        d, mn, vn = _adam_math(w_ref[...], g, m_ref[...], v_ref[...])
        go_ref[...] = g
        d_ref[...] = d
        mo_ref[...] = mn
        vo_ref[...] = vn

    shp = jax.ShapeDtypeStruct((r, cc), F32)
    return pl.pallas_call(body, name=name, grid=(cc // tc,),
                          in_specs=[tile, pl.BlockSpec((N_DEV, r, tc), lambda i: (0, 0, i)), tile, tile],
                          out_specs=(tile,) * 4, out_shape=(shp,) * 4, compiler_params=_cparams(("arbitrary",)))(w, gslots, m, v)


PACK_TILE = 8 * LANES


def _packed_rows(shape, mode):
    r, w = shape
    return -(-r // 8) * 8 if mode == "rows" else -(-(r * w) // PACK_TILE) * 8


def _pack_small(arrays, modes, lead=False):
    out = []
    for a, mode in zip(arrays, modes):
        a = a.astype(F32) if lead else a.astype(F32)[None]
        if mode == "rows":
            out.append(jnp.pad(a, ((0, 0), (0, (-a.shape[1]) % 8), (0, LANES - a.shape[2]))))
        else:
            flat = a.reshape(a.shape[0], -1)
            out.append(jnp.pad(flat, ((0, 0), (0, (-flat.shape[1]) % PACK_TILE))).reshape(a.shape[0], -1, LANES))
    out = jnp.concatenate(out, axis=1)
    return out if lead else out[0]


def _take_small(packed, row0, shape, mode):
    r, w = shape
    lead = packed.ndim == 3
    if mode == "rows":
        return packed[:, row0:row0 + r, :w] if lead else packed[row0:row0 + r, :w]
    per_row = -(-w // LANES)
    if lead:
        return packed[:, row0:row0 + r * per_row].reshape(packed.shape[0], r, per_row * LANES)[:, :, :w]
    rows = []
    for i in range(r):
        pieces = [packed[row0 + i * per_row + j:row0 + i * per_row + j + 1, :] for j in range(per_row)]
        rows.append((pieces[0] if per_row == 1 else jnp.concatenate(pieces, axis=1))[:, :w])
    return rows[0] if r == 1 else jnp.concatenate(rows, axis=0)


def _adamw_small(slots, specs, ws, ms, vs, loss_row):
    n = len(specs)

    def body(*refs):
        slots_ref, w_refs, m_refs, v_refs = refs[0], refs[1:1 + n], refs[1 + n:1 + 2 * n], refs[1 + 2 * n:1 + 3 * n]
        out_refs, loss_ref = refs[1 + 3 * n:1 + 7 * n], refs[1 + 7 * n]
        gp = slots_ref[0]
        for s in range(1, N_DEV):
            gp = gp + slots_ref[s]
        read = lambda ref: ref[0] if len(ref.shape) == 3 else ref[...]
        for k, (shape, mode, row0) in enumerate(specs):
            g = _take_small(gp, row0, shape, mode)
            d, mn, vn = _adam_math(read(w_refs[k]), g, read(m_refs[k]), read(v_refs[k]))
            for ref, val in zip(out_refs[4 * k:4 * k + 4], (g, d, mn, vn)):
                if len(ref.shape) == 3:
                    ref[0] = val
                else:
                    ref[...] = val
        loss_ref[...] = gp[loss_row:loss_row + 1, :]

    vmem = pl.BlockSpec(memory_space=pltpu.VMEM)
    out_shape = [jax.ShapeDtypeStruct(w.shape, F32) for w in ws for _ in range(4)] + [jax.ShapeDtypeStruct((1, LANES), F32)]
    outs = pl.pallas_call(body, name="adamw_small", in_specs=[vmem] * (1 + 3 * n), out_specs=[vmem] * (4 * n + 1),
                          out_shape=out_shape)(slots, *ws, *ms, *vs)
    return [outs[4 * k:4 * k + 4] for k in range(n)], outs[4 * n]


def _rope_tables(t):
    half = 8
    inv_freq = ROPE_THETA ** (-jnp.arange(half, dtype=F32) / half)
    ang = jnp.arange(t, dtype=F32)[:, None] * inv_freq
    cos = jnp.concatenate([jnp.cos(ang), jnp.cos(ang), jnp.ones((t, SWA_HD - 16), F32)], axis=1)
    sin = jnp.concatenate([jnp.sin(ang), jnp.sin(ang), jnp.zeros((t, SWA_HD - 16), F32)], axis=1)
    return cos, sin


def _pad_to(a, rows=None, cols=None):
    r = 0 if rows is None else rows - a.shape[0]
    c = 0 if cols is None else cols - a.shape[1]
    return jnp.pad(a, ((0, r), (0, c)))


ORIG0 = dict(gq=(0, 256), gk=(256, 256), gv=(512, 512), glow=(1024, 16), r=(1040, 512), k=(1552, 512), v=(2064, 512),
             xw=(2576, 64), xa=(2640, 64), gate=(2704, 1024))
ORIG0_ORDER = ["gq", "gk", "gv", "glow", "r", "k", "v", "xw", "xa", "gate"]


def _w0t_to_padded(wt):
    rows, at = [], 0
    for name, (off, width) in sorted(C0.items(), key=lambda kv: kv[1][0]):
        assert off == at
        src, src_w = ORIG0[name]
        rows.append(_pad_to(wt[src:src + src_w], rows=width))
        at += width
    rows.append(jnp.zeros((N0P - at, wt.shape[1]), wt.dtype))
    return jnp.concatenate(rows, axis=0)


def _w0t_from_padded(wpt):
    return jnp.concatenate([wpt[C0[n][0]:C0[n][0] + ORIG0[n][1]] for n in ORIG0_ORDER], axis=0)


def _w1t_to_mine(wt):
    return jnp.concatenate([wt[1536:2560], wt[:1536]], axis=0)


def _w1t_from_mine(wt):
    return jnp.concatenate([wt[1024:2560], wt[:1024]], axis=0)


def kernel(x, norm_w, w_in0, gla_gk_up, gla_gk_bias, gla_norm_w, rwkv_mu, rwkv_w0, rwkv_w_up, rwkv_a0, rwkv_a_up, rwkv_k_k, rwkv_k_a, rwkv_r_k, rwkv_ln_w, rwkv_ln_b, w_out0, w_in1, b_in1, attn_sinks, w_out1, b_out1, final_norm_w, loss_target, m_norm_w, m_w_in0, m_gla_gk_up, m_gla_gk_bias, m_gla_norm_w, m_rwkv_mu, m_rwkv_w0, m_rwkv_w_up, m_rwkv_a0, m_rwkv_a_up, m_rwkv_k_k, m_rwkv_k_a, m_rwkv_r_k, m_rwkv_ln_w, m_rwkv_ln_b, m_w_out0, m_w_in1, m_b_in1, m_attn_sinks, m_w_out1, m_b_out1, m_final_norm_w, v_norm_w, v_w_in0, v_gla_gk_up, v_gla_gk_bias, v_gla_norm_w, v_rwkv_mu, v_rwkv_w0, v_rwkv_w_up, v_rwkv_a0, v_rwkv_a_up, v_rwkv_k_k, v_rwkv_k_a, v_rwkv_r_k, v_rwkv_ln_w, v_rwkv_ln_b, v_w_out0, v_w_in1, v_b_in1, v_attn_sinks, v_w_out1, v_b_out1, v_final_norm_w):
    weights = dict(norm_w=norm_w, w_in0=w_in0, gla_gk_up=gla_gk_up, gla_gk_bias=gla_gk_bias, gla_norm_w=gla_norm_w, rwkv_mu=rwkv_mu,
                   rwkv_w0=rwkv_w0, rwkv_w_up=rwkv_w_up, rwkv_a0=rwkv_a0, rwkv_a_up=rwkv_a_up, rwkv_k_k=rwkv_k_k, rwkv_k_a=rwkv_k_a,
                   rwkv_r_k=rwkv_r_k, rwkv_ln_w=rwkv_ln_w, rwkv_ln_b=rwkv_ln_b, w_out0=w_out0, w_in1=w_in1, b_in1=b_in1,
                   attn_sinks=attn_sinks, w_out1=w_out1, b_out1=b_out1, final_norm_w=final_norm_w)
    moms = dict(norm_w=m_norm_w, w_in0=m_w_in0, gla_gk_up=m_gla_gk_up, gla_gk_bias=m_gla_gk_bias, gla_norm_w=m_gla_norm_w,
                rwkv_mu=m_rwkv_mu, rwkv_w0=m_rwkv_w0, rwkv_w_up=m_rwkv_w_up, rwkv_a0=m_rwkv_a0, rwkv_a_up=m_rwkv_a_up,
                rwkv_k_k=m_rwkv_k_k, rwkv_k_a=m_rwkv_k_a, rwkv_r_k=m_rwkv_r_k, rwkv_ln_w=m_rwkv_ln_w, rwkv_ln_b=m_rwkv_ln_b,
                w_out0=m_w_out0, w_in1=m_w_in1, b_in1=m_b_in1, attn_sinks=m_attn_sinks, w_out1=m_w_out1, b_out1=m_b_out1,
                final_norm_w=m_final_norm_w)
    vars_ = dict(norm_w=v_norm_w, w_in0=v_w_in0, gla_gk_up=v_gla_gk_up, gla_gk_bias=v_gla_gk_bias, gla_norm_w=v_gla_norm_w,
                 rwkv_mu=v_rwkv_mu, rwkv_w0=v_rwkv_w0, rwkv_w_up=v_rwkv_w_up, rwkv_a0=v_rwkv_a0, rwkv_a_up=v_rwkv_a_up,
                 rwkv_k_k=v_rwkv_k_k, rwkv_k_a=v_rwkv_k_a, rwkv_r_k=v_rwkv_r_k, rwkv_ln_w=v_rwkv_ln_w, rwkv_ln_b=v_rwkv_ln_b,
                 w_out0=v_w_out0, w_in1=v_w_in1, b_in1=v_b_in1, attn_sinks=v_attn_sinks, w_out1=v_w_out1, b_out1=v_b_out1,
                 final_norm_w=v_final_norm_w)
    names = list(weights)
    big = ["w_in0", "w_out0", "w_in1", "w_out1"]
    small_sharded = ["gla_gk_up", "rwkv_w_up", "rwkv_a_up", "b_in1", "b_out1"]
    replicated = [n for n in names if n not in big and n not in small_sharded]

    xs = x[0]
    tgt = loss_target[0]
    t = xs.shape[0]

    def view(w):
        shape = tuple(w.shape[-2:]) if w.ndim >= 2 else (1, w.shape[0])
        return shape, ("rows" if shape[0] > 1 and shape[1] <= LANES else "flat")

    def layout(ns, row0=0):
        specs = []
        for n in ns:
            shape, mode = view(weights[n])
            specs.append((shape, mode, row0))
            row0 += _packed_rows(shape, mode)
        return specs, row0

    sh_specs, n_shard_rows = layout(small_sharded)
    rep_specs, loss_row = layout(replicated, n_shard_rows)
    sh_modes, rep_modes = [s[1] for s in sh_specs], [s[1] for s in rep_specs]

    small_shard_pack = _pack_small([weights[n].reshape(view(weights[n])[0]) for n in small_sharded], sh_modes)
    g_in0, g_small = _all_gather([w_in0[0].T.astype(BF16), small_shard_pack])
    w0t = _w0t_to_padded(g_in0.reshape(-1, D_MODEL))
    later_shards = [w_out0[0].astype(BF16), w_in1[0].T.astype(BF16), w_out1[0].astype(BF16)]
    gs = [_take_small(g_small, row0, shape, mode) for shape, mode, row0 in sh_specs]
    join_cols = lambda a: jnp.transpose(a, (1, 0, 2)).reshape(a.shape[1], -1)
    gk_up, w_up, a_up = join_cols(gs[0]), join_cols(gs[1]), join_cols(gs[2])
    b_in, b_out = gs[3].reshape(1, -1), gs[4].reshape(1, -1)

    gk_up_p = _pad_to(gk_up, rows=128)
    mu = rwkv_mu
    rwkv_params = [mu[:, 0:512], mu[:, 512:1024], mu[:, 1024:1536], _pad_to(mu[:, 1536:1600], cols=128), _pad_to(mu[:, 1600:1664], cols=128),
                   rwkv_w0, _pad_to(w_up, rows=128), rwkv_a0, _pad_to(a_up, rows=128), rwkv_k_k, rwkv_k_a, rwkv_r_k.reshape(1, 512),
                   rwkv_ln_w, rwkv_ln_b]
    bq, bk, bv = b_in[:, :1024], b_in[:, 1024:1280], b_in[:, 1280:1536]
    cos, sin = _rope_tables(t)
    nw0, nw1, fw = norm_w[0:1], norm_w[1:2], final_norm_w.reshape(1, D_MODEL)

    hn0 = _norm_fwd("norm0_fwd", xs, nw0)
    proj0 = _matmul("proj0", hn0, w0t, "nt", 512, 1024)
    o_a, gla_states = _gla_fwd(proj0, gk_up_p, gla_gk_bias, gla_norm_w)
    o_b, rwkv_states, rwkv_prevs, (g_out0, g_in1, g_out1) = _rwkv_fwd(proj0, rwkv_params, later_shards, ["gather"] * 3)
    wo0 = g_out0.reshape(1024, D_MODEL)
    w1t = _w1t_to_mine(g_in1.reshape(-1, D_MODEL))
    wo1 = g_out1.reshape(1024, D_MODEL)
    og0 = _gate_fwd("gate0_fwd", [o_a, o_b], proj0)
    y0 = _matmul("out0", og0, wo0, "nn", 512, 1024)
    h1, hn1 = _norm_fwd("norm1_fwd", xs, nw1, y0)
    proj1 = _matmul("proj1", hn1, w1t, "nt", 512, 1280)
    o_c, kst, vst = _swa_fwd(proj1, cos, sin, bq, bk, bv, attn_sinks)
    og1 = _gate_fwd("gate1_fwd", [o_c], proj1)
    y1 = _matmul("out1", og1, wo1, "nn", 512, 1024)
    dh2, loss_part, d_b_out, d_fw = _top(h1, y1, b_out, fw, tgt)

    dog1 = _matmul("out1_dx", dh2, wo1, "nt", 512, 1024)
    d_wo1 = _matmul("out1_dw", og1, dh2, "tn", 512, 512)
    d_oc, d_gate1 = _gate_bwd("gate1_bwd", [o_c], proj1, dog1)
    dq, dk, dv, d_bq, d_bk, d_bv, d_sinks = _swa_bwd(proj1, cos, sin, bq, bk, bv, attn_sinks, kst, vst, d_oc)
    dproj1 = jnp.concatenate([d_gate1, dq, dk, dv], axis=1).astype(BF16)
    dhn1 = _matmul("proj1_dx", dproj1, w1t, "nn", 512, 1024)
    d_w1t = _matmul("proj1_dw", dproj1, hn1, "tn", 512, 1024)
    dh1, d_nw1 = _norm_bwd("norm1_bwd", h1, nw1, dhn1, dh2)
    dog0 = _matmul("out0_dx", dh1, wo0, "nt", 512, 1024)
    d_wo0 = _matmul("out0_dw", og0, dh1, "tn", 512, 512)
    d_oa, d_ob, d_gate0 = _gate_bwd("gate0_bwd", [o_a, o_b], proj0, dog0)
    dgq, dgk, dgv, dglow, d_gk_up, d_gk_bias, d_gla_nw = _gla_bwd(proj0, gk_up_p, gla_gk_bias, gla_norm_w, gla_states, d_oa)
    row_blocks = lambda a: a.astype(BF16).reshape(N_DEV, -1, D_MODEL)
    early = [row_blocks(_w1t_from_mine(d_w1t)), row_blocks(d_wo1), row_blocks(d_wo0)]
    (dr, dkk, dvv, dxw, dxa), d_rp, (r_in1, r_out1, r_out0) = _rwkv_bwd(
        proj0, rwkv_params, rwkv_states, rwkv_prevs, d_ob, early, ["scatter"] * 3)
    dproj0 = jnp.concatenate([d_gate0, dgv, dr, dkk, dvv, dgq, dgk, dglow, dxw, dxa, jnp.zeros((t, 128), F32)], axis=1).astype(BF16)
    dhn0 = _matmul("proj0_dx", dproj0, w0t, "nn", 512, 1024)
    d_w0t = _matmul("proj0_dw", dproj0, hn0, "tn", 512, 1024)
    grad_x, d_nw0 = _norm_bwd("norm0_bwd", xs, nw0, dhn0, dh1)

    contrib = dict(
        norm_w=jnp.concatenate([d_nw0, d_nw1], axis=0), gla_gk_bias=d_gk_bias, gla_norm_w=d_gla_nw,
        rwkv_mu=jnp.concatenate([d_rp[0], d_rp[1], d_rp[2], d_rp[3][:, :64], d_rp[4][:, :64]], axis=1),
        rwkv_w0=d_rp[5], rwkv_a0=d_rp[7], rwkv_k_k=d_rp[9], rwkv_k_a=d_rp[10], rwkv_r_k=d_rp[11].reshape(RWKV_HEADS, RWKV_N),
        rwkv_ln_w=d_rp[12], rwkv_ln_b=d_rp[13], attn_sinks=d_sinks, final_norm_w=d_fw)
    rep_pack = _pack_small([contrib[n] for n in replicated] + [loss_part[:, :1]], rep_modes + ["flat"])

    d_w0 = _w0t_from_padded(d_w0t)
    d_b_in = jnp.concatenate([d_bq, d_bk, d_bv], axis=1)
    full_small = [d_gk_up[:16], d_rp[6][:64], d_rp[8][:64], d_b_in, d_b_out]
    split_cols = lambda a: jnp.transpose(a.reshape(a.shape[0], N_DEV, -1), (1, 0, 2))
    small_parts = [split_cols(a) for a in full_small]
    small_pack = _pack_small(small_parts, sh_modes, lead=True)
    r_in0, r_small, r_rep = _exchange([row_blocks(d_w0), small_pack, rep_pack], ["scatter", "scatter", "gather"])

    res = {}
    res["w_in0"] = tuple(a.T[None] for a in _adamw("adamw_w_in0", w_in0[0].T, r_in0, m_w_in0[0].T, v_w_in0[0].T, 256))
    res["w_out0"] = tuple(a[None] for a in _adamw("adamw_w_out0", w_out0[0], r_out0, m_w_out0[0], v_w_out0[0], 256))
    res["w_in1"] = tuple(a.T[None] for a in _adamw("adamw_w_in1", w_in1[0].T, r_in1, m_w_in1[0].T, v_w_in1[0].T, 256))
    res["w_out1"] = tuple(a[None] for a in _adamw("adamw_w_out1", w_out1[0], r_out1, m_w_out1[0], v_w_out1[0], 256))
    small_names = small_sharded + replicated
    slots = jnp.concatenate([r_small, r_rep], axis=1)
    as_2d = lambda a: a.reshape(1, -1) if a.ndim == 1 else a
    small_res, loss_row_out = _adamw_small(slots, sh_specs + rep_specs, [as_2d(weights[n]) for n in small_names],
                                           [as_2d(moms[n]) for n in small_names], [as_2d(vars_[n]) for n in small_names], loss_row)
    for n, vals in zip(small_names, small_res):
        res[n] = tuple(val.reshape(weights[n].shape) for val in vals)
    loss = loss_row_out[0, 0]
    return (loss, grad_x[None], *[res[n][0] for n in names], *[res[n][1] for n in names],
            *[res[n][2] for n in names], *[res[n][3] for n in names])
```

```python
import functools

import jax
import jax.numpy as jnp
from jax import lax
from jax.experimental import pallas as pl
from jax.experimental.pallas import tpu as pltpu

F32 = jnp.float32
BF16 = jnp.bfloat16
HI = lax.Precision.HIGHEST

D_MODEL = 1024
NORM_EPS = 1e-5
GLA_HEADS, GLA_DK, GLA_DV = 4, 64, 128
GLA_NORMALIZER = 16.0
GLA_CHUNK = 64
GLA_STEP = 256
RWKV_HEADS, RWKV_N = 8, 64
RWKV_LN_EPS = 64e-5
RWKV_CHUNK = 128
SWA_Q_HEADS, SWA_KV_HEADS, SWA_GROUP, SWA_HD = 16, 4, 4, 64
WINDOW = 128
SWA_STEP = 256
ROPE_THETA = 500000.0
NEG = -1e30
N_DEV = 8
LANES = 128

ADAM_LR, ADAM_B1, ADAM_B2, ADAM_EPS, ADAM_WD, ADAM_STEP = 0.001, 0.9, 0.999, 1e-08, 0.01, 10

N0P = 4096
C0 = dict(gate=(0, 1024), gv=(1024, 512), r=(1536, 512), k=(2048, 512), v=(2560, 512), gq=(3072, 256), gk=(3328, 256),
          glow=(3584, 128), xw=(3712, 128), xa=(3840, 128))
N1P = 2560
C1 = dict(gate=(0, 1024), q=(1024, 1024), k=(2048, 256), v=(2304, 256))

VMEM_LIMIT = 56 * 1024 * 1024

P_LORA = 1
P_GLA = 1
P_RWKV_G = 2
P_RWKV = 1
P_SWA = 1
P_ROPE = 3


def _cparams(sem=None):
    return pltpu.CompilerParams(dimension_semantics=sem, vmem_limit_bytes=VMEM_LIMIT)


DIMS = dict(nn=(((1,), (0,)), ((), ())), nt=(((1,), (1,)), ((), ())), tn=(((0,), (0,)), ((), ())))


def _split_bf16(a):
    hi = a.astype(BF16)
    return hi, (a - hi.astype(F32)).astype(BF16)


def _dot(a, b, mode, passes):
    dg = lambda p, q: lax.dot_general(p, q, DIMS[mode], preferred_element_type=F32)
    if passes == 1:
        return dg(a.astype(BF16), b.astype(BF16))
    if passes == 2:
        ah, (bh, bl) = a.astype(BF16), _split_bf16(b)
        return dg(ah, bh) + dg(ah, bl)
    if passes == 3:
        (ah, al), (bh, bl) = _split_bf16(a), _split_bf16(b)
        return dg(ah, bh) + dg(al, bh) + dg(ah, bl)
    return lax.dot_general(a, b, DIMS[mode], precision=HI, preferred_element_type=F32)


@functools.partial(jax.custom_vjp, nondiff_argnums=(2, 3))
def mmx(a, b, mode, passes):
    return _dot(a, b, mode, passes)


def _mmx_fwd(a, b, mode, passes):
    return _dot(a, b, mode, passes), (a, b)


def _mmx_bwd(mode, passes, res, g):
    a, b = res
    if mode == "nn":
        return _dot(g, b, "nt", passes), _dot(a, g, "tn", passes)
    if mode == "nt":
        return _dot(g, b, "nn", passes), _dot(g, a, "tn", passes)
    return _dot(b, g, "nt", passes), _dot(a, g, "nn", passes)


mmx.defvjp(_mmx_fwd, _mmx_bwd)


def _tri_dot(tri, x):
    t = tri.astype(BF16)
    x1 = x.astype(BF16)
    r1 = x - x1.astype(F32)
    x2 = r1.astype(BF16)
    x3 = (r1 - x2.astype(F32)).astype(BF16)
    dg = lambda q: jnp.dot(t, q, preferred_element_type=F32)
    return dg(x1) + dg(x2) + dg(x3)


@jax.custom_vjp
def cumsum_rows(x):
    return _tri_dot(tril_ones(x.shape[0]), x)


def _cumsum_fwd(x):
    return cumsum_rows(x), None


def _cumsum_bwd(_, g):
    i, j = _iota2(g.shape[0], g.shape[0])
    return (_tri_dot(jnp.where(i <= j, 1.0, 0.0).astype(F32), g),)


cumsum_rows.defvjp(_cumsum_fwd, _cumsum_bwd)


def _head_dot(x):
    i, j = _iota2(LANES, LANES)
    shift = RWKV_N.bit_length() - 1
    same = jnp.where(jnp.right_shift(i, shift) == jnp.right_shift(j, shift), 1.0, 0.0).astype(F32)
    return jnp.concatenate([_ones_right(x[:, g * LANES:(g + 1) * LANES], same) for g in range(x.shape[1] // LANES)], axis=1)


def _ones_right(x, ones):
    t = ones.astype(BF16)
    x1 = x.astype(BF16)
    r1 = x - x1.astype(F32)
    x2 = r1.astype(BF16)
    x3 = (r1 - x2.astype(F32)).astype(BF16)
    dg = lambda q: jnp.dot(q, t, preferred_element_type=F32)
    return dg(x1) + dg(x2) + dg(x3)


@jax.custom_vjp
def head_sum(x):
    return _head_dot(x)


def _head_sum_fwd(x):
    return head_sum(x), None


def _head_sum_bwd(_, g):
    return (_head_dot(g),)


head_sum.defvjp(_head_sum_fwd, _head_sum_bwd)


def cat_rows(*xs):
    return jnp.concatenate(xs, axis=0)


def _iota2(n, m):
    return lax.broadcasted_iota(jnp.int32, (n, m), 0), lax.broadcasted_iota(jnp.int32, (n, m), 1)


def tril_ones(c, strict=False):
    i, j = _iota2(c, c)
    return jnp.where((i > j) if strict else (i >= j), 1.0, 0.0).astype(F32)


def row_of(x, r):
    i = lax.broadcasted_iota(jnp.int32, x.shape, 0)
    return jnp.sum(jnp.where(i == r, x, 0.0), axis=0, keepdims=True)


@jax.custom_vjp
def shift_rows(x, prev):
    r = lax.broadcasted_iota(jnp.int32, x.shape, 0)
    return jnp.where(r == 0, prev, pltpu.roll(x, 1, 0))


def _shift_fwd(x, prev):
    return shift_rows(x, prev), None


def _shift_bwd(_, g):
    c = g.shape[0]
    r = lax.broadcasted_iota(jnp.int32, g.shape, 0)
    return jnp.where(r == c - 1, 0.0, pltpu.roll(g, c - 1, 0)), row_of(g, 0)


shift_rows.defvjp(_shift_fwd, _shift_bwd)


def log_sigmoid(x):
    return jnp.minimum(x, 0.0) - jnp.log(1.0 + jnp.exp(-jnp.abs(x)))


def softplus(x):
    return jnp.maximum(x, 0.0) + jnp.log(1.0 + jnp.exp(-jnp.abs(x)))


def sigmoid(x):
    return 1.0 / (1.0 + jnp.exp(-x))


def rms(x, w, eps=NORM_EPS):
    return x * lax.rsqrt(jnp.mean(x * x, axis=-1, keepdims=True) + eps) * w


def gla_chunk(state, toks, params):
    q, k, v, glow = toks
    gk_up, bias, norm_w = params
    c = GLA_CHUNK
    subs, heads = range(glow.shape[0] // c), range(GLA_HEADS)
    rows = lambda x, j: x[j * c:(j + 1) * c]
    hk = lambda x, h: x[:, h * GLA_DK:(h + 1) * GLA_DK]
    hv = lambda x, h: x[:, h * GLA_DV:(h + 1) * GLA_DV]
    ltri = tril_ones(c)
    g = log_sigmoid(mmx(glow, gk_up, "nn", P_LORA) + bias) / GLA_NORMALIZER
    b = [cumsum_rows(rows(g, j)) for j in subs]
    ref = [lax.stop_gradient(row_of(b[j], c // 2)) for j in subs]
    last = [row_of(b[j], c - 1) for j in subs]
    ql = [rows(q, j) * (GLA_DK ** -0.5) * jnp.exp(b[j] - ref[j]) for j in subs]
    kr = [rows(k, j) * jnp.exp(ref[j] - b[j]) for j in subs]
    kl = [rows(k, j) * jnp.exp(last[j] - b[j]) for j in subs]
    vj = [rows(v, j) for j in subs]
    e_ref, e_last = [jnp.exp(x) for x in ref], [jnp.exp(x) for x in last]
    att = [[mmx(hk(ql[j], h), hk(kr[j], h), "nt", P_GLA) * ltri for h in heads] for j in subs]
    o_in = [[mmx(att[j][h], hv(vj[j], h), "nn", P_GLA) for h in heads] for j in subs]
    kv = [[mmx(hv(vj[j], h), hk(kl[j], h), "tn", P_GLA) for h in heads] for j in subs]
    o = []
    for j in subs:
        o.append([o_in[j][h] + mmx(hk(ql[j], h), state[h] * hk(e_ref[j], h), "nt", P_GLA) for h in heads])
        state = [state[h] * hk(e_last[j], h) + kv[j][h] for h in heads]
    o = [[x * lax.rsqrt(jnp.mean(x * x, axis=-1, keepdims=True) + NORM_EPS) * norm_w for x in oj] for oj in o]
    return cat_rows(*[jnp.concatenate(oj, axis=1) for oj in o]), state


SOLVE_BLOCK = 128


def solve_unit_lower(ps, ws):
    n = ps[0].shape[0]
    heads = range(len(ps))
    if n > SOLVE_BLOCK:
        half = n // 2
        top = solve_unit_lower([p[:half, :half] for p in ps], [w[:half] for w in ws])
        rest = [ws[h][half:] + mmx(ps[h][half:, :half], top[h], "nn", P_RWKV) for h in heads]
        bottom = solve_unit_lower([p[half:, half:] for p in ps], rest)
        return [cat_rows(top[h], bottom[h]) for h in heads]
    u, p = ws, ps
    levels = max(1, (n - 1).bit_length())
    for it in range(levels):
        if it + 1 < levels:
            y = [mmx(p[h], jnp.concatenate([p[h], u[h]], axis=1), "nn", P_RWKV) for h in heads]
            u = [u[h] + y[h][:, n:] for h in heads]
            p = [y[h][:, :n] for h in heads]
        else:
            u = [u[h] + mmx(p[h], u[h], "nn", P_RWKV) for h in heads]
    return u


def rwkv_chunk(state, toks, params):
    S, pr, pk, pv, pxw, pxa = state
    r_, k_, v_, xw_, xa_ = toks
    mu_r, mu_k, mu_v, mu_xw, mu_xa, w0, w_up, a0, a_up, k_k, k_a, r_k, ln_w, ln_b = params
    c, n = xw_.shape[0], RWKV_N
    heads = range(RWKV_HEADS)
    hs = lambda x, h: x[:, h * n:(h + 1) * n]
    ltri = tril_ones(c)
    stri = tril_ones(c, strict=True)

    def lerp(x, prev, mu):
        return x + (shift_rows(x, prev) - x) * mu

    xw = jnp.tanh(lerp(xw_, pxw, mu_xw))
    xa = lerp(xa_, pxa, mu_xa)
    r = lerp(r_, pr, mu_r)
    k = lerp(k_, pk, mu_k)
    v = lerp(v_, pv, mu_v)
    w = -softplus(-(w0 + mmx(xw, w_up, "nn", P_LORA))) - 0.5
    lw = -jnp.exp(w)
    asig = sigmoid(a0 + mmx(xa, a_up, "nn", P_LORA))
    kk = k * k_k
    kk = kk / jnp.maximum(jnp.sqrt(head_sum(kk * kk)), 1e-12)
    k2 = k * (1.0 + (asig - 1.0) * k_a)
    b = kk * asig
    cum = cumsum_rows(lw)
    ref = lax.stop_gradient(row_of(cum, c // 2))
    last = row_of(cum, c - 1)
    at = -kk * jnp.exp(cum - lw - ref)
    rt = r * jnp.exp(cum - ref)
    e_out = jnp.exp(ref - cum)
    bt, kt = b * e_out, k2 * e_out
    e_tail = jnp.exp(last - cum)
    bl, kl = b * e_tail, k2 * e_tail
    e_ref, e_last = jnp.exp(ref), jnp.exp(last)
    g = [mmx(cat_rows(hs(at, h), hs(rt, h)), cat_rows(hs(bt, h), hs(kt, h), S[h] * hs(e_ref, h)), "nt", P_RWKV_G) for h in heads]
    aab = [x[:c, :c] * stri for x in g]
    aak = [x[:c, c:2 * c] * stri for x in g]
    arb = [x[c:, :c] * ltri for x in g]
    ark = [x[c:, c:2 * c] * ltri for x in g]
    av = [mmx(cat_rows(aak[h], ark[h]), hs(v, h), "nn", P_RWKV) for h in heads]
    u = solve_unit_lower(aab, [g[h][:c, 2 * c:] + av[h][:c] for h in heads])
    o = [g[h][c:, 2 * c:] + av[h][c:] + mmx(arb[h], u[h], "nn", P_RWKV) for h in heads]
    s1 = [S[h] * hs(e_last, h) + mmx(cat_rows(u[h], hs(v, h)), cat_rows(hs(bl, h), hs(kl, h)), "tn", P_RWKV) for h in heads]
    o = jnp.concatenate(o, axis=1)
    d = o - head_sum(o) * (1.0 / n)
    var = head_sum(d * d) * (1.0 / n)
    o = d * lax.rsqrt(var + RWKV_LN_EPS) * ln_w + ln_b + head_sum(r * k2 * r_k) * v
    new_state = (s1, row_of(r_, c - 1), row_of(k_, c - 1), row_of(v_, c - 1), row_of(xw_, c - 1), row_of(xa_, c - 1))
    return o, new_state


def rope_mat():
    i, j = _iota2(SWA_HD, SWA_HD)
    plus = (j >= 8) & (j < 16) & (i == j - 8)
    minus = (j < 8) & (i == j + 8)
    return jnp.where(plus, 1.0, 0.0).astype(F32) - jnp.where(minus, 1.0, 0.0).astype(F32)


def swa_chunk(state, toks, params, first):
    kprev, vprev = state
    q_, k_, v_, cos, sin = toks
    bq, bk, bv, sinks = params
    c, ng = WINDOW, SWA_GROUP
    n_sub = cos.shape[0] // c
    units = [(j, g) for j in range(n_sub) for g in range(SWA_KV_HEADS)]
    rows = lambda x, j: x[j * c:(j + 1) * c]
    hs = lambda g: range(g * ng, (g + 1) * ng)
    rm = rope_mat()
    qi, kj = _iota2(ng * c, 2 * c)
    qpos = qi & (c - 1)
    cur_ok = (kj >= c) & (qpos >= kj - c)
    prev_ok = (kj < c) & (kj > qpos)
    ok = [cur_ok | (prev_ok & jnp.logical_not(first))] + [cur_ok | prev_ok] * (n_sub - 1)
    cs, sn = [rows(cos, j) for j in range(n_sub)], [rows(sin, j) for j in range(n_sub)]
    cs_g, sn_g = [cat_rows(*[x] * ng) for x in cs], [cat_rows(*[x] * ng) for x in sn]

    def rope(x, cos_, sin_):
        return x * cos_ + mmx(x, rm, "nn", P_ROPE) * sin_

    k = {(j, g): rope(rows(k_[g], j) + bk[g], cs[j], sn[j]) for j, g in units}
    v = {(j, g): rows(v_[g], j) + bv[g] for j, g in units}
    q = {(j, g): rope(cat_rows(*[rows(q_[h], j) + bq[h] for h in hs(g)]), cs_g[j], sn_g[j]) * (SWA_HD ** -0.5) for j, g in units}
    kp = lambda j, g: kprev[g] if j == 0 else k[(j - 1, g)]
    vp = lambda j, g: vprev[g] if j == 0 else v[(j - 1, g)]
    s = {(j, g): jnp.where(ok[j], mmx(q[(j, g)], cat_rows(kp(j, g), k[(j, g)]), "nt", P_SWA), NEG) for j, g in units}
    sink = [cat_rows(*[jnp.broadcast_to(sinks[h], (c, 1)) for h in hs(g)]) for g in range(SWA_KV_HEADS)]
    m = {(j, g): lax.stop_gradient(jnp.maximum(jnp.max(s[(j, g)], axis=-1, keepdims=True), sink[g])) for j, g in units}
    p = {u: jnp.exp(s[u] - m[u]) for u in units}
    ones = jnp.ones((2 * c, SWA_HD), F32)
    pv = {(j, g): mmx(p[(j, g)], jnp.concatenate([cat_rows(vp(j, g), v[(j, g)]), ones], axis=1), "nn", P_SWA) for j, g in units}
    o = {(j, g): pv[(j, g)][:, :SWA_HD] / (pv[(j, g)][:, SWA_HD:] + jnp.exp(sink[g] - m[(j, g)])) for j, g in units}
    outs = [cat_rows(*[o[(j, g)][i * c:(i + 1) * c] for j in range(n_sub)]) for g in range(SWA_KV_HEADS) for i in range(ng)]
    last = n_sub - 1
    return outs, ([k[(last, g)] for g in range(SWA_KV_HEADS)], [v[(last, g)] for g in range(SWA_KV_HEADS)])


def _heads(ref, n, w, rows=slice(None)):
    return [ref[rows, h * w:(h + 1) * w] for h in range(n)]


def _put_heads(ref, vals, w, rows=slice(None), add=False):
    for h, val in enumerate(vals):
        if add:
            ref[rows, h * w:(h + 1) * w] += val
        else:
            ref[rows, h * w:(h + 1) * w] = val


def _col(block_w, name, table):
    off, w = table[name]
    assert off % block_w == 0 and w % block_w == 0
    return off // block_w


def _tok_spec(c, w, colblock, n=None):
    if n is None:
        return pl.BlockSpec((c, w), lambda i: (i, colblock))
    return pl.BlockSpec((c, w), lambda i: (n - 1 - i, colblock))


def _full_spec(shape):
    return pl.BlockSpec(shape, lambda i: (0,) * len(shape))


def _matmul(name, a, b, mode, tm, tn, out_dtype=F32):
    (m, kd) = (a.shape[1], a.shape[0]) if mode == "tn" else a.shape
    n = b.shape[0] if mode == "nt" else b.shape[1]
    assert m % tm == 0 and n % tn == 0
    a_spec = pl.BlockSpec((kd, tm), lambda j, i: (0, i)) if mode == "tn" else pl.BlockSpec((tm, kd), lambda j, i: (i, 0))
    b_spec = pl.BlockSpec((tn, kd), lambda j, i: (j, 0)) if mode == "nt" else pl.BlockSpec((kd, tn), lambda j, i: (0, j))

    def body(a_ref, b_ref, o_ref):
        o_ref[...] = lax.dot_general(a_ref[...].astype(BF16), b_ref[...].astype(BF16), DIMS[mode],
                                     preferred_element_type=F32).astype(out_dtype)

    return pl.pallas_call(
        body, name=name, grid=(n // tn, m // tm), in_specs=[a_spec, b_spec],
        out_specs=pl.BlockSpec((tm, tn), lambda j, i: (i, j)), out_shape=jax.ShapeDtypeStruct((m, n), out_dtype),
        compiler_params=_cparams(("arbitrary", "arbitrary")))(a, b)


TOK_TILE = 512


def _norm_fwd(name, x, w):
    t, d = x.shape
    tile = pl.BlockSpec((TOK_TILE, d), lambda i: (i, 0))

    def body(x_ref, w_ref, hn_ref):
        hn_ref[...] = rms(x_ref[...], w_ref[...]).astype(BF16)

    return pl.pallas_call(body, name=name, grid=(t // TOK_TILE,), in_specs=[tile, _full_spec((1, d))], out_specs=tile,
                          out_shape=jax.ShapeDtypeStruct((t, d), BF16), compiler_params=_cparams(("arbitrary",)))(x, w)


def _matmul_fused(name, a, b, mode, tiles, rows, outs, sums, epilogue):
    m, kd = a.shape
    n = b.shape[1] if mode == "nn" else b.shape[0]
    tm = TOK_TILE
    nt_, nr, no = len(tiles), len(rows), len(outs)

    def body(*refs):
        a_ref, b_ref = refs[:2]
        tile_refs, row_refs = refs[2:2 + nt_], refs[2 + nt_:2 + nt_ + nr]
        out_refs, sum_refs = refs[2 + nt_ + nr:2 + nt_ + nr + no], refs[2 + nt_ + nr + no:]

        @pl.when(pl.program_id(0) == 0)
        def _():
            for ref in sum_refs:
                ref[...] = jnp.zeros_like(ref)

        acc = lax.dot_general(a_ref[...].astype(BF16), b_ref[...].astype(BF16), DIMS[mode], preferred_element_type=F32)
        res = epilogue(acc, *[r[...] for r in tile_refs], *[r[...] for r in row_refs])
        for ref, val in zip(out_refs, res[:no]):
            ref[...] = val.astype(ref.dtype)
        for ref, val in zip(sum_refs, res[no:]):
            ref[...] += val

    in_specs = [pl.BlockSpec((tm, kd), lambda i: (i, 0)), _full_spec(b.shape)]
    in_specs += [pl.BlockSpec((tm, w), functools.partial(lambda i, cb: (i, cb), cb=cb)) for _, w, cb in tiles]
    in_specs += [_full_spec(r.shape) for r in rows]
    out_specs = [pl.BlockSpec((tm, w), lambda i: (i, 0)) for w, _ in outs] + [_full_spec((1, w)) for w in sums]
    out_shape = [jax.ShapeDtypeStruct((m, w), dt) for w, dt in outs] + [jax.ShapeDtypeStruct((1, w), F32) for w in sums]
    return pl.pallas_call(body, name=name, grid=(m // tm,), in_specs=in_specs, out_specs=out_specs, out_shape=out_shape,
                          compiler_params=_cparams(("arbitrary",)))(a, b, *[t[0] for t in tiles], *rows)


def _resid_norm(y, x, w):
    h = x + y
    return h, rms(h, w)


def _norm_back(dhn, h, dres, w):
    _, vjp = jax.vjp(rms, h, w)
    dh, dw = vjp(dhn)
    return dh + dres, dw


def _gate_back(dog, *o_and_gate):
    outs, g = o_and_gate[:-1], o_and_gate[-1]
    s = sigmoid(g)
    silu, dsilu = g * s, s * (1.0 + g * (1.0 - s))
    d_outs, c = [], 0
    for o in outs:
        w = o.shape[1]
        d_outs.append(dog[:, c:c + w] * silu[:, c:c + w])
        c += w
    o_all = outs[0] if len(outs) == 1 else jnp.concatenate(outs, axis=1)
    return (*d_outs, dog * o_all * dsilu)


def _loss_head(y1, h1, target, b_out, fw):
    def f(h2, w):
        err = rms(h2, w) - target
        return 0.5 * jnp.sum(jnp.mean(err * err, axis=-1, keepdims=True), axis=0, keepdims=True)

    loss, vjp = jax.vjp(f, h1 + y1 + b_out, fw)
    dh2, dfw = vjp(jnp.ones((1, 1), F32))
    return dh2, jnp.broadcast_to(loss, (1, LANES)), jnp.sum(dh2, axis=0, keepdims=True), dfw


def _gate_fwd(name, outs, proj):
    t = proj.shape[0]
    widths = [o.shape[1] for o in outs]
    n = len(outs)

    def body(*refs):
        o_refs, g_ref, og_ref = refs[:n], refs[n], refs[n + 1]
        c = 0
        for o_ref, w in zip(o_refs, widths):
            g = g_ref[:, c:c + w]
            og_ref[:, c:c + w] = (o_ref[...] * (g * sigmoid(g))).astype(BF16)
            c += w

    in_specs = [pl.BlockSpec((TOK_TILE, w), lambda i: (i, 0)) for w in widths] + [pl.BlockSpec((TOK_TILE, 1024), lambda i: (i, 0))]
    return pl.pallas_call(body, name=name, grid=(t // TOK_TILE,), in_specs=in_specs,
                          out_specs=pl.BlockSpec((TOK_TILE, 1024), lambda i: (i, 0)),
                          out_shape=jax.ShapeDtypeStruct((t, 1024), BF16), compiler_params=_cparams(("arbitrary",)))(*outs, proj)


def _gla_load(q_ref, k_ref, v_ref, gl_ref, up_ref, bias_ref, nw_ref):
    toks = (q_ref[...], k_ref[...], v_ref[...], gl_ref[...])
    params = (up_ref[...], bias_ref[...], nw_ref[...])
    return toks, params


def _gla_specs(c, n=None):
    toks = [_tok_spec(c, 256, _col(256, "gq", C0), n), _tok_spec(c, 256, _col(256, "gk", C0), n),
            _tok_spec(c, 512, _col(512, "gv", C0), n), _tok_spec(c, 128, _col(128, "glow", C0), n)]
    params = [_full_spec((128, 256)), _full_spec((1, 256)), _full_spec((1, 128))]
    return toks, params


def _gla_fwd(proj0, gk_up, gk_bias, norm_w):
    t = proj0.shape[0]
    c = GLA_STEP
    nc = t // c
    toks_s, params_s = _gla_specs(c)

    def body(q_ref, k_ref, v_ref, gl_ref, up_ref, bias_ref, nw_ref, o_ref, st_ref, s_scr):
        @pl.when(pl.program_id(0) == 0)
        def _():
            s_scr[...] = jnp.zeros_like(s_scr)

        st_ref[...] = s_scr[...]
        toks, params = _gla_load(q_ref, k_ref, v_ref, gl_ref, up_ref, bias_ref, nw_ref)
        state = [s_scr[h * GLA_DV:(h + 1) * GLA_DV, :] for h in range(GLA_HEADS)]
        o_ref[...], new = gla_chunk(state, toks, params)
        for h in range(GLA_HEADS):
            s_scr[h * GLA_DV:(h + 1) * GLA_DV, :] = new[h]

    return pl.pallas_call(
        body, name="gla_fwd", grid=(nc,), in_specs=toks_s + params_s,
        out_specs=(_tok_spec(c, 512, 0), pl.BlockSpec((512, GLA_DK), lambda i: (i, 0))),
        out_shape=(jax.ShapeDtypeStruct((t, 512), F32), jax.ShapeDtypeStruct((nc * 512, GLA_DK), F32)),
        scratch_shapes=[pltpu.VMEM((512, GLA_DK), F32)], compiler_params=_cparams(("arbitrary",)))(
            proj0, proj0, proj0, proj0, gk_up, gk_bias, norm_w)


def _gla_bwd(proj0, gk_up, gk_bias, norm_w, states, do):
    t = proj0.shape[0]
    c = GLA_STEP
    nc = t // c
    toks_s, params_s = _gla_specs(c, nc)

    def body(q_ref, k_ref, v_ref, gl_ref, up_ref, bias_ref, nw_ref, st_ref, do_ref,
             dq_ref, dk_ref, dv_ref, dgl_ref, dup_ref, dbias_ref, dnw_ref, ds_scr):
        @pl.when(pl.program_id(0) == 0)
        def _():
            ds_scr[...] = jnp.zeros_like(ds_scr)
            dup_ref[...] = jnp.zeros_like(dup_ref)
            dbias_ref[...] = jnp.zeros_like(dbias_ref)
            dnw_ref[...] = jnp.zeros_like(dnw_ref)

        toks, params = _gla_load(q_ref, k_ref, v_ref, gl_ref, up_ref, bias_ref, nw_ref)
        rows = lambda h: slice(h * GLA_DV, (h + 1) * GLA_DV)
        state = [st_ref[rows(h), :] for h in range(GLA_HEADS)]
        _, vjp = jax.vjp(gla_chunk, state, toks, params)
        dstate_in = [ds_scr[rows(h), :] for h in range(GLA_HEADS)]
        dstate, (dq_ref[...], dk_ref[...], dv_ref[...], dgl_ref[...]), (dup, dbias, dnw) = vjp((do_ref[...], dstate_in))
        dup_ref[...] += dup
        dbias_ref[...] += dbias
        dnw_ref[...] += dnw
        for h in range(GLA_HEADS):
            ds_scr[rows(h), :] = dstate[h]

    rev = lambda w: pl.BlockSpec((c, w), lambda i: (nc - 1 - i, 0))
    return pl.pallas_call(
        body, name="gla_bwd", grid=(nc,),
        in_specs=toks_s + params_s + [pl.BlockSpec((512, GLA_DK), lambda i: (nc - 1 - i, 0)), rev(512)],
        out_specs=(rev(256), rev(256), rev(512), rev(128), _full_spec((128, 256)), _full_spec((1, 256)), _full_spec((1, 128))),
        out_shape=(jax.ShapeDtypeStruct((t, 256), F32), jax.ShapeDtypeStruct((t, 256), F32), jax.ShapeDtypeStruct((t, 512), F32),
                   jax.ShapeDtypeStruct((t, 128), F32), jax.ShapeDtypeStruct((128, 256), F32), jax.ShapeDtypeStruct((1, 256), F32),
                   jax.ShapeDtypeStruct((1, 128), F32)),
        scratch_shapes=[pltpu.VMEM((512, GLA_DK), F32)], compiler_params=_cparams(("arbitrary",)))(
            proj0, proj0, proj0, proj0, gk_up, gk_bias, norm_w, states, do)


RWKV_PARAM_SHAPES = [(1, 512), (1, 512), (1, 512), (1, 128), (1, 128), (1, 512), (128, 512), (1, 512), (128, 512),
                     (1, 512), (1, 512), (1, 512), (1, 512), (1, 512)]
PREV_W = 1792
PREV_COLS = [slice(0, 512), slice(512, 1024), slice(1024, 1536), slice(1536, 1664), slice(1664, 1792)]


def _rwkv_load(r_ref, k_ref, v_ref, xw_ref, xa_ref, p_refs):
    toks = (r_ref[...], k_ref[...], v_ref[...], xw_ref[...], xa_ref[...])
    return toks, tuple(p[...] for p in p_refs)


def _rwkv_state(s_ref, prev_ref):
    n = RWKV_N
    S = [s_ref[h * n:(h + 1) * n, :] for h in range(RWKV_HEADS)]
    return (S,) + tuple(prev_ref[0:1, cols] for cols in PREV_COLS)


def _rwkv_put_state(s_ref, prev_ref, state):
    n = RWKV_N
    for h in range(RWKV_HEADS):
        s_ref[h * n:(h + 1) * n, :] = state[0][h]
    for cols, val in zip(PREV_COLS, state[1:]):
        prev_ref[0:1, cols] = val


def _rwkv_specs(c, n=None):
    toks = [_tok_spec(c, 512, _col(512, "r", C0), n), _tok_spec(c, 512, _col(512, "k", C0), n),
            _tok_spec(c, 512, _col(512, "v", C0), n), _tok_spec(c, 128, _col(128, "xw", C0), n),
            _tok_spec(c, 128, _col(128, "xa", C0), n)]
    return toks, [_full_spec(s) for s in RWKV_PARAM_SHAPES]


def _rwkv_fwd(proj0, params, comm, kinds):
    t = proj0.shape[0]
    c = RWKV_CHUNK
    nc = t // c
    toks_s, params_s = _rwkv_specs(c)
    npar, ncomm = len(params), len(comm)

    def body(*refs):
        tok_refs, p_refs = refs[:5], refs[5:5 + npar]
        comm_in = refs[5 + npar:5 + npar + ncomm]
        o_ref, st_ref, pst_ref = refs[5 + npar + ncomm:8 + npar + ncomm]
        comm_out = refs[8 + npar + ncomm:8 + npar + 2 * ncomm]
        s_scr, prev_scr = refs[8 + npar + 2 * ncomm:10 + npar + 2 * ncomm]
        sems = refs[10 + npar + 2 * ncomm:]
        i = pl.program_id(0)

        @pl.when(i == 0)
        def _():
            _comm_start(*_comm_copies(comm_in, comm_out, kinds, *sems))
            s_scr[...] = jnp.zeros_like(s_scr)
            prev_scr[...] = jnp.zeros_like(prev_scr)

        st_ref[...] = s_scr[...]
        pst_ref[...] = prev_scr[...]
        toks, prm = _rwkv_load(*tok_refs, p_refs)
        o_ref[...], new = rwkv_chunk(_rwkv_state(s_scr, prev_scr), toks, prm)
        _rwkv_put_state(s_scr, prev_scr, new)

        @pl.when(i == nc - 1)
        def _():
            _comm_wait(*_comm_copies(comm_in, comm_out, kinds, *sems))

    outs = pl.pallas_call(
        body, name="rwkv_fwd", grid=(nc,), in_specs=toks_s + params_s + [ANY] * ncomm,
        out_specs=[_tok_spec(c, 512, 0), pl.BlockSpec((512, RWKV_N), lambda i: (i, 0)), pl.BlockSpec((8, PREV_W), lambda i: (i, 0))]
        + [ANY] * ncomm,
        out_shape=[jax.ShapeDtypeStruct((t, 512), F32), jax.ShapeDtypeStruct((nc * 512, RWKV_N), F32),
                   jax.ShapeDtypeStruct((nc * 8, PREV_W), F32)] + _comm_out_shapes(comm, kinds),
        scratch_shapes=[pltpu.VMEM((512, RWKV_N), F32), pltpu.VMEM((8, PREV_W), F32)] + _comm_scratch(ncomm),
        compiler_params=_cparams(("arbitrary",)))(proj0, proj0, proj0, proj0, proj0, *params, *comm)
    return outs[0], outs[1], outs[2], outs[3:]


def _rwkv_bwd(proj0, params, states, prevs, do, comm, kinds):
    t = proj0.shape[0]
    c = RWKV_CHUNK
    nc = t // c
    toks_s, params_s = _rwkv_specs(c, nc)
    npar, ncomm = len(params), len(comm)

    def body(*refs):
        tok_refs, p_refs = refs[:5], refs[5:5 + npar]
        st_ref, pst_ref, do_ref = refs[5 + npar:8 + npar]
        comm_in = refs[8 + npar:8 + npar + ncomm]
        outs = refs[8 + npar + ncomm:]
        dtok_refs, dp_refs, comm_out = outs[:5], outs[5:5 + npar], outs[5 + npar:5 + npar + ncomm]
        ds_scr, dprev_scr = outs[5 + npar + ncomm:7 + npar + ncomm]
        sems = outs[7 + npar + ncomm:]
        i = pl.program_id(0)

        @pl.when(i == 0)
        def _():
            _comm_start(*_comm_copies(comm_in, comm_out, kinds, *sems))
            ds_scr[...] = jnp.zeros_like(ds_scr)
            dprev_scr[...] = jnp.zeros_like(dprev_scr)
            for dp in dp_refs:
                dp[...] = jnp.zeros_like(dp)

        toks, prm = _rwkv_load(*tok_refs, p_refs)
        _, vjp = jax.vjp(rwkv_chunk, _rwkv_state(st_ref, pst_ref), toks, prm)
        dstate, dtoks, dprm = vjp((do_ref[...], _rwkv_state(ds_scr, dprev_scr)))
        for ref, val in zip(dtok_refs, dtoks):
            ref[...] = val
        for ref, val in zip(dp_refs, dprm):
            ref[...] += val
        _rwkv_put_state(ds_scr, dprev_scr, dstate)

        @pl.when(i == nc - 1)
        def _():
            _comm_wait(*_comm_copies(comm_in, comm_out, kinds, *sems))

    rev = lambda w: pl.BlockSpec((c, w), lambda i: (nc - 1 - i, 0))
    outs = pl.pallas_call(
        body, name="rwkv_bwd", grid=(nc,),
        in_specs=toks_s + params_s + [pl.BlockSpec((512, RWKV_N), lambda i: (nc - 1 - i, 0)),
                                      pl.BlockSpec((8, PREV_W), lambda i: (nc - 1 - i, 0)), rev(512)] + [ANY] * ncomm,
        out_specs=[rev(512), rev(512), rev(512), rev(128), rev(128)] + params_s + [ANY] * ncomm,
        out_shape=[jax.ShapeDtypeStruct((t, w), F32) for w in (512, 512, 512, 128, 128)]
        + [jax.ShapeDtypeStruct(s, F32) for s in RWKV_PARAM_SHAPES] + _comm_out_shapes(comm, kinds),
        scratch_shapes=[pltpu.VMEM((512, RWKV_N), F32), pltpu.VMEM((8, PREV_W), F32)] + _comm_scratch(ncomm),
        compiler_params=_cparams(("arbitrary",)))(proj0, proj0, proj0, proj0, proj0, *params, states, prevs, do, *comm)
    return outs[:5], outs[5:5 + npar], outs[5 + npar:]


def _swa_load(q_ref, k_ref, v_ref, cos_ref, sin_ref, bq_ref, bk_ref, bv_ref, sk_ref):
    toks = (_heads(q_ref, 16, SWA_HD), _heads(k_ref, 4, SWA_HD), _heads(v_ref, 4, SWA_HD), cos_ref[...], sin_ref[...])
    params = (_heads(bq_ref, 16, SWA_HD), _heads(bk_ref, 4, SWA_HD), _heads(bv_ref, 4, SWA_HD), _heads(sk_ref, 16, 1))
    return toks, params


def _swa_specs(c, n=None):
    toks = [_tok_spec(c, 1024, _col(1024, "q", C1), n), _tok_spec(c, 256, _col(256, "k", C1), n),
            _tok_spec(c, 256, _col(256, "v", C1), n), _tok_spec(c, SWA_HD, 0, n), _tok_spec(c, SWA_HD, 0, n)]
    params = [_full_spec((1, 1024)), _full_spec((1, 256)), _full_spec((1, 256)), _full_spec((1, 16))]
    return toks, params


def _swa_fwd(proj1, cos, sin, bq, bk, bv, sinks):
    t = proj1.shape[0]
    c = SWA_STEP
    nb = t // c
    toks_s, params_s = _swa_specs(c)
    state_spec = pl.BlockSpec((WINDOW, 256), lambda i: (i, 0))

    def body(q_ref, k_ref, v_ref, cos_ref, sin_ref, bq_ref, bk_ref, bv_ref, sk_ref, o_ref, kst_ref, vst_ref, k_scr, v_scr):
        first = pl.program_id(0) == 0

        @pl.when(first)
        def _():
            k_scr[...] = jnp.zeros_like(k_scr)
            v_scr[...] = jnp.zeros_like(v_scr)

        kst_ref[...] = k_scr[...]
        vst_ref[...] = v_scr[...]
        toks, params = _swa_load(q_ref, k_ref, v_ref, cos_ref, sin_ref, bq_ref, bk_ref, bv_ref, sk_ref)
        outs, (kn, vn) = swa_chunk((_heads(k_scr, 4, SWA_HD), _heads(v_scr, 4, SWA_HD)), toks, params, first)
        _put_heads(o_ref, outs, SWA_HD)
        _put_heads(k_scr, kn, SWA_HD)
        _put_heads(v_scr, vn, SWA_HD)

    return pl.pallas_call(
        body, name="swa_fwd", grid=(nb,), in_specs=toks_s + params_s,
        out_specs=(_tok_spec(c, 1024, 0), state_spec, state_spec),
        out_shape=(jax.ShapeDtypeStruct((t, 1024), F32), jax.ShapeDtypeStruct((nb * WINDOW, 256), F32),
                   jax.ShapeDtypeStruct((nb * WINDOW, 256), F32)),
        scratch_shapes=[pltpu.VMEM((WINDOW, 256), F32), pltpu.VMEM((WINDOW, 256), F32)],
        compiler_params=_cparams(("arbitrary",)))(proj1, proj1, proj1, cos, sin, bq, bk, bv, sinks)


def _swa_bwd(proj1, cos, sin, bq, bk, bv, sinks, kst, vst, do):
    t = proj1.shape[0]
    c = SWA_STEP
    nb = t // c
    toks_s, params_s = _swa_specs(c, nb)
    state_spec = pl.BlockSpec((WINDOW, 256), lambda i: (nb - 1 - i, 0))

    def body(q_ref, k_ref, v_ref, cos_ref, sin_ref, bq_ref, bk_ref, bv_ref, sk_ref, kst_ref, vst_ref, do_ref,
             dq_ref, dk_ref, dv_ref, dbq_ref, dbk_ref, dbv_ref, dsk_ref, dk_scr, dv_scr):
        i = pl.program_id(0)

        @pl.when(i == 0)
        def _():
            dk_scr[...] = jnp.zeros_like(dk_scr)
            dv_scr[...] = jnp.zeros_like(dv_scr)
            for ref in (dbq_ref, dbk_ref, dbv_ref, dsk_ref):
                ref[...] = jnp.zeros_like(ref)

        first = i == nb - 1
        toks, params = _swa_load(q_ref, k_ref, v_ref, cos_ref, sin_ref, bq_ref, bk_ref, bv_ref, sk_ref)
        f = functools.partial(swa_chunk, first=first)
        _, vjp = jax.vjp(f, (_heads(kst_ref, 4, SWA_HD), _heads(vst_ref, 4, SWA_HD)), toks, params)
        dstate_in = (_heads(dk_scr, 4, SWA_HD), _heads(dv_scr, 4, SWA_HD))
        (dkp, dvp), (dq, dk, dv, _, _), (dbq, dbk, dbv, dsk) = vjp((_heads(do_ref, 16, SWA_HD), dstate_in))
        _put_heads(dq_ref, dq, SWA_HD)
        _put_heads(dk_ref, dk, SWA_HD)
        _put_heads(dv_ref, dv, SWA_HD)
        _put_heads(dbq_ref, dbq, SWA_HD, add=True)
        _put_heads(dbk_ref, dbk, SWA_HD, add=True)
        _put_heads(dbv_ref, dbv, SWA_HD, add=True)
        _put_heads(dsk_ref, dsk, 1, add=True)
        _put_heads(dk_scr, dkp, SWA_HD)
        _put_heads(dv_scr, dvp, SWA_HD)

    rev = lambda w: pl.BlockSpec((c, w), lambda i: (nb - 1 - i, 0))
    return pl.pallas_call(
        body, name="swa_bwd", grid=(nb,), in_specs=toks_s + params_s + [state_spec, state_spec, rev(1024)],
        out_specs=(rev(1024), rev(256), rev(256), _full_spec((1, 1024)), _full_spec((1, 256)), _full_spec((1, 256)), _full_spec((1, 16))),
        out_shape=(jax.ShapeDtypeStruct((t, 1024), F32), jax.ShapeDtypeStruct((t, 256), F32), jax.ShapeDtypeStruct((t, 256), F32),
                   jax.ShapeDtypeStruct((1, 1024), F32), jax.ShapeDtypeStruct((1, 256), F32), jax.ShapeDtypeStruct((1, 256), F32),
                   jax.ShapeDtypeStruct((1, 16), F32)),
        scratch_shapes=[pltpu.VMEM((WINDOW, 256), F32), pltpu.VMEM((WINDOW, 256), F32)],
        compiler_params=_cparams(("arbitrary",)))(proj1, proj1, proj1, cos, sin, bq, bk, bv, sinks, kst, vst, do)


MESH = pl.DeviceIdType.MESH
ANY = pl.BlockSpec(memory_space=pl.ANY)


def _my_place():
    return lax.axis_index("x"), lax.axis_index("y"), lax.axis_index("c")


def _all_gather(shards):
    n = len(shards)

    def body(*refs):
        in_refs, out_refs = refs[:n], refs[n:2 * n]
        send_sems, recv_sems, local_sems = refs[2 * n:]
        x, y, c = _my_place()
        me, sibling = (x, y, c), (x, y, 1 - c)
        chips = [(1 - x, y), (x, 1 - y), (1 - x, 1 - y)]

        def slot(out_ref, place):
            px, py, pc = place
            return out_ref.at[4 * px + 2 * py + pc]

        def copy(a, k, block, to, src=None):
            return pltpu.make_async_remote_copy(
                src_ref=slot(out_refs[a], block) if src is None else src, dst_ref=slot(out_refs[a], block),
                send_sem=send_sems.at[a, k], recv_sem=recv_sems.at[a, k], device_id=to, device_id_type=MESH)

        mine = [pltpu.make_async_copy(in_refs[a], slot(out_refs[a], me), local_sems.at[a]) for a in range(n)]
        for cp in mine:
            cp.start()
        first = []
        for a in range(n):
            first.append(copy(a, 0, me, sibling, src=in_refs[a]))
            first += [copy(a, 1 + j, me, (*chip, c), src=in_refs[a]) for j, chip in enumerate(chips)]
        for cp in first:
            cp.start()
        passed = []
        for j, chip in enumerate(chips):
            for a in range(n):
                copy(a, 1 + j, (*chip, c), me).wait_recv()
                fwd = copy(a, 4 + j, (*chip, c), sibling)
                fwd.start()
                passed.append(fwd)
        for a in range(n):
            copy(a, 0, sibling, me).wait_recv()
            for j, chip in enumerate(chips):
                copy(a, 4 + j, (*chip, 1 - c), me).wait_recv()
        for cp in first + passed:
            cp.wait_send()
        for cp in mine:
            cp.wait()

    return pl.pallas_call(
        body, name="all_gather_weights", in_specs=[ANY] * n, out_specs=[ANY] * n,
        out_shape=[jax.ShapeDtypeStruct((N_DEV,) + s.shape, s.dtype) for s in shards],
        scratch_shapes=_comm_scratch(n))(*shards)


def _comm_copies(in_refs, out_refs, kinds, send_sems, recv_sems, local_sems):
    x, y, c = _my_place()
    my_idx = 4 * x + 2 * y + c
    src = lambda a, idx: in_refs[a] if kinds[a] == "gather" else in_refs[a].at[idx]
    local = [pltpu.make_async_copy(src(a, my_idx), out_refs[a].at[my_idx], local_sems.at[a]) for a in range(len(kinds))]
    remote = []
    for rel in range(1, N_DEV):
        px, py, pc = x ^ ((rel >> 2) & 1), y ^ ((rel >> 1) & 1), c ^ (rel & 1)
        for a in range(len(kinds)):
            remote.append(pltpu.make_async_remote_copy(
                src_ref=src(a, 4 * px + 2 * py + pc), dst_ref=out_refs[a].at[my_idx], send_sem=send_sems.at[a, rel - 1],
                recv_sem=recv_sems.at[a, rel - 1], device_id=(px, py, pc), device_id_type=MESH))
    return local, remote


def _comm_start(local, remote):
    for cp in local + remote:
        cp.start()


def _comm_wait(local, remote):
    for cp in remote:
        cp.wait_recv()
    for cp in remote:
        cp.wait_send()
    for cp in local:
        cp.wait()


def _comm_out_shapes(arrays, kinds):
    return [jax.ShapeDtypeStruct(((N_DEV,) + a.shape) if k == "gather" else a.shape, a.dtype) for a, k in zip(arrays, kinds)]


def _comm_scratch(n):
    return [pltpu.SemaphoreType.DMA((n, N_DEV - 1)), pltpu.SemaphoreType.DMA((n, N_DEV - 1)), pltpu.SemaphoreType.DMA((n,))]


def _exchange(arrays, kinds):
    n = len(arrays)

    def body(*refs):
        copies = _comm_copies(refs[:n], refs[n:2 * n], kinds, *refs[2 * n:])
        _comm_start(*copies)
        _comm_wait(*copies)

    return pl.pallas_call(body, name="exchange_grads", in_specs=[ANY] * n, out_specs=[ANY] * n,
                          out_shape=_comm_out_shapes(arrays, kinds), scratch_shapes=_comm_scratch(n))(*arrays)


def _adam_math(w, g, m, v):
    m = ADAM_B1 * m + (1.0 - ADAM_B1) * g
    v = ADAM_B2 * v + (1.0 - ADAM_B2) * (g * g)
    m_hat = m / (1.0 - ADAM_B1 ** ADAM_STEP)
    v_hat = v / (1.0 - ADAM_B2 ** ADAM_STEP)
    delta = -ADAM_LR * (m_hat / (jnp.sqrt(v_hat) + ADAM_EPS) + ADAM_WD * w)
    return delta, m, v


def _adamw(name, w, gslots, m, v, tc):
    r, cc = w.shape
    assert cc % tc == 0
    tile = pl.BlockSpec((r, tc), lambda i: (0, i))

    def body(w_ref, g_ref, m_ref, v_ref, go_ref, d_ref, mo_ref, vo_ref):
        g = g_ref[0].astype(F32)
        for s in range(1, N_DEV):
            g = g + g_ref[s].astype(F32)
        d, mn, vn = _adam_math(w_ref[...], g, m_ref[...], v_ref[...])
        go_ref[...] = g
        d_ref[...] = d
        mo_ref[...] = mn
        vo_ref[...] = vn

    shp = jax.ShapeDtypeStruct((r, cc), F32)
    return pl.pallas_call(body, name=name, grid=(cc // tc,),
                          in_specs=[tile, pl.BlockSpec((N_DEV, r, tc), lambda i: (0, 0, i)), tile, tile],
                          out_specs=(tile,) * 4, out_shape=(shp,) * 4, compiler_params=_cparams(("arbitrary",)))(w, gslots, m, v)


PACK_TILE = 8 * LANES


def _packed_rows(shape, mode):
    r, w = shape
    return -(-r // 8) * 8 if mode == "rows" else -(-(r * w) // PACK_TILE) * 8


def _pack_small(arrays, modes, lead=False):
    out = []
    for a, mode in zip(arrays, modes):
        a = a.astype(F32) if lead else a.astype(F32)[None]
        if mode == "rows":
            out.append(jnp.pad(a, ((0, 0), (0, (-a.shape[1]) % 8), (0, LANES - a.shape[2]))))
        else:
            flat = a.reshape(a.shape[0], -1)
            out.append(jnp.pad(flat, ((0, 0), (0, (-flat.shape[1]) % PACK_TILE))).reshape(a.shape[0], -1, LANES))
    out = jnp.concatenate(out, axis=1)
    return out if lead else out[0]


def _take_small(packed, row0, shape, mode):
    r, w = shape
    lead = packed.ndim == 3
    if mode == "rows":
        return packed[:, row0:row0 + r, :w] if lead else packed[row0:row0 + r, :w]
    per_row = -(-w // LANES)
    if lead:
        return packed[:, row0:row0 + r * per_row].reshape(packed.shape[0], r, per_row * LANES)[:, :, :w]
    rows = []
    for i in range(r):
        pieces = [packed[row0 + i * per_row + j:row0 + i * per_row + j + 1, :] for j in range(per_row)]
        rows.append((pieces[0] if per_row == 1 else jnp.concatenate(pieces, axis=1))[:, :w])
    return rows[0] if r == 1 else jnp.concatenate(rows, axis=0)


def _adamw_small(slots, specs, ws, ms, vs, loss_row):
    n = len(specs)

    def body(*refs):
        slots_ref, w_refs, m_refs, v_refs = refs[0], refs[1:1 + n], refs[1 + n:1 + 2 * n], refs[1 + 2 * n:1 + 3 * n]
        out_refs, loss_ref = refs[1 + 3 * n:1 + 7 * n], refs[1 + 7 * n]
        gp = slots_ref[0]
        for s in range(1, N_DEV):
            gp = gp + slots_ref[s]
        read = lambda ref: ref[0] if len(ref.shape) == 3 else ref[...]
        for k, (shape, mode, row0) in enumerate(specs):
            g = _take_small(gp, row0, shape, mode)
            d, mn, vn = _adam_math(read(w_refs[k]), g, read(m_refs[k]), read(v_refs[k]))
            for ref, val in zip(out_refs[4 * k:4 * k + 4], (g, d, mn, vn)):
                if len(ref.shape) == 3:
                    ref[0] = val
                else:
                    ref[...] = val
        loss_ref[...] = gp[loss_row:loss_row + 1, :]

    vmem = pl.BlockSpec(memory_space=pltpu.VMEM)
    out_shape = [jax.ShapeDtypeStruct(w.shape, F32) for w in ws for _ in range(4)] + [jax.ShapeDtypeStruct((1, LANES), F32)]
    outs = pl.pallas_call(body, name="adamw_small", in_specs=[vmem] * (1 + 3 * n), out_specs=[vmem] * (4 * n + 1),
                          out_shape=out_shape)(slots, *ws, *ms, *vs)
    return [outs[4 * k:4 * k + 4] for k in range(n)], outs[4 * n]


def _rope_tables(t):
    half = 8
    inv_freq = ROPE_THETA ** (-jnp.arange(half, dtype=F32) / half)
    ang = jnp.arange(t, dtype=F32)[:, None] * inv_freq
    cos = jnp.concatenate([jnp.cos(ang), jnp.cos(ang), jnp.ones((t, SWA_HD - 16), F32)], axis=1)
    sin = jnp.concatenate([jnp.sin(ang), jnp.sin(ang), jnp.zeros((t, SWA_HD - 16), F32)], axis=1)
    return cos, sin


def _pad_to(a, rows=None, cols=None):
    r = 0 if rows is None else rows - a.shape[0]
    c = 0 if cols is None else cols - a.shape[1]
    return jnp.pad(a, ((0, r), (0, c)))


ORIG0 = dict(gq=(0, 256), gk=(256, 256), gv=(512, 512), glow=(1024, 16), r=(1040, 512), k=(1552, 512), v=(2064, 512),
             xw=(2576, 64), xa=(2640, 64), gate=(2704, 1024))
ORIG0_ORDER = ["gq", "gk", "gv", "glow", "r", "k", "v", "xw", "xa", "gate"]


def _w0t_to_padded(wt):
    rows, at = [], 0
    for name, (off, width) in sorted(C0.items(), key=lambda kv: kv[1][0]):
        assert off == at
        src, src_w = ORIG0[name]
        rows.append(_pad_to(wt[src:src + src_w], rows=width))
        at += width
    rows.append(jnp.zeros((N0P - at, wt.shape[1]), wt.dtype))
    return jnp.concatenate(rows, axis=0)


def _w0t_from_padded(wpt):
    return jnp.concatenate([wpt[C0[n][0]:C0[n][0] + ORIG0[n][1]] for n in ORIG0_ORDER], axis=0)


def _w1t_to_mine(wt):
    return jnp.concatenate([wt[1536:2560], wt[:1536]], axis=0)


def _w1t_from_mine(wt):
    return jnp.concatenate([wt[1024:2560], wt[:1024]], axis=0)


def kernel(x, norm_w, w_in0, gla_gk_up, gla_gk_bias, gla_norm_w, rwkv_mu, rwkv_w0, rwkv_w_up, rwkv_a0, rwkv_a_up, rwkv_k_k, rwkv_k_a, rwkv_r_k, rwkv_ln_w, rwkv_ln_b, w_out0, w_in1, b_in1, attn_sinks, w_out1, b_out1, final_norm_w, loss_target, m_norm_w, m_w_in0, m_gla_gk_up, m_gla_gk_bias, m_gla_norm_w, m_rwkv_mu, m_rwkv_w0, m_rwkv_w_up, m_rwkv_a0, m_rwkv_a_up, m_rwkv_k_k, m_rwkv_k_a, m_rwkv_r_k, m_rwkv_ln_w, m_rwkv_ln_b, m_w_out0, m_w_in1, m_b_in1, m_attn_sinks, m_w_out1, m_b_out1, m_final_norm_w, v_norm_w, v_w_in0, v_gla_gk_up, v_gla_gk_bias, v_gla_norm_w, v_rwkv_mu, v_rwkv_w0, v_rwkv_w_up, v_rwkv_a0, v_rwkv_a_up, v_rwkv_k_k, v_rwkv_k_a, v_rwkv_r_k, v_rwkv_ln_w, v_rwkv_ln_b, v_w_out0, v_w_in1, v_b_in1, v_attn_sinks, v_w_out1, v_b_out1, v_final_norm_w):
    weights = dict(norm_w=norm_w, w_in0=w_in0, gla_gk_up=gla_gk_up, gla_gk_bias=gla_gk_bias, gla_norm_w=gla_norm_w, rwkv_mu=rwkv_mu,
                   rwkv_w0=rwkv_w0, rwkv_w_up=rwkv_w_up, rwkv_a0=rwkv_a0, rwkv_a_up=rwkv_a_up, rwkv_k_k=rwkv_k_k, rwkv_k_a=rwkv_k_a,
                   rwkv_r_k=rwkv_r_k, rwkv_ln_w=rwkv_ln_w, rwkv_ln_b=rwkv_ln_b, w_out0=w_out0, w_in1=w_in1, b_in1=b_in1,
                   attn_sinks=attn_sinks, w_out1=w_out1, b_out1=b_out1, final_norm_w=final_norm_w)
    moms = dict(norm_w=m_norm_w, w_in0=m_w_in0, gla_gk_up=m_gla_gk_up, gla_gk_bias=m_gla_gk_bias, gla_norm_w=m_gla_norm_w,
                rwkv_mu=m_rwkv_mu, rwkv_w0=m_rwkv_w0, rwkv_w_up=m_rwkv_w_up, rwkv_a0=m_rwkv_a0, rwkv_a_up=m_rwkv_a_up,
                rwkv_k_k=m_rwkv_k_k, rwkv_k_a=m_rwkv_k_a, rwkv_r_k=m_rwkv_r_k, rwkv_ln_w=m_rwkv_ln_w, rwkv_ln_b=m_rwkv_ln_b,
                w_out0=m_w_out0, w_in1=m_w_in1, b_in1=m_b_in1, attn_sinks=m_attn_sinks, w_out1=m_w_out1, b_out1=m_b_out1,
                final_norm_w=m_final_norm_w)
    vars_ = dict(norm_w=v_norm_w, w_in0=v_w_in0, gla_gk_up=v_gla_gk_up, gla_gk_bias=v_gla_gk_bias, gla_norm_w=v_gla_norm_w,
                 rwkv_mu=v_rwkv_mu, rwkv_w0=v_rwkv_w0, rwkv_w_up=v_rwkv_w_up, rwkv_a0=v_rwkv_a0, rwkv_a_up=v_rwkv_a_up,
                 rwkv_k_k=v_rwkv_k_k, rwkv_k_a=v_rwkv_k_a, rwkv_r_k=v_rwkv_r_k, rwkv_ln_w=v_rwkv_ln_w, rwkv_ln_b=v_rwkv_ln_b,
                 w_out0=v_w_out0, w_in1=v_w_in1, b_in1=v_b_in1, attn_sinks=v_attn_sinks, w_out1=v_w_out1, b_out1=v_b_out1,
                 final_norm_w=v_final_norm_w)
    names = list(weights)
    big = ["w_in0", "w_out0", "w_in1", "w_out1"]
    small_sharded = ["gla_gk_up", "rwkv_w_up", "rwkv_a_up", "b_in1", "b_out1"]
    replicated = [n for n in names if n not in big and n not in small_sharded]

    xs = x[0]
    tgt = loss_target[0]
    t = xs.shape[0]

    def view(w):
        shape = tuple(w.shape[-2:]) if w.ndim >= 2 else (1, w.shape[0])
        return shape, ("rows" if shape[0] > 1 and shape[1] <= LANES else "flat")

    def layout(ns, row0=0):
        specs = []
        for n in ns:
            shape, mode = view(weights[n])
            specs.append((shape, mode, row0))
            row0 += _packed_rows(shape, mode)
        return specs, row0

    sh_specs, n_shard_rows = layout(small_sharded)
    rep_specs, loss_row = layout(replicated, n_shard_rows)
    sh_modes, rep_modes = [s[1] for s in sh_specs], [s[1] for s in rep_specs]

    small_shard_pack = _pack_small([weights[n].reshape(view(weights[n])[0]) for n in small_sharded], sh_modes)
    g_in0, g_small = _all_gather([w_in0[0].T.astype(BF16), small_shard_pack])
    w0t = _w0t_to_padded(g_in0.reshape(-1, D_MODEL))
    later_shards = [w_out0[0].astype(BF16), w_in1[0].T.astype(BF16), w_out1[0].astype(BF16)]
    gs = [_take_small(g_small, row0, shape, mode) for shape, mode, row0 in sh_specs]
    join_cols = lambda a: jnp.transpose(a, (1, 0, 2)).reshape(a.shape[1], -1)
    gk_up, w_up, a_up = join_cols(gs[0]), join_cols(gs[1]), join_cols(gs[2])
    b_in, b_out = gs[3].reshape(1, -1), gs[4].reshape(1, -1)

    gk_up_p = _pad_to(gk_up, rows=128)
    mu = rwkv_mu
    rwkv_params = [mu[:, 0:512], mu[:, 512:1024], mu[:, 1024:1536], _pad_to(mu[:, 1536:1600], cols=128), _pad_to(mu[:, 1600:1664], cols=128),
                   rwkv_w0, _pad_to(w_up, rows=128), rwkv_a0, _pad_to(a_up, rows=128), rwkv_k_k, rwkv_k_a, rwkv_r_k.reshape(1, 512),
                   rwkv_ln_w, rwkv_ln_b]
    bq, bk, bv = b_in[:, :1024], b_in[:, 1024:1280], b_in[:, 1280:1536]
    cos, sin = _rope_tables(t)
    nw0, nw1, fw = norm_w[0:1], norm_w[1:2], final_norm_w.reshape(1, D_MODEL)

    d = D_MODEL
    wide = lambda arr: (arr, d, 0)
    hn0 = _norm_fwd("norm0_fwd", xs, nw0)
    proj0 = _matmul("proj0", hn0, w0t, "nt", 1024, 1024)
    o_a, gla_states = _gla_fwd(proj0, gk_up_p, gla_gk_bias, gla_norm_w)
    o_b, rwkv_states, rwkv_prevs, (g_out0, g_in1, g_out1) = _rwkv_fwd(proj0, rwkv_params, later_shards, ["gather"] * 3)
    wo0 = g_out0.reshape(1024, D_MODEL)
    w1t = _w1t_to_mine(g_in1.reshape(-1, D_MODEL))
    wo1 = g_out1.reshape(1024, D_MODEL)
    og0 = _gate_fwd("gate0_fwd", [o_a, o_b], proj0)
    h1, hn1 = _matmul_fused("out0_norm1", og0, wo0, "nn", [wide(xs)], [nw1], [(d, F32), (d, BF16)], [], _resid_norm)
    proj1 = _matmul("proj1", hn1, w1t, "nt", 1024, 1280)
    o_c, kst, vst = _swa_fwd(proj1, cos, sin, bq, bk, bv, attn_sinks)
    og1 = _gate_fwd("gate1_fwd", [o_c], proj1)
    dh2, loss_part, d_b_out, d_fw = _matmul_fused("out1_loss", og1, wo1, "nn", [wide(h1), wide(tgt)], [b_out, fw],
                                                  [(d, F32)], [LANES, d, d], _loss_head)

    d_oc, d_gate1 = _matmul_fused("out1_dx_gate1", dh2, wo1, "nt", [wide(o_c), wide(proj1)], [], [(d, F32), (d, F32)], [], _gate_back)
    d_wo1 = _matmul("out1_dw", og1, dh2, "tn", 512, 512)
    dq, dk, dv, d_bq, d_bk, d_bv, d_sinks = _swa_bwd(proj1, cos, sin, bq, bk, bv, attn_sinks, kst, vst, d_oc)
    dproj1 = jnp.concatenate([d_gate1, dq, dk, dv], axis=1).astype(BF16)
    dh1, d_nw1 = _matmul_fused("proj1_dx_norm1", dproj1, w1t, "nn", [wide(h1), wide(dh2)], [nw1], [(d, F32)], [d], _norm_back)
    d_w1t = _matmul("proj1_dw", dproj1, hn1, "tn", 512, 1024)
    d_oa, d_ob, d_gate0 = _matmul_fused("out0_dx_gate0", dh1, wo0, "nt", [(o_a, 512, 0), (o_b, 512, 0), wide(proj0)], [],
                                        [(512, F32), (512, F32), (d, F32)], [], _gate_back)
    d_wo0 = _matmul("out0_dw", og0, dh1, "tn", 512, 512)
    dgq, dgk, dgv, dglow, d_gk_up, d_gk_bias, d_gla_nw = _gla_bwd(proj0, gk_up_p, gla_gk_bias, gla_norm_w, gla_states, d_oa)
    row_blocks = lambda a: a.astype(BF16).reshape(N_DEV, -1, D_MODEL)
    early = [row_blocks(_w1t_from_mine(d_w1t)), row_blocks(d_wo1), row_blocks(d_wo0)]
    (dr, dkk, dvv, dxw, dxa), d_rp, (r_in1, r_out1, r_out0) = _rwkv_bwd(
        proj0, rwkv_params, rwkv_states, rwkv_prevs, d_ob, early, ["scatter"] * 3)
    dproj0 = jnp.concatenate([d_gate0, dgv, dr, dkk, dvv, dgq, dgk, dglow, dxw, dxa, jnp.zeros((t, 128), F32)], axis=1).astype(BF16)
    d_w0t = _matmul("proj0_dw", dproj0, hn0, "tn", 512, 1024)
    grad_x, d_nw0 = _matmul_fused("proj0_dx_norm0", dproj0, w0t, "nn", [wide(xs), wide(dh1)], [nw0], [(d, F32)], [d], _norm_back)

    contrib = dict(
        norm_w=jnp.concatenate([d_nw0, d_nw1], axis=0), gla_gk_bias=d_gk_bias, gla_norm_w=d_gla_nw,
        rwkv_mu=jnp.concatenate([d_rp[0], d_rp[1], d_rp[2], d_rp[3][:, :64], d_rp[4][:, :64]], axis=1),
        rwkv_w0=d_rp[5], rwkv_a0=d_rp[7], rwkv_k_k=d_rp[9], rwkv_k_a=d_rp[10], rwkv_r_k=d_rp[11].reshape(RWKV_HEADS, RWKV_N),
        rwkv_ln_w=d_rp[12], rwkv_ln_b=d_rp[13], attn_sinks=d_sinks, final_norm_w=d_fw)
    rep_pack = _pack_small([contrib[n] for n in replicated] + [loss_part[:, :1]], rep_modes + ["flat"])

    d_w0 = _w0t_from_padded(d_w0t)
    d_b_in = jnp.concatenate([d_bq, d_bk, d_bv], axis=1)
    full_small = [d_gk_up[:16], d_rp[6][:64], d_rp[8][:64], d_b_in, d_b_out]
    split_cols = lambda a: jnp.transpose(a.reshape(a.shape[0], N_DEV, -1), (1, 0, 2))
    small_parts = [split_cols(a) for a in full_small]
    small_pack = _pack_small(small_parts, sh_modes, lead=True)
    r_in0, r_small, r_rep = _exchange([row_blocks(d_w0), small_pack, rep_pack], ["scatter", "scatter", "gather"])

    res = {}
    res["w_in0"] = tuple(a.T[None] for a in _adamw("adamw_w_in0", w_in0[0].T, r_in0, m_w_in0[0].T, v_w_in0[0].T, 256))
    res["w_out0"] = tuple(a[None] for a in _adamw("adamw_w_out0", w_out0[0], r_out0, m_w_out0[0], v_w_out0[0], 256))
    res["w_in1"] = tuple(a.T[None] for a in _adamw("adamw_w_in1", w_in1[0].T, r_in1, m_w_in1[0].T, v_w_in1[0].T, 256))
    res["w_out1"] = tuple(a[None] for a in _adamw("adamw_w_out1", w_out1[0], r_out1, m_w_out1[0], v_w_out1[0], 256))
    small_names = small_sharded + replicated
    slots = jnp.concatenate([r_small, r_rep], axis=1)
    as_2d = lambda a: a.reshape(1, -1) if a.ndim == 1 else a
    small_res, loss_row_out = _adamw_small(slots, sh_specs + rep_specs, [as_2d(weights[n]) for n in small_names],
                                           [as_2d(moms[n]) for n in small_names], [as_2d(vars_[n]) for n in small_names], loss_row)
    for n, vals in zip(small_names, small_res):
        res[n] = tuple(val.reshape(weights[n].shape) for val in vals)
    loss = loss_row_out[0, 0]
    return (loss, grad_x[None], *[res[n][0] for n in names], *[res[n][1] for n in names],
            *[res[n][2] for n in names], *[res[n][3] for n in names])
```

```python
import functools

import jax
import jax.numpy as jnp
from jax import lax
from jax.experimental import pallas as pl
from jax.experimental.pallas import tpu as pltpu

F32 = jnp.float32
BF16 = jnp.bfloat16
HI = lax.Precision.HIGHEST

D_MODEL = 1024
NORM_EPS = 1e-5
GLA_HEADS, GLA_DK, GLA_DV = 4, 64, 128
GLA_NORMALIZER = 16.0
GLA_CHUNK = 64
GLA_STEP = 256
RWKV_HEADS, RWKV_N = 8, 64
RWKV_LN_EPS = 64e-5
RWKV_CHUNK = 128
SWA_Q_HEADS, SWA_KV_HEADS, SWA_GROUP, SWA_HD = 16, 4, 4, 64
WINDOW = 128
SWA_STEP = 256
ROPE_THETA = 500000.0
NEG = -1e30
N_DEV = 8
LANES = 128

ADAM_LR, ADAM_B1, ADAM_B2, ADAM_EPS, ADAM_WD, ADAM_STEP = 0.001, 0.9, 0.999, 1e-08, 0.01, 10

N0P = 4096
C0 = dict(gate=(0, 1024), gv=(1024, 512), r=(1536, 512), k=(2048, 512), v=(2560, 512), gq=(3072, 256), gk=(3328, 256),
          glow=(3584, 128), xw=(3712, 128), xa=(3840, 128))
N1P = 2560
C1 = dict(gate=(0, 1024), q=(1024, 1024), k=(2048, 256), v=(2304, 256))

VMEM_LIMIT = 56 * 1024 * 1024

P_LORA = 1
P_GLA = 1
P_RWKV_G = 2
P_RWKV = 1
P_SWA = 1
P_ROPE = 3


def _cparams(sem=None):
    return pltpu.CompilerParams(dimension_semantics=sem, vmem_limit_bytes=VMEM_LIMIT)


DIMS = dict(nn=(((1,), (0,)), ((), ())), nt=(((1,), (1,)), ((), ())), tn=(((0,), (0,)), ((), ())))


def _split_bf16(a):
    hi = a.astype(BF16)
    return hi, (a - hi.astype(F32)).astype(BF16)


def _dot(a, b, mode, passes):
    dg = lambda p, q: lax.dot_general(p, q, DIMS[mode], preferred_element_type=F32)
    if passes == 1:
        return dg(a.astype(BF16), b.astype(BF16))
    if passes == 2:
        ah, (bh, bl) = a.astype(BF16), _split_bf16(b)
        return dg(ah, bh) + dg(ah, bl)
    if passes == 3:
        (ah, al), (bh, bl) = _split_bf16(a), _split_bf16(b)
        return dg(ah, bh) + dg(al, bh) + dg(ah, bl)
    return lax.dot_general(a, b, DIMS[mode], precision=HI, preferred_element_type=F32)


@functools.partial(jax.custom_vjp, nondiff_argnums=(2, 3))
def mmx(a, b, mode, passes):
    return _dot(a, b, mode, passes)


def _mmx_fwd(a, b, mode, passes):
    return _dot(a, b, mode, passes), (a, b)


def _mmx_bwd(mode, passes, res, g):
    a, b = res
    if mode == "nn":
        return _dot(g, b, "nt", passes), _dot(a, g, "tn", passes)
    if mode == "nt":
        return _dot(g, b, "nn", passes), _dot(g, a, "tn", passes)
    return _dot(b, g, "nt", passes), _dot(a, g, "nn", passes)


mmx.defvjp(_mmx_fwd, _mmx_bwd)


def _tri_dot(tri, x):
    t = tri.astype(BF16)
    x1 = x.astype(BF16)
    r1 = x - x1.astype(F32)
    x2 = r1.astype(BF16)
    x3 = (r1 - x2.astype(F32)).astype(BF16)
    dg = lambda q: jnp.dot(t, q, preferred_element_type=F32)
    return dg(x1) + dg(x2) + dg(x3)


@jax.custom_vjp
def cumsum_rows(x):
    return _tri_dot(tril_ones(x.shape[0]), x)


def _cumsum_fwd(x):
    return cumsum_rows(x), None


def _cumsum_bwd(_, g):
    i, j = _iota2(g.shape[0], g.shape[0])
    return (_tri_dot(jnp.where(i <= j, 1.0, 0.0).astype(F32), g),)


cumsum_rows.defvjp(_cumsum_fwd, _cumsum_bwd)


def _head_dot(x):
    i, j = _iota2(LANES, LANES)
    shift = RWKV_N.bit_length() - 1
    same = jnp.where(jnp.right_shift(i, shift) == jnp.right_shift(j, shift), 1.0, 0.0).astype(F32)
    return jnp.concatenate([_ones_right(x[:, g * LANES:(g + 1) * LANES], same) for g in range(x.shape[1] // LANES)], axis=1)


def _ones_right(x, ones):
    t = ones.astype(BF16)
    x1 = x.astype(BF16)
    r1 = x - x1.astype(F32)
    x2 = r1.astype(BF16)
    x3 = (r1 - x2.astype(F32)).astype(BF16)
    dg = lambda q: jnp.dot(q, t, preferred_element_type=F32)
    return dg(x1) + dg(x2) + dg(x3)


@jax.custom_vjp
def head_sum(x):
    return _head_dot(x)


def _head_sum_fwd(x):
    return head_sum(x), None


def _head_sum_bwd(_, g):
    return (_head_dot(g),)


head_sum.defvjp(_head_sum_fwd, _head_sum_bwd)


def cat_rows(*xs):
    return jnp.concatenate(xs, axis=0)


def _iota2(n, m):
    return lax.broadcasted_iota(jnp.int32, (n, m), 0), lax.broadcasted_iota(jnp.int32, (n, m), 1)


def tril_ones(c, strict=False):
    i, j = _iota2(c, c)
    return jnp.where((i > j) if strict else (i >= j), 1.0, 0.0).astype(F32)


def row_of(x, r):
    i = lax.broadcasted_iota(jnp.int32, x.shape, 0)
    return jnp.sum(jnp.where(i == r, x, 0.0), axis=0, keepdims=True)


@jax.custom_vjp
def shift_rows(x, prev):
    r = lax.broadcasted_iota(jnp.int32, x.shape, 0)
    return jnp.where(r == 0, prev, pltpu.roll(x, 1, 0))


def _shift_fwd(x, prev):
    return shift_rows(x, prev), None


def _shift_bwd(_, g):
    c = g.shape[0]
    r = lax.broadcasted_iota(jnp.int32, g.shape, 0)
    return jnp.where(r == c - 1, 0.0, pltpu.roll(g, c - 1, 0)), row_of(g, 0)


shift_rows.defvjp(_shift_fwd, _shift_bwd)


def log_sigmoid(x):
    return jnp.minimum(x, 0.0) - jnp.log(1.0 + jnp.exp(-jnp.abs(x)))


def softplus(x):
    return jnp.maximum(x, 0.0) + jnp.log(1.0 + jnp.exp(-jnp.abs(x)))


def sigmoid(x):
    return 1.0 / (1.0 + jnp.exp(-x))


def rms(x, w, eps=NORM_EPS):
    return x * lax.rsqrt(jnp.mean(x * x, axis=-1, keepdims=True) + eps) * w


def gla_chunk(state, toks, params):
    q, k, v, glow = toks
    gk_up, bias, norm_w = params
    c = GLA_CHUNK
    subs, heads = range(glow.shape[0] // c), range(GLA_HEADS)
    rows = lambda x, j: x[j * c:(j + 1) * c]
    hk = lambda x, h: x[:, h * GLA_DK:(h + 1) * GLA_DK]
    hv = lambda x, h: x[:, h * GLA_DV:(h + 1) * GLA_DV]
    ltri = tril_ones(c)
    g = log_sigmoid(mmx(glow, gk_up, "nn", P_LORA) + bias) / GLA_NORMALIZER
    b = [cumsum_rows(rows(g, j)) for j in subs]
    ref = [lax.stop_gradient(row_of(b[j], c // 2)) for j in subs]
    last = [row_of(b[j], c - 1) for j in subs]
    ql = [rows(q, j) * (GLA_DK ** -0.5) * jnp.exp(b[j] - ref[j]) for j in subs]
    kr = [rows(k, j) * jnp.exp(ref[j] - b[j]) for j in subs]
    kl = [rows(k, j) * jnp.exp(last[j] - b[j]) for j in subs]
    vj = [rows(v, j) for j in subs]
    e_ref, e_last = [jnp.exp(x) for x in ref], [jnp.exp(x) for x in last]
    att = [[mmx(hk(ql[j], h), hk(kr[j], h), "nt", P_GLA) * ltri for h in heads] for j in subs]
    o_in = [[mmx(att[j][h], hv(vj[j], h), "nn", P_GLA) for h in heads] for j in subs]
    kv = [[mmx(hv(vj[j], h), hk(kl[j], h), "tn", P_GLA) for h in heads] for j in subs]
    o = []
    for j in subs:
        o.append([o_in[j][h] + mmx(hk(ql[j], h), state[h] * hk(e_ref[j], h), "nt", P_GLA) for h in heads])
        state = [state[h] * hk(e_last[j], h) + kv[j][h] for h in heads]
    o = [[x * lax.rsqrt(jnp.mean(x * x, axis=-1, keepdims=True) + NORM_EPS) * norm_w for x in oj] for oj in o]
    return cat_rows(*[jnp.concatenate(oj, axis=1) for oj in o]), state


SOLVE_BLOCK = 128


def solve_unit_lower(ps, ws):
    n = ps[0].shape[0]
    heads = range(len(ps))
    if n > SOLVE_BLOCK:
        half = n // 2
        top = solve_unit_lower([p[:half, :half] for p in ps], [w[:half] for w in ws])
        rest = [ws[h][half:] + mmx(ps[h][half:, :half], top[h], "nn", P_RWKV) for h in heads]
        bottom = solve_unit_lower([p[half:, half:] for p in ps], rest)
        return [cat_rows(top[h], bottom[h]) for h in heads]
    u, p = ws, ps
    levels = max(1, (n - 1).bit_length())
    for it in range(levels):
        if it + 1 < levels:
            y = [mmx(p[h], jnp.concatenate([p[h], u[h]], axis=1), "nn", P_RWKV) for h in heads]
            u = [u[h] + y[h][:, n:] for h in heads]
            p = [y[h][:, :n] for h in heads]
        else:
            u = [u[h] + mmx(p[h], u[h], "nn", P_RWKV) for h in heads]
    return u


def rwkv_chunk(state, toks, params):
    S, pr, pk, pv, pxw, pxa = state
    r_, k_, v_, xw_, xa_ = toks
    mu_r, mu_k, mu_v, mu_xw, mu_xa, w0, w_up, a0, a_up, k_k, k_a, r_k, ln_w, ln_b = params
    c, n = xw_.shape[0], RWKV_N
    heads = range(RWKV_HEADS)
    hs = lambda x, h: x[:, h * n:(h + 1) * n]
    ltri = tril_ones(c)
    stri = tril_ones(c, strict=True)

    def lerp(x, prev, mu):
        return x + (shift_rows(x, prev) - x) * mu

    xw = jnp.tanh(lerp(xw_, pxw, mu_xw))
    xa = lerp(xa_, pxa, mu_xa)
    r = lerp(r_, pr, mu_r)
    k = lerp(k_, pk, mu_k)
    v = lerp(v_, pv, mu_v)
    w = -softplus(-(w0 + mmx(xw, w_up, "nn", P_LORA))) - 0.5
    lw = -jnp.exp(w)
    asig = sigmoid(a0 + mmx(xa, a_up, "nn", P_LORA))
    kk = k * k_k
    kk = kk / jnp.maximum(jnp.sqrt(head_sum(kk * kk)), 1e-12)
    k2 = k * (1.0 + (asig - 1.0) * k_a)
    b = kk * asig
    cum = cumsum_rows(lw)
    ref = lax.stop_gradient(row_of(cum, c // 2))
    last = row_of(cum, c - 1)
    at = -kk * jnp.exp(cum - lw - ref)
    rt = r * jnp.exp(cum - ref)
    e_out = jnp.exp(ref - cum)
    bt, kt = b * e_out, k2 * e_out
    e_tail = jnp.exp(last - cum)
    bl, kl = b * e_tail, k2 * e_tail
    e_ref, e_last = jnp.exp(ref), jnp.exp(last)
    g = [mmx(cat_rows(hs(at, h), hs(rt, h)), cat_rows(hs(bt, h), hs(kt, h), S[h] * hs(e_ref, h)), "nt", P_RWKV_G) for h in heads]
    aab = [x[:c, :c] * stri for x in g]
    aak = [x[:c, c:2 * c] * stri for x in g]
    arb = [x[c:, :c] * ltri for x in g]
    ark = [x[c:, c:2 * c] * ltri for x in g]
    av = [mmx(cat_rows(aak[h], ark[h]), hs(v, h), "nn", P_RWKV) for h in heads]
    u = solve_unit_lower(aab, [g[h][:c, 2 * c:] + av[h][:c] for h in heads])
    o = [g[h][c:, 2 * c:] + av[h][c:] + mmx(arb[h], u[h], "nn", P_RWKV) for h in heads]
    s1 = [S[h] * hs(e_last, h) + mmx(cat_rows(u[h], hs(v, h)), cat_rows(hs(bl, h), hs(kl, h)), "tn", P_RWKV) for h in heads]
    o = jnp.concatenate(o, axis=1)
    d = o - head_sum(o) * (1.0 / n)
    var = head_sum(d * d) * (1.0 / n)
    o = d * lax.rsqrt(var + RWKV_LN_EPS) * ln_w + ln_b + head_sum(r * k2 * r_k) * v
    new_state = (s1, row_of(r_, c - 1), row_of(k_, c - 1), row_of(v_, c - 1), row_of(xw_, c - 1), row_of(xa_, c - 1))
    return o, new_state


def rope_mat():
    i, j = _iota2(SWA_HD, SWA_HD)
    plus = (j >= 8) & (j < 16) & (i == j - 8)
    minus = (j < 8) & (i == j + 8)
    return jnp.where(plus, 1.0, 0.0).astype(F32) - jnp.where(minus, 1.0, 0.0).astype(F32)


def swa_chunk(state, toks, params, first):
    kprev, vprev = state
    q_, k_, v_, cos, sin = toks
    bq, bk, bv, sinks = params
    c, ng = WINDOW, SWA_GROUP
    n_sub = cos.shape[0] // c
    units = [(j, g) for j in range(n_sub) for g in range(SWA_KV_HEADS)]
    rows = lambda x, j: x[j * c:(j + 1) * c]
    hs = lambda g: range(g * ng, (g + 1) * ng)
    rm = rope_mat()
    qi, kj = _iota2(ng * c, 2 * c)
    qpos = qi & (c - 1)
    cur_ok = (kj >= c) & (qpos >= kj - c)
    prev_ok = (kj < c) & (kj > qpos)
    ok = [cur_ok | (prev_ok & jnp.logical_not(first))] + [cur_ok | prev_ok] * (n_sub - 1)
    cs, sn = [rows(cos, j) for j in range(n_sub)], [rows(sin, j) for j in range(n_sub)]
    cs_g, sn_g = [cat_rows(*[x] * ng) for x in cs], [cat_rows(*[x] * ng) for x in sn]

    def rope(x, cos_, sin_):
        return x * cos_ + mmx(x, rm, "nn", P_ROPE) * sin_

    k = {(j, g): rope(rows(k_[g], j) + bk[g], cs[j], sn[j]) for j, g in units}
    v = {(j, g): rows(v_[g], j) + bv[g] for j, g in units}
    q = {(j, g): rope(cat_rows(*[rows(q_[h], j) + bq[h] for h in hs(g)]), cs_g[j], sn_g[j]) * (SWA_HD ** -0.5) for j, g in units}
    kp = lambda j, g: kprev[g] if j == 0 else k[(j - 1, g)]
    vp = lambda j, g: vprev[g] if j == 0 else v[(j - 1, g)]
    s = {(j, g): jnp.where(ok[j], mmx(q[(j, g)], cat_rows(kp(j, g), k[(j, g)]), "nt", P_SWA), NEG) for j, g in units}
    sink = [cat_rows(*[jnp.broadcast_to(sinks[h], (c, 1)) for h in hs(g)]) for g in range(SWA_KV_HEADS)]
    m = {(j, g): lax.stop_gradient(jnp.maximum(jnp.max(s[(j, g)], axis=-1, keepdims=True), sink[g])) for j, g in units}
    p = {u: jnp.exp(s[u] - m[u]) for u in units}
    ones = jnp.ones((2 * c, SWA_HD), F32)
    pv = {(j, g): mmx(p[(j, g)], jnp.concatenate([cat_rows(vp(j, g), v[(j, g)]), ones], axis=1), "nn", P_SWA) for j, g in units}
    o = {(j, g): pv[(j, g)][:, :SWA_HD] / (pv[(j, g)][:, SWA_HD:] + jnp.exp(sink[g] - m[(j, g)])) for j, g in units}
    outs = [cat_rows(*[o[(j, g)][i * c:(i + 1) * c] for j in range(n_sub)]) for g in range(SWA_KV_HEADS) for i in range(ng)]
    last = n_sub - 1
    return outs, ([k[(last, g)] for g in range(SWA_KV_HEADS)], [v[(last, g)] for g in range(SWA_KV_HEADS)])


def _heads(ref, n, w, rows=slice(None)):
    return [ref[rows, h * w:(h + 1) * w] for h in range(n)]


def _put_heads(ref, vals, w, rows=slice(None), add=False):
    for h, val in enumerate(vals):
        if add:
            ref[rows, h * w:(h + 1) * w] += val
        else:
            ref[rows, h * w:(h + 1) * w] = val


def _col(block_w, name, table):
    off, w = table[name]
    assert off % block_w == 0 and w % block_w == 0
    return off // block_w


def _tok_spec(c, w, colblock, n=None):
    if n is None:
        return pl.BlockSpec((c, w), lambda i: (i, colblock))
    return pl.BlockSpec((c, w), lambda i: (n - 1 - i, colblock))


def _full_spec(shape):
    return pl.BlockSpec(shape, lambda i: (0,) * len(shape))


def _matmul(name, a, b, mode, tm, tn, out_dtype=F32):
    (m, kd) = (a.shape[1], a.shape[0]) if mode == "tn" else a.shape
    n = b.shape[0] if mode == "nt" else b.shape[1]
    assert m % tm == 0 and n % tn == 0
    a_spec = pl.BlockSpec((kd, tm), lambda j, i: (0, i)) if mode == "tn" else pl.BlockSpec((tm, kd), lambda j, i: (i, 0))
    b_spec = pl.BlockSpec((tn, kd), lambda j, i: (j, 0)) if mode == "nt" else pl.BlockSpec((kd, tn), lambda j, i: (0, j))

    def body(a_ref, b_ref, o_ref):
        o_ref[...] = lax.dot_general(a_ref[...].astype(BF16), b_ref[...].astype(BF16), DIMS[mode],
                                     preferred_element_type=F32).astype(out_dtype)

    return pl.pallas_call(
        body, name=name, grid=(n // tn, m // tm), in_specs=[a_spec, b_spec],
        out_specs=pl.BlockSpec((tm, tn), lambda j, i: (i, j)), out_shape=jax.ShapeDtypeStruct((m, n), out_dtype),
        compiler_params=_cparams(("arbitrary", "arbitrary")))(a, b)


TOK_TILE = 512


def _norm_fwd(name, x, w):
    t, d = x.shape
    tile = pl.BlockSpec((TOK_TILE, d), lambda i: (i, 0))

    def body(x_ref, w_ref, hn_ref):
        hn_ref[...] = rms(x_ref[...], w_ref[...]).astype(BF16)

    return pl.pallas_call(body, name=name, grid=(t // TOK_TILE,), in_specs=[tile, _full_spec((1, d))], out_specs=tile,
                          out_shape=jax.ShapeDtypeStruct((t, d), BF16), compiler_params=_cparams(("arbitrary",)))(x, w)


def _matmul_fused(name, a, b, mode, tiles, rows, outs, sums, epilogue, comm=(), kinds=()):
    m, kd = a.shape
    n = b.shape[1] if mode == "nn" else b.shape[0]
    tm = TOK_TILE
    steps = m // tm
    nt_, nr, no, ns, ncomm = len(tiles), len(rows), len(outs), len(sums), len(comm)

    def body(*refs):
        a_ref, b_ref = refs[:2]
        at = 2
        tile_refs, row_refs, comm_in = refs[at:at + nt_], refs[at + nt_:at + nt_ + nr], refs[at + nt_ + nr:at + nt_ + nr + ncomm]
        at += nt_ + nr + ncomm
        out_refs, sum_refs, comm_out = refs[at:at + no], refs[at + no:at + no + ns], refs[at + no + ns:at + no + ns + ncomm]
        sems = refs[at + no + ns + ncomm:]
        i = pl.program_id(0)

        @pl.when(i == 0)
        def _():
            if ncomm:
                _comm_start(*_comm_copies(comm_in, comm_out, kinds, *sems))
            for ref in sum_refs:
                ref[...] = jnp.zeros_like(ref)

        acc = lax.dot_general(a_ref[...].astype(BF16), b_ref[...].astype(BF16), DIMS[mode], preferred_element_type=F32)
        res = epilogue(acc, *[r[...] for r in tile_refs], *[r[...] for r in row_refs])
        for ref, val in zip(out_refs, res[:no]):
            ref[...] = val.astype(ref.dtype)
        for ref, val in zip(sum_refs, res[no:]):
            ref[...] += val

        if ncomm:
            @pl.when(i == steps - 1)
            def _():
                _comm_wait(*_comm_copies(comm_in, comm_out, kinds, *sems))

    in_specs = [pl.BlockSpec((tm, kd), lambda i: (i, 0)), _full_spec(b.shape)]
    in_specs += [pl.BlockSpec((tm, w), functools.partial(lambda i, cb: (i, cb), cb=cb)) for _, w, cb in tiles]
    in_specs += [_full_spec(r.shape) for r in rows] + [ANY] * ncomm
    out_specs = [pl.BlockSpec((tm, w), lambda i: (i, 0)) for w, _ in outs] + [_full_spec((1, w)) for w in sums] + [ANY] * ncomm
    out_shape = ([jax.ShapeDtypeStruct((m, w), dt) for w, dt in outs] + [jax.ShapeDtypeStruct((1, w), F32) for w in sums]
                 + _comm_out_shapes(comm, kinds))
    return pl.pallas_call(body, name=name, grid=(steps,), in_specs=in_specs, out_specs=out_specs, out_shape=out_shape,
                          scratch_shapes=_comm_scratch(ncomm) if ncomm else [],
                          compiler_params=_cparams(("arbitrary",)))(a, b, *[t[0] for t in tiles], *rows, *comm)


def _resid_norm(y, x, w):
    h = x + y
    return h, rms(h, w)


def _norm_back(dhn, h, dres, w):
    _, vjp = jax.vjp(rms, h, w)
    dh, dw = vjp(dhn)
    return dh + dres, dw


def _gate_back(dog, *o_and_gate):
    outs, g = o_and_gate[:-1], o_and_gate[-1]
    s = sigmoid(g)
    silu, dsilu = g * s, s * (1.0 + g * (1.0 - s))
    d_outs, c = [], 0
    for o in outs:
        w = o.shape[1]
        d_outs.append(dog[:, c:c + w] * silu[:, c:c + w])
        c += w
    o_all = outs[0] if len(outs) == 1 else jnp.concatenate(outs, axis=1)
    return (*d_outs, dog * o_all * dsilu)


def _loss_head(y1, h1, target, b_out, fw):
    def f(h2, w):
        err = rms(h2, w) - target
        return 0.5 * jnp.sum(jnp.mean(err * err, axis=-1, keepdims=True), axis=0, keepdims=True)

    loss, vjp = jax.vjp(f, h1 + y1 + b_out, fw)
    dh2, dfw = vjp(jnp.ones((1, 1), F32))
    return dh2, jnp.broadcast_to(loss, (1, LANES)), jnp.sum(dh2, axis=0, keepdims=True), dfw


def _gate_fwd(name, outs, proj):
    t = proj.shape[0]
    widths = [o.shape[1] for o in outs]
    n = len(outs)

    def body(*refs):
        o_refs, g_ref, og_ref = refs[:n], refs[n], refs[n + 1]
        c = 0
        for o_ref, w in zip(o_refs, widths):
            g = g_ref[:, c:c + w]
            og_ref[:, c:c + w] = (o_ref[...] * (g * sigmoid(g))).astype(BF16)
            c += w

    in_specs = [pl.BlockSpec((TOK_TILE, w), lambda i: (i, 0)) for w in widths] + [pl.BlockSpec((TOK_TILE, 1024), lambda i: (i, 0))]
    return pl.pallas_call(body, name=name, grid=(t // TOK_TILE,), in_specs=in_specs,
                          out_specs=pl.BlockSpec((TOK_TILE, 1024), lambda i: (i, 0)),
                          out_shape=jax.ShapeDtypeStruct((t, 1024), BF16), compiler_params=_cparams(("arbitrary",)))(*outs, proj)


def _gla_load(q_ref, k_ref, v_ref, gl_ref, up_ref, bias_ref, nw_ref):
    toks = (q_ref[...], k_ref[...], v_ref[...], gl_ref[...])
    params = (up_ref[...], bias_ref[...], nw_ref[...])
    return toks, params


def _gla_specs(c, n=None):
    toks = [_tok_spec(c, 256, _col(256, "gq", C0), n), _tok_spec(c, 256, _col(256, "gk", C0), n),
            _tok_spec(c, 512, _col(512, "gv", C0), n), _tok_spec(c, 128, _col(128, "glow", C0), n)]
    params = [_full_spec((128, 256)), _full_spec((1, 256)), _full_spec((1, 128))]
    return toks, params


def _gla_fwd(proj0, gk_up, gk_bias, norm_w):
    t = proj0.shape[0]
    c = GLA_STEP
    nc = t // c
    toks_s, params_s = _gla_specs(c)

    def body(q_ref, k_ref, v_ref, gl_ref, up_ref, bias_ref, nw_ref, o_ref, st_ref, s_scr):
        @pl.when(pl.program_id(0) == 0)
        def _():
            s_scr[...] = jnp.zeros_like(s_scr)

        st_ref[...] = s_scr[...]
        toks, params = _gla_load(q_ref, k_ref, v_ref, gl_ref, up_ref, bias_ref, nw_ref)
        state = [s_scr[h * GLA_DV:(h + 1) * GLA_DV, :] for h in range(GLA_HEADS)]
        o_ref[...], new = gla_chunk(state, toks, params)
        for h in range(GLA_HEADS):
            s_scr[h * GLA_DV:(h + 1) * GLA_DV, :] = new[h]

    return pl.pallas_call(
        body, name="gla_fwd", grid=(nc,), in_specs=toks_s + params_s,
        out_specs=(_tok_spec(c, 512, 0), pl.BlockSpec((512, GLA_DK), lambda i: (i, 0))),
        out_shape=(jax.ShapeDtypeStruct((t, 512), F32), jax.ShapeDtypeStruct((nc * 512, GLA_DK), F32)),
        scratch_shapes=[pltpu.VMEM((512, GLA_DK), F32)], compiler_params=_cparams(("arbitrary",)))(
            proj0, proj0, proj0, proj0, gk_up, gk_bias, norm_w)


def _gla_bwd(proj0, gk_up, gk_bias, norm_w, states, do):
    t = proj0.shape[0]
    c = GLA_STEP
    nc = t // c
    toks_s, params_s = _gla_specs(c, nc)

    def body(q_ref, k_ref, v_ref, gl_ref, up_ref, bias_ref, nw_ref, st_ref, do_ref,
             dq_ref, dk_ref, dv_ref, dgl_ref, dup_ref, dbias_ref, dnw_ref, ds_scr):
        @pl.when(pl.program_id(0) == 0)
        def _():
            ds_scr[...] = jnp.zeros_like(ds_scr)
            dup_ref[...] = jnp.zeros_like(dup_ref)
            dbias_ref[...] = jnp.zeros_like(dbias_ref)
            dnw_ref[...] = jnp.zeros_like(dnw_ref)

        toks, params = _gla_load(q_ref, k_ref, v_ref, gl_ref, up_ref, bias_ref, nw_ref)
        rows = lambda h: slice(h * GLA_DV, (h + 1) * GLA_DV)
        state = [st_ref[rows(h), :] for h in range(GLA_HEADS)]
        _, vjp = jax.vjp(gla_chunk, state, toks, params)
        dstate_in = [ds_scr[rows(h), :] for h in range(GLA_HEADS)]
        dstate, (dq_ref[...], dk_ref[...], dv_ref[...], dgl_ref[...]), (dup, dbias, dnw) = vjp((do_ref[...], dstate_in))
        dup_ref[...] += dup
        dbias_ref[...] += dbias
        dnw_ref[...] += dnw
        for h in range(GLA_HEADS):
            ds_scr[rows(h), :] = dstate[h]

    rev = lambda w: pl.BlockSpec((c, w), lambda i: (nc - 1 - i, 0))
    return pl.pallas_call(
        body, name="gla_bwd", grid=(nc,),
        in_specs=toks_s + params_s + [pl.BlockSpec((512, GLA_DK), lambda i: (nc - 1 - i, 0)), rev(512)],
        out_specs=(rev(256), rev(256), rev(512), rev(128), _full_spec((128, 256)), _full_spec((1, 256)), _full_spec((1, 128))),
        out_shape=(jax.ShapeDtypeStruct((t, 256), F32), jax.ShapeDtypeStruct((t, 256), F32), jax.ShapeDtypeStruct((t, 512), F32),
                   jax.ShapeDtypeStruct((t, 128), F32), jax.ShapeDtypeStruct((128, 256), F32), jax.ShapeDtypeStruct((1, 256), F32),
                   jax.ShapeDtypeStruct((1, 128), F32)),
        scratch_shapes=[pltpu.VMEM((512, GLA_DK), F32)], compiler_params=_cparams(("arbitrary",)))(
            proj0, proj0, proj0, proj0, gk_up, gk_bias, norm_w, states, do)


RWKV_PARAM_SHAPES = [(1, 512), (1, 512), (1, 512), (1, 128), (1, 128), (1, 512), (128, 512), (1, 512), (128, 512),
                     (1, 512), (1, 512), (1, 512), (1, 512), (1, 512)]
PREV_W = 1792
PREV_COLS = [slice(0, 512), slice(512, 1024), slice(1024, 1536), slice(1536, 1664), slice(1664, 1792)]


def _rwkv_load(r_ref, k_ref, v_ref, xw_ref, xa_ref, p_refs):
    toks = (r_ref[...], k_ref[...], v_ref[...], xw_ref[...], xa_ref[...])
    return toks, tuple(p[...] for p in p_refs)


def _rwkv_state(s_ref, prev_ref):
    n = RWKV_N
    S = [s_ref[h * n:(h + 1) * n, :] for h in range(RWKV_HEADS)]
    return (S,) + tuple(prev_ref[0:1, cols] for cols in PREV_COLS)


def _rwkv_put_state(s_ref, prev_ref, state):
    n = RWKV_N
    for h in range(RWKV_HEADS):
        s_ref[h * n:(h + 1) * n, :] = state[0][h]
    for cols, val in zip(PREV_COLS, state[1:]):
        prev_ref[0:1, cols] = val


def _rwkv_specs(c, n=None):
    toks = [_tok_spec(c, 512, _col(512, "r", C0), n), _tok_spec(c, 512, _col(512, "k", C0), n),
            _tok_spec(c, 512, _col(512, "v", C0), n), _tok_spec(c, 128, _col(128, "xw", C0), n),
            _tok_spec(c, 128, _col(128, "xa", C0), n)]
    return toks, [_full_spec(s) for s in RWKV_PARAM_SHAPES]


def _rwkv_fwd(proj0, params, comm, kinds):
    t = proj0.shape[0]
    c = RWKV_CHUNK
    nc = t // c
    toks_s, params_s = _rwkv_specs(c)
    npar, ncomm = len(params), len(comm)

    def body(*refs):
        tok_refs, p_refs = refs[:5], refs[5:5 + npar]
        comm_in = refs[5 + npar:5 + npar + ncomm]
        o_ref, st_ref, pst_ref = refs[5 + npar + ncomm:8 + npar + ncomm]
        comm_out = refs[8 + npar + ncomm:8 + npar + 2 * ncomm]
        s_scr, prev_scr = refs[8 + npar + 2 * ncomm:10 + npar + 2 * ncomm]
        sems = refs[10 + npar + 2 * ncomm:]
        i = pl.program_id(0)

        @pl.when(i == 0)
        def _():
            _comm_start(*_comm_copies(comm_in, comm_out, kinds, *sems))
            s_scr[...] = jnp.zeros_like(s_scr)
            prev_scr[...] = jnp.zeros_like(prev_scr)

        st_ref[...] = s_scr[...]
        pst_ref[...] = prev_scr[...]
        toks, prm = _rwkv_load(*tok_refs, p_refs)
        o_ref[...], new = rwkv_chunk(_rwkv_state(s_scr, prev_scr), toks, prm)
        _rwkv_put_state(s_scr, prev_scr, new)

        @pl.when(i == nc - 1)
        def _():
            _comm_wait(*_comm_copies(comm_in, comm_out, kinds, *sems))

    outs = pl.pallas_call(
        body, name="rwkv_fwd", grid=(nc,), in_specs=toks_s + params_s + [ANY] * ncomm,
        out_specs=[_tok_spec(c, 512, 0), pl.BlockSpec((512, RWKV_N), lambda i: (i, 0)), pl.BlockSpec((8, PREV_W), lambda i: (i, 0))]
        + [ANY] * ncomm,
        out_shape=[jax.ShapeDtypeStruct((t, 512), F32), jax.ShapeDtypeStruct((nc * 512, RWKV_N), F32),
                   jax.ShapeDtypeStruct((nc * 8, PREV_W), F32)] + _comm_out_shapes(comm, kinds),
        scratch_shapes=[pltpu.VMEM((512, RWKV_N), F32), pltpu.VMEM((8, PREV_W), F32)] + _comm_scratch(ncomm),
        compiler_params=_cparams(("arbitrary",)))(proj0, proj0, proj0, proj0, proj0, *params, *comm)
    return outs[0], outs[1], outs[2], outs[3:]


def _rwkv_bwd(proj0, params, states, prevs, do, comm, kinds):
    t = proj0.shape[0]
    c = RWKV_CHUNK
    nc = t // c
    toks_s, params_s = _rwkv_specs(c, nc)
    npar, ncomm = len(params), len(comm)

    def body(*refs):
        tok_refs, p_refs = refs[:5], refs[5:5 + npar]
        st_ref, pst_ref, do_ref = refs[5 + npar:8 + npar]
        comm_in = refs[8 + npar:8 + npar + ncomm]
        outs = refs[8 + npar + ncomm:]
        dtok_refs, dp_refs, comm_out = outs[:5], outs[5:5 + npar], outs[5 + npar:5 + npar + ncomm]
        ds_scr, dprev_scr = outs[5 + npar + ncomm:7 + npar + ncomm]
        sems = outs[7 + npar + ncomm:]
        i = pl.program_id(0)

        @pl.when(i == 0)
        def _():
            _comm_start(*_comm_copies(comm_in, comm_out, kinds, *sems))
            ds_scr[...] = jnp.zeros_like(ds_scr)
            dprev_scr[...] = jnp.zeros_like(dprev_scr)
            for dp in dp_refs:
                dp[...] = jnp.zeros_like(dp)

        toks, prm = _rwkv_load(*tok_refs, p_refs)
        _, vjp = jax.vjp(rwkv_chunk, _rwkv_state(st_ref, pst_ref), toks, prm)
        dstate, dtoks, dprm = vjp((do_ref[...], _rwkv_state(ds_scr, dprev_scr)))
        for ref, val in zip(dtok_refs, dtoks):
            ref[...] = val
        for ref, val in zip(dp_refs, dprm):
            ref[...] += val
        _rwkv_put_state(ds_scr, dprev_scr, dstate)

        @pl.when(i == nc - 1)
        def _():
            _comm_wait(*_comm_copies(comm_in, comm_out, kinds, *sems))

    rev = lambda w: pl.BlockSpec((c, w), lambda i: (nc - 1 - i, 0))
    outs = pl.pallas_call(
        body, name="rwkv_bwd", grid=(nc,),
        in_specs=toks_s + params_s + [pl.BlockSpec((512, RWKV_N), lambda i: (nc - 1 - i, 0)),
                                      pl.BlockSpec((8, PREV_W), lambda i: (nc - 1 - i, 0)), rev(512)] + [ANY] * ncomm,
        out_specs=[rev(512), rev(512), rev(512), rev(128), rev(128)] + params_s + [ANY] * ncomm,
        out_shape=[jax.ShapeDtypeStruct((t, w), F32) for w in (512, 512, 512, 128, 128)]
        + [jax.ShapeDtypeStruct(s, F32) for s in RWKV_PARAM_SHAPES] + _comm_out_shapes(comm, kinds),
        scratch_shapes=[pltpu.VMEM((512, RWKV_N), F32), pltpu.VMEM((8, PREV_W), F32)] + _comm_scratch(ncomm),
        compiler_params=_cparams(("arbitrary",)))(proj0, proj0, proj0, proj0, proj0, *params, states, prevs, do, *comm)
    return outs[:5], outs[5:5 + npar], outs[5 + npar:]


def _swa_load(q_ref, k_ref, v_ref, cos_ref, sin_ref, bq_ref, bk_ref, bv_ref, sk_ref):
    toks = (_heads(q_ref, 16, SWA_HD), _heads(k_ref, 4, SWA_HD), _heads(v_ref, 4, SWA_HD), cos_ref[...], sin_ref[...])
    params = (_heads(bq_ref, 16, SWA_HD), _heads(bk_ref, 4, SWA_HD), _heads(bv_ref, 4, SWA_HD), _heads(sk_ref, 16, 1))
    return toks, params


def _swa_specs(c, n=None):
    toks = [_tok_spec(c, 1024, _col(1024, "q", C1), n), _tok_spec(c, 256, _col(256, "k", C1), n),
            _tok_spec(c, 256, _col(256, "v", C1), n), _tok_spec(c, SWA_HD, 0, n), _tok_spec(c, SWA_HD, 0, n)]
    params = [_full_spec((1, 1024)), _full_spec((1, 256)), _full_spec((1, 256)), _full_spec((1, 16))]
    return toks, params


def _swa_fwd(proj1, cos, sin, bq, bk, bv, sinks):
    t = proj1.shape[0]
    c = SWA_STEP
    nb = t // c
    toks_s, params_s = _swa_specs(c)
    state_spec = pl.BlockSpec((WINDOW, 256), lambda i: (i, 0))

    def body(q_ref, k_ref, v_ref, cos_ref, sin_ref, bq_ref, bk_ref, bv_ref, sk_ref, o_ref, kst_ref, vst_ref, k_scr, v_scr):
        first = pl.program_id(0) == 0

        @pl.when(first)
        def _():
            k_scr[...] = jnp.zeros_like(k_scr)
            v_scr[...] = jnp.zeros_like(v_scr)

        kst_ref[...] = k_scr[...]
        vst_ref[...] = v_scr[...]
        toks, params = _swa_load(q_ref, k_ref, v_ref, cos_ref, sin_ref, bq_ref, bk_ref, bv_ref, sk_ref)
        outs, (kn, vn) = swa_chunk((_heads(k_scr, 4, SWA_HD), _heads(v_scr, 4, SWA_HD)), toks, params, first)
        _put_heads(o_ref, outs, SWA_HD)
        _put_heads(k_scr, kn, SWA_HD)
        _put_heads(v_scr, vn, SWA_HD)

    return pl.pallas_call(
        body, name="swa_fwd", grid=(nb,), in_specs=toks_s + params_s,
        out_specs=(_tok_spec(c, 1024, 0), state_spec, state_spec),
        out_shape=(jax.ShapeDtypeStruct((t, 1024), F32), jax.ShapeDtypeStruct((nb * WINDOW, 256), F32),
                   jax.ShapeDtypeStruct((nb * WINDOW, 256), F32)),
        scratch_shapes=[pltpu.VMEM((WINDOW, 256), F32), pltpu.VMEM((WINDOW, 256), F32)],
        compiler_params=_cparams(("arbitrary",)))(proj1, proj1, proj1, cos, sin, bq, bk, bv, sinks)


def _swa_bwd(proj1, cos, sin, bq, bk, bv, sinks, kst, vst, do):
    t = proj1.shape[0]
    c = SWA_STEP
    nb = t // c
    toks_s, params_s = _swa_specs(c, nb)
    state_spec = pl.BlockSpec((WINDOW, 256), lambda i: (nb - 1 - i, 0))

    def body(q_ref, k_ref, v_ref, cos_ref, sin_ref, bq_ref, bk_ref, bv_ref, sk_ref, kst_ref, vst_ref, do_ref,
             dq_ref, dk_ref, dv_ref, dbq_ref, dbk_ref, dbv_ref, dsk_ref, dk_scr, dv_scr):
        i = pl.program_id(0)

        @pl.when(i == 0)
        def _():
            dk_scr[...] = jnp.zeros_like(dk_scr)
            dv_scr[...] = jnp.zeros_like(dv_scr)
            for ref in (dbq_ref, dbk_ref, dbv_ref, dsk_ref):
                ref[...] = jnp.zeros_like(ref)

        first = i == nb - 1
        toks, params = _swa_load(q_ref, k_ref, v_ref, cos_ref, sin_ref, bq_ref, bk_ref, bv_ref, sk_ref)
        f = functools.partial(swa_chunk, first=first)
        _, vjp = jax.vjp(f, (_heads(kst_ref, 4, SWA_HD), _heads(vst_ref, 4, SWA_HD)), toks, params)
        dstate_in = (_heads(dk_scr, 4, SWA_HD), _heads(dv_scr, 4, SWA_HD))
        (dkp, dvp), (dq, dk, dv, _, _), (dbq, dbk, dbv, dsk) = vjp((_heads(do_ref, 16, SWA_HD), dstate_in))
        _put_heads(dq_ref, dq, SWA_HD)
        _put_heads(dk_ref, dk, SWA_HD)
        _put_heads(dv_ref, dv, SWA_HD)
        _put_heads(dbq_ref, dbq, SWA_HD, add=True)
        _put_heads(dbk_ref, dbk, SWA_HD, add=True)
        _put_heads(dbv_ref, dbv, SWA_HD, add=True)
        _put_heads(dsk_ref, dsk, 1, add=True)
        _put_heads(dk_scr, dkp, SWA_HD)
        _put_heads(dv_scr, dvp, SWA_HD)

    rev = lambda w: pl.BlockSpec((c, w), lambda i: (nb - 1 - i, 0))
    return pl.pallas_call(
        body, name="swa_bwd", grid=(nb,), in_specs=toks_s + params_s + [state_spec, state_spec, rev(1024)],
        out_specs=(rev(1024), rev(256), rev(256), _full_spec((1, 1024)), _full_spec((1, 256)), _full_spec((1, 256)), _full_spec((1, 16))),
        out_shape=(jax.ShapeDtypeStruct((t, 1024), F32), jax.ShapeDtypeStruct((t, 256), F32), jax.ShapeDtypeStruct((t, 256), F32),
                   jax.ShapeDtypeStruct((1, 1024), F32), jax.ShapeDtypeStruct((1, 256), F32), jax.ShapeDtypeStruct((1, 256), F32),
                   jax.ShapeDtypeStruct((1, 16), F32)),
        scratch_shapes=[pltpu.VMEM((WINDOW, 256), F32), pltpu.VMEM((WINDOW, 256), F32)],
        compiler_params=_cparams(("arbitrary",)))(proj1, proj1, proj1, cos, sin, bq, bk, bv, sinks, kst, vst, do)


MESH = pl.DeviceIdType.MESH
ANY = pl.BlockSpec(memory_space=pl.ANY)


def _my_place():
    return lax.axis_index("x"), lax.axis_index("y"), lax.axis_index("c")


def _all_gather(shards):
    n = len(shards)

    def body(*refs):
        in_refs, out_refs = refs[:n], refs[n:2 * n]
        send_sems, recv_sems, local_sems = refs[2 * n:]
        x, y, c = _my_place()
        me, sibling = (x, y, c), (x, y, 1 - c)
        chips = [(1 - x, y), (x, 1 - y), (1 - x, 1 - y)]

        def slot(out_ref, place):
            px, py, pc = place
            return out_ref.at[4 * px + 2 * py + pc]

        def copy(a, k, block, to, src=None):
            return pltpu.make_async_remote_copy(
                src_ref=slot(out_refs[a], block) if src is None else src, dst_ref=slot(out_refs[a], block),
                send_sem=send_sems.at[a, k], recv_sem=recv_sems.at[a, k], device_id=to, device_id_type=MESH)

        mine = [pltpu.make_async_copy(in_refs[a], slot(out_refs[a], me), local_sems.at[a]) for a in range(n)]
        for cp in mine:
            cp.start()
        first = []
        for a in range(n):
            first.append(copy(a, 0, me, sibling, src=in_refs[a]))
            first += [copy(a, 1 + j, me, (*chip, c), src=in_refs[a]) for j, chip in enumerate(chips)]
        for cp in first:
            cp.start()
        passed = []
        for j, chip in enumerate(chips):
            for a in range(n):
                copy(a, 1 + j, (*chip, c), me).wait_recv()
                fwd = copy(a, 4 + j, (*chip, c), sibling)
                fwd.start()
                passed.append(fwd)
        for a in range(n):
            copy(a, 0, sibling, me).wait_recv()
            for j, chip in enumerate(chips):
                copy(a, 4 + j, (*chip, 1 - c), me).wait_recv()
        for cp in first + passed:
            cp.wait_send()
        for cp in mine:
            cp.wait()

    return pl.pallas_call(
        body, name="all_gather_weights", in_specs=[ANY] * n, out_specs=[ANY] * n,
        out_shape=[jax.ShapeDtypeStruct((N_DEV,) + s.shape, s.dtype) for s in shards],
        scratch_shapes=_comm_scratch(n))(*shards)


def _comm_copies(in_refs, out_refs, kinds, send_sems, recv_sems, local_sems):
    x, y, c = _my_place()
    my_idx = 4 * x + 2 * y + c
    src = lambda a, idx: in_refs[a] if kinds[a] == "gather" else in_refs[a].at[idx]
    local = [pltpu.make_async_copy(src(a, my_idx), out_refs[a].at[my_idx], local_sems.at[a]) for a in range(len(kinds))]
    remote = []
    for rel in range(1, N_DEV):
        px, py, pc = x ^ ((rel >> 2) & 1), y ^ ((rel >> 1) & 1), c ^ (rel & 1)
        for a in range(len(kinds)):
            remote.append(pltpu.make_async_remote_copy(
                src_ref=src(a, 4 * px + 2 * py + pc), dst_ref=out_refs[a].at[my_idx], send_sem=send_sems.at[a, rel - 1],
                recv_sem=recv_sems.at[a, rel - 1], device_id=(px, py, pc), device_id_type=MESH))
    return local, remote


def _comm_start(local, remote):
    for cp in local + remote:
        cp.start()


def _comm_wait(local, remote):
    for cp in remote:
        cp.wait_recv()
    for cp in remote:
        cp.wait_send()
    for cp in local:
        cp.wait()


def _comm_out_shapes(arrays, kinds):
    return [jax.ShapeDtypeStruct(((N_DEV,) + a.shape) if k == "gather" else a.shape, a.dtype) for a, k in zip(arrays, kinds)]


def _comm_scratch(n):
    return [pltpu.SemaphoreType.DMA((n, N_DEV - 1)), pltpu.SemaphoreType.DMA((n, N_DEV - 1)), pltpu.SemaphoreType.DMA((n,))]


def _exchange(arrays, kinds):
    n = len(arrays)

    def body(*refs):
        copies = _comm_copies(refs[:n], refs[n:2 * n], kinds, *refs[2 * n:])
        _comm_start(*copies)
        _comm_wait(*copies)

    return pl.pallas_call(body, name="exchange_grads", in_specs=[ANY] * n, out_specs=[ANY] * n,
                          out_shape=_comm_out_shapes(arrays, kinds), scratch_shapes=_comm_scratch(n))(*arrays)


def _adam_math(w, g, m, v):
    m = ADAM_B1 * m + (1.0 - ADAM_B1) * g
    v = ADAM_B2 * v + (1.0 - ADAM_B2) * (g * g)
    m_hat = m / (1.0 - ADAM_B1 ** ADAM_STEP)
    v_hat = v / (1.0 - ADAM_B2 ** ADAM_STEP)
    delta = -ADAM_LR * (m_hat / (jnp.sqrt(v_hat) + ADAM_EPS) + ADAM_WD * w)
    return delta, m, v


def _adamw(name, w, gslots, m, v, tc):
    r, cc = w.shape
    assert cc % tc == 0
    tile = pl.BlockSpec((r, tc), lambda i: (0, i))

    def body(w_ref, g_ref, m_ref, v_ref, go_ref, d_ref, mo_ref, vo_ref):
        g = g_ref[0].astype(F32)
        for s in range(1, N_DEV):
            g = g + g_ref[s].astype(F32)
        d, mn, vn = _adam_math(w_ref[...], g, m_ref[...], v_ref[...])
        go_ref[...] = g
        d_ref[...] = d
        mo_ref[...] = mn
        vo_ref[...] = vn

    shp = jax.ShapeDtypeStruct((r, cc), F32)
    return pl.pallas_call(body, name=name, grid=(cc // tc,),
                          in_specs=[tile, pl.BlockSpec((N_DEV, r, tc), lambda i: (0, 0, i)), tile, tile],
                          out_specs=(tile,) * 4, out_shape=(shp,) * 4, compiler_params=_cparams(("arbitrary",)))(w, gslots, m, v)


PACK_TILE = 8 * LANES


def _packed_rows(shape, mode):
    r, w = shape
    return -(-r // 8) * 8 if mode == "rows" else -(-(r * w) // PACK_TILE) * 8


def _pack_small(arrays, modes, lead=False):
    out = []
    for a, mode in zip(arrays, modes):
        a = a.astype(F32) if lead else a.astype(F32)[None]
        if mode == "rows":
            out.append(jnp.pad(a, ((0, 0), (0, (-a.shape[1]) % 8), (0, LANES - a.shape[2]))))
        else:
            flat = a.reshape(a.shape[0], -1)
            out.append(jnp.pad(flat, ((0, 0), (0, (-flat.shape[1]) % PACK_TILE))).reshape(a.shape[0], -1, LANES))
    out = jnp.concatenate(out, axis=1)
    return out if lead else out[0]


def _take_small(packed, row0, shape, mode):
    r, w = shape
    lead = packed.ndim == 3
    if mode == "rows":
        return packed[:, row0:row0 + r, :w] if lead else packed[row0:row0 + r, :w]
    per_row = -(-w // LANES)
    if lead:
        return packed[:, row0:row0 + r * per_row].reshape(packed.shape[0], r, per_row * LANES)[:, :, :w]
    rows = []
    for i in range(r):
        pieces = [packed[row0 + i * per_row + j:row0 + i * per_row + j + 1, :] for j in range(per_row)]
        rows.append((pieces[0] if per_row == 1 else jnp.concatenate(pieces, axis=1))[:, :w])
    return rows[0] if r == 1 else jnp.concatenate(rows, axis=0)


def _adamw_small(slots, specs, ws, ms, vs, loss_row):
    n = len(specs)

    def body(*refs):
        slots_ref, w_refs, m_refs, v_refs = refs[0], refs[1:1 + n], refs[1 + n:1 + 2 * n], refs[1 + 2 * n:1 + 3 * n]
        out_refs, loss_ref = refs[1 + 3 * n:1 + 7 * n], refs[1 + 7 * n]
        gp = slots_ref[0]
        for s in range(1, N_DEV):
            gp = gp + slots_ref[s]
        read = lambda ref: ref[0] if len(ref.shape) == 3 else ref[...]
        for k, (shape, mode, row0) in enumerate(specs):
            g = _take_small(gp, row0, shape, mode)
            d, mn, vn = _adam_math(read(w_refs[k]), g, read(m_refs[k]), read(v_refs[k]))
            for ref, val in zip(out_refs[4 * k:4 * k + 4], (g, d, mn, vn)):
                if len(ref.shape) == 3:
                    ref[0] = val
                else:
                    ref[...] = val
        loss_ref[...] = gp[loss_row:loss_row + 1, :]

    vmem = pl.BlockSpec(memory_space=pltpu.VMEM)
    out_shape = [jax.ShapeDtypeStruct(w.shape, F32) for w in ws for _ in range(4)] + [jax.ShapeDtypeStruct((1, LANES), F32)]
    outs = pl.pallas_call(body, name="adamw_small", in_specs=[vmem] * (1 + 3 * n), out_specs=[vmem] * (4 * n + 1),
                          out_shape=out_shape)(slots, *ws, *ms, *vs)
    return [outs[4 * k:4 * k + 4] for k in range(n)], outs[4 * n]


def _rope_tables(t):
    half = 8
    inv_freq = ROPE_THETA ** (-jnp.arange(half, dtype=F32) / half)
    ang = jnp.arange(t, dtype=F32)[:, None] * inv_freq
    cos = jnp.concatenate([jnp.cos(ang), jnp.cos(ang), jnp.ones((t, SWA_HD - 16), F32)], axis=1)
    sin = jnp.concatenate([jnp.sin(ang), jnp.sin(ang), jnp.zeros((t, SWA_HD - 16), F32)], axis=1)
    return cos, sin


def _pad_to(a, rows=None, cols=None):
    r = 0 if rows is None else rows - a.shape[0]
    c = 0 if cols is None else cols - a.shape[1]
    return jnp.pad(a, ((0, r), (0, c)))


ORIG0 = dict(gq=(0, 256), gk=(256, 256), gv=(512, 512), glow=(1024, 16), r=(1040, 512), k=(1552, 512), v=(2064, 512),
             xw=(2576, 64), xa=(2640, 64), gate=(2704, 1024))
ORIG0_ORDER = ["gq", "gk", "gv", "glow", "r", "k", "v", "xw", "xa", "gate"]


def _w0t_to_padded(wt):
    rows, at = [], 0
    for name, (off, width) in sorted(C0.items(), key=lambda kv: kv[1][0]):
        assert off == at
        src, src_w = ORIG0[name]
        rows.append(_pad_to(wt[src:src + src_w], rows=width))
        at += width
    rows.append(jnp.zeros((N0P - at, wt.shape[1]), wt.dtype))
    return jnp.concatenate(rows, axis=0)


def _w0t_from_padded(wpt):
    return jnp.concatenate([wpt[C0[n][0]:C0[n][0] + ORIG0[n][1]] for n in ORIG0_ORDER], axis=0)


def _w1t_to_mine(wt):
    return jnp.concatenate([wt[1536:2560], wt[:1536]], axis=0)


def _w1t_from_mine(wt):
    return jnp.concatenate([wt[1024:2560], wt[:1024]], axis=0)


def kernel(x, norm_w, w_in0, gla_gk_up, gla_gk_bias, gla_norm_w, rwkv_mu, rwkv_w0, rwkv_w_up, rwkv_a0, rwkv_a_up, rwkv_k_k, rwkv_k_a, rwkv_r_k, rwkv_ln_w, rwkv_ln_b, w_out0, w_in1, b_in1, attn_sinks, w_out1, b_out1, final_norm_w, loss_target, m_norm_w, m_w_in0, m_gla_gk_up, m_gla_gk_bias, m_gla_norm_w, m_rwkv_mu, m_rwkv_w0, m_rwkv_w_up, m_rwkv_a0, m_rwkv_a_up, m_rwkv_k_k, m_rwkv_k_a, m_rwkv_r_k, m_rwkv_ln_w, m_rwkv_ln_b, m_w_out0, m_w_in1, m_b_in1, m_attn_sinks, m_w_out1, m_b_out1, m_final_norm_w, v_norm_w, v_w_in0, v_gla_gk_up, v_gla_gk_bias, v_gla_norm_w, v_rwkv_mu, v_rwkv_w0, v_rwkv_w_up, v_rwkv_a0, v_rwkv_a_up, v_rwkv_k_k, v_rwkv_k_a, v_rwkv_r_k, v_rwkv_ln_w, v_rwkv_ln_b, v_w_out0, v_w_in1, v_b_in1, v_attn_sinks, v_w_out1, v_b_out1, v_final_norm_w):
    weights = dict(norm_w=norm_w, w_in0=w_in0, gla_gk_up=gla_gk_up, gla_gk_bias=gla_gk_bias, gla_norm_w=gla_norm_w, rwkv_mu=rwkv_mu,
                   rwkv_w0=rwkv_w0, rwkv_w_up=rwkv_w_up, rwkv_a0=rwkv_a0, rwkv_a_up=rwkv_a_up, rwkv_k_k=rwkv_k_k, rwkv_k_a=rwkv_k_a,
                   rwkv_r_k=rwkv_r_k, rwkv_ln_w=rwkv_ln_w, rwkv_ln_b=rwkv_ln_b, w_out0=w_out0, w_in1=w_in1, b_in1=b_in1,
                   attn_sinks=attn_sinks, w_out1=w_out1, b_out1=b_out1, final_norm_w=final_norm_w)
    moms = dict(norm_w=m_norm_w, w_in0=m_w_in0, gla_gk_up=m_gla_gk_up, gla_gk_bias=m_gla_gk_bias, gla_norm_w=m_gla_norm_w,
                rwkv_mu=m_rwkv_mu, rwkv_w0=m_rwkv_w0, rwkv_w_up=m_rwkv_w_up, rwkv_a0=m_rwkv_a0, rwkv_a_up=m_rwkv_a_up,
                rwkv_k_k=m_rwkv_k_k, rwkv_k_a=m_rwkv_k_a, rwkv_r_k=m_rwkv_r_k, rwkv_ln_w=m_rwkv_ln_w, rwkv_ln_b=m_rwkv_ln_b,
                w_out0=m_w_out0, w_in1=m_w_in1, b_in1=m_b_in1, attn_sinks=m_attn_sinks, w_out1=m_w_out1, b_out1=m_b_out1,
                final_norm_w=m_final_norm_w)
    vars_ = dict(norm_w=v_norm_w, w_in0=v_w_in0, gla_gk_up=v_gla_gk_up, gla_gk_bias=v_gla_gk_bias, gla_norm_w=v_gla_norm_w,
                 rwkv_mu=v_rwkv_mu, rwkv_w0=v_rwkv_w0, rwkv_w_up=v_rwkv_w_up, rwkv_a0=v_rwkv_a0, rwkv_a_up=v_rwkv_a_up,
                 rwkv_k_k=v_rwkv_k_k, rwkv_k_a=v_rwkv_k_a, rwkv_r_k=v_rwkv_r_k, rwkv_ln_w=v_rwkv_ln_w, rwkv_ln_b=v_rwkv_ln_b,
                 w_out0=v_w_out0, w_in1=v_w_in1, b_in1=v_b_in1, attn_sinks=v_attn_sinks, w_out1=v_w_out1, b_out1=v_b_out1,
                 final_norm_w=v_final_norm_w)
    names = list(weights)
    big = ["w_in0", "w_out0", "w_in1", "w_out1"]
    small_sharded = ["gla_gk_up", "rwkv_w_up", "rwkv_a_up", "b_in1", "b_out1"]
    replicated = [n for n in names if n not in big and n not in small_sharded]

    xs = x[0]
    tgt = loss_target[0]
    t = xs.shape[0]

    def view(w):
        shape = tuple(w.shape[-2:]) if w.ndim >= 2 else (1, w.shape[0])
        return shape, ("rows" if shape[0] > 1 and shape[1] <= LANES else "flat")

    def layout(ns, row0=0):
        specs = []
        for n in ns:
            shape, mode = view(weights[n])
            specs.append((shape, mode, row0))
            row0 += _packed_rows(shape, mode)
        return specs, row0

    sh_specs, n_shard_rows = layout(small_sharded)
    rep_specs, loss_row = layout(replicated, n_shard_rows)
    sh_modes, rep_modes = [s[1] for s in sh_specs], [s[1] for s in rep_specs]

    small_shard_pack = _pack_small([weights[n].reshape(view(weights[n])[0]) for n in small_sharded], sh_modes)
    g_in0, g_small = _all_gather([w_in0[0].T.astype(BF16), small_shard_pack])
    w0t = _w0t_to_padded(g_in0.reshape(-1, D_MODEL))
    later_shards = [w_out0[0].astype(BF16), w_in1[0].T.astype(BF16), w_out1[0].astype(BF16)]
    gs = [_take_small(g_small, row0, shape, mode) for shape, mode, row0 in sh_specs]
    join_cols = lambda a: jnp.transpose(a, (1, 0, 2)).reshape(a.shape[1], -1)
    gk_up, w_up, a_up = join_cols(gs[0]), join_cols(gs[1]), join_cols(gs[2])
    b_in, b_out = gs[3].reshape(1, -1), gs[4].reshape(1, -1)

    gk_up_p = _pad_to(gk_up, rows=128)
    mu = rwkv_mu
    rwkv_params = [mu[:, 0:512], mu[:, 512:1024], mu[:, 1024:1536], _pad_to(mu[:, 1536:1600], cols=128), _pad_to(mu[:, 1600:1664], cols=128),
                   rwkv_w0, _pad_to(w_up, rows=128), rwkv_a0, _pad_to(a_up, rows=128), rwkv_k_k, rwkv_k_a, rwkv_r_k.reshape(1, 512),
                   rwkv_ln_w, rwkv_ln_b]
    bq, bk, bv = b_in[:, :1024], b_in[:, 1024:1280], b_in[:, 1280:1536]
    cos, sin = _rope_tables(t)
    nw0, nw1, fw = norm_w[0:1], norm_w[1:2], final_norm_w.reshape(1, D_MODEL)

    d = D_MODEL
    wide = lambda arr: (arr, d, 0)
    hn0 = _norm_fwd("norm0_fwd", xs, nw0)
    proj0 = _matmul("proj0", hn0, w0t, "nt", 1024, 1024)
    o_a, gla_states = _gla_fwd(proj0, gk_up_p, gla_gk_bias, gla_norm_w)
    o_b, rwkv_states, rwkv_prevs, (g_out0, g_in1, g_out1) = _rwkv_fwd(proj0, rwkv_params, later_shards, ["gather"] * 3)
    wo0 = g_out0.reshape(1024, D_MODEL)
    w1t = _w1t_to_mine(g_in1.reshape(-1, D_MODEL))
    wo1 = g_out1.reshape(1024, D_MODEL)
    og0 = _gate_fwd("gate0_fwd", [o_a, o_b], proj0)
    h1, hn1 = _matmul_fused("out0_norm1", og0, wo0, "nn", [wide(xs)], [nw1], [(d, F32), (d, BF16)], [], _resid_norm)
    proj1 = _matmul("proj1", hn1, w1t, "nt", 1024, 1280)
    o_c, kst, vst = _swa_fwd(proj1, cos, sin, bq, bk, bv, attn_sinks)
    og1 = _gate_fwd("gate1_fwd", [o_c], proj1)
    dh2, loss_part, d_b_out, d_fw = _matmul_fused("out1_loss", og1, wo1, "nn", [wide(h1), wide(tgt)], [b_out, fw],
                                                  [(d, F32)], [LANES, d, d], _loss_head)

    d_oc, d_gate1 = _matmul_fused("out1_dx_gate1", dh2, wo1, "nt", [wide(o_c), wide(proj1)], [], [(d, F32), (d, F32)], [], _gate_back)
    d_wo1 = _matmul("out1_dw", og1, dh2, "tn", 512, 512, BF16)
    dq, dk, dv, d_bq, d_bk, d_bv, d_sinks = _swa_bwd(proj1, cos, sin, bq, bk, bv, attn_sinks, kst, vst, d_oc)
    dproj1 = jnp.concatenate([d_gate1, dq, dk, dv], axis=1).astype(BF16)
    dh1, d_nw1 = _matmul_fused("proj1_dx_norm1", dproj1, w1t, "nn", [wide(h1), wide(dh2)], [nw1], [(d, F32)], [d], _norm_back)
    d_w1t = _matmul("proj1_dw", dproj1, hn1, "tn", 512, 1024, BF16)
    d_oa, d_ob, d_gate0 = _matmul_fused("out0_dx_gate0", dh1, wo0, "nt", [(o_a, 512, 0), (o_b, 512, 0), wide(proj0)], [],
                                        [(512, F32), (512, F32), (d, F32)], [], _gate_back)
    d_wo0 = _matmul("out0_dw", og0, dh1, "tn", 512, 512, BF16)
    dgq, dgk, dgv, dglow, d_gk_up, d_gk_bias, d_gla_nw = _gla_bwd(proj0, gk_up_p, gla_gk_bias, gla_norm_w, gla_states, d_oa)
    row_blocks = lambda a: a.astype(BF16).reshape(N_DEV, -1, D_MODEL)
    early = [row_blocks(_w1t_from_mine(d_w1t)), row_blocks(d_wo1), row_blocks(d_wo0)]
    (dr, dkk, dvv, dxw, dxa), d_rp, (r_in1, r_out1, r_out0) = _rwkv_bwd(
        proj0, rwkv_params, rwkv_states, rwkv_prevs, d_ob, early, ["scatter"] * 3)
    dproj0 = jnp.concatenate([d_gate0, dgv, dr, dkk, dvv, dgq, dgk, dglow, dxw, dxa, jnp.zeros((t, 128), F32)], axis=1).astype(BF16)
    d_w0 = _w0t_from_padded(_matmul("proj0_dw", dproj0, hn0, "tn", 512, 1024, BF16))
    grad_x, d_nw0, r_in0 = _matmul_fused("proj0_dx_norm0", dproj0, w0t, "nn", [wide(xs), wide(dh1)], [nw0], [(d, F32)], [d],
                                         _norm_back, [row_blocks(d_w0)], ["scatter"])

    contrib = dict(
        norm_w=jnp.concatenate([d_nw0, d_nw1], axis=0), gla_gk_bias=d_gk_bias, gla_norm_w=d_gla_nw,
        rwkv_mu=jnp.concatenate([d_rp[0], d_rp[1], d_rp[2], d_rp[3][:, :64], d_rp[4][:, :64]], axis=1),
        rwkv_w0=d_rp[5], rwkv_a0=d_rp[7], rwkv_k_k=d_rp[9], rwkv_k_a=d_rp[10], rwkv_r_k=d_rp[11].reshape(RWKV_HEADS, RWKV_N),
        rwkv_ln_w=d_rp[12], rwkv_ln_b=d_rp[13], attn_sinks=d_sinks, final_norm_w=d_fw)
    rep_pack = _pack_small([contrib[n] for n in replicated] + [loss_part[:, :1]], rep_modes + ["flat"])

    d_b_in = jnp.concatenate([d_bq, d_bk, d_bv], axis=1)
    full_small = [d_gk_up[:16], d_rp[6][:64], d_rp[8][:64], d_b_in, d_b_out]
    split_cols = lambda a: jnp.transpose(a.reshape(a.shape[0], N_DEV, -1), (1, 0, 2))
    small_parts = [split_cols(a) for a in full_small]
    small_pack = _pack_small(small_parts, sh_modes, lead=True)
    r_small, r_rep = _exchange([small_pack, rep_pack], ["scatter", "gather"])

    res = {}
    res["w_in0"] = tuple(a.T[None] for a in _adamw("adamw_w_in0", w_in0[0].T, r_in0, m_w_in0[0].T, v_w_in0[0].T, 256))
    res["w_out0"] = tuple(a[None] for a in _adamw("adamw_w_out0", w_out0[0], r_out0, m_w_out0[0], v_w_out0[0], 256))
    res["w_in1"] = tuple(a.T[None] for a in _adamw("adamw_w_in1", w_in1[0].T, r_in1, m_w_in1[0].T, v_w_in1[0].T, 256))
    res["w_out1"] = tuple(a[None] for a in _adamw("adamw_w_out1", w_out1[0], r_out1, m_w_out1[0], v_w_out1[0], 256))
    small_names = small_sharded + replicated
    slots = jnp.concatenate([r_small, r_rep], axis=1)
    as_2d = lambda a: a.reshape(1, -1) if a.ndim == 1 else a
    small_res, loss_row_out = _adamw_small(slots, sh_specs + rep_specs, [as_2d(weights[n]) for n in small_names],
                                           [as_2d(moms[n]) for n in small_names], [as_2d(vars_[n]) for n in small_names], loss_row)
    for n, vals in zip(small_names, small_res):
        res[n] = tuple(val.reshape(weights[n].shape) for val in vals)
    loss = loss_row_out[0, 0]
    return (loss, grad_x[None], *[res[n][0] for n in names], *[res[n][1] for n in names],
            *[res[n][2] for n in names], *[res[n][3] for n in names])
```

```python
import functools

import jax
import jax.numpy as jnp
from jax import lax
from jax.experimental import pallas as pl
from jax.experimental.pallas import tpu as pltpu

F32 = jnp.float32
BF16 = jnp.bfloat16
HI = lax.Precision.HIGHEST

D_MODEL = 1024
NORM_EPS = 1e-5
GLA_HEADS, GLA_DK, GLA_DV = 4, 64, 128
GLA_NORMALIZER = 16.0
GLA_CHUNK = 64
GLA_STEP = 256
RWKV_HEADS, RWKV_N = 8, 64
RWKV_LN_EPS = 64e-5
RWKV_CHUNK = 128
SWA_Q_HEADS, SWA_KV_HEADS, SWA_GROUP, SWA_HD = 16, 4, 4, 64
WINDOW = 128
SWA_STEP = 256
ROPE_THETA = 500000.0
NEG = -1e30
N_DEV = 8
LANES = 128

ADAM_LR, ADAM_B1, ADAM_B2, ADAM_EPS, ADAM_WD, ADAM_STEP = 0.001, 0.9, 0.999, 1e-08, 0.01, 10

N0P = 4096
C0 = dict(gate=(0, 1024), gv=(1024, 512), r=(1536, 512), k=(2048, 512), v=(2560, 512), gq=(3072, 256), gk=(3328, 256),
          glow=(3584, 128), xw=(3712, 128), xa=(3840, 128))
N1P = 2560
C1 = dict(gate=(0, 1024), q=(1024, 1024), k=(2048, 256), v=(2304, 256))

VMEM_LIMIT = 56 * 1024 * 1024

P_LORA = 1
P_GLA = 1
P_RWKV_G = 2
P_RWKV = 1
P_SWA = 1


def _cparams(sem=None):
    return pltpu.CompilerParams(dimension_semantics=sem, vmem_limit_bytes=VMEM_LIMIT)


DIMS = dict(nn=(((1,), (0,)), ((), ())), nt=(((1,), (1,)), ((), ())), tn=(((0,), (0,)), ((), ())))


def _split_bf16(a):
    hi = a.astype(BF16)
    return hi, (a - hi.astype(F32)).astype(BF16)


def _dot(a, b, mode, passes):
    dg = lambda p, q: lax.dot_general(p, q, DIMS[mode], preferred_element_type=F32)
    if passes == 1:
        return dg(a.astype(BF16), b.astype(BF16))
    if passes == 2:
        ah, (bh, bl) = a.astype(BF16), _split_bf16(b)
        return dg(ah, bh) + dg(ah, bl)
    if passes == 3:
        (ah, al), (bh, bl) = _split_bf16(a), _split_bf16(b)
        return dg(ah, bh) + dg(al, bh) + dg(ah, bl)
    return lax.dot_general(a, b, DIMS[mode], precision=HI, preferred_element_type=F32)


@functools.partial(jax.custom_vjp, nondiff_argnums=(2, 3))
def mmx(a, b, mode, passes):
    return _dot(a, b, mode, passes)


def _mmx_fwd(a, b, mode, passes):
    return _dot(a, b, mode, passes), (a, b)


def _mmx_bwd(mode, passes, res, g):
    a, b = res
    if mode == "nn":
        return _dot(g, b, "nt", passes), _dot(a, g, "tn", passes)
    if mode == "nt":
        return _dot(g, b, "nn", passes), _dot(g, a, "tn", passes)
    return _dot(b, g, "nt", passes), _dot(a, g, "nn", passes)


mmx.defvjp(_mmx_fwd, _mmx_bwd)


def _tri_dot(tri, x):
    t = tri.astype(BF16)
    x1 = x.astype(BF16)
    r1 = x - x1.astype(F32)
    x2 = r1.astype(BF16)
    x3 = (r1 - x2.astype(F32)).astype(BF16)
    dg = lambda q: jnp.dot(t, q, preferred_element_type=F32)
    return dg(x1) + dg(x2) + dg(x3)


@jax.custom_vjp
def cumsum_rows(x):
    return _tri_dot(tril_ones(x.shape[0]), x)


def _cumsum_fwd(x):
    return cumsum_rows(x), None


def _cumsum_bwd(_, g):
    i, j = _iota2(g.shape[0], g.shape[0])
    return (_tri_dot(jnp.where(i <= j, 1.0, 0.0).astype(F32), g),)


cumsum_rows.defvjp(_cumsum_fwd, _cumsum_bwd)


def _head_dot(x):
    i, j = _iota2(LANES, LANES)
    shift = RWKV_N.bit_length() - 1
    same = jnp.where(jnp.right_shift(i, shift) == jnp.right_shift(j, shift), 1.0, 0.0).astype(F32)
    return jnp.concatenate([_ones_right(x[:, g * LANES:(g + 1) * LANES], same) for g in range(x.shape[1] // LANES)], axis=1)


def _ones_right(x, ones):
    t = ones.astype(BF16)
    x1 = x.astype(BF16)
    r1 = x - x1.astype(F32)
    x2 = r1.astype(BF16)
    x3 = (r1 - x2.astype(F32)).astype(BF16)
    dg = lambda q: jnp.dot(q, t, preferred_element_type=F32)
    return dg(x1) + dg(x2) + dg(x3)


@jax.custom_vjp
def head_sum(x):
    return _head_dot(x)


def _head_sum_fwd(x):
    return head_sum(x), None


def _head_sum_bwd(_, g):
    return (_head_dot(g),)


head_sum.defvjp(_head_sum_fwd, _head_sum_bwd)


def cat_rows(*xs):
    return jnp.concatenate(xs, axis=0)


def _iota2(n, m):
    return lax.broadcasted_iota(jnp.int32, (n, m), 0), lax.broadcasted_iota(jnp.int32, (n, m), 1)


def tril_ones(c, strict=False):
    i, j = _iota2(c, c)
    return jnp.where((i > j) if strict else (i >= j), 1.0, 0.0).astype(F32)


def row_of(x, r):
    i = lax.broadcasted_iota(jnp.int32, x.shape, 0)
    return jnp.sum(jnp.where(i == r, x, 0.0), axis=0, keepdims=True)


@jax.custom_vjp
def shift_rows(x, prev):
    r = lax.broadcasted_iota(jnp.int32, x.shape, 0)
    return jnp.where(r == 0, prev, pltpu.roll(x, 1, 0))


def _shift_fwd(x, prev):
    return shift_rows(x, prev), None


def _shift_bwd(_, g):
    c = g.shape[0]
    r = lax.broadcasted_iota(jnp.int32, g.shape, 0)
    return jnp.where(r == c - 1, 0.0, pltpu.roll(g, c - 1, 0)), row_of(g, 0)


shift_rows.defvjp(_shift_fwd, _shift_bwd)


def log_sigmoid(x):
    return jnp.minimum(x, 0.0) - jnp.log(1.0 + jnp.exp(-jnp.abs(x)))


def softplus(x):
    return jnp.maximum(x, 0.0) + jnp.log(1.0 + jnp.exp(-jnp.abs(x)))


def sigmoid(x):
    return 1.0 / (1.0 + jnp.exp(-x))


def rms(x, w, eps=NORM_EPS):
    return x * lax.rsqrt(jnp.mean(x * x, axis=-1, keepdims=True) + eps) * w


def gla_chunk(state, toks, params):
    q, k, v, glow = toks
    gk_up, bias, norm_w = params
    c = GLA_CHUNK
    subs, heads = range(glow.shape[0] // c), range(GLA_HEADS)
    rows = lambda x, j: x[j * c:(j + 1) * c]
    hk = lambda x, h: x[:, h * GLA_DK:(h + 1) * GLA_DK]
    hv = lambda x, h: x[:, h * GLA_DV:(h + 1) * GLA_DV]
    ltri = tril_ones(c)
    g = log_sigmoid(mmx(glow, gk_up, "nn", P_LORA) + bias) / GLA_NORMALIZER
    b = [cumsum_rows(rows(g, j)) for j in subs]
    ref = [lax.stop_gradient(row_of(b[j], c // 2)) for j in subs]
    last = [row_of(b[j], c - 1) for j in subs]
    ql = [rows(q, j) * (GLA_DK ** -0.5) * jnp.exp(b[j] - ref[j]) for j in subs]
    kr = [rows(k, j) * jnp.exp(ref[j] - b[j]) for j in subs]
    kl = [rows(k, j) * jnp.exp(last[j] - b[j]) for j in subs]
    vj = [rows(v, j) for j in subs]
    e_ref, e_last = [jnp.exp(x) for x in ref], [jnp.exp(x) for x in last]
    att = [[mmx(hk(ql[j], h), hk(kr[j], h), "nt", P_GLA) * ltri for h in heads] for j in subs]
    o_in = [[mmx(att[j][h], hv(vj[j], h), "nn", P_GLA) for h in heads] for j in subs]
    kv = [[mmx(hv(vj[j], h), hk(kl[j], h), "tn", P_GLA) for h in heads] for j in subs]
    o = []
    for j in subs:
        o.append([o_in[j][h] + mmx(hk(ql[j], h), state[h] * hk(e_ref[j], h), "nt", P_GLA) for h in heads])
        state = [state[h] * hk(e_last[j], h) + kv[j][h] for h in heads]
    o = [[x * lax.rsqrt(jnp.mean(x * x, axis=-1, keepdims=True) + NORM_EPS) * norm_w for x in oj] for oj in o]
    return cat_rows(*[jnp.concatenate(oj, axis=1) for oj in o]), state


SOLVE_BLOCK = 128


def solve_unit_lower(ps, ws):
    n = ps[0].shape[0]
    heads = range(len(ps))
    if n > SOLVE_BLOCK:
        half = n // 2
        top = solve_unit_lower([p[:half, :half] for p in ps], [w[:half] for w in ws])
        rest = [ws[h][half:] + mmx(ps[h][half:, :half], top[h], "nn", P_RWKV) for h in heads]
        bottom = solve_unit_lower([p[half:, half:] for p in ps], rest)
        return [cat_rows(top[h], bottom[h]) for h in heads]
    u, p = ws, ps
    levels = max(1, (n - 1).bit_length())
    for it in range(levels):
        if it + 1 < levels:
            y = [mmx(p[h], jnp.concatenate([p[h], u[h]], axis=1), "nn", P_RWKV) for h in heads]
            u = [u[h] + y[h][:, n:] for h in heads]
            p = [y[h][:, :n] for h in heads]
        else:
            u = [u[h] + mmx(p[h], u[h], "nn", P_RWKV) for h in heads]
    return u


def rwkv_chunk(state, toks, params):
    S, pr, pk, pv, pxw, pxa = state
    r_, k_, v_, xw_, xa_ = toks
    mu_r, mu_k, mu_v, mu_xw, mu_xa, w0, w_up, a0, a_up, k_k, k_a, r_k, ln_w, ln_b = params
    c, n = xw_.shape[0], RWKV_N
    heads = range(RWKV_HEADS)
    hs = lambda x, h: x[:, h * n:(h + 1) * n]
    ltri = tril_ones(c)
    stri = tril_ones(c, strict=True)

    def lerp(x, prev, mu):
        return x + (shift_rows(x, prev) - x) * mu

    xw = jnp.tanh(lerp(xw_, pxw, mu_xw))
    xa = lerp(xa_, pxa, mu_xa)
    r = lerp(r_, pr, mu_r)
    k = lerp(k_, pk, mu_k)
    v = lerp(v_, pv, mu_v)
    w = -softplus(-(w0 + mmx(xw, w_up, "nn", P_LORA))) - 0.5
    lw = -jnp.exp(w)
    asig = sigmoid(a0 + mmx(xa, a_up, "nn", P_LORA))
    kk = k * k_k
    kk = kk / jnp.maximum(jnp.sqrt(head_sum(kk * kk)), 1e-12)
    k2 = k * (1.0 + (asig - 1.0) * k_a)
    b = kk * asig
    cum = cumsum_rows(lw)
    ref = lax.stop_gradient(row_of(cum, c // 2))
    last = row_of(cum, c - 1)
    at = -kk * jnp.exp(cum - lw - ref)
    rt = r * jnp.exp(cum - ref)
    e_out = jnp.exp(ref - cum)
    bt, kt = b * e_out, k2 * e_out
    e_tail = jnp.exp(last - cum)
    bl, kl = b * e_tail, k2 * e_tail
    e_ref, e_last = jnp.exp(ref), jnp.exp(last)
    g = [mmx(cat_rows(hs(at, h), hs(rt, h)), cat_rows(hs(bt, h), hs(kt, h), S[h] * hs(e_ref, h)), "nt", P_RWKV_G) for h in heads]
    aab = [x[:c, :c] * stri for x in g]
    aak = [x[:c, c:2 * c] * stri for x in g]
    arb = [x[c:, :c] * ltri for x in g]
    ark = [x[c:, c:2 * c] * ltri for x in g]
    av = [mmx(cat_rows(aak[h], ark[h]), hs(v, h), "nn", P_RWKV) for h in heads]
    u = solve_unit_lower(aab, [g[h][:c, 2 * c:] + av[h][:c] for h in heads])
    o = [g[h][c:, 2 * c:] + av[h][c:] + mmx(arb[h], u[h], "nn", P_RWKV) for h in heads]
    s1 = [S[h] * hs(e_last, h) + mmx(cat_rows(u[h], hs(v, h)), cat_rows(hs(bl, h), hs(kl, h)), "tn", P_RWKV) for h in heads]
    o = jnp.concatenate(o, axis=1)
    d = o - head_sum(o) * (1.0 / n)
    var = head_sum(d * d) * (1.0 / n)
    o = d * lax.rsqrt(var + RWKV_LN_EPS) * ln_w + ln_b + head_sum(r * k2 * r_k) * v
    new_state = (s1, row_of(r_, c - 1), row_of(k_, c - 1), row_of(v_, c - 1), row_of(xw_, c - 1), row_of(xa_, c - 1))
    return o, new_state


ROPE_HALF = 8


def _rot_half_raw(x):
    lane = lax.broadcasted_iota(jnp.int32, (x.shape[0], LANES), 1) & (SWA_HD - 1)
    out = []
    for i in range(x.shape[1] // LANES):
        g = x[:, i * LANES:(i + 1) * LANES]
        up, down = pltpu.roll(g, LANES - ROPE_HALF, 1), pltpu.roll(g, ROPE_HALF, 1)
        out.append(jnp.where(lane < ROPE_HALF, -up, jnp.where(lane < 2 * ROPE_HALF, down, 0.0)))
    return out[0] if len(out) == 1 else jnp.concatenate(out, axis=1)


@jax.custom_vjp
def rot_half(x):
    return _rot_half_raw(x)


rot_half.defvjp(lambda x: (_rot_half_raw(x), None), lambda _, g: (-_rot_half_raw(g),))


def rope(x, cos2, sin2):
    reps = x.shape[1] // LANES
    tile = lambda t: t if reps == 1 else jnp.concatenate([t] * reps, axis=1)
    return x * tile(cos2) + rot_half(x) * tile(sin2)


def swa_chunk(state, toks, params, first):
    kprev, vprev = state
    q_, k_, v_, cos, sin = toks
    bq, bk, bv, sinks = params
    c, ng = WINDOW, SWA_GROUP
    n_sub = cos.shape[0] // c
    units = [(j, g) for j in range(n_sub) for g in range(SWA_KV_HEADS)]
    rows = lambda x, j: x[j * c:(j + 1) * c]
    hs = lambda g: range(g * ng, (g + 1) * ng)
    head = lambda x, h: x[:, h * SWA_HD:(h + 1) * SWA_HD]
    qi, kj = _iota2(ng * c, 2 * c)
    qpos = qi & (c - 1)
    cur_ok = (kj >= c) & (qpos >= kj - c)
    prev_ok = (kj < c) & (kj > qpos)
    ok = [cur_ok | (prev_ok & jnp.logical_not(first))] + [cur_ok | prev_ok] * (n_sub - 1)
    q_all = rope(q_ + bq, cos, sin) * (SWA_HD ** -0.5)
    k_all = rope(k_ + bk, cos, sin)
    v_all = v_ + bv
    k = {(j, g): rows(head(k_all, g), j) for j, g in units}
    v = {(j, g): rows(head(v_all, g), j) for j, g in units}
    q = {(j, g): cat_rows(*[rows(head(q_all, h), j) for h in hs(g)]) for j, g in units}
    kp = lambda j, g: kprev[g] if j == 0 else k[(j - 1, g)]
    vp = lambda j, g: vprev[g] if j == 0 else v[(j - 1, g)]
    s = {(j, g): jnp.where(ok[j], mmx(q[(j, g)], cat_rows(kp(j, g), k[(j, g)]), "nt", P_SWA), NEG) for j, g in units}
    sink = [cat_rows(*[jnp.broadcast_to(sinks[h], (c, 1)) for h in hs(g)]) for g in range(SWA_KV_HEADS)]
    m = {(j, g): lax.stop_gradient(jnp.maximum(jnp.max(s[(j, g)], axis=-1, keepdims=True), sink[g])) for j, g in units}
    p = {u: jnp.exp(s[u] - m[u]) for u in units}
    ones = jnp.ones((2 * c, SWA_HD), F32)
    pv = {(j, g): mmx(p[(j, g)], cat_rows(vp(j, g), v[(j, g)]), "nn", P_SWA) for j, g in units}
    den = {u: mmx(p[u], ones, "nn", P_SWA) for u in units}
    o = {(j, g): pv[(j, g)] / (den[(j, g)] + jnp.exp(sink[g] - m[(j, g)])) for j, g in units}
    outs = [cat_rows(*[o[(j, g)][i * c:(i + 1) * c] for j in range(n_sub)]) for g in range(SWA_KV_HEADS) for i in range(ng)]
    last = n_sub - 1
    return outs, ([k[(last, g)] for g in range(SWA_KV_HEADS)], [v[(last, g)] for g in range(SWA_KV_HEADS)])


def _heads(ref, n, w, rows=slice(None)):
    return [ref[rows, h * w:(h + 1) * w] for h in range(n)]


def _put_heads(ref, vals, w, rows=slice(None), add=False):
    for h, val in enumerate(vals):
        if add:
            ref[rows, h * w:(h + 1) * w] += val
        else:
            ref[rows, h * w:(h + 1) * w] = val


def _col(block_w, name, table):
    off, w = table[name]
    assert off % block_w == 0 and w % block_w == 0
    return off // block_w


def _tok_spec(c, w, colblock, n=None):
    if n is None:
        return pl.BlockSpec((c, w), lambda i: (i, colblock))
    return pl.BlockSpec((c, w), lambda i: (n - 1 - i, colblock))


def _full_spec(shape):
    return pl.BlockSpec(shape, lambda i: (0,) * len(shape))


def _matmul(name, a, b, mode, tm, tn, out_dtype=F32):
    (m, kd) = (a.shape[1], a.shape[0]) if mode == "tn" else a.shape
    n = b.shape[0] if mode == "nt" else b.shape[1]
    assert m % tm == 0 and n % tn == 0
    a_spec = pl.BlockSpec((kd, tm), lambda j, i: (0, i)) if mode == "tn" else pl.BlockSpec((tm, kd), lambda j, i: (i, 0))
    b_spec = pl.BlockSpec((tn, kd), lambda j, i: (j, 0)) if mode == "nt" else pl.BlockSpec((kd, tn), lambda j, i: (0, j))

    def body(a_ref, b_ref, o_ref):
        o_ref[...] = lax.dot_general(a_ref[...].astype(BF16), b_ref[...].astype(BF16), DIMS[mode],
                                     preferred_element_type=F32).astype(out_dtype)

    return pl.pallas_call(
        body, name=name, grid=(n // tn, m // tm), in_specs=[a_spec, b_spec],
        out_specs=pl.BlockSpec((tm, tn), lambda j, i: (i, j)), out_shape=jax.ShapeDtypeStruct((m, n), out_dtype),
        compiler_params=_cparams(("arbitrary", "arbitrary")))(a, b)


TOK_TILE = 512


def _norm_fwd(name, x, w):
    t, d = x.shape
    tile = pl.BlockSpec((TOK_TILE, d), lambda i: (i, 0))

    def body(x_ref, w_ref, hn_ref):
        hn_ref[...] = rms(x_ref[...], w_ref[...]).astype(BF16)

    return pl.pallas_call(body, name=name, grid=(t // TOK_TILE,), in_specs=[tile, _full_spec((1, d))], out_specs=tile,
                          out_shape=jax.ShapeDtypeStruct((t, d), BF16), compiler_params=_cparams(("arbitrary",)))(x, w)


def _matmul_fused(name, a, b, mode, tiles, rows, outs, sums, epilogue, comm=(), kinds=()):
    m, kd = a.shape
    n = b.shape[1] if mode == "nn" else b.shape[0]
    tm = TOK_TILE
    steps = m // tm
    nt_, nr, no, ns, ncomm = len(tiles), len(rows), len(outs), len(sums), len(comm)

    def body(*refs):
        a_ref, b_ref = refs[:2]
        at = 2
        tile_refs, row_refs, comm_in = refs[at:at + nt_], refs[at + nt_:at + nt_ + nr], refs[at + nt_ + nr:at + nt_ + nr + ncomm]
        at += nt_ + nr + ncomm
        out_refs, sum_refs, comm_out = refs[at:at + no], refs[at + no:at + no + ns], refs[at + no + ns:at + no + ns + ncomm]
        sems = refs[at + no + ns + ncomm:]
        i = pl.program_id(0)

        @pl.when(i == 0)
        def _():
            if ncomm:
                _comm_start(*_comm_copies(comm_in, comm_out, kinds, *sems))
            for ref in sum_refs:
                ref[...] = jnp.zeros_like(ref)

        acc = lax.dot_general(a_ref[...].astype(BF16), b_ref[...].astype(BF16), DIMS[mode], preferred_element_type=F32)
        res = epilogue(acc, *[r[...] for r in tile_refs], *[r[...] for r in row_refs])
        for ref, val in zip(out_refs, res[:no]):
            ref[...] = val.astype(ref.dtype)
        for ref, val in zip(sum_refs, res[no:]):
            ref[...] += val

        if ncomm:
            @pl.when(i == steps - 1)
            def _():
                _comm_wait(*_comm_copies(comm_in, comm_out, kinds, *sems))

    in_specs = [pl.BlockSpec((tm, kd), lambda i: (i, 0)), _full_spec(b.shape)]
    in_specs += [pl.BlockSpec((tm, w), functools.partial(lambda i, cb: (i, cb), cb=cb)) for _, w, cb in tiles]
    in_specs += [_full_spec(r.shape) for r in rows] + [ANY] * ncomm
    out_specs = [pl.BlockSpec((tm, w), lambda i: (i, 0)) for w, _ in outs] + [_full_spec((1, w)) for w in sums] + [ANY] * ncomm
    out_shape = ([jax.ShapeDtypeStruct((m, w), dt) for w, dt in outs] + [jax.ShapeDtypeStruct((1, w), F32) for w in sums]
                 + _comm_out_shapes(comm, kinds))
    return pl.pallas_call(body, name=name, grid=(steps,), in_specs=in_specs, out_specs=out_specs, out_shape=out_shape,
                          scratch_shapes=_comm_scratch(ncomm) if ncomm else [],
                          compiler_params=_cparams(("arbitrary",)))(a, b, *[t[0] for t in tiles], *rows, *comm)


def _resid_norm(y, x, w):
    h = x + y
    return h, rms(h, w)


def _norm_back(dhn, h, dres, w):
    _, vjp = jax.vjp(rms, h, w)
    dh, dw = vjp(dhn)
    return dh + dres, dw


def _gate_back(dog, *o_and_gate):
    outs, g = o_and_gate[:-1], o_and_gate[-1]
    s = sigmoid(g)
    silu, dsilu = g * s, s * (1.0 + g * (1.0 - s))
    d_outs, c = [], 0
    for o in outs:
        w = o.shape[1]
        d_outs.append(dog[:, c:c + w] * silu[:, c:c + w])
        c += w
    o_all = outs[0] if len(outs) == 1 else jnp.concatenate(outs, axis=1)
    return (*d_outs, dog * o_all * dsilu)


def _loss_head(y1, h1, target, b_out, fw):
    def f(h2, w):
        err = rms(h2, w) - target
        return 0.5 * jnp.sum(jnp.mean(err * err, axis=-1, keepdims=True), axis=0, keepdims=True)

    loss, vjp = jax.vjp(f, h1 + y1 + b_out, fw)
    dh2, dfw = vjp(jnp.ones((1, 1), F32))
    return dh2, jnp.broadcast_to(loss, (1, LANES)), jnp.sum(dh2, axis=0, keepdims=True), dfw


def _gate_fwd(name, outs, proj):
    t = proj.shape[0]
    widths = [o.shape[1] for o in outs]
    n = len(outs)

    def body(*refs):
        o_refs, g_ref, og_ref = refs[:n], refs[n], refs[n + 1]
        c = 0
        for o_ref, w in zip(o_refs, widths):
            g = g_ref[:, c:c + w]
            og_ref[:, c:c + w] = (o_ref[...] * (g * sigmoid(g))).astype(BF16)
            c += w

    in_specs = [pl.BlockSpec((TOK_TILE, w), lambda i: (i, 0)) for w in widths] + [pl.BlockSpec((TOK_TILE, 1024), lambda i: (i, 0))]
    return pl.pallas_call(body, name=name, grid=(t // TOK_TILE,), in_specs=in_specs,
                          out_specs=pl.BlockSpec((TOK_TILE, 1024), lambda i: (i, 0)),
                          out_shape=jax.ShapeDtypeStruct((t, 1024), BF16), compiler_params=_cparams(("arbitrary",)))(*outs, proj)


def _gla_load(q_ref, k_ref, v_ref, gl_ref, up_ref, bias_ref, nw_ref):
    toks = (q_ref[...], k_ref[...], v_ref[...], gl_ref[...])
    params = (up_ref[...], bias_ref[...], nw_ref[...])
    return toks, params


def _gla_specs(c, n=None):
    toks = [_tok_spec(c, 256, _col(256, "gq", C0), n), _tok_spec(c, 256, _col(256, "gk", C0), n),
            _tok_spec(c, 512, _col(512, "gv", C0), n), _tok_spec(c, 128, _col(128, "glow", C0), n)]
    params = [_full_spec((128, 256)), _full_spec((1, 256)), _full_spec((1, 128))]
    return toks, params


def _gla_fwd(proj0, gk_up, gk_bias, norm_w):
    t = proj0.shape[0]
    c = GLA_STEP
    nc = t // c
    toks_s, params_s = _gla_specs(c)

    def body(q_ref, k_ref, v_ref, gl_ref, up_ref, bias_ref, nw_ref, o_ref, st_ref, s_scr):
        @pl.when(pl.program_id(0) == 0)
        def _():
            s_scr[...] = jnp.zeros_like(s_scr)

        st_ref[...] = s_scr[...]
        toks, params = _gla_load(q_ref, k_ref, v_ref, gl_ref, up_ref, bias_ref, nw_ref)
        state = [s_scr[h * GLA_DV:(h + 1) * GLA_DV, :] for h in range(GLA_HEADS)]
        o_ref[...], new = gla_chunk(state, toks, params)
        for h in range(GLA_HEADS):
            s_scr[h * GLA_DV:(h + 1) * GLA_DV, :] = new[h]

    return pl.pallas_call(
        body, name="gla_fwd", grid=(nc,), in_specs=toks_s + params_s,
        out_specs=(_tok_spec(c, 512, 0), pl.BlockSpec((512, GLA_DK), lambda i: (i, 0))),
        out_shape=(jax.ShapeDtypeStruct((t, 512), F32), jax.ShapeDtypeStruct((nc * 512, GLA_DK), F32)),
        scratch_shapes=[pltpu.VMEM((512, GLA_DK), F32)], compiler_params=_cparams(("arbitrary",)))(
            proj0, proj0, proj0, proj0, gk_up, gk_bias, norm_w)


def _gla_bwd(proj0, gk_up, gk_bias, norm_w, states, do):
    t = proj0.shape[0]
    c = GLA_STEP
    nc = t // c
    toks_s, params_s = _gla_specs(c, nc)

    def body(q_ref, k_ref, v_ref, gl_ref, up_ref, bias_ref, nw_ref, st_ref, do_ref,
             dq_ref, dk_ref, dv_ref, dgl_ref, dup_ref, dbias_ref, dnw_ref, ds_scr):
        @pl.when(pl.program_id(0) == 0)
        def _():
            ds_scr[...] = jnp.zeros_like(ds_scr)
            dup_ref[...] = jnp.zeros_like(dup_ref)
            dbias_ref[...] = jnp.zeros_like(dbias_ref)
            dnw_ref[...] = jnp.zeros_like(dnw_ref)

        toks, params = _gla_load(q_ref, k_ref, v_ref, gl_ref, up_ref, bias_ref, nw_ref)
        rows = lambda h: slice(h * GLA_DV, (h + 1) * GLA_DV)
        state = [st_ref[rows(h), :] for h in range(GLA_HEADS)]
        _, vjp = jax.vjp(gla_chunk, state, toks, params)
        dstate_in = [ds_scr[rows(h), :] for h in range(GLA_HEADS)]
        dstate, (dq_ref[...], dk_ref[...], dv_ref[...], dgl_ref[...]), (dup, dbias, dnw) = vjp((do_ref[...], dstate_in))
        dup_ref[...] += dup
        dbias_ref[...] += dbias
        dnw_ref[...] += dnw
        for h in range(GLA_HEADS):
            ds_scr[rows(h), :] = dstate[h]

    rev = lambda w: pl.BlockSpec((c, w), lambda i: (nc - 1 - i, 0))
    return pl.pallas_call(
        body, name="gla_bwd", grid=(nc,),
        in_specs=toks_s + params_s + [pl.BlockSpec((512, GLA_DK), lambda i: (nc - 1 - i, 0)), rev(512)],
        out_specs=(rev(256), rev(256), rev(512), rev(128), _full_spec((128, 256)), _full_spec((1, 256)), _full_spec((1, 128))),
        out_shape=(jax.ShapeDtypeStruct((t, 256), F32), jax.ShapeDtypeStruct((t, 256), F32), jax.ShapeDtypeStruct((t, 512), F32),
                   jax.ShapeDtypeStruct((t, 128), F32), jax.ShapeDtypeStruct((128, 256), F32), jax.ShapeDtypeStruct((1, 256), F32),
                   jax.ShapeDtypeStruct((1, 128), F32)),
        scratch_shapes=[pltpu.VMEM((512, GLA_DK), F32)], compiler_params=_cparams(("arbitrary",)))(
            proj0, proj0, proj0, proj0, gk_up, gk_bias, norm_w, states, do)


RWKV_PARAM_SHAPES = [(1, 512), (1, 512), (1, 512), (1, 128), (1, 128), (1, 512), (128, 512), (1, 512), (128, 512),
                     (1, 512), (1, 512), (1, 512), (1, 512), (1, 512)]
PREV_W = 1792
PREV_COLS = [slice(0, 512), slice(512, 1024), slice(1024, 1536), slice(1536, 1664), slice(1664, 1792)]


def _rwkv_load(r_ref, k_ref, v_ref, xw_ref, xa_ref, p_refs):
    toks = (r_ref[...], k_ref[...], v_ref[...], xw_ref[...], xa_ref[...])
    return toks, tuple(p[...] for p in p_refs)


def _rwkv_state(s_ref, prev_ref):
    n = RWKV_N
    S = [s_ref[h * n:(h + 1) * n, :] for h in range(RWKV_HEADS)]
    return (S,) + tuple(prev_ref[0:1, cols] for cols in PREV_COLS)


def _rwkv_put_state(s_ref, prev_ref, state):
    n = RWKV_N
    for h in range(RWKV_HEADS):
        s_ref[h * n:(h + 1) * n, :] = state[0][h]
    for cols, val in zip(PREV_COLS, state[1:]):
        prev_ref[0:1, cols] = val


def _rwkv_specs(c, n=None):
    toks = [_tok_spec(c, 512, _col(512, "r", C0), n), _tok_spec(c, 512, _col(512, "k", C0), n),
            _tok_spec(c, 512, _col(512, "v", C0), n), _tok_spec(c, 128, _col(128, "xw", C0), n),
            _tok_spec(c, 128, _col(128, "xa", C0), n)]
    return toks, [_full_spec(s) for s in RWKV_PARAM_SHAPES]


def _rwkv_fwd(proj0, params, comm, kinds):
    t = proj0.shape[0]
    c = RWKV_CHUNK
    nc = t // c
    toks_s, params_s = _rwkv_specs(c)
    npar, ncomm = len(params), len(comm)

    def body(*refs):
        tok_refs, p_refs = refs[:5], refs[5:5 + npar]
        comm_in = refs[5 + npar:5 + npar + ncomm]
        o_ref, st_ref, pst_ref = refs[5 + npar + ncomm:8 + npar + ncomm]
        comm_out = refs[8 + npar + ncomm:8 + npar + 2 * ncomm]
        s_scr, prev_scr = refs[8 + npar + 2 * ncomm:10 + npar + 2 * ncomm]
        sems = refs[10 + npar + 2 * ncomm:]
        i = pl.program_id(0)

        @pl.when(i == 0)
        def _():
            _comm_start(*_comm_copies(comm_in, comm_out, kinds, *sems))
            s_scr[...] = jnp.zeros_like(s_scr)
            prev_scr[...] = jnp.zeros_like(prev_scr)

        st_ref[...] = s_scr[...]
        pst_ref[...] = prev_scr[...]
        toks, prm = _rwkv_load(*tok_refs, p_refs)
        o_ref[...], new = rwkv_chunk(_rwkv_state(s_scr, prev_scr), toks, prm)
        _rwkv_put_state(s_scr, prev_scr, new)

        @pl.when(i == nc - 1)
        def _():
            _comm_wait(*_comm_copies(comm_in, comm_out, kinds, *sems))

    outs = pl.pallas_call(
        body, name="rwkv_fwd", grid=(nc,), in_specs=toks_s + params_s + [ANY] * ncomm,
        out_specs=[_tok_spec(c, 512, 0), pl.BlockSpec((512, RWKV_N), lambda i: (i, 0)), pl.BlockSpec((8, PREV_W), lambda i: (i, 0))]
        + [ANY] * ncomm,
        out_shape=[jax.ShapeDtypeStruct((t, 512), F32), jax.ShapeDtypeStruct((nc * 512, RWKV_N), F32),
                   jax.ShapeDtypeStruct((nc * 8, PREV_W), F32)] + _comm_out_shapes(comm, kinds),
        scratch_shapes=[pltpu.VMEM((512, RWKV_N), F32), pltpu.VMEM((8, PREV_W), F32)] + _comm_scratch(ncomm),
        compiler_params=_cparams(("arbitrary",)))(proj0, proj0, proj0, proj0, proj0, *params, *comm)
    return outs[0], outs[1], outs[2], outs[3:]


def _rwkv_bwd(proj0, params, states, prevs, do, comm, kinds):
    t = proj0.shape[0]
    c = RWKV_CHUNK
    nc = t // c
    toks_s, params_s = _rwkv_specs(c, nc)
    npar, ncomm = len(params), len(comm)

    def body(*refs):
        tok_refs, p_refs = refs[:5], refs[5:5 + npar]
        st_ref, pst_ref, do_ref = refs[5 + npar:8 + npar]
        comm_in = refs[8 + npar:8 + npar + ncomm]
        outs = refs[8 + npar + ncomm:]
        dtok_refs, dp_refs, comm_out = outs[:5], outs[5:5 + npar], outs[5 + npar:5 + npar + ncomm]
        ds_scr, dprev_scr = outs[5 + npar + ncomm:7 + npar + ncomm]
        sems = outs[7 + npar + ncomm:]
        i = pl.program_id(0)

        @pl.when(i == 0)
        def _():
            _comm_start(*_comm_copies(comm_in, comm_out, kinds, *sems))
            ds_scr[...] = jnp.zeros_like(ds_scr)
            dprev_scr[...] = jnp.zeros_like(dprev_scr)
            for dp in dp_refs:
                dp[...] = jnp.zeros_like(dp)

        toks, prm = _rwkv_load(*tok_refs, p_refs)
        _, vjp = jax.vjp(rwkv_chunk, _rwkv_state(st_ref, pst_ref), toks, prm)
        dstate, dtoks, dprm = vjp((do_ref[...], _rwkv_state(ds_scr, dprev_scr)))
        for ref, val in zip(dtok_refs, dtoks):
            ref[...] = val
        for ref, val in zip(dp_refs, dprm):
            ref[...] += val
        _rwkv_put_state(ds_scr, dprev_scr, dstate)

        @pl.when(i == nc - 1)
        def _():
            _comm_wait(*_comm_copies(comm_in, comm_out, kinds, *sems))

    rev = lambda w: pl.BlockSpec((c, w), lambda i: (nc - 1 - i, 0))
    outs = pl.pallas_call(
        body, name="rwkv_bwd", grid=(nc,),
        in_specs=toks_s + params_s + [pl.BlockSpec((512, RWKV_N), lambda i: (nc - 1 - i, 0)),
                                      pl.BlockSpec((8, PREV_W), lambda i: (nc - 1 - i, 0)), rev(512)] + [ANY] * ncomm,
        out_specs=[rev(512), rev(512), rev(512), rev(128), rev(128)] + params_s + [ANY] * ncomm,
        out_shape=[jax.ShapeDtypeStruct((t, w), F32) for w in (512, 512, 512, 128, 128)]
        + [jax.ShapeDtypeStruct(s, F32) for s in RWKV_PARAM_SHAPES] + _comm_out_shapes(comm, kinds),
        scratch_shapes=[pltpu.VMEM((512, RWKV_N), F32), pltpu.VMEM((8, PREV_W), F32)] + _comm_scratch(ncomm),
        compiler_params=_cparams(("arbitrary",)))(proj0, proj0, proj0, proj0, proj0, *params, states, prevs, do, *comm)
    return outs[:5], outs[5:5 + npar], outs[5 + npar:]


def _swa_load(q_ref, k_ref, v_ref, cos_ref, sin_ref, bq_ref, bk_ref, bv_ref, sk_ref):
    toks = (q_ref[...], k_ref[...], v_ref[...], cos_ref[...], sin_ref[...])
    params = (bq_ref[...], bk_ref[...], bv_ref[...], _heads(sk_ref, 16, 1))
    return toks, params


def _swa_specs(c, n=None):
    toks = [_tok_spec(c, 1024, _col(1024, "q", C1), n), _tok_spec(c, 256, _col(256, "k", C1), n),
            _tok_spec(c, 256, _col(256, "v", C1), n), _tok_spec(c, LANES, 0, n), _tok_spec(c, LANES, 0, n)]
    params = [_full_spec((1, 1024)), _full_spec((1, 256)), _full_spec((1, 256)), _full_spec((1, 16))]
    return toks, params


def _swa_fwd(proj1, cos, sin, bq, bk, bv, sinks):
    t = proj1.shape[0]
    c = SWA_STEP
    nb = t // c
    toks_s, params_s = _swa_specs(c)
    state_spec = pl.BlockSpec((WINDOW, 256), lambda i: (i, 0))

    def body(q_ref, k_ref, v_ref, cos_ref, sin_ref, bq_ref, bk_ref, bv_ref, sk_ref, o_ref, kst_ref, vst_ref, k_scr, v_scr):
        first = pl.program_id(0) == 0

        @pl.when(first)
        def _():
            k_scr[...] = jnp.zeros_like(k_scr)
            v_scr[...] = jnp.zeros_like(v_scr)

        kst_ref[...] = k_scr[...]
        vst_ref[...] = v_scr[...]
        toks, params = _swa_load(q_ref, k_ref, v_ref, cos_ref, sin_ref, bq_ref, bk_ref, bv_ref, sk_ref)
        outs, (kn, vn) = swa_chunk((_heads(k_scr, 4, SWA_HD), _heads(v_scr, 4, SWA_HD)), toks, params, first)
        _put_heads(o_ref, outs, SWA_HD)
        _put_heads(k_scr, kn, SWA_HD)
        _put_heads(v_scr, vn, SWA_HD)

    return pl.pallas_call(
        body, name="swa_fwd", grid=(nb,), in_specs=toks_s + params_s,
        out_specs=(_tok_spec(c, 1024, 0), state_spec, state_spec),
        out_shape=(jax.ShapeDtypeStruct((t, 1024), F32), jax.ShapeDtypeStruct((nb * WINDOW, 256), F32),
                   jax.ShapeDtypeStruct((nb * WINDOW, 256), F32)),
        scratch_shapes=[pltpu.VMEM((WINDOW, 256), F32), pltpu.VMEM((WINDOW, 256), F32)],
        compiler_params=_cparams(("arbitrary",)))(proj1, proj1, proj1, cos, sin, bq, bk, bv, sinks)


def _swa_bwd(proj1, cos, sin, bq, bk, bv, sinks, kst, vst, do):
    t = proj1.shape[0]
    c = SWA_STEP
    nb = t // c
    toks_s, params_s = _swa_specs(c, nb)
    state_spec = pl.BlockSpec((WINDOW, 256), lambda i: (nb - 1 - i, 0))

    def body(q_ref, k_ref, v_ref, cos_ref, sin_ref, bq_ref, bk_ref, bv_ref, sk_ref, kst_ref, vst_ref, do_ref,
             dq_ref, dk_ref, dv_ref, dbq_ref, dbk_ref, dbv_ref, dsk_ref, dk_scr, dv_scr):
        i = pl.program_id(0)

        @pl.when(i == 0)
        def _():
            dk_scr[...] = jnp.zeros_like(dk_scr)
            dv_scr[...] = jnp.zeros_like(dv_scr)
            for ref in (dbq_ref, dbk_ref, dbv_ref, dsk_ref):
                ref[...] = jnp.zeros_like(ref)

        first = i == nb - 1
        toks, params = _swa_load(q_ref, k_ref, v_ref, cos_ref, sin_ref, bq_ref, bk_ref, bv_ref, sk_ref)
        f = functools.partial(swa_chunk, first=first)
        _, vjp = jax.vjp(f, (_heads(kst_ref, 4, SWA_HD), _heads(vst_ref, 4, SWA_HD)), toks, params)
        dstate_in = (_heads(dk_scr, 4, SWA_HD), _heads(dv_scr, 4, SWA_HD))
        (dkp, dvp), (dq, dk, dv, _, _), (dbq, dbk, dbv, dsk) = vjp((_heads(do_ref, 16, SWA_HD), dstate_in))
        dq_ref[...], dk_ref[...], dv_ref[...] = dq, dk, dv
        dbq_ref[...] += dbq
        dbk_ref[...] += dbk
        dbv_ref[...] += dbv
        _put_heads(dsk_ref, dsk, 1, add=True)
        _put_heads(dk_scr, dkp, SWA_HD)
        _put_heads(dv_scr, dvp, SWA_HD)

    rev = lambda w: pl.BlockSpec((c, w), lambda i: (nb - 1 - i, 0))
    return pl.pallas_call(
        body, name="swa_bwd", grid=(nb,), in_specs=toks_s + params_s + [state_spec, state_spec, rev(1024)],
        out_specs=(rev(1024), rev(256), rev(256), _full_spec((1, 1024)), _full_spec((1, 256)), _full_spec((1, 256)), _full_spec((1, 16))),
        out_shape=(jax.ShapeDtypeStruct((t, 1024), F32), jax.ShapeDtypeStruct((t, 256), F32), jax.ShapeDtypeStruct((t, 256), F32),
                   jax.ShapeDtypeStruct((1, 1024), F32), jax.ShapeDtypeStruct((1, 256), F32), jax.ShapeDtypeStruct((1, 256), F32),
                   jax.ShapeDtypeStruct((1, 16), F32)),
        scratch_shapes=[pltpu.VMEM((WINDOW, 256), F32), pltpu.VMEM((WINDOW, 256), F32)],
        compiler_params=_cparams(("arbitrary",)))(proj1, proj1, proj1, cos, sin, bq, bk, bv, sinks, kst, vst, do)


MESH = pl.DeviceIdType.MESH
ANY = pl.BlockSpec(memory_space=pl.ANY)


def _my_place():
    return lax.axis_index("x"), lax.axis_index("y"), lax.axis_index("c")


def _all_gather(shards):
    n = len(shards)

    def body(*refs):
        in_refs, out_refs = refs[:n], refs[n:2 * n]
        send_sems, recv_sems, local_sems = refs[2 * n:]
        x, y, c = _my_place()
        me, sibling = (x, y, c), (x, y, 1 - c)
        chips = [(1 - x, y), (x, 1 - y), (1 - x, 1 - y)]

        def slot(out_ref, place):
            px, py, pc = place
            return out_ref.at[4 * px + 2 * py + pc]

        def copy(a, k, block, to, src=None):
            return pltpu.make_async_remote_copy(
                src_ref=slot(out_refs[a], block) if src is None else src, dst_ref=slot(out_refs[a], block),
                send_sem=send_sems.at[a, k], recv_sem=recv_sems.at[a, k], device_id=to, device_id_type=MESH)

        mine = [pltpu.make_async_copy(in_refs[a], slot(out_refs[a], me), local_sems.at[a]) for a in range(n)]
        for cp in mine:
            cp.start()
        first = []
        for a in range(n):
            first.append(copy(a, 0, me, sibling, src=in_refs[a]))
            first += [copy(a, 1 + j, me, (*chip, c), src=in_refs[a]) for j, chip in enumerate(chips)]
        for cp in first:
            cp.start()
        passed = []
        for j, chip in enumerate(chips):
            for a in range(n):
                copy(a, 1 + j, (*chip, c), me).wait_recv()
                fwd = copy(a, 4 + j, (*chip, c), sibling)
                fwd.start()
                passed.append(fwd)
        for a in range(n):
            copy(a, 0, sibling, me).wait_recv()
            for j, chip in enumerate(chips):
                copy(a, 4 + j, (*chip, 1 - c), me).wait_recv()
        for cp in first + passed:
            cp.wait_send()
        for cp in mine:
            cp.wait()

    return pl.pallas_call(
        body, name="all_gather_weights", in_specs=[ANY] * n, out_specs=[ANY] * n,
        out_shape=[jax.ShapeDtypeStruct((N_DEV,) + s.shape, s.dtype) for s in shards],
        scratch_shapes=_comm_scratch(n))(*shards)


def _comm_copies(in_refs, out_refs, kinds, send_sems, recv_sems, local_sems):
    x, y, c = _my_place()
    my_idx = 4 * x + 2 * y + c
    src = lambda a, idx: in_refs[a] if kinds[a] == "gather" else in_refs[a].at[idx]
    local = [pltpu.make_async_copy(src(a, my_idx), out_refs[a].at[my_idx], local_sems.at[a]) for a in range(len(kinds))]
    remote = []
    for rel in range(1, N_DEV):
        px, py, pc = x ^ ((rel >> 2) & 1), y ^ ((rel >> 1) & 1), c ^ (rel & 1)
        for a in range(len(kinds)):
            remote.append(pltpu.make_async_remote_copy(
                src_ref=src(a, 4 * px + 2 * py + pc), dst_ref=out_refs[a].at[my_idx], send_sem=send_sems.at[a, rel - 1],
                recv_sem=recv_sems.at[a, rel - 1], device_id=(px, py, pc), device_id_type=MESH))
    return local, remote


def _comm_start(local, remote):
    for cp in local + remote:
        cp.start()


def _comm_wait(local, remote):
    for cp in remote:
        cp.wait_recv()
    for cp in remote:
        cp.wait_send()
    for cp in local:
        cp.wait()


def _comm_out_shapes(arrays, kinds):
    return [jax.ShapeDtypeStruct(((N_DEV,) + a.shape) if k == "gather" else a.shape, a.dtype) for a, k in zip(arrays, kinds)]


def _comm_scratch(n):
    return [pltpu.SemaphoreType.DMA((n, N_DEV - 1)), pltpu.SemaphoreType.DMA((n, N_DEV - 1)), pltpu.SemaphoreType.DMA((n,))]


def _exchange(arrays, kinds):
    n = len(arrays)

    def body(*refs):
        copies = _comm_copies(refs[:n], refs[n:2 * n], kinds, *refs[2 * n:])
        _comm_start(*copies)
        _comm_wait(*copies)

    return pl.pallas_call(body, name="exchange_grads", in_specs=[ANY] * n, out_specs=[ANY] * n,
                          out_shape=_comm_out_shapes(arrays, kinds), scratch_shapes=_comm_scratch(n))(*arrays)


def _adam_math(w, g, m, v):
    m = ADAM_B1 * m + (1.0 - ADAM_B1) * g
    v = ADAM_B2 * v + (1.0 - ADAM_B2) * (g * g)
    m_hat = m / (1.0 - ADAM_B1 ** ADAM_STEP)
    v_hat = v / (1.0 - ADAM_B2 ** ADAM_STEP)
    delta = -ADAM_LR * (m_hat / (jnp.sqrt(v_hat) + ADAM_EPS) + ADAM_WD * w)
    return delta, m, v


def _adamw(name, w, gslots, m, v, tc):
    r, cc = w.shape
    assert cc % tc == 0
    tile = pl.BlockSpec((r, tc), lambda i: (0, i))

    def body(w_ref, g_ref, m_ref, v_ref, go_ref, d_ref, mo_ref, vo_ref):
        g = g_ref[0].astype(F32)
        for s in range(1, N_DEV):
            g = g + g_ref[s].astype(F32)
        d, mn, vn = _adam_math(w_ref[...], g, m_ref[...], v_ref[...])
        go_ref[...] = g
        d_ref[...] = d
        mo_ref[...] = mn
        vo_ref[...] = vn

    shp = jax.ShapeDtypeStruct((r, cc), F32)
    return pl.pallas_call(body, name=name, grid=(cc // tc,),
                          in_specs=[tile, pl.BlockSpec((N_DEV, r, tc), lambda i: (0, 0, i)), tile, tile],
                          out_specs=(tile,) * 4, out_shape=(shp,) * 4, compiler_params=_cparams(("arbitrary",)))(w, gslots, m, v)


PACK_TILE = 8 * LANES


def _packed_rows(shape, mode):
    r, w = shape
    return -(-r // 8) * 8 if mode == "rows" else -(-(r * w) // PACK_TILE) * 8


def _pack_small(arrays, modes, lead=False):
    out = []
    for a, mode in zip(arrays, modes):
        a = a.astype(F32) if lead else a.astype(F32)[None]
        if mode == "rows":
            out.append(jnp.pad(a, ((0, 0), (0, (-a.shape[1]) % 8), (0, LANES - a.shape[2]))))
        else:
            flat = a.reshape(a.shape[0], -1)
            out.append(jnp.pad(flat, ((0, 0), (0, (-flat.shape[1]) % PACK_TILE))).reshape(a.shape[0], -1, LANES))
    out = jnp.concatenate(out, axis=1)
    return out if lead else out[0]


def _take_small(packed, row0, shape, mode):
    r, w = shape
    lead = packed.ndim == 3
    if mode == "rows":
        return packed[:, row0:row0 + r, :w] if lead else packed[row0:row0 + r, :w]
    per_row = -(-w // LANES)
    if lead:
        return packed[:, row0:row0 + r * per_row].reshape(packed.shape[0], r, per_row * LANES)[:, :, :w]
    rows = []
    for i in range(r):
        pieces = [packed[row0 + i * per_row + j:row0 + i * per_row + j + 1, :] for j in range(per_row)]
        rows.append((pieces[0] if per_row == 1 else jnp.concatenate(pieces, axis=1))[:, :w])
    return rows[0] if r == 1 else jnp.concatenate(rows, axis=0)


def _adamw_small(slots, specs, ws, ms, vs, loss_row):
    n = len(specs)

    def body(*refs):
        slots_ref, w_refs, m_refs, v_refs = refs[0], refs[1:1 + n], refs[1 + n:1 + 2 * n], refs[1 + 2 * n:1 + 3 * n]
        out_refs, loss_ref = refs[1 + 3 * n:1 + 7 * n], refs[1 + 7 * n]
        gp = slots_ref[0]
        for s in range(1, N_DEV):
            gp = gp + slots_ref[s]
        read = lambda ref: ref[0] if len(ref.shape) == 3 else ref[...]
        for k, (shape, mode, row0) in enumerate(specs):
            g = _take_small(gp, row0, shape, mode)
            d, mn, vn = _adam_math(read(w_refs[k]), g, read(m_refs[k]), read(v_refs[k]))
            for ref, val in zip(out_refs[4 * k:4 * k + 4], (g, d, mn, vn)):
                if len(ref.shape) == 3:
                    ref[0] = val
                else:
                    ref[...] = val
        loss_ref[...] = gp[loss_row:loss_row + 1, :]

    vmem = pl.BlockSpec(memory_space=pltpu.VMEM)
    out_shape = [jax.ShapeDtypeStruct(w.shape, F32) for w in ws for _ in range(4)] + [jax.ShapeDtypeStruct((1, LANES), F32)]
    outs = pl.pallas_call(body, name="adamw_small", in_specs=[vmem] * (1 + 3 * n), out_specs=[vmem] * (4 * n + 1),
                          out_shape=out_shape)(slots, *ws, *ms, *vs)
    return [outs[4 * k:4 * k + 4] for k in range(n)], outs[4 * n]


def _rope_tables(t):
    half = ROPE_HALF
    inv_freq = ROPE_THETA ** (-jnp.arange(half, dtype=F32) / half)
    ang = jnp.arange(t, dtype=F32)[:, None] * inv_freq
    cos = jnp.concatenate([jnp.cos(ang), jnp.cos(ang), jnp.ones((t, SWA_HD - 2 * half), F32)], axis=1)
    sin = jnp.concatenate([jnp.sin(ang), jnp.sin(ang), jnp.zeros((t, SWA_HD - 2 * half), F32)], axis=1)
    reps = LANES // SWA_HD
    return jnp.concatenate([cos] * reps, axis=1), jnp.concatenate([sin] * reps, axis=1)


def _pad_to(a, rows=None, cols=None):
    r = 0 if rows is None else rows - a.shape[0]
    c = 0 if cols is None else cols - a.shape[1]
    return jnp.pad(a, ((0, r), (0, c)))


ORIG0 = dict(gq=(0, 256), gk=(256, 256), gv=(512, 512), glow=(1024, 16), r=(1040, 512), k=(1552, 512), v=(2064, 512),
             xw=(2576, 64), xa=(2640, 64), gate=(2704, 1024))
ORIG0_ORDER = ["gq", "gk", "gv", "glow", "r", "k", "v", "xw", "xa", "gate"]


def _w0t_to_padded(wt):
    rows, at = [], 0
    for name, (off, width) in sorted(C0.items(), key=lambda kv: kv[1][0]):
        assert off == at
        src, src_w = ORIG0[name]
        rows.append(_pad_to(wt[src:src + src_w], rows=width))
        at += width
    rows.append(jnp.zeros((N0P - at, wt.shape[1]), wt.dtype))
    return jnp.concatenate(rows, axis=0)


def _w0t_from_padded(wpt):
    return jnp.concatenate([wpt[C0[n][0]:C0[n][0] + ORIG0[n][1]] for n in ORIG0_ORDER], axis=0)


def _w1t_to_mine(wt):
    return jnp.concatenate([wt[1536:2560], wt[:1536]], axis=0)


def _w1t_from_mine(wt):
    return jnp.concatenate([wt[1024:2560], wt[:1024]], axis=0)


def kernel(x, norm_w, w_in0, gla_gk_up, gla_gk_bias, gla_norm_w, rwkv_mu, rwkv_w0, rwkv_w_up, rwkv_a0, rwkv_a_up, rwkv_k_k, rwkv_k_a, rwkv_r_k, rwkv_ln_w, rwkv_ln_b, w_out0, w_in1, b_in1, attn_sinks, w_out1, b_out1, final_norm_w, loss_target, m_norm_w, m_w_in0, m_gla_gk_up, m_gla_gk_bias, m_gla_norm_w, m_rwkv_mu, m_rwkv_w0, m_rwkv_w_up, m_rwkv_a0, m_rwkv_a_up, m_rwkv_k_k, m_rwkv_k_a, m_rwkv_r_k, m_rwkv_ln_w, m_rwkv_ln_b, m_w_out0, m_w_in1, m_b_in1, m_attn_sinks, m_w_out1, m_b_out1, m_final_norm_w, v_norm_w, v_w_in0, v_gla_gk_up, v_gla_gk_bias, v_gla_norm_w, v_rwkv_mu, v_rwkv_w0, v_rwkv_w_up, v_rwkv_a0, v_rwkv_a_up, v_rwkv_k_k, v_rwkv_k_a, v_rwkv_r_k, v_rwkv_ln_w, v_rwkv_ln_b, v_w_out0, v_w_in1, v_b_in1, v_attn_sinks, v_w_out1, v_b_out1, v_final_norm_w):
    weights = dict(norm_w=norm_w, w_in0=w_in0, gla_gk_up=gla_gk_up, gla_gk_bias=gla_gk_bias, gla_norm_w=gla_norm_w, rwkv_mu=rwkv_mu,
                   rwkv_w0=rwkv_w0, rwkv_w_up=rwkv_w_up, rwkv_a0=rwkv_a0, rwkv_a_up=rwkv_a_up, rwkv_k_k=rwkv_k_k, rwkv_k_a=rwkv_k_a,
                   rwkv_r_k=rwkv_r_k, rwkv_ln_w=rwkv_ln_w, rwkv_ln_b=rwkv_ln_b, w_out0=w_out0, w_in1=w_in1, b_in1=b_in1,
                   attn_sinks=attn_sinks, w_out1=w_out1, b_out1=b_out1, final_norm_w=final_norm_w)
    moms = dict(norm_w=m_norm_w, w_in0=m_w_in0, gla_gk_up=m_gla_gk_up, gla_gk_bias=m_gla_gk_bias, gla_norm_w=m_gla_norm_w,
                rwkv_mu=m_rwkv_mu, rwkv_w0=m_rwkv_w0, rwkv_w_up=m_rwkv_w_up, rwkv_a0=m_rwkv_a0, rwkv_a_up=m_rwkv_a_up,
                rwkv_k_k=m_rwkv_k_k, rwkv_k_a=m_rwkv_k_a, rwkv_r_k=m_rwkv_r_k, rwkv_ln_w=m_rwkv_ln_w, rwkv_ln_b=m_rwkv_ln_b,
                w_out0=m_w_out0, w_in1=m_w_in1, b_in1=m_b_in1, attn_sinks=m_attn_sinks, w_out1=m_w_out1, b_out1=m_b_out1,
                final_norm_w=m_final_norm_w)
    vars_ = dict(norm_w=v_norm_w, w_in0=v_w_in0, gla_gk_up=v_gla_gk_up, gla_gk_bias=v_gla_gk_bias, gla_norm_w=v_gla_norm_w,
                 rwkv_mu=v_rwkv_mu, rwkv_w0=v_rwkv_w0, rwkv_w_up=v_rwkv_w_up, rwkv_a0=v_rwkv_a0, rwkv_a_up=v_rwkv_a_up,
                 rwkv_k_k=v_rwkv_k_k, rwkv_k_a=v_rwkv_k_a, rwkv_r_k=v_rwkv_r_k, rwkv_ln_w=v_rwkv_ln_w, rwkv_ln_b=v_rwkv_ln_b,
                 w_out0=v_w_out0, w_in1=v_w_in1, b_in1=v_b_in1, attn_sinks=v_attn_sinks, w_out1=v_w_out1, b_out1=v_b_out1,
                 final_norm_w=v_final_norm_w)
    names = list(weights)
    big = ["w_in0", "w_out0", "w_in1", "w_out1"]
    small_sharded = ["gla_gk_up", "rwkv_w_up", "rwkv_a_up", "b_in1", "b_out1"]
    replicated = [n for n in names if n not in big and n not in small_sharded]

    xs = x[0]
    tgt = loss_target[0]
    t = xs.shape[0]

    def view(w):
        shape = tuple(w.shape[-2:]) if w.ndim >= 2 else (1, w.shape[0])
        return shape, ("rows" if shape[0] > 1 and shape[1] <= LANES else "flat")

    def layout(ns, row0=0):
        specs = []
        for n in ns:
            shape, mode = view(weights[n])
            specs.append((shape, mode, row0))
            row0 += _packed_rows(shape, mode)
        return specs, row0

    sh_specs, n_shard_rows = layout(small_sharded)
    rep_specs, loss_row = layout(replicated, n_shard_rows)
    sh_modes, rep_modes = [s[1] for s in sh_specs], [s[1] for s in rep_specs]

    small_shard_pack = _pack_small([weights[n].reshape(view(weights[n])[0]) for n in small_sharded], sh_modes)
    g_in0, g_small = _all_gather([w_in0[0].T.astype(BF16), small_shard_pack])
    w0t = _w0t_to_padded(g_in0.reshape(-1, D_MODEL))
    later_shards = [w_out0[0].astype(BF16), w_in1[0].T.astype(BF16), w_out1[0].astype(BF16)]
    gs = [_take_small(g_small, row0, shape, mode) for shape, mode, row0 in sh_specs]
    join_cols = lambda a: jnp.transpose(a, (1, 0, 2)).reshape(a.shape[1], -1)
    gk_up, w_up, a_up = join_cols(gs[0]), join_cols(gs[1]), join_cols(gs[2])
    b_in, b_out = gs[3].reshape(1, -1), gs[4].reshape(1, -1)

    gk_up_p = _pad_to(gk_up, rows=128)
    mu = rwkv_mu
    rwkv_params = [mu[:, 0:512], mu[:, 512:1024], mu[:, 1024:1536], _pad_to(mu[:, 1536:1600], cols=128), _pad_to(mu[:, 1600:1664], cols=128),
                   rwkv_w0, _pad_to(w_up, rows=128), rwkv_a0, _pad_to(a_up, rows=128), rwkv_k_k, rwkv_k_a, rwkv_r_k.reshape(1, 512),
                   rwkv_ln_w, rwkv_ln_b]
    bq, bk, bv = b_in[:, :1024], b_in[:, 1024:1280], b_in[:, 1280:1536]
    cos, sin = _rope_tables(t)
    nw0, nw1, fw = norm_w[0:1], norm_w[1:2], final_norm_w.reshape(1, D_MODEL)

    d = D_MODEL
    wide = lambda arr: (arr, d, 0)
    hn0 = _norm_fwd("norm0_fwd", xs, nw0)
    proj0 = _matmul("proj0", hn0, w0t, "nt", 1024, 1024)
    o_a, gla_states = _gla_fwd(proj0, gk_up_p, gla_gk_bias, gla_norm_w)
    o_b, rwkv_states, rwkv_prevs, (g_out0, g_in1, g_out1) = _rwkv_fwd(proj0, rwkv_params, later_shards, ["gather"] * 3)
    wo0 = g_out0.reshape(1024, D_MODEL)
    w1t = _w1t_to_mine(g_in1.reshape(-1, D_MODEL))
    wo1 = g_out1.reshape(1024, D_MODEL)
    og0 = _gate_fwd("gate0_fwd", [o_a, o_b], proj0)
    h1, hn1 = _matmul_fused("out0_norm1", og0, wo0, "nn", [wide(xs)], [nw1], [(d, F32), (d, BF16)], [], _resid_norm)
    proj1 = _matmul("proj1", hn1, w1t, "nt", 1024, 1280)
    o_c, kst, vst = _swa_fwd(proj1, cos, sin, bq, bk, bv, attn_sinks)
    og1 = _gate_fwd("gate1_fwd", [o_c], proj1)
    dh2, loss_part, d_b_out, d_fw = _matmul_fused("out1_loss", og1, wo1, "nn", [wide(h1), wide(tgt)], [b_out, fw],
                                                  [(d, F32)], [LANES, d, d], _loss_head)

    d_oc, d_gate1 = _matmul_fused("out1_dx_gate1", dh2, wo1, "nt", [wide(o_c), wide(proj1)], [], [(d, F32), (d, F32)], [], _gate_back)
    d_wo1 = _matmul("out1_dw", og1, dh2, "tn", 512, 512, BF16)
    dq, dk, dv, d_bq, d_bk, d_bv, d_sinks = _swa_bwd(proj1, cos, sin, bq, bk, bv, attn_sinks, kst, vst, d_oc)
    dproj1 = jnp.concatenate([d_gate1, dq, dk, dv], axis=1).astype(BF16)
    dh1, d_nw1 = _matmul_fused("proj1_dx_norm1", dproj1, w1t, "nn", [wide(h1), wide(dh2)], [nw1], [(d, F32)], [d], _norm_back)
    d_w1t = _matmul("proj1_dw", dproj1, hn1, "tn", 512, 1024, BF16)
    d_oa, d_ob, d_gate0 = _matmul_fused("out0_dx_gate0", dh1, wo0, "nt", [(o_a, 512, 0), (o_b, 512, 0), wide(proj0)], [],
                                        [(512, F32), (512, F32), (d, F32)], [], _gate_back)
    d_wo0 = _matmul("out0_dw", og0, dh1, "tn", 512, 512, BF16)
    dgq, dgk, dgv, dglow, d_gk_up, d_gk_bias, d_gla_nw = _gla_bwd(proj0, gk_up_p, gla_gk_bias, gla_norm_w, gla_states, d_oa)
    row_blocks = lambda a: a.astype(BF16).reshape(N_DEV, -1, D_MODEL)
    early = [row_blocks(_w1t_from_mine(d_w1t)), row_blocks(d_wo1), row_blocks(d_wo0)]
    (dr, dkk, dvv, dxw, dxa), d_rp, (r_in1, r_out1, r_out0) = _rwkv_bwd(
        proj0, rwkv_params, rwkv_states, rwkv_prevs, d_ob, early, ["scatter"] * 3)
    dproj0 = jnp.concatenate([d_gate0, dgv, dr, dkk, dvv, dgq, dgk, dglow, dxw, dxa, jnp.zeros((t, 128), F32)], axis=1).astype(BF16)
    d_w0 = _w0t_from_padded(_matmul("proj0_dw", dproj0, hn0, "tn", 512, 1024, BF16))
    grad_x, d_nw0, r_in0 = _matmul_fused("proj0_dx_norm0", dproj0, w0t, "nn", [wide(xs), wide(dh1)], [nw0], [(d, F32)], [d],
                                         _norm_back, [row_blocks(d_w0)], ["scatter"])

    contrib = dict(
        norm_w=jnp.concatenate([d_nw0, d_nw1], axis=0), gla_gk_bias=d_gk_bias, gla_norm_w=d_gla_nw,
        rwkv_mu=jnp.concatenate([d_rp[0], d_rp[1], d_rp[2], d_rp[3][:, :64], d_rp[4][:, :64]], axis=1),
        rwkv_w0=d_rp[5], rwkv_a0=d_rp[7], rwkv_k_k=d_rp[9], rwkv_k_a=d_rp[10], rwkv_r_k=d_rp[11].reshape(RWKV_HEADS, RWKV_N),
        rwkv_ln_w=d_rp[12], rwkv_ln_b=d_rp[13], attn_sinks=d_sinks, final_norm_w=d_fw)
    rep_pack = _pack_small([contrib[n] for n in replicated] + [loss_part[:, :1]], rep_modes + ["flat"])

    d_b_in = jnp.concatenate([d_bq, d_bk, d_bv], axis=1)
    full_small = [d_gk_up[:16], d_rp[6][:64], d_rp[8][:64], d_b_in, d_b_out]
    split_cols = lambda a: jnp.transpose(a.reshape(a.shape[0], N_DEV, -1), (1, 0, 2))
    small_parts = [split_cols(a) for a in full_small]
    small_pack = _pack_small(small_parts, sh_modes, lead=True)
    r_small, r_rep = _exchange([small_pack, rep_pack], ["scatter", "gather"])

    res = {}
    res["w_in0"] = tuple(a.T[None] for a in _adamw("adamw_w_in0", w_in0[0].T, r_in0, m_w_in0[0].T, v_w_in0[0].T, 256))
    res["w_out0"] = tuple(a[None] for a in _adamw("adamw_w_out0", w_out0[0], r_out0, m_w_out0[0], v_w_out0[0], 256))
    res["w_in1"] = tuple(a.T[None] for a in _adamw("adamw_w_in1", w_in1[0].T, r_in1, m_w_in1[0].T, v_w_in1[0].T, 256))
    res["w_out1"] = tuple(a[None] for a in _adamw("adamw_w_out1", w_out1[0], r_out1, m_w_out1[0], v_w_out1[0], 256))
    small_names = small_sharded + replicated
    slots = jnp.concatenate([r_small, r_rep], axis=1)
    as_2d = lambda a: a.reshape(1, -1) if a.ndim == 1 else a
    small_res, loss_row_out = _adamw_small(slots, sh_specs + rep_specs, [as_2d(weights[n]) for n in small_names],
                                           [as_2d(moms[n]) for n in small_names], [as_2d(vars_[n]) for n in small_names], loss_row)
    for n, vals in zip(small_names, small_res):
        res[n] = tuple(val.reshape(weights[n].shape) for val in vals)
    loss = loss_row_out[0, 0]
    return (loss, grad_x[None], *[res[n][0] for n in names], *[res[n][1] for n in names],
            *[res[n][2] for n in names], *[res[n][3] for n in names])
```

```python
import functools

import jax
import jax.numpy as jnp
from jax import lax
from jax.experimental import pallas as pl
from jax.experimental.pallas import tpu as pltpu

F32 = jnp.float32
BF16 = jnp.bfloat16
HI = lax.Precision.HIGHEST

D_MODEL = 1024
NORM_EPS = 1e-5
GLA_HEADS, GLA_DK, GLA_DV = 4, 64, 128
GLA_NORMALIZER = 16.0
GLA_CHUNK = 64
GLA_STEP = 256
RWKV_HEADS, RWKV_N = 8, 64
RWKV_LN_EPS = 64e-5
RWKV_CHUNK = 128
SWA_Q_HEADS, SWA_KV_HEADS, SWA_GROUP, SWA_HD = 16, 4, 4, 64
WINDOW = 128
SWA_STEP = 256
ROPE_THETA = 500000.0
NEG = -1e30
N_DEV = 8
LANES = 128

ADAM_LR, ADAM_B1, ADAM_B2, ADAM_EPS, ADAM_WD, ADAM_STEP = 0.001, 0.9, 0.999, 1e-08, 0.01, 10

N0P = 4096
C0 = dict(gate=(0, 1024), gv=(1024, 512), r=(1536, 512), k=(2048, 512), v=(2560, 512), gq=(3072, 256), gk=(3328, 256),
          glow=(3584, 128), xw=(3712, 128), xa=(3840, 128))
N1P = 2560
C1 = dict(gate=(0, 1024), q=(1024, 1024), k=(2048, 256), v=(2304, 256))

VMEM_LIMIT = 56 * 1024 * 1024

P_LORA = 1
P_GLA = 1
P_RWKV_G = 2
P_RWKV = 1
P_SWA = 1


def _cparams(sem=None):
    return pltpu.CompilerParams(dimension_semantics=sem, vmem_limit_bytes=VMEM_LIMIT)


DIMS = dict(nn=(((1,), (0,)), ((), ())), nt=(((1,), (1,)), ((), ())), tn=(((0,), (0,)), ((), ())))


def _split_bf16(a):
    hi = a.astype(BF16)
    return hi, (a - hi.astype(F32)).astype(BF16)


def _dot(a, b, mode, passes):
    dg = lambda p, q: lax.dot_general(p, q, DIMS[mode], preferred_element_type=F32)
    if passes == 1:
        return dg(a.astype(BF16), b.astype(BF16))
    if passes == 2:
        ah, (bh, bl) = a.astype(BF16), _split_bf16(b)
        return dg(ah, bh) + dg(ah, bl)
    if passes == 3:
        (ah, al), (bh, bl) = _split_bf16(a), _split_bf16(b)
        return dg(ah, bh) + dg(al, bh) + dg(ah, bl)
    return lax.dot_general(a, b, DIMS[mode], precision=HI, preferred_element_type=F32)


@functools.partial(jax.custom_vjp, nondiff_argnums=(2, 3))
def mmx(a, b, mode, passes):
    return _dot(a, b, mode, passes)


def _mmx_fwd(a, b, mode, passes):
    return _dot(a, b, mode, passes), (a, b)


def _mmx_bwd(mode, passes, res, g):
    a, b = res
    if mode == "nn":
        return _dot(g, b, "nt", passes), _dot(a, g, "tn", passes)
    if mode == "nt":
        return _dot(g, b, "nn", passes), _dot(g, a, "tn", passes)
    return _dot(b, g, "nt", passes), _dot(a, g, "nn", passes)


mmx.defvjp(_mmx_fwd, _mmx_bwd)


def _tri_dot(tri, x):
    t = tri.astype(BF16)
    x1 = x.astype(BF16)
    r1 = x - x1.astype(F32)
    x2 = r1.astype(BF16)
    x3 = (r1 - x2.astype(F32)).astype(BF16)
    dg = lambda q: jnp.dot(t, q, preferred_element_type=F32)
    return dg(x1) + dg(x2) + dg(x3)


@jax.custom_vjp
def cumsum_rows(x):
    return _tri_dot(tril_ones(x.shape[0]), x)


def _cumsum_fwd(x):
    return cumsum_rows(x), None


def _cumsum_bwd(_, g):
    i, j = _iota2(g.shape[0], g.shape[0])
    return (_tri_dot(jnp.where(i <= j, 1.0, 0.0).astype(F32), g),)


cumsum_rows.defvjp(_cumsum_fwd, _cumsum_bwd)


def _head_dot(x):
    i, j = _iota2(LANES, LANES)
    shift = RWKV_N.bit_length() - 1
    same = jnp.where(jnp.right_shift(i, shift) == jnp.right_shift(j, shift), 1.0, 0.0).astype(F32)
    return jnp.concatenate([_ones_right(x[:, g * LANES:(g + 1) * LANES], same) for g in range(x.shape[1] // LANES)], axis=1)


def _ones_right(x, ones):
    t = ones.astype(BF16)
    x1 = x.astype(BF16)
    r1 = x - x1.astype(F32)
    x2 = r1.astype(BF16)
    x3 = (r1 - x2.astype(F32)).astype(BF16)
    dg = lambda q: jnp.dot(q, t, preferred_element_type=F32)
    return dg(x1) + dg(x2) + dg(x3)


@jax.custom_vjp
def head_sum(x):
    return _head_dot(x)


def _head_sum_fwd(x):
    return head_sum(x), None


def _head_sum_bwd(_, g):
    return (_head_dot(g),)


head_sum.defvjp(_head_sum_fwd, _head_sum_bwd)


def cat_rows(*xs):
    return jnp.concatenate(xs, axis=0)


def _iota2(n, m):
    return lax.broadcasted_iota(jnp.int32, (n, m), 0), lax.broadcasted_iota(jnp.int32, (n, m), 1)


def tril_ones(c, strict=False):
    i, j = _iota2(c, c)
    return jnp.where((i > j) if strict else (i >= j), 1.0, 0.0).astype(F32)


def row_of(x, r):
    i = lax.broadcasted_iota(jnp.int32, x.shape, 0)
    return jnp.sum(jnp.where(i == r, x, 0.0), axis=0, keepdims=True)


@jax.custom_vjp
def shift_rows(x, prev):
    r = lax.broadcasted_iota(jnp.int32, x.shape, 0)
    return jnp.where(r == 0, prev, pltpu.roll(x, 1, 0))


def _shift_fwd(x, prev):
    return shift_rows(x, prev), None


def _shift_bwd(_, g):
    c = g.shape[0]
    r = lax.broadcasted_iota(jnp.int32, g.shape, 0)
    return jnp.where(r == c - 1, 0.0, pltpu.roll(g, c - 1, 0)), row_of(g, 0)


shift_rows.defvjp(_shift_fwd, _shift_bwd)


def log_sigmoid(x):
    return jnp.minimum(x, 0.0) - jnp.log(1.0 + jnp.exp(-jnp.abs(x)))


def softplus(x):
    return jnp.maximum(x, 0.0) + jnp.log(1.0 + jnp.exp(-jnp.abs(x)))


def sigmoid(x):
    return 1.0 / (1.0 + jnp.exp(-x))


def rms(x, w, eps=NORM_EPS):
    return x * lax.rsqrt(jnp.mean(x * x, axis=-1, keepdims=True) + eps) * w


def gla_chunk(state, toks, params):
    q, k, v, glow = toks
    gk_up, bias, norm_w = params
    c = GLA_CHUNK
    subs, heads = range(glow.shape[0] // c), range(GLA_HEADS)
    rows = lambda x, j: x[j * c:(j + 1) * c]
    hk = lambda x, h: x[:, h * GLA_DK:(h + 1) * GLA_DK]
    hv = lambda x, h: x[:, h * GLA_DV:(h + 1) * GLA_DV]
    ltri = tril_ones(c)
    g = log_sigmoid(mmx(glow, gk_up, "nn", P_LORA) + bias) / GLA_NORMALIZER
    b = [cumsum_rows(rows(g, j)) for j in subs]
    ref = [lax.stop_gradient(row_of(b[j], c // 2)) for j in subs]
    last = [row_of(b[j], c - 1) for j in subs]
    ql = [rows(q, j) * (GLA_DK ** -0.5) * jnp.exp(b[j] - ref[j]) for j in subs]
    kr = [rows(k, j) * jnp.exp(ref[j] - b[j]) for j in subs]
    kl = [rows(k, j) * jnp.exp(last[j] - b[j]) for j in subs]
    vj = [rows(v, j) for j in subs]
    e_ref, e_last = [jnp.exp(x) for x in ref], [jnp.exp(x) for x in last]
    att = [[mmx(hk(ql[j], h), hk(kr[j], h), "nt", P_GLA) * ltri for h in heads] for j in subs]
    o_in = [[mmx(att[j][h], hv(vj[j], h), "nn", P_GLA) for h in heads] for j in subs]
    kv = [[mmx(hv(vj[j], h), hk(kl[j], h), "tn", P_GLA) for h in heads] for j in subs]
    o = []
    for j in subs:
        o.append([o_in[j][h] + mmx(hk(ql[j], h), state[h] * hk(e_ref[j], h), "nt", P_GLA) for h in heads])
        state = [state[h] * hk(e_last[j], h) + kv[j][h] for h in heads]
    o = [[x * lax.rsqrt(jnp.mean(x * x, axis=-1, keepdims=True) + NORM_EPS) * norm_w for x in oj] for oj in o]
    return cat_rows(*[jnp.concatenate(oj, axis=1) for oj in o]), state


SOLVE_BLOCK = 128


def solve_unit_lower(ps, ws):
    n = ps[0].shape[0]
    heads = range(len(ps))
    if n > SOLVE_BLOCK:
        half = n // 2
        top = solve_unit_lower([p[:half, :half] for p in ps], [w[:half] for w in ws])
        rest = [ws[h][half:] + mmx(ps[h][half:, :half], top[h], "nn", P_RWKV) for h in heads]
        bottom = solve_unit_lower([p[half:, half:] for p in ps], rest)
        return [cat_rows(top[h], bottom[h]) for h in heads]
    u, p = ws, ps
    levels = max(1, (n - 1).bit_length())
    for it in range(levels):
        if it + 1 < levels:
            y = [mmx(p[h], jnp.concatenate([p[h], u[h]], axis=1), "nn", P_RWKV) for h in heads]
            u = [u[h] + y[h][:, n:] for h in heads]
            p = [y[h][:, :n] for h in heads]
        else:
            u = [u[h] + mmx(p[h], u[h], "nn", P_RWKV) for h in heads]
    return u


def rwkv_chunk(state, toks, params):
    S, pr, pk, pv, pxw, pxa = state
    r_, k_, v_, xw_, xa_ = toks
    mu_r, mu_k, mu_v, mu_xw, mu_xa, w0, w_up, a0, a_up, k_k, k_a, r_k, ln_w, ln_b = params
    c, n = xw_.shape[0], RWKV_N
    heads = range(RWKV_HEADS)
    hs = lambda x, h: x[:, h * n:(h + 1) * n]
    ltri = tril_ones(c)
    stri = tril_ones(c, strict=True)

    def lerp(x, prev, mu):
        return x + (shift_rows(x, prev) - x) * mu

    xw = jnp.tanh(lerp(xw_, pxw, mu_xw))
    xa = lerp(xa_, pxa, mu_xa)
    r = lerp(r_, pr, mu_r)
    k = lerp(k_, pk, mu_k)
    v = lerp(v_, pv, mu_v)
    w = -softplus(-(w0 + mmx(xw, w_up, "nn", P_LORA))) - 0.5
    lw = -jnp.exp(w)
    asig = sigmoid(a0 + mmx(xa, a_up, "nn", P_LORA))
    kk = k * k_k
    kk = kk * lax.rsqrt(jnp.maximum(head_sum(kk * kk), 1e-24))
    k2 = k * (1.0 + (asig - 1.0) * k_a)
    b = kk * asig
    cum = cumsum_rows(lw)
    ref = lax.stop_gradient(row_of(cum, c // 2))
    last = row_of(cum, c - 1)
    at = -kk * jnp.exp(cum - lw - ref)
    rt = r * jnp.exp(cum - ref)
    e_out = jnp.exp(ref - cum)
    bt, kt = b * e_out, k2 * e_out
    e_tail = jnp.exp(last - cum)
    bl, kl = b * e_tail, k2 * e_tail
    e_ref, e_last = jnp.exp(ref), jnp.exp(last)
    g = [mmx(cat_rows(hs(at, h), hs(rt, h)), cat_rows(hs(bt, h), hs(kt, h), S[h] * hs(e_ref, h)), "nt", P_RWKV_G) for h in heads]
    aab = [x[:c, :c] * stri for x in g]
    aak = [x[:c, c:2 * c] * stri for x in g]
    arb = [x[c:, :c] * ltri for x in g]
    ark = [x[c:, c:2 * c] * ltri for x in g]
    av = [mmx(cat_rows(aak[h], ark[h]), hs(v, h), "nn", P_RWKV) for h in heads]
    u = solve_unit_lower(aab, [g[h][:c, 2 * c:] + av[h][:c] for h in heads])
    o = [g[h][c:, 2 * c:] + av[h][c:] + mmx(arb[h], u[h], "nn", P_RWKV) for h in heads]
    s1 = [S[h] * hs(e_last, h) + mmx(cat_rows(u[h], hs(v, h)), cat_rows(hs(bl, h), hs(kl, h)), "tn", P_RWKV) for h in heads]
    o = jnp.concatenate(o, axis=1)
    d = o - head_sum(o) * (1.0 / n)
    var = head_sum(d * d) * (1.0 / n)
    o = d * lax.rsqrt(var + RWKV_LN_EPS) * ln_w + ln_b + head_sum(r * k2 * r_k) * v
    new_state = (s1, row_of(r_, c - 1), row_of(k_, c - 1), row_of(v_, c - 1), row_of(xw_, c - 1), row_of(xa_, c - 1))
    return o, new_state


ROPE_HALF = 8


def _rot_half_raw(x):
    lane = lax.broadcasted_iota(jnp.int32, (x.shape[0], LANES), 1) & (SWA_HD - 1)
    out = []
    for i in range(x.shape[1] // LANES):
        g = x[:, i * LANES:(i + 1) * LANES]
        up, down = pltpu.roll(g, LANES - ROPE_HALF, 1), pltpu.roll(g, ROPE_HALF, 1)
        out.append(jnp.where(lane < ROPE_HALF, -up, jnp.where(lane < 2 * ROPE_HALF, down, 0.0)))
    return out[0] if len(out) == 1 else jnp.concatenate(out, axis=1)


@jax.custom_vjp
def rot_half(x):
    return _rot_half_raw(x)


rot_half.defvjp(lambda x: (_rot_half_raw(x), None), lambda _, g: (-_rot_half_raw(g),))


def rope(x, cos2, sin2):
    reps = x.shape[1] // LANES
    tile = lambda t: t if reps == 1 else jnp.concatenate([t] * reps, axis=1)
    return x * tile(cos2) + rot_half(x) * tile(sin2)


def swa_chunk(state, toks, params, first):
    kprev, vprev = state
    q_, k_, v_, cos, sin = toks
    bq, bk, bv, sinks = params
    c, ng = WINDOW, SWA_GROUP
    n_sub = cos.shape[0] // c
    units = [(j, g) for j in range(n_sub) for g in range(SWA_KV_HEADS)]
    rows = lambda x, j: x[j * c:(j + 1) * c]
    hs = lambda g: range(g * ng, (g + 1) * ng)
    head = lambda x, h: x[:, h * SWA_HD:(h + 1) * SWA_HD]
    qi, kj = _iota2(ng * c, 2 * c)
    qpos = qi & (c - 1)
    cur_ok = (kj >= c) & (qpos >= kj - c)
    prev_ok = (kj < c) & (kj > qpos)
    ok = [cur_ok | (prev_ok & jnp.logical_not(first))] + [cur_ok | prev_ok] * (n_sub - 1)
    q_all = rope(q_ + bq, cos, sin) * (SWA_HD ** -0.5)
    k_all = rope(k_ + bk, cos, sin)
    v_all = v_ + bv
    k = {(j, g): rows(head(k_all, g), j) for j, g in units}
    v = {(j, g): rows(head(v_all, g), j) for j, g in units}
    q = {(j, g): cat_rows(*[rows(head(q_all, h), j) for h in hs(g)]) for j, g in units}
    kp = lambda j, g: kprev[g] if j == 0 else k[(j - 1, g)]
    vp = lambda j, g: vprev[g] if j == 0 else v[(j - 1, g)]
    s = {(j, g): jnp.where(ok[j], mmx(q[(j, g)], cat_rows(kp(j, g), k[(j, g)]), "nt", P_SWA), NEG) for j, g in units}
    sink = [cat_rows(*[jnp.broadcast_to(sinks[h], (c, 1)) for h in hs(g)]) for g in range(SWA_KV_HEADS)]
    m = {(j, g): lax.stop_gradient(jnp.maximum(jnp.max(s[(j, g)], axis=-1, keepdims=True), sink[g])) for j, g in units}
    p = {u: jnp.exp(s[u] - m[u]) for u in units}
    ones = jnp.ones((2 * c, SWA_HD), F32)
    pv = {(j, g): mmx(p[(j, g)], cat_rows(vp(j, g), v[(j, g)]), "nn", P_SWA) for j, g in units}
    den = {u: mmx(p[u], ones, "nn", P_SWA) for u in units}
    o = {(j, g): pv[(j, g)] / (den[(j, g)] + jnp.exp(sink[g] - m[(j, g)])) for j, g in units}
    outs = [cat_rows(*[o[(j, g)][i * c:(i + 1) * c] for j in range(n_sub)]) for g in range(SWA_KV_HEADS) for i in range(ng)]
    last = n_sub - 1
    return outs, ([k[(last, g)] for g in range(SWA_KV_HEADS)], [v[(last, g)] for g in range(SWA_KV_HEADS)])


def _heads(ref, n, w, rows=slice(None)):
    return [ref[rows, h * w:(h + 1) * w] for h in range(n)]


def _put_heads(ref, vals, w, rows=slice(None), add=False):
    for h, val in enumerate(vals):
        if add:
            ref[rows, h * w:(h + 1) * w] += val
        else:
            ref[rows, h * w:(h + 1) * w] = val


def _col(block_w, name, table):
    off, w = table[name]
    assert off % block_w == 0 and w % block_w == 0
    return off // block_w


def _tok_spec(c, w, colblock, n=None):
    if n is None:
        return pl.BlockSpec((c, w), lambda i: (i, colblock))
    return pl.BlockSpec((c, w), lambda i: (n - 1 - i, colblock))


def _full_spec(shape):
    return pl.BlockSpec(shape, lambda i: (0,) * len(shape))


def _matmul(name, a, b, mode, tm, tn, out_dtype=F32):
    (m, kd) = (a.shape[1], a.shape[0]) if mode == "tn" else a.shape
    n = b.shape[0] if mode == "nt" else b.shape[1]
    assert m % tm == 0 and n % tn == 0
    a_spec = pl.BlockSpec((kd, tm), lambda j, i: (0, i)) if mode == "tn" else pl.BlockSpec((tm, kd), lambda j, i: (i, 0))
    b_spec = pl.BlockSpec((tn, kd), lambda j, i: (j, 0)) if mode == "nt" else pl.BlockSpec((kd, tn), lambda j, i: (0, j))

    def body(a_ref, b_ref, o_ref):
        o_ref[...] = lax.dot_general(a_ref[...].astype(BF16), b_ref[...].astype(BF16), DIMS[mode],
                                     preferred_element_type=F32).astype(out_dtype)

    return pl.pallas_call(
        body, name=name, grid=(n // tn, m // tm), in_specs=[a_spec, b_spec],
        out_specs=pl.BlockSpec((tm, tn), lambda j, i: (i, j)), out_shape=jax.ShapeDtypeStruct((m, n), out_dtype),
        compiler_params=_cparams(("arbitrary", "arbitrary")))(a, b)


TOK_TILE = 512


def _norm_fwd(name, x, w):
    t, d = x.shape
    tile = pl.BlockSpec((TOK_TILE, d), lambda i: (i, 0))

    def body(x_ref, w_ref, hn_ref):
        hn_ref[...] = rms(x_ref[...], w_ref[...]).astype(BF16)

    return pl.pallas_call(body, name=name, grid=(t // TOK_TILE,), in_specs=[tile, _full_spec((1, d))], out_specs=tile,
                          out_shape=jax.ShapeDtypeStruct((t, d), BF16), compiler_params=_cparams(("arbitrary",)))(x, w)


def _matmul_fused(name, a, b, mode, tiles, rows, outs, sums, epilogue, comm=(), kinds=()):
    m, kd = a.shape
    n = b.shape[1] if mode == "nn" else b.shape[0]
    tm = TOK_TILE
    steps = m // tm
    nt_, nr, no, ns, ncomm = len(tiles), len(rows), len(outs), len(sums), len(comm)

    def body(*refs):
        a_ref, b_ref = refs[:2]
        at = 2
        tile_refs, row_refs, comm_in = refs[at:at + nt_], refs[at + nt_:at + nt_ + nr], refs[at + nt_ + nr:at + nt_ + nr + ncomm]
        at += nt_ + nr + ncomm
        out_refs, sum_refs, comm_out = refs[at:at + no], refs[at + no:at + no + ns], refs[at + no + ns:at + no + ns + ncomm]
        sems = refs[at + no + ns + ncomm:]
        i = pl.program_id(0)

        @pl.when(i == 0)
        def _():
            if ncomm:
                _comm_start(*_comm_copies(comm_in, comm_out, kinds, *sems))
            for ref in sum_refs:
                ref[...] = jnp.zeros_like(ref)

        acc = lax.dot_general(a_ref[...].astype(BF16), b_ref[...].astype(BF16), DIMS[mode], preferred_element_type=F32)
        res = epilogue(acc, *[r[...] for r in tile_refs], *[r[...] for r in row_refs])
        for ref, val in zip(out_refs, res[:no]):
            ref[...] = val.astype(ref.dtype)
        for ref, val in zip(sum_refs, res[no:]):
            ref[...] += val

        if ncomm:
            @pl.when(i == steps - 1)
            def _():
                _comm_wait(*_comm_copies(comm_in, comm_out, kinds, *sems))

    in_specs = [pl.BlockSpec((tm, kd), lambda i: (i, 0)), _full_spec(b.shape)]
    in_specs += [pl.BlockSpec((tm, w), functools.partial(lambda i, cb: (i, cb), cb=cb)) for _, w, cb in tiles]
    in_specs += [_full_spec(r.shape) for r in rows] + [ANY] * ncomm
    out_specs = [pl.BlockSpec((tm, w), lambda i: (i, 0)) for w, _ in outs] + [_full_spec((1, w)) for w in sums] + [ANY] * ncomm
    out_shape = ([jax.ShapeDtypeStruct((m, w), dt) for w, dt in outs] + [jax.ShapeDtypeStruct((1, w), F32) for w in sums]
                 + _comm_out_shapes(comm, kinds))
    return pl.pallas_call(body, name=name, grid=(steps,), in_specs=in_specs, out_specs=out_specs, out_shape=out_shape,
                          scratch_shapes=_comm_scratch(ncomm) if ncomm else [],
                          compiler_params=_cparams(("arbitrary",)))(a, b, *[t[0] for t in tiles], *rows, *comm)


def _resid_norm(y, x, w):
    h = x + y
    return h, rms(h, w)


def _norm_back(dhn, h, dres, w):
    _, vjp = jax.vjp(rms, h, w)
    dh, dw = vjp(dhn)
    return dh + dres, dw


def _gate_back(dog, *o_and_gate):
    outs, g = o_and_gate[:-1], o_and_gate[-1]
    s = sigmoid(g)
    silu, dsilu = g * s, s * (1.0 + g * (1.0 - s))
    d_outs, c = [], 0
    for o in outs:
        w = o.shape[1]
        d_outs.append(dog[:, c:c + w] * silu[:, c:c + w])
        c += w
    o_all = outs[0] if len(outs) == 1 else jnp.concatenate(outs, axis=1)
    return (*d_outs, dog * o_all * dsilu)


def _loss_head(y1, h1, target, b_out, fw):
    def f(h2, w):
        err = rms(h2, w) - target
        return 0.5 * jnp.sum(jnp.mean(err * err, axis=-1, keepdims=True), axis=0, keepdims=True)

    loss, vjp = jax.vjp(f, h1 + y1 + b_out, fw)
    dh2, dfw = vjp(jnp.ones((1, 1), F32))
    return dh2, jnp.broadcast_to(loss, (1, LANES)), jnp.sum(dh2, axis=0, keepdims=True), dfw


def _gate_fwd(name, outs, proj):
    t = proj.shape[0]
    widths = [o.shape[1] for o in outs]
    n = len(outs)

    def body(*refs):
        o_refs, g_ref, og_ref = refs[:n], refs[n], refs[n + 1]
        c = 0
        for o_ref, w in zip(o_refs, widths):
            g = g_ref[:, c:c + w]
            og_ref[:, c:c + w] = (o_ref[...] * (g * sigmoid(g))).astype(BF16)
            c += w

    in_specs = [pl.BlockSpec((TOK_TILE, w), lambda i: (i, 0)) for w in widths] + [pl.BlockSpec((TOK_TILE, 1024), lambda i: (i, 0))]
    return pl.pallas_call(body, name=name, grid=(t // TOK_TILE,), in_specs=in_specs,
                          out_specs=pl.BlockSpec((TOK_TILE, 1024), lambda i: (i, 0)),
                          out_shape=jax.ShapeDtypeStruct((t, 1024), BF16), compiler_params=_cparams(("arbitrary",)))(*outs, proj)


def _gla_load(q_ref, k_ref, v_ref, gl_ref, up_ref, bias_ref, nw_ref):
    toks = (q_ref[...], k_ref[...], v_ref[...], gl_ref[...])
    params = (up_ref[...], bias_ref[...], nw_ref[...])
    return toks, params


def _gla_specs(c, n=None):
    toks = [_tok_spec(c, 256, _col(256, "gq", C0), n), _tok_spec(c, 256, _col(256, "gk", C0), n),
            _tok_spec(c, 512, _col(512, "gv", C0), n), _tok_spec(c, 128, _col(128, "glow", C0), n)]
    params = [_full_spec((128, 256)), _full_spec((1, 256)), _full_spec((1, 128))]
    return toks, params


def _gla_fwd(proj0, gk_up, gk_bias, norm_w):
    t = proj0.shape[0]
    c = GLA_STEP
    nc = t // c
    toks_s, params_s = _gla_specs(c)

    def body(q_ref, k_ref, v_ref, gl_ref, up_ref, bias_ref, nw_ref, o_ref, st_ref, s_scr):
        @pl.when(pl.program_id(0) == 0)
        def _():
            s_scr[...] = jnp.zeros_like(s_scr)

        st_ref[...] = s_scr[...]
        toks, params = _gla_load(q_ref, k_ref, v_ref, gl_ref, up_ref, bias_ref, nw_ref)
        state = [s_scr[h * GLA_DV:(h + 1) * GLA_DV, :] for h in range(GLA_HEADS)]
        o_ref[...], new = gla_chunk(state, toks, params)
        for h in range(GLA_HEADS):
            s_scr[h * GLA_DV:(h + 1) * GLA_DV, :] = new[h]

    return pl.pallas_call(
        body, name="gla_fwd", grid=(nc,), in_specs=toks_s + params_s,
        out_specs=(_tok_spec(c, 512, 0), pl.BlockSpec((512, GLA_DK), lambda i: (i, 0))),
        out_shape=(jax.ShapeDtypeStruct((t, 512), F32), jax.ShapeDtypeStruct((nc * 512, GLA_DK), F32)),
        scratch_shapes=[pltpu.VMEM((512, GLA_DK), F32)], compiler_params=_cparams(("arbitrary",)))(
            proj0, proj0, proj0, proj0, gk_up, gk_bias, norm_w)


def _gla_bwd(proj0, gk_up, gk_bias, norm_w, states, do):
    t = proj0.shape[0]
    c = GLA_STEP
    nc = t // c
    toks_s, params_s = _gla_specs(c, nc)

    def body(q_ref, k_ref, v_ref, gl_ref, up_ref, bias_ref, nw_ref, st_ref, do_ref,
             dq_ref, dk_ref, dv_ref, dgl_ref, dup_ref, dbias_ref, dnw_ref, ds_scr):
        @pl.when(pl.program_id(0) == 0)
        def _():
            ds_scr[...] = jnp.zeros_like(ds_scr)
            dup_ref[...] = jnp.zeros_like(dup_ref)
            dbias_ref[...] = jnp.zeros_like(dbias_ref)
            dnw_ref[...] = jnp.zeros_like(dnw_ref)

        toks, params = _gla_load(q_ref, k_ref, v_ref, gl_ref, up_ref, bias_ref, nw_ref)
        rows = lambda h: slice(h * GLA_DV, (h + 1) * GLA_DV)
        state = [st_ref[rows(h), :] for h in range(GLA_HEADS)]
        _, vjp = jax.vjp(gla_chunk, state, toks, params)
        dstate_in = [ds_scr[rows(h), :] for h in range(GLA_HEADS)]
        dstate, dtoks, (dup, dbias, dnw) = vjp((do_ref[...], dstate_in))
        for ref, val in zip((dq_ref, dk_ref, dv_ref, dgl_ref), dtoks):
            ref[...] = val.astype(ref.dtype)
        dup_ref[...] += dup
        dbias_ref[...] += dbias
        dnw_ref[...] += dnw
        for h in range(GLA_HEADS):
            ds_scr[rows(h), :] = dstate[h]

    rev = lambda w: pl.BlockSpec((c, w), lambda i: (nc - 1 - i, 0))
    return pl.pallas_call(
        body, name="gla_bwd", grid=(nc,),
        in_specs=toks_s + params_s + [pl.BlockSpec((512, GLA_DK), lambda i: (nc - 1 - i, 0)), rev(512)],
        out_specs=(rev(256), rev(256), rev(512), rev(128), _full_spec((128, 256)), _full_spec((1, 256)), _full_spec((1, 128))),
        out_shape=(jax.ShapeDtypeStruct((t, 256), BF16), jax.ShapeDtypeStruct((t, 256), BF16), jax.ShapeDtypeStruct((t, 512), BF16),
                   jax.ShapeDtypeStruct((t, 128), BF16), jax.ShapeDtypeStruct((128, 256), F32), jax.ShapeDtypeStruct((1, 256), F32),
                   jax.ShapeDtypeStruct((1, 128), F32)),
        scratch_shapes=[pltpu.VMEM((512, GLA_DK), F32)], compiler_params=_cparams(("arbitrary",)))(
            proj0, proj0, proj0, proj0, gk_up, gk_bias, norm_w, states, do)


RWKV_PARAM_SHAPES = [(1, 512), (1, 512), (1, 512), (1, 128), (1, 128), (1, 512), (128, 512), (1, 512), (128, 512),
                     (1, 512), (1, 512), (1, 512), (1, 512), (1, 512)]
PREV_W = 1792
PREV_COLS = [slice(0, 512), slice(512, 1024), slice(1024, 1536), slice(1536, 1664), slice(1664, 1792)]


def _rwkv_load(r_ref, k_ref, v_ref, xw_ref, xa_ref, p_refs):
    toks = (r_ref[...], k_ref[...], v_ref[...], xw_ref[...], xa_ref[...])
    return toks, tuple(p[...] for p in p_refs)


def _rwkv_state(s_ref, prev_ref):
    n = RWKV_N
    S = [s_ref[h * n:(h + 1) * n, :] for h in range(RWKV_HEADS)]
    return (S,) + tuple(prev_ref[0:1, cols] for cols in PREV_COLS)


def _rwkv_put_state(s_ref, prev_ref, state):
    n = RWKV_N
    for h in range(RWKV_HEADS):
        s_ref[h * n:(h + 1) * n, :] = state[0][h]
    for cols, val in zip(PREV_COLS, state[1:]):
        prev_ref[0:1, cols] = val


def _rwkv_specs(c, n=None):
    toks = [_tok_spec(c, 512, _col(512, "r", C0), n), _tok_spec(c, 512, _col(512, "k", C0), n),
            _tok_spec(c, 512, _col(512, "v", C0), n), _tok_spec(c, 128, _col(128, "xw", C0), n),
            _tok_spec(c, 128, _col(128, "xa", C0), n)]
    return toks, [_full_spec(s) for s in RWKV_PARAM_SHAPES]


def _rwkv_fwd(proj0, params, comm, kinds):
    t = proj0.shape[0]
    c = RWKV_CHUNK
    nc = t // c
    toks_s, params_s = _rwkv_specs(c)
    npar, ncomm = len(params), len(comm)

    def body(*refs):
        tok_refs, p_refs = refs[:5], refs[5:5 + npar]
        comm_in = refs[5 + npar:5 + npar + ncomm]
        o_ref, st_ref, pst_ref = refs[5 + npar + ncomm:8 + npar + ncomm]
        comm_out = refs[8 + npar + ncomm:8 + npar + 2 * ncomm]
        s_scr, prev_scr = refs[8 + npar + 2 * ncomm:10 + npar + 2 * ncomm]
        sems = refs[10 + npar + 2 * ncomm:]
        i = pl.program_id(0)

        @pl.when(i == 0)
        def _():
            _comm_start(*_comm_copies(comm_in, comm_out, kinds, *sems))
            s_scr[...] = jnp.zeros_like(s_scr)
            prev_scr[...] = jnp.zeros_like(prev_scr)

        st_ref[...] = s_scr[...]
        pst_ref[...] = prev_scr[...]
        toks, prm = _rwkv_load(*tok_refs, p_refs)
        o_ref[...], new = rwkv_chunk(_rwkv_state(s_scr, prev_scr), toks, prm)
        _rwkv_put_state(s_scr, prev_scr, new)

        @pl.when(i == nc - 1)
        def _():
            _comm_wait(*_comm_copies(comm_in, comm_out, kinds, *sems))

    outs = pl.pallas_call(
        body, name="rwkv_fwd", grid=(nc,), in_specs=toks_s + params_s + [ANY] * ncomm,
        out_specs=[_tok_spec(c, 512, 0), pl.BlockSpec((512, RWKV_N), lambda i: (i, 0)), pl.BlockSpec((8, PREV_W), lambda i: (i, 0))]
        + [ANY] * ncomm,
        out_shape=[jax.ShapeDtypeStruct((t, 512), F32), jax.ShapeDtypeStruct((nc * 512, RWKV_N), F32),
                   jax.ShapeDtypeStruct((nc * 8, PREV_W), F32)] + _comm_out_shapes(comm, kinds),
        scratch_shapes=[pltpu.VMEM((512, RWKV_N), F32), pltpu.VMEM((8, PREV_W), F32)] + _comm_scratch(ncomm),
        compiler_params=_cparams(("arbitrary",)))(proj0, proj0, proj0, proj0, proj0, *params, *comm)
    return outs[0], outs[1], outs[2], outs[3:]


def _rwkv_bwd(proj0, params, states, prevs, do, comm, kinds):
    t = proj0.shape[0]
    c = RWKV_CHUNK
    nc = t // c
    toks_s, params_s = _rwkv_specs(c, nc)
    npar, ncomm = len(params), len(comm)

    def body(*refs):
        tok_refs, p_refs = refs[:5], refs[5:5 + npar]
        st_ref, pst_ref, do_ref = refs[5 + npar:8 + npar]
        comm_in = refs[8 + npar:8 + npar + ncomm]
        outs = refs[8 + npar + ncomm:]
        dtok_refs, dp_refs, comm_out = outs[:5], outs[5:5 + npar], outs[5 + npar:5 + npar + ncomm]
        ds_scr, dprev_scr = outs[5 + npar + ncomm:7 + npar + ncomm]
        sems = outs[7 + npar + ncomm:]
        i = pl.program_id(0)

        @pl.when(i == 0)
        def _():
            _comm_start(*_comm_copies(comm_in, comm_out, kinds, *sems))
            ds_scr[...] = jnp.zeros_like(ds_scr)
            dprev_scr[...] = jnp.zeros_like(dprev_scr)
            for dp in dp_refs:
                dp[...] = jnp.zeros_like(dp)

        toks, prm = _rwkv_load(*tok_refs, p_refs)
        _, vjp = jax.vjp(rwkv_chunk, _rwkv_state(st_ref, pst_ref), toks, prm)
        dstate, dtoks, dprm = vjp((do_ref[...], _rwkv_state(ds_scr, dprev_scr)))
        for ref, val in zip(dtok_refs, dtoks):
            ref[...] = val.astype(ref.dtype)
        for ref, val in zip(dp_refs, dprm):
            ref[...] += val
        _rwkv_put_state(ds_scr, dprev_scr, dstate)

        @pl.when(i == nc - 1)
        def _():
            _comm_wait(*_comm_copies(comm_in, comm_out, kinds, *sems))

    rev = lambda w: pl.BlockSpec((c, w), lambda i: (nc - 1 - i, 0))
    outs = pl.pallas_call(
        body, name="rwkv_bwd", grid=(nc,),
        in_specs=toks_s + params_s + [pl.BlockSpec((512, RWKV_N), lambda i: (nc - 1 - i, 0)),
                                      pl.BlockSpec((8, PREV_W), lambda i: (nc - 1 - i, 0)), rev(512)] + [ANY] * ncomm,
        out_specs=[rev(512), rev(512), rev(512), rev(128), rev(128)] + params_s + [ANY] * ncomm,
        out_shape=[jax.ShapeDtypeStruct((t, w), BF16) for w in (512, 512, 512, 128, 128)]
        + [jax.ShapeDtypeStruct(s, F32) for s in RWKV_PARAM_SHAPES] + _comm_out_shapes(comm, kinds),
        scratch_shapes=[pltpu.VMEM((512, RWKV_N), F32), pltpu.VMEM((8, PREV_W), F32)] + _comm_scratch(ncomm),
        compiler_params=_cparams(("arbitrary",)))(proj0, proj0, proj0, proj0, proj0, *params, states, prevs, do, *comm)
    return outs[:5], outs[5:5 + npar], outs[5 + npar:]


def _swa_load(q_ref, k_ref, v_ref, cos_ref, sin_ref, bq_ref, bk_ref, bv_ref, sk_ref):
    toks = (q_ref[...], k_ref[...], v_ref[...], cos_ref[...], sin_ref[...])
    params = (bq_ref[...], bk_ref[...], bv_ref[...], _heads(sk_ref, 16, 1))
    return toks, params


def _swa_specs(c, n=None):
    toks = [_tok_spec(c, 1024, _col(1024, "q", C1), n), _tok_spec(c, 256, _col(256, "k", C1), n),
            _tok_spec(c, 256, _col(256, "v", C1), n), _tok_spec(c, LANES, 0, n), _tok_spec(c, LANES, 0, n)]
    params = [_full_spec((1, 1024)), _full_spec((1, 256)), _full_spec((1, 256)), _full_spec((1, 16))]
    return toks, params


def _swa_fwd(proj1, cos, sin, bq, bk, bv, sinks):
    t = proj1.shape[0]
    c = SWA_STEP
    nb = t // c
    toks_s, params_s = _swa_specs(c)
    state_spec = pl.BlockSpec((WINDOW, 256), lambda i: (i, 0))

    def body(q_ref, k_ref, v_ref, cos_ref, sin_ref, bq_ref, bk_ref, bv_ref, sk_ref, o_ref, kst_ref, vst_ref, k_scr, v_scr):
        first = pl.program_id(0) == 0

        @pl.when(first)
        def _():
            k_scr[...] = jnp.zeros_like(k_scr)
            v_scr[...] = jnp.zeros_like(v_scr)

        kst_ref[...] = k_scr[...]
        vst_ref[...] = v_scr[...]
        toks, params = _swa_load(q_ref, k_ref, v_ref, cos_ref, sin_ref, bq_ref, bk_ref, bv_ref, sk_ref)
        outs, (kn, vn) = swa_chunk((_heads(k_scr, 4, SWA_HD), _heads(v_scr, 4, SWA_HD)), toks, params, first)
        _put_heads(o_ref, outs, SWA_HD)
        _put_heads(k_scr, kn, SWA_HD)
        _put_heads(v_scr, vn, SWA_HD)

    return pl.pallas_call(
        body, name="swa_fwd", grid=(nb,), in_specs=toks_s + params_s,
        out_specs=(_tok_spec(c, 1024, 0), state_spec, state_spec),
        out_shape=(jax.ShapeDtypeStruct((t, 1024), F32), jax.ShapeDtypeStruct((nb * WINDOW, 256), F32),
                   jax.ShapeDtypeStruct((nb * WINDOW, 256), F32)),
        scratch_shapes=[pltpu.VMEM((WINDOW, 256), F32), pltpu.VMEM((WINDOW, 256), F32)],
        compiler_params=_cparams(("arbitrary",)))(proj1, proj1, proj1, cos, sin, bq, bk, bv, sinks)


def _swa_bwd(proj1, cos, sin, bq, bk, bv, sinks, kst, vst, do):
    t = proj1.shape[0]
    c = SWA_STEP
    nb = t // c
    toks_s, params_s = _swa_specs(c, nb)
    state_spec = pl.BlockSpec((WINDOW, 256), lambda i: (nb - 1 - i, 0))

    def body(q_ref, k_ref, v_ref, cos_ref, sin_ref, bq_ref, bk_ref, bv_ref, sk_ref, kst_ref, vst_ref, do_ref,
             dq_ref, dk_ref, dv_ref, dbq_ref, dbk_ref, dbv_ref, dsk_ref, dk_scr, dv_scr):
        i = pl.program_id(0)

        @pl.when(i == 0)
        def _():
            dk_scr[...] = jnp.zeros_like(dk_scr)
            dv_scr[...] = jnp.zeros_like(dv_scr)
            for ref in (dbq_ref, dbk_ref, dbv_ref, dsk_ref):
                ref[...] = jnp.zeros_like(ref)

        first = i == nb - 1
        toks, params = _swa_load(q_ref, k_ref, v_ref, cos_ref, sin_ref, bq_ref, bk_ref, bv_ref, sk_ref)
        f = functools.partial(swa_chunk, first=first)
        _, vjp = jax.vjp(f, (_heads(kst_ref, 4, SWA_HD), _heads(vst_ref, 4, SWA_HD)), toks, params)
        dstate_in = (_heads(dk_scr, 4, SWA_HD), _heads(dv_scr, 4, SWA_HD))
        (dkp, dvp), (dq, dk, dv, _, _), (dbq, dbk, dbv, dsk) = vjp((_heads(do_ref, 16, SWA_HD), dstate_in))
        dq_ref[...], dk_ref[...], dv_ref[...] = dq.astype(BF16), dk.astype(BF16), dv.astype(BF16)
        dbq_ref[...] += dbq
        dbk_ref[...] += dbk
        dbv_ref[...] += dbv
        _put_heads(dsk_ref, dsk, 1, add=True)
        _put_heads(dk_scr, dkp, SWA_HD)
        _put_heads(dv_scr, dvp, SWA_HD)

    rev = lambda w: pl.BlockSpec((c, w), lambda i: (nb - 1 - i, 0))
    return pl.pallas_call(
        body, name="swa_bwd", grid=(nb,), in_specs=toks_s + params_s + [state_spec, state_spec, rev(1024)],
        out_specs=(rev(1024), rev(256), rev(256), _full_spec((1, 1024)), _full_spec((1, 256)), _full_spec((1, 256)), _full_spec((1, 16))),
        out_shape=(jax.ShapeDtypeStruct((t, 1024), BF16), jax.ShapeDtypeStruct((t, 256), BF16), jax.ShapeDtypeStruct((t, 256), BF16),
                   jax.ShapeDtypeStruct((1, 1024), F32), jax.ShapeDtypeStruct((1, 256), F32), jax.ShapeDtypeStruct((1, 256), F32),
                   jax.ShapeDtypeStruct((1, 16), F32)),
        scratch_shapes=[pltpu.VMEM((WINDOW, 256), F32), pltpu.VMEM((WINDOW, 256), F32)],
        compiler_params=_cparams(("arbitrary",)))(proj1, proj1, proj1, cos, sin, bq, bk, bv, sinks, kst, vst, do)


MESH = pl.DeviceIdType.MESH
ANY = pl.BlockSpec(memory_space=pl.ANY)


def _my_place():
    return lax.axis_index("x"), lax.axis_index("y"), lax.axis_index("c")


def _all_gather(shards):
    n = len(shards)

    def body(*refs):
        in_refs, out_refs = refs[:n], refs[n:2 * n]
        send_sems, recv_sems, local_sems = refs[2 * n:]
        x, y, c = _my_place()
        me, sibling = (x, y, c), (x, y, 1 - c)
        chips = [(1 - x, y), (x, 1 - y), (1 - x, 1 - y)]

        def slot(out_ref, place):
            px, py, pc = place
            return out_ref.at[4 * px + 2 * py + pc]

        def copy(a, k, block, to, src=None):
            return pltpu.make_async_remote_copy(
                src_ref=slot(out_refs[a], block) if src is None else src, dst_ref=slot(out_refs[a], block),
                send_sem=send_sems.at[a, k], recv_sem=recv_sems.at[a, k], device_id=to, device_id_type=MESH)

        mine = [pltpu.make_async_copy(in_refs[a], slot(out_refs[a], me), local_sems.at[a]) for a in range(n)]
        for cp in mine:
            cp.start()
        first = []
        for a in range(n):
            first.append(copy(a, 0, me, sibling, src=in_refs[a]))
            first += [copy(a, 1 + j, me, (*chip, c), src=in_refs[a]) for j, chip in enumerate(chips)]
        for cp in first:
            cp.start()
        passed = []
        for j, chip in enumerate(chips):
            for a in range(n):
                copy(a, 1 + j, (*chip, c), me).wait_recv()
                fwd = copy(a, 4 + j, (*chip, c), sibling)
                fwd.start()
                passed.append(fwd)
        for a in range(n):
            copy(a, 0, sibling, me).wait_recv()
            for j, chip in enumerate(chips):
                copy(a, 4 + j, (*chip, 1 - c), me).wait_recv()
        for cp in first + passed:
            cp.wait_send()
        for cp in mine:
            cp.wait()

    return pl.pallas_call(
        body, name="all_gather_weights", in_specs=[ANY] * n, out_specs=[ANY] * n,
        out_shape=[jax.ShapeDtypeStruct((N_DEV,) + s.shape, s.dtype) for s in shards],
        scratch_shapes=_comm_scratch(n))(*shards)


def _comm_copies(in_refs, out_refs, kinds, send_sems, recv_sems, local_sems):
    x, y, c = _my_place()
    my_idx = 4 * x + 2 * y + c
    src = lambda a, idx: in_refs[a] if kinds[a] == "gather" else in_refs[a].at[idx]
    local = [pltpu.make_async_copy(src(a, my_idx), out_refs[a].at[my_idx], local_sems.at[a]) for a in range(len(kinds))]
    remote = []
    for rel in range(1, N_DEV):
        px, py, pc = x ^ ((rel >> 2) & 1), y ^ ((rel >> 1) & 1), c ^ (rel & 1)
        for a in range(len(kinds)):
            remote.append(pltpu.make_async_remote_copy(
                src_ref=src(a, 4 * px + 2 * py + pc), dst_ref=out_refs[a].at[my_idx], send_sem=send_sems.at[a, rel - 1],
                recv_sem=recv_sems.at[a, rel - 1], device_id=(px, py, pc), device_id_type=MESH))
    return local, remote


def _comm_start(local, remote):
    for cp in local + remote:
        cp.start()


def _comm_wait(local, remote):
    for cp in remote:
        cp.wait_recv()
    for cp in remote:
        cp.wait_send()
    for cp in local:
        cp.wait()


def _comm_out_shapes(arrays, kinds):
    return [jax.ShapeDtypeStruct(((N_DEV,) + a.shape) if k == "gather" else a.shape, a.dtype) for a, k in zip(arrays, kinds)]


def _comm_scratch(n):
    return [pltpu.SemaphoreType.DMA((n, N_DEV - 1)), pltpu.SemaphoreType.DMA((n, N_DEV - 1)), pltpu.SemaphoreType.DMA((n,))]


def _exchange(arrays, kinds):
    n = len(arrays)

    def body(*refs):
        copies = _comm_copies(refs[:n], refs[n:2 * n], kinds, *refs[2 * n:])
        _comm_start(*copies)
        _comm_wait(*copies)

    return pl.pallas_call(body, name="exchange_grads", in_specs=[ANY] * n, out_specs=[ANY] * n,
                          out_shape=_comm_out_shapes(arrays, kinds), scratch_shapes=_comm_scratch(n))(*arrays)


def _adam_math(w, g, m, v):
    m = ADAM_B1 * m + (1.0 - ADAM_B1) * g
    v = ADAM_B2 * v + (1.0 - ADAM_B2) * (g * g)
    m_hat = m / (1.0 - ADAM_B1 ** ADAM_STEP)
    v_hat = v / (1.0 - ADAM_B2 ** ADAM_STEP)
    delta = -ADAM_LR * (m_hat / (jnp.sqrt(v_hat) + ADAM_EPS) + ADAM_WD * w)
    return delta, m, v


def _adamw(name, w, gslots, m, v, tc):
    r, cc = w.shape
    assert cc % tc == 0
    tile = pl.BlockSpec((r, tc), lambda i: (0, i))

    def body(w_ref, g_ref, m_ref, v_ref, go_ref, d_ref, mo_ref, vo_ref):
        g = g_ref[0].astype(F32)
        for s in range(1, N_DEV):
            g = g + g_ref[s].astype(F32)
        d, mn, vn = _adam_math(w_ref[...], g, m_ref[...], v_ref[...])
        go_ref[...] = g
        d_ref[...] = d
        mo_ref[...] = mn
        vo_ref[...] = vn

    shp = jax.ShapeDtypeStruct((r, cc), F32)
    return pl.pallas_call(body, name=name, grid=(cc // tc,),
                          in_specs=[tile, pl.BlockSpec((N_DEV, r, tc), lambda i: (0, 0, i)), tile, tile],
                          out_specs=(tile,) * 4, out_shape=(shp,) * 4, compiler_params=_cparams(("arbitrary",)))(w, gslots, m, v)


PACK_TILE = 8 * LANES


def _packed_rows(shape, mode):
    r, w = shape
    return -(-r // 8) * 8 if mode == "rows" else -(-(r * w) // PACK_TILE) * 8


def _pack_small(arrays, modes, lead=False):
    out = []
    for a, mode in zip(arrays, modes):
        a = a.astype(F32) if lead else a.astype(F32)[None]
        if mode == "rows":
            out.append(jnp.pad(a, ((0, 0), (0, (-a.shape[1]) % 8), (0, LANES - a.shape[2]))))
        else:
            flat = a.reshape(a.shape[0], -1)
            out.append(jnp.pad(flat, ((0, 0), (0, (-flat.shape[1]) % PACK_TILE))).reshape(a.shape[0], -1, LANES))
    out = jnp.concatenate(out, axis=1)
    return out if lead else out[0]


def _take_small(packed, row0, shape, mode):
    r, w = shape
    lead = packed.ndim == 3
    if mode == "rows":
        return packed[:, row0:row0 + r, :w] if lead else packed[row0:row0 + r, :w]
    per_row = -(-w // LANES)
    if lead:
        return packed[:, row0:row0 + r * per_row].reshape(packed.shape[0], r, per_row * LANES)[:, :, :w]
    rows = []
    for i in range(r):
        pieces = [packed[row0 + i * per_row + j:row0 + i * per_row + j + 1, :] for j in range(per_row)]
        rows.append((pieces[0] if per_row == 1 else jnp.concatenate(pieces, axis=1))[:, :w])
    return rows[0] if r == 1 else jnp.concatenate(rows, axis=0)


def _adamw_small(slots, specs, ws, ms, vs, loss_row):
    n = len(specs)

    def body(*refs):
        slots_ref, w_refs, m_refs, v_refs = refs[0], refs[1:1 + n], refs[1 + n:1 + 2 * n], refs[1 + 2 * n:1 + 3 * n]
        out_refs, loss_ref = refs[1 + 3 * n:1 + 7 * n], refs[1 + 7 * n]
        gp = slots_ref[0]
        for s in range(1, N_DEV):
            gp = gp + slots_ref[s]
        read = lambda ref: ref[0] if len(ref.shape) == 3 else ref[...]
        for k, (shape, mode, row0) in enumerate(specs):
            g = _take_small(gp, row0, shape, mode)
            d, mn, vn = _adam_math(read(w_refs[k]), g, read(m_refs[k]), read(v_refs[k]))
            for ref, val in zip(out_refs[4 * k:4 * k + 4], (g, d, mn, vn)):
                if len(ref.shape) == 3:
                    ref[0] = val
                else:
                    ref[...] = val
        loss_ref[...] = gp[loss_row:loss_row + 1, :]

    vmem = pl.BlockSpec(memory_space=pltpu.VMEM)
    out_shape = [jax.ShapeDtypeStruct(w.shape, F32) for w in ws for _ in range(4)] + [jax.ShapeDtypeStruct((1, LANES), F32)]
    outs = pl.pallas_call(body, name="adamw_small", in_specs=[vmem] * (1 + 3 * n), out_specs=[vmem] * (4 * n + 1),
                          out_shape=out_shape)(slots, *ws, *ms, *vs)
    return [outs[4 * k:4 * k + 4] for k in range(n)], outs[4 * n]


def _rope_tables(t):
    dim = jnp.arange(LANES) % SWA_HD
    inv_freq = ROPE_THETA ** (-(dim % ROPE_HALF).astype(F32) / ROPE_HALF)
    ang = jnp.arange(t, dtype=F32)[:, None] * jnp.where(dim < 2 * ROPE_HALF, inv_freq, 0.0)[None, :]
    return jnp.cos(ang), jnp.sin(ang)


def _pad_to(a, rows=None, cols=None):
    r = 0 if rows is None else rows - a.shape[0]
    c = 0 if cols is None else cols - a.shape[1]
    return jnp.pad(a, ((0, r), (0, c)))


ORIG0 = dict(gq=(0, 256), gk=(256, 256), gv=(512, 512), glow=(1024, 16), r=(1040, 512), k=(1552, 512), v=(2064, 512),
             xw=(2576, 64), xa=(2640, 64), gate=(2704, 1024))
ORIG0_ORDER = ["gq", "gk", "gv", "glow", "r", "k", "v", "xw", "xa", "gate"]


def _w0t_to_padded(wt):
    rows, at = [], 0
    for name, (off, width) in sorted(C0.items(), key=lambda kv: kv[1][0]):
        assert off == at
        src, src_w = ORIG0[name]
        rows.append(_pad_to(wt[src:src + src_w], rows=width))
        at += width
    rows.append(jnp.zeros((N0P - at, wt.shape[1]), wt.dtype))
    return jnp.concatenate(rows, axis=0)


def _w0t_from_padded(wpt):
    return jnp.concatenate([wpt[C0[n][0]:C0[n][0] + ORIG0[n][1]] for n in ORIG0_ORDER], axis=0)


def _w1t_to_mine(wt):
    return jnp.concatenate([wt[1536:2560], wt[:1536]], axis=0)


def _w1t_from_mine(wt):
    return jnp.concatenate([wt[1024:2560], wt[:1024]], axis=0)


def kernel(x, norm_w, w_in0, gla_gk_up, gla_gk_bias, gla_norm_w, rwkv_mu, rwkv_w0, rwkv_w_up, rwkv_a0, rwkv_a_up, rwkv_k_k, rwkv_k_a, rwkv_r_k, rwkv_ln_w, rwkv_ln_b, w_out0, w_in1, b_in1, attn_sinks, w_out1, b_out1, final_norm_w, loss_target, m_norm_w, m_w_in0, m_gla_gk_up, m_gla_gk_bias, m_gla_norm_w, m_rwkv_mu, m_rwkv_w0, m_rwkv_w_up, m_rwkv_a0, m_rwkv_a_up, m_rwkv_k_k, m_rwkv_k_a, m_rwkv_r_k, m_rwkv_ln_w, m_rwkv_ln_b, m_w_out0, m_w_in1, m_b_in1, m_attn_sinks, m_w_out1, m_b_out1, m_final_norm_w, v_norm_w, v_w_in0, v_gla_gk_up, v_gla_gk_bias, v_gla_norm_w, v_rwkv_mu, v_rwkv_w0, v_rwkv_w_up, v_rwkv_a0, v_rwkv_a_up, v_rwkv_k_k, v_rwkv_k_a, v_rwkv_r_k, v_rwkv_ln_w, v_rwkv_ln_b, v_w_out0, v_w_in1, v_b_in1, v_attn_sinks, v_w_out1, v_b_out1, v_final_norm_w):
    weights = dict(norm_w=norm_w, w_in0=w_in0, gla_gk_up=gla_gk_up, gla_gk_bias=gla_gk_bias, gla_norm_w=gla_norm_w, rwkv_mu=rwkv_mu,
                   rwkv_w0=rwkv_w0, rwkv_w_up=rwkv_w_up, rwkv_a0=rwkv_a0, rwkv_a_up=rwkv_a_up, rwkv_k_k=rwkv_k_k, rwkv_k_a=rwkv_k_a,
                   rwkv_r_k=rwkv_r_k, rwkv_ln_w=rwkv_ln_w, rwkv_ln_b=rwkv_ln_b, w_out0=w_out0, w_in1=w_in1, b_in1=b_in1,
                   attn_sinks=attn_sinks, w_out1=w_out1, b_out1=b_out1, final_norm_w=final_norm_w)
    moms = dict(norm_w=m_norm_w, w_in0=m_w_in0, gla_gk_up=m_gla_gk_up, gla_gk_bias=m_gla_gk_bias, gla_norm_w=m_gla_norm_w,
                rwkv_mu=m_rwkv_mu, rwkv_w0=m_rwkv_w0, rwkv_w_up=m_rwkv_w_up, rwkv_a0=m_rwkv_a0, rwkv_a_up=m_rwkv_a_up,
                rwkv_k_k=m_rwkv_k_k, rwkv_k_a=m_rwkv_k_a, rwkv_r_k=m_rwkv_r_k, rwkv_ln_w=m_rwkv_ln_w, rwkv_ln_b=m_rwkv_ln_b,
                w_out0=m_w_out0, w_in1=m_w_in1, b_in1=m_b_in1, attn_sinks=m_attn_sinks, w_out1=m_w_out1, b_out1=m_b_out1,
                final_norm_w=m_final_norm_w)
    vars_ = dict(norm_w=v_norm_w, w_in0=v_w_in0, gla_gk_up=v_gla_gk_up, gla_gk_bias=v_gla_gk_bias, gla_norm_w=v_gla_norm_w,
                 rwkv_mu=v_rwkv_mu, rwkv_w0=v_rwkv_w0, rwkv_w_up=v_rwkv_w_up, rwkv_a0=v_rwkv_a0, rwkv_a_up=v_rwkv_a_up,
                 rwkv_k_k=v_rwkv_k_k, rwkv_k_a=v_rwkv_k_a, rwkv_r_k=v_rwkv_r_k, rwkv_ln_w=v_rwkv_ln_w, rwkv_ln_b=v_rwkv_ln_b,
                 w_out0=v_w_out0, w_in1=v_w_in1, b_in1=v_b_in1, attn_sinks=v_attn_sinks, w_out1=v_w_out1, b_out1=v_b_out1,
                 final_norm_w=v_final_norm_w)
    names = list(weights)
    big = ["w_in0", "w_out0", "w_in1", "w_out1"]
    small_sharded = ["gla_gk_up", "rwkv_w_up", "rwkv_a_up", "b_in1", "b_out1"]
    replicated = [n for n in names if n not in big and n not in small_sharded]

    xs = x[0]
    tgt = loss_target[0]
    t = xs.shape[0]

    def view(w):
        shape = tuple(w.shape[-2:]) if w.ndim >= 2 else (1, w.shape[0])
        return shape, ("rows" if shape[0] > 1 and shape[1] <= LANES else "flat")

    def layout(ns, row0=0):
        specs = []
        for n in ns:
            shape, mode = view(weights[n])
            specs.append((shape, mode, row0))
            row0 += _packed_rows(shape, mode)
        return specs, row0

    sh_specs, n_shard_rows = layout(small_sharded)
    rep_specs, loss_row = layout(replicated, n_shard_rows)
    sh_modes, rep_modes = [s[1] for s in sh_specs], [s[1] for s in rep_specs]

    small_shard_pack = _pack_small([weights[n].reshape(view(weights[n])[0]) for n in small_sharded], sh_modes)
    g_in0, g_small = _all_gather([w_in0[0].T.astype(BF16), small_shard_pack])
    w0t = _w0t_to_padded(g_in0.reshape(-1, D_MODEL))
    later_shards = [w_out0[0].astype(BF16), w_in1[0].T.astype(BF16), w_out1[0].astype(BF16)]
    gs = [_take_small(g_small, row0, shape, mode) for shape, mode, row0 in sh_specs]
    join_cols = lambda a: jnp.transpose(a, (1, 0, 2)).reshape(a.shape[1], -1)
    gk_up, w_up, a_up = join_cols(gs[0]), join_cols(gs[1]), join_cols(gs[2])
    b_in, b_out = gs[3].reshape(1, -1), gs[4].reshape(1, -1)

    gk_up_p = _pad_to(gk_up, rows=128)
    mu = rwkv_mu
    rwkv_params = [mu[:, 0:512], mu[:, 512:1024], mu[:, 1024:1536], _pad_to(mu[:, 1536:1600], cols=128), _pad_to(mu[:, 1600:1664], cols=128),
                   rwkv_w0, _pad_to(w_up, rows=128), rwkv_a0, _pad_to(a_up, rows=128), rwkv_k_k, rwkv_k_a, rwkv_r_k.reshape(1, 512),
                   rwkv_ln_w, rwkv_ln_b]
    bq, bk, bv = b_in[:, :1024], b_in[:, 1024:1280], b_in[:, 1280:1536]
    cos, sin = _rope_tables(t)
    nw0, nw1, fw = norm_w[0:1], norm_w[1:2], final_norm_w.reshape(1, D_MODEL)

    d = D_MODEL
    wide = lambda arr: (arr, d, 0)
    hn0 = _norm_fwd("norm0_fwd", xs, nw0)
    proj0 = _matmul("proj0", hn0, w0t, "nt", 1024, 1024)
    o_a, gla_states = _gla_fwd(proj0, gk_up_p, gla_gk_bias, gla_norm_w)
    o_b, rwkv_states, rwkv_prevs, (g_out0, g_in1, g_out1) = _rwkv_fwd(proj0, rwkv_params, later_shards, ["gather"] * 3)
    wo0 = g_out0.reshape(1024, D_MODEL)
    w1t = _w1t_to_mine(g_in1.reshape(-1, D_MODEL))
    wo1 = g_out1.reshape(1024, D_MODEL)
    og0 = _gate_fwd("gate0_fwd", [o_a, o_b], proj0)
    h1, hn1 = _matmul_fused("out0_norm1", og0, wo0, "nn", [wide(xs)], [nw1], [(d, F32), (d, BF16)], [], _resid_norm)
    proj1 = _matmul("proj1", hn1, w1t, "nt", 1024, 1280)
    o_c, kst, vst = _swa_fwd(proj1, cos, sin, bq, bk, bv, attn_sinks)
    og1 = _gate_fwd("gate1_fwd", [o_c], proj1)
    dh2, loss_part, d_b_out, d_fw = _matmul_fused("out1_loss", og1, wo1, "nn", [wide(h1), wide(tgt)], [b_out, fw],
                                                  [(d, F32)], [LANES, d, d], _loss_head)

    d_oc, d_gate1 = _matmul_fused("out1_dx_gate1", dh2, wo1, "nt", [wide(o_c), wide(proj1)], [], [(d, F32), (d, BF16)], [], _gate_back)
    d_wo1 = _matmul("out1_dw", og1, dh2, "tn", 512, 512, BF16)
    dq, dk, dv, d_bq, d_bk, d_bv, d_sinks = _swa_bwd(proj1, cos, sin, bq, bk, bv, attn_sinks, kst, vst, d_oc)
    dproj1 = jnp.concatenate([d_gate1, dq, dk, dv], axis=1)
    dh1, d_nw1 = _matmul_fused("proj1_dx_norm1", dproj1, w1t, "nn", [wide(h1), wide(dh2)], [nw1], [(d, F32)], [d], _norm_back)
    d_w1t = _matmul("proj1_dw", dproj1, hn1, "tn", 512, 1024, BF16)
    d_oa, d_ob, d_gate0 = _matmul_fused("out0_dx_gate0", dh1, wo0, "nt", [(o_a, 512, 0), (o_b, 512, 0), wide(proj0)], [],
                                        [(512, F32), (512, F32), (d, BF16)], [], _gate_back)
    d_wo0 = _matmul("out0_dw", og0, dh1, "tn", 512, 512, BF16)
    dgq, dgk, dgv, dglow, d_gk_up, d_gk_bias, d_gla_nw = _gla_bwd(proj0, gk_up_p, gla_gk_bias, gla_norm_w, gla_states, d_oa)
    row_blocks = lambda a: a.astype(BF16).reshape(N_DEV, -1, D_MODEL)
    early = [row_blocks(_w1t_from_mine(d_w1t)), row_blocks(d_wo1), row_blocks(d_wo0)]
    (dr, dkk, dvv, dxw, dxa), d_rp, (r_in1, r_out1, r_out0) = _rwkv_bwd(
        proj0, rwkv_params, rwkv_states, rwkv_prevs, d_ob, early, ["scatter"] * 3)
    dproj0 = jnp.concatenate([d_gate0, dgv, dr, dkk, dvv, dgq, dgk, dglow, dxw, dxa, jnp.zeros((t, 128), BF16)], axis=1)
    d_w0 = _w0t_from_padded(_matmul("proj0_dw", dproj0, hn0, "tn", 512, 1024, BF16))
    grad_x, d_nw0, r_in0 = _matmul_fused("proj0_dx_norm0", dproj0, w0t, "nn", [wide(xs), wide(dh1)], [nw0], [(d, F32)], [d],
                                         _norm_back, [row_blocks(d_w0)], ["scatter"])

    contrib = dict(
        norm_w=jnp.concatenate([d_nw0, d_nw1], axis=0), gla_gk_bias=d_gk_bias, gla_norm_w=d_gla_nw,
        rwkv_mu=jnp.concatenate([d_rp[0], d_rp[1], d_rp[2], d_rp[3][:, :64], d_rp[4][:, :64]], axis=1),
        rwkv_w0=d_rp[5], rwkv_a0=d_rp[7], rwkv_k_k=d_rp[9], rwkv_k_a=d_rp[10], rwkv_r_k=d_rp[11].reshape(RWKV_HEADS, RWKV_N),
        rwkv_ln_w=d_rp[12], rwkv_ln_b=d_rp[13], attn_sinks=d_sinks, final_norm_w=d_fw)
    rep_pack = _pack_small([contrib[n] for n in replicated] + [loss_part[:, :1]], rep_modes + ["flat"])

    d_b_in = jnp.concatenate([d_bq, d_bk, d_bv], axis=1)
    full_small = [d_gk_up[:16], d_rp[6][:64], d_rp[8][:64], d_b_in, d_b_out]
    split_cols = lambda a: jnp.transpose(a.reshape(a.shape[0], N_DEV, -1), (1, 0, 2))
    small_parts = [split_cols(a) for a in full_small]
    small_pack = _pack_small(small_parts, sh_modes, lead=True)
    r_small, r_rep = _exchange([small_pack, rep_pack], ["scatter", "gather"])

    res = {}
    res["w_in0"] = tuple(a.T[None] for a in _adamw("adamw_w_in0", w_in0[0].T, r_in0, m_w_in0[0].T, v_w_in0[0].T, 256))
    res["w_out0"] = tuple(a[None] for a in _adamw("adamw_w_out0", w_out0[0], r_out0, m_w_out0[0], v_w_out0[0], 256))
    res["w_in1"] = tuple(a.T[None] for a in _adamw("adamw_w_in1", w_in1[0].T, r_in1, m_w_in1[0].T, v_w_in1[0].T, 256))
    res["w_out1"] = tuple(a[None] for a in _adamw("adamw_w_out1", w_out1[0], r_out1, m_w_out1[0], v_w_out1[0], 256))
    small_names = small_sharded + replicated
    slots = jnp.concatenate([r_small, r_rep], axis=1)
    as_2d = lambda a: a.reshape(1, -1) if a.ndim == 1 else a
    small_res, loss_row_out = _adamw_small(slots, sh_specs + rep_specs, [as_2d(weights[n]) for n in small_names],
                                           [as_2d(moms[n]) for n in small_names], [as_2d(vars_[n]) for n in small_names], loss_row)
    for n, vals in zip(small_names, small_res):
        res[n] = tuple(val.reshape(weights[n].shape) for val in vals)
    loss = loss_row_out[0, 0]
    return (loss, grad_x[None], *[res[n][0] for n in names], *[res[n][1] for n in names],
            *[res[n][2] for n in names], *[res[n][3] for n in names])
```

```python
import functools

import jax
import jax.numpy as jnp
from jax import lax
from jax.experimental import pallas as pl
from jax.experimental.pallas import tpu as pltpu

F32 = jnp.float32
BF16 = jnp.bfloat16
HI = lax.Precision.HIGHEST

D_MODEL = 1024
NORM_EPS = 1e-5
GLA_HEADS, GLA_DK, GLA_DV = 4, 64, 128
GLA_NORMALIZER = 16.0
GLA_CHUNK = 64
GLA_STEP = 256
RWKV_HEADS, RWKV_N = 8, 64
RWKV_LN_EPS = 64e-5
RWKV_CHUNK = 128
SWA_Q_HEADS, SWA_KV_HEADS, SWA_GROUP, SWA_HD = 16, 4, 4, 64
WINDOW = 128
SWA_STEP = 256
ROPE_THETA = 500000.0
NEG = -1e30
N_DEV = 8
LANES = 128

ADAM_LR, ADAM_B1, ADAM_B2, ADAM_EPS, ADAM_WD, ADAM_STEP = 0.001, 0.9, 0.999, 1e-08, 0.01, 10

N0P = 4096
C0 = dict(gate=(0, 1024), gv=(1024, 512), r=(1536, 512), k=(2048, 512), v=(2560, 512), gq=(3072, 256), gk=(3328, 256),
          glow=(3584, 128), xw=(3712, 128), xa=(3840, 128))
N1P = 2560
C1 = dict(gate=(0, 1024), q=(1024, 1024), k=(2048, 256), v=(2304, 256))

VMEM_LIMIT = 56 * 1024 * 1024

P_LORA = 1
P_GLA = 1
P_RWKV_G = 2
P_RWKV = 1
P_SWA = 1


def _cparams(sem=None):
    return pltpu.CompilerParams(dimension_semantics=sem, vmem_limit_bytes=VMEM_LIMIT)


DIMS = dict(nn=(((1,), (0,)), ((), ())), nt=(((1,), (1,)), ((), ())), tn=(((0,), (0,)), ((), ())))


def _split_bf16(a):
    hi = a.astype(BF16)
    return hi, (a - hi.astype(F32)).astype(BF16)


def _dot(a, b, mode, passes):
    dg = lambda p, q: lax.dot_general(p, q, DIMS[mode], preferred_element_type=F32)
    if passes == 1:
        return dg(a.astype(BF16), b.astype(BF16))
    if passes == 2:
        ah, (bh, bl) = a.astype(BF16), _split_bf16(b)
        return dg(ah, bh) + dg(ah, bl)
    if passes == 3:
        (ah, al), (bh, bl) = _split_bf16(a), _split_bf16(b)
        return dg(ah, bh) + dg(al, bh) + dg(ah, bl)
    return lax.dot_general(a, b, DIMS[mode], precision=HI, preferred_element_type=F32)


@functools.partial(jax.custom_vjp, nondiff_argnums=(2, 3))
def mmx(a, b, mode, passes):
    return _dot(a, b, mode, passes)


def _mmx_fwd(a, b, mode, passes):
    return _dot(a, b, mode, passes), (a, b)


def _mmx_bwd(mode, passes, res, g):
    a, b = res
    if mode == "nn":
        return _dot(g, b, "nt", passes), _dot(a, g, "tn", passes)
    if mode == "nt":
        return _dot(g, b, "nn", passes), _dot(g, a, "tn", passes)
    return _dot(b, g, "nt", passes), _dot(a, g, "nn", passes)


mmx.defvjp(_mmx_fwd, _mmx_bwd)


def _tri_dot(tri, x):
    t = tri.astype(BF16)
    x1 = x.astype(BF16)
    r1 = x - x1.astype(F32)
    x2 = r1.astype(BF16)
    x3 = (r1 - x2.astype(F32)).astype(BF16)
    dg = lambda q: jnp.dot(t, q, preferred_element_type=F32)
    return dg(x1) + dg(x2) + dg(x3)


@jax.custom_vjp
def cumsum_rows(x):
    return _tri_dot(tril_ones(x.shape[0]), x)


def _cumsum_fwd(x):
    return cumsum_rows(x), None


def _cumsum_bwd(_, g):
    i, j = _iota2(g.shape[0], g.shape[0])
    return (_tri_dot(jnp.where(i <= j, 1.0, 0.0).astype(F32), g),)


cumsum_rows.defvjp(_cumsum_fwd, _cumsum_bwd)


def _head_dot(x):
    i, j = _iota2(LANES, LANES)
    shift = RWKV_N.bit_length() - 1
    same = jnp.where(jnp.right_shift(i, shift) == jnp.right_shift(j, shift), 1.0, 0.0).astype(F32)
    return jnp.concatenate([_ones_right(x[:, g * LANES:(g + 1) * LANES], same) for g in range(x.shape[1] // LANES)], axis=1)


def _ones_right(x, ones):
    t = ones.astype(BF16)
    x1 = x.astype(BF16)
    r1 = x - x1.astype(F32)
    x2 = r1.astype(BF16)
    x3 = (r1 - x2.astype(F32)).astype(BF16)
    dg = lambda q: jnp.dot(q, t, preferred_element_type=F32)
    return dg(x1) + dg(x2) + dg(x3)


@jax.custom_vjp
def head_sum(x):
    return _head_dot(x)


def _head_sum_fwd(x):
    return head_sum(x), None


def _head_sum_bwd(_, g):
    return (_head_dot(g),)


head_sum.defvjp(_head_sum_fwd, _head_sum_bwd)


def cat_rows(*xs):
    return jnp.concatenate(xs, axis=0)


def _iota2(n, m):
    return lax.broadcasted_iota(jnp.int32, (n, m), 0), lax.broadcasted_iota(jnp.int32, (n, m), 1)


def tril_ones(c, strict=False):
    i, j = _iota2(c, c)
    return jnp.where((i > j) if strict else (i >= j), 1.0, 0.0).astype(F32)


def row_of(x, r):
    i = lax.broadcasted_iota(jnp.int32, x.shape, 0)
    return jnp.sum(jnp.where(i == r, x, 0.0), axis=0, keepdims=True)


@jax.custom_vjp
def shift_rows(x, prev):
    r = lax.broadcasted_iota(jnp.int32, x.shape, 0)
    return jnp.where(r == 0, prev, pltpu.roll(x, 1, 0))


def _shift_fwd(x, prev):
    return shift_rows(x, prev), None


def _shift_bwd(_, g):
    c = g.shape[0]
    r = lax.broadcasted_iota(jnp.int32, g.shape, 0)
    return jnp.where(r == c - 1, 0.0, pltpu.roll(g, c - 1, 0)), row_of(g, 0)


shift_rows.defvjp(_shift_fwd, _shift_bwd)


def log_sigmoid(x):
    return jnp.minimum(x, 0.0) - jnp.log(1.0 + jnp.exp(-jnp.abs(x)))


def softplus(x):
    return jnp.maximum(x, 0.0) + jnp.log(1.0 + jnp.exp(-jnp.abs(x)))


def sigmoid(x):
    return 1.0 / (1.0 + jnp.exp(-x))


def rms(x, w, eps=NORM_EPS):
    return x * lax.rsqrt(jnp.mean(x * x, axis=-1, keepdims=True) + eps) * w


def gla_chunk(state, toks, params):
    q, k, v, glow = toks
    gk_up, bias, norm_w = params
    c = GLA_CHUNK
    subs, heads = range(glow.shape[0] // c), range(GLA_HEADS)
    rows = lambda x, j: x[j * c:(j + 1) * c]
    hk = lambda x, h: x[:, h * GLA_DK:(h + 1) * GLA_DK]
    hv = lambda x, h: x[:, h * GLA_DV:(h + 1) * GLA_DV]
    ltri = tril_ones(c)
    g = log_sigmoid(mmx(glow, gk_up, "nn", P_LORA) + bias) / GLA_NORMALIZER
    b = [cumsum_rows(rows(g, j)) for j in subs]
    ref = [lax.stop_gradient(row_of(b[j], c // 2)) for j in subs]
    last = [row_of(b[j], c - 1) for j in subs]
    ql = [rows(q, j) * (GLA_DK ** -0.5) * jnp.exp(b[j] - ref[j]) for j in subs]
    kr = [rows(k, j) * jnp.exp(ref[j] - b[j]) for j in subs]
    kl = [rows(k, j) * jnp.exp(last[j] - b[j]) for j in subs]
    vj = [rows(v, j) for j in subs]
    e_ref, e_last = [jnp.exp(x) for x in ref], [jnp.exp(x) for x in last]
    att = [[mmx(hk(ql[j], h), hk(kr[j], h), "nt", P_GLA) * ltri for h in heads] for j in subs]
    o_in = [[mmx(att[j][h], hv(vj[j], h), "nn", P_GLA) for h in heads] for j in subs]
    kv = [[mmx(hv(vj[j], h), hk(kl[j], h), "tn", P_GLA) for h in heads] for j in subs]
    o = []
    for j in subs:
        o.append([o_in[j][h] + mmx(hk(ql[j], h), state[h] * hk(e_ref[j], h), "nt", P_GLA) for h in heads])
        state = [state[h] * hk(e_last[j], h) + kv[j][h] for h in heads]
    o = [[x * lax.rsqrt(jnp.mean(x * x, axis=-1, keepdims=True) + NORM_EPS) * norm_w for x in oj] for oj in o]
    return cat_rows(*[jnp.concatenate(oj, axis=1) for oj in o]), state


SOLVE_BLOCK = 128


def solve_unit_lower(ps, ws):
    n = ps[0].shape[0]
    heads = range(len(ps))
    if n > SOLVE_BLOCK:
        half = n // 2
        top = solve_unit_lower([p[:half, :half] for p in ps], [w[:half] for w in ws])
        rest = [ws[h][half:] + mmx(ps[h][half:, :half], top[h], "nn", P_RWKV) for h in heads]
        bottom = solve_unit_lower([p[half:, half:] for p in ps], rest)
        return [cat_rows(top[h], bottom[h]) for h in heads]
    u, p = ws, ps
    levels = max(1, (n - 1).bit_length())
    for it in range(levels):
        if it + 1 < levels:
            y = [mmx(p[h], jnp.concatenate([p[h], u[h]], axis=1), "nn", P_RWKV) for h in heads]
            u = [u[h] + y[h][:, n:] for h in heads]
            p = [y[h][:, :n] for h in heads]
        else:
            u = [u[h] + mmx(p[h], u[h], "nn", P_RWKV) for h in heads]
    return u


def rwkv_chunk(state, toks, params):
    S, pr, pk, pv, pxw, pxa = state
    r_, k_, v_, xw_, xa_ = toks
    mu_r, mu_k, mu_v, mu_xw, mu_xa, w0, w_up, a0, a_up, k_k, k_a, r_k, ln_w, ln_b = params
    c, n = xw_.shape[0], RWKV_N
    heads = range(RWKV_HEADS)
    hs = lambda x, h: x[:, h * n:(h + 1) * n]
    ltri = tril_ones(c)
    stri = tril_ones(c, strict=True)

    def lerp(x, prev, mu):
        return x + (shift_rows(x, prev) - x) * mu

    xw = jnp.tanh(lerp(xw_, pxw, mu_xw))
    xa = lerp(xa_, pxa, mu_xa)
    r = lerp(r_, pr, mu_r)
    k = lerp(k_, pk, mu_k)
    v = lerp(v_, pv, mu_v)
    w = -softplus(-(w0 + mmx(xw, w_up, "nn", P_LORA))) - 0.5
    lw = -jnp.exp(w)
    asig = sigmoid(a0 + mmx(xa, a_up, "nn", P_LORA))
    kk = k * k_k
    kk = kk * lax.rsqrt(jnp.maximum(head_sum(kk * kk), 1e-24))
    k2 = k * (1.0 + (asig - 1.0) * k_a)
    b = kk * asig
    cum = cumsum_rows(lw)
    ref = lax.stop_gradient(row_of(cum, c // 2))
    last = row_of(cum, c - 1)
    at = -kk * jnp.exp(cum - lw - ref)
    rt = r * jnp.exp(cum - ref)
    e_out = jnp.exp(ref - cum)
    bt, kt = b * e_out, k2 * e_out
    e_tail = jnp.exp(last - cum)
    bl, kl = b * e_tail, k2 * e_tail
    e_ref, e_last = jnp.exp(ref), jnp.exp(last)
    g = [mmx(cat_rows(hs(at, h), hs(rt, h)), cat_rows(hs(bt, h), hs(kt, h), S[h] * hs(e_ref, h)), "nt", P_RWKV_G) for h in heads]
    aab = [x[:c, :c] * stri for x in g]
    aak = [x[:c, c:2 * c] * stri for x in g]
    arb = [x[c:, :c] * ltri for x in g]
    ark = [x[c:, c:2 * c] * ltri for x in g]
    av = [mmx(cat_rows(aak[h], ark[h]), hs(v, h), "nn", P_RWKV) for h in heads]
    u = solve_unit_lower(aab, [g[h][:c, 2 * c:] + av[h][:c] for h in heads])
    o = [g[h][c:, 2 * c:] + av[h][c:] + mmx(arb[h], u[h], "nn", P_RWKV) for h in heads]
    s1 = [S[h] * hs(e_last, h) + mmx(cat_rows(u[h], hs(v, h)), cat_rows(hs(bl, h), hs(kl, h)), "tn", P_RWKV) for h in heads]
    o = jnp.concatenate(o, axis=1)
    d = o - head_sum(o) * (1.0 / n)
    var = head_sum(d * d) * (1.0 / n)
    o = d * lax.rsqrt(var + RWKV_LN_EPS) * ln_w + ln_b + head_sum(r * k2 * r_k) * v
    new_state = (s1, row_of(r_, c - 1), row_of(k_, c - 1), row_of(v_, c - 1), row_of(xw_, c - 1), row_of(xa_, c - 1))
    return o, new_state


ROPE_HALF = 8


def _rot_half_raw(x):
    lane = lax.broadcasted_iota(jnp.int32, (x.shape[0], LANES), 1) & (SWA_HD - 1)
    out = []
    for i in range(x.shape[1] // LANES):
        g = x[:, i * LANES:(i + 1) * LANES]
        up, down = pltpu.roll(g, LANES - ROPE_HALF, 1), pltpu.roll(g, ROPE_HALF, 1)
        out.append(jnp.where(lane < ROPE_HALF, -up, jnp.where(lane < 2 * ROPE_HALF, down, 0.0)))
    return out[0] if len(out) == 1 else jnp.concatenate(out, axis=1)


@jax.custom_vjp
def rot_half(x):
    return _rot_half_raw(x)


rot_half.defvjp(lambda x: (_rot_half_raw(x), None), lambda _, g: (-_rot_half_raw(g),))


def rope(x, cos2, sin2):
    reps = x.shape[1] // LANES
    tile = lambda t: t if reps == 1 else jnp.concatenate([t] * reps, axis=1)
    return x * tile(cos2) + rot_half(x) * tile(sin2)


def swa_chunk(state, toks, params, first):
    kprev, vprev = state
    q_, k_, v_, cos, sin = toks
    bq, bk, bv, sinks = params
    c, ng = WINDOW, SWA_GROUP
    n_sub = cos.shape[0] // c
    units = [(j, g) for j in range(n_sub) for g in range(SWA_KV_HEADS)]
    rows = lambda x, j: x[j * c:(j + 1) * c]
    hs = lambda g: range(g * ng, (g + 1) * ng)
    head = lambda x, h: x[:, h * SWA_HD:(h + 1) * SWA_HD]
    qi, kj = _iota2(ng * c, 2 * c)
    qpos = qi & (c - 1)
    cur_ok = (kj >= c) & (qpos >= kj - c)
    prev_ok = (kj < c) & (kj > qpos)
    ok = [cur_ok | (prev_ok & jnp.logical_not(first))] + [cur_ok | prev_ok] * (n_sub - 1)
    q_all = rope(q_ + bq, cos, sin) * (SWA_HD ** -0.5)
    k_all = rope(k_ + bk, cos, sin)
    v_all = v_ + bv
    k = {(j, g): rows(head(k_all, g), j) for j, g in units}
    v = {(j, g): rows(head(v_all, g), j) for j, g in units}
    q = {(j, g): cat_rows(*[rows(head(q_all, h), j) for h in hs(g)]) for j, g in units}
    kp = lambda j, g: kprev[g] if j == 0 else k[(j - 1, g)]
    vp = lambda j, g: vprev[g] if j == 0 else v[(j - 1, g)]
    s = {(j, g): jnp.where(ok[j], mmx(q[(j, g)], cat_rows(kp(j, g), k[(j, g)]), "nt", P_SWA), NEG) for j, g in units}
    sink = [cat_rows(*[jnp.broadcast_to(sinks[h], (c, 1)) for h in hs(g)]) for g in range(SWA_KV_HEADS)]
    m = {(j, g): lax.stop_gradient(jnp.maximum(jnp.max(s[(j, g)], axis=-1, keepdims=True), sink[g])) for j, g in units}
    p = {u: jnp.exp(s[u] - m[u]) for u in units}
    ones = jnp.ones((2 * c, SWA_HD), F32)
    pv = {(j, g): mmx(p[(j, g)], cat_rows(vp(j, g), v[(j, g)]), "nn", P_SWA) for j, g in units}
    den = {u: mmx(p[u], ones, "nn", P_SWA) for u in units}
    o = {(j, g): pv[(j, g)] / (den[(j, g)] + jnp.exp(sink[g] - m[(j, g)])) for j, g in units}
    outs = [cat_rows(*[o[(j, g)][i * c:(i + 1) * c] for j in range(n_sub)]) for g in range(SWA_KV_HEADS) for i in range(ng)]
    last = n_sub - 1
    return outs, ([k[(last, g)] for g in range(SWA_KV_HEADS)], [v[(last, g)] for g in range(SWA_KV_HEADS)])


def _heads(ref, n, w, rows=slice(None)):
    return [ref[rows, h * w:(h + 1) * w] for h in range(n)]


def _put_heads(ref, vals, w, rows=slice(None), add=False):
    for h, val in enumerate(vals):
        if add:
            ref[rows, h * w:(h + 1) * w] += val
        else:
            ref[rows, h * w:(h + 1) * w] = val


def _col(block_w, name, table):
    off, w = table[name]
    assert off % block_w == 0 and w % block_w == 0
    return off // block_w


def _tok_spec(c, w, colblock, n=None):
    if n is None:
        return pl.BlockSpec((c, w), lambda i: (i, colblock))
    return pl.BlockSpec((c, w), lambda i: (n - 1 - i, colblock))


def _full_spec(shape):
    return pl.BlockSpec(shape, lambda i: (0,) * len(shape))


def _matmul(name, a, b, mode, tm, tn, out_dtype=F32):
    (m, kd) = (a.shape[1], a.shape[0]) if mode == "tn" else a.shape
    n = b.shape[0] if mode == "nt" else b.shape[1]
    assert m % tm == 0 and n % tn == 0
    a_spec = pl.BlockSpec((kd, tm), lambda j, i: (0, i)) if mode == "tn" else pl.BlockSpec((tm, kd), lambda j, i: (i, 0))
    b_spec = pl.BlockSpec((tn, kd), lambda j, i: (j, 0)) if mode == "nt" else pl.BlockSpec((kd, tn), lambda j, i: (0, j))

    def body(a_ref, b_ref, o_ref):
        o_ref[...] = lax.dot_general(a_ref[...].astype(BF16), b_ref[...].astype(BF16), DIMS[mode],
                                     preferred_element_type=F32).astype(out_dtype)

    return pl.pallas_call(
        body, name=name, grid=(n // tn, m // tm), in_specs=[a_spec, b_spec],
        out_specs=pl.BlockSpec((tm, tn), lambda j, i: (i, j)), out_shape=jax.ShapeDtypeStruct((m, n), out_dtype),
        compiler_params=_cparams(("arbitrary", "arbitrary")))(a, b)


TOK_TILE = 512


def _matmul_fused(name, a, b, mode, tiles, rows, outs, sums, epilogue, comm=(), kinds=()):
    made = callable(a)
    m = tiles[0][0].shape[0] if made else a.shape[0]
    kd = b.shape[0] if mode == "nn" else b.shape[1]
    n = b.shape[1] if mode == "nn" else b.shape[0]
    tm = TOK_TILE
    steps = m // tm
    if made:
        outs = [(kd, BF16)] + list(outs)
    nt_, nr, no, ns, ncomm = len(tiles), len(rows), len(outs), len(sums), len(comm)

    def body(*refs):
        at = 1 if made else 2
        b_ref = refs[at - 1]
        tile_refs, row_refs, comm_in = refs[at:at + nt_], refs[at + nt_:at + nt_ + nr], refs[at + nt_ + nr:at + nt_ + nr + ncomm]
        at += nt_ + nr + ncomm
        out_refs, sum_refs, comm_out = refs[at:at + no], refs[at + no:at + no + ns], refs[at + no + ns:at + no + ns + ncomm]
        sems = refs[at + no + ns + ncomm:]
        i = pl.program_id(0)

        @pl.when(i == 0)
        def _():
            if ncomm:
                _comm_start(*_comm_copies(comm_in, comm_out, kinds, *sems))
            for ref in sum_refs:
                ref[...] = jnp.zeros_like(ref)

        extras = [r[...] for r in tile_refs] + [r[...] for r in row_refs]
        a_blk = (a(*extras) if made else refs[0][...]).astype(BF16)
        acc = lax.dot_general(a_blk, b_ref[...].astype(BF16), DIMS[mode], preferred_element_type=F32)
        res = epilogue(acc, *extras)
        if made:
            res = (a_blk,) + tuple(res)
        for ref, val in zip(out_refs, res[:no]):
            ref[...] = val.astype(ref.dtype)
        for ref, val in zip(sum_refs, res[no:]):
            ref[...] += val

        if ncomm:
            @pl.when(i == steps - 1)
            def _():
                _comm_wait(*_comm_copies(comm_in, comm_out, kinds, *sems))

    in_specs = ([] if made else [pl.BlockSpec((tm, kd), lambda i: (i, 0))]) + [_full_spec(b.shape)]
    in_specs += [pl.BlockSpec((tm, w), functools.partial(lambda i, cb: (i, cb), cb=cb)) for _, w, cb in tiles]
    in_specs += [_full_spec(r.shape) for r in rows] + [ANY] * ncomm
    out_specs = [pl.BlockSpec((tm, w), lambda i: (i, 0)) for w, _ in outs] + [_full_spec((1, w)) for w in sums] + [ANY] * ncomm
    out_shape = ([jax.ShapeDtypeStruct((m, w), dt) for w, dt in outs] + [jax.ShapeDtypeStruct((1, w), F32) for w in sums]
                 + _comm_out_shapes(comm, kinds))
    return pl.pallas_call(body, name=name, grid=(steps,), in_specs=in_specs, out_specs=out_specs, out_shape=out_shape,
                          scratch_shapes=_comm_scratch(ncomm) if ncomm else [],
                          compiler_params=_cparams(("arbitrary",)))(*([] if made else [a]), b, *[t[0] for t in tiles], *rows, *comm)


def _resid_norm(y, x, w):
    h = x + y
    return h, rms(h, w)


def _norm_back(dhn, h, dres, w):
    _, vjp = jax.vjp(rms, h, w)
    dh, dw = vjp(dhn)
    return dh + dres, dw


def _gate_back(dog, *o_and_gate):
    outs, g = o_and_gate[:-1], o_and_gate[-1]
    s = sigmoid(g)
    silu, dsilu = g * s, s * (1.0 + g * (1.0 - s))
    d_outs, c = [], 0
    for o in outs:
        w = o.shape[1]
        d_outs.append(dog[:, c:c + w] * silu[:, c:c + w])
        c += w
    o_all = outs[0] if len(outs) == 1 else jnp.concatenate(outs, axis=1)
    return (*d_outs, dog * o_all * dsilu)


def _loss_head(y1, h1, target, b_out, fw):
    def f(h2, w):
        err = rms(h2, w) - target
        return 0.5 * jnp.sum(jnp.mean(err * err, axis=-1, keepdims=True), axis=0, keepdims=True)

    loss, vjp = jax.vjp(f, h1 + y1 + b_out, fw)
    dh2, dfw = vjp(jnp.ones((1, 1), F32))
    return dh2, jnp.broadcast_to(loss, (1, LANES)), jnp.sum(dh2, axis=0, keepdims=True), dfw


def _gla_load(q_ref, k_ref, v_ref, gl_ref, up_ref, bias_ref, nw_ref):
    toks = (q_ref[...], k_ref[...], v_ref[...], gl_ref[...])
    params = (up_ref[...], bias_ref[...], nw_ref[...])
    return toks, params


def _gla_specs(c, n=None):
    toks = [_tok_spec(c, 256, _col(256, "gq", C0), n), _tok_spec(c, 256, _col(256, "gk", C0), n),
            _tok_spec(c, 512, _col(512, "gv", C0), n), _tok_spec(c, 128, _col(128, "glow", C0), n)]
    params = [_full_spec((128, 256)), _full_spec((1, 256)), _full_spec((1, 128))]
    return toks, params


def _gla_fwd(proj0, gk_up, gk_bias, norm_w):
    t = proj0.shape[0]
    c = GLA_STEP
    nc = t // c
    toks_s, params_s = _gla_specs(c)

    def body(q_ref, k_ref, v_ref, gl_ref, up_ref, bias_ref, nw_ref, o_ref, st_ref, s_scr):
        @pl.when(pl.program_id(0) == 0)
        def _():
            s_scr[...] = jnp.zeros_like(s_scr)

        st_ref[...] = s_scr[...]
        toks, params = _gla_load(q_ref, k_ref, v_ref, gl_ref, up_ref, bias_ref, nw_ref)
        state = [s_scr[h * GLA_DV:(h + 1) * GLA_DV, :] for h in range(GLA_HEADS)]
        o_ref[...], new = gla_chunk(state, toks, params)
        for h in range(GLA_HEADS):
            s_scr[h * GLA_DV:(h + 1) * GLA_DV, :] = new[h]

    return pl.pallas_call(
        body, name="gla_fwd", grid=(nc,), in_specs=toks_s + params_s,
        out_specs=(_tok_spec(c, 512, 0), pl.BlockSpec((512, GLA_DK), lambda i: (i, 0))),
        out_shape=(jax.ShapeDtypeStruct((t, 512), F32), jax.ShapeDtypeStruct((nc * 512, GLA_DK), F32)),
        scratch_shapes=[pltpu.VMEM((512, GLA_DK), F32)], compiler_params=_cparams(("arbitrary",)))(
            proj0, proj0, proj0, proj0, gk_up, gk_bias, norm_w)


def _gla_bwd(proj0, gk_up, gk_bias, norm_w, states, do):
    t = proj0.shape[0]
    c = GLA_STEP
    nc = t // c
    toks_s, params_s = _gla_specs(c, nc)

    def body(q_ref, k_ref, v_ref, gl_ref, up_ref, bias_ref, nw_ref, st_ref, do_ref,
             dq_ref, dk_ref, dv_ref, dgl_ref, dup_ref, dbias_ref, dnw_ref, ds_scr):
        @pl.when(pl.program_id(0) == 0)
        def _():
            ds_scr[...] = jnp.zeros_like(ds_scr)
            dup_ref[...] = jnp.zeros_like(dup_ref)
            dbias_ref[...] = jnp.zeros_like(dbias_ref)
            dnw_ref[...] = jnp.zeros_like(dnw_ref)

        toks, params = _gla_load(q_ref, k_ref, v_ref, gl_ref, up_ref, bias_ref, nw_ref)
        rows = lambda h: slice(h * GLA_DV, (h + 1) * GLA_DV)
        state = [st_ref[rows(h), :] for h in range(GLA_HEADS)]
        _, vjp = jax.vjp(gla_chunk, state, toks, params)
        dstate_in = [ds_scr[rows(h), :] for h in range(GLA_HEADS)]
        dstate, dtoks, (dup, dbias, dnw) = vjp((do_ref[...], dstate_in))
        for ref, val in zip((dq_ref, dk_ref, dv_ref, dgl_ref), dtoks):
            ref[...] = val.astype(ref.dtype)
        dup_ref[...] += dup
        dbias_ref[...] += dbias
        dnw_ref[...] += dnw
        for h in range(GLA_HEADS):
            ds_scr[rows(h), :] = dstate[h]

    rev = lambda w: pl.BlockSpec((c, w), lambda i: (nc - 1 - i, 0))
    return pl.pallas_call(
        body, name="gla_bwd", grid=(nc,),
        in_specs=toks_s + params_s + [pl.BlockSpec((512, GLA_DK), lambda i: (nc - 1 - i, 0)), rev(512)],
        out_specs=(rev(256), rev(256), rev(512), rev(128), _full_spec((128, 256)), _full_spec((1, 256)), _full_spec((1, 128))),
        out_shape=(jax.ShapeDtypeStruct((t, 256), BF16), jax.ShapeDtypeStruct((t, 256), BF16), jax.ShapeDtypeStruct((t, 512), BF16),
                   jax.ShapeDtypeStruct((t, 128), BF16), jax.ShapeDtypeStruct((128, 256), F32), jax.ShapeDtypeStruct((1, 256), F32),
                   jax.ShapeDtypeStruct((1, 128), F32)),
        scratch_shapes=[pltpu.VMEM((512, GLA_DK), F32)], compiler_params=_cparams(("arbitrary",)))(
            proj0, proj0, proj0, proj0, gk_up, gk_bias, norm_w, states, do)


RWKV_PARAM_SHAPES = [(1, 512), (1, 512), (1, 512), (1, 128), (1, 128), (1, 512), (128, 512), (1, 512), (128, 512),
                     (1, 512), (1, 512), (1, 512), (1, 512), (1, 512)]
PREV_W = 1792
PREV_COLS = [slice(0, 512), slice(512, 1024), slice(1024, 1536), slice(1536, 1664), slice(1664, 1792)]


def _rwkv_load(r_ref, k_ref, v_ref, xw_ref, xa_ref, p_refs):
    toks = (r_ref[...], k_ref[...], v_ref[...], xw_ref[...], xa_ref[...])
    return toks, tuple(p[...] for p in p_refs)


def _rwkv_state(s_ref, prev_ref):
    n = RWKV_N
    S = [s_ref[h * n:(h + 1) * n, :] for h in range(RWKV_HEADS)]
    return (S,) + tuple(prev_ref[0:1, cols] for cols in PREV_COLS)


def _rwkv_put_state(s_ref, prev_ref, state):
    n = RWKV_N
    for h in range(RWKV_HEADS):
        s_ref[h * n:(h + 1) * n, :] = state[0][h]
    for cols, val in zip(PREV_COLS, state[1:]):
        prev_ref[0:1, cols] = val


def _rwkv_specs(c, n=None):
    toks = [_tok_spec(c, 512, _col(512, "r", C0), n), _tok_spec(c, 512, _col(512, "k", C0), n),
            _tok_spec(c, 512, _col(512, "v", C0), n), _tok_spec(c, 128, _col(128, "xw", C0), n),
            _tok_spec(c, 128, _col(128, "xa", C0), n)]
    return toks, [_full_spec(s) for s in RWKV_PARAM_SHAPES]


def _rwkv_fwd(proj0, params, comm, kinds):
    t = proj0.shape[0]
    c = RWKV_CHUNK
    nc = t // c
    toks_s, params_s = _rwkv_specs(c)
    npar, ncomm = len(params), len(comm)

    def body(*refs):
        tok_refs, p_refs = refs[:5], refs[5:5 + npar]
        comm_in = refs[5 + npar:5 + npar + ncomm]
        o_ref, st_ref, pst_ref = refs[5 + npar + ncomm:8 + npar + ncomm]
        comm_out = refs[8 + npar + ncomm:8 + npar + 2 * ncomm]
        s_scr, prev_scr = refs[8 + npar + 2 * ncomm:10 + npar + 2 * ncomm]
        sems = refs[10 + npar + 2 * ncomm:]
        i = pl.program_id(0)

        @pl.when(i == 0)
        def _():
            _comm_start(*_comm_copies(comm_in, comm_out, kinds, *sems))
            s_scr[...] = jnp.zeros_like(s_scr)
            prev_scr[...] = jnp.zeros_like(prev_scr)

        st_ref[...] = s_scr[...]
        pst_ref[...] = prev_scr[...]
        toks, prm = _rwkv_load(*tok_refs, p_refs)
        o_ref[...], new = rwkv_chunk(_rwkv_state(s_scr, prev_scr), toks, prm)
        _rwkv_put_state(s_scr, prev_scr, new)

        @pl.when(i == nc - 1)
        def _():
            _comm_wait(*_comm_copies(comm_in, comm_out, kinds, *sems))

    outs = pl.pallas_call(
        body, name="rwkv_fwd", grid=(nc,), in_specs=toks_s + params_s + [ANY] * ncomm,
        out_specs=[_tok_spec(c, 512, 0), pl.BlockSpec((512, RWKV_N), lambda i: (i, 0)), pl.BlockSpec((8, PREV_W), lambda i: (i, 0))]
        + [ANY] * ncomm,
        out_shape=[jax.ShapeDtypeStruct((t, 512), F32), jax.ShapeDtypeStruct((nc * 512, RWKV_N), F32),
                   jax.ShapeDtypeStruct((nc * 8, PREV_W), F32)] + _comm_out_shapes(comm, kinds),
        scratch_shapes=[pltpu.VMEM((512, RWKV_N), F32), pltpu.VMEM((8, PREV_W), F32)] + _comm_scratch(ncomm),
        compiler_params=_cparams(("arbitrary",)))(proj0, proj0, proj0, proj0, proj0, *params, *comm)
    return outs[0], outs[1], outs[2], outs[3:]


def _rwkv_bwd(proj0, params, states, prevs, do, comm, kinds):
    t = proj0.shape[0]
    c = RWKV_CHUNK
    nc = t // c
    toks_s, params_s = _rwkv_specs(c, nc)
    npar, ncomm = len(params), len(comm)

    def body(*refs):
        tok_refs, p_refs = refs[:5], refs[5:5 + npar]
        st_ref, pst_ref, do_ref = refs[5 + npar:8 + npar]
        comm_in = refs[8 + npar:8 + npar + ncomm]
        outs = refs[8 + npar + ncomm:]
        dtok_refs, dp_refs, comm_out = outs[:5], outs[5:5 + npar], outs[5 + npar:5 + npar + ncomm]
        ds_scr, dprev_scr = outs[5 + npar + ncomm:7 + npar + ncomm]
        sems = outs[7 + npar + ncomm:]
        i = pl.program_id(0)

        @pl.when(i == 0)
        def _():
            _comm_start(*_comm_copies(comm_in, comm_out, kinds, *sems))
            ds_scr[...] = jnp.zeros_like(ds_scr)
            dprev_scr[...] = jnp.zeros_like(dprev_scr)
            for dp in dp_refs:
                dp[...] = jnp.zeros_like(dp)

        toks, prm = _rwkv_load(*tok_refs, p_refs)
        _, vjp = jax.vjp(rwkv_chunk, _rwkv_state(st_ref, pst_ref), toks, prm)
        dstate, dtoks, dprm = vjp((do_ref[...], _rwkv_state(ds_scr, dprev_scr)))
        for ref, val in zip(dtok_refs, dtoks):
            ref[...] = val.astype(ref.dtype)
        for ref, val in zip(dp_refs, dprm):
            ref[...] += val
        _rwkv_put_state(ds_scr, dprev_scr, dstate)

        @pl.when(i == nc - 1)
        def _():
            _comm_wait(*_comm_copies(comm_in, comm_out, kinds, *sems))

    rev = lambda w: pl.BlockSpec((c, w), lambda i: (nc - 1 - i, 0))
    outs = pl.pallas_call(
        body, name="rwkv_bwd", grid=(nc,),
        in_specs=toks_s + params_s + [pl.BlockSpec((512, RWKV_N), lambda i: (nc - 1 - i, 0)),
                                      pl.BlockSpec((8, PREV_W), lambda i: (nc - 1 - i, 0)), rev(512)] + [ANY] * ncomm,
        out_specs=[rev(512), rev(512), rev(512), rev(128), rev(128)] + params_s + [ANY] * ncomm,
        out_shape=[jax.ShapeDtypeStruct((t, w), BF16) for w in (512, 512, 512, 128, 128)]
        + [jax.ShapeDtypeStruct(s, F32) for s in RWKV_PARAM_SHAPES] + _comm_out_shapes(comm, kinds),
        scratch_shapes=[pltpu.VMEM((512, RWKV_N), F32), pltpu.VMEM((8, PREV_W), F32)] + _comm_scratch(ncomm),
        compiler_params=_cparams(("arbitrary",)))(proj0, proj0, proj0, proj0, proj0, *params, states, prevs, do, *comm)
    return outs[:5], outs[5:5 + npar], outs[5 + npar:]


def _swa_load(q_ref, k_ref, v_ref, cos_ref, sin_ref, bq_ref, bk_ref, bv_ref, sk_ref):
    toks = (q_ref[...], k_ref[...], v_ref[...], cos_ref[...], sin_ref[...])
    params = (bq_ref[...], bk_ref[...], bv_ref[...], _heads(sk_ref, 16, 1))
    return toks, params


def _swa_specs(c, n=None):
    toks = [_tok_spec(c, 1024, _col(1024, "q", C1), n), _tok_spec(c, 256, _col(256, "k", C1), n),
            _tok_spec(c, 256, _col(256, "v", C1), n), _tok_spec(c, LANES, 0, n), _tok_spec(c, LANES, 0, n)]
    params = [_full_spec((1, 1024)), _full_spec((1, 256)), _full_spec((1, 256)), _full_spec((1, 16))]
    return toks, params


def _swa_fwd(proj1, cos, sin, bq, bk, bv, sinks):
    t = proj1.shape[0]
    c = SWA_STEP
    nb = t // c
    toks_s, params_s = _swa_specs(c)
    state_spec = pl.BlockSpec((WINDOW, 256), lambda i: (i, 0))

    def body(q_ref, k_ref, v_ref, cos_ref, sin_ref, bq_ref, bk_ref, bv_ref, sk_ref, o_ref, kst_ref, vst_ref, k_scr, v_scr):
        first = pl.program_id(0) == 0

        @pl.when(first)
        def _():
            k_scr[...] = jnp.zeros_like(k_scr)
            v_scr[...] = jnp.zeros_like(v_scr)

        kst_ref[...] = k_scr[...]
        vst_ref[...] = v_scr[...]
        toks, params = _swa_load(q_ref, k_ref, v_ref, cos_ref, sin_ref, bq_ref, bk_ref, bv_ref, sk_ref)
        outs, (kn, vn) = swa_chunk((_heads(k_scr, 4, SWA_HD), _heads(v_scr, 4, SWA_HD)), toks, params, first)
        _put_heads(o_ref, outs, SWA_HD)
        _put_heads(k_scr, kn, SWA_HD)
        _put_heads(v_scr, vn, SWA_HD)

    return pl.pallas_call(
        body, name="swa_fwd", grid=(nb,), in_specs=toks_s + params_s,
        out_specs=(_tok_spec(c, 1024, 0), state_spec, state_spec),
        out_shape=(jax.ShapeDtypeStruct((t, 1024), F32), jax.ShapeDtypeStruct((nb * WINDOW, 256), F32),
                   jax.ShapeDtypeStruct((nb * WINDOW, 256), F32)),
        scratch_shapes=[pltpu.VMEM((WINDOW, 256), F32), pltpu.VMEM((WINDOW, 256), F32)],
        compiler_params=_cparams(("arbitrary",)))(proj1, proj1, proj1, cos, sin, bq, bk, bv, sinks)


def _swa_bwd(proj1, cos, sin, bq, bk, bv, sinks, kst, vst, do):
    t = proj1.shape[0]
    c = SWA_STEP
    nb = t // c
    toks_s, params_s = _swa_specs(c, nb)
    state_spec = pl.BlockSpec((WINDOW, 256), lambda i: (nb - 1 - i, 0))

    def body(q_ref, k_ref, v_ref, cos_ref, sin_ref, bq_ref, bk_ref, bv_ref, sk_ref, kst_ref, vst_ref, do_ref,
             dq_ref, dk_ref, dv_ref, dbq_ref, dbk_ref, dbv_ref, dsk_ref, dk_scr, dv_scr):
        i = pl.program_id(0)

        @pl.when(i == 0)
        def _():
            dk_scr[...] = jnp.zeros_like(dk_scr)
            dv_scr[...] = jnp.zeros_like(dv_scr)
            for ref in (dbq_ref, dbk_ref, dbv_ref, dsk_ref):
                ref[...] = jnp.zeros_like(ref)

        first = i == nb - 1
        toks, params = _swa_load(q_ref, k_ref, v_ref, cos_ref, sin_ref, bq_ref, bk_ref, bv_ref, sk_ref)
        f = functools.partial(swa_chunk, first=first)
        _, vjp = jax.vjp(f, (_heads(kst_ref, 4, SWA_HD), _heads(vst_ref, 4, SWA_HD)), toks, params)
        dstate_in = (_heads(dk_scr, 4, SWA_HD), _heads(dv_scr, 4, SWA_HD))
        (dkp, dvp), (dq, dk, dv, _, _), (dbq, dbk, dbv, dsk) = vjp((_heads(do_ref, 16, SWA_HD), dstate_in))
        dq_ref[...], dk_ref[...], dv_ref[...] = dq.astype(BF16), dk.astype(BF16), dv.astype(BF16)
        dbq_ref[...] += dbq
        dbk_ref[...] += dbk
        dbv_ref[...] += dbv
        _put_heads(dsk_ref, dsk, 1, add=True)
        _put_heads(dk_scr, dkp, SWA_HD)
        _put_heads(dv_scr, dvp, SWA_HD)

    rev = lambda w: pl.BlockSpec((c, w), lambda i: (nb - 1 - i, 0))
    return pl.pallas_call(
        body, name="swa_bwd", grid=(nb,), in_specs=toks_s + params_s + [state_spec, state_spec, rev(1024)],
        out_specs=(rev(1024), rev(256), rev(256), _full_spec((1, 1024)), _full_spec((1, 256)), _full_spec((1, 256)), _full_spec((1, 16))),
        out_shape=(jax.ShapeDtypeStruct((t, 1024), BF16), jax.ShapeDtypeStruct((t, 256), BF16), jax.ShapeDtypeStruct((t, 256), BF16),
                   jax.ShapeDtypeStruct((1, 1024), F32), jax.ShapeDtypeStruct((1, 256), F32), jax.ShapeDtypeStruct((1, 256), F32),
                   jax.ShapeDtypeStruct((1, 16), F32)),
        scratch_shapes=[pltpu.VMEM((WINDOW, 256), F32), pltpu.VMEM((WINDOW, 256), F32)],
        compiler_params=_cparams(("arbitrary",)))(proj1, proj1, proj1, cos, sin, bq, bk, bv, sinks, kst, vst, do)


MESH = pl.DeviceIdType.MESH
ANY = pl.BlockSpec(memory_space=pl.ANY)


def _my_place():
    return lax.axis_index("x"), lax.axis_index("y"), lax.axis_index("c")


def _all_gather(shards):
    n = len(shards)

    def body(*refs):
        in_refs, out_refs = refs[:n], refs[n:2 * n]
        send_sems, recv_sems, local_sems = refs[2 * n:]
        x, y, c = _my_place()
        me, sibling = (x, y, c), (x, y, 1 - c)
        chips = [(1 - x, y), (x, 1 - y), (1 - x, 1 - y)]

        def slot(out_ref, place):
            px, py, pc = place
            return out_ref.at[4 * px + 2 * py + pc]

        def copy(a, k, block, to, src=None):
            return pltpu.make_async_remote_copy(
                src_ref=slot(out_refs[a], block) if src is None else src, dst_ref=slot(out_refs[a], block),
                send_sem=send_sems.at[a, k], recv_sem=recv_sems.at[a, k], device_id=to, device_id_type=MESH)

        mine = [pltpu.make_async_copy(in_refs[a], slot(out_refs[a], me), local_sems.at[a]) for a in range(n)]
        for cp in mine:
            cp.start()
        first = []
        for a in range(n):
            first.append(copy(a, 0, me, sibling, src=in_refs[a]))
            first += [copy(a, 1 + j, me, (*chip, c), src=in_refs[a]) for j, chip in enumerate(chips)]
        for cp in first:
            cp.start()
        passed = []
        for j, chip in enumerate(chips):
            for a in range(n):
                copy(a, 1 + j, (*chip, c), me).wait_recv()
                fwd = copy(a, 4 + j, (*chip, c), sibling)
                fwd.start()
                passed.append(fwd)
        for a in range(n):
            copy(a, 0, sibling, me).wait_recv()
            for j, chip in enumerate(chips):
                copy(a, 4 + j, (*chip, 1 - c), me).wait_recv()
        for cp in first + passed:
            cp.wait_send()
        for cp in mine:
            cp.wait()

    return pl.pallas_call(
        body, name="all_gather_weights", in_specs=[ANY] * n, out_specs=[ANY] * n,
        out_shape=[jax.ShapeDtypeStruct((N_DEV,) + s.shape, s.dtype) for s in shards],
        scratch_shapes=_comm_scratch(n))(*shards)


def _comm_copies(in_refs, out_refs, kinds, send_sems, recv_sems, local_sems):
    x, y, c = _my_place()
    my_idx = 4 * x + 2 * y + c
    src = lambda a, idx: in_refs[a] if kinds[a] == "gather" else in_refs[a].at[idx]
    local = [pltpu.make_async_copy(src(a, my_idx), out_refs[a].at[my_idx], local_sems.at[a]) for a in range(len(kinds))]
    remote = []
    for rel in range(1, N_DEV):
        px, py, pc = x ^ ((rel >> 2) & 1), y ^ ((rel >> 1) & 1), c ^ (rel & 1)
        for a in range(len(kinds)):
            remote.append(pltpu.make_async_remote_copy(
                src_ref=src(a, 4 * px + 2 * py + pc), dst_ref=out_refs[a].at[my_idx], send_sem=send_sems.at[a, rel - 1],
                recv_sem=recv_sems.at[a, rel - 1], device_id=(px, py, pc), device_id_type=MESH))
    return local, remote


def _comm_start(local, remote):
    for cp in local + remote:
        cp.start()


def _comm_wait(local, remote):
    for cp in remote:
        cp.wait_recv()
    for cp in remote:
        cp.wait_send()
    for cp in local:
        cp.wait()


def _comm_out_shapes(arrays, kinds):
    return [jax.ShapeDtypeStruct(((N_DEV,) + a.shape) if k == "gather" else a.shape, a.dtype) for a, k in zip(arrays, kinds)]


def _comm_scratch(n):
    return [pltpu.SemaphoreType.DMA((n, N_DEV - 1)), pltpu.SemaphoreType.DMA((n, N_DEV - 1)), pltpu.SemaphoreType.DMA((n,))]


def _exchange(arrays, kinds):
    n = len(arrays)

    def body(*refs):
        copies = _comm_copies(refs[:n], refs[n:2 * n], kinds, *refs[2 * n:])
        _comm_start(*copies)
        _comm_wait(*copies)

    return pl.pallas_call(body, name="exchange_grads", in_specs=[ANY] * n, out_specs=[ANY] * n,
                          out_shape=_comm_out_shapes(arrays, kinds), scratch_shapes=_comm_scratch(n))(*arrays)


def _adam_math(w, g, m, v):
    m = ADAM_B1 * m + (1.0 - ADAM_B1) * g
    v = ADAM_B2 * v + (1.0 - ADAM_B2) * (g * g)
    m_hat = m / (1.0 - ADAM_B1 ** ADAM_STEP)
    v_hat = v / (1.0 - ADAM_B2 ** ADAM_STEP)
    delta = -ADAM_LR * (m_hat / (jnp.sqrt(v_hat) + ADAM_EPS) + ADAM_WD * w)
    return delta, m, v


def _adamw(name, w, gslots, m, v, tc):
    r, cc = w.shape
    assert cc % tc == 0
    tile = pl.BlockSpec((r, tc), lambda i: (0, i))

    def body(w_ref, g_ref, m_ref, v_ref, go_ref, d_ref, mo_ref, vo_ref):
        g = g_ref[0].astype(F32)
        for s in range(1, N_DEV):
            g = g + g_ref[s].astype(F32)
        d, mn, vn = _adam_math(w_ref[...], g, m_ref[...], v_ref[...])
        go_ref[...] = g
        d_ref[...] = d
        mo_ref[...] = mn
        vo_ref[...] = vn

    shp = jax.ShapeDtypeStruct((r, cc), F32)
    return pl.pallas_call(body, name=name, grid=(cc // tc,),
                          in_specs=[tile, pl.BlockSpec((N_DEV, r, tc), lambda i: (0, 0, i)), tile, tile],
                          out_specs=(tile,) * 4, out_shape=(shp,) * 4, compiler_params=_cparams(("arbitrary",)))(w, gslots, m, v)


PACK_TILE = 8 * LANES


def _packed_rows(shape, mode):
    r, w = shape
    return -(-r // 8) * 8 if mode == "rows" else -(-(r * w) // PACK_TILE) * 8


def _pack_small(arrays, modes, lead=False):
    out = []
    for a, mode in zip(arrays, modes):
        a = a.astype(F32) if lead else a.astype(F32)[None]
        if mode == "rows":
            out.append(jnp.pad(a, ((0, 0), (0, (-a.shape[1]) % 8), (0, LANES - a.shape[2]))))
        else:
            flat = a.reshape(a.shape[0], -1)
            out.append(jnp.pad(flat, ((0, 0), (0, (-flat.shape[1]) % PACK_TILE))).reshape(a.shape[0], -1, LANES))
    out = jnp.concatenate(out, axis=1)
    return out if lead else out[0]


def _take_small(packed, row0, shape, mode):
    r, w = shape
    lead = packed.ndim == 3
    if mode == "rows":
        return packed[:, row0:row0 + r, :w] if lead else packed[row0:row0 + r, :w]
    per_row = -(-w // LANES)
    if lead:
        return packed[:, row0:row0 + r * per_row].reshape(packed.shape[0], r, per_row * LANES)[:, :, :w]
    rows = []
    for i in range(r):
        pieces = [packed[row0 + i * per_row + j:row0 + i * per_row + j + 1, :] for j in range(per_row)]
        rows.append((pieces[0] if per_row == 1 else jnp.concatenate(pieces, axis=1))[:, :w])
    return rows[0] if r == 1 else jnp.concatenate(rows, axis=0)


def _adamw_small(slots, specs, ws, ms, vs, loss_row):
    n = len(specs)

    def body(*refs):
        slots_ref, w_refs, m_refs, v_refs = refs[0], refs[1:1 + n], refs[1 + n:1 + 2 * n], refs[1 + 2 * n:1 + 3 * n]
        out_refs, loss_ref = refs[1 + 3 * n:1 + 7 * n], refs[1 + 7 * n]
        gp = slots_ref[0]
        for s in range(1, N_DEV):
            gp = gp + slots_ref[s]
        read = lambda ref: ref[0] if len(ref.shape) == 3 else ref[...]
        for k, (shape, mode, row0) in enumerate(specs):
            g = _take_small(gp, row0, shape, mode)
            d, mn, vn = _adam_math(read(w_refs[k]), g, read(m_refs[k]), read(v_refs[k]))
            for ref, val in zip(out_refs[4 * k:4 * k + 4], (g, d, mn, vn)):
                if len(ref.shape) == 3:
                    ref[0] = val
                else:
                    ref[...] = val
        loss_ref[...] = gp[loss_row:loss_row + 1, :]

    vmem = pl.BlockSpec(memory_space=pltpu.VMEM)
    out_shape = [jax.ShapeDtypeStruct(w.shape, F32) for w in ws for _ in range(4)] + [jax.ShapeDtypeStruct((1, LANES), F32)]
    outs = pl.pallas_call(body, name="adamw_small", in_specs=[vmem] * (1 + 3 * n), out_specs=[vmem] * (4 * n + 1),
                          out_shape=out_shape)(slots, *ws, *ms, *vs)
    return [outs[4 * k:4 * k + 4] for k in range(n)], outs[4 * n]


def _rope_tables(t):
    dim = jnp.arange(LANES) % SWA_HD
    inv_freq = ROPE_THETA ** (-(dim % ROPE_HALF).astype(F32) / ROPE_HALF)
    ang = jnp.arange(t, dtype=F32)[:, None] * jnp.where(dim < 2 * ROPE_HALF, inv_freq, 0.0)[None, :]
    return jnp.cos(ang), jnp.sin(ang)


def _pad_to(a, rows=None, cols=None):
    r = 0 if rows is None else rows - a.shape[0]
    c = 0 if cols is None else cols - a.shape[1]
    return jnp.pad(a, ((0, r), (0, c)))


ORIG0 = dict(gq=(0, 256), gk=(256, 256), gv=(512, 512), glow=(1024, 16), r=(1040, 512), k=(1552, 512), v=(2064, 512),
             xw=(2576, 64), xa=(2640, 64), gate=(2704, 1024))
ORIG0_ORDER = ["gq", "gk", "gv", "glow", "r", "k", "v", "xw", "xa", "gate"]


def _w0t_to_padded(wt):
    rows, at = [], 0
    for name, (off, width) in sorted(C0.items(), key=lambda kv: kv[1][0]):
        assert off == at
        src, src_w = ORIG0[name]
        rows.append(_pad_to(wt[src:src + src_w], rows=width))
        at += width
    rows.append(jnp.zeros((N0P - at, wt.shape[1]), wt.dtype))
    return jnp.concatenate(rows, axis=0)


def _w0t_from_padded(wpt):
    return jnp.concatenate([wpt[C0[n][0]:C0[n][0] + ORIG0[n][1]] for n in ORIG0_ORDER], axis=0)


def _w1t_to_mine(wt):
    return jnp.concatenate([wt[1536:2560], wt[:1536]], axis=0)


def _w1t_from_mine(wt):
    return jnp.concatenate([wt[1024:2560], wt[:1024]], axis=0)


def kernel(x, norm_w, w_in0, gla_gk_up, gla_gk_bias, gla_norm_w, rwkv_mu, rwkv_w0, rwkv_w_up, rwkv_a0, rwkv_a_up, rwkv_k_k, rwkv_k_a, rwkv_r_k, rwkv_ln_w, rwkv_ln_b, w_out0, w_in1, b_in1, attn_sinks, w_out1, b_out1, final_norm_w, loss_target, m_norm_w, m_w_in0, m_gla_gk_up, m_gla_gk_bias, m_gla_norm_w, m_rwkv_mu, m_rwkv_w0, m_rwkv_w_up, m_rwkv_a0, m_rwkv_a_up, m_rwkv_k_k, m_rwkv_k_a, m_rwkv_r_k, m_rwkv_ln_w, m_rwkv_ln_b, m_w_out0, m_w_in1, m_b_in1, m_attn_sinks, m_w_out1, m_b_out1, m_final_norm_w, v_norm_w, v_w_in0, v_gla_gk_up, v_gla_gk_bias, v_gla_norm_w, v_rwkv_mu, v_rwkv_w0, v_rwkv_w_up, v_rwkv_a0, v_rwkv_a_up, v_rwkv_k_k, v_rwkv_k_a, v_rwkv_r_k, v_rwkv_ln_w, v_rwkv_ln_b, v_w_out0, v_w_in1, v_b_in1, v_attn_sinks, v_w_out1, v_b_out1, v_final_norm_w):
    weights = dict(norm_w=norm_w, w_in0=w_in0, gla_gk_up=gla_gk_up, gla_gk_bias=gla_gk_bias, gla_norm_w=gla_norm_w, rwkv_mu=rwkv_mu,
                   rwkv_w0=rwkv_w0, rwkv_w_up=rwkv_w_up, rwkv_a0=rwkv_a0, rwkv_a_up=rwkv_a_up, rwkv_k_k=rwkv_k_k, rwkv_k_a=rwkv_k_a,
                   rwkv_r_k=rwkv_r_k, rwkv_ln_w=rwkv_ln_w, rwkv_ln_b=rwkv_ln_b, w_out0=w_out0, w_in1=w_in1, b_in1=b_in1,
                   attn_sinks=attn_sinks, w_out1=w_out1, b_out1=b_out1, final_norm_w=final_norm_w)
    moms = dict(norm_w=m_norm_w, w_in0=m_w_in0, gla_gk_up=m_gla_gk_up, gla_gk_bias=m_gla_gk_bias, gla_norm_w=m_gla_norm_w,
                rwkv_mu=m_rwkv_mu, rwkv_w0=m_rwkv_w0, rwkv_w_up=m_rwkv_w_up, rwkv_a0=m_rwkv_a0, rwkv_a_up=m_rwkv_a_up,
                rwkv_k_k=m_rwkv_k_k, rwkv_k_a=m_rwkv_k_a, rwkv_r_k=m_rwkv_r_k, rwkv_ln_w=m_rwkv_ln_w, rwkv_ln_b=m_rwkv_ln_b,
                w_out0=m_w_out0, w_in1=m_w_in1, b_in1=m_b_in1, attn_sinks=m_attn_sinks, w_out1=m_w_out1, b_out1=m_b_out1,
                final_norm_w=m_final_norm_w)
    vars_ = dict(norm_w=v_norm_w, w_in0=v_w_in0, gla_gk_up=v_gla_gk_up, gla_gk_bias=v_gla_gk_bias, gla_norm_w=v_gla_norm_w,
                 rwkv_mu=v_rwkv_mu, rwkv_w0=v_rwkv_w0, rwkv_w_up=v_rwkv_w_up, rwkv_a0=v_rwkv_a0, rwkv_a_up=v_rwkv_a_up,
                 rwkv_k_k=v_rwkv_k_k, rwkv_k_a=v_rwkv_k_a, rwkv_r_k=v_rwkv_r_k, rwkv_ln_w=v_rwkv_ln_w, rwkv_ln_b=v_rwkv_ln_b,
                 w_out0=v_w_out0, w_in1=v_w_in1, b_in1=v_b_in1, attn_sinks=v_attn_sinks, w_out1=v_w_out1, b_out1=v_b_out1,
                 final_norm_w=v_final_norm_w)
    names = list(weights)
    big = ["w_in0", "w_out0", "w_in1", "w_out1"]
    small_sharded = ["gla_gk_up", "rwkv_w_up", "rwkv_a_up", "b_in1", "b_out1"]
    replicated = [n for n in names if n not in big and n not in small_sharded]

    xs = x[0]
    tgt = loss_target[0]
    t = xs.shape[0]

    def view(w):
        shape = tuple(w.shape[-2:]) if w.ndim >= 2 else (1, w.shape[0])
        return shape, ("rows" if shape[0] > 1 and shape[1] <= LANES else "flat")

    def layout(ns, row0=0):
        specs = []
        for n in ns:
            shape, mode = view(weights[n])
            specs.append((shape, mode, row0))
            row0 += _packed_rows(shape, mode)
        return specs, row0

    sh_specs, n_shard_rows = layout(small_sharded)
    rep_specs, loss_row = layout(replicated, n_shard_rows)
    sh_modes, rep_modes = [s[1] for s in sh_specs], [s[1] for s in rep_specs]

    small_shard_pack = _pack_small([weights[n].reshape(view(weights[n])[0]) for n in small_sharded], sh_modes)
    g_in0, g_small = _all_gather([w_in0[0].T.astype(BF16), small_shard_pack])
    w0t = _w0t_to_padded(g_in0.reshape(-1, D_MODEL))
    later_shards = [w_out0[0].astype(BF16), w_in1[0].T.astype(BF16), w_out1[0].astype(BF16)]
    gs = [_take_small(g_small, row0, shape, mode) for shape, mode, row0 in sh_specs]
    join_cols = lambda a: jnp.transpose(a, (1, 0, 2)).reshape(a.shape[1], -1)
    gk_up, w_up, a_up = join_cols(gs[0]), join_cols(gs[1]), join_cols(gs[2])
    b_in, b_out = gs[3].reshape(1, -1), gs[4].reshape(1, -1)

    gk_up_p = _pad_to(gk_up, rows=128)
    mu = rwkv_mu
    rwkv_params = [mu[:, 0:512], mu[:, 512:1024], mu[:, 1024:1536], _pad_to(mu[:, 1536:1600], cols=128), _pad_to(mu[:, 1600:1664], cols=128),
                   rwkv_w0, _pad_to(w_up, rows=128), rwkv_a0, _pad_to(a_up, rows=128), rwkv_k_k, rwkv_k_a, rwkv_r_k.reshape(1, 512),
                   rwkv_ln_w, rwkv_ln_b]
    bq, bk, bv = b_in[:, :1024], b_in[:, 1024:1280], b_in[:, 1280:1536]
    cos, sin = _rope_tables(t)
    nw0, nw1, fw = norm_w[0:1], norm_w[1:2], final_norm_w.reshape(1, D_MODEL)

    d = D_MODEL
    wide = lambda arr: (arr, d, 0)
    silu = lambda g: g * sigmoid(g)
    hn0, proj0 = _matmul_fused("norm0_proj0", rms, w0t, "nt", [wide(xs)], [nw0], [(N0P, F32)], [], lambda acc, x, w: (acc,))
    o_a, gla_states = _gla_fwd(proj0, gk_up_p, gla_gk_bias, gla_norm_w)
    o_b, rwkv_states, rwkv_prevs, (g_out0, g_in1, g_out1) = _rwkv_fwd(proj0, rwkv_params, later_shards, ["gather"] * 3)
    wo0 = g_out0.reshape(1024, D_MODEL)
    w1t = _w1t_to_mine(g_in1.reshape(-1, D_MODEL))
    wo1 = g_out1.reshape(1024, D_MODEL)
    og0, h1, hn1 = _matmul_fused(
        "gate0_out0_norm1", lambda oa, ob, gate, x, w: jnp.concatenate([oa, ob], axis=1) * silu(gate), wo0, "nn",
        [(o_a, 512, 0), (o_b, 512, 0), wide(proj0), wide(xs)], [nw1], [(d, F32), (d, BF16)], [],
        lambda acc, oa, ob, gate, x, w: _resid_norm(acc, x, w))
    proj1 = _matmul("proj1", hn1, w1t, "nt", 1024, 1280)
    o_c, kst, vst = _swa_fwd(proj1, cos, sin, bq, bk, bv, attn_sinks)
    og1, dh2, loss_part, d_b_out, d_fw = _matmul_fused(
        "gate1_out1_loss", lambda oc, gate, h, tg, b, w: oc * silu(gate), wo1, "nn",
        [wide(o_c), wide(proj1), wide(h1), wide(tgt)], [b_out, fw], [(d, F32)], [LANES, d, d],
        lambda acc, oc, gate, h, tg, b, w: _loss_head(acc, h, tg, b, w))

    d_oc, d_gate1 = _matmul_fused("out1_dx_gate1", dh2, wo1, "nt", [wide(o_c), wide(proj1)], [], [(d, F32), (d, BF16)], [], _gate_back)
    d_wo1 = _matmul("out1_dw", og1, dh2, "tn", 512, 512, BF16)
    dq, dk, dv, d_bq, d_bk, d_bv, d_sinks = _swa_bwd(proj1, cos, sin, bq, bk, bv, attn_sinks, kst, vst, d_oc)
    dproj1 = jnp.concatenate([d_gate1, dq, dk, dv], axis=1)
    dh1, d_nw1 = _matmul_fused("proj1_dx_norm1", dproj1, w1t, "nn", [wide(h1), wide(dh2)], [nw1], [(d, F32)], [d], _norm_back)
    d_w1t = _matmul("proj1_dw", dproj1, hn1, "tn", 512, 1024, BF16)
    d_oa, d_ob, d_gate0 = _matmul_fused("out0_dx_gate0", dh1, wo0, "nt", [(o_a, 512, 0), (o_b, 512, 0), wide(proj0)], [],
                                        [(512, F32), (512, F32), (d, BF16)], [], _gate_back)
    d_wo0 = _matmul("out0_dw", og0, dh1, "tn", 512, 512, BF16)
    dgq, dgk, dgv, dglow, d_gk_up, d_gk_bias, d_gla_nw = _gla_bwd(proj0, gk_up_p, gla_gk_bias, gla_norm_w, gla_states, d_oa)
    row_blocks = lambda a: a.astype(BF16).reshape(N_DEV, -1, D_MODEL)
    early = [row_blocks(_w1t_from_mine(d_w1t)), row_blocks(d_wo1), row_blocks(d_wo0)]
    (dr, dkk, dvv, dxw, dxa), d_rp, (r_in1, r_out1, r_out0) = _rwkv_bwd(
        proj0, rwkv_params, rwkv_states, rwkv_prevs, d_ob, early, ["scatter"] * 3)
    dproj0 = jnp.concatenate([d_gate0, dgv, dr, dkk, dvv, dgq, dgk, dglow, dxw, dxa, jnp.zeros((t, 128), BF16)], axis=1)
    d_w0 = _w0t_from_padded(_matmul("proj0_dw", dproj0, hn0, "tn", 512, 1024, BF16))
    grad_x, d_nw0, r_in0 = _matmul_fused("proj0_dx_norm0", dproj0, w0t, "nn", [wide(xs), wide(dh1)], [nw0], [(d, F32)], [d],
                                         _norm_back, [row_blocks(d_w0)], ["scatter"])

    contrib = dict(
        norm_w=jnp.concatenate([d_nw0, d_nw1], axis=0), gla_gk_bias=d_gk_bias, gla_norm_w=d_gla_nw,
        rwkv_mu=jnp.concatenate([d_rp[0], d_rp[1], d_rp[2], d_rp[3][:, :64], d_rp[4][:, :64]], axis=1),
        rwkv_w0=d_rp[5], rwkv_a0=d_rp[7], rwkv_k_k=d_rp[9], rwkv_k_a=d_rp[10], rwkv_r_k=d_rp[11].reshape(RWKV_HEADS, RWKV_N),
        rwkv_ln_w=d_rp[12], rwkv_ln_b=d_rp[13], attn_sinks=d_sinks, final_norm_w=d_fw)
    rep_pack = _pack_small([contrib[n] for n in replicated] + [loss_part[:, :1]], rep_modes + ["flat"])

    d_b_in = jnp.concatenate([d_bq, d_bk, d_bv], axis=1)
    full_small = [d_gk_up[:16], d_rp[6][:64], d_rp[8][:64], d_b_in, d_b_out]
    split_cols = lambda a: jnp.transpose(a.reshape(a.shape[0], N_DEV, -1), (1, 0, 2))
    small_parts = [split_cols(a) for a in full_small]
    small_pack = _pack_small(small_parts, sh_modes, lead=True)
    r_small, r_rep = _exchange([small_pack, rep_pack], ["scatter", "gather"])

    res = {}
    res["w_in0"] = tuple(a.T[None] for a in _adamw("adamw_w_in0", w_in0[0].T, r_in0, m_w_in0[0].T, v_w_in0[0].T, 256))
    res["w_out0"] = tuple(a[None] for a in _adamw("adamw_w_out0", w_out0[0], r_out0, m_w_out0[0], v_w_out0[0], 256))
    res["w_in1"] = tuple(a.T[None] for a in _adamw("adamw_w_in1", w_in1[0].T, r_in1, m_w_in1[0].T, v_w_in1[0].T, 256))
    res["w_out1"] = tuple(a[None] for a in _adamw("adamw_w_out1", w_out1[0], r_out1, m_w_out1[0], v_w_out1[0], 256))
    small_names = small_sharded + replicated
    slots = jnp.concatenate([r_small, r_rep], axis=1)
    as_2d = lambda a: a.reshape(1, -1) if a.ndim == 1 else a
    small_res, loss_row_out = _adamw_small(slots, sh_specs + rep_specs, [as_2d(weights[n]) for n in small_names],
                                           [as_2d(moms[n]) for n in small_names], [as_2d(vars_[n]) for n in small_names], loss_row)
    for n, vals in zip(small_names, small_res):
        res[n] = tuple(val.reshape(weights[n].shape) for val in vals)
    loss = loss_row_out[0, 0]
    return (loss, grad_x[None], *[res[n][0] for n in names], *[res[n][1] for n in names],
            *[res[n][2] for n in names], *[res[n][3] for n in names])
```

```python
import functools

import jax
import jax.numpy as jnp
from jax import lax
from jax.experimental import pallas as pl
from jax.experimental.pallas import tpu as pltpu

F32 = jnp.float32
BF16 = jnp.bfloat16
HI = lax.Precision.HIGHEST

D_MODEL = 1024
NORM_EPS = 1e-5
GLA_HEADS, GLA_DK, GLA_DV = 4, 64, 128
GLA_NORMALIZER = 16.0
GLA_CHUNK = 64
GLA_STEP = 512
RWKV_HEADS, RWKV_N = 8, 64
RWKV_LN_EPS = 64e-5
RWKV_CHUNK = 128
SWA_Q_HEADS, SWA_KV_HEADS, SWA_GROUP, SWA_HD = 16, 4, 4, 64
WINDOW = 128
SWA_STEP = 512
ROPE_THETA = 500000.0
NEG = -1e30
N_DEV = 8
LANES = 128

ADAM_LR, ADAM_B1, ADAM_B2, ADAM_EPS, ADAM_WD, ADAM_STEP = 0.001, 0.9, 0.999, 1e-08, 0.01, 10

GLA_KEY, GLA_VAL = GLA_HEADS * GLA_DK, GLA_HEADS * GLA_DV
RWKV_W = RWKV_HEADS * RWKV_N
SWA_KV = SWA_KV_HEADS * SWA_HD
MIX = GLA_VAL + RWKV_W
LOW = LANES

N0P = 4096
C0 = dict(gate=(0, MIX), gv=(1024, GLA_VAL), r=(1536, RWKV_W), k=(2048, RWKV_W), v=(2560, RWKV_W), gq=(3072, GLA_KEY),
          gk=(3328, GLA_KEY), glow=(3584, LOW), xw=(3712, LOW), xa=(3840, LOW))
N1P = 2560
C1 = dict(gate=(0, MIX), q=(1024, MIX), k=(2048, SWA_KV), v=(2304, SWA_KV))

VMEM_LIMIT = 56 * 1024 * 1024

P_LORA = 1
P_GLA = 1
P_RWKV_G = 2
P_RWKV = 1
P_SWA = 1


def _cparams(sem=None):
    return pltpu.CompilerParams(dimension_semantics=sem, vmem_limit_bytes=VMEM_LIMIT)


DIMS = dict(nn=(((1,), (0,)), ((), ())), nt=(((1,), (1,)), ((), ())), tn=(((0,), (0,)), ((), ())))


def _split_bf16(a):
    hi = a.astype(BF16)
    return hi, (a - hi.astype(F32)).astype(BF16)


def _dot(a, b, mode, passes):
    dg = lambda p, q: lax.dot_general(p, q, DIMS[mode], preferred_element_type=F32)
    if passes == 1:
        return dg(a.astype(BF16), b.astype(BF16))
    if passes == 2:
        ah, (bh, bl) = a.astype(BF16), _split_bf16(b)
        return dg(ah, bh) + dg(ah, bl)
    if passes == 3:
        (ah, al), (bh, bl) = _split_bf16(a), _split_bf16(b)
        return dg(ah, bh) + dg(al, bh) + dg(ah, bl)
    return lax.dot_general(a, b, DIMS[mode], precision=HI, preferred_element_type=F32)


@functools.partial(jax.custom_vjp, nondiff_argnums=(2, 3))
def mmx(a, b, mode, passes):
    return _dot(a, b, mode, passes)


def _mmx_fwd(a, b, mode, passes):
    return _dot(a, b, mode, passes), (a, b)


def _mmx_bwd(mode, passes, res, g):
    a, b = res
    if mode == "nn":
        return _dot(g, b, "nt", passes), _dot(a, g, "tn", passes)
    if mode == "nt":
        return _dot(g, b, "nn", passes), _dot(g, a, "tn", passes)
    return _dot(b, g, "nt", passes), _dot(a, g, "nn", passes)


mmx.defvjp(_mmx_fwd, _mmx_bwd)


def _tri_dot(tri, x):
    t = tri.astype(BF16)
    x1 = x.astype(BF16)
    r1 = x - x1.astype(F32)
    x2 = r1.astype(BF16)
    x3 = (r1 - x2.astype(F32)).astype(BF16)
    dg = lambda q: jnp.dot(t, q, preferred_element_type=F32)
    return dg(x1) + dg(x2) + dg(x3)


@jax.custom_vjp
def cumsum_rows(x):
    return _tri_dot(tril_ones(x.shape[0]), x)


def _cumsum_fwd(x):
    return cumsum_rows(x), None


def _cumsum_bwd(_, g):
    i, j = _iota2(g.shape[0], g.shape[0])
    return (_tri_dot(jnp.where(i <= j, 1.0, 0.0).astype(F32), g),)


cumsum_rows.defvjp(_cumsum_fwd, _cumsum_bwd)


def _head_dot(x):
    i, j = _iota2(LANES, LANES)
    shift = RWKV_N.bit_length() - 1
    same = jnp.where(jnp.right_shift(i, shift) == jnp.right_shift(j, shift), 1.0, 0.0).astype(F32)
    return jnp.concatenate([_ones_right(x[:, g * LANES:(g + 1) * LANES], same) for g in range(x.shape[1] // LANES)], axis=1)


def _ones_right(x, ones):
    t = ones.astype(BF16)
    x1 = x.astype(BF16)
    r1 = x - x1.astype(F32)
    x2 = r1.astype(BF16)
    x3 = (r1 - x2.astype(F32)).astype(BF16)
    dg = lambda q: jnp.dot(q, t, preferred_element_type=F32)
    return dg(x1) + dg(x2) + dg(x3)


@jax.custom_vjp
def head_sum(x):
    return _head_dot(x)


def _head_sum_fwd(x):
    return head_sum(x), None


def _head_sum_bwd(_, g):
    return (_head_dot(g),)


head_sum.defvjp(_head_sum_fwd, _head_sum_bwd)


def cat_rows(*xs):
    return jnp.concatenate(xs, axis=0)


def _iota2(n, m):
    return lax.broadcasted_iota(jnp.int32, (n, m), 0), lax.broadcasted_iota(jnp.int32, (n, m), 1)


def tril_ones(c, strict=False):
    i, j = _iota2(c, c)
    return jnp.where((i > j) if strict else (i >= j), 1.0, 0.0).astype(F32)


def row_of(x, r):
    i = lax.broadcasted_iota(jnp.int32, x.shape, 0)
    return jnp.sum(jnp.where(i == r, x, 0.0), axis=0, keepdims=True)


@jax.custom_vjp
def shift_rows(x, prev):
    r = lax.broadcasted_iota(jnp.int32, x.shape, 0)
    return jnp.where(r == 0, prev, pltpu.roll(x, 1, 0))


def _shift_fwd(x, prev):
    return shift_rows(x, prev), None


def _shift_bwd(_, g):
    c = g.shape[0]
    r = lax.broadcasted_iota(jnp.int32, g.shape, 0)
    return jnp.where(r == c - 1, 0.0, pltpu.roll(g, c - 1, 0)), row_of(g, 0)


shift_rows.defvjp(_shift_fwd, _shift_bwd)


def log_sigmoid(x):
    return jnp.minimum(x, 0.0) - jnp.log(1.0 + jnp.exp(-jnp.abs(x)))


def softplus(x):
    return jnp.maximum(x, 0.0) + jnp.log(1.0 + jnp.exp(-jnp.abs(x)))


def sigmoid(x):
    return 1.0 / (1.0 + jnp.exp(-x))


def rms(x, w, eps=NORM_EPS):
    return x * lax.rsqrt(jnp.mean(x * x, axis=-1, keepdims=True) + eps) * w


def gla_chunk(state, toks, params):
    q, k, v, glow = toks
    gk_up, bias, norm_w = params
    c = GLA_CHUNK
    subs, heads = range(glow.shape[0] // c), range(GLA_HEADS)
    rows = lambda x, j: x[j * c:(j + 1) * c]
    hk = lambda x, h: x[:, h * GLA_DK:(h + 1) * GLA_DK]
    hv = lambda x, h: x[:, h * GLA_DV:(h + 1) * GLA_DV]
    ltri = tril_ones(c)
    g = log_sigmoid(mmx(glow, gk_up, "nn", P_LORA) + bias) / GLA_NORMALIZER
    b = [cumsum_rows(rows(g, j)) for j in subs]
    ref = [lax.stop_gradient(row_of(b[j], c // 2)) for j in subs]
    last = [row_of(b[j], c - 1) for j in subs]
    ql = [rows(q, j) * (GLA_DK ** -0.5) * jnp.exp(b[j] - ref[j]) for j in subs]
    kr = [rows(k, j) * jnp.exp(ref[j] - b[j]) for j in subs]
    kl = [rows(k, j) * jnp.exp(last[j] - b[j]) for j in subs]
    vj = [rows(v, j) for j in subs]
    e_ref, e_last = [jnp.exp(x) for x in ref], [jnp.exp(x) for x in last]
    att = [[mmx(hk(ql[j], h), hk(kr[j], h), "nt", P_GLA) * ltri for h in heads] for j in subs]
    o_in = [[mmx(att[j][h], hv(vj[j], h), "nn", P_GLA) for h in heads] for j in subs]
    kv = [[mmx(hv(vj[j], h), hk(kl[j], h), "tn", P_GLA) for h in heads] for j in subs]
    o = []
    for j in subs:
        o.append([o_in[j][h] + mmx(hk(ql[j], h), state[h] * hk(e_ref[j], h), "nt", P_GLA) for h in heads])
        state = [state[h] * hk(e_last[j], h) + kv[j][h] for h in heads]
    o = [[x * lax.rsqrt(jnp.mean(x * x, axis=-1, keepdims=True) + NORM_EPS) * norm_w for x in oj] for oj in o]
    return cat_rows(*[jnp.concatenate(oj, axis=1) for oj in o]), state


SOLVE_BLOCK = 128


def solve_unit_lower(ps, ws):
    n = ps[0].shape[0]
    heads = range(len(ps))
    if n > SOLVE_BLOCK:
        half = n // 2
        top = solve_unit_lower([p[:half, :half] for p in ps], [w[:half] for w in ws])
        rest = [ws[h][half:] + mmx(ps[h][half:, :half], top[h], "nn", P_RWKV) for h in heads]
        bottom = solve_unit_lower([p[half:, half:] for p in ps], rest)
        return [cat_rows(top[h], bottom[h]) for h in heads]
    u, p = ws, ps
    levels = max(1, (n - 1).bit_length())
    for it in range(levels):
        if it + 1 < levels:
            y = [mmx(p[h], jnp.concatenate([p[h], u[h]], axis=1), "nn", P_RWKV) for h in heads]
            u = [u[h] + y[h][:, n:] for h in heads]
            p = [y[h][:, :n] for h in heads]
        else:
            u = [u[h] + mmx(p[h], u[h], "nn", P_RWKV) for h in heads]
    return u


def rwkv_chunk(state, toks, params):
    S, pr, pk, pv, pxw, pxa = state
    r_, k_, v_, xw_, xa_ = toks
    mu_r, mu_k, mu_v, mu_xw, mu_xa, w0, w_up, a0, a_up, k_k, k_a, r_k, ln_w, ln_b = params
    c, n = xw_.shape[0], RWKV_N
    heads = range(RWKV_HEADS)
    hs = lambda x, h: x[:, h * n:(h + 1) * n]
    ltri = tril_ones(c)
    stri = tril_ones(c, strict=True)

    def lerp(x, prev, mu):
        return x + (shift_rows(x, prev) - x) * mu

    xw = jnp.tanh(lerp(xw_, pxw, mu_xw))
    xa = lerp(xa_, pxa, mu_xa)
    r = lerp(r_, pr, mu_r)
    k = lerp(k_, pk, mu_k)
    v = lerp(v_, pv, mu_v)
    w = -softplus(-(w0 + mmx(xw, w_up, "nn", P_LORA))) - 0.5
    lw = -jnp.exp(w)
    asig = sigmoid(a0 + mmx(xa, a_up, "nn", P_LORA))
    kk = k * k_k
    kk = kk * lax.rsqrt(jnp.maximum(head_sum(kk * kk), 1e-24))
    k2 = k * (1.0 + (asig - 1.0) * k_a)
    b = kk * asig
    cum = cumsum_rows(lw)
    ref = lax.stop_gradient(row_of(cum, c // 2))
    last = row_of(cum, c - 1)
    at = -kk * jnp.exp(cum - lw - ref)
    rt = r * jnp.exp(cum - ref)
    e_out = jnp.exp(ref - cum)
    bt, kt = b * e_out, k2 * e_out
    e_tail = jnp.exp(last - cum)
    bl, kl = b * e_tail, k2 * e_tail
    e_ref, e_last = jnp.exp(ref), jnp.exp(last)
    g = [mmx(cat_rows(hs(at, h), hs(rt, h)), cat_rows(hs(bt, h), hs(kt, h), S[h] * hs(e_ref, h)), "nt", P_RWKV_G) for h in heads]
    aab = [x[:c, :c] * stri for x in g]
    aak = [x[:c, c:2 * c] * stri for x in g]
    arb = [x[c:, :c] * ltri for x in g]
    ark = [x[c:, c:2 * c] * ltri for x in g]
    av = [mmx(cat_rows(aak[h], ark[h]), hs(v, h), "nn", P_RWKV) for h in heads]
    u = solve_unit_lower(aab, [g[h][:c, 2 * c:] + av[h][:c] for h in heads])
    o = [g[h][c:, 2 * c:] + av[h][c:] + mmx(arb[h], u[h], "nn", P_RWKV) for h in heads]
    s1 = [S[h] * hs(e_last, h) + mmx(cat_rows(u[h], hs(v, h)), cat_rows(hs(bl, h), hs(kl, h)), "tn", P_RWKV) for h in heads]
    o = jnp.concatenate(o, axis=1)
    d = o - head_sum(o) * (1.0 / n)
    var = head_sum(d * d) * (1.0 / n)
    o = d * lax.rsqrt(var + RWKV_LN_EPS) * ln_w + ln_b + head_sum(r * k2 * r_k) * v
    new_state = (s1, row_of(r_, c - 1), row_of(k_, c - 1), row_of(v_, c - 1), row_of(xw_, c - 1), row_of(xa_, c - 1))
    return o, new_state


ROPE_HALF = 8


def _rot_half_raw(x):
    lane = lax.broadcasted_iota(jnp.int32, (x.shape[0], LANES), 1) & (SWA_HD - 1)
    out = []
    for i in range(x.shape[1] // LANES):
        g = x[:, i * LANES:(i + 1) * LANES]
        up, down = pltpu.roll(g, LANES - ROPE_HALF, 1), pltpu.roll(g, ROPE_HALF, 1)
        out.append(jnp.where(lane < ROPE_HALF, -up, jnp.where(lane < 2 * ROPE_HALF, down, 0.0)))
    return out[0] if len(out) == 1 else jnp.concatenate(out, axis=1)


@jax.custom_vjp
def rot_half(x):
    return _rot_half_raw(x)


rot_half.defvjp(lambda x: (_rot_half_raw(x), None), lambda _, g: (-_rot_half_raw(g),))


def rope(x, cos2, sin2):
    reps = x.shape[1] // LANES
    tile = lambda t: t if reps == 1 else jnp.concatenate([t] * reps, axis=1)
    return x * tile(cos2) + rot_half(x) * tile(sin2)


def swa_chunk(state, toks, params, first):
    kprev, vprev = state
    q_, k_, v_, cos, sin = toks
    bq, bk, bv, sinks = params
    c, ng = WINDOW, SWA_GROUP
    n_sub = cos.shape[0] // c
    units = [(j, g) for j in range(n_sub) for g in range(SWA_KV_HEADS)]
    rows = lambda x, j: x[j * c:(j + 1) * c]
    hs = lambda g: range(g * ng, (g + 1) * ng)
    head = lambda x, h: x[:, h * SWA_HD:(h + 1) * SWA_HD]
    qi, kj = _iota2(ng * c, 2 * c)
    qpos = qi & (c - 1)
    cur_ok = (kj >= c) & (qpos >= kj - c)
    prev_ok = (kj < c) & (kj > qpos)
    ok = [cur_ok | (prev_ok & jnp.logical_not(first))] + [cur_ok | prev_ok] * (n_sub - 1)
    q_all = rope(q_ + bq, cos, sin) * (SWA_HD ** -0.5)
    k_all = rope(k_ + bk, cos, sin)
    v_all = v_ + bv
    k = {(j, g): rows(head(k_all, g), j) for j, g in units}
    v = {(j, g): rows(head(v_all, g), j) for j, g in units}
    q = {(j, g): cat_rows(*[rows(head(q_all, h), j) for h in hs(g)]) for j, g in units}
    kp = lambda j, g: kprev[g] if j == 0 else k[(j - 1, g)]
    vp = lambda j, g: vprev[g] if j == 0 else v[(j - 1, g)]
    s = {(j, g): jnp.where(ok[j], mmx(q[(j, g)], cat_rows(kp(j, g), k[(j, g)]), "nt", P_SWA), NEG) for j, g in units}
    sink = [cat_rows(*[jnp.broadcast_to(sinks[h], (c, 1)) for h in hs(g)]) for g in range(SWA_KV_HEADS)]
    m = {(j, g): lax.stop_gradient(jnp.maximum(jnp.max(s[(j, g)], axis=-1, keepdims=True), sink[g])) for j, g in units}
    p = {u: jnp.exp(s[u] - m[u]) for u in units}
    ones = jnp.ones((2 * c, SWA_HD), F32)
    pv = {(j, g): mmx(p[(j, g)], cat_rows(vp(j, g), v[(j, g)]), "nn", P_SWA) for j, g in units}
    den = {u: mmx(p[u], ones, "nn", P_SWA) for u in units}
    o = {(j, g): pv[(j, g)] / (den[(j, g)] + jnp.exp(sink[g] - m[(j, g)])) for j, g in units}
    outs = [cat_rows(*[o[(j, g)][i * c:(i + 1) * c] for j in range(n_sub)]) for g in range(SWA_KV_HEADS) for i in range(ng)]
    last = n_sub - 1
    return outs, ([k[(last, g)] for g in range(SWA_KV_HEADS)], [v[(last, g)] for g in range(SWA_KV_HEADS)])


def _heads(ref, n, w, rows=slice(None)):
    return [ref[rows, h * w:(h + 1) * w] for h in range(n)]


def _put_heads(ref, vals, w, rows=slice(None), add=False):
    for h, val in enumerate(vals):
        if add:
            ref[rows, h * w:(h + 1) * w] += val
        else:
            ref[rows, h * w:(h + 1) * w] = val


def _col(block_w, name, table):
    off, w = table[name]
    assert off % block_w == 0 and w % block_w == 0
    return off // block_w


def _tok_spec(c, w, colblock, n=None):
    if n is None:
        return pl.BlockSpec((c, w), lambda i: (i, colblock))
    return pl.BlockSpec((c, w), lambda i: (n - 1 - i, colblock))


def _full_spec(shape):
    return pl.BlockSpec(shape, lambda i: (0,) * len(shape))


def _matmul(name, a, b, mode, tm, tn, out_dtype=F32):
    (m, kd) = (a.shape[1], a.shape[0]) if mode == "tn" else a.shape
    n = b.shape[0] if mode == "nt" else b.shape[1]
    assert m % tm == 0 and n % tn == 0
    a_spec = pl.BlockSpec((kd, tm), lambda j, i: (0, i)) if mode == "tn" else pl.BlockSpec((tm, kd), lambda j, i: (i, 0))
    b_spec = pl.BlockSpec((tn, kd), lambda j, i: (j, 0)) if mode == "nt" else pl.BlockSpec((kd, tn), lambda j, i: (0, j))

    def body(a_ref, b_ref, o_ref):
        o_ref[...] = lax.dot_general(a_ref[...].astype(BF16), b_ref[...].astype(BF16), DIMS[mode],
                                     preferred_element_type=F32).astype(out_dtype)

    return pl.pallas_call(
        body, name=name, grid=(n // tn, m // tm), in_specs=[a_spec, b_spec],
        out_specs=pl.BlockSpec((tm, tn), lambda j, i: (i, j)), out_shape=jax.ShapeDtypeStruct((m, n), out_dtype),
        compiler_params=_cparams(("arbitrary", "arbitrary")))(a, b)


TOK_TILE = 512
PROJ_ROWS = 1024
DW_COLS = 512
ADAM_COLS = 256


def _matmul_fused(name, a, b, mode, tiles, rows, outs, sums, epilogue, comm=(), kinds=()):
    made = callable(a)
    m = tiles[0][0].shape[0] if made else a.shape[0]
    kd = b.shape[0] if mode == "nn" else b.shape[1]
    n = b.shape[1] if mode == "nn" else b.shape[0]
    tm = TOK_TILE
    steps = m // tm
    if made:
        outs = [(kd, BF16)] + list(outs)
    nt_, nr, no, ns, ncomm = len(tiles), len(rows), len(outs), len(sums), len(comm)

    def body(*refs):
        at = 1 if made else 2
        b_ref = refs[at - 1]
        tile_refs, row_refs, comm_in = refs[at:at + nt_], refs[at + nt_:at + nt_ + nr], refs[at + nt_ + nr:at + nt_ + nr + ncomm]
        at += nt_ + nr + ncomm
        out_refs, sum_refs, comm_out = refs[at:at + no], refs[at + no:at + no + ns], refs[at + no + ns:at + no + ns + ncomm]
        sems = refs[at + no + ns + ncomm:]
        i = pl.program_id(0)

        @pl.when(i == 0)
        def _():
            if ncomm:
                _comm_start(*_comm_copies(comm_in, comm_out, kinds, *sems))
            for ref in sum_refs:
                ref[...] = jnp.zeros_like(ref)

        extras = [r[...] for r in tile_refs] + [r[...] for r in row_refs]
        a_blk = (a(*extras) if made else refs[0][...]).astype(BF16)
        acc = lax.dot_general(a_blk, b_ref[...].astype(BF16), DIMS[mode], preferred_element_type=F32)
        res = epilogue(acc, *extras)
        if made:
            res = (a_blk,) + tuple(res)
        for ref, val in zip(out_refs, res[:no]):
            ref[...] = val.astype(ref.dtype)
        for ref, val in zip(sum_refs, res[no:]):
            ref[...] += val

        if ncomm:
            @pl.when(i == steps - 1)
            def _():
                _comm_wait(*_comm_copies(comm_in, comm_out, kinds, *sems))

    in_specs = ([] if made else [pl.BlockSpec((tm, kd), lambda i: (i, 0))]) + [_full_spec(b.shape)]
    in_specs += [pl.BlockSpec((tm, w), functools.partial(lambda i, cb: (i, cb), cb=cb)) for _, w, cb in tiles]
    in_specs += [_full_spec(r.shape) for r in rows] + [ANY] * ncomm
    out_specs = [pl.BlockSpec((tm, w), lambda i: (i, 0)) for w, _ in outs] + [_full_spec((1, w)) for w in sums] + [ANY] * ncomm
    out_shape = ([jax.ShapeDtypeStruct((m, w), dt) for w, dt in outs] + [jax.ShapeDtypeStruct((1, w), F32) for w in sums]
                 + _comm_out_shapes(comm, kinds))
    return pl.pallas_call(body, name=name, grid=(steps,), in_specs=in_specs, out_specs=out_specs, out_shape=out_shape,
                          scratch_shapes=_comm_scratch(ncomm) if ncomm else [],
                          compiler_params=_cparams(("arbitrary",)))(*([] if made else [a]), b, *[t[0] for t in tiles], *rows, *comm)


def _resid_norm(y, x, w):
    h = x + y
    return h, rms(h, w)


def _norm_back(dhn, h, dres, w):
    _, vjp = jax.vjp(rms, h, w)
    dh, dw = vjp(dhn)
    return dh + dres, dw


def _gate_back(dog, *o_and_gate):
    outs, g = o_and_gate[:-1], o_and_gate[-1]
    s = sigmoid(g)
    silu, dsilu = g * s, s * (1.0 + g * (1.0 - s))
    d_outs, c = [], 0
    for o in outs:
        w = o.shape[1]
        d_outs.append(dog[:, c:c + w] * silu[:, c:c + w])
        c += w
    o_all = outs[0] if len(outs) == 1 else jnp.concatenate(outs, axis=1)
    return (*d_outs, dog * o_all * dsilu)


def _loss_head(y1, h1, target, b_out, fw):
    def f(h2, w):
        err = rms(h2, w) - target
        return 0.5 * jnp.sum(jnp.mean(err * err, axis=-1, keepdims=True), axis=0, keepdims=True)

    loss, vjp = jax.vjp(f, h1 + y1 + b_out, fw)
    dh2, dfw = vjp(jnp.ones((1, 1), F32))
    return dh2, jnp.broadcast_to(loss, (1, LANES)), jnp.sum(dh2, axis=0, keepdims=True), dfw


def _gla_load(q_ref, k_ref, v_ref, gl_ref, up_ref, bias_ref, nw_ref):
    toks = (q_ref[...], k_ref[...], v_ref[...], gl_ref[...])
    params = (up_ref[...], bias_ref[...], nw_ref[...])
    return toks, params


def _gla_specs(c, n=None):
    toks = [_tok_spec(c, GLA_KEY, _col(GLA_KEY, "gq", C0), n), _tok_spec(c, GLA_KEY, _col(GLA_KEY, "gk", C0), n),
            _tok_spec(c, GLA_VAL, _col(GLA_VAL, "gv", C0), n), _tok_spec(c, LOW, _col(LOW, "glow", C0), n)]
    return toks, [_full_spec(s) for s in GLA_PARAM_SHAPES]


GLA_PARAM_SHAPES = [(LOW, GLA_KEY), (1, GLA_KEY), (1, GLA_DV)]
GLA_STATE = (GLA_HEADS * GLA_DV, GLA_DK)


def _gla_fwd(proj0, gk_up, gk_bias, norm_w):
    t = proj0.shape[0]
    c = GLA_STEP
    nc = t // c
    toks_s, params_s = _gla_specs(c)

    def body(q_ref, k_ref, v_ref, gl_ref, up_ref, bias_ref, nw_ref, o_ref, st_ref, s_scr):
        @pl.when(pl.program_id(0) == 0)
        def _():
            s_scr[...] = jnp.zeros_like(s_scr)

        st_ref[...] = s_scr[...]
        toks, params = _gla_load(q_ref, k_ref, v_ref, gl_ref, up_ref, bias_ref, nw_ref)
        state = [s_scr[h * GLA_DV:(h + 1) * GLA_DV, :] for h in range(GLA_HEADS)]
        o_ref[...], new = gla_chunk(state, toks, params)
        for h in range(GLA_HEADS):
            s_scr[h * GLA_DV:(h + 1) * GLA_DV, :] = new[h]

    return pl.pallas_call(
        body, name="gla_fwd", grid=(nc,), in_specs=toks_s + params_s,
        out_specs=(_tok_spec(c, GLA_VAL, 0), pl.BlockSpec(GLA_STATE, lambda i: (i, 0))),
        out_shape=(jax.ShapeDtypeStruct((t, GLA_VAL), F32), jax.ShapeDtypeStruct((nc * GLA_STATE[0], GLA_DK), F32)),
        scratch_shapes=[pltpu.VMEM(GLA_STATE, F32)], compiler_params=_cparams(("arbitrary",)))(
            proj0, proj0, proj0, proj0, gk_up, gk_bias, norm_w)


def _gla_bwd(proj0, gk_up, gk_bias, norm_w, states, do):
    t = proj0.shape[0]
    c = GLA_STEP
    nc = t // c
    toks_s, params_s = _gla_specs(c, nc)

    def body(q_ref, k_ref, v_ref, gl_ref, up_ref, bias_ref, nw_ref, st_ref, do_ref,
             dq_ref, dk_ref, dv_ref, dgl_ref, dup_ref, dbias_ref, dnw_ref, ds_scr):
        @pl.when(pl.program_id(0) == 0)
        def _():
            ds_scr[...] = jnp.zeros_like(ds_scr)
            dup_ref[...] = jnp.zeros_like(dup_ref)
            dbias_ref[...] = jnp.zeros_like(dbias_ref)
            dnw_ref[...] = jnp.zeros_like(dnw_ref)

        toks, params = _gla_load(q_ref, k_ref, v_ref, gl_ref, up_ref, bias_ref, nw_ref)
        rows = lambda h: slice(h * GLA_DV, (h + 1) * GLA_DV)
        state = [st_ref[rows(h), :] for h in range(GLA_HEADS)]
        _, vjp = jax.vjp(gla_chunk, state, toks, params)
        dstate_in = [ds_scr[rows(h), :] for h in range(GLA_HEADS)]
        dstate, dtoks, (dup, dbias, dnw) = vjp((do_ref[...], dstate_in))
        for ref, val in zip((dq_ref, dk_ref, dv_ref, dgl_ref), dtoks):
            ref[...] = val.astype(ref.dtype)
        dup_ref[...] += dup
        dbias_ref[...] += dbias
        dnw_ref[...] += dnw
        for h in range(GLA_HEADS):
            ds_scr[rows(h), :] = dstate[h]

    rev = lambda w: pl.BlockSpec((c, w), lambda i: (nc - 1 - i, 0))
    tok_widths = (GLA_KEY, GLA_KEY, GLA_VAL, LOW)
    return pl.pallas_call(
        body, name="gla_bwd", grid=(nc,),
        in_specs=toks_s + params_s + [pl.BlockSpec(GLA_STATE, lambda i: (nc - 1 - i, 0)), rev(GLA_VAL)],
        out_specs=[rev(w) for w in tok_widths] + params_s,
        out_shape=[jax.ShapeDtypeStruct((t, w), BF16) for w in tok_widths] + [jax.ShapeDtypeStruct(s, F32) for s in GLA_PARAM_SHAPES],
        scratch_shapes=[pltpu.VMEM(GLA_STATE, F32)], compiler_params=_cparams(("arbitrary",)))(
            proj0, proj0, proj0, proj0, gk_up, gk_bias, norm_w, states, do)


RWKV_PARAM_SHAPES = [(1, RWKV_W), (1, RWKV_W), (1, RWKV_W), (1, LOW), (1, LOW), (1, RWKV_W), (LOW, RWKV_W), (1, RWKV_W),
                     (LOW, RWKV_W), (1, RWKV_W), (1, RWKV_W), (1, RWKV_W), (1, RWKV_W), (1, RWKV_W)]
RWKV_STATE = (RWKV_HEADS * RWKV_N, RWKV_N)
RWKV_TOK_WIDTHS = (RWKV_W, RWKV_W, RWKV_W, LOW, LOW)
PREV_W = sum(RWKV_TOK_WIDTHS)
PREV_COLS = [slice(sum(RWKV_TOK_WIDTHS[:i]), sum(RWKV_TOK_WIDTHS[:i + 1])) for i in range(len(RWKV_TOK_WIDTHS))]


def _rwkv_load(r_ref, k_ref, v_ref, xw_ref, xa_ref, p_refs):
    toks = (r_ref[...], k_ref[...], v_ref[...], xw_ref[...], xa_ref[...])
    return toks, tuple(p[...] for p in p_refs)


def _rwkv_state(s_ref, prev_ref):
    n = RWKV_N
    S = [s_ref[h * n:(h + 1) * n, :] for h in range(RWKV_HEADS)]
    return (S,) + tuple(prev_ref[0:1, cols] for cols in PREV_COLS)


def _rwkv_put_state(s_ref, prev_ref, state):
    n = RWKV_N
    for h in range(RWKV_HEADS):
        s_ref[h * n:(h + 1) * n, :] = state[0][h]
    for cols, val in zip(PREV_COLS, state[1:]):
        prev_ref[0:1, cols] = val


def _rwkv_specs(c, n=None):
    toks = [_tok_spec(c, w, _col(w, name, C0), n) for name, w in zip(("r", "k", "v", "xw", "xa"), RWKV_TOK_WIDTHS)]
    return toks, [_full_spec(s) for s in RWKV_PARAM_SHAPES]


def _rwkv_fwd(proj0, params, comm, kinds):
    t = proj0.shape[0]
    c = RWKV_CHUNK
    nc = t // c
    toks_s, params_s = _rwkv_specs(c)
    npar, ncomm = len(params), len(comm)

    def body(*refs):
        tok_refs, p_refs = refs[:5], refs[5:5 + npar]
        comm_in = refs[5 + npar:5 + npar + ncomm]
        o_ref, st_ref, pst_ref = refs[5 + npar + ncomm:8 + npar + ncomm]
        comm_out = refs[8 + npar + ncomm:8 + npar + 2 * ncomm]
        s_scr, prev_scr = refs[8 + npar + 2 * ncomm:10 + npar + 2 * ncomm]
        sems = refs[10 + npar + 2 * ncomm:]
        i = pl.program_id(0)

        @pl.when(i == 0)
        def _():
            _comm_start(*_comm_copies(comm_in, comm_out, kinds, *sems))
            s_scr[...] = jnp.zeros_like(s_scr)
            prev_scr[...] = jnp.zeros_like(prev_scr)

        st_ref[...] = s_scr[...]
        pst_ref[...] = prev_scr[...]
        toks, prm = _rwkv_load(*tok_refs, p_refs)
        o_ref[...], new = rwkv_chunk(_rwkv_state(s_scr, prev_scr), toks, prm)
        _rwkv_put_state(s_scr, prev_scr, new)

        @pl.when(i == nc - 1)
        def _():
            _comm_wait(*_comm_copies(comm_in, comm_out, kinds, *sems))

    outs = pl.pallas_call(
        body, name="rwkv_fwd", grid=(nc,), in_specs=toks_s + params_s + [ANY] * ncomm,
        out_specs=[_tok_spec(c, RWKV_W, 0), pl.BlockSpec(RWKV_STATE, lambda i: (i, 0)), pl.BlockSpec((8, PREV_W), lambda i: (i, 0))]
        + [ANY] * ncomm,
        out_shape=[jax.ShapeDtypeStruct((t, RWKV_W), F32), jax.ShapeDtypeStruct((nc * RWKV_STATE[0], RWKV_N), F32),
                   jax.ShapeDtypeStruct((nc * 8, PREV_W), F32)] + _comm_out_shapes(comm, kinds),
        scratch_shapes=[pltpu.VMEM(RWKV_STATE, F32), pltpu.VMEM((8, PREV_W), F32)] + _comm_scratch(ncomm),
        compiler_params=_cparams(("arbitrary",)))(proj0, proj0, proj0, proj0, proj0, *params, *comm)
    return outs[0], outs[1], outs[2], outs[3:]


def _rwkv_bwd(proj0, params, states, prevs, do, comm, kinds):
    t = proj0.shape[0]
    c = RWKV_CHUNK
    nc = t // c
    toks_s, params_s = _rwkv_specs(c, nc)
    npar, ncomm = len(params), len(comm)

    def body(*refs):
        tok_refs, p_refs = refs[:5], refs[5:5 + npar]
        st_ref, pst_ref, do_ref = refs[5 + npar:8 + npar]
        comm_in = refs[8 + npar:8 + npar + ncomm]
        outs = refs[8 + npar + ncomm:]
        dtok_refs, dp_refs, comm_out = outs[:5], outs[5:5 + npar], outs[5 + npar:5 + npar + ncomm]
        ds_scr, dprev_scr = outs[5 + npar + ncomm:7 + npar + ncomm]
        sems = outs[7 + npar + ncomm:]
        i = pl.program_id(0)

        @pl.when(i == 0)
        def _():
            _comm_start(*_comm_copies(comm_in, comm_out, kinds, *sems))
            ds_scr[...] = jnp.zeros_like(ds_scr)
            dprev_scr[...] = jnp.zeros_like(dprev_scr)
            for dp in dp_refs:
                dp[...] = jnp.zeros_like(dp)

        toks, prm = _rwkv_load(*tok_refs, p_refs)
        _, vjp = jax.vjp(rwkv_chunk, _rwkv_state(st_ref, pst_ref), toks, prm)
        dstate, dtoks, dprm = vjp((do_ref[...], _rwkv_state(ds_scr, dprev_scr)))
        for ref, val in zip(dtok_refs, dtoks):
            ref[...] = val.astype(ref.dtype)
        for ref, val in zip(dp_refs, dprm):
            ref[...] += val
        _rwkv_put_state(ds_scr, dprev_scr, dstate)

        @pl.when(i == nc - 1)
        def _():
            _comm_wait(*_comm_copies(comm_in, comm_out, kinds, *sems))

    rev = lambda w: pl.BlockSpec((c, w), lambda i: (nc - 1 - i, 0))
    outs = pl.pallas_call(
        body, name="rwkv_bwd", grid=(nc,),
        in_specs=toks_s + params_s + [pl.BlockSpec(RWKV_STATE, lambda i: (nc - 1 - i, 0)),
                                      pl.BlockSpec((8, PREV_W), lambda i: (nc - 1 - i, 0)), rev(RWKV_W)] + [ANY] * ncomm,
        out_specs=[rev(w) for w in RWKV_TOK_WIDTHS] + params_s + [ANY] * ncomm,
        out_shape=[jax.ShapeDtypeStruct((t, w), BF16) for w in RWKV_TOK_WIDTHS]
        + [jax.ShapeDtypeStruct(s, F32) for s in RWKV_PARAM_SHAPES] + _comm_out_shapes(comm, kinds),
        scratch_shapes=[pltpu.VMEM(RWKV_STATE, F32), pltpu.VMEM((8, PREV_W), F32)] + _comm_scratch(ncomm),
        compiler_params=_cparams(("arbitrary",)))(proj0, proj0, proj0, proj0, proj0, *params, states, prevs, do, *comm)
    return outs[:5], outs[5:5 + npar], outs[5 + npar:]


def _swa_load(q_ref, k_ref, v_ref, cos_ref, sin_ref, bq_ref, bk_ref, bv_ref, sk_ref):
    toks = (q_ref[...], k_ref[...], v_ref[...], cos_ref[...], sin_ref[...])
    params = (bq_ref[...], bk_ref[...], bv_ref[...], _heads(sk_ref, SWA_Q_HEADS, 1))
    return toks, params


SWA_TOK_WIDTHS = (MIX, SWA_KV, SWA_KV)
SWA_PARAM_SHAPES = [(1, MIX), (1, SWA_KV), (1, SWA_KV), (1, SWA_Q_HEADS)]
SWA_STATE = (WINDOW, SWA_KV)


def _swa_specs(c, n=None):
    toks = [_tok_spec(c, w, _col(w, name, C1), n) for name, w in zip(("q", "k", "v"), SWA_TOK_WIDTHS)]
    toks += [_tok_spec(c, LANES, 0, n), _tok_spec(c, LANES, 0, n)]
    return toks, [_full_spec(s) for s in SWA_PARAM_SHAPES]


def _swa_fwd(proj1, cos, sin, bq, bk, bv, sinks):
    t = proj1.shape[0]
    c = SWA_STEP
    nb = t // c
    toks_s, params_s = _swa_specs(c)
    state_spec = pl.BlockSpec(SWA_STATE, lambda i: (i, 0))
    kv = SWA_KV_HEADS

    def body(q_ref, k_ref, v_ref, cos_ref, sin_ref, bq_ref, bk_ref, bv_ref, sk_ref, o_ref, kst_ref, vst_ref, k_scr, v_scr):
        first = pl.program_id(0) == 0

        @pl.when(first)
        def _():
            k_scr[...] = jnp.zeros_like(k_scr)
            v_scr[...] = jnp.zeros_like(v_scr)

        kst_ref[...] = k_scr[...]
        vst_ref[...] = v_scr[...]
        toks, params = _swa_load(q_ref, k_ref, v_ref, cos_ref, sin_ref, bq_ref, bk_ref, bv_ref, sk_ref)
        outs, (kn, vn) = swa_chunk((_heads(k_scr, kv, SWA_HD), _heads(v_scr, kv, SWA_HD)), toks, params, first)
        _put_heads(o_ref, outs, SWA_HD)
        _put_heads(k_scr, kn, SWA_HD)
        _put_heads(v_scr, vn, SWA_HD)

    saved = jax.ShapeDtypeStruct((nb * WINDOW, SWA_KV), F32)
    return pl.pallas_call(
        body, name="swa_fwd", grid=(nb,), in_specs=toks_s + params_s,
        out_specs=(_tok_spec(c, MIX, 0), state_spec, state_spec),
        out_shape=(jax.ShapeDtypeStruct((t, MIX), F32), saved, saved),
        scratch_shapes=[pltpu.VMEM(SWA_STATE, F32), pltpu.VMEM(SWA_STATE, F32)],
        compiler_params=_cparams(("arbitrary",)))(proj1, proj1, proj1, cos, sin, bq, bk, bv, sinks)


def _swa_bwd(proj1, cos, sin, bq, bk, bv, sinks, kst, vst, do):
    t = proj1.shape[0]
    c = SWA_STEP
    nb = t // c
    toks_s, params_s = _swa_specs(c, nb)
    state_spec = pl.BlockSpec(SWA_STATE, lambda i: (nb - 1 - i, 0))
    kv = SWA_KV_HEADS

    def body(q_ref, k_ref, v_ref, cos_ref, sin_ref, bq_ref, bk_ref, bv_ref, sk_ref, kst_ref, vst_ref, do_ref,
             dq_ref, dk_ref, dv_ref, dbq_ref, dbk_ref, dbv_ref, dsk_ref, dk_scr, dv_scr):
        i = pl.program_id(0)

        @pl.when(i == 0)
        def _():
            dk_scr[...] = jnp.zeros_like(dk_scr)
            dv_scr[...] = jnp.zeros_like(dv_scr)
            for ref in (dbq_ref, dbk_ref, dbv_ref, dsk_ref):
                ref[...] = jnp.zeros_like(ref)

        first = i == nb - 1
        toks, params = _swa_load(q_ref, k_ref, v_ref, cos_ref, sin_ref, bq_ref, bk_ref, bv_ref, sk_ref)
        f = functools.partial(swa_chunk, first=first)
        _, vjp = jax.vjp(f, (_heads(kst_ref, kv, SWA_HD), _heads(vst_ref, kv, SWA_HD)), toks, params)
        dstate_in = (_heads(dk_scr, kv, SWA_HD), _heads(dv_scr, kv, SWA_HD))
        (dkp, dvp), (dq, dk, dv, _, _), (dbq, dbk, dbv, dsk) = vjp((_heads(do_ref, SWA_Q_HEADS, SWA_HD), dstate_in))
        dq_ref[...], dk_ref[...], dv_ref[...] = dq.astype(BF16), dk.astype(BF16), dv.astype(BF16)
        dbq_ref[...] += dbq
        dbk_ref[...] += dbk
        dbv_ref[...] += dbv
        _put_heads(dsk_ref, dsk, 1, add=True)
        _put_heads(dk_scr, dkp, SWA_HD)
        _put_heads(dv_scr, dvp, SWA_HD)

    rev = lambda w: pl.BlockSpec((c, w), lambda i: (nb - 1 - i, 0))
    return pl.pallas_call(
        body, name="swa_bwd", grid=(nb,), in_specs=toks_s + params_s + [state_spec, state_spec, rev(MIX)],
        out_specs=[rev(w) for w in SWA_TOK_WIDTHS] + params_s,
        out_shape=[jax.ShapeDtypeStruct((t, w), BF16) for w in SWA_TOK_WIDTHS] + [jax.ShapeDtypeStruct(s, F32) for s in SWA_PARAM_SHAPES],
        scratch_shapes=[pltpu.VMEM(SWA_STATE, F32), pltpu.VMEM(SWA_STATE, F32)],
        compiler_params=_cparams(("arbitrary",)))(proj1, proj1, proj1, cos, sin, bq, bk, bv, sinks, kst, vst, do)


MESH = pl.DeviceIdType.MESH
ANY = pl.BlockSpec(memory_space=pl.ANY)


def _my_place():
    return lax.axis_index("x"), lax.axis_index("y"), lax.axis_index("c")


def _all_gather(shards):
    n = len(shards)

    def body(*refs):
        in_refs, out_refs = refs[:n], refs[n:2 * n]
        send_sems, recv_sems, local_sems = refs[2 * n:]
        x, y, c = _my_place()
        me, sibling = (x, y, c), (x, y, 1 - c)
        chips = [(1 - x, y), (x, 1 - y), (1 - x, 1 - y)]

        def slot(out_ref, place):
            px, py, pc = place
            return out_ref.at[4 * px + 2 * py + pc]

        def copy(a, k, block, to, src=None):
            return pltpu.make_async_remote_copy(
                src_ref=slot(out_refs[a], block) if src is None else src, dst_ref=slot(out_refs[a], block),
                send_sem=send_sems.at[a, k], recv_sem=recv_sems.at[a, k], device_id=to, device_id_type=MESH)

        mine = [pltpu.make_async_copy(in_refs[a], slot(out_refs[a], me), local_sems.at[a]) for a in range(n)]
        for cp in mine:
            cp.start()
        first = []
        for a in range(n):
            first.append(copy(a, 0, me, sibling, src=in_refs[a]))
            first += [copy(a, 1 + j, me, (*chip, c), src=in_refs[a]) for j, chip in enumerate(chips)]
        for cp in first:
            cp.start()
        passed = []
        for j, chip in enumerate(chips):
            for a in range(n):
                copy(a, 1 + j, (*chip, c), me).wait_recv()
                fwd = copy(a, 4 + j, (*chip, c), sibling)
                fwd.start()
                passed.append(fwd)
        for a in range(n):
            copy(a, 0, sibling, me).wait_recv()
            for j, chip in enumerate(chips):
                copy(a, 4 + j, (*chip, 1 - c), me).wait_recv()
        for cp in first + passed:
            cp.wait_send()
        for cp in mine:
            cp.wait()

    return pl.pallas_call(
        body, name="all_gather_weights", in_specs=[ANY] * n, out_specs=[ANY] * n,
        out_shape=[jax.ShapeDtypeStruct((N_DEV,) + s.shape, s.dtype) for s in shards],
        scratch_shapes=_comm_scratch(n))(*shards)


def _comm_copies(in_refs, out_refs, kinds, send_sems, recv_sems, local_sems):
    x, y, c = _my_place()
    my_idx = 4 * x + 2 * y + c
    src = lambda a, idx: in_refs[a] if kinds[a] == "gather" else in_refs[a].at[idx]
    local = [pltpu.make_async_copy(src(a, my_idx), out_refs[a].at[my_idx], local_sems.at[a]) for a in range(len(kinds))]
    remote = []
    for rel in range(1, N_DEV):
        px, py, pc = x ^ ((rel >> 2) & 1), y ^ ((rel >> 1) & 1), c ^ (rel & 1)
        for a in range(len(kinds)):
            remote.append(pltpu.make_async_remote_copy(
                src_ref=src(a, 4 * px + 2 * py + pc), dst_ref=out_refs[a].at[my_idx], send_sem=send_sems.at[a, rel - 1],
                recv_sem=recv_sems.at[a, rel - 1], device_id=(px, py, pc), device_id_type=MESH))
    return local, remote


def _comm_start(local, remote):
    for cp in local + remote:
        cp.start()


def _comm_wait(local, remote):
    for cp in remote:
        cp.wait_recv()
    for cp in remote:
        cp.wait_send()
    for cp in local:
        cp.wait()


def _comm_out_shapes(arrays, kinds):
    return [jax.ShapeDtypeStruct(((N_DEV,) + a.shape) if k == "gather" else a.shape, a.dtype) for a, k in zip(arrays, kinds)]


def _comm_scratch(n):
    return [pltpu.SemaphoreType.DMA((n, N_DEV - 1)), pltpu.SemaphoreType.DMA((n, N_DEV - 1)), pltpu.SemaphoreType.DMA((n,))]


def _exchange(arrays, kinds):
    n = len(arrays)

    def body(*refs):
        copies = _comm_copies(refs[:n], refs[n:2 * n], kinds, *refs[2 * n:])
        _comm_start(*copies)
        _comm_wait(*copies)

    return pl.pallas_call(body, name="exchange_grads", in_specs=[ANY] * n, out_specs=[ANY] * n,
                          out_shape=_comm_out_shapes(arrays, kinds), scratch_shapes=_comm_scratch(n))(*arrays)


def _adam_math(w, g, m, v):
    m = ADAM_B1 * m + (1.0 - ADAM_B1) * g
    v = ADAM_B2 * v + (1.0 - ADAM_B2) * (g * g)
    m_hat = m / (1.0 - ADAM_B1 ** ADAM_STEP)
    v_hat = v / (1.0 - ADAM_B2 ** ADAM_STEP)
    delta = -ADAM_LR * (m_hat / (jnp.sqrt(v_hat) + ADAM_EPS) + ADAM_WD * w)
    return delta, m, v


def _adamw(name, w, gslots, m, v, tc):
    r, cc = w.shape
    assert cc % tc == 0
    tile = pl.BlockSpec((r, tc), lambda i: (0, i))

    def body(w_ref, g_ref, m_ref, v_ref, go_ref, d_ref, mo_ref, vo_ref):
        g = g_ref[0].astype(F32)
        for s in range(1, N_DEV):
            g = g + g_ref[s].astype(F32)
        d, mn, vn = _adam_math(w_ref[...], g, m_ref[...], v_ref[...])
        go_ref[...] = g
        d_ref[...] = d
        mo_ref[...] = mn
        vo_ref[...] = vn

    shp = jax.ShapeDtypeStruct((r, cc), F32)
    return pl.pallas_call(body, name=name, grid=(cc // tc,),
                          in_specs=[tile, pl.BlockSpec((N_DEV, r, tc), lambda i: (0, 0, i)), tile, tile],
                          out_specs=(tile,) * 4, out_shape=(shp,) * 4, compiler_params=_cparams(("arbitrary",)))(w, gslots, m, v)


PACK_TILE = 8 * LANES


def _packed_rows(shape, mode):
    r, w = shape
    return -(-r // 8) * 8 if mode == "rows" else -(-(r * w) // PACK_TILE) * 8


def _pack_small(arrays, modes, lead=False):
    out = []
    for a, mode in zip(arrays, modes):
        a = a.astype(F32) if lead else a.astype(F32)[None]
        if mode == "rows":
            out.append(jnp.pad(a, ((0, 0), (0, (-a.shape[1]) % 8), (0, LANES - a.shape[2]))))
        else:
            flat = a.reshape(a.shape[0], -1)
            out.append(jnp.pad(flat, ((0, 0), (0, (-flat.shape[1]) % PACK_TILE))).reshape(a.shape[0], -1, LANES))
    out = jnp.concatenate(out, axis=1)
    return out if lead else out[0]


def _take_small(packed, row0, shape, mode):
    r, w = shape
    lead = packed.ndim == 3
    if mode == "rows":
        return packed[:, row0:row0 + r, :w] if lead else packed[row0:row0 + r, :w]
    per_row = -(-w // LANES)
    if lead:
        return packed[:, row0:row0 + r * per_row].reshape(packed.shape[0], r, per_row * LANES)[:, :, :w]
    rows = []
    for i in range(r):
        pieces = [packed[row0 + i * per_row + j:row0 + i * per_row + j + 1, :] for j in range(per_row)]
        rows.append((pieces[0] if per_row == 1 else jnp.concatenate(pieces, axis=1))[:, :w])
    return rows[0] if r == 1 else jnp.concatenate(rows, axis=0)


def _adamw_small(slots, specs, ws, ms, vs, loss_row):
    n = len(specs)

    def body(*refs):
        slots_ref, w_refs, m_refs, v_refs = refs[0], refs[1:1 + n], refs[1 + n:1 + 2 * n], refs[1 + 2 * n:1 + 3 * n]
        out_refs, loss_ref = refs[1 + 3 * n:1 + 7 * n], refs[1 + 7 * n]
        gp = slots_ref[0]
        for s in range(1, N_DEV):
            gp = gp + slots_ref[s]
        read = lambda ref: ref[0] if len(ref.shape) == 3 else ref[...]
        for k, (shape, mode, row0) in enumerate(specs):
            g = _take_small(gp, row0, shape, mode)
            d, mn, vn = _adam_math(read(w_refs[k]), g, read(m_refs[k]), read(v_refs[k]))
            for ref, val in zip(out_refs[4 * k:4 * k + 4], (g, d, mn, vn)):
                if len(ref.shape) == 3:
                    ref[0] = val
                else:
                    ref[...] = val
        loss_ref[...] = gp[loss_row:loss_row + 1, :]

    vmem = pl.BlockSpec(memory_space=pltpu.VMEM)
    out_shape = [jax.ShapeDtypeStruct(w.shape, F32) for w in ws for _ in range(4)] + [jax.ShapeDtypeStruct((1, LANES), F32)]
    outs = pl.pallas_call(body, name="adamw_small", in_specs=[vmem] * (1 + 3 * n), out_specs=[vmem] * (4 * n + 1),
                          out_shape=out_shape)(slots, *ws, *ms, *vs)
    return [outs[4 * k:4 * k + 4] for k in range(n)], outs[4 * n]


def _rope_tables(t):
    dim = jnp.arange(LANES) % SWA_HD
    inv_freq = ROPE_THETA ** (-(dim % ROPE_HALF).astype(F32) / ROPE_HALF)
    ang = jnp.arange(t, dtype=F32)[:, None] * jnp.where(dim < 2 * ROPE_HALF, inv_freq, 0.0)[None, :]
    return jnp.cos(ang), jnp.sin(ang)


def _pad_to(a, rows=None, cols=None):
    r = 0 if rows is None else rows - a.shape[0]
    c = 0 if cols is None else cols - a.shape[1]
    return jnp.pad(a, ((0, r), (0, c)))


ORIG0 = dict(gq=(0, 256), gk=(256, 256), gv=(512, 512), glow=(1024, 16), r=(1040, 512), k=(1552, 512), v=(2064, 512),
             xw=(2576, 64), xa=(2640, 64), gate=(2704, 1024))
ORIG0_ORDER = ["gq", "gk", "gv", "glow", "r", "k", "v", "xw", "xa", "gate"]


def _w0t_to_padded(wt):
    rows, at = [], 0
    for name, (off, width) in sorted(C0.items(), key=lambda kv: kv[1][0]):
        assert off == at
        src, src_w = ORIG0[name]
        rows.append(_pad_to(wt[src:src + src_w], rows=width))
        at += width
    rows.append(jnp.zeros((N0P - at, wt.shape[1]), wt.dtype))
    return jnp.concatenate(rows, axis=0)


def _w0t_from_padded(wpt):
    return jnp.concatenate([wpt[C0[n][0]:C0[n][0] + ORIG0[n][1]] for n in ORIG0_ORDER], axis=0)


def _w1t_to_mine(wt):
    return jnp.concatenate([wt[1536:2560], wt[:1536]], axis=0)


def _w1t_from_mine(wt):
    return jnp.concatenate([wt[1024:2560], wt[:1024]], axis=0)


def kernel(x, norm_w, w_in0, gla_gk_up, gla_gk_bias, gla_norm_w, rwkv_mu, rwkv_w0, rwkv_w_up, rwkv_a0, rwkv_a_up, rwkv_k_k, rwkv_k_a, rwkv_r_k, rwkv_ln_w, rwkv_ln_b, w_out0, w_in1, b_in1, attn_sinks, w_out1, b_out1, final_norm_w, loss_target, m_norm_w, m_w_in0, m_gla_gk_up, m_gla_gk_bias, m_gla_norm_w, m_rwkv_mu, m_rwkv_w0, m_rwkv_w_up, m_rwkv_a0, m_rwkv_a_up, m_rwkv_k_k, m_rwkv_k_a, m_rwkv_r_k, m_rwkv_ln_w, m_rwkv_ln_b, m_w_out0, m_w_in1, m_b_in1, m_attn_sinks, m_w_out1, m_b_out1, m_final_norm_w, v_norm_w, v_w_in0, v_gla_gk_up, v_gla_gk_bias, v_gla_norm_w, v_rwkv_mu, v_rwkv_w0, v_rwkv_w_up, v_rwkv_a0, v_rwkv_a_up, v_rwkv_k_k, v_rwkv_k_a, v_rwkv_r_k, v_rwkv_ln_w, v_rwkv_ln_b, v_w_out0, v_w_in1, v_b_in1, v_attn_sinks, v_w_out1, v_b_out1, v_final_norm_w):
    weights = dict(norm_w=norm_w, w_in0=w_in0, gla_gk_up=gla_gk_up, gla_gk_bias=gla_gk_bias, gla_norm_w=gla_norm_w, rwkv_mu=rwkv_mu,
                   rwkv_w0=rwkv_w0, rwkv_w_up=rwkv_w_up, rwkv_a0=rwkv_a0, rwkv_a_up=rwkv_a_up, rwkv_k_k=rwkv_k_k, rwkv_k_a=rwkv_k_a,
                   rwkv_r_k=rwkv_r_k, rwkv_ln_w=rwkv_ln_w, rwkv_ln_b=rwkv_ln_b, w_out0=w_out0, w_in1=w_in1, b_in1=b_in1,
                   attn_sinks=attn_sinks, w_out1=w_out1, b_out1=b_out1, final_norm_w=final_norm_w)
    moms = dict(norm_w=m_norm_w, w_in0=m_w_in0, gla_gk_up=m_gla_gk_up, gla_gk_bias=m_gla_gk_bias, gla_norm_w=m_gla_norm_w,
                rwkv_mu=m_rwkv_mu, rwkv_w0=m_rwkv_w0, rwkv_w_up=m_rwkv_w_up, rwkv_a0=m_rwkv_a0, rwkv_a_up=m_rwkv_a_up,
                rwkv_k_k=m_rwkv_k_k, rwkv_k_a=m_rwkv_k_a, rwkv_r_k=m_rwkv_r_k, rwkv_ln_w=m_rwkv_ln_w, rwkv_ln_b=m_rwkv_ln_b,
                w_out0=m_w_out0, w_in1=m_w_in1, b_in1=m_b_in1, attn_sinks=m_attn_sinks, w_out1=m_w_out1, b_out1=m_b_out1,
                final_norm_w=m_final_norm_w)
    vars_ = dict(norm_w=v_norm_w, w_in0=v_w_in0, gla_gk_up=v_gla_gk_up, gla_gk_bias=v_gla_gk_bias, gla_norm_w=v_gla_norm_w,
                 rwkv_mu=v_rwkv_mu, rwkv_w0=v_rwkv_w0, rwkv_w_up=v_rwkv_w_up, rwkv_a0=v_rwkv_a0, rwkv_a_up=v_rwkv_a_up,
                 rwkv_k_k=v_rwkv_k_k, rwkv_k_a=v_rwkv_k_a, rwkv_r_k=v_rwkv_r_k, rwkv_ln_w=v_rwkv_ln_w, rwkv_ln_b=v_rwkv_ln_b,
                 w_out0=v_w_out0, w_in1=v_w_in1, b_in1=v_b_in1, attn_sinks=v_attn_sinks, w_out1=v_w_out1, b_out1=v_b_out1,
                 final_norm_w=v_final_norm_w)
    names = list(weights)
    big = ["w_in0", "w_out0", "w_in1", "w_out1"]
    small_sharded = ["gla_gk_up", "rwkv_w_up", "rwkv_a_up", "b_in1", "b_out1"]
    replicated = [n for n in names if n not in big and n not in small_sharded]

    xs = x[0]
    tgt = loss_target[0]
    t = xs.shape[0]

    def view(w):
        shape = tuple(w.shape[-2:]) if w.ndim >= 2 else (1, w.shape[0])
        return shape, ("rows" if shape[0] > 1 and shape[1] <= LANES else "flat")

    def layout(ns, row0=0):
        specs = []
        for n in ns:
            shape, mode = view(weights[n])
            specs.append((shape, mode, row0))
            row0 += _packed_rows(shape, mode)
        return specs, row0

    sh_specs, n_shard_rows = layout(small_sharded)
    rep_specs, loss_row = layout(replicated, n_shard_rows)
    sh_modes, rep_modes = [s[1] for s in sh_specs], [s[1] for s in rep_specs]

    small_shard_pack = _pack_small([weights[n].reshape(view(weights[n])[0]) for n in small_sharded], sh_modes)
    g_in0, g_small = _all_gather([w_in0[0].T.astype(BF16), small_shard_pack])
    w0t = _w0t_to_padded(g_in0.reshape(-1, D_MODEL))
    later_shards = [w_out0[0].astype(BF16), w_in1[0].T.astype(BF16), w_out1[0].astype(BF16)]
    gs = [_take_small(g_small, row0, shape, mode) for shape, mode, row0 in sh_specs]
    join_cols = lambda a: jnp.transpose(a, (1, 0, 2)).reshape(a.shape[1], -1)
    gk_up, w_up, a_up = join_cols(gs[0]), join_cols(gs[1]), join_cols(gs[2])
    b_in, b_out = gs[3].reshape(1, -1), gs[4].reshape(1, -1)

    gk_up_p = _pad_to(gk_up, rows=LOW)
    w3, rank = 3 * RWKV_W, rwkv_w_up.shape[1]
    mu = rwkv_mu
    rwkv_params = [mu[:, 0:RWKV_W], mu[:, RWKV_W:2 * RWKV_W], mu[:, 2 * RWKV_W:w3], _pad_to(mu[:, w3:w3 + rank], cols=LOW),
                   _pad_to(mu[:, w3 + rank:], cols=LOW), rwkv_w0, _pad_to(w_up, rows=LOW), rwkv_a0, _pad_to(a_up, rows=LOW),
                   rwkv_k_k, rwkv_k_a, rwkv_r_k.reshape(1, RWKV_W), rwkv_ln_w, rwkv_ln_b]
    bq, bk, bv = b_in[:, :MIX], b_in[:, MIX:MIX + SWA_KV], b_in[:, MIX + SWA_KV:]
    cos, sin = _rope_tables(t)
    nw0, nw1, fw = norm_w[0:1], norm_w[1:2], final_norm_w.reshape(1, D_MODEL)

    d = D_MODEL
    wide = lambda arr: (arr, d, 0)
    silu = lambda g: g * sigmoid(g)
    hn0, proj0 = _matmul_fused("norm0_proj0", rms, w0t, "nt", [wide(xs)], [nw0], [(N0P, F32)], [], lambda acc, x, w: (acc,))
    o_a, gla_states = _gla_fwd(proj0, gk_up_p, gla_gk_bias, gla_norm_w)
    o_b, rwkv_states, rwkv_prevs, (g_out0, g_in1, g_out1) = _rwkv_fwd(proj0, rwkv_params, later_shards, ["gather"] * 3)
    wo0 = g_out0.reshape(MIX, D_MODEL)
    w1t = _w1t_to_mine(g_in1.reshape(-1, D_MODEL))
    wo1 = g_out1.reshape(MIX, D_MODEL)
    og0, h1, hn1 = _matmul_fused(
        "gate0_out0_norm1", lambda oa, ob, gate, x, w: jnp.concatenate([oa, ob], axis=1) * silu(gate), wo0, "nn",
        [(o_a, GLA_VAL, 0), (o_b, RWKV_W, 0), wide(proj0), wide(xs)], [nw1], [(d, F32), (d, BF16)], [],
        lambda acc, oa, ob, gate, x, w: _resid_norm(acc, x, w))
    proj1 = _matmul("proj1", hn1, w1t, "nt", PROJ_ROWS, N1P // 2)
    o_c, kst, vst = _swa_fwd(proj1, cos, sin, bq, bk, bv, attn_sinks)
    og1, dh2, loss_part, d_b_out, d_fw = _matmul_fused(
        "gate1_out1_loss", lambda oc, gate, h, tg, b, w: oc * silu(gate), wo1, "nn",
        [wide(o_c), wide(proj1), wide(h1), wide(tgt)], [b_out, fw], [(d, F32)], [LANES, d, d],
        lambda acc, oc, gate, h, tg, b, w: _loss_head(acc, h, tg, b, w))

    d_oc, d_gate1 = _matmul_fused("out1_dx_gate1", dh2, wo1, "nt", [wide(o_c), wide(proj1)], [], [(d, F32), (d, BF16)], [], _gate_back)
    d_wo1 = _matmul("out1_dw", og1, dh2, "tn", DW_COLS, DW_COLS, BF16)
    dq, dk, dv, d_bq, d_bk, d_bv, d_sinks = _swa_bwd(proj1, cos, sin, bq, bk, bv, attn_sinks, kst, vst, d_oc)
    dproj1 = jnp.concatenate([d_gate1, dq, dk, dv], axis=1)
    dh1, d_nw1 = _matmul_fused("proj1_dx_norm1", dproj1, w1t, "nn", [wide(h1), wide(dh2)], [nw1], [(d, F32)], [d], _norm_back)
    d_w1t = _matmul("proj1_dw", dproj1, hn1, "tn", DW_COLS, d, BF16)
    d_oa, d_ob, d_gate0 = _matmul_fused("out0_dx_gate0", dh1, wo0, "nt", [(o_a, GLA_VAL, 0), (o_b, RWKV_W, 0), wide(proj0)], [],
                                        [(GLA_VAL, F32), (RWKV_W, F32), (d, BF16)], [], _gate_back)
    d_wo0 = _matmul("out0_dw", og0, dh1, "tn", DW_COLS, DW_COLS, BF16)
    dgq, dgk, dgv, dglow, d_gk_up, d_gk_bias, d_gla_nw = _gla_bwd(proj0, gk_up_p, gla_gk_bias, gla_norm_w, gla_states, d_oa)
    row_blocks = lambda a: a.astype(BF16).reshape(N_DEV, -1, D_MODEL)
    early = [row_blocks(_w1t_from_mine(d_w1t)), row_blocks(d_wo1), row_blocks(d_wo0)]
    (dr, dkk, dvv, dxw, dxa), d_rp, (r_in1, r_out1, r_out0) = _rwkv_bwd(
        proj0, rwkv_params, rwkv_states, rwkv_prevs, d_ob, early, ["scatter"] * 3)
    pad = jnp.zeros((t, N0P - C0["xa"][0] - C0["xa"][1]), BF16)
    dproj0 = jnp.concatenate([d_gate0, dgv, dr, dkk, dvv, dgq, dgk, dglow, dxw, dxa, pad], axis=1)
    d_w0 = _w0t_from_padded(_matmul("proj0_dw", dproj0, hn0, "tn", DW_COLS, d, BF16))
    grad_x, d_nw0, r_in0 = _matmul_fused("proj0_dx_norm0", dproj0, w0t, "nn", [wide(xs), wide(dh1)], [nw0], [(d, F32)], [d],
                                         _norm_back, [row_blocks(d_w0)], ["scatter"])

    contrib = dict(
        norm_w=jnp.concatenate([d_nw0, d_nw1], axis=0), gla_gk_bias=d_gk_bias, gla_norm_w=d_gla_nw,
        rwkv_mu=jnp.concatenate([d_rp[0], d_rp[1], d_rp[2], d_rp[3][:, :rank], d_rp[4][:, :rank]], axis=1),
        rwkv_w0=d_rp[5], rwkv_a0=d_rp[7], rwkv_k_k=d_rp[9], rwkv_k_a=d_rp[10], rwkv_r_k=d_rp[11].reshape(RWKV_HEADS, RWKV_N),
        rwkv_ln_w=d_rp[12], rwkv_ln_b=d_rp[13], attn_sinks=d_sinks, final_norm_w=d_fw)
    rep_pack = _pack_small([contrib[n] for n in replicated] + [loss_part[:, :1]], rep_modes + ["flat"])

    d_b_in = jnp.concatenate([d_bq, d_bk, d_bv], axis=1)
    full_small = [d_gk_up[:gk_up.shape[0]], d_rp[6][:rank], d_rp[8][:rank], d_b_in, d_b_out]
    split_cols = lambda a: jnp.transpose(a.reshape(a.shape[0], N_DEV, -1), (1, 0, 2))
    small_parts = [split_cols(a) for a in full_small]
    small_pack = _pack_small(small_parts, sh_modes, lead=True)
    r_small, r_rep = _exchange([small_pack, rep_pack], ["scatter", "gather"])

    res = {}
    res["w_in0"] = tuple(a.T[None] for a in _adamw("adamw_w_in0", w_in0[0].T, r_in0, m_w_in0[0].T, v_w_in0[0].T, ADAM_COLS))
    res["w_out0"] = tuple(a[None] for a in _adamw("adamw_w_out0", w_out0[0], r_out0, m_w_out0[0], v_w_out0[0], ADAM_COLS))
    res["w_in1"] = tuple(a.T[None] for a in _adamw("adamw_w_in1", w_in1[0].T, r_in1, m_w_in1[0].T, v_w_in1[0].T, ADAM_COLS))
    res["w_out1"] = tuple(a[None] for a in _adamw("adamw_w_out1", w_out1[0], r_out1, m_w_out1[0], v_w_out1[0], ADAM_COLS))
    small_names = small_sharded + replicated
    slots = jnp.concatenate([r_small, r_rep], axis=1)
    as_2d = lambda a: a.reshape(1, -1) if a.ndim == 1 else a
    small_res, loss_row_out = _adamw_small(slots, sh_specs + rep_specs, [as_2d(weights[n]) for n in small_names],
                                           [as_2d(moms[n]) for n in small_names], [as_2d(vars_[n]) for n in small_names], loss_row)
    for n, vals in zip(small_names, small_res):
        res[n] = tuple(val.reshape(weights[n].shape) for val in vals)
    loss = loss_row_out[0, 0]
    return (loss, grad_x[None], *[res[n][0] for n in names], *[res[n][1] for n in names],
            *[res[n][2] for n in names], *[res[n][3] for n in names])
```

```python
import functools

import jax
import jax.numpy as jnp
from jax import lax
from jax.experimental import pallas as pl
from jax.experimental.pallas import tpu as pltpu
from jax.experimental.pallas import tpu_sc as plsc

F32 = jnp.float32
BF16 = jnp.bfloat16
HI = lax.Precision.HIGHEST

D_MODEL = 1024
NORM_EPS = 1e-5
GLA_HEADS, GLA_DK, GLA_DV = 4, 64, 128
GLA_NORMALIZER = 16.0
GLA_CHUNK = 64
GLA_STEP = 512
RWKV_HEADS, RWKV_N = 8, 64
RWKV_LN_EPS = 64e-5
RWKV_CHUNK = 128
SWA_Q_HEADS, SWA_KV_HEADS, SWA_GROUP, SWA_HD = 16, 4, 4, 64
WINDOW = 128
SWA_STEP = 512
ROPE_THETA = 500000.0
NEG = -1e30
N_DEV = 8
LANES = 128

ADAM_LR, ADAM_B1, ADAM_B2, ADAM_EPS, ADAM_WD, ADAM_STEP = 0.001, 0.9, 0.999, 1e-08, 0.01, 10

GLA_KEY, GLA_VAL = GLA_HEADS * GLA_DK, GLA_HEADS * GLA_DV
RWKV_W = RWKV_HEADS * RWKV_N
SWA_KV = SWA_KV_HEADS * SWA_HD
MIX = GLA_VAL + RWKV_W
LOW = LANES

N0P = 4096
C0 = dict(gate=(0, MIX), gv=(1024, GLA_VAL), r=(1536, RWKV_W), k=(2048, RWKV_W), v=(2560, RWKV_W), gq=(3072, GLA_KEY),
          gk=(3328, GLA_KEY), glow=(3584, LOW), xw=(3712, LOW), xa=(3840, LOW))
N1P = 2560
C1 = dict(gate=(0, MIX), q=(1024, MIX), k=(2048, SWA_KV), v=(2304, SWA_KV))

VMEM_LIMIT = 56 * 1024 * 1024

P_LORA = 1
P_GLA = 1
P_RWKV_G = 2
P_RWKV = 1
P_SWA = 1


def _cparams(sem=None):
    return pltpu.CompilerParams(dimension_semantics=sem, vmem_limit_bytes=VMEM_LIMIT)


DIMS = dict(nn=(((1,), (0,)), ((), ())), nt=(((1,), (1,)), ((), ())), tn=(((0,), (0,)), ((), ())))


def _split_bf16(a):
    hi = a.astype(BF16)
    return hi, (a - hi.astype(F32)).astype(BF16)


def _dot(a, b, mode, passes):
    dg = lambda p, q: lax.dot_general(p, q, DIMS[mode], preferred_element_type=F32)
    if passes == 1:
        return dg(a.astype(BF16), b.astype(BF16))
    if passes == 2:
        ah, (bh, bl) = a.astype(BF16), _split_bf16(b)
        return dg(ah, bh) + dg(ah, bl)
    if passes == 3:
        (ah, al), (bh, bl) = _split_bf16(a), _split_bf16(b)
        return dg(ah, bh) + dg(al, bh) + dg(ah, bl)
    return lax.dot_general(a, b, DIMS[mode], precision=HI, preferred_element_type=F32)


@functools.partial(jax.custom_vjp, nondiff_argnums=(2, 3))
def mmx(a, b, mode, passes):
    return _dot(a, b, mode, passes)


def _mmx_fwd(a, b, mode, passes):
    return _dot(a, b, mode, passes), (a, b)


def _mmx_bwd(mode, passes, res, g):
    a, b = res
    if mode == "nn":
        return _dot(g, b, "nt", passes), _dot(a, g, "tn", passes)
    if mode == "nt":
        return _dot(g, b, "nn", passes), _dot(g, a, "tn", passes)
    return _dot(b, g, "nt", passes), _dot(a, g, "nn", passes)


mmx.defvjp(_mmx_fwd, _mmx_bwd)


def _tri_dot(tri, x):
    t = tri.astype(BF16)
    x1 = x.astype(BF16)
    r1 = x - x1.astype(F32)
    x2 = r1.astype(BF16)
    x3 = (r1 - x2.astype(F32)).astype(BF16)
    dg = lambda q: jnp.dot(t, q, preferred_element_type=F32)
    return dg(x1) + dg(x2) + dg(x3)


@jax.custom_vjp
def cumsum_rows(x):
    return _tri_dot(tril_ones(x.shape[0]), x)


def _cumsum_fwd(x):
    return cumsum_rows(x), None


def _cumsum_bwd(_, g):
    i, j = _iota2(g.shape[0], g.shape[0])
    return (_tri_dot(jnp.where(i <= j, 1.0, 0.0).astype(F32), g),)


cumsum_rows.defvjp(_cumsum_fwd, _cumsum_bwd)


def _head_dot(x):
    i, j = _iota2(LANES, LANES)
    shift = RWKV_N.bit_length() - 1
    same = jnp.where(jnp.right_shift(i, shift) == jnp.right_shift(j, shift), 1.0, 0.0).astype(F32)
    return jnp.concatenate([_ones_right(x[:, g * LANES:(g + 1) * LANES], same) for g in range(x.shape[1] // LANES)], axis=1)


def _ones_right(x, ones):
    t = ones.astype(BF16)
    x1 = x.astype(BF16)
    x2 = (x - x1.astype(F32)).astype(BF16)
    dg = lambda q: jnp.dot(q, t, preferred_element_type=F32)
    return dg(x1) + dg(x2)


@jax.custom_vjp
def head_sum(x):
    return _head_dot(x)


def _head_sum_fwd(x):
    return head_sum(x), None


def _head_sum_bwd(_, g):
    return (_head_dot(g),)


head_sum.defvjp(_head_sum_fwd, _head_sum_bwd)


def cat_rows(*xs):
    return jnp.concatenate(xs, axis=0)


def _iota2(n, m):
    return lax.broadcasted_iota(jnp.int32, (n, m), 0), lax.broadcasted_iota(jnp.int32, (n, m), 1)


def tril_ones(c, strict=False):
    i, j = _iota2(c, c)
    return jnp.where((i > j) if strict else (i >= j), 1.0, 0.0).astype(F32)


def row_of(x, r):
    i = lax.broadcasted_iota(jnp.int32, x.shape, 0)
    return jnp.sum(jnp.where(i == r, x, 0.0), axis=0, keepdims=True)


@jax.custom_vjp
def shift_rows(x, prev):
    r = lax.broadcasted_iota(jnp.int32, x.shape, 0)
    return jnp.where(r == 0, prev, pltpu.roll(x, 1, 0))


def _shift_fwd(x, prev):
    return shift_rows(x, prev), None


def _shift_bwd(_, g):
    c = g.shape[0]
    r = lax.broadcasted_iota(jnp.int32, g.shape, 0)
    return jnp.where(r == c - 1, 0.0, pltpu.roll(g, c - 1, 0)), row_of(g, 0)


shift_rows.defvjp(_shift_fwd, _shift_bwd)


def log_sigmoid(x):
    return jnp.minimum(x, 0.0) - jnp.log(1.0 + jnp.exp(-jnp.abs(x)))


def softplus(x):
    return jnp.maximum(x, 0.0) + jnp.log(1.0 + jnp.exp(-jnp.abs(x)))


def sigmoid(x):
    return 1.0 / (1.0 + jnp.exp(-x))


def rms(x, w, eps=NORM_EPS):
    return x * lax.rsqrt(jnp.mean(x * x, axis=-1, keepdims=True) + eps) * w


def gla_chunk(state, toks, params):
    q, k, v, glow = toks
    gk_up, bias, norm_w = params
    c = GLA_CHUNK
    subs, heads = range(glow.shape[0] // c), range(GLA_HEADS)
    rows = lambda x, j: x[j * c:(j + 1) * c]
    hk = lambda x, h: x[:, h * GLA_DK:(h + 1) * GLA_DK]
    hv = lambda x, h: x[:, h * GLA_DV:(h + 1) * GLA_DV]
    ltri = tril_ones(c)
    g = log_sigmoid(mmx(glow, gk_up, "nn", P_LORA) + bias) / GLA_NORMALIZER
    b = [cumsum_rows(rows(g, j)) for j in subs]
    ref = [lax.stop_gradient(row_of(b[j], c // 2)) for j in subs]
    last = [row_of(b[j], c - 1) for j in subs]
    ql = [rows(q, j) * (GLA_DK ** -0.5) * jnp.exp(b[j] - ref[j]) for j in subs]
    kr = [rows(k, j) * jnp.exp(ref[j] - b[j]) for j in subs]
    kl = [rows(k, j) * jnp.exp(last[j] - b[j]) for j in subs]
    vj = [rows(v, j) for j in subs]
    e_ref, e_last = [jnp.exp(x) for x in ref], [jnp.exp(x) for x in last]
    att = [[mmx(hk(ql[j], h), hk(kr[j], h), "nt", P_GLA) * ltri for h in heads] for j in subs]
    o_in = [[mmx(att[j][h], hv(vj[j], h), "nn", P_GLA) for h in heads] for j in subs]
    kv = [[mmx(hv(vj[j], h), hk(kl[j], h), "tn", P_GLA) for h in heads] for j in subs]
    o = []
    for j in subs:
        o.append([o_in[j][h] + mmx(hk(ql[j], h), state[h] * hk(e_ref[j], h), "nt", P_GLA) for h in heads])
        state = [state[h] * hk(e_last[j], h) + kv[j][h] for h in heads]
    o = [[x * lax.rsqrt(jnp.mean(x * x, axis=-1, keepdims=True) + NORM_EPS) * norm_w for x in oj] for oj in o]
    return cat_rows(*[jnp.concatenate(oj, axis=1) for oj in o]), state


SOLVE_BLOCK = 128


def solve_unit_lower(ps, ws):
    n = ps[0].shape[0]
    heads = range(len(ps))
    if n > SOLVE_BLOCK:
        half = n // 2
        top = solve_unit_lower([p[:half, :half] for p in ps], [w[:half] for w in ws])
        rest = [ws[h][half:] + mmx(ps[h][half:, :half], top[h], "nn", P_RWKV) for h in heads]
        bottom = solve_unit_lower([p[half:, half:] for p in ps], rest)
        return [cat_rows(top[h], bottom[h]) for h in heads]
    u, p = ws, ps
    levels = max(1, (n - 1).bit_length())
    for it in range(levels):
        if it + 1 < levels:
            y = [mmx(p[h], jnp.concatenate([p[h], u[h]], axis=1), "nn", P_RWKV) for h in heads]
            u = [u[h] + y[h][:, n:] for h in heads]
            p = [y[h][:, :n] for h in heads]
        else:
            u = [u[h] + mmx(p[h], u[h], "nn", P_RWKV) for h in heads]
    return u


def rwkv_chunk(state, toks, params):
    S, pr, pk, pv, pxw, pxa = state
    r_, k_, v_, xw_, xa_ = toks
    mu_r, mu_k, mu_v, mu_xw, mu_xa, w0, w_up, a0, a_up, k_k, k_a, r_k, ln_w, ln_b = params
    c, n = xw_.shape[0], RWKV_N
    heads = range(RWKV_HEADS)
    hs = lambda x, h: x[:, h * n:(h + 1) * n]
    ltri = tril_ones(c)
    stri = tril_ones(c, strict=True)

    def lerp(x, prev, mu):
        return x + (shift_rows(x, prev) - x) * mu

    xw = jnp.tanh(lerp(xw_, pxw, mu_xw))
    xa = lerp(xa_, pxa, mu_xa)
    r = lerp(r_, pr, mu_r)
    k = lerp(k_, pk, mu_k)
    v = lerp(v_, pv, mu_v)
    w = -softplus(-(w0 + mmx(xw, w_up, "nn", P_LORA))) - 0.5
    lw = -jnp.exp(w)
    asig = sigmoid(a0 + mmx(xa, a_up, "nn", P_LORA))
    kk = k * k_k
    kk = kk * lax.rsqrt(jnp.maximum(head_sum(kk * kk), 1e-24))
    k2 = k * (1.0 + (asig - 1.0) * k_a)
    b = kk * asig
    cum = cumsum_rows(lw)
    ref = lax.stop_gradient(row_of(cum, c // 2))
    last = row_of(cum, c - 1)
    at = -kk * jnp.exp(cum - lw - ref)
    rt = r * jnp.exp(cum - ref)
    e_out = jnp.exp(ref - cum)
    bt, kt = b * e_out, k2 * e_out
    e_tail = jnp.exp(last - cum)
    bl, kl = b * e_tail, k2 * e_tail
    e_ref, e_last = jnp.exp(ref), jnp.exp(last)
    g = [mmx(cat_rows(hs(at, h), hs(rt, h)), cat_rows(hs(bt, h), hs(kt, h), S[h] * hs(e_ref, h)), "nt", P_RWKV_G) for h in heads]
    aab = [x[:c, :c] * stri for x in g]
    aak = [x[:c, c:2 * c] * stri for x in g]
    arb = [x[c:, :c] * ltri for x in g]
    ark = [x[c:, c:2 * c] * ltri for x in g]
    av = [mmx(cat_rows(aak[h], ark[h]), hs(v, h), "nn", P_RWKV) for h in heads]
    u = solve_unit_lower(aab, [g[h][:c, 2 * c:] + av[h][:c] for h in heads])
    o = [g[h][c:, 2 * c:] + av[h][c:] + mmx(arb[h], u[h], "nn", P_RWKV) for h in heads]
    s1 = [S[h] * hs(e_last, h) + mmx(cat_rows(u[h], hs(v, h)), cat_rows(hs(bl, h), hs(kl, h)), "tn", P_RWKV) for h in heads]
    o = jnp.concatenate(o, axis=1)
    d = o - head_sum(o) * (1.0 / n)
    var = head_sum(d * d) * (1.0 / n)
    o = d * lax.rsqrt(var + RWKV_LN_EPS) * ln_w + ln_b + head_sum(r * k2 * r_k) * v
    new_state = (s1, row_of(r_, c - 1), row_of(k_, c - 1), row_of(v_, c - 1), row_of(xw_, c - 1), row_of(xa_, c - 1))
    return o, new_state


ROPE_HALF = 8


def _rot_half_raw(x):
    lane = lax.broadcasted_iota(jnp.int32, (x.shape[0], LANES), 1) & (SWA_HD - 1)
    out = []
    for i in range(x.shape[1] // LANES):
        g = x[:, i * LANES:(i + 1) * LANES]
        up, down = pltpu.roll(g, LANES - ROPE_HALF, 1), pltpu.roll(g, ROPE_HALF, 1)
        out.append(jnp.where(lane < ROPE_HALF, -up, jnp.where(lane < 2 * ROPE_HALF, down, 0.0)))
    return out[0] if len(out) == 1 else jnp.concatenate(out, axis=1)


@jax.custom_vjp
def rot_half(x):
    return _rot_half_raw(x)


rot_half.defvjp(lambda x: (_rot_half_raw(x), None), lambda _, g: (-_rot_half_raw(g),))


def rope(x, cos2, sin2):
    reps = x.shape[1] // LANES
    tile = lambda t: t if reps == 1 else jnp.concatenate([t] * reps, axis=1)
    return x * tile(cos2) + rot_half(x) * tile(sin2)


def swa_chunk(state, toks, params, first):
    kprev, vprev = state
    q_, k_, v_, cos, sin = toks
    bq, bk, bv, sinks = params
    c, ng = WINDOW, SWA_GROUP
    n_sub = cos.shape[0] // c
    units = [(j, g) for j in range(n_sub) for g in range(SWA_KV_HEADS)]
    rows = lambda x, j: x[j * c:(j + 1) * c]
    hs = lambda g: range(g * ng, (g + 1) * ng)
    head = lambda x, h: x[:, h * SWA_HD:(h + 1) * SWA_HD]
    qi, kj = _iota2(ng * c, 2 * c)
    qpos = qi & (c - 1)
    cur_ok = (kj >= c) & (qpos >= kj - c)
    prev_ok = (kj < c) & (kj > qpos)
    ok = [cur_ok | (prev_ok & jnp.logical_not(first))] + [cur_ok | prev_ok] * (n_sub - 1)
    q_all = rope(q_ + bq, cos, sin) * (SWA_HD ** -0.5)
    k_all = rope(k_ + bk, cos, sin)
    v_all = v_ + bv
    k = {(j, g): rows(head(k_all, g), j) for j, g in units}
    v = {(j, g): rows(head(v_all, g), j) for j, g in units}
    q = {(j, g): cat_rows(*[rows(head(q_all, h), j) for h in hs(g)]) for j, g in units}
    kp = lambda j, g: kprev[g] if j == 0 else k[(j - 1, g)]
    vp = lambda j, g: vprev[g] if j == 0 else v[(j - 1, g)]
    s = {(j, g): jnp.where(ok[j], mmx(q[(j, g)], cat_rows(kp(j, g), k[(j, g)]), "nt", P_SWA), NEG) for j, g in units}
    sink = [cat_rows(*[jnp.broadcast_to(sinks[h], (c, 1)) for h in hs(g)]) for g in range(SWA_KV_HEADS)]
    m = {(j, g): lax.stop_gradient(jnp.maximum(jnp.max(s[(j, g)], axis=-1, keepdims=True), sink[g])) for j, g in units}
    p = {u: jnp.exp(s[u] - m[u]) for u in units}
    ones = jnp.ones((2 * c, SWA_HD), F32)
    pv = {(j, g): mmx(p[(j, g)], cat_rows(vp(j, g), v[(j, g)]), "nn", P_SWA) for j, g in units}
    den = {u: mmx(p[u], ones, "nn", P_SWA) for u in units}
    o = {(j, g): pv[(j, g)] / (den[(j, g)] + jnp.exp(sink[g] - m[(j, g)])) for j, g in units}
    outs = [cat_rows(*[o[(j, g)][i * c:(i + 1) * c] for j in range(n_sub)]) for g in range(SWA_KV_HEADS) for i in range(ng)]
    last = n_sub - 1
    return outs, ([k[(last, g)] for g in range(SWA_KV_HEADS)], [v[(last, g)] for g in range(SWA_KV_HEADS)])


def _heads(ref, n, w, rows=slice(None)):
    return [ref[rows, h * w:(h + 1) * w] for h in range(n)]


def _put_heads(ref, vals, w, rows=slice(None), add=False):
    for h, val in enumerate(vals):
        if add:
            ref[rows, h * w:(h + 1) * w] += val
        else:
            ref[rows, h * w:(h + 1) * w] = val


def _col(block_w, name, table):
    off, w = table[name]
    assert off % block_w == 0 and w % block_w == 0
    return off // block_w


def _tok_spec(c, w, colblock, n=None):
    if n is None:
        return pl.BlockSpec((c, w), lambda i: (i, colblock))
    return pl.BlockSpec((c, w), lambda i: (n - 1 - i, colblock))


def _full_spec(shape):
    return pl.BlockSpec(shape, lambda i: (0,) * len(shape))


def _matmul(name, a, b, mode, tm, tn, out_dtype=F32):
    (m, kd) = (a.shape[1], a.shape[0]) if mode == "tn" else a.shape
    n = b.shape[0] if mode == "nt" else b.shape[1]
    assert m % tm == 0 and n % tn == 0
    a_spec = pl.BlockSpec((kd, tm), lambda j, i: (0, i)) if mode == "tn" else pl.BlockSpec((tm, kd), lambda j, i: (i, 0))
    b_spec = pl.BlockSpec((tn, kd), lambda j, i: (j, 0)) if mode == "nt" else pl.BlockSpec((kd, tn), lambda j, i: (0, j))

    def body(a_ref, b_ref, o_ref):
        o_ref[...] = lax.dot_general(a_ref[...].astype(BF16), b_ref[...].astype(BF16), DIMS[mode],
                                     preferred_element_type=F32).astype(out_dtype)

    return pl.pallas_call(
        body, name=name, grid=(n // tn, m // tm), in_specs=[a_spec, b_spec],
        out_specs=pl.BlockSpec((tm, tn), lambda j, i: (i, j)), out_shape=jax.ShapeDtypeStruct((m, n), out_dtype),
        compiler_params=_cparams(("arbitrary", "arbitrary")))(a, b)


TOK_TILE = 512
PROJ_ROWS = 1024
DW_COLS = 512
ADAM_COLS = 256


def _matmul_fused(name, a, b, mode, tiles, rows, outs, sums, epilogue, comm=(), kinds=()):
    made = callable(a)
    m = tiles[0][0].shape[0] if made else a.shape[0]
    kd = b.shape[0] if mode == "nn" else b.shape[1]
    n = b.shape[1] if mode == "nn" else b.shape[0]
    tm = TOK_TILE
    steps = m // tm
    if made:
        outs = [(kd, BF16)] + list(outs)
    nt_, nr, no, ns, ncomm = len(tiles), len(rows), len(outs), len(sums), len(comm)

    def body(*refs):
        at = 1 if made else 2
        b_ref = refs[at - 1]
        tile_refs, row_refs, comm_in = refs[at:at + nt_], refs[at + nt_:at + nt_ + nr], refs[at + nt_ + nr:at + nt_ + nr + ncomm]
        at += nt_ + nr + ncomm
        out_refs, sum_refs, comm_out = refs[at:at + no], refs[at + no:at + no + ns], refs[at + no + ns:at + no + ns + ncomm]
        sems = refs[at + no + ns + ncomm:]
        i = pl.program_id(0)

        @pl.when(i == 0)
        def _():
            if ncomm:
                _comm_start(*_comm_copies(comm_in, comm_out, kinds, *sems))
            for ref in sum_refs:
                ref[...] = jnp.zeros_like(ref)

        extras = [r[...] for r in tile_refs] + [r[...] for r in row_refs]
        a_blk = (a(*extras) if made else refs[0][...]).astype(BF16)
        acc = lax.dot_general(a_blk, b_ref[...].astype(BF16), DIMS[mode], preferred_element_type=F32)
        res = epilogue(acc, *extras)
        if made:
            res = (a_blk,) + tuple(res)
        for ref, val in zip(out_refs, res[:no]):
            ref[...] = val.astype(ref.dtype)
        for ref, val in zip(sum_refs, res[no:]):
            ref[...] += val

        if ncomm:
            @pl.when(i == steps - 1)
            def _():
                _comm_wait(*_comm_copies(comm_in, comm_out, kinds, *sems))

    in_specs = ([] if made else [pl.BlockSpec((tm, kd), lambda i: (i, 0))]) + [_full_spec(b.shape)]
    in_specs += [pl.BlockSpec((tm, w), functools.partial(lambda i, cb: (i, cb), cb=cb)) for _, w, cb in tiles]
    in_specs += [_full_spec(r.shape) for r in rows] + [ANY] * ncomm
    out_specs = [pl.BlockSpec((tm, w), lambda i: (i, 0)) for w, _ in outs] + [_full_spec((1, w)) for w in sums] + [ANY] * ncomm
    out_shape = ([jax.ShapeDtypeStruct((m, w), dt) for w, dt in outs] + [jax.ShapeDtypeStruct((1, w), F32) for w in sums]
                 + _comm_out_shapes(comm, kinds))
    return pl.pallas_call(body, name=name, grid=(steps,), in_specs=in_specs, out_specs=out_specs, out_shape=out_shape,
                          scratch_shapes=_comm_scratch(ncomm) if ncomm else [],
                          compiler_params=_cparams(("arbitrary",)))(*([] if made else [a]), b, *[t[0] for t in tiles], *rows, *comm)


def _resid_norm(y, x, w):
    h = x + y
    return h, rms(h, w)


def _norm_back(dhn, h, dres, w):
    _, vjp = jax.vjp(rms, h, w)
    dh, dw = vjp(dhn)
    return dh + dres, dw


def _gate_back(dog, *o_and_gate):
    outs, g = o_and_gate[:-1], o_and_gate[-1]
    s = sigmoid(g)
    silu, dsilu = g * s, s * (1.0 + g * (1.0 - s))
    d_outs, c = [], 0
    for o in outs:
        w = o.shape[1]
        d_outs.append(dog[:, c:c + w] * silu[:, c:c + w])
        c += w
    o_all = outs[0] if len(outs) == 1 else jnp.concatenate(outs, axis=1)
    return (*d_outs, dog * o_all * dsilu)


def _loss_head(y1, h1, target, b_out, fw):
    def f(h2, w):
        err = rms(h2, w) - target
        return 0.5 * jnp.sum(jnp.mean(err * err, axis=-1, keepdims=True), axis=0, keepdims=True)

    loss, vjp = jax.vjp(f, h1 + y1 + b_out, fw)
    dh2, dfw = vjp(jnp.ones((1, 1), F32))
    return dh2, jnp.broadcast_to(loss, (1, LANES)), jnp.sum(dh2, axis=0, keepdims=True), dfw


def _gla_load(q_ref, k_ref, v_ref, gl_ref, up_ref, bias_ref, nw_ref):
    toks = (q_ref[...], k_ref[...], v_ref[...], gl_ref[...])
    params = (up_ref[...], bias_ref[...], nw_ref[...])
    return toks, params


def _gla_specs(c, n=None):
    toks = [_tok_spec(c, GLA_KEY, _col(GLA_KEY, "gq", C0), n), _tok_spec(c, GLA_KEY, _col(GLA_KEY, "gk", C0), n),
            _tok_spec(c, GLA_VAL, _col(GLA_VAL, "gv", C0), n), _tok_spec(c, LOW, _col(LOW, "glow", C0), n)]
    return toks, [_full_spec(s) for s in GLA_PARAM_SHAPES]


GLA_PARAM_SHAPES = [(LOW, GLA_KEY), (1, GLA_KEY), (1, GLA_DV)]
GLA_STATE = (GLA_HEADS * GLA_DV, GLA_DK)


def _gla_fwd(proj0, gk_up, gk_bias, norm_w):
    t = proj0.shape[0]
    c = GLA_STEP
    nc = t // c
    toks_s, params_s = _gla_specs(c)

    def body(q_ref, k_ref, v_ref, gl_ref, up_ref, bias_ref, nw_ref, o_ref, st_ref, s_scr):
        @pl.when(pl.program_id(0) == 0)
        def _():
            s_scr[...] = jnp.zeros_like(s_scr)

        st_ref[...] = s_scr[...]
        toks, params = _gla_load(q_ref, k_ref, v_ref, gl_ref, up_ref, bias_ref, nw_ref)
        state = [s_scr[h * GLA_DV:(h + 1) * GLA_DV, :] for h in range(GLA_HEADS)]
        o_ref[...], new = gla_chunk(state, toks, params)
        for h in range(GLA_HEADS):
            s_scr[h * GLA_DV:(h + 1) * GLA_DV, :] = new[h]

    return pl.pallas_call(
        body, name="gla_fwd", grid=(nc,), in_specs=toks_s + params_s,
        out_specs=(_tok_spec(c, GLA_VAL, 0), pl.BlockSpec(GLA_STATE, lambda i: (i, 0))),
        out_shape=(jax.ShapeDtypeStruct((t, GLA_VAL), F32), jax.ShapeDtypeStruct((nc * GLA_STATE[0], GLA_DK), F32)),
        scratch_shapes=[pltpu.VMEM(GLA_STATE, F32)], compiler_params=_cparams(("arbitrary",)))(
            proj0, proj0, proj0, proj0, gk_up, gk_bias, norm_w)


def _gla_bwd(proj0, gk_up, gk_bias, norm_w, states, do):
    t = proj0.shape[0]
    c = GLA_STEP
    nc = t // c
    toks_s, params_s = _gla_specs(c, nc)

    def body(q_ref, k_ref, v_ref, gl_ref, up_ref, bias_ref, nw_ref, st_ref, do_ref,
             dq_ref, dk_ref, dv_ref, dgl_ref, dup_ref, dbias_ref, dnw_ref, ds_scr):
        @pl.when(pl.program_id(0) == 0)
        def _():
            ds_scr[...] = jnp.zeros_like(ds_scr)
            dup_ref[...] = jnp.zeros_like(dup_ref)
            dbias_ref[...] = jnp.zeros_like(dbias_ref)
            dnw_ref[...] = jnp.zeros_like(dnw_ref)

        toks, params = _gla_load(q_ref, k_ref, v_ref, gl_ref, up_ref, bias_ref, nw_ref)
        rows = lambda h: slice(h * GLA_DV, (h + 1) * GLA_DV)
        state = [st_ref[rows(h), :] for h in range(GLA_HEADS)]
        _, vjp = jax.vjp(gla_chunk, state, toks, params)
        dstate_in = [ds_scr[rows(h), :] for h in range(GLA_HEADS)]
        dstate, dtoks, (dup, dbias, dnw) = vjp((do_ref[...], dstate_in))
        for ref, val in zip((dq_ref, dk_ref, dv_ref, dgl_ref), dtoks):
            ref[...] = val.astype(ref.dtype)
        dup_ref[...] += dup
        dbias_ref[...] += dbias
        dnw_ref[...] += dnw
        for h in range(GLA_HEADS):
            ds_scr[rows(h), :] = dstate[h]

    rev = lambda w: pl.BlockSpec((c, w), lambda i: (nc - 1 - i, 0))
    tok_widths = (GLA_KEY, GLA_KEY, GLA_VAL, LOW)
    return pl.pallas_call(
        body, name="gla_bwd", grid=(nc,),
        in_specs=toks_s + params_s + [pl.BlockSpec(GLA_STATE, lambda i: (nc - 1 - i, 0)), rev(GLA_VAL)],
        out_specs=[rev(w) for w in tok_widths] + params_s,
        out_shape=[jax.ShapeDtypeStruct((t, w), BF16) for w in tok_widths] + [jax.ShapeDtypeStruct(s, F32) for s in GLA_PARAM_SHAPES],
        scratch_shapes=[pltpu.VMEM(GLA_STATE, F32)], compiler_params=_cparams(("arbitrary",)))(
            proj0, proj0, proj0, proj0, gk_up, gk_bias, norm_w, states, do)


RWKV_PARAM_SHAPES = [(1, RWKV_W), (1, RWKV_W), (1, RWKV_W), (1, LOW), (1, LOW), (1, RWKV_W), (LOW, RWKV_W), (1, RWKV_W),
                     (LOW, RWKV_W), (1, RWKV_W), (1, RWKV_W), (1, RWKV_W), (1, RWKV_W), (1, RWKV_W)]
RWKV_STATE = (RWKV_HEADS * RWKV_N, RWKV_N)
RWKV_TOK_WIDTHS = (RWKV_W, RWKV_W, RWKV_W, LOW, LOW)
PREV_W = sum(RWKV_TOK_WIDTHS)
PREV_COLS = [slice(sum(RWKV_TOK_WIDTHS[:i]), sum(RWKV_TOK_WIDTHS[:i + 1])) for i in range(len(RWKV_TOK_WIDTHS))]


def _rwkv_load(r_ref, k_ref, v_ref, xw_ref, xa_ref, p_refs):
    toks = (r_ref[...], k_ref[...], v_ref[...], xw_ref[...], xa_ref[...])
    return toks, tuple(p[...] for p in p_refs)


def _rwkv_state(s_ref, prev_ref):
    n = RWKV_N
    S = [s_ref[h * n:(h + 1) * n, :] for h in range(RWKV_HEADS)]
    return (S,) + tuple(prev_ref[0:1, cols] for cols in PREV_COLS)


def _rwkv_put_state(s_ref, prev_ref, state):
    n = RWKV_N
    for h in range(RWKV_HEADS):
        s_ref[h * n:(h + 1) * n, :] = state[0][h]
    for cols, val in zip(PREV_COLS, state[1:]):
        prev_ref[0:1, cols] = val


def _rwkv_specs(c, n=None):
    toks = [_tok_spec(c, w, _col(w, name, C0), n) for name, w in zip(("r", "k", "v", "xw", "xa"), RWKV_TOK_WIDTHS)]
    return toks, [_full_spec(s) for s in RWKV_PARAM_SHAPES]


def _rwkv_fwd(proj0, params, comm, kinds):
    t = proj0.shape[0]
    c = RWKV_CHUNK
    nc = t // c
    toks_s, params_s = _rwkv_specs(c)
    npar, ncomm = len(params), len(comm)

    def body(*refs):
        tok_refs, p_refs = refs[:5], refs[5:5 + npar]
        comm_in = refs[5 + npar:5 + npar + ncomm]
        o_ref, st_ref, pst_ref = refs[5 + npar + ncomm:8 + npar + ncomm]
        comm_out = refs[8 + npar + ncomm:8 + npar + 2 * ncomm]
        s_scr, prev_scr = refs[8 + npar + 2 * ncomm:10 + npar + 2 * ncomm]
        sems = refs[10 + npar + 2 * ncomm:]
        i = pl.program_id(0)

        @pl.when(i == 0)
        def _():
            _comm_start(*_comm_copies(comm_in, comm_out, kinds, *sems))
            s_scr[...] = jnp.zeros_like(s_scr)
            prev_scr[...] = jnp.zeros_like(prev_scr)

        st_ref[...] = s_scr[...]
        pst_ref[...] = prev_scr[...]
        toks, prm = _rwkv_load(*tok_refs, p_refs)
        o_ref[...], new = rwkv_chunk(_rwkv_state(s_scr, prev_scr), toks, prm)
        _rwkv_put_state(s_scr, prev_scr, new)

        @pl.when(i == nc - 1)
        def _():
            _comm_wait(*_comm_copies(comm_in, comm_out, kinds, *sems))

    outs = pl.pallas_call(
        body, name="rwkv_fwd", grid=(nc,), in_specs=toks_s + params_s + [ANY] * ncomm,
        out_specs=[_tok_spec(c, RWKV_W, 0), pl.BlockSpec(RWKV_STATE, lambda i: (i, 0)), pl.BlockSpec((8, PREV_W), lambda i: (i, 0))]
        + [ANY] * ncomm,
        out_shape=[jax.ShapeDtypeStruct((t, RWKV_W), F32), jax.ShapeDtypeStruct((nc * RWKV_STATE[0], RWKV_N), F32),
                   jax.ShapeDtypeStruct((nc * 8, PREV_W), F32)] + _comm_out_shapes(comm, kinds),
        scratch_shapes=[pltpu.VMEM(RWKV_STATE, F32), pltpu.VMEM((8, PREV_W), F32)] + _comm_scratch(ncomm),
        compiler_params=_cparams(("arbitrary",)))(proj0, proj0, proj0, proj0, proj0, *params, *comm)
    return outs[0], outs[1], outs[2], outs[3:]


def _rwkv_bwd(proj0, params, states, prevs, do, comm, kinds):
    t = proj0.shape[0]
    c = RWKV_CHUNK
    nc = t // c
    toks_s, params_s = _rwkv_specs(c, nc)
    npar, ncomm = len(params), len(comm)

    def body(*refs):
        tok_refs, p_refs = refs[:5], refs[5:5 + npar]
        st_ref, pst_ref, do_ref = refs[5 + npar:8 + npar]
        comm_in = refs[8 + npar:8 + npar + ncomm]
        outs = refs[8 + npar + ncomm:]
        dtok_refs, dp_refs, comm_out = outs[:5], outs[5:5 + npar], outs[5 + npar:5 + npar + ncomm]
        ds_scr, dprev_scr = outs[5 + npar + ncomm:7 + npar + ncomm]
        sems = outs[7 + npar + ncomm:]
        i = pl.program_id(0)

        @pl.when(i == 0)
        def _():
            _comm_start(*_comm_copies(comm_in, comm_out, kinds, *sems))
            ds_scr[...] = jnp.zeros_like(ds_scr)
            dprev_scr[...] = jnp.zeros_like(dprev_scr)
            for dp in dp_refs:
                dp[...] = jnp.zeros_like(dp)

        toks, prm = _rwkv_load(*tok_refs, p_refs)
        _, vjp = jax.vjp(rwkv_chunk, _rwkv_state(st_ref, pst_ref), toks, prm)
        dstate, dtoks, dprm = vjp((do_ref[...], _rwkv_state(ds_scr, dprev_scr)))
        for ref, val in zip(dtok_refs, dtoks):
            ref[...] = val.astype(ref.dtype)
        for ref, val in zip(dp_refs, dprm):
            ref[...] += val
        _rwkv_put_state(ds_scr, dprev_scr, dstate)

        @pl.when(i == nc - 1)
        def _():
            _comm_wait(*_comm_copies(comm_in, comm_out, kinds, *sems))

    rev = lambda w: pl.BlockSpec((c, w), lambda i: (nc - 1 - i, 0))
    outs = pl.pallas_call(
        body, name="rwkv_bwd", grid=(nc,),
        in_specs=toks_s + params_s + [pl.BlockSpec(RWKV_STATE, lambda i: (nc - 1 - i, 0)),
                                      pl.BlockSpec((8, PREV_W), lambda i: (nc - 1 - i, 0)), rev(RWKV_W)] + [ANY] * ncomm,
        out_specs=[rev(w) for w in RWKV_TOK_WIDTHS] + params_s + [ANY] * ncomm,
        out_shape=[jax.ShapeDtypeStruct((t, w), BF16) for w in RWKV_TOK_WIDTHS]
        + [jax.ShapeDtypeStruct(s, F32) for s in RWKV_PARAM_SHAPES] + _comm_out_shapes(comm, kinds),
        scratch_shapes=[pltpu.VMEM(RWKV_STATE, F32), pltpu.VMEM((8, PREV_W), F32)] + _comm_scratch(ncomm),
        compiler_params=_cparams(("arbitrary",)))(proj0, proj0, proj0, proj0, proj0, *params, states, prevs, do, *comm)
    return outs[:5], outs[5:5 + npar], outs[5 + npar:]


def _swa_load(q_ref, k_ref, v_ref, cos_ref, sin_ref, bq_ref, bk_ref, bv_ref, sk_ref):
    toks = (q_ref[...], k_ref[...], v_ref[...], cos_ref[...], sin_ref[...])
    params = (bq_ref[...], bk_ref[...], bv_ref[...], _heads(sk_ref, SWA_Q_HEADS, 1))
    return toks, params


SWA_TOK_WIDTHS = (MIX, SWA_KV, SWA_KV)
SWA_PARAM_SHAPES = [(1, MIX), (1, SWA_KV), (1, SWA_KV), (1, SWA_Q_HEADS)]
SWA_STATE = (WINDOW, SWA_KV)


def _swa_specs(c, n=None):
    toks = [_tok_spec(c, w, _col(w, name, C1), n) for name, w in zip(("q", "k", "v"), SWA_TOK_WIDTHS)]
    toks += [_tok_spec(c, LANES, 0, n), _tok_spec(c, LANES, 0, n)]
    return toks, [_full_spec(s) for s in SWA_PARAM_SHAPES]


def _swa_fwd(proj1, cos, sin, bq, bk, bv, sinks):
    t = proj1.shape[0]
    c = SWA_STEP
    nb = t // c
    toks_s, params_s = _swa_specs(c)
    state_spec = pl.BlockSpec(SWA_STATE, lambda i: (i, 0))
    kv = SWA_KV_HEADS

    def body(q_ref, k_ref, v_ref, cos_ref, sin_ref, bq_ref, bk_ref, bv_ref, sk_ref, o_ref, kst_ref, vst_ref, k_scr, v_scr):
        first = pl.program_id(0) == 0

        @pl.when(first)
        def _():
            k_scr[...] = jnp.zeros_like(k_scr)
            v_scr[...] = jnp.zeros_like(v_scr)

        kst_ref[...] = k_scr[...]
        vst_ref[...] = v_scr[...]
        toks, params = _swa_load(q_ref, k_ref, v_ref, cos_ref, sin_ref, bq_ref, bk_ref, bv_ref, sk_ref)
        outs, (kn, vn) = swa_chunk((_heads(k_scr, kv, SWA_HD), _heads(v_scr, kv, SWA_HD)), toks, params, first)
        _put_heads(o_ref, outs, SWA_HD)
        _put_heads(k_scr, kn, SWA_HD)
        _put_heads(v_scr, vn, SWA_HD)

    saved = jax.ShapeDtypeStruct((nb * WINDOW, SWA_KV), F32)
    return pl.pallas_call(
        body, name="swa_fwd", grid=(nb,), in_specs=toks_s + params_s,
        out_specs=(_tok_spec(c, MIX, 0), state_spec, state_spec),
        out_shape=(jax.ShapeDtypeStruct((t, MIX), F32), saved, saved),
        scratch_shapes=[pltpu.VMEM(SWA_STATE, F32), pltpu.VMEM(SWA_STATE, F32)],
        compiler_params=_cparams(("arbitrary",)))(proj1, proj1, proj1, cos, sin, bq, bk, bv, sinks)


def _swa_bwd(proj1, cos, sin, bq, bk, bv, sinks, kst, vst, do):
    t = proj1.shape[0]
    c = SWA_STEP
    nb = t // c
    toks_s, params_s = _swa_specs(c, nb)
    state_spec = pl.BlockSpec(SWA_STATE, lambda i: (nb - 1 - i, 0))
    kv = SWA_KV_HEADS

    def body(q_ref, k_ref, v_ref, cos_ref, sin_ref, bq_ref, bk_ref, bv_ref, sk_ref, kst_ref, vst_ref, do_ref,
             dq_ref, dk_ref, dv_ref, dbq_ref, dbk_ref, dbv_ref, dsk_ref, dk_scr, dv_scr):
        i = pl.program_id(0)

        @pl.when(i == 0)
        def _():
            dk_scr[...] = jnp.zeros_like(dk_scr)
            dv_scr[...] = jnp.zeros_like(dv_scr)
            for ref in (dbq_ref, dbk_ref, dbv_ref, dsk_ref):
                ref[...] = jnp.zeros_like(ref)

        first = i == nb - 1
        toks, params = _swa_load(q_ref, k_ref, v_ref, cos_ref, sin_ref, bq_ref, bk_ref, bv_ref, sk_ref)
        f = functools.partial(swa_chunk, first=first)
        _, vjp = jax.vjp(f, (_heads(kst_ref, kv, SWA_HD), _heads(vst_ref, kv, SWA_HD)), toks, params)
        dstate_in = (_heads(dk_scr, kv, SWA_HD), _heads(dv_scr, kv, SWA_HD))
        (dkp, dvp), (dq, dk, dv, _, _), (dbq, dbk, dbv, dsk) = vjp((_heads(do_ref, SWA_Q_HEADS, SWA_HD), dstate_in))
        dq_ref[...], dk_ref[...], dv_ref[...] = dq.astype(BF16), dk.astype(BF16), dv.astype(BF16)
        dbq_ref[...] += dbq
        dbk_ref[...] += dbk
        dbv_ref[...] += dbv
        _put_heads(dsk_ref, dsk, 1, add=True)
        _put_heads(dk_scr, dkp, SWA_HD)
        _put_heads(dv_scr, dvp, SWA_HD)

    rev = lambda w: pl.BlockSpec((c, w), lambda i: (nb - 1 - i, 0))
    return pl.pallas_call(
        body, name="swa_bwd", grid=(nb,), in_specs=toks_s + params_s + [state_spec, state_spec, rev(MIX)],
        out_specs=[rev(w) for w in SWA_TOK_WIDTHS] + params_s,
        out_shape=[jax.ShapeDtypeStruct((t, w), BF16) for w in SWA_TOK_WIDTHS] + [jax.ShapeDtypeStruct(s, F32) for s in SWA_PARAM_SHAPES],
        scratch_shapes=[pltpu.VMEM(SWA_STATE, F32), pltpu.VMEM(SWA_STATE, F32)],
        compiler_params=_cparams(("arbitrary",)))(proj1, proj1, proj1, cos, sin, bq, bk, bv, sinks, kst, vst, do)


MESH = pl.DeviceIdType.MESH
ANY = pl.BlockSpec(memory_space=pl.ANY)


def _my_place():
    return lax.axis_index("x"), lax.axis_index("y"), lax.axis_index("c")


def _all_gather(shards):
    n = len(shards)

    def body(*refs):
        in_refs, out_refs = refs[:n], refs[n:2 * n]
        send_sems, recv_sems, local_sems = refs[2 * n:]
        x, y, c = _my_place()
        me, sibling = (x, y, c), (x, y, 1 - c)
        chips = [(1 - x, y), (x, 1 - y), (1 - x, 1 - y)]

        def slot(out_ref, place):
            px, py, pc = place
            return out_ref.at[4 * px + 2 * py + pc]

        def copy(a, k, block, to, src=None):
            return pltpu.make_async_remote_copy(
                src_ref=slot(out_refs[a], block) if src is None else src, dst_ref=slot(out_refs[a], block),
                send_sem=send_sems.at[a, k], recv_sem=recv_sems.at[a, k], device_id=to, device_id_type=MESH)

        mine = [pltpu.make_async_copy(in_refs[a], slot(out_refs[a], me), local_sems.at[a]) for a in range(n)]
        for cp in mine:
            cp.start()
        first = []
        for a in range(n):
            first.append(copy(a, 0, me, sibling, src=in_refs[a]))
            first += [copy(a, 1 + j, me, (*chip, c), src=in_refs[a]) for j, chip in enumerate(chips)]
        for cp in first:
            cp.start()
        passed = []
        for j, chip in enumerate(chips):
            for a in range(n):
                copy(a, 1 + j, (*chip, c), me).wait_recv()
                fwd = copy(a, 4 + j, (*chip, c), sibling)
                fwd.start()
                passed.append(fwd)
        for a in range(n):
            copy(a, 0, sibling, me).wait_recv()
            for j, chip in enumerate(chips):
                copy(a, 4 + j, (*chip, 1 - c), me).wait_recv()
        for cp in first + passed:
            cp.wait_send()
        for cp in mine:
            cp.wait()

    return pl.pallas_call(
        body, name="all_gather_weights", in_specs=[ANY] * n, out_specs=[ANY] * n,
        out_shape=[jax.ShapeDtypeStruct((N_DEV,) + s.shape, s.dtype) for s in shards],
        scratch_shapes=_comm_scratch(n))(*shards)


def _comm_copies(in_refs, out_refs, kinds, send_sems, recv_sems, local_sems):
    x, y, c = _my_place()
    my_idx = 4 * x + 2 * y + c
    src = lambda a, idx: in_refs[a] if kinds[a] == "gather" else in_refs[a].at[idx]
    local = [pltpu.make_async_copy(src(a, my_idx), out_refs[a].at[my_idx], local_sems.at[a]) for a in range(len(kinds))]
    remote = []
    for rel in range(1, N_DEV):
        px, py, pc = x ^ ((rel >> 2) & 1), y ^ ((rel >> 1) & 1), c ^ (rel & 1)
        for a in range(len(kinds)):
            remote.append(pltpu.make_async_remote_copy(
                src_ref=src(a, 4 * px + 2 * py + pc), dst_ref=out_refs[a].at[my_idx], send_sem=send_sems.at[a, rel - 1],
                recv_sem=recv_sems.at[a, rel - 1], device_id=(px, py, pc), device_id_type=MESH))
    return local, remote


def _comm_start(local, remote):
    for cp in local + remote:
        cp.start()


def _comm_wait(local, remote):
    for cp in remote:
        cp.wait_recv()
    for cp in remote:
        cp.wait_send()
    for cp in local:
        cp.wait()


def _comm_out_shapes(arrays, kinds):
    return [jax.ShapeDtypeStruct(((N_DEV,) + a.shape) if k == "gather" else a.shape, a.dtype) for a, k in zip(arrays, kinds)]


def _comm_scratch(n):
    return [pltpu.SemaphoreType.DMA((n, N_DEV - 1)), pltpu.SemaphoreType.DMA((n, N_DEV - 1)), pltpu.SemaphoreType.DMA((n,))]


def _sequencer_scatter(name, parts, collective_id):
    src = jax.new_ref(parts, memory_space=pltpu.MemorySpace.HBM)
    dst = jax.empty_ref(jax.ShapeDtypeStruct(parts.shape, parts.dtype), memory_space=pltpu.MemorySpace.HBM)

    @pl.kernel(mesh=plsc.ScalarSubcoreMesh(axis_name="sequencer", num_cores=1), name=name,
               scratch_types=(pltpu.SemaphoreType.DMA((N_DEV - 1,)), pltpu.SemaphoreType.DMA((N_DEV - 1,))),
               compiler_params=pltpu.CompilerParams(collective_id=collective_id))
    def launch(send_sems, recv_sems):
        x, y, c = _my_place()
        my_idx = 4 * x + 2 * y + c
        peers = [(x ^ ((rel >> 2) & 1), y ^ ((rel >> 1) & 1), c ^ (rel & 1)) for rel in range(1, N_DEV)]
        barrier = pltpu.get_barrier_semaphore()
        for peer in peers:
            pl.semaphore_signal(barrier, inc=1, device_id=peer, device_id_type=MESH)
        pl.semaphore_wait(barrier, N_DEV - 1)
        copies = [pltpu.make_async_remote_copy(
            src_ref=src.at[4 * px + 2 * py + pc], dst_ref=dst.at[my_idx], send_sem=send_sems.at[k], recv_sem=recv_sems.at[k],
            device_id=(px, py, pc), device_id_type=MESH) for k, (px, py, pc) in enumerate(peers)]
        for cp in copies:
            cp.start()
        for cp in copies:
            cp.wait_recv()
        for cp in copies:
            cp.wait_send()

    launch()
    return dst[...]


def _exchange(arrays, kinds):
    n = len(arrays)

    def body(*refs):
        copies = _comm_copies(refs[:n], refs[n:2 * n], kinds, *refs[2 * n:])
        _comm_start(*copies)
        _comm_wait(*copies)

    return pl.pallas_call(body, name="exchange_grads", in_specs=[ANY] * n, out_specs=[ANY] * n,
                          out_shape=_comm_out_shapes(arrays, kinds), scratch_shapes=_comm_scratch(n))(*arrays)


def _adam_math(w, g, m, v):
    m = ADAM_B1 * m + (1.0 - ADAM_B1) * g
    v = ADAM_B2 * v + (1.0 - ADAM_B2) * (g * g)
    m_hat = m / (1.0 - ADAM_B1 ** ADAM_STEP)
    v_hat = v / (1.0 - ADAM_B2 ** ADAM_STEP)
    delta = -ADAM_LR * (m_hat / (jnp.sqrt(v_hat) + ADAM_EPS) + ADAM_WD * w)
    return delta, m, v


def _adamw(name, w, gslots, m, v, tc):
    r, cc = w.shape
    assert cc % tc == 0
    tile = pl.BlockSpec((r, tc), lambda i: (0, i))

    def body(w_ref, g_ref, m_ref, v_ref, go_ref, d_ref, mo_ref, vo_ref):
        g = g_ref[0].astype(F32)
        for s in range(1, N_DEV):
            g = g + g_ref[s].astype(F32)
        d, mn, vn = _adam_math(w_ref[...], g, m_ref[...], v_ref[...])
        go_ref[...] = g
        d_ref[...] = d
        mo_ref[...] = mn
        vo_ref[...] = vn

    shp = jax.ShapeDtypeStruct((r, cc), F32)
    return pl.pallas_call(body, name=name, grid=(cc // tc,),
                          in_specs=[tile, pl.BlockSpec((N_DEV, r, tc), lambda i: (0, 0, i)), tile, tile],
                          out_specs=(tile,) * 4, out_shape=(shp,) * 4, compiler_params=_cparams(("arbitrary",)))(w, gslots, m, v)


PACK_TILE = 8 * LANES


def _packed_rows(shape, mode):
    r, w = shape
    return -(-r // 8) * 8 if mode == "rows" else -(-(r * w) // PACK_TILE) * 8


def _pack_small(arrays, modes, lead=False):
    out = []
    for a, mode in zip(arrays, modes):
        a = a.astype(F32) if lead else a.astype(F32)[None]
        if mode == "rows":
            out.append(jnp.pad(a, ((0, 0), (0, (-a.shape[1]) % 8), (0, LANES - a.shape[2]))))
        else:
            flat = a.reshape(a.shape[0], -1)
            out.append(jnp.pad(flat, ((0, 0), (0, (-flat.shape[1]) % PACK_TILE))).reshape(a.shape[0], -1, LANES))
    out = jnp.concatenate(out, axis=1)
    return out if lead else out[0]


def _take_small(packed, row0, shape, mode):
    r, w = shape
    lead = packed.ndim == 3
    if mode == "rows":
        return packed[:, row0:row0 + r, :w] if lead else packed[row0:row0 + r, :w]
    per_row = -(-w // LANES)
    if lead:
        return packed[:, row0:row0 + r * per_row].reshape(packed.shape[0], r, per_row * LANES)[:, :, :w]
    rows = []
    for i in range(r):
        pieces = [packed[row0 + i * per_row + j:row0 + i * per_row + j + 1, :] for j in range(per_row)]
        rows.append((pieces[0] if per_row == 1 else jnp.concatenate(pieces, axis=1))[:, :w])
    return rows[0] if r == 1 else jnp.concatenate(rows, axis=0)


def _adamw_small(slots, specs, ws, ms, vs, loss_row):
    n = len(specs)

    def body(*refs):
        slots_ref, w_refs, m_refs, v_refs = refs[0], refs[1:1 + n], refs[1 + n:1 + 2 * n], refs[1 + 2 * n:1 + 3 * n]
        out_refs, loss_ref = refs[1 + 3 * n:1 + 7 * n], refs[1 + 7 * n]
        gp = slots_ref[0]
        for s in range(1, N_DEV):
            gp = gp + slots_ref[s]
        read = lambda ref: ref[0] if len(ref.shape) == 3 else ref[...]
        for k, (shape, mode, row0) in enumerate(specs):
            g = _take_small(gp, row0, shape, mode)
            d, mn, vn = _adam_math(read(w_refs[k]), g, read(m_refs[k]), read(v_refs[k]))
            for ref, val in zip(out_refs[4 * k:4 * k + 4], (g, d, mn, vn)):
                if len(ref.shape) == 3:
                    ref[0] = val
                else:
                    ref[...] = val
        loss_ref[...] = gp[loss_row:loss_row + 1, :]

    vmem = pl.BlockSpec(memory_space=pltpu.VMEM)
    out_shape = [jax.ShapeDtypeStruct(w.shape, F32) for w in ws for _ in range(4)] + [jax.ShapeDtypeStruct((1, LANES), F32)]
    outs = pl.pallas_call(body, name="adamw_small", in_specs=[vmem] * (1 + 3 * n), out_specs=[vmem] * (4 * n + 1),
                          out_shape=out_shape)(slots, *ws, *ms, *vs)
    return [outs[4 * k:4 * k + 4] for k in range(n)], outs[4 * n]


def _rope_tables(t):
    dim = jnp.arange(LANES) % SWA_HD
    inv_freq = ROPE_THETA ** (-(dim % ROPE_HALF).astype(F32) / ROPE_HALF)
    ang = jnp.arange(t, dtype=F32)[:, None] * jnp.where(dim < 2 * ROPE_HALF, inv_freq, 0.0)[None, :]
    return jnp.cos(ang), jnp.sin(ang)


def _pad_to(a, rows=None, cols=None):
    r = 0 if rows is None else rows - a.shape[0]
    c = 0 if cols is None else cols - a.shape[1]
    return jnp.pad(a, ((0, r), (0, c)))


ORIG0 = dict(gq=(0, 256), gk=(256, 256), gv=(512, 512), glow=(1024, 16), r=(1040, 512), k=(1552, 512), v=(2064, 512),
             xw=(2576, 64), xa=(2640, 64), gate=(2704, 1024))
ORIG0_ORDER = ["gq", "gk", "gv", "glow", "r", "k", "v", "xw", "xa", "gate"]


def _w0t_to_padded(wt):
    rows, at = [], 0
    for name, (off, width) in sorted(C0.items(), key=lambda kv: kv[1][0]):
        assert off == at
        src, src_w = ORIG0[name]
        rows.append(_pad_to(wt[src:src + src_w], rows=width))
        at += width
    rows.append(jnp.zeros((N0P - at, wt.shape[1]), wt.dtype))
    return jnp.concatenate(rows, axis=0)


def _w0t_from_padded(wpt):
    return jnp.concatenate([wpt[C0[n][0]:C0[n][0] + ORIG0[n][1]] for n in ORIG0_ORDER], axis=0)


def _w1t_to_mine(wt):
    return jnp.concatenate([wt[1536:2560], wt[:1536]], axis=0)


def _w1t_from_mine(wt):
    return jnp.concatenate([wt[1024:2560], wt[:1024]], axis=0)


def kernel(x, norm_w, w_in0, gla_gk_up, gla_gk_bias, gla_norm_w, rwkv_mu, rwkv_w0, rwkv_w_up, rwkv_a0, rwkv_a_up, rwkv_k_k, rwkv_k_a, rwkv_r_k, rwkv_ln_w, rwkv_ln_b, w_out0, w_in1, b_in1, attn_sinks, w_out1, b_out1, final_norm_w, loss_target, m_norm_w, m_w_in0, m_gla_gk_up, m_gla_gk_bias, m_gla_norm_w, m_rwkv_mu, m_rwkv_w0, m_rwkv_w_up, m_rwkv_a0, m_rwkv_a_up, m_rwkv_k_k, m_rwkv_k_a, m_rwkv_r_k, m_rwkv_ln_w, m_rwkv_ln_b, m_w_out0, m_w_in1, m_b_in1, m_attn_sinks, m_w_out1, m_b_out1, m_final_norm_w, v_norm_w, v_w_in0, v_gla_gk_up, v_gla_gk_bias, v_gla_norm_w, v_rwkv_mu, v_rwkv_w0, v_rwkv_w_up, v_rwkv_a0, v_rwkv_a_up, v_rwkv_k_k, v_rwkv_k_a, v_rwkv_r_k, v_rwkv_ln_w, v_rwkv_ln_b, v_w_out0, v_w_in1, v_b_in1, v_attn_sinks, v_w_out1, v_b_out1, v_final_norm_w):
    weights = dict(norm_w=norm_w, w_in0=w_in0, gla_gk_up=gla_gk_up, gla_gk_bias=gla_gk_bias, gla_norm_w=gla_norm_w, rwkv_mu=rwkv_mu,
                   rwkv_w0=rwkv_w0, rwkv_w_up=rwkv_w_up, rwkv_a0=rwkv_a0, rwkv_a_up=rwkv_a_up, rwkv_k_k=rwkv_k_k, rwkv_k_a=rwkv_k_a,
                   rwkv_r_k=rwkv_r_k, rwkv_ln_w=rwkv_ln_w, rwkv_ln_b=rwkv_ln_b, w_out0=w_out0, w_in1=w_in1, b_in1=b_in1,
                   attn_sinks=attn_sinks, w_out1=w_out1, b_out1=b_out1, final_norm_w=final_norm_w)
    moms = dict(norm_w=m_norm_w, w_in0=m_w_in0, gla_gk_up=m_gla_gk_up, gla_gk_bias=m_gla_gk_bias, gla_norm_w=m_gla_norm_w,
                rwkv_mu=m_rwkv_mu, rwkv_w0=m_rwkv_w0, rwkv_w_up=m_rwkv_w_up, rwkv_a0=m_rwkv_a0, rwkv_a_up=m_rwkv_a_up,
                rwkv_k_k=m_rwkv_k_k, rwkv_k_a=m_rwkv_k_a, rwkv_r_k=m_rwkv_r_k, rwkv_ln_w=m_rwkv_ln_w, rwkv_ln_b=m_rwkv_ln_b,
                w_out0=m_w_out0, w_in1=m_w_in1, b_in1=m_b_in1, attn_sinks=m_attn_sinks, w_out1=m_w_out1, b_out1=m_b_out1,
                final_norm_w=m_final_norm_w)
    vars_ = dict(norm_w=v_norm_w, w_in0=v_w_in0, gla_gk_up=v_gla_gk_up, gla_gk_bias=v_gla_gk_bias, gla_norm_w=v_gla_norm_w,
                 rwkv_mu=v_rwkv_mu, rwkv_w0=v_rwkv_w0, rwkv_w_up=v_rwkv_w_up, rwkv_a0=v_rwkv_a0, rwkv_a_up=v_rwkv_a_up,
                 rwkv_k_k=v_rwkv_k_k, rwkv_k_a=v_rwkv_k_a, rwkv_r_k=v_rwkv_r_k, rwkv_ln_w=v_rwkv_ln_w, rwkv_ln_b=v_rwkv_ln_b,
                 w_out0=v_w_out0, w_in1=v_w_in1, b_in1=v_b_in1, attn_sinks=v_attn_sinks, w_out1=v_w_out1, b_out1=v_b_out1,
                 final_norm_w=v_final_norm_w)
    names = list(weights)
    big = ["w_in0", "w_out0", "w_in1", "w_out1"]
    small_sharded = ["gla_gk_up", "rwkv_w_up", "rwkv_a_up", "b_in1", "b_out1"]
    replicated = [n for n in names if n not in big and n not in small_sharded]

    xs = x[0]
    tgt = loss_target[0]
    t = xs.shape[0]

    def view(w):
        shape = tuple(w.shape[-2:]) if w.ndim >= 2 else (1, w.shape[0])
        return shape, ("rows" if shape[0] > 1 and shape[1] <= LANES else "flat")

    def layout(ns, row0=0):
        specs = []
        for n in ns:
            shape, mode = view(weights[n])
            specs.append((shape, mode, row0))
            row0 += _packed_rows(shape, mode)
        return specs, row0

    sh_specs, n_shard_rows = layout(small_sharded)
    rep_specs, loss_row = layout(replicated, n_shard_rows)
    sh_modes, rep_modes = [s[1] for s in sh_specs], [s[1] for s in rep_specs]

    small_shard_pack = _pack_small([weights[n].reshape(view(weights[n])[0]) for n in small_sharded], sh_modes)
    g_in0, g_small = _all_gather([w_in0[0].T.astype(BF16), small_shard_pack])
    w0t = _w0t_to_padded(g_in0.reshape(-1, D_MODEL))
    later_shards = [w_out0[0].astype(BF16), w_in1[0].T.astype(BF16), w_out1[0].astype(BF16)]
    gs = [_take_small(g_small, row0, shape, mode) for shape, mode, row0 in sh_specs]
    join_cols = lambda a: jnp.transpose(a, (1, 0, 2)).reshape(a.shape[1], -1)
    gk_up, w_up, a_up = join_cols(gs[0]), join_cols(gs[1]), join_cols(gs[2])
    b_in, b_out = gs[3].reshape(1, -1), gs[4].reshape(1, -1)

    gk_up_p = _pad_to(gk_up, rows=LOW)
    w3, rank = 3 * RWKV_W, rwkv_w_up.shape[1]
    mu = rwkv_mu
    rwkv_params = [mu[:, 0:RWKV_W], mu[:, RWKV_W:2 * RWKV_W], mu[:, 2 * RWKV_W:w3], _pad_to(mu[:, w3:w3 + rank], cols=LOW),
                   _pad_to(mu[:, w3 + rank:], cols=LOW), rwkv_w0, _pad_to(w_up, rows=LOW), rwkv_a0, _pad_to(a_up, rows=LOW),
                   rwkv_k_k, rwkv_k_a, rwkv_r_k.reshape(1, RWKV_W), rwkv_ln_w, rwkv_ln_b]
    bq, bk, bv = b_in[:, :MIX], b_in[:, MIX:MIX + SWA_KV], b_in[:, MIX + SWA_KV:]
    cos, sin = _rope_tables(t)
    nw0, nw1, fw = norm_w[0:1], norm_w[1:2], final_norm_w.reshape(1, D_MODEL)

    d = D_MODEL
    wide = lambda arr: (arr, d, 0)
    silu = lambda g: g * sigmoid(g)
    hn0, proj0 = _matmul_fused("norm0_proj0", rms, w0t, "nt", [wide(xs)], [nw0], [(N0P, F32)], [], lambda acc, x, w: (acc,))
    o_a, gla_states = _gla_fwd(proj0, gk_up_p, gla_gk_bias, gla_norm_w)
    o_b, rwkv_states, rwkv_prevs, (g_out0, g_in1, g_out1) = _rwkv_fwd(proj0, rwkv_params, later_shards, ["gather"] * 3)
    wo0 = g_out0.reshape(MIX, D_MODEL)
    w1t = _w1t_to_mine(g_in1.reshape(-1, D_MODEL))
    wo1 = g_out1.reshape(MIX, D_MODEL)
    og0, h1, hn1 = _matmul_fused(
        "gate0_out0_norm1", lambda oa, ob, gate, x, w: jnp.concatenate([oa, ob], axis=1) * silu(gate), wo0, "nn",
        [(o_a, GLA_VAL, 0), (o_b, RWKV_W, 0), wide(proj0), wide(xs)], [nw1], [(d, F32), (d, BF16)], [],
        lambda acc, oa, ob, gate, x, w: _resid_norm(acc, x, w))
    proj1 = _matmul("proj1", hn1, w1t, "nt", PROJ_ROWS, N1P // 2)
    o_c, kst, vst = _swa_fwd(proj1, cos, sin, bq, bk, bv, attn_sinks)
    og1, dh2, loss_part, d_b_out, d_fw = _matmul_fused(
        "gate1_out1_loss", lambda oc, gate, h, tg, b, w: oc * silu(gate), wo1, "nn",
        [wide(o_c), wide(proj1), wide(h1), wide(tgt)], [b_out, fw], [(d, F32)], [LANES, d, d],
        lambda acc, oc, gate, h, tg, b, w: _loss_head(acc, h, tg, b, w))

    d_oc, d_gate1 = _matmul_fused("out1_dx_gate1", dh2, wo1, "nt", [wide(o_c), wide(proj1)], [], [(d, F32), (d, BF16)], [], _gate_back)
    d_wo1 = _matmul("out1_dw", og1, dh2, "tn", DW_COLS, DW_COLS, BF16)
    dq, dk, dv, d_bq, d_bk, d_bv, d_sinks = _swa_bwd(proj1, cos, sin, bq, bk, bv, attn_sinks, kst, vst, d_oc)
    dproj1 = jnp.concatenate([d_gate1, dq, dk, dv], axis=1)
    dh1, d_nw1 = _matmul_fused("proj1_dx_norm1", dproj1, w1t, "nn", [wide(h1), wide(dh2)], [nw1], [(d, F32)], [d], _norm_back)
    d_w1t = _matmul("proj1_dw", dproj1, hn1, "tn", DW_COLS, d, BF16)
    d_oa, d_ob, d_gate0 = _matmul_fused("out0_dx_gate0", dh1, wo0, "nt", [(o_a, GLA_VAL, 0), (o_b, RWKV_W, 0), wide(proj0)], [],
                                        [(GLA_VAL, F32), (RWKV_W, F32), (d, BF16)], [], _gate_back)
    d_wo0 = _matmul("out0_dw", og0, dh1, "tn", DW_COLS, DW_COLS, BF16)
    dgq, dgk, dgv, dglow, d_gk_up, d_gk_bias, d_gla_nw = _gla_bwd(proj0, gk_up_p, gla_gk_bias, gla_norm_w, gla_states, d_oa)
    row_blocks = lambda a: a.astype(BF16).reshape(N_DEV, -1, D_MODEL)
    early = [row_blocks(_w1t_from_mine(d_w1t)), row_blocks(d_wo1), row_blocks(d_wo0)]
    (dr, dkk, dvv, dxw, dxa), d_rp, (r_in1, r_out1, r_out0) = _rwkv_bwd(
        proj0, rwkv_params, rwkv_states, rwkv_prevs, d_ob, early, ["scatter"] * 3)
    pad = jnp.zeros((t, N0P - C0["xa"][0] - C0["xa"][1]), BF16)
    dproj0 = jnp.concatenate([d_gate0, dgv, dr, dkk, dvv, dgq, dgk, dglow, dxw, dxa, pad], axis=1)
    d_w0 = row_blocks(_w0t_from_padded(_matmul("proj0_dw", dproj0, hn0, "tn", DW_COLS, d, BF16)))
    r_in0 = _sequencer_scatter("exchange_w_in0_grad", d_w0, 0)
    grad_x, d_nw0 = _matmul_fused("proj0_dx_norm0", dproj0, w0t, "nn", [wide(xs), wide(dh1)], [nw0], [(d, F32)], [d], _norm_back)

    contrib = dict(
        norm_w=jnp.concatenate([d_nw0, d_nw1], axis=0), gla_gk_bias=d_gk_bias, gla_norm_w=d_gla_nw,
        rwkv_mu=jnp.concatenate([d_rp[0], d_rp[1], d_rp[2], d_rp[3][:, :rank], d_rp[4][:, :rank]], axis=1),
        rwkv_w0=d_rp[5], rwkv_a0=d_rp[7], rwkv_k_k=d_rp[9], rwkv_k_a=d_rp[10], rwkv_r_k=d_rp[11].reshape(RWKV_HEADS, RWKV_N),
        rwkv_ln_w=d_rp[12], rwkv_ln_b=d_rp[13], attn_sinks=d_sinks, final_norm_w=d_fw)
    rep_pack = _pack_small([contrib[n] for n in replicated] + [loss_part[:, :1]], rep_modes + ["flat"])

    d_b_in = jnp.concatenate([d_bq, d_bk, d_bv], axis=1)
    full_small = [d_gk_up[:gk_up.shape[0]], d_rp[6][:rank], d_rp[8][:rank], d_b_in, d_b_out]
    split_cols = lambda a: jnp.transpose(a.reshape(a.shape[0], N_DEV, -1), (1, 0, 2))
    small_parts = [split_cols(a) for a in full_small]
    small_pack = _pack_small(small_parts, sh_modes, lead=True)
    r_small, r_rep = _exchange([small_pack, rep_pack], ["scatter", "gather"])

    res = {}
    res["w_out0"] = tuple(a[None] for a in _adamw("adamw_w_out0", w_out0[0], r_out0, m_w_out0[0], v_w_out0[0], ADAM_COLS))
    res["w_in1"] = tuple(a.T[None] for a in _adamw("adamw_w_in1", w_in1[0].T, r_in1, m_w_in1[0].T, v_w_in1[0].T, ADAM_COLS))
    res["w_out1"] = tuple(a[None] for a in _adamw("adamw_w_out1", w_out1[0], r_out1, m_w_out1[0], v_w_out1[0], ADAM_COLS))
    small_names = small_sharded + replicated
    slots = jnp.concatenate([r_small, r_rep], axis=1)
    as_2d = lambda a: a.reshape(1, -1) if a.ndim == 1 else a
    small_res, loss_row_out = _adamw_small(slots, sh_specs + rep_specs, [as_2d(weights[n]) for n in small_names],
                                           [as_2d(moms[n]) for n in small_names], [as_2d(vars_[n]) for n in small_names], loss_row)
    for n, vals in zip(small_names, small_res):
        res[n] = tuple(val.reshape(weights[n].shape) for val in vals)
    loss = loss_row_out[0, 0]
    my_idx = 4 * lax.axis_index("x") + 2 * lax.axis_index("y") + lax.axis_index("c")
    r_in0 = lax.dynamic_update_slice(r_in0, lax.dynamic_slice(d_w0, (my_idx, 0, 0), (1,) + d_w0.shape[1:]), (my_idx, 0, 0))
    res["w_in0"] = tuple(a.T[None] for a in _adamw("adamw_w_in0", w_in0[0].T, r_in0, m_w_in0[0].T, v_w_in0[0].T, ADAM_COLS))
    return (loss, grad_x[None], *[res[n][0] for n in names], *[res[n][1] for n in names],
            *[res[n][2] for n in names], *[res[n][3] for n in names])
```

```python
import functools

import jax
import jax.numpy as jnp
from jax import lax
from jax.experimental import pallas as pl
from jax.experimental.pallas import tpu as pltpu
from jax.experimental.pallas import tpu_sc as plsc

F32 = jnp.float32
BF16 = jnp.bfloat16
HI = lax.Precision.HIGHEST

D_MODEL = 1024
NORM_EPS = 1e-5
GLA_HEADS, GLA_DK, GLA_DV = 4, 64, 128
GLA_NORMALIZER = 16.0
GLA_CHUNK = 64
GLA_STEP = 512
RWKV_HEADS, RWKV_N = 8, 64
RWKV_LN_EPS = 64e-5
RWKV_CHUNK = 128
SWA_Q_HEADS, SWA_KV_HEADS, SWA_GROUP, SWA_HD = 16, 4, 4, 64
WINDOW = 128
SWA_STEP = 512
ROPE_THETA = 500000.0
NEG = -1e30
N_DEV = 8
LANES = 128

ADAM_LR, ADAM_B1, ADAM_B2, ADAM_EPS, ADAM_WD, ADAM_STEP = 0.001, 0.9, 0.999, 1e-08, 0.01, 10

GLA_KEY, GLA_VAL = GLA_HEADS * GLA_DK, GLA_HEADS * GLA_DV
RWKV_W = RWKV_HEADS * RWKV_N
SWA_KV = SWA_KV_HEADS * SWA_HD
MIX = GLA_VAL + RWKV_W
LOW = LANES

N0P = 4096
C0 = dict(gate=(0, MIX), gv=(1024, GLA_VAL), r=(1536, RWKV_W), k=(2048, RWKV_W), v=(2560, RWKV_W), gq=(3072, GLA_KEY),
          gk=(3328, GLA_KEY), glow=(3584, LOW), xw=(3712, LOW), xa=(3840, LOW))
N1P = 2560
C1 = dict(gate=(0, MIX), q=(1024, MIX), k=(2048, SWA_KV), v=(2304, SWA_KV))

VMEM_LIMIT = 56 * 1024 * 1024

P_LORA = 1
P_GLA = 1
P_RWKV_G = 2
P_RWKV = 1
P_SWA = 1


def _cparams(sem=None):
    return pltpu.CompilerParams(dimension_semantics=sem, vmem_limit_bytes=VMEM_LIMIT)


DIMS = dict(nn=(((1,), (0,)), ((), ())), nt=(((1,), (1,)), ((), ())), tn=(((0,), (0,)), ((), ())))


def _split_bf16(a):
    hi = a.astype(BF16)
    return hi, (a - hi.astype(F32)).astype(BF16)


def _dot(a, b, mode, passes):
    dg = lambda p, q: lax.dot_general(p, q, DIMS[mode], preferred_element_type=F32)
    if passes == 1:
        return dg(a.astype(BF16), b.astype(BF16))
    if passes == 2:
        ah, (bh, bl) = a.astype(BF16), _split_bf16(b)
        return dg(ah, bh) + dg(ah, bl)
    if passes == 3:
        (ah, al), (bh, bl) = _split_bf16(a), _split_bf16(b)
        return dg(ah, bh) + dg(al, bh) + dg(ah, bl)
    return lax.dot_general(a, b, DIMS[mode], precision=HI, preferred_element_type=F32)


@functools.partial(jax.custom_vjp, nondiff_argnums=(2, 3))
def mmx(a, b, mode, passes):
    return _dot(a, b, mode, passes)


def _mmx_fwd(a, b, mode, passes):
    return _dot(a, b, mode, passes), (a, b)


def _mmx_bwd(mode, passes, res, g):
    a, b = res
    if mode == "nn":
        return _dot(g, b, "nt", passes), _dot(a, g, "tn", passes)
    if mode == "nt":
        return _dot(g, b, "nn", passes), _dot(g, a, "tn", passes)
    return _dot(b, g, "nt", passes), _dot(a, g, "nn", passes)


mmx.defvjp(_mmx_fwd, _mmx_bwd)


def _tri_dot(tri, x):
    t = tri.astype(BF16)
    x1 = x.astype(BF16)
    r1 = x - x1.astype(F32)
    x2 = r1.astype(BF16)
    x3 = (r1 - x2.astype(F32)).astype(BF16)
    dg = lambda q: jnp.dot(t, q, preferred_element_type=F32)
    return dg(x1) + dg(x2) + dg(x3)


@jax.custom_vjp
def cumsum_rows(x):
    return _tri_dot(tril_ones(x.shape[0]), x)


def _cumsum_fwd(x):
    return cumsum_rows(x), None


def _cumsum_bwd(_, g):
    i, j = _iota2(g.shape[0], g.shape[0])
    return (_tri_dot(jnp.where(i <= j, 1.0, 0.0).astype(F32), g),)


cumsum_rows.defvjp(_cumsum_fwd, _cumsum_bwd)


def _head_dot(x):
    i, j = _iota2(LANES, LANES)
    shift = RWKV_N.bit_length() - 1
    same = jnp.where(jnp.right_shift(i, shift) == jnp.right_shift(j, shift), 1.0, 0.0).astype(F32)
    return jnp.concatenate([_ones_right(x[:, g * LANES:(g + 1) * LANES], same) for g in range(x.shape[1] // LANES)], axis=1)


def _ones_right(x, ones):
    t = ones.astype(BF16)
    x1 = x.astype(BF16)
    x2 = (x - x1.astype(F32)).astype(BF16)
    dg = lambda q: jnp.dot(q, t, preferred_element_type=F32)
    return dg(x1) + dg(x2)


@jax.custom_vjp
def head_sum(x):
    return _head_dot(x)


def _head_sum_fwd(x):
    return head_sum(x), None


def _head_sum_bwd(_, g):
    return (_head_dot(g),)


head_sum.defvjp(_head_sum_fwd, _head_sum_bwd)


def cat_rows(*xs):
    return jnp.concatenate(xs, axis=0)


def _iota2(n, m):
    return lax.broadcasted_iota(jnp.int32, (n, m), 0), lax.broadcasted_iota(jnp.int32, (n, m), 1)


def tril_ones(c, strict=False):
    i, j = _iota2(c, c)
    return jnp.where((i > j) if strict else (i >= j), 1.0, 0.0).astype(F32)


def row_of(x, r):
    i = lax.broadcasted_iota(jnp.int32, x.shape, 0)
    return jnp.sum(jnp.where(i == r, x, 0.0), axis=0, keepdims=True)


@jax.custom_vjp
def shift_rows(x, prev):
    r = lax.broadcasted_iota(jnp.int32, x.shape, 0)
    return jnp.where(r == 0, prev, pltpu.roll(x, 1, 0))


def _shift_fwd(x, prev):
    return shift_rows(x, prev), None


def _shift_bwd(_, g):
    c = g.shape[0]
    r = lax.broadcasted_iota(jnp.int32, g.shape, 0)
    return jnp.where(r == c - 1, 0.0, pltpu.roll(g, c - 1, 0)), row_of(g, 0)


shift_rows.defvjp(_shift_fwd, _shift_bwd)


def log_sigmoid(x):
    return jnp.minimum(x, 0.0) - jnp.log(1.0 + jnp.exp(-jnp.abs(x)))


def softplus(x):
    return jnp.maximum(x, 0.0) + jnp.log(1.0 + jnp.exp(-jnp.abs(x)))


def sigmoid(x):
    return 1.0 / (1.0 + jnp.exp(-x))


def rms(x, w, eps=NORM_EPS):
    return x * lax.rsqrt(jnp.mean(x * x, axis=-1, keepdims=True) + eps) * w


def gla_chunk(state, toks, params):
    q, k, v, glow = toks
    gk_up, bias, norm_w = params
    c = GLA_CHUNK
    subs, heads = range(glow.shape[0] // c), range(GLA_HEADS)
    rows = lambda x, j: x[j * c:(j + 1) * c]
    hk = lambda x, h: x[:, h * GLA_DK:(h + 1) * GLA_DK]
    hv = lambda x, h: x[:, h * GLA_DV:(h + 1) * GLA_DV]
    ltri = tril_ones(c)
    g = log_sigmoid(mmx(glow, gk_up, "nn", P_LORA) + bias) / GLA_NORMALIZER
    b = [cumsum_rows(rows(g, j)) for j in subs]
    ref = [lax.stop_gradient(row_of(b[j], c // 2)) for j in subs]
    last = [row_of(b[j], c - 1) for j in subs]
    ql = [rows(q, j) * (GLA_DK ** -0.5) * jnp.exp(b[j] - ref[j]) for j in subs]
    kr = [rows(k, j) * jnp.exp(ref[j] - b[j]) for j in subs]
    kl = [rows(k, j) * jnp.exp(last[j] - b[j]) for j in subs]
    vj = [rows(v, j) for j in subs]
    e_ref, e_last = [jnp.exp(x) for x in ref], [jnp.exp(x) for x in last]
    att = [[mmx(hk(ql[j], h), hk(kr[j], h), "nt", P_GLA) * ltri for h in heads] for j in subs]
    o_in = [[mmx(att[j][h], hv(vj[j], h), "nn", P_GLA) for h in heads] for j in subs]
    kv = [[mmx(hv(vj[j], h), hk(kl[j], h), "tn", P_GLA) for h in heads] for j in subs]
    o = []
    for j in subs:
        o.append([o_in[j][h] + mmx(hk(ql[j], h), state[h] * hk(e_ref[j], h), "nt", P_GLA) for h in heads])
        state = [state[h] * hk(e_last[j], h) + kv[j][h] for h in heads]
    o = [[x * lax.rsqrt(jnp.mean(x * x, axis=-1, keepdims=True) + NORM_EPS) * norm_w for x in oj] for oj in o]
    return cat_rows(*[jnp.concatenate(oj, axis=1) for oj in o]), state


SOLVE_BLOCK = 128


def solve_unit_lower(ps, ws):
    n = ps[0].shape[0]
    heads = range(len(ps))
    if n > SOLVE_BLOCK:
        half = n // 2
        top = solve_unit_lower([p[:half, :half] for p in ps], [w[:half] for w in ws])
        rest = [ws[h][half:] + mmx(ps[h][half:, :half], top[h], "nn", P_RWKV) for h in heads]
        bottom = solve_unit_lower([p[half:, half:] for p in ps], rest)
        return [cat_rows(top[h], bottom[h]) for h in heads]
    u, p = ws, ps
    levels = max(1, (n - 1).bit_length())
    for it in range(levels):
        if it + 1 < levels:
            y = [mmx(p[h], jnp.concatenate([p[h], u[h]], axis=1), "nn", P_RWKV) for h in heads]
            u = [u[h] + y[h][:, n:] for h in heads]
            p = [y[h][:, :n] for h in heads]
        else:
            u = [u[h] + mmx(p[h], u[h], "nn", P_RWKV) for h in heads]
    return u


def rwkv_chunk(state, toks, params):
    S, pr, pk, pv, pxw, pxa = state
    r_, k_, v_, xw_, xa_ = toks
    mu_r, mu_k, mu_v, mu_xw, mu_xa, w0, w_up, a0, a_up, k_k, k_a, r_k, ln_w, ln_b = params
    c, n = xw_.shape[0], RWKV_N
    heads = range(RWKV_HEADS)
    hs = lambda x, h: x[:, h * n:(h + 1) * n]
    ltri = tril_ones(c)
    stri = tril_ones(c, strict=True)

    def lerp(x, prev, mu):
        return x + (shift_rows(x, prev) - x) * mu

    xw = jnp.tanh(lerp(xw_, pxw, mu_xw))
    xa = lerp(xa_, pxa, mu_xa)
    r = lerp(r_, pr, mu_r)
    k = lerp(k_, pk, mu_k)
    v = lerp(v_, pv, mu_v)
    w = -softplus(-(w0 + mmx(xw, w_up, "nn", P_LORA))) - 0.5
    lw = -jnp.exp(w)
    asig = sigmoid(a0 + mmx(xa, a_up, "nn", P_LORA))
    kk = k * k_k
    kk = kk * lax.rsqrt(jnp.maximum(head_sum(kk * kk), 1e-24))
    k2 = k * (1.0 + (asig - 1.0) * k_a)
    b = kk * asig
    cum = cumsum_rows(lw)
    ref = lax.stop_gradient(row_of(cum, c // 2))
    last = row_of(cum, c - 1)
    at = -kk * jnp.exp(cum - lw - ref)
    rt = r * jnp.exp(cum - ref)
    e_out = jnp.exp(ref - cum)
    bt, kt = b * e_out, k2 * e_out
    e_tail = jnp.exp(last - cum)
    bl, kl = b * e_tail, k2 * e_tail
    e_ref, e_last = jnp.exp(ref), jnp.exp(last)
    g = [mmx(cat_rows(hs(at, h), hs(rt, h)), cat_rows(hs(bt, h), hs(kt, h), S[h] * hs(e_ref, h)), "nt", P_RWKV_G) for h in heads]
    aab = [x[:c, :c] * stri for x in g]
    aak = [x[:c, c:2 * c] * stri for x in g]
    arb = [x[c:, :c] * ltri for x in g]
    ark = [x[c:, c:2 * c] * ltri for x in g]
    av = [mmx(cat_rows(aak[h], ark[h]), hs(v, h), "nn", P_RWKV) for h in heads]
    u = solve_unit_lower(aab, [g[h][:c, 2 * c:] + av[h][:c] for h in heads])
    o = [g[h][c:, 2 * c:] + av[h][c:] + mmx(arb[h], u[h], "nn", P_RWKV) for h in heads]
    s1 = [S[h] * hs(e_last, h) + mmx(cat_rows(u[h], hs(v, h)), cat_rows(hs(bl, h), hs(kl, h)), "tn", P_RWKV) for h in heads]
    o = jnp.concatenate(o, axis=1)
    d = o - head_sum(o) * (1.0 / n)
    var = head_sum(d * d) * (1.0 / n)
    o = d * lax.rsqrt(var + RWKV_LN_EPS) * ln_w + ln_b + head_sum(r * k2 * r_k) * v
    new_state = (s1, row_of(r_, c - 1), row_of(k_, c - 1), row_of(v_, c - 1), row_of(xw_, c - 1), row_of(xa_, c - 1))
    return o, new_state


ROPE_HALF = 8


def _rot_half_raw(x):
    lane = lax.broadcasted_iota(jnp.int32, (x.shape[0], LANES), 1) & (SWA_HD - 1)
    out = []
    for i in range(x.shape[1] // LANES):
        g = x[:, i * LANES:(i + 1) * LANES]
        up, down = pltpu.roll(g, LANES - ROPE_HALF, 1), pltpu.roll(g, ROPE_HALF, 1)
        out.append(jnp.where(lane < ROPE_HALF, -up, jnp.where(lane < 2 * ROPE_HALF, down, 0.0)))
    return out[0] if len(out) == 1 else jnp.concatenate(out, axis=1)


@jax.custom_vjp
def rot_half(x):
    return _rot_half_raw(x)


rot_half.defvjp(lambda x: (_rot_half_raw(x), None), lambda _, g: (-_rot_half_raw(g),))


def rope(x, cos2, sin2):
    reps = x.shape[1] // LANES
    tile = lambda t: t if reps == 1 else jnp.concatenate([t] * reps, axis=1)
    return x * tile(cos2) + rot_half(x) * tile(sin2)


def swa_chunk(state, toks, params, first):
    kprev, vprev = state
    q_, k_, v_, cos, sin = toks
    bq, bk, bv, sinks = params
    c, ng = WINDOW, SWA_GROUP
    n_sub = cos.shape[0] // c
    units = [(j, g) for j in range(n_sub) for g in range(SWA_KV_HEADS)]
    rows = lambda x, j: x[j * c:(j + 1) * c]
    hs = lambda g: range(g * ng, (g + 1) * ng)
    head = lambda x, h: x[:, h * SWA_HD:(h + 1) * SWA_HD]
    qi, kj = _iota2(ng * c, 2 * c)
    qpos = qi & (c - 1)
    cur_ok = (kj >= c) & (qpos >= kj - c)
    prev_ok = (kj < c) & (kj > qpos)
    ok = [cur_ok | (prev_ok & jnp.logical_not(first))] + [cur_ok | prev_ok] * (n_sub - 1)
    q_all = rope(q_ + bq, cos, sin) * (SWA_HD ** -0.5)
    k_all = rope(k_ + bk, cos, sin)
    v_all = v_ + bv
    k = {(j, g): rows(head(k_all, g), j) for j, g in units}
    v = {(j, g): rows(head(v_all, g), j) for j, g in units}
    q = {(j, g): cat_rows(*[rows(head(q_all, h), j) for h in hs(g)]) for j, g in units}
    kp = lambda j, g: kprev[g] if j == 0 else k[(j - 1, g)]
    vp = lambda j, g: vprev[g] if j == 0 else v[(j - 1, g)]
    s = {(j, g): jnp.where(ok[j], mmx(q[(j, g)], cat_rows(kp(j, g), k[(j, g)]), "nt", P_SWA), NEG) for j, g in units}
    sink = [cat_rows(*[jnp.broadcast_to(sinks[h], (c, 1)) for h in hs(g)]) for g in range(SWA_KV_HEADS)]
    m = {(j, g): lax.stop_gradient(jnp.maximum(jnp.max(s[(j, g)], axis=-1, keepdims=True), sink[g])) for j, g in units}
    p = {u: jnp.exp(s[u] - m[u]) for u in units}
    ones = jnp.ones((2 * c, SWA_HD), F32)
    pv = {(j, g): mmx(p[(j, g)], cat_rows(vp(j, g), v[(j, g)]), "nn", P_SWA) for j, g in units}
    den = {u: mmx(p[u], ones, "nn", P_SWA) for u in units}
    o = {(j, g): pv[(j, g)] / (den[(j, g)] + jnp.exp(sink[g] - m[(j, g)])) for j, g in units}
    outs = [cat_rows(*[o[(j, g)][i * c:(i + 1) * c] for j in range(n_sub)]) for g in range(SWA_KV_HEADS) for i in range(ng)]
    last = n_sub - 1
    return outs, ([k[(last, g)] for g in range(SWA_KV_HEADS)], [v[(last, g)] for g in range(SWA_KV_HEADS)])


def _heads(ref, n, w, rows=slice(None)):
    return [ref[rows, h * w:(h + 1) * w] for h in range(n)]


def _put_heads(ref, vals, w, rows=slice(None), add=False):
    for h, val in enumerate(vals):
        if add:
            ref[rows, h * w:(h + 1) * w] += val
        else:
            ref[rows, h * w:(h + 1) * w] = val


def _col(block_w, name, table):
    off, w = table[name]
    assert off % block_w == 0 and w % block_w == 0
    return off // block_w


def _tok_spec(c, w, colblock, n=None):
    if n is None:
        return pl.BlockSpec((c, w), lambda i: (i, colblock))
    return pl.BlockSpec((c, w), lambda i: (n - 1 - i, colblock))


def _full_spec(shape):
    return pl.BlockSpec(shape, lambda i: (0,) * len(shape))


def _matmul(name, a, b, mode, tm, tn, out_dtype=F32):
    (m, kd) = (a.shape[1], a.shape[0]) if mode == "tn" else a.shape
    n = b.shape[0] if mode == "nt" else b.shape[1]
    assert m % tm == 0 and n % tn == 0
    a_spec = pl.BlockSpec((kd, tm), lambda j, i: (0, i)) if mode == "tn" else pl.BlockSpec((tm, kd), lambda j, i: (i, 0))
    b_spec = pl.BlockSpec((tn, kd), lambda j, i: (j, 0)) if mode == "nt" else pl.BlockSpec((kd, tn), lambda j, i: (0, j))

    def body(a_ref, b_ref, o_ref):
        o_ref[...] = lax.dot_general(a_ref[...].astype(BF16), b_ref[...].astype(BF16), DIMS[mode],
                                     preferred_element_type=F32).astype(out_dtype)

    return pl.pallas_call(
        body, name=name, grid=(n // tn, m // tm), in_specs=[a_spec, b_spec],
        out_specs=pl.BlockSpec((tm, tn), lambda j, i: (i, j)), out_shape=jax.ShapeDtypeStruct((m, n), out_dtype),
        compiler_params=_cparams(("arbitrary", "arbitrary")))(a, b)


TOK_TILE = 512
PROJ_ROWS = 1024
DW_COLS = 512
ADAM_COLS = 256


def _matmul_fused(name, a, b, mode, tiles, rows, outs, sums, epilogue, comm=(), kinds=()):
    made = callable(a)
    m = tiles[0][0].shape[0] if made else a.shape[0]
    kd = b.shape[0] if mode == "nn" else b.shape[1]
    n = b.shape[1] if mode == "nn" else b.shape[0]
    tm = TOK_TILE
    steps = m // tm
    if made:
        outs = [(kd, BF16)] + list(outs)
    nt_, nr, no, ns, ncomm = len(tiles), len(rows), len(outs), len(sums), len(comm)

    def body(*refs):
        at = 1 if made else 2
        b_ref = refs[at - 1]
        tile_refs, row_refs, comm_in = refs[at:at + nt_], refs[at + nt_:at + nt_ + nr], refs[at + nt_ + nr:at + nt_ + nr + ncomm]
        at += nt_ + nr + ncomm
        out_refs, sum_refs, comm_out = refs[at:at + no], refs[at + no:at + no + ns], refs[at + no + ns:at + no + ns + ncomm]
        sems = refs[at + no + ns + ncomm:]
        i = pl.program_id(0)

        @pl.when(i == 0)
        def _():
            if ncomm:
                _comm_start(*_comm_copies(comm_in, comm_out, kinds, *sems))
            for ref in sum_refs:
                ref[...] = jnp.zeros_like(ref)

        extras = [r[...] for r in tile_refs] + [r[...] for r in row_refs]
        a_blk = (a(*extras) if made else refs[0][...]).astype(BF16)
        acc = lax.dot_general(a_blk, b_ref[...].astype(BF16), DIMS[mode], preferred_element_type=F32)
        res = epilogue(acc, *extras)
        if made:
            res = (a_blk,) + tuple(res)
        for ref, val in zip(out_refs, res[:no]):
            ref[...] = val.astype(ref.dtype)
        for ref, val in zip(sum_refs, res[no:]):
            ref[...] += val

        if ncomm:
            @pl.when(i == steps - 1)
            def _():
                _comm_wait(*_comm_copies(comm_in, comm_out, kinds, *sems))

    in_specs = ([] if made else [pl.BlockSpec((tm, kd), lambda i: (i, 0))]) + [_full_spec(b.shape)]
    in_specs += [pl.BlockSpec((tm, w), functools.partial(lambda i, cb: (i, cb), cb=cb)) for _, w, cb in tiles]
    in_specs += [_full_spec(r.shape) for r in rows] + [ANY] * ncomm
    out_specs = [pl.BlockSpec((tm, w), lambda i: (i, 0)) for w, _ in outs] + [_full_spec((1, w)) for w in sums] + [ANY] * ncomm
    out_shape = ([jax.ShapeDtypeStruct((m, w), dt) for w, dt in outs] + [jax.ShapeDtypeStruct((1, w), F32) for w in sums]
                 + _comm_out_shapes(comm, kinds))
    return pl.pallas_call(body, name=name, grid=(steps,), in_specs=in_specs, out_specs=out_specs, out_shape=out_shape,
                          scratch_shapes=_comm_scratch(ncomm) if ncomm else [],
                          compiler_params=_cparams(("arbitrary",)))(*([] if made else [a]), b, *[t[0] for t in tiles], *rows, *comm)


def _resid_norm(y, x, w):
    h = x + y
    return h, rms(h, w)


def _norm_back(dhn, h, dres, w):
    _, vjp = jax.vjp(rms, h, w)
    dh, dw = vjp(dhn)
    return dh + dres, dw


def _gate_back(dog, *o_and_gate):
    outs, g = o_and_gate[:-1], o_and_gate[-1]
    s = sigmoid(g)
    silu, dsilu = g * s, s * (1.0 + g * (1.0 - s))
    d_outs, c = [], 0
    for o in outs:
        w = o.shape[1]
        d_outs.append(dog[:, c:c + w] * silu[:, c:c + w])
        c += w
    o_all = outs[0] if len(outs) == 1 else jnp.concatenate(outs, axis=1)
    return (*d_outs, dog * o_all * dsilu)


def _loss_head(y1, h1, target, b_out, fw):
    def f(h2, w):
        err = rms(h2, w) - target
        return 0.5 * jnp.sum(jnp.mean(err * err, axis=-1, keepdims=True), axis=0, keepdims=True)

    loss, vjp = jax.vjp(f, h1 + y1 + b_out, fw)
    dh2, dfw = vjp(jnp.ones((1, 1), F32))
    return dh2, jnp.broadcast_to(loss, (1, LANES)), jnp.sum(dh2, axis=0, keepdims=True), dfw


def _gla_load(q_ref, k_ref, v_ref, gl_ref, up_ref, bias_ref, nw_ref):
    toks = (q_ref[...], k_ref[...], v_ref[...], gl_ref[...])
    params = (up_ref[...], bias_ref[...], nw_ref[...])
    return toks, params


def _gla_specs(c, n=None):
    toks = [_tok_spec(c, GLA_KEY, _col(GLA_KEY, "gq", C0), n), _tok_spec(c, GLA_KEY, _col(GLA_KEY, "gk", C0), n),
            _tok_spec(c, GLA_VAL, _col(GLA_VAL, "gv", C0), n), _tok_spec(c, LOW, _col(LOW, "glow", C0), n)]
    return toks, [_full_spec(s) for s in GLA_PARAM_SHAPES]


GLA_PARAM_SHAPES = [(LOW, GLA_KEY), (1, GLA_KEY), (1, GLA_DV)]
GLA_STATE = (GLA_HEADS * GLA_DV, GLA_DK)


def _gla_fwd(proj0, gk_up, gk_bias, norm_w):
    t = proj0.shape[0]
    c = GLA_STEP
    nc = t // c
    toks_s, params_s = _gla_specs(c)

    def body(q_ref, k_ref, v_ref, gl_ref, up_ref, bias_ref, nw_ref, o_ref, st_ref, s_scr):
        @pl.when(pl.program_id(0) == 0)
        def _():
            s_scr[...] = jnp.zeros_like(s_scr)

        st_ref[...] = s_scr[...]
        toks, params = _gla_load(q_ref, k_ref, v_ref, gl_ref, up_ref, bias_ref, nw_ref)
        state = [s_scr[h * GLA_DV:(h + 1) * GLA_DV, :] for h in range(GLA_HEADS)]
        o_ref[...], new = gla_chunk(state, toks, params)
        for h in range(GLA_HEADS):
            s_scr[h * GLA_DV:(h + 1) * GLA_DV, :] = new[h]

    return pl.pallas_call(
        body, name="gla_fwd", grid=(nc,), in_specs=toks_s + params_s,
        out_specs=(_tok_spec(c, GLA_VAL, 0), pl.BlockSpec(GLA_STATE, lambda i: (i, 0))),
        out_shape=(jax.ShapeDtypeStruct((t, GLA_VAL), F32), jax.ShapeDtypeStruct((nc * GLA_STATE[0], GLA_DK), F32)),
        scratch_shapes=[pltpu.VMEM(GLA_STATE, F32)], compiler_params=_cparams(("arbitrary",)))(
            proj0, proj0, proj0, proj0, gk_up, gk_bias, norm_w)


def _gla_bwd(proj0, gk_up, gk_bias, norm_w, states, do):
    t = proj0.shape[0]
    c = GLA_STEP
    nc = t // c
    toks_s, params_s = _gla_specs(c, nc)

    def body(q_ref, k_ref, v_ref, gl_ref, up_ref, bias_ref, nw_ref, st_ref, do_ref,
             dq_ref, dk_ref, dv_ref, dgl_ref, dup_ref, dbias_ref, dnw_ref, ds_scr):
        @pl.when(pl.program_id(0) == 0)
        def _():
            ds_scr[...] = jnp.zeros_like(ds_scr)
            dup_ref[...] = jnp.zeros_like(dup_ref)
            dbias_ref[...] = jnp.zeros_like(dbias_ref)
            dnw_ref[...] = jnp.zeros_like(dnw_ref)

        toks, params = _gla_load(q_ref, k_ref, v_ref, gl_ref, up_ref, bias_ref, nw_ref)
        rows = lambda h: slice(h * GLA_DV, (h + 1) * GLA_DV)
        state = [st_ref[rows(h), :] for h in range(GLA_HEADS)]
        _, vjp = jax.vjp(gla_chunk, state, toks, params)
        dstate_in = [ds_scr[rows(h), :] for h in range(GLA_HEADS)]
        dstate, dtoks, (dup, dbias, dnw) = vjp((do_ref[...], dstate_in))
        for ref, val in zip((dq_ref, dk_ref, dv_ref, dgl_ref), dtoks):
            ref[...] = val.astype(ref.dtype)
        dup_ref[...] += dup
        dbias_ref[...] += dbias
        dnw_ref[...] += dnw
        for h in range(GLA_HEADS):
            ds_scr[rows(h), :] = dstate[h]

    rev = lambda w: pl.BlockSpec((c, w), lambda i: (nc - 1 - i, 0))
    tok_widths = (GLA_KEY, GLA_KEY, GLA_VAL, LOW)
    return pl.pallas_call(
        body, name="gla_bwd", grid=(nc,),
        in_specs=toks_s + params_s + [pl.BlockSpec(GLA_STATE, lambda i: (nc - 1 - i, 0)), rev(GLA_VAL)],
        out_specs=[rev(w) for w in tok_widths] + params_s,
        out_shape=[jax.ShapeDtypeStruct((t, w), BF16) for w in tok_widths] + [jax.ShapeDtypeStruct(s, F32) for s in GLA_PARAM_SHAPES],
        scratch_shapes=[pltpu.VMEM(GLA_STATE, F32)], compiler_params=_cparams(("arbitrary",)))(
            proj0, proj0, proj0, proj0, gk_up, gk_bias, norm_w, states, do)


RWKV_PARAM_SHAPES = [(1, RWKV_W), (1, RWKV_W), (1, RWKV_W), (1, LOW), (1, LOW), (1, RWKV_W), (LOW, RWKV_W), (1, RWKV_W),
                     (LOW, RWKV_W), (1, RWKV_W), (1, RWKV_W), (1, RWKV_W), (1, RWKV_W), (1, RWKV_W)]
RWKV_STATE = (RWKV_HEADS * RWKV_N, RWKV_N)
RWKV_TOK_WIDTHS = (RWKV_W, RWKV_W, RWKV_W, LOW, LOW)
PREV_W = sum(RWKV_TOK_WIDTHS)
PREV_COLS = [slice(sum(RWKV_TOK_WIDTHS[:i]), sum(RWKV_TOK_WIDTHS[:i + 1])) for i in range(len(RWKV_TOK_WIDTHS))]


def _rwkv_load(r_ref, k_ref, v_ref, xw_ref, xa_ref, p_refs):
    toks = (r_ref[...], k_ref[...], v_ref[...], xw_ref[...], xa_ref[...])
    return toks, tuple(p[...] for p in p_refs)


def _rwkv_state(s_ref, prev_ref):
    n = RWKV_N
    S = [s_ref[h * n:(h + 1) * n, :] for h in range(RWKV_HEADS)]
    return (S,) + tuple(prev_ref[0:1, cols] for cols in PREV_COLS)


def _rwkv_put_state(s_ref, prev_ref, state):
    n = RWKV_N
    for h in range(RWKV_HEADS):
        s_ref[h * n:(h + 1) * n, :] = state[0][h]
    for cols, val in zip(PREV_COLS, state[1:]):
        prev_ref[0:1, cols] = val


def _rwkv_specs(c, n=None):
    toks = [_tok_spec(c, w, _col(w, name, C0), n) for name, w in zip(("r", "k", "v", "xw", "xa"), RWKV_TOK_WIDTHS)]
    return toks, [_full_spec(s) for s in RWKV_PARAM_SHAPES]


def _rwkv_fwd(proj0, params, comm, kinds):
    t = proj0.shape[0]
    c = RWKV_CHUNK
    nc = t // c
    toks_s, params_s = _rwkv_specs(c)
    npar, ncomm = len(params), len(comm)

    def body(*refs):
        tok_refs, p_refs = refs[:5], refs[5:5 + npar]
        comm_in = refs[5 + npar:5 + npar + ncomm]
        o_ref, st_ref, pst_ref = refs[5 + npar + ncomm:8 + npar + ncomm]
        comm_out = refs[8 + npar + ncomm:8 + npar + 2 * ncomm]
        s_scr, prev_scr = refs[8 + npar + 2 * ncomm:10 + npar + 2 * ncomm]
        sems = refs[10 + npar + 2 * ncomm:]
        i = pl.program_id(0)

        @pl.when(i == 0)
        def _():
            _comm_start(*_comm_copies(comm_in, comm_out, kinds, *sems))
            s_scr[...] = jnp.zeros_like(s_scr)
            prev_scr[...] = jnp.zeros_like(prev_scr)

        st_ref[...] = s_scr[...]
        pst_ref[...] = prev_scr[...]
        toks, prm = _rwkv_load(*tok_refs, p_refs)
        o_ref[...], new = rwkv_chunk(_rwkv_state(s_scr, prev_scr), toks, prm)
        _rwkv_put_state(s_scr, prev_scr, new)

        @pl.when(i == nc - 1)
        def _():
            _comm_wait(*_comm_copies(comm_in, comm_out, kinds, *sems))

    outs = pl.pallas_call(
        body, name="rwkv_fwd", grid=(nc,), in_specs=toks_s + params_s + [ANY] * ncomm,
        out_specs=[_tok_spec(c, RWKV_W, 0), pl.BlockSpec(RWKV_STATE, lambda i: (i, 0)), pl.BlockSpec((8, PREV_W), lambda i: (i, 0))]
        + [ANY] * ncomm,
        out_shape=[jax.ShapeDtypeStruct((t, RWKV_W), F32), jax.ShapeDtypeStruct((nc * RWKV_STATE[0], RWKV_N), F32),
                   jax.ShapeDtypeStruct((nc * 8, PREV_W), F32)] + _comm_out_shapes(comm, kinds),
        scratch_shapes=[pltpu.VMEM(RWKV_STATE, F32), pltpu.VMEM((8, PREV_W), F32)] + _comm_scratch(ncomm),
        compiler_params=_cparams(("arbitrary",)))(proj0, proj0, proj0, proj0, proj0, *params, *comm)
    return outs[0], outs[1], outs[2], outs[3:]


def _rwkv_bwd(proj0, params, states, prevs, do, comm, kinds):
    t = proj0.shape[0]
    c = RWKV_CHUNK
    nc = t // c
    toks_s, params_s = _rwkv_specs(c, nc)
    npar, ncomm = len(params), len(comm)

    def body(*refs):
        tok_refs, p_refs = refs[:5], refs[5:5 + npar]
        st_ref, pst_ref, do_ref = refs[5 + npar:8 + npar]
        comm_in = refs[8 + npar:8 + npar + ncomm]
        outs = refs[8 + npar + ncomm:]
        dtok_refs, dp_refs, comm_out = outs[:5], outs[5:5 + npar], outs[5 + npar:5 + npar + ncomm]
        ds_scr, dprev_scr = outs[5 + npar + ncomm:7 + npar + ncomm]
        sems = outs[7 + npar + ncomm:]
        i = pl.program_id(0)

        @pl.when(i == 0)
        def _():
            _comm_start(*_comm_copies(comm_in, comm_out, kinds, *sems))
            ds_scr[...] = jnp.zeros_like(ds_scr)
            dprev_scr[...] = jnp.zeros_like(dprev_scr)
            for dp in dp_refs:
                dp[...] = jnp.zeros_like(dp)

        toks, prm = _rwkv_load(*tok_refs, p_refs)
        _, vjp = jax.vjp(rwkv_chunk, _rwkv_state(st_ref, pst_ref), toks, prm)
        dstate, dtoks, dprm = vjp((do_ref[...], _rwkv_state(ds_scr, dprev_scr)))
        for ref, val in zip(dtok_refs, dtoks):
            ref[...] = val.astype(ref.dtype)
        for ref, val in zip(dp_refs, dprm):
            ref[...] += val
        _rwkv_put_state(ds_scr, dprev_scr, dstate)

        @pl.when(i == nc - 1)
        def _():
            _comm_wait(*_comm_copies(comm_in, comm_out, kinds, *sems))

    rev = lambda w: pl.BlockSpec((c, w), lambda i: (nc - 1 - i, 0))
    outs = pl.pallas_call(
        body, name="rwkv_bwd", grid=(nc,),
        in_specs=toks_s + params_s + [pl.BlockSpec(RWKV_STATE, lambda i: (nc - 1 - i, 0)),
                                      pl.BlockSpec((8, PREV_W), lambda i: (nc - 1 - i, 0)), rev(RWKV_W)] + [ANY] * ncomm,
        out_specs=[rev(w) for w in RWKV_TOK_WIDTHS] + params_s + [ANY] * ncomm,
        out_shape=[jax.ShapeDtypeStruct((t, w), BF16) for w in RWKV_TOK_WIDTHS]
        + [jax.ShapeDtypeStruct(s, F32) for s in RWKV_PARAM_SHAPES] + _comm_out_shapes(comm, kinds),
        scratch_shapes=[pltpu.VMEM(RWKV_STATE, F32), pltpu.VMEM((8, PREV_W), F32)] + _comm_scratch(ncomm),
        compiler_params=_cparams(("arbitrary",)))(proj0, proj0, proj0, proj0, proj0, *params, states, prevs, do, *comm)
    return outs[:5], outs[5:5 + npar], outs[5 + npar:]


def _swa_load(q_ref, k_ref, v_ref, cos_ref, sin_ref, bq_ref, bk_ref, bv_ref, sk_ref):
    toks = (q_ref[...], k_ref[...], v_ref[...], cos_ref[...], sin_ref[...])
    params = (bq_ref[...], bk_ref[...], bv_ref[...], _heads(sk_ref, SWA_Q_HEADS, 1))
    return toks, params


SWA_TOK_WIDTHS = (MIX, SWA_KV, SWA_KV)
SWA_PARAM_SHAPES = [(1, MIX), (1, SWA_KV), (1, SWA_KV), (1, SWA_Q_HEADS)]
SWA_STATE = (WINDOW, SWA_KV)


def _swa_specs(c, n=None):
    toks = [_tok_spec(c, w, _col(w, name, C1), n) for name, w in zip(("q", "k", "v"), SWA_TOK_WIDTHS)]
    toks += [_tok_spec(c, LANES, 0, n), _tok_spec(c, LANES, 0, n)]
    return toks, [_full_spec(s) for s in SWA_PARAM_SHAPES]


def _swa_fwd(proj1, cos, sin, bq, bk, bv, sinks):
    t = proj1.shape[0]
    c = SWA_STEP
    nb = t // c
    toks_s, params_s = _swa_specs(c)
    state_spec = pl.BlockSpec(SWA_STATE, lambda i: (i, 0))
    kv = SWA_KV_HEADS

    def body(q_ref, k_ref, v_ref, cos_ref, sin_ref, bq_ref, bk_ref, bv_ref, sk_ref, o_ref, kst_ref, vst_ref, k_scr, v_scr):
        first = pl.program_id(0) == 0

        @pl.when(first)
        def _():
            k_scr[...] = jnp.zeros_like(k_scr)
            v_scr[...] = jnp.zeros_like(v_scr)

        kst_ref[...] = k_scr[...]
        vst_ref[...] = v_scr[...]
        toks, params = _swa_load(q_ref, k_ref, v_ref, cos_ref, sin_ref, bq_ref, bk_ref, bv_ref, sk_ref)
        outs, (kn, vn) = swa_chunk((_heads(k_scr, kv, SWA_HD), _heads(v_scr, kv, SWA_HD)), toks, params, first)
        _put_heads(o_ref, outs, SWA_HD)
        _put_heads(k_scr, kn, SWA_HD)
        _put_heads(v_scr, vn, SWA_HD)

    saved = jax.ShapeDtypeStruct((nb * WINDOW, SWA_KV), F32)
    return pl.pallas_call(
        body, name="swa_fwd", grid=(nb,), in_specs=toks_s + params_s,
        out_specs=(_tok_spec(c, MIX, 0), state_spec, state_spec),
        out_shape=(jax.ShapeDtypeStruct((t, MIX), F32), saved, saved),
        scratch_shapes=[pltpu.VMEM(SWA_STATE, F32), pltpu.VMEM(SWA_STATE, F32)],
        compiler_params=_cparams(("arbitrary",)))(proj1, proj1, proj1, cos, sin, bq, bk, bv, sinks)


def _swa_bwd(proj1, cos, sin, bq, bk, bv, sinks, kst, vst, do):
    t = proj1.shape[0]
    c = SWA_STEP
    nb = t // c
    toks_s, params_s = _swa_specs(c, nb)
    state_spec = pl.BlockSpec(SWA_STATE, lambda i: (nb - 1 - i, 0))
    kv = SWA_KV_HEADS

    def body(q_ref, k_ref, v_ref, cos_ref, sin_ref, bq_ref, bk_ref, bv_ref, sk_ref, kst_ref, vst_ref, do_ref,
             dq_ref, dk_ref, dv_ref, dbq_ref, dbk_ref, dbv_ref, dsk_ref, dk_scr, dv_scr):
        i = pl.program_id(0)

        @pl.when(i == 0)
        def _():
            dk_scr[...] = jnp.zeros_like(dk_scr)
            dv_scr[...] = jnp.zeros_like(dv_scr)
            for ref in (dbq_ref, dbk_ref, dbv_ref, dsk_ref):
                ref[...] = jnp.zeros_like(ref)

        first = i == nb - 1
        toks, params = _swa_load(q_ref, k_ref, v_ref, cos_ref, sin_ref, bq_ref, bk_ref, bv_ref, sk_ref)
        f = functools.partial(swa_chunk, first=first)
        _, vjp = jax.vjp(f, (_heads(kst_ref, kv, SWA_HD), _heads(vst_ref, kv, SWA_HD)), toks, params)
        dstate_in = (_heads(dk_scr, kv, SWA_HD), _heads(dv_scr, kv, SWA_HD))
        (dkp, dvp), (dq, dk, dv, _, _), (dbq, dbk, dbv, dsk) = vjp((_heads(do_ref, SWA_Q_HEADS, SWA_HD), dstate_in))
        dq_ref[...], dk_ref[...], dv_ref[...] = dq.astype(BF16), dk.astype(BF16), dv.astype(BF16)
        dbq_ref[...] += dbq
        dbk_ref[...] += dbk
        dbv_ref[...] += dbv
        _put_heads(dsk_ref, dsk, 1, add=True)
        _put_heads(dk_scr, dkp, SWA_HD)
        _put_heads(dv_scr, dvp, SWA_HD)

    rev = lambda w: pl.BlockSpec((c, w), lambda i: (nb - 1 - i, 0))
    return pl.pallas_call(
        body, name="swa_bwd", grid=(nb,), in_specs=toks_s + params_s + [state_spec, state_spec, rev(MIX)],
        out_specs=[rev(w) for w in SWA_TOK_WIDTHS] + params_s,
        out_shape=[jax.ShapeDtypeStruct((t, w), BF16) for w in SWA_TOK_WIDTHS] + [jax.ShapeDtypeStruct(s, F32) for s in SWA_PARAM_SHAPES],
        scratch_shapes=[pltpu.VMEM(SWA_STATE, F32), pltpu.VMEM(SWA_STATE, F32)],
        compiler_params=_cparams(("arbitrary",)))(proj1, proj1, proj1, cos, sin, bq, bk, bv, sinks, kst, vst, do)


MESH = pl.DeviceIdType.MESH
ANY = pl.BlockSpec(memory_space=pl.ANY)


def _my_place():
    return lax.axis_index("x"), lax.axis_index("y"), lax.axis_index("c")


def _all_gather(shards):
    n = len(shards)

    def body(*refs):
        in_refs, out_refs = refs[:n], refs[n:2 * n]
        send_sems, recv_sems, local_sems = refs[2 * n:]
        x, y, c = _my_place()
        me, sibling = (x, y, c), (x, y, 1 - c)
        chips = [(1 - x, y), (x, 1 - y), (1 - x, 1 - y)]

        def slot(out_ref, place):
            px, py, pc = place
            return out_ref.at[4 * px + 2 * py + pc]

        def copy(a, k, block, to, src=None):
            return pltpu.make_async_remote_copy(
                src_ref=slot(out_refs[a], block) if src is None else src, dst_ref=slot(out_refs[a], block),
                send_sem=send_sems.at[a, k], recv_sem=recv_sems.at[a, k], device_id=to, device_id_type=MESH)

        mine = [pltpu.make_async_copy(in_refs[a], slot(out_refs[a], me), local_sems.at[a]) for a in range(n)]
        for cp in mine:
            cp.start()
        first = []
        for a in range(n):
            first.append(copy(a, 0, me, sibling, src=in_refs[a]))
            first += [copy(a, 1 + j, me, (*chip, c), src=in_refs[a]) for j, chip in enumerate(chips)]
        for cp in first:
            cp.start()
        passed = []
        for j, chip in enumerate(chips):
            for a in range(n):
                copy(a, 1 + j, (*chip, c), me).wait_recv()
                fwd = copy(a, 4 + j, (*chip, c), sibling)
                fwd.start()
                passed.append(fwd)
        for a in range(n):
            copy(a, 0, sibling, me).wait_recv()
            for j, chip in enumerate(chips):
                copy(a, 4 + j, (*chip, 1 - c), me).wait_recv()
        for cp in first + passed:
            cp.wait_send()
        for cp in mine:
            cp.wait()

    return pl.pallas_call(
        body, name="all_gather_weights", in_specs=[ANY] * n, out_specs=[ANY] * n,
        out_shape=[jax.ShapeDtypeStruct((N_DEV,) + s.shape, s.dtype) for s in shards],
        scratch_shapes=_comm_scratch(n))(*shards)


def _comm_copies(in_refs, out_refs, kinds, send_sems, recv_sems, local_sems):
    x, y, c = _my_place()
    my_idx = 4 * x + 2 * y + c
    src = lambda a, idx: in_refs[a] if kinds[a] == "gather" else in_refs[a].at[idx]
    local = [pltpu.make_async_copy(src(a, my_idx), out_refs[a].at[my_idx], local_sems.at[a]) for a in range(len(kinds))]
    remote = []
    for rel in range(1, N_DEV):
        px, py, pc = x ^ ((rel >> 2) & 1), y ^ ((rel >> 1) & 1), c ^ (rel & 1)
        for a in range(len(kinds)):
            remote.append(pltpu.make_async_remote_copy(
                src_ref=src(a, 4 * px + 2 * py + pc), dst_ref=out_refs[a].at[my_idx], send_sem=send_sems.at[a, rel - 1],
                recv_sem=recv_sems.at[a, rel - 1], device_id=(px, py, pc), device_id_type=MESH))
    return local, remote


def _comm_start(local, remote):
    for cp in local + remote:
        cp.start()


def _comm_wait(local, remote):
    for cp in remote:
        cp.wait_recv()
    for cp in remote:
        cp.wait_send()
    for cp in local:
        cp.wait()


def _comm_out_shapes(arrays, kinds):
    return [jax.ShapeDtypeStruct(((N_DEV,) + a.shape) if k == "gather" else a.shape, a.dtype) for a, k in zip(arrays, kinds)]


def _comm_scratch(n):
    return [pltpu.SemaphoreType.DMA((n, N_DEV - 1)), pltpu.SemaphoreType.DMA((n, N_DEV - 1)), pltpu.SemaphoreType.DMA((n,))]


def _sequencer_exchange(name, arrays, kinds, collective_id):
    n = len(arrays)
    srcs = [jax.new_ref(a, memory_space=pltpu.MemorySpace.HBM) for a in arrays]
    dsts = [jax.empty_ref(s, memory_space=pltpu.MemorySpace.HBM) for s in _comm_out_shapes(arrays, kinds)]

    @pl.kernel(mesh=plsc.ScalarSubcoreMesh(axis_name="sequencer", num_cores=1), name=name,
               scratch_types=(pltpu.SemaphoreType.DMA((n, N_DEV - 1)), pltpu.SemaphoreType.DMA((n, N_DEV - 1))),
               compiler_params=pltpu.CompilerParams(collective_id=collective_id))
    def launch(send_sems, recv_sems):
        x, y, c = _my_place()
        my_idx = 4 * x + 2 * y + c
        peers = [(x ^ ((rel >> 2) & 1), y ^ ((rel >> 1) & 1), c ^ (rel & 1)) for rel in range(1, N_DEV)]
        barrier = pltpu.get_barrier_semaphore()
        for peer in peers:
            pl.semaphore_signal(barrier, inc=1, device_id=peer, device_id_type=MESH)
        pl.semaphore_wait(barrier, N_DEV - 1)
        copies = [pltpu.make_async_remote_copy(
            src_ref=srcs[a] if kinds[a] == "gather" else srcs[a].at[4 * px + 2 * py + pc], dst_ref=dsts[a].at[my_idx],
            send_sem=send_sems.at[a, k], recv_sem=recv_sems.at[a, k], device_id=(px, py, pc), device_id_type=MESH)
            for k, (px, py, pc) in enumerate(peers) for a in range(n)]
        for cp in copies:
            cp.start()
        for cp in copies:
            cp.wait_recv()
        for cp in copies:
            cp.wait_send()

    launch()
    return [dst[...] for dst in dsts]


def _own_block(received, mine, kind):
    my_idx = 4 * lax.axis_index("x") + 2 * lax.axis_index("y") + lax.axis_index("c")
    block = mine[None] if kind == "gather" else lax.dynamic_slice(mine, (my_idx,) + (0,) * (mine.ndim - 1), (1,) + mine.shape[1:])
    return lax.dynamic_update_slice(received, block, (my_idx,) + (0,) * (received.ndim - 1))


def _exchange(arrays, kinds):
    n = len(arrays)

    def body(*refs):
        copies = _comm_copies(refs[:n], refs[n:2 * n], kinds, *refs[2 * n:])
        _comm_start(*copies)
        _comm_wait(*copies)

    return pl.pallas_call(body, name="exchange_grads", in_specs=[ANY] * n, out_specs=[ANY] * n,
                          out_shape=_comm_out_shapes(arrays, kinds), scratch_shapes=_comm_scratch(n))(*arrays)


def _adam_math(w, g, m, v):
    m = ADAM_B1 * m + (1.0 - ADAM_B1) * g
    v = ADAM_B2 * v + (1.0 - ADAM_B2) * (g * g)
    m_hat = m / (1.0 - ADAM_B1 ** ADAM_STEP)
    v_hat = v / (1.0 - ADAM_B2 ** ADAM_STEP)
    delta = -ADAM_LR * (m_hat / (jnp.sqrt(v_hat) + ADAM_EPS) + ADAM_WD * w)
    return delta, m, v


def _adamw(name, w, gslots, m, v, tc):
    r, cc = w.shape
    assert cc % tc == 0
    tile = pl.BlockSpec((r, tc), lambda i: (0, i))

    def body(w_ref, g_ref, m_ref, v_ref, go_ref, d_ref, mo_ref, vo_ref):
        g = g_ref[0].astype(F32)
        for s in range(1, N_DEV):
            g = g + g_ref[s].astype(F32)
        d, mn, vn = _adam_math(w_ref[...], g, m_ref[...], v_ref[...])
        go_ref[...] = g
        d_ref[...] = d
        mo_ref[...] = mn
        vo_ref[...] = vn

    shp = jax.ShapeDtypeStruct((r, cc), F32)
    return pl.pallas_call(body, name=name, grid=(cc // tc,),
                          in_specs=[tile, pl.BlockSpec((N_DEV, r, tc), lambda i: (0, 0, i)), tile, tile],
                          out_specs=(tile,) * 4, out_shape=(shp,) * 4, compiler_params=_cparams(("arbitrary",)))(w, gslots, m, v)


PACK_TILE = 8 * LANES


def _packed_rows(shape, mode):
    r, w = shape
    return -(-r // 8) * 8 if mode == "rows" else -(-(r * w) // PACK_TILE) * 8


def _pack_small(arrays, modes, lead=False):
    out = []
    for a, mode in zip(arrays, modes):
        a = a.astype(F32) if lead else a.astype(F32)[None]
        if mode == "rows":
            out.append(jnp.pad(a, ((0, 0), (0, (-a.shape[1]) % 8), (0, LANES - a.shape[2]))))
        else:
            flat = a.reshape(a.shape[0], -1)
            out.append(jnp.pad(flat, ((0, 0), (0, (-flat.shape[1]) % PACK_TILE))).reshape(a.shape[0], -1, LANES))
    out = jnp.concatenate(out, axis=1)
    return out if lead else out[0]


def _take_small(packed, row0, shape, mode):
    r, w = shape
    lead = packed.ndim == 3
    if mode == "rows":
        return packed[:, row0:row0 + r, :w] if lead else packed[row0:row0 + r, :w]
    per_row = -(-w // LANES)
    if lead:
        return packed[:, row0:row0 + r * per_row].reshape(packed.shape[0], r, per_row * LANES)[:, :, :w]
    rows = []
    for i in range(r):
        pieces = [packed[row0 + i * per_row + j:row0 + i * per_row + j + 1, :] for j in range(per_row)]
        rows.append((pieces[0] if per_row == 1 else jnp.concatenate(pieces, axis=1))[:, :w])
    return rows[0] if r == 1 else jnp.concatenate(rows, axis=0)


def _adamw_small(slots, specs, ws, ms, vs, loss_row):
    n = len(specs)

    def body(*refs):
        slots_ref, w_refs, m_refs, v_refs = refs[0], refs[1:1 + n], refs[1 + n:1 + 2 * n], refs[1 + 2 * n:1 + 3 * n]
        out_refs, loss_ref = refs[1 + 3 * n:1 + 7 * n], refs[1 + 7 * n]
        gp = slots_ref[0]
        for s in range(1, N_DEV):
            gp = gp + slots_ref[s]
        read = lambda ref: ref[0] if len(ref.shape) == 3 else ref[...]
        for k, (shape, mode, row0) in enumerate(specs):
            g = _take_small(gp, row0, shape, mode)
            d, mn, vn = _adam_math(read(w_refs[k]), g, read(m_refs[k]), read(v_refs[k]))
            for ref, val in zip(out_refs[4 * k:4 * k + 4], (g, d, mn, vn)):
                if len(ref.shape) == 3:
                    ref[0] = val
                else:
                    ref[...] = val
        loss_ref[...] = gp[loss_row:loss_row + 1, :]

    vmem = pl.BlockSpec(memory_space=pltpu.VMEM)
    out_shape = [jax.ShapeDtypeStruct(w.shape, F32) for w in ws for _ in range(4)] + [jax.ShapeDtypeStruct((1, LANES), F32)]
    outs = pl.pallas_call(body, name="adamw_small", in_specs=[vmem] * (1 + 3 * n), out_specs=[vmem] * (4 * n + 1),
                          out_shape=out_shape)(slots, *ws, *ms, *vs)
    return [outs[4 * k:4 * k + 4] for k in range(n)], outs[4 * n]


def _rope_tables(t):
    dim = jnp.arange(LANES) % SWA_HD
    inv_freq = ROPE_THETA ** (-(dim % ROPE_HALF).astype(F32) / ROPE_HALF)
    ang = jnp.arange(t, dtype=F32)[:, None] * jnp.where(dim < 2 * ROPE_HALF, inv_freq, 0.0)[None, :]
    return jnp.cos(ang), jnp.sin(ang)


def _pad_to(a, rows=None, cols=None):
    r = 0 if rows is None else rows - a.shape[0]
    c = 0 if cols is None else cols - a.shape[1]
    return jnp.pad(a, ((0, r), (0, c)))


ORIG0 = dict(gq=(0, 256), gk=(256, 256), gv=(512, 512), glow=(1024, 16), r=(1040, 512), k=(1552, 512), v=(2064, 512),
             xw=(2576, 64), xa=(2640, 64), gate=(2704, 1024))
ORIG0_ORDER = ["gq", "gk", "gv", "glow", "r", "k", "v", "xw", "xa", "gate"]


def _w0t_to_padded(wt):
    rows, at = [], 0
    for name, (off, width) in sorted(C0.items(), key=lambda kv: kv[1][0]):
        assert off == at
        src, src_w = ORIG0[name]
        rows.append(_pad_to(wt[src:src + src_w], rows=width))
        at += width
    rows.append(jnp.zeros((N0P - at, wt.shape[1]), wt.dtype))
    return jnp.concatenate(rows, axis=0)


def _w0t_from_padded(wpt):
    return jnp.concatenate([wpt[C0[n][0]:C0[n][0] + ORIG0[n][1]] for n in ORIG0_ORDER], axis=0)


def _w1t_to_mine(wt):
    return jnp.concatenate([wt[1536:2560], wt[:1536]], axis=0)


def _w1t_from_mine(wt):
    return jnp.concatenate([wt[1024:2560], wt[:1024]], axis=0)


def kernel(x, norm_w, w_in0, gla_gk_up, gla_gk_bias, gla_norm_w, rwkv_mu, rwkv_w0, rwkv_w_up, rwkv_a0, rwkv_a_up, rwkv_k_k, rwkv_k_a, rwkv_r_k, rwkv_ln_w, rwkv_ln_b, w_out0, w_in1, b_in1, attn_sinks, w_out1, b_out1, final_norm_w, loss_target, m_norm_w, m_w_in0, m_gla_gk_up, m_gla_gk_bias, m_gla_norm_w, m_rwkv_mu, m_rwkv_w0, m_rwkv_w_up, m_rwkv_a0, m_rwkv_a_up, m_rwkv_k_k, m_rwkv_k_a, m_rwkv_r_k, m_rwkv_ln_w, m_rwkv_ln_b, m_w_out0, m_w_in1, m_b_in1, m_attn_sinks, m_w_out1, m_b_out1, m_final_norm_w, v_norm_w, v_w_in0, v_gla_gk_up, v_gla_gk_bias, v_gla_norm_w, v_rwkv_mu, v_rwkv_w0, v_rwkv_w_up, v_rwkv_a0, v_rwkv_a_up, v_rwkv_k_k, v_rwkv_k_a, v_rwkv_r_k, v_rwkv_ln_w, v_rwkv_ln_b, v_w_out0, v_w_in1, v_b_in1, v_attn_sinks, v_w_out1, v_b_out1, v_final_norm_w):
    weights = dict(norm_w=norm_w, w_in0=w_in0, gla_gk_up=gla_gk_up, gla_gk_bias=gla_gk_bias, gla_norm_w=gla_norm_w, rwkv_mu=rwkv_mu,
                   rwkv_w0=rwkv_w0, rwkv_w_up=rwkv_w_up, rwkv_a0=rwkv_a0, rwkv_a_up=rwkv_a_up, rwkv_k_k=rwkv_k_k, rwkv_k_a=rwkv_k_a,
                   rwkv_r_k=rwkv_r_k, rwkv_ln_w=rwkv_ln_w, rwkv_ln_b=rwkv_ln_b, w_out0=w_out0, w_in1=w_in1, b_in1=b_in1,
                   attn_sinks=attn_sinks, w_out1=w_out1, b_out1=b_out1, final_norm_w=final_norm_w)
    moms = dict(norm_w=m_norm_w, w_in0=m_w_in0, gla_gk_up=m_gla_gk_up, gla_gk_bias=m_gla_gk_bias, gla_norm_w=m_gla_norm_w,
                rwkv_mu=m_rwkv_mu, rwkv_w0=m_rwkv_w0, rwkv_w_up=m_rwkv_w_up, rwkv_a0=m_rwkv_a0, rwkv_a_up=m_rwkv_a_up,
                rwkv_k_k=m_rwkv_k_k, rwkv_k_a=m_rwkv_k_a, rwkv_r_k=m_rwkv_r_k, rwkv_ln_w=m_rwkv_ln_w, rwkv_ln_b=m_rwkv_ln_b,
                w_out0=m_w_out0, w_in1=m_w_in1, b_in1=m_b_in1, attn_sinks=m_attn_sinks, w_out1=m_w_out1, b_out1=m_b_out1,
                final_norm_w=m_final_norm_w)
    vars_ = dict(norm_w=v_norm_w, w_in0=v_w_in0, gla_gk_up=v_gla_gk_up, gla_gk_bias=v_gla_gk_bias, gla_norm_w=v_gla_norm_w,
                 rwkv_mu=v_rwkv_mu, rwkv_w0=v_rwkv_w0, rwkv_w_up=v_rwkv_w_up, rwkv_a0=v_rwkv_a0, rwkv_a_up=v_rwkv_a_up,
                 rwkv_k_k=v_rwkv_k_k, rwkv_k_a=v_rwkv_k_a, rwkv_r_k=v_rwkv_r_k, rwkv_ln_w=v_rwkv_ln_w, rwkv_ln_b=v_rwkv_ln_b,
                 w_out0=v_w_out0, w_in1=v_w_in1, b_in1=v_b_in1, attn_sinks=v_attn_sinks, w_out1=v_w_out1, b_out1=v_b_out1,
                 final_norm_w=v_final_norm_w)
    names = list(weights)
    big = ["w_in0", "w_out0", "w_in1", "w_out1"]
    small_sharded = ["gla_gk_up", "rwkv_w_up", "rwkv_a_up", "b_in1", "b_out1"]
    replicated = [n for n in names if n not in big and n not in small_sharded]

    xs = x[0]
    tgt = loss_target[0]
    t = xs.shape[0]

    def view(w):
        shape = tuple(w.shape[-2:]) if w.ndim >= 2 else (1, w.shape[0])
        return shape, ("rows" if shape[0] > 1 and shape[1] <= LANES else "flat")

    def layout(ns, row0=0):
        specs = []
        for n in ns:
            shape, mode = view(weights[n])
            specs.append((shape, mode, row0))
            row0 += _packed_rows(shape, mode)
        return specs, row0

    late = ["norm_w"]
    replicated = [n for n in replicated if n not in late]
    sh_specs, n_shard_rows = layout(small_sharded)
    rep_specs, loss_row = layout(replicated, n_shard_rows)
    late_specs, _ = layout(late, loss_row + _packed_rows((1, 1), "flat"))
    sh_modes, rep_modes, late_modes = [s[1] for s in sh_specs], [s[1] for s in rep_specs], [s[1] for s in late_specs]

    small_shard_pack = _pack_small([weights[n].reshape(view(weights[n])[0]) for n in small_sharded], sh_modes)
    g_in0, g_small = _all_gather([w_in0[0].T.astype(BF16), small_shard_pack])
    w0t = _w0t_to_padded(g_in0.reshape(-1, D_MODEL))
    later_shards = [w_out0[0].astype(BF16), w_in1[0].T.astype(BF16), w_out1[0].astype(BF16)]
    gs = [_take_small(g_small, row0, shape, mode) for shape, mode, row0 in sh_specs]
    join_cols = lambda a: jnp.transpose(a, (1, 0, 2)).reshape(a.shape[1], -1)
    gk_up, w_up, a_up = join_cols(gs[0]), join_cols(gs[1]), join_cols(gs[2])
    b_in, b_out = gs[3].reshape(1, -1), gs[4].reshape(1, -1)

    gk_up_p = _pad_to(gk_up, rows=LOW)
    w3, rank = 3 * RWKV_W, rwkv_w_up.shape[1]
    mu = rwkv_mu
    rwkv_params = [mu[:, 0:RWKV_W], mu[:, RWKV_W:2 * RWKV_W], mu[:, 2 * RWKV_W:w3], _pad_to(mu[:, w3:w3 + rank], cols=LOW),
                   _pad_to(mu[:, w3 + rank:], cols=LOW), rwkv_w0, _pad_to(w_up, rows=LOW), rwkv_a0, _pad_to(a_up, rows=LOW),
                   rwkv_k_k, rwkv_k_a, rwkv_r_k.reshape(1, RWKV_W), rwkv_ln_w, rwkv_ln_b]
    bq, bk, bv = b_in[:, :MIX], b_in[:, MIX:MIX + SWA_KV], b_in[:, MIX + SWA_KV:]
    cos, sin = _rope_tables(t)
    nw0, nw1, fw = norm_w[0:1], norm_w[1:2], final_norm_w.reshape(1, D_MODEL)

    d = D_MODEL
    wide = lambda arr: (arr, d, 0)
    silu = lambda g: g * sigmoid(g)
    hn0, proj0 = _matmul_fused("norm0_proj0", rms, w0t, "nt", [wide(xs)], [nw0], [(N0P, F32)], [], lambda acc, x, w: (acc,))
    o_a, gla_states = _gla_fwd(proj0, gk_up_p, gla_gk_bias, gla_norm_w)
    o_b, rwkv_states, rwkv_prevs, (g_out0, g_in1, g_out1) = _rwkv_fwd(proj0, rwkv_params, later_shards, ["gather"] * 3)
    wo0 = g_out0.reshape(MIX, D_MODEL)
    w1t = _w1t_to_mine(g_in1.reshape(-1, D_MODEL))
    wo1 = g_out1.reshape(MIX, D_MODEL)
    og0, h1, hn1 = _matmul_fused(
        "gate0_out0_norm1", lambda oa, ob, gate, x, w: jnp.concatenate([oa, ob], axis=1) * silu(gate), wo0, "nn",
        [(o_a, GLA_VAL, 0), (o_b, RWKV_W, 0), wide(proj0), wide(xs)], [nw1], [(d, F32), (d, BF16)], [],
        lambda acc, oa, ob, gate, x, w: _resid_norm(acc, x, w))
    proj1 = _matmul("proj1", hn1, w1t, "nt", PROJ_ROWS, N1P // 2)
    o_c, kst, vst = _swa_fwd(proj1, cos, sin, bq, bk, bv, attn_sinks)
    og1, dh2, loss_part, d_b_out, d_fw = _matmul_fused(
        "gate1_out1_loss", lambda oc, gate, h, tg, b, w: oc * silu(gate), wo1, "nn",
        [wide(o_c), wide(proj1), wide(h1), wide(tgt)], [b_out, fw], [(d, F32)], [LANES, d, d],
        lambda acc, oc, gate, h, tg, b, w: _loss_head(acc, h, tg, b, w))

    d_oc, d_gate1 = _matmul_fused("out1_dx_gate1", dh2, wo1, "nt", [wide(o_c), wide(proj1)], [], [(d, F32), (d, BF16)], [], _gate_back)
    d_wo1 = _matmul("out1_dw", og1, dh2, "tn", DW_COLS, DW_COLS, BF16)
    dq, dk, dv, d_bq, d_bk, d_bv, d_sinks = _swa_bwd(proj1, cos, sin, bq, bk, bv, attn_sinks, kst, vst, d_oc)
    dproj1 = jnp.concatenate([d_gate1, dq, dk, dv], axis=1)
    dh1, d_nw1 = _matmul_fused("proj1_dx_norm1", dproj1, w1t, "nn", [wide(h1), wide(dh2)], [nw1], [(d, F32)], [d], _norm_back)
    d_w1t = _matmul("proj1_dw", dproj1, hn1, "tn", DW_COLS, d, BF16)
    d_oa, d_ob, d_gate0 = _matmul_fused("out0_dx_gate0", dh1, wo0, "nt", [(o_a, GLA_VAL, 0), (o_b, RWKV_W, 0), wide(proj0)], [],
                                        [(GLA_VAL, F32), (RWKV_W, F32), (d, BF16)], [], _gate_back)
    d_wo0 = _matmul("out0_dw", og0, dh1, "tn", DW_COLS, DW_COLS, BF16)
    dgq, dgk, dgv, dglow, d_gk_up, d_gk_bias, d_gla_nw = _gla_bwd(proj0, gk_up_p, gla_gk_bias, gla_norm_w, gla_states, d_oa)
    row_blocks = lambda a: a.astype(BF16).reshape(N_DEV, -1, D_MODEL)
    early = [row_blocks(_w1t_from_mine(d_w1t)), row_blocks(d_wo1), row_blocks(d_wo0)]
    (dr, dkk, dvv, dxw, dxa), d_rp, (r_in1, r_out1, r_out0) = _rwkv_bwd(
        proj0, rwkv_params, rwkv_states, rwkv_prevs, d_ob, early, ["scatter"] * 3)
    pad = jnp.zeros((t, N0P - C0["xa"][0] - C0["xa"][1]), BF16)
    dproj0 = jnp.concatenate([d_gate0, dgv, dr, dkk, dvv, dgq, dgk, dglow, dxw, dxa, pad], axis=1)
    d_w0 = row_blocks(_w0t_from_padded(_matmul("proj0_dw", dproj0, hn0, "tn", DW_COLS, d, BF16)))

    contrib = dict(
        gla_gk_bias=d_gk_bias, gla_norm_w=d_gla_nw,
        rwkv_mu=jnp.concatenate([d_rp[0], d_rp[1], d_rp[2], d_rp[3][:, :rank], d_rp[4][:, :rank]], axis=1),
        rwkv_w0=d_rp[5], rwkv_a0=d_rp[7], rwkv_k_k=d_rp[9], rwkv_k_a=d_rp[10], rwkv_r_k=d_rp[11].reshape(RWKV_HEADS, RWKV_N),
        rwkv_ln_w=d_rp[12], rwkv_ln_b=d_rp[13], attn_sinks=d_sinks, final_norm_w=d_fw)
    rep_pack = _pack_small([contrib[n] for n in replicated] + [loss_part[:, :1]], rep_modes + ["flat"])
    d_b_in = jnp.concatenate([d_bq, d_bk, d_bv], axis=1)
    full_small = [d_gk_up[:gk_up.shape[0]], d_rp[6][:rank], d_rp[8][:rank], d_b_in, d_b_out]
    split_cols = lambda a: jnp.transpose(a.reshape(a.shape[0], N_DEV, -1), (1, 0, 2))
    small_pack = _pack_small([split_cols(a) for a in full_small], sh_modes, lead=True)

    sent, sent_kinds = [d_w0, small_pack, rep_pack], ["scatter", "scatter", "gather"]
    received = _sequencer_exchange("exchange_last_grads", sent, sent_kinds, 0)
    grad_x, d_nw0 = _matmul_fused("proj0_dx_norm0", dproj0, w0t, "nn", [wide(xs), wide(dh1)], [nw0], [(d, F32)], [d], _norm_back)
    late_pack = _pack_small([jnp.concatenate([d_nw0, d_nw1], axis=0)], late_modes)
    r_late, = _exchange([late_pack], ["gather"])

    res = {}
    res["w_out0"] = tuple(a[None] for a in _adamw("adamw_w_out0", w_out0[0], r_out0, m_w_out0[0], v_w_out0[0], ADAM_COLS))
    res["w_in1"] = tuple(a.T[None] for a in _adamw("adamw_w_in1", w_in1[0].T, r_in1, m_w_in1[0].T, v_w_in1[0].T, ADAM_COLS))
    res["w_out1"] = tuple(a[None] for a in _adamw("adamw_w_out1", w_out1[0], r_out1, m_w_out1[0], v_w_out1[0], ADAM_COLS))
    r_in0, r_small, r_rep = [_own_block(r, mine, kind) for r, mine, kind in zip(received, sent, sent_kinds)]
    small_names = small_sharded + replicated + late
    slots = jnp.concatenate([r_small, r_rep, r_late], axis=1)
    as_2d = lambda a: a.reshape(1, -1) if a.ndim == 1 else a
    small_res, loss_row_out = _adamw_small(slots, sh_specs + rep_specs + late_specs, [as_2d(weights[n]) for n in small_names],
                                           [as_2d(moms[n]) for n in small_names], [as_2d(vars_[n]) for n in small_names], loss_row)
    for n, vals in zip(small_names, small_res):
        res[n] = tuple(val.reshape(weights[n].shape) for val in vals)
    loss = loss_row_out[0, 0]
    res["w_in0"] = tuple(a.T[None] for a in _adamw("adamw_w_in0", w_in0[0].T, r_in0, m_w_in0[0].T, v_w_in0[0].T, ADAM_COLS))
    return (loss, grad_x[None], *[res[n][0] for n in names], *[res[n][1] for n in names],
            *[res[n][2] for n in names], *[res[n][3] for n in names])
```

```python
import functools

import jax
import jax.numpy as jnp
from jax import lax
from jax.experimental import pallas as pl
from jax.experimental.pallas import tpu as pltpu
from jax.experimental.pallas import tpu_sc as plsc

F32 = jnp.float32
BF16 = jnp.bfloat16
HI = lax.Precision.HIGHEST

D_MODEL = 1024
NORM_EPS = 1e-5
GLA_HEADS, GLA_DK, GLA_DV = 4, 64, 128
GLA_NORMALIZER = 16.0
GLA_CHUNK = 64
GLA_STEP = 512
RWKV_HEADS, RWKV_N = 8, 64
RWKV_LN_EPS = 64e-5
RWKV_CHUNK = 128
SWA_Q_HEADS, SWA_KV_HEADS, SWA_GROUP, SWA_HD = 16, 4, 4, 64
WINDOW = 128
SWA_STEP = 512
ROPE_THETA = 500000.0
NEG = -1e30
N_DEV = 8
LANES = 128

ADAM_LR, ADAM_B1, ADAM_B2, ADAM_EPS, ADAM_WD, ADAM_STEP = 0.001, 0.9, 0.999, 1e-08, 0.01, 10

GLA_KEY, GLA_VAL = GLA_HEADS * GLA_DK, GLA_HEADS * GLA_DV
RWKV_W = RWKV_HEADS * RWKV_N
SWA_KV = SWA_KV_HEADS * SWA_HD
MIX = GLA_VAL + RWKV_W
LOW = LANES

N0P = 4096
C0 = dict(gate=(0, MIX), gv=(1024, GLA_VAL), r=(1536, RWKV_W), k=(2048, RWKV_W), v=(2560, RWKV_W), gq=(3072, GLA_KEY),
          gk=(3328, GLA_KEY), glow=(3584, LOW), xw=(3712, LOW), xa=(3840, LOW))
N1P = 2560
C1 = dict(gate=(0, MIX), q=(1024, MIX), k=(2048, SWA_KV), v=(2304, SWA_KV))

VMEM_LIMIT = 56 * 1024 * 1024

P_LORA = 1
P_GLA = 1
P_RWKV_G = 2
P_RWKV = 1
P_SWA = 1


def _cparams(sem=None):
    return pltpu.CompilerParams(dimension_semantics=sem, vmem_limit_bytes=VMEM_LIMIT)


DIMS = dict(nn=(((1,), (0,)), ((), ())), nt=(((1,), (1,)), ((), ())), tn=(((0,), (0,)), ((), ())))


def _split_bf16(a):
    hi = a.astype(BF16)
    return hi, (a - hi.astype(F32)).astype(BF16)


def _dot(a, b, mode, passes):
    dg = lambda p, q: lax.dot_general(p, q, DIMS[mode], preferred_element_type=F32)
    if passes == 1:
        return dg(a.astype(BF16), b.astype(BF16))
    if passes == 2:
        ah, (bh, bl) = a.astype(BF16), _split_bf16(b)
        return dg(ah, bh) + dg(ah, bl)
    if passes == 3:
        (ah, al), (bh, bl) = _split_bf16(a), _split_bf16(b)
        return dg(ah, bh) + dg(al, bh) + dg(ah, bl)
    return lax.dot_general(a, b, DIMS[mode], precision=HI, preferred_element_type=F32)


@functools.partial(jax.custom_vjp, nondiff_argnums=(2, 3))
def mmx(a, b, mode, passes):
    return _dot(a, b, mode, passes)


def _mmx_fwd(a, b, mode, passes):
    return _dot(a, b, mode, passes), (a, b)


def _mmx_bwd(mode, passes, res, g):
    a, b = res
    if mode == "nn":
        return _dot(g, b, "nt", passes), _dot(a, g, "tn", passes)
    if mode == "nt":
        return _dot(g, b, "nn", passes), _dot(g, a, "tn", passes)
    return _dot(b, g, "nt", passes), _dot(a, g, "nn", passes)


mmx.defvjp(_mmx_fwd, _mmx_bwd)


def _tri_dot(tri, x):
    t = tri.astype(BF16)
    x1 = x.astype(BF16)
    r1 = x - x1.astype(F32)
    x2 = r1.astype(BF16)
    x3 = (r1 - x2.astype(F32)).astype(BF16)
    dg = lambda q: jnp.dot(t, q, preferred_element_type=F32)
    return dg(x1) + dg(x2) + dg(x3)


@jax.custom_vjp
def cumsum_rows(x):
    return _tri_dot(tril_ones(x.shape[0]), x)


def _cumsum_fwd(x):
    return cumsum_rows(x), None


def _cumsum_bwd(_, g):
    i, j = _iota2(g.shape[0], g.shape[0])
    return (_tri_dot(jnp.where(i <= j, 1.0, 0.0).astype(F32), g),)


cumsum_rows.defvjp(_cumsum_fwd, _cumsum_bwd)


def _head_dot(x):
    i, j = _iota2(LANES, LANES)
    shift = RWKV_N.bit_length() - 1
    same = jnp.where(jnp.right_shift(i, shift) == jnp.right_shift(j, shift), 1.0, 0.0).astype(F32)
    return jnp.concatenate([_ones_right(x[:, g * LANES:(g + 1) * LANES], same) for g in range(x.shape[1] // LANES)], axis=1)


def _ones_right(x, ones):
    t = ones.astype(BF16)
    x1 = x.astype(BF16)
    x2 = (x - x1.astype(F32)).astype(BF16)
    dg = lambda q: jnp.dot(q, t, preferred_element_type=F32)
    return dg(x1) + dg(x2)


@jax.custom_vjp
def head_sum(x):
    return _head_dot(x)


def _head_sum_fwd(x):
    return head_sum(x), None


def _head_sum_bwd(_, g):
    return (_head_dot(g),)


head_sum.defvjp(_head_sum_fwd, _head_sum_bwd)


def cat_rows(*xs):
    return jnp.concatenate(xs, axis=0)


def _iota2(n, m):
    return lax.broadcasted_iota(jnp.int32, (n, m), 0), lax.broadcasted_iota(jnp.int32, (n, m), 1)


def tril_ones(c, strict=False):
    i, j = _iota2(c, c)
    return jnp.where((i > j) if strict else (i >= j), 1.0, 0.0).astype(F32)


def row_of(x, r):
    i = lax.broadcasted_iota(jnp.int32, x.shape, 0)
    return jnp.sum(jnp.where(i == r, x, 0.0), axis=0, keepdims=True)


@jax.custom_vjp
def shift_rows(x, prev):
    r = lax.broadcasted_iota(jnp.int32, x.shape, 0)
    return jnp.where(r == 0, prev, pltpu.roll(x, 1, 0))


def _shift_fwd(x, prev):
    return shift_rows(x, prev), None


def _shift_bwd(_, g):
    c = g.shape[0]
    r = lax.broadcasted_iota(jnp.int32, g.shape, 0)
    return jnp.where(r == c - 1, 0.0, pltpu.roll(g, c - 1, 0)), row_of(g, 0)


shift_rows.defvjp(_shift_fwd, _shift_bwd)


def log_sigmoid(x):
    return jnp.minimum(x, 0.0) - jnp.log(1.0 + jnp.exp(-jnp.abs(x)))


def softplus(x):
    return jnp.maximum(x, 0.0) + jnp.log(1.0 + jnp.exp(-jnp.abs(x)))


def sigmoid(x):
    return 1.0 / (1.0 + jnp.exp(-x))


def rms(x, w, eps=NORM_EPS):
    return x * lax.rsqrt(jnp.mean(x * x, axis=-1, keepdims=True) + eps) * w


def gla_chunk(state, toks, params):
    q, k, v, glow = toks
    gk_up, bias, norm_w = params
    c = GLA_CHUNK
    subs, heads = range(glow.shape[0] // c), range(GLA_HEADS)
    rows = lambda x, j: x[j * c:(j + 1) * c]
    hk = lambda x, h: x[:, h * GLA_DK:(h + 1) * GLA_DK]
    hv = lambda x, h: x[:, h * GLA_DV:(h + 1) * GLA_DV]
    ltri = tril_ones(c)
    g = log_sigmoid(mmx(glow, gk_up, "nn", P_LORA) + bias) / GLA_NORMALIZER
    b = [cumsum_rows(rows(g, j)) for j in subs]
    ref = [lax.stop_gradient(row_of(b[j], c // 2)) for j in subs]
    last = [row_of(b[j], c - 1) for j in subs]
    ql = [rows(q, j) * (GLA_DK ** -0.5) * jnp.exp(b[j] - ref[j]) for j in subs]
    kr = [rows(k, j) * jnp.exp(ref[j] - b[j]) for j in subs]
    kl = [rows(k, j) * jnp.exp(last[j] - b[j]) for j in subs]
    vj = [rows(v, j) for j in subs]
    e_ref, e_last = [jnp.exp(x) for x in ref], [jnp.exp(x) for x in last]
    att = [[mmx(hk(ql[j], h), hk(kr[j], h), "nt", P_GLA) * ltri for h in heads] for j in subs]
    o_in = [[mmx(att[j][h], hv(vj[j], h), "nn", P_GLA) for h in heads] for j in subs]
    kv = [[mmx(hv(vj[j], h), hk(kl[j], h), "tn", P_GLA) for h in heads] for j in subs]
    o = []
    for j in subs:
        o.append([o_in[j][h] + mmx(hk(ql[j], h), state[h] * hk(e_ref[j], h), "nt", P_GLA) for h in heads])
        state = [state[h] * hk(e_last[j], h) + kv[j][h] for h in heads]
    o = [[x * lax.rsqrt(jnp.mean(x * x, axis=-1, keepdims=True) + NORM_EPS) * norm_w for x in oj] for oj in o]
    return cat_rows(*[jnp.concatenate(oj, axis=1) for oj in o]), state


SOLVE_BLOCK = 128


def solve_unit_lower(ps, ws):
    n = ps[0].shape[0]
    heads = range(len(ps))
    if n > SOLVE_BLOCK:
        half = n // 2
        top = solve_unit_lower([p[:half, :half] for p in ps], [w[:half] for w in ws])
        rest = [ws[h][half:] + mmx(ps[h][half:, :half], top[h], "nn", P_RWKV) for h in heads]
        bottom = solve_unit_lower([p[half:, half:] for p in ps], rest)
        return [cat_rows(top[h], bottom[h]) for h in heads]
    u, p = ws, ps
    levels = max(1, (n - 1).bit_length())
    for it in range(levels):
        if it + 1 < levels:
            y = [mmx(p[h], jnp.concatenate([p[h], u[h]], axis=1), "nn", P_RWKV) for h in heads]
            u = [u[h] + y[h][:, n:] for h in heads]
            p = [y[h][:, :n] for h in heads]
        else:
            u = [u[h] + mmx(p[h], u[h], "nn", P_RWKV) for h in heads]
    return u


def rwkv_chunk(state, toks, params):
    S, pr, pk, pv, pxw, pxa = state
    r_, k_, v_, xw_, xa_ = toks
    mu_r, mu_k, mu_v, mu_xw, mu_xa, w0, w_up, a0, a_up, k_k, k_a, r_k, ln_w, ln_b = params
    c, n = xw_.shape[0], RWKV_N
    heads = range(RWKV_HEADS)
    hs = lambda x, h: x[:, h * n:(h + 1) * n]
    ltri = tril_ones(c)
    stri = tril_ones(c, strict=True)

    def lerp(x, prev, mu):
        return x + (shift_rows(x, prev) - x) * mu

    xw = jnp.tanh(lerp(xw_, pxw, mu_xw))
    xa = lerp(xa_, pxa, mu_xa)
    r = lerp(r_, pr, mu_r)
    k = lerp(k_, pk, mu_k)
    v = lerp(v_, pv, mu_v)
    w = -softplus(-(w0 + mmx(xw, w_up, "nn", P_LORA))) - 0.5
    lw = -jnp.exp(w)
    asig = sigmoid(a0 + mmx(xa, a_up, "nn", P_LORA))
    kk = k * k_k
    kk = kk * lax.rsqrt(jnp.maximum(head_sum(kk * kk), 1e-24))
    k2 = k * (1.0 + (asig - 1.0) * k_a)
    b = kk * asig
    cum = cumsum_rows(lw)
    ref = lax.stop_gradient(row_of(cum, c // 2))
    last = row_of(cum, c - 1)
    at = -kk * jnp.exp(cum - lw - ref)
    rt = r * jnp.exp(cum - ref)
    e_out = jnp.exp(ref - cum)
    bt, kt = b * e_out, k2 * e_out
    e_tail = jnp.exp(last - cum)
    bl, kl = b * e_tail, k2 * e_tail
    e_ref, e_last = jnp.exp(ref), jnp.exp(last)
    g = [mmx(cat_rows(hs(at, h), hs(rt, h)), cat_rows(hs(bt, h), hs(kt, h), S[h] * hs(e_ref, h)), "nt", P_RWKV_G) for h in heads]
    aab = [x[:c, :c] * stri for x in g]
    aak = [x[:c, c:2 * c] * stri for x in g]
    arb = [x[c:, :c] * ltri for x in g]
    ark = [x[c:, c:2 * c] * ltri for x in g]
    av = [mmx(cat_rows(aak[h], ark[h]), hs(v, h), "nn", P_RWKV) for h in heads]
    u = solve_unit_lower(aab, [g[h][:c, 2 * c:] + av[h][:c] for h in heads])
    o = [g[h][c:, 2 * c:] + av[h][c:] + mmx(arb[h], u[h], "nn", P_RWKV) for h in heads]
    s1 = [S[h] * hs(e_last, h) + mmx(cat_rows(u[h], hs(v, h)), cat_rows(hs(bl, h), hs(kl, h)), "tn", P_RWKV) for h in heads]
    o = jnp.concatenate(o, axis=1)
    d = o - head_sum(o) * (1.0 / n)
    var = head_sum(d * d) * (1.0 / n)
    o = d * lax.rsqrt(var + RWKV_LN_EPS) * ln_w + ln_b + head_sum(r * k2 * r_k) * v
    new_state = (s1, row_of(r_, c - 1), row_of(k_, c - 1), row_of(v_, c - 1), row_of(xw_, c - 1), row_of(xa_, c - 1))
    return o, new_state


ROPE_HALF = 8


def _rot_half_raw(x):
    lane = lax.broadcasted_iota(jnp.int32, (x.shape[0], LANES), 1) & (SWA_HD - 1)
    out = []
    for i in range(x.shape[1] // LANES):
        g = x[:, i * LANES:(i + 1) * LANES]
        up, down = pltpu.roll(g, LANES - ROPE_HALF, 1), pltpu.roll(g, ROPE_HALF, 1)
        out.append(jnp.where(lane < ROPE_HALF, -up, jnp.where(lane < 2 * ROPE_HALF, down, 0.0)))
    return out[0] if len(out) == 1 else jnp.concatenate(out, axis=1)


@jax.custom_vjp
def rot_half(x):
    return _rot_half_raw(x)


rot_half.defvjp(lambda x: (_rot_half_raw(x), None), lambda _, g: (-_rot_half_raw(g),))


def rope(x, cos2, sin2):
    reps = x.shape[1] // LANES
    tile = lambda t: t if reps == 1 else jnp.concatenate([t] * reps, axis=1)
    return x * tile(cos2) + rot_half(x) * tile(sin2)


def swa_chunk(state, toks, params, first):
    kprev, vprev = state
    q_, k_, v_, cos, sin = toks
    bq, bk, bv, sinks = params
    c, ng = WINDOW, SWA_GROUP
    n_sub = cos.shape[0] // c
    units = [(j, g) for j in range(n_sub) for g in range(SWA_KV_HEADS)]
    rows = lambda x, j: x[j * c:(j + 1) * c]
    hs = lambda g: range(g * ng, (g + 1) * ng)
    head = lambda x, h: x[:, h * SWA_HD:(h + 1) * SWA_HD]
    qi, kj = _iota2(ng * c, 2 * c)
    qpos = qi & (c - 1)
    cur_ok = (kj >= c) & (qpos >= kj - c)
    prev_ok = (kj < c) & (kj > qpos)
    ok = [cur_ok | (prev_ok & jnp.logical_not(first))] + [cur_ok | prev_ok] * (n_sub - 1)
    q_all = rope(q_ + bq, cos, sin) * (SWA_HD ** -0.5)
    k_all = rope(k_ + bk, cos, sin)
    v_all = v_ + bv
    k = {(j, g): rows(head(k_all, g), j) for j, g in units}
    v = {(j, g): rows(head(v_all, g), j) for j, g in units}
    q = {(j, g): cat_rows(*[rows(head(q_all, h), j) for h in hs(g)]) for j, g in units}
    kp = lambda j, g: kprev[g] if j == 0 else k[(j - 1, g)]
    vp = lambda j, g: vprev[g] if j == 0 else v[(j - 1, g)]
    s = {(j, g): jnp.where(ok[j], mmx(q[(j, g)], cat_rows(kp(j, g), k[(j, g)]), "nt", P_SWA), NEG) for j, g in units}
    sink = [cat_rows(*[jnp.broadcast_to(sinks[h], (c, 1)) for h in hs(g)]) for g in range(SWA_KV_HEADS)]
    m = {(j, g): lax.stop_gradient(jnp.maximum(jnp.max(s[(j, g)], axis=-1, keepdims=True), sink[g])) for j, g in units}
    p = {u: jnp.exp(s[u] - m[u]) for u in units}
    ones = jnp.ones((2 * c, SWA_HD), F32)
    pv = {(j, g): mmx(p[(j, g)], cat_rows(vp(j, g), v[(j, g)]), "nn", P_SWA) for j, g in units}
    den = {u: mmx(p[u], ones, "nn", P_SWA) for u in units}
    o = {(j, g): pv[(j, g)] / (den[(j, g)] + jnp.exp(sink[g] - m[(j, g)])) for j, g in units}
    outs = [cat_rows(*[o[(j, g)][i * c:(i + 1) * c] for j in range(n_sub)]) for g in range(SWA_KV_HEADS) for i in range(ng)]
    last = n_sub - 1
    return outs, ([k[(last, g)] for g in range(SWA_KV_HEADS)], [v[(last, g)] for g in range(SWA_KV_HEADS)])


def _heads(ref, n, w, rows=slice(None)):
    return [ref[rows, h * w:(h + 1) * w] for h in range(n)]


def _put_heads(ref, vals, w, rows=slice(None), add=False):
    for h, val in enumerate(vals):
        if add:
            ref[rows, h * w:(h + 1) * w] += val
        else:
            ref[rows, h * w:(h + 1) * w] = val


def _col(block_w, name, table):
    off, w = table[name]
    assert off % block_w == 0 and w % block_w == 0
    return off // block_w


def _tok_spec(c, w, colblock, n=None):
    if n is None:
        return pl.BlockSpec((c, w), lambda i: (i, colblock))
    return pl.BlockSpec((c, w), lambda i: (n - 1 - i, colblock))


def _full_spec(shape):
    return pl.BlockSpec(shape, lambda i: (0,) * len(shape))


def _matmul(name, a, b, mode, tm, tn, out_dtype=F32):
    (m, kd) = (a.shape[1], a.shape[0]) if mode == "tn" else a.shape
    n = b.shape[0] if mode == "nt" else b.shape[1]
    assert m % tm == 0 and n % tn == 0
    a_spec = pl.BlockSpec((kd, tm), lambda j, i: (0, i)) if mode == "tn" else pl.BlockSpec((tm, kd), lambda j, i: (i, 0))
    b_spec = pl.BlockSpec((tn, kd), lambda j, i: (j, 0)) if mode == "nt" else pl.BlockSpec((kd, tn), lambda j, i: (0, j))

    def body(a_ref, b_ref, o_ref):
        o_ref[...] = lax.dot_general(a_ref[...].astype(BF16), b_ref[...].astype(BF16), DIMS[mode],
                                     preferred_element_type=F32).astype(out_dtype)

    return pl.pallas_call(
        body, name=name, grid=(n // tn, m // tm), in_specs=[a_spec, b_spec],
        out_specs=pl.BlockSpec((tm, tn), lambda j, i: (i, j)), out_shape=jax.ShapeDtypeStruct((m, n), out_dtype),
        compiler_params=_cparams(("arbitrary", "arbitrary")))(a, b)


TOK_TILE = 512
PROJ_ROWS = 1024
DW_COLS = 512
ADAM_COLS = 256


def _matmul_fused(name, a, b, mode, tiles, rows, outs, sums, epilogue, comm=(), kinds=()):
    made = callable(a)
    m = tiles[0][0].shape[0] if made else a.shape[0]
    kd = b.shape[0] if mode == "nn" else b.shape[1]
    n = b.shape[1] if mode == "nn" else b.shape[0]
    tm = TOK_TILE
    steps = m // tm
    if made:
        outs = [(kd, BF16)] + list(outs)
    nt_, nr, no, ns, ncomm = len(tiles), len(rows), len(outs), len(sums), len(comm)

    def body(*refs):
        at = 1 if made else 2
        b_ref = refs[at - 1]
        tile_refs, row_refs, comm_in = refs[at:at + nt_], refs[at + nt_:at + nt_ + nr], refs[at + nt_ + nr:at + nt_ + nr + ncomm]
        at += nt_ + nr + ncomm
        out_refs, sum_refs, comm_out = refs[at:at + no], refs[at + no:at + no + ns], refs[at + no + ns:at + no + ns + ncomm]
        sems = refs[at + no + ns + ncomm:]
        i = pl.program_id(0)

        @pl.when(i == 0)
        def _():
            if ncomm:
                _comm_start(*_comm_copies(comm_in, comm_out, kinds, *sems))
            for ref in sum_refs:
                ref[...] = jnp.zeros_like(ref)

        extras = [r[...] for r in tile_refs] + [r[...] for r in row_refs]
        a_blk = (a(*extras) if made else refs[0][...]).astype(BF16)
        acc = lax.dot_general(a_blk, b_ref[...].astype(BF16), DIMS[mode], preferred_element_type=F32)
        res = epilogue(acc, *extras)
        if made:
            res = (a_blk,) + tuple(res)
        for ref, val in zip(out_refs, res[:no]):
            ref[...] = val.astype(ref.dtype)
        for ref, val in zip(sum_refs, res[no:]):
            ref[...] += val

        if ncomm:
            @pl.when(i == steps - 1)
            def _():
                _comm_wait(*_comm_copies(comm_in, comm_out, kinds, *sems))

    in_specs = ([] if made else [pl.BlockSpec((tm, kd), lambda i: (i, 0))]) + [_full_spec(b.shape)]
    in_specs += [pl.BlockSpec((tm, w), functools.partial(lambda i, cb: (i, cb), cb=cb)) for _, w, cb in tiles]
    in_specs += [_full_spec(r.shape) for r in rows] + [ANY] * ncomm
    out_specs = [pl.BlockSpec((tm, w), lambda i: (i, 0)) for w, _ in outs] + [_full_spec((1, w)) for w in sums] + [ANY] * ncomm
    out_shape = ([jax.ShapeDtypeStruct((m, w), dt) for w, dt in outs] + [jax.ShapeDtypeStruct((1, w), F32) for w in sums]
                 + _comm_out_shapes(comm, kinds))
    return pl.pallas_call(body, name=name, grid=(steps,), in_specs=in_specs, out_specs=out_specs, out_shape=out_shape,
                          scratch_shapes=_comm_scratch(ncomm) if ncomm else [],
                          compiler_params=_cparams(("arbitrary",)))(*([] if made else [a]), b, *[t[0] for t in tiles], *rows, *comm)


def _resid_norm(y, x, w):
    h = x + y
    return h, rms(h, w)


def _norm_back(dhn, h, dres, w):
    _, vjp = jax.vjp(rms, h, w)
    dh, dw = vjp(dhn)
    return dh + dres, dw


def _gate_back(dog, *o_and_gate):
    outs, g = o_and_gate[:-1], o_and_gate[-1]
    s = sigmoid(g)
    silu, dsilu = g * s, s * (1.0 + g * (1.0 - s))
    d_outs, c = [], 0
    for o in outs:
        w = o.shape[1]
        d_outs.append(dog[:, c:c + w] * silu[:, c:c + w])
        c += w
    o_all = outs[0] if len(outs) == 1 else jnp.concatenate(outs, axis=1)
    return (*d_outs, dog * o_all * dsilu)


def _loss_head(y1, h1, target, b_out, fw):
    def f(h2, w):
        err = rms(h2, w) - target
        return 0.5 * jnp.sum(jnp.mean(err * err, axis=-1, keepdims=True), axis=0, keepdims=True)

    loss, vjp = jax.vjp(f, h1 + y1 + b_out, fw)
    dh2, dfw = vjp(jnp.ones((1, 1), F32))
    return dh2, jnp.broadcast_to(loss, (1, LANES)), jnp.sum(dh2, axis=0, keepdims=True), dfw


def _gla_load(q_ref, k_ref, v_ref, gl_ref, up_ref, bias_ref, nw_ref):
    toks = (q_ref[...], k_ref[...], v_ref[...], gl_ref[...])
    params = (up_ref[...], bias_ref[...], nw_ref[...])
    return toks, params


def _gla_specs(c, n=None):
    toks = [_tok_spec(c, GLA_KEY, _col(GLA_KEY, "gq", C0), n), _tok_spec(c, GLA_KEY, _col(GLA_KEY, "gk", C0), n),
            _tok_spec(c, GLA_VAL, _col(GLA_VAL, "gv", C0), n), _tok_spec(c, LOW, _col(LOW, "glow", C0), n)]
    return toks, [_full_spec(s) for s in GLA_PARAM_SHAPES]


GLA_PARAM_SHAPES = [(LOW, GLA_KEY), (1, GLA_KEY), (1, GLA_DV)]
GLA_STATE = (GLA_HEADS * GLA_DV, GLA_DK)


def _gla_fwd(proj0, gk_up, gk_bias, norm_w):
    t = proj0.shape[0]
    c = GLA_STEP
    nc = t // c
    toks_s, params_s = _gla_specs(c)

    def body(q_ref, k_ref, v_ref, gl_ref, up_ref, bias_ref, nw_ref, o_ref, st_ref, s_scr):
        @pl.when(pl.program_id(0) == 0)
        def _():
            s_scr[...] = jnp.zeros_like(s_scr)

        st_ref[...] = s_scr[...]
        toks, params = _gla_load(q_ref, k_ref, v_ref, gl_ref, up_ref, bias_ref, nw_ref)
        state = [s_scr[h * GLA_DV:(h + 1) * GLA_DV, :] for h in range(GLA_HEADS)]
        o_ref[...], new = gla_chunk(state, toks, params)
        for h in range(GLA_HEADS):
            s_scr[h * GLA_DV:(h + 1) * GLA_DV, :] = new[h]

    return pl.pallas_call(
        body, name="gla_fwd", grid=(nc,), in_specs=toks_s + params_s,
        out_specs=(_tok_spec(c, GLA_VAL, 0), pl.BlockSpec(GLA_STATE, lambda i: (i, 0))),
        out_shape=(jax.ShapeDtypeStruct((t, GLA_VAL), F32), jax.ShapeDtypeStruct((nc * GLA_STATE[0], GLA_DK), F32)),
        scratch_shapes=[pltpu.VMEM(GLA_STATE, F32)], compiler_params=_cparams(("arbitrary",)))(
            proj0, proj0, proj0, proj0, gk_up, gk_bias, norm_w)


def _gla_bwd(proj0, gk_up, gk_bias, norm_w, states, do):
    t = proj0.shape[0]
    c = GLA_STEP
    nc = t // c
    toks_s, params_s = _gla_specs(c, nc)

    def body(q_ref, k_ref, v_ref, gl_ref, up_ref, bias_ref, nw_ref, st_ref, do_ref,
             dq_ref, dk_ref, dv_ref, dgl_ref, dup_ref, dbias_ref, dnw_ref, ds_scr):
        @pl.when(pl.program_id(0) == 0)
        def _():
            ds_scr[...] = jnp.zeros_like(ds_scr)
            dup_ref[...] = jnp.zeros_like(dup_ref)
            dbias_ref[...] = jnp.zeros_like(dbias_ref)
            dnw_ref[...] = jnp.zeros_like(dnw_ref)

        toks, params = _gla_load(q_ref, k_ref, v_ref, gl_ref, up_ref, bias_ref, nw_ref)
        rows = lambda h: slice(h * GLA_DV, (h + 1) * GLA_DV)
        state = [st_ref[rows(h), :] for h in range(GLA_HEADS)]
        _, vjp = jax.vjp(gla_chunk, state, toks, params)
        dstate_in = [ds_scr[rows(h), :] for h in range(GLA_HEADS)]
        dstate, dtoks, (dup, dbias, dnw) = vjp((do_ref[...], dstate_in))
        for ref, val in zip((dq_ref, dk_ref, dv_ref, dgl_ref), dtoks):
            ref[...] = val.astype(ref.dtype)
        dup_ref[...] += dup
        dbias_ref[...] += dbias
        dnw_ref[...] += dnw
        for h in range(GLA_HEADS):
            ds_scr[rows(h), :] = dstate[h]

    rev = lambda w: pl.BlockSpec((c, w), lambda i: (nc - 1 - i, 0))
    tok_widths = (GLA_KEY, GLA_KEY, GLA_VAL, LOW)
    return pl.pallas_call(
        body, name="gla_bwd", grid=(nc,),
        in_specs=toks_s + params_s + [pl.BlockSpec(GLA_STATE, lambda i: (nc - 1 - i, 0)), rev(GLA_VAL)],
        out_specs=[rev(w) for w in tok_widths] + params_s,
        out_shape=[jax.ShapeDtypeStruct((t, w), BF16) for w in tok_widths] + [jax.ShapeDtypeStruct(s, F32) for s in GLA_PARAM_SHAPES],
        scratch_shapes=[pltpu.VMEM(GLA_STATE, F32)], compiler_params=_cparams(("arbitrary",)))(
            proj0, proj0, proj0, proj0, gk_up, gk_bias, norm_w, states, do)


RWKV_PARAM_SHAPES = [(1, RWKV_W), (1, RWKV_W), (1, RWKV_W), (1, LOW), (1, LOW), (1, RWKV_W), (LOW, RWKV_W), (1, RWKV_W),
                     (LOW, RWKV_W), (1, RWKV_W), (1, RWKV_W), (1, RWKV_W), (1, RWKV_W), (1, RWKV_W)]
RWKV_STATE = (RWKV_HEADS * RWKV_N, RWKV_N)
RWKV_TOK_WIDTHS = (RWKV_W, RWKV_W, RWKV_W, LOW, LOW)
PREV_W = sum(RWKV_TOK_WIDTHS)
PREV_COLS = [slice(sum(RWKV_TOK_WIDTHS[:i]), sum(RWKV_TOK_WIDTHS[:i + 1])) for i in range(len(RWKV_TOK_WIDTHS))]


def _rwkv_load(r_ref, k_ref, v_ref, xw_ref, xa_ref, p_refs):
    toks = (r_ref[...], k_ref[...], v_ref[...], xw_ref[...], xa_ref[...])
    return toks, tuple(p[...] for p in p_refs)


def _rwkv_state(s_ref, prev_ref):
    n = RWKV_N
    S = [s_ref[h * n:(h + 1) * n, :] for h in range(RWKV_HEADS)]
    return (S,) + tuple(prev_ref[0:1, cols] for cols in PREV_COLS)


def _rwkv_put_state(s_ref, prev_ref, state):
    n = RWKV_N
    for h in range(RWKV_HEADS):
        s_ref[h * n:(h + 1) * n, :] = state[0][h]
    for cols, val in zip(PREV_COLS, state[1:]):
        prev_ref[0:1, cols] = val


def _rwkv_specs(c, n=None):
    toks = [_tok_spec(c, w, _col(w, name, C0), n) for name, w in zip(("r", "k", "v", "xw", "xa"), RWKV_TOK_WIDTHS)]
    return toks, [_full_spec(s) for s in RWKV_PARAM_SHAPES]


def _rwkv_fwd(proj0, params, comm, kinds):
    t = proj0.shape[0]
    c = RWKV_CHUNK
    nc = t // c
    toks_s, params_s = _rwkv_specs(c)
    npar, ncomm = len(params), len(comm)

    def body(*refs):
        tok_refs, p_refs = refs[:5], refs[5:5 + npar]
        comm_in = refs[5 + npar:5 + npar + ncomm]
        o_ref, st_ref, pst_ref = refs[5 + npar + ncomm:8 + npar + ncomm]
        comm_out = refs[8 + npar + ncomm:8 + npar + 2 * ncomm]
        s_scr, prev_scr = refs[8 + npar + 2 * ncomm:10 + npar + 2 * ncomm]
        sems = refs[10 + npar + 2 * ncomm:]
        i = pl.program_id(0)

        @pl.when(i == 0)
        def _():
            _comm_start(*_comm_copies(comm_in, comm_out, kinds, *sems))
            s_scr[...] = jnp.zeros_like(s_scr)
            prev_scr[...] = jnp.zeros_like(prev_scr)

        st_ref[...] = s_scr[...]
        pst_ref[...] = prev_scr[...]
        toks, prm = _rwkv_load(*tok_refs, p_refs)
        o_ref[...], new = rwkv_chunk(_rwkv_state(s_scr, prev_scr), toks, prm)
        _rwkv_put_state(s_scr, prev_scr, new)

        @pl.when(i == nc - 1)
        def _():
            _comm_wait(*_comm_copies(comm_in, comm_out, kinds, *sems))

    outs = pl.pallas_call(
        body, name="rwkv_fwd", grid=(nc,), in_specs=toks_s + params_s + [ANY] * ncomm,
        out_specs=[_tok_spec(c, RWKV_W, 0), pl.BlockSpec(RWKV_STATE, lambda i: (i, 0)), pl.BlockSpec((8, PREV_W), lambda i: (i, 0))]
        + [ANY] * ncomm,
        out_shape=[jax.ShapeDtypeStruct((t, RWKV_W), F32), jax.ShapeDtypeStruct((nc * RWKV_STATE[0], RWKV_N), F32),
                   jax.ShapeDtypeStruct((nc * 8, PREV_W), F32)] + _comm_out_shapes(comm, kinds),
        scratch_shapes=[pltpu.VMEM(RWKV_STATE, F32), pltpu.VMEM((8, PREV_W), F32)] + _comm_scratch(ncomm),
        compiler_params=_cparams(("arbitrary",)))(proj0, proj0, proj0, proj0, proj0, *params, *comm)
    return outs[0], outs[1], outs[2], outs[3:]


def _rwkv_bwd(proj0, params, states, prevs, do, comm, kinds):
    t = proj0.shape[0]
    c = RWKV_CHUNK
    nc = t // c
    toks_s, params_s = _rwkv_specs(c, nc)
    npar, ncomm = len(params), len(comm)

    def body(*refs):
        tok_refs, p_refs = refs[:5], refs[5:5 + npar]
        st_ref, pst_ref, do_ref = refs[5 + npar:8 + npar]
        comm_in = refs[8 + npar:8 + npar + ncomm]
        outs = refs[8 + npar + ncomm:]
        dtok_refs, dp_refs, comm_out = outs[:5], outs[5:5 + npar], outs[5 + npar:5 + npar + ncomm]
        ds_scr, dprev_scr = outs[5 + npar + ncomm:7 + npar + ncomm]
        sems = outs[7 + npar + ncomm:]
        i = pl.program_id(0)

        @pl.when(i == 0)
        def _():
            _comm_start(*_comm_copies(comm_in, comm_out, kinds, *sems))
            ds_scr[...] = jnp.zeros_like(ds_scr)
            dprev_scr[...] = jnp.zeros_like(dprev_scr)
            for dp in dp_refs:
                dp[...] = jnp.zeros_like(dp)

        toks, prm = _rwkv_load(*tok_refs, p_refs)
        _, vjp = jax.vjp(rwkv_chunk, _rwkv_state(st_ref, pst_ref), toks, prm)
        dstate, dtoks, dprm = vjp((do_ref[...], _rwkv_state(ds_scr, dprev_scr)))
        for ref, val in zip(dtok_refs, dtoks):
            ref[...] = val.astype(ref.dtype)
        for ref, val in zip(dp_refs, dprm):
            ref[...] += val
        _rwkv_put_state(ds_scr, dprev_scr, dstate)

        @pl.when(i == nc - 1)
        def _():
            _comm_wait(*_comm_copies(comm_in, comm_out, kinds, *sems))

    rev = lambda w: pl.BlockSpec((c, w), lambda i: (nc - 1 - i, 0))
    outs = pl.pallas_call(
        body, name="rwkv_bwd", grid=(nc,),
        in_specs=toks_s + params_s + [pl.BlockSpec(RWKV_STATE, lambda i: (nc - 1 - i, 0)),
                                      pl.BlockSpec((8, PREV_W), lambda i: (nc - 1 - i, 0)), rev(RWKV_W)] + [ANY] * ncomm,
        out_specs=[rev(w) for w in RWKV_TOK_WIDTHS] + params_s + [ANY] * ncomm,
        out_shape=[jax.ShapeDtypeStruct((t, w), BF16) for w in RWKV_TOK_WIDTHS]
        + [jax.ShapeDtypeStruct(s, F32) for s in RWKV_PARAM_SHAPES] + _comm_out_shapes(comm, kinds),
        scratch_shapes=[pltpu.VMEM(RWKV_STATE, F32), pltpu.VMEM((8, PREV_W), F32)] + _comm_scratch(ncomm),
        compiler_params=_cparams(("arbitrary",)))(proj0, proj0, proj0, proj0, proj0, *params, states, prevs, do, *comm)
    return outs[:5], outs[5:5 + npar], outs[5 + npar:]


def _swa_load(q_ref, k_ref, v_ref, cos_ref, sin_ref, bq_ref, bk_ref, bv_ref, sk_ref):
    toks = (q_ref[...], k_ref[...], v_ref[...], cos_ref[...], sin_ref[...])
    params = (bq_ref[...], bk_ref[...], bv_ref[...], _heads(sk_ref, SWA_Q_HEADS, 1))
    return toks, params


SWA_TOK_WIDTHS = (MIX, SWA_KV, SWA_KV)
SWA_PARAM_SHAPES = [(1, MIX), (1, SWA_KV), (1, SWA_KV), (1, SWA_Q_HEADS)]
SWA_STATE = (WINDOW, SWA_KV)


def _swa_specs(c, n=None):
    toks = [_tok_spec(c, w, _col(w, name, C1), n) for name, w in zip(("q", "k", "v"), SWA_TOK_WIDTHS)]
    toks += [_tok_spec(c, LANES, 0, n), _tok_spec(c, LANES, 0, n)]
    return toks, [_full_spec(s) for s in SWA_PARAM_SHAPES]


def _swa_fwd(proj1, cos, sin, bq, bk, bv, sinks):
    t = proj1.shape[0]
    c = SWA_STEP
    nb = t // c
    toks_s, params_s = _swa_specs(c)
    state_spec = pl.BlockSpec(SWA_STATE, lambda i: (i, 0))
    kv = SWA_KV_HEADS

    def body(q_ref, k_ref, v_ref, cos_ref, sin_ref, bq_ref, bk_ref, bv_ref, sk_ref, o_ref, kst_ref, vst_ref, k_scr, v_scr):
        first = pl.program_id(0) == 0

        @pl.when(first)
        def _():
            k_scr[...] = jnp.zeros_like(k_scr)
            v_scr[...] = jnp.zeros_like(v_scr)

        kst_ref[...] = k_scr[...]
        vst_ref[...] = v_scr[...]
        toks, params = _swa_load(q_ref, k_ref, v_ref, cos_ref, sin_ref, bq_ref, bk_ref, bv_ref, sk_ref)
        outs, (kn, vn) = swa_chunk((_heads(k_scr, kv, SWA_HD), _heads(v_scr, kv, SWA_HD)), toks, params, first)
        _put_heads(o_ref, outs, SWA_HD)
        _put_heads(k_scr, kn, SWA_HD)
        _put_heads(v_scr, vn, SWA_HD)

    saved = jax.ShapeDtypeStruct((nb * WINDOW, SWA_KV), F32)
    return pl.pallas_call(
        body, name="swa_fwd", grid=(nb,), in_specs=toks_s + params_s,
        out_specs=(_tok_spec(c, MIX, 0), state_spec, state_spec),
        out_shape=(jax.ShapeDtypeStruct((t, MIX), F32), saved, saved),
        scratch_shapes=[pltpu.VMEM(SWA_STATE, F32), pltpu.VMEM(SWA_STATE, F32)],
        compiler_params=_cparams(("arbitrary",)))(proj1, proj1, proj1, cos, sin, bq, bk, bv, sinks)


def _swa_bwd(proj1, cos, sin, bq, bk, bv, sinks, kst, vst, do):
    t = proj1.shape[0]
    c = SWA_STEP
    nb = t // c
    toks_s, params_s = _swa_specs(c, nb)
    state_spec = pl.BlockSpec(SWA_STATE, lambda i: (nb - 1 - i, 0))
    kv = SWA_KV_HEADS

    def body(q_ref, k_ref, v_ref, cos_ref, sin_ref, bq_ref, bk_ref, bv_ref, sk_ref, kst_ref, vst_ref, do_ref,
             dq_ref, dk_ref, dv_ref, dbq_ref, dbk_ref, dbv_ref, dsk_ref, dk_scr, dv_scr):
        i = pl.program_id(0)

        @pl.when(i == 0)
        def _():
            dk_scr[...] = jnp.zeros_like(dk_scr)
            dv_scr[...] = jnp.zeros_like(dv_scr)
            for ref in (dbq_ref, dbk_ref, dbv_ref, dsk_ref):
                ref[...] = jnp.zeros_like(ref)

        first = i == nb - 1
        toks, params = _swa_load(q_ref, k_ref, v_ref, cos_ref, sin_ref, bq_ref, bk_ref, bv_ref, sk_ref)
        f = functools.partial(swa_chunk, first=first)
        _, vjp = jax.vjp(f, (_heads(kst_ref, kv, SWA_HD), _heads(vst_ref, kv, SWA_HD)), toks, params)
        dstate_in = (_heads(dk_scr, kv, SWA_HD), _heads(dv_scr, kv, SWA_HD))
        (dkp, dvp), (dq, dk, dv, _, _), (dbq, dbk, dbv, dsk) = vjp((_heads(do_ref, SWA_Q_HEADS, SWA_HD), dstate_in))
        dq_ref[...], dk_ref[...], dv_ref[...] = dq.astype(BF16), dk.astype(BF16), dv.astype(BF16)
        dbq_ref[...] += dbq
        dbk_ref[...] += dbk
        dbv_ref[...] += dbv
        _put_heads(dsk_ref, dsk, 1, add=True)
        _put_heads(dk_scr, dkp, SWA_HD)
        _put_heads(dv_scr, dvp, SWA_HD)

    rev = lambda w: pl.BlockSpec((c, w), lambda i: (nb - 1 - i, 0))
    return pl.pallas_call(
        body, name="swa_bwd", grid=(nb,), in_specs=toks_s + params_s + [state_spec, state_spec, rev(MIX)],
        out_specs=[rev(w) for w in SWA_TOK_WIDTHS] + params_s,
        out_shape=[jax.ShapeDtypeStruct((t, w), BF16) for w in SWA_TOK_WIDTHS] + [jax.ShapeDtypeStruct(s, F32) for s in SWA_PARAM_SHAPES],
        scratch_shapes=[pltpu.VMEM(SWA_STATE, F32), pltpu.VMEM(SWA_STATE, F32)],
        compiler_params=_cparams(("arbitrary",)))(proj1, proj1, proj1, cos, sin, bq, bk, bv, sinks, kst, vst, do)


MESH = pl.DeviceIdType.MESH
ANY = pl.BlockSpec(memory_space=pl.ANY)


def _my_place():
    return lax.axis_index("x"), lax.axis_index("y"), lax.axis_index("c")


def _all_gather(shards):
    n = len(shards)

    def body(*refs):
        in_refs, out_refs = refs[:n], refs[n:2 * n]
        send_sems, recv_sems, local_sems = refs[2 * n:]
        x, y, c = _my_place()
        me, sibling = (x, y, c), (x, y, 1 - c)
        chips = [(1 - x, y), (x, 1 - y), (1 - x, 1 - y)]

        def slot(out_ref, place):
            px, py, pc = place
            return out_ref.at[4 * px + 2 * py + pc]

        def copy(a, k, block, to, src=None):
            return pltpu.make_async_remote_copy(
                src_ref=slot(out_refs[a], block) if src is None else src, dst_ref=slot(out_refs[a], block),
                send_sem=send_sems.at[a, k], recv_sem=recv_sems.at[a, k], device_id=to, device_id_type=MESH)

        mine = [pltpu.make_async_copy(in_refs[a], slot(out_refs[a], me), local_sems.at[a]) for a in range(n)]
        for cp in mine:
            cp.start()
        first = []
        for a in range(n):
            first.append(copy(a, 0, me, sibling, src=in_refs[a]))
            first += [copy(a, 1 + j, me, (*chip, c), src=in_refs[a]) for j, chip in enumerate(chips)]
        for cp in first:
            cp.start()
        passed = []
        for j, chip in enumerate(chips):
            for a in range(n):
                copy(a, 1 + j, (*chip, c), me).wait_recv()
                fwd = copy(a, 4 + j, (*chip, c), sibling)
                fwd.start()
                passed.append(fwd)
        for a in range(n):
            copy(a, 0, sibling, me).wait_recv()
            for j, chip in enumerate(chips):
                copy(a, 4 + j, (*chip, 1 - c), me).wait_recv()
        for cp in first + passed:
            cp.wait_send()
        for cp in mine:
            cp.wait()

    return pl.pallas_call(
        body, name="all_gather_weights", in_specs=[ANY] * n, out_specs=[ANY] * n,
        out_shape=[jax.ShapeDtypeStruct((N_DEV,) + s.shape, s.dtype) for s in shards],
        scratch_shapes=_comm_scratch(n))(*shards)


def _comm_copies(in_refs, out_refs, kinds, send_sems, recv_sems, local_sems):
    x, y, c = _my_place()
    my_idx = 4 * x + 2 * y + c
    src = lambda a, idx: in_refs[a] if kinds[a] == "gather" else in_refs[a].at[idx]
    local = [pltpu.make_async_copy(src(a, my_idx), out_refs[a].at[my_idx], local_sems.at[a]) for a in range(len(kinds))]
    remote = []
    for rel in range(1, N_DEV):
        px, py, pc = x ^ ((rel >> 2) & 1), y ^ ((rel >> 1) & 1), c ^ (rel & 1)
        for a in range(len(kinds)):
            remote.append(pltpu.make_async_remote_copy(
                src_ref=src(a, 4 * px + 2 * py + pc), dst_ref=out_refs[a].at[my_idx], send_sem=send_sems.at[a, rel - 1],
                recv_sem=recv_sems.at[a, rel - 1], device_id=(px, py, pc), device_id_type=MESH))
    return local, remote


def _comm_start(local, remote):
    for cp in local + remote:
        cp.start()


def _comm_wait(local, remote):
    for cp in remote:
        cp.wait_recv()
    for cp in remote:
        cp.wait_send()
    for cp in local:
        cp.wait()


def _comm_out_shapes(arrays, kinds):
    return [jax.ShapeDtypeStruct(((N_DEV,) + a.shape) if k == "gather" else a.shape, a.dtype) for a, k in zip(arrays, kinds)]


def _comm_scratch(n):
    return [pltpu.SemaphoreType.DMA((n, N_DEV - 1)), pltpu.SemaphoreType.DMA((n, N_DEV - 1)), pltpu.SemaphoreType.DMA((n,))]


def _sequencer_exchange(name, arrays, kinds, collective_id):
    n = len(arrays)
    srcs = [jax.new_ref(a, memory_space=pltpu.MemorySpace.HBM) for a in arrays]
    dsts = [jax.empty_ref(s, memory_space=pltpu.MemorySpace.HBM) for s in _comm_out_shapes(arrays, kinds)]

    @pl.kernel(mesh=plsc.ScalarSubcoreMesh(axis_name="sequencer", num_cores=1), name=name,
               scratch_types=(pltpu.SemaphoreType.DMA((n, N_DEV - 1)), pltpu.SemaphoreType.DMA((n, N_DEV - 1))),
               compiler_params=pltpu.CompilerParams(collective_id=collective_id))
    def launch(send_sems, recv_sems):
        x, y, c = _my_place()
        my_idx = 4 * x + 2 * y + c
        peers = [(x ^ ((rel >> 2) & 1), y ^ ((rel >> 1) & 1), c ^ (rel & 1)) for rel in range(1, N_DEV)]
        barrier = pltpu.get_barrier_semaphore()
        for peer in peers:
            pl.semaphore_signal(barrier, inc=1, device_id=peer, device_id_type=MESH)
        pl.semaphore_wait(barrier, N_DEV - 1)
        copies = [pltpu.make_async_remote_copy(
            src_ref=srcs[a] if kinds[a] == "gather" else srcs[a].at[4 * px + 2 * py + pc], dst_ref=dsts[a].at[my_idx],
            send_sem=send_sems.at[a, k], recv_sem=recv_sems.at[a, k], device_id=(px, py, pc), device_id_type=MESH)
            for k, (px, py, pc) in enumerate(peers) for a in range(n)]
        for cp in copies:
            cp.start()
        for cp in copies:
            cp.wait_recv()
        for cp in copies:
            cp.wait_send()

    launch()
    return [dst[...] for dst in dsts]


def _own_block(received, mine, kind):
    my_idx = 4 * lax.axis_index("x") + 2 * lax.axis_index("y") + lax.axis_index("c")
    block = mine[None] if kind == "gather" else lax.dynamic_slice(mine, (my_idx,) + (0,) * (mine.ndim - 1), (1,) + mine.shape[1:])
    return lax.dynamic_update_slice(received, block, (my_idx,) + (0,) * (received.ndim - 1))


def _exchange(arrays, kinds):
    n = len(arrays)

    def body(*refs):
        copies = _comm_copies(refs[:n], refs[n:2 * n], kinds, *refs[2 * n:])
        _comm_start(*copies)
        _comm_wait(*copies)

    return pl.pallas_call(body, name="exchange_grads", in_specs=[ANY] * n, out_specs=[ANY] * n,
                          out_shape=_comm_out_shapes(arrays, kinds), scratch_shapes=_comm_scratch(n))(*arrays)


def _adam_math(w, g, m, v):
    m = ADAM_B1 * m + (1.0 - ADAM_B1) * g
    v = ADAM_B2 * v + (1.0 - ADAM_B2) * (g * g)
    m_hat = m / (1.0 - ADAM_B1 ** ADAM_STEP)
    v_hat = v / (1.0 - ADAM_B2 ** ADAM_STEP)
    delta = -ADAM_LR * (m_hat / (jnp.sqrt(v_hat) + ADAM_EPS) + ADAM_WD * w)
    return delta, m, v


def _adamw(name, w, gslots, m, v, tc):
    r, cc = w.shape
    assert cc % tc == 0
    tile = pl.BlockSpec((r, tc), lambda i: (0, i))

    def body(w_ref, g_ref, m_ref, v_ref, go_ref, d_ref, mo_ref, vo_ref):
        g = g_ref[0].astype(F32)
        for s in range(1, N_DEV):
            g = g + g_ref[s].astype(F32)
        d, mn, vn = _adam_math(w_ref[...], g, m_ref[...], v_ref[...])
        go_ref[...] = g
        d_ref[...] = d
        mo_ref[...] = mn
        vo_ref[...] = vn

    shp = jax.ShapeDtypeStruct((r, cc), F32)
    return pl.pallas_call(body, name=name, grid=(cc // tc,),
                          in_specs=[tile, pl.BlockSpec((N_DEV, r, tc), lambda i: (0, 0, i)), tile, tile],
                          out_specs=(tile,) * 4, out_shape=(shp,) * 4, compiler_params=_cparams(("arbitrary",)))(w, gslots, m, v)


PACK_TILE = 8 * LANES


def _packed_rows(shape, mode):
    r, w = shape
    return -(-r // 8) * 8 if mode == "rows" else -(-(r * w) // PACK_TILE) * 8


def _pack_small(arrays, modes, lead=False):
    out = []
    for a, mode in zip(arrays, modes):
        a = a.astype(F32) if lead else a.astype(F32)[None]
        if mode == "rows":
            out.append(jnp.pad(a, ((0, 0), (0, (-a.shape[1]) % 8), (0, LANES - a.shape[2]))))
        else:
            flat = a.reshape(a.shape[0], -1)
            out.append(jnp.pad(flat, ((0, 0), (0, (-flat.shape[1]) % PACK_TILE))).reshape(a.shape[0], -1, LANES))
    out = jnp.concatenate(out, axis=1)
    return out if lead else out[0]


def _take_small(packed, row0, shape, mode):
    r, w = shape
    lead = packed.ndim == 3
    if mode == "rows":
        return packed[:, row0:row0 + r, :w] if lead else packed[row0:row0 + r, :w]
    per_row = -(-w // LANES)
    if lead:
        return packed[:, row0:row0 + r * per_row].reshape(packed.shape[0], r, per_row * LANES)[:, :, :w]
    rows = []
    for i in range(r):
        pieces = [packed[row0 + i * per_row + j:row0 + i * per_row + j + 1, :] for j in range(per_row)]
        rows.append((pieces[0] if per_row == 1 else jnp.concatenate(pieces, axis=1))[:, :w])
    return rows[0] if r == 1 else jnp.concatenate(rows, axis=0)


def _adamw_small(slots, specs, ws, ms, vs, loss_row):
    n = len(specs)

    def body(*refs):
        slots_ref, w_refs, m_refs, v_refs = refs[0], refs[1:1 + n], refs[1 + n:1 + 2 * n], refs[1 + 2 * n:1 + 3 * n]
        out_refs, loss_ref = refs[1 + 3 * n:1 + 7 * n], refs[1 + 7 * n]
        gp = slots_ref[0]
        for s in range(1, N_DEV):
            gp = gp + slots_ref[s]
        read = lambda ref: ref[0] if len(ref.shape) == 3 else ref[...]
        for k, (shape, mode, row0) in enumerate(specs):
            g = _take_small(gp, row0, shape, mode)
            d, mn, vn = _adam_math(read(w_refs[k]), g, read(m_refs[k]), read(v_refs[k]))
            for ref, val in zip(out_refs[4 * k:4 * k + 4], (g, d, mn, vn)):
                if len(ref.shape) == 3:
                    ref[0] = val
                else:
                    ref[...] = val
        loss_ref[...] = gp[loss_row:loss_row + 1, :]

    vmem = pl.BlockSpec(memory_space=pltpu.VMEM)
    out_shape = [jax.ShapeDtypeStruct(w.shape, F32) for w in ws for _ in range(4)] + [jax.ShapeDtypeStruct((1, LANES), F32)]
    outs = pl.pallas_call(body, name="adamw_small", in_specs=[vmem] * (1 + 3 * n), out_specs=[vmem] * (4 * n + 1),
                          out_shape=out_shape)(slots, *ws, *ms, *vs)
    return [outs[4 * k:4 * k + 4] for k in range(n)], outs[4 * n]


def _rope_tables(t):
    dim = jnp.arange(LANES) % SWA_HD
    inv_freq = ROPE_THETA ** (-(dim % ROPE_HALF).astype(F32) / ROPE_HALF)
    ang = jnp.arange(t, dtype=F32)[:, None] * jnp.where(dim < 2 * ROPE_HALF, inv_freq, 0.0)[None, :]
    return jnp.cos(ang), jnp.sin(ang)


def _pad_to(a, rows=None, cols=None):
    r = 0 if rows is None else rows - a.shape[0]
    c = 0 if cols is None else cols - a.shape[1]
    return jnp.pad(a, ((0, r), (0, c)))


ORIG0 = dict(gq=(0, 256), gk=(256, 256), gv=(512, 512), glow=(1024, 16), r=(1040, 512), k=(1552, 512), v=(2064, 512),
             xw=(2576, 64), xa=(2640, 64), gate=(2704, 1024))
ORIG0_ORDER = ["gq", "gk", "gv", "glow", "r", "k", "v", "xw", "xa", "gate"]


def _w0t_to_padded(wt):
    rows, at = [], 0
    for name, (off, width) in sorted(C0.items(), key=lambda kv: kv[1][0]):
        assert off == at
        src, src_w = ORIG0[name]
        rows.append(_pad_to(wt[src:src + src_w], rows=width))
        at += width
    rows.append(jnp.zeros((N0P - at, wt.shape[1]), wt.dtype))
    return jnp.concatenate(rows, axis=0)


def _w1t_to_mine(wt):
    return jnp.concatenate([wt[1536:2560], wt[:1536]], axis=0)


def _w1t_from_mine(wt):
    return jnp.concatenate([wt[1024:2560], wt[:1024]], axis=0)


def kernel(x, norm_w, w_in0, gla_gk_up, gla_gk_bias, gla_norm_w, rwkv_mu, rwkv_w0, rwkv_w_up, rwkv_a0, rwkv_a_up, rwkv_k_k, rwkv_k_a, rwkv_r_k, rwkv_ln_w, rwkv_ln_b, w_out0, w_in1, b_in1, attn_sinks, w_out1, b_out1, final_norm_w, loss_target, m_norm_w, m_w_in0, m_gla_gk_up, m_gla_gk_bias, m_gla_norm_w, m_rwkv_mu, m_rwkv_w0, m_rwkv_w_up, m_rwkv_a0, m_rwkv_a_up, m_rwkv_k_k, m_rwkv_k_a, m_rwkv_r_k, m_rwkv_ln_w, m_rwkv_ln_b, m_w_out0, m_w_in1, m_b_in1, m_attn_sinks, m_w_out1, m_b_out1, m_final_norm_w, v_norm_w, v_w_in0, v_gla_gk_up, v_gla_gk_bias, v_gla_norm_w, v_rwkv_mu, v_rwkv_w0, v_rwkv_w_up, v_rwkv_a0, v_rwkv_a_up, v_rwkv_k_k, v_rwkv_k_a, v_rwkv_r_k, v_rwkv_ln_w, v_rwkv_ln_b, v_w_out0, v_w_in1, v_b_in1, v_attn_sinks, v_w_out1, v_b_out1, v_final_norm_w):
    weights = dict(norm_w=norm_w, w_in0=w_in0, gla_gk_up=gla_gk_up, gla_gk_bias=gla_gk_bias, gla_norm_w=gla_norm_w, rwkv_mu=rwkv_mu,
                   rwkv_w0=rwkv_w0, rwkv_w_up=rwkv_w_up, rwkv_a0=rwkv_a0, rwkv_a_up=rwkv_a_up, rwkv_k_k=rwkv_k_k, rwkv_k_a=rwkv_k_a,
                   rwkv_r_k=rwkv_r_k, rwkv_ln_w=rwkv_ln_w, rwkv_ln_b=rwkv_ln_b, w_out0=w_out0, w_in1=w_in1, b_in1=b_in1,
                   attn_sinks=attn_sinks, w_out1=w_out1, b_out1=b_out1, final_norm_w=final_norm_w)
    moms = dict(norm_w=m_norm_w, w_in0=m_w_in0, gla_gk_up=m_gla_gk_up, gla_gk_bias=m_gla_gk_bias, gla_norm_w=m_gla_norm_w,
                rwkv_mu=m_rwkv_mu, rwkv_w0=m_rwkv_w0, rwkv_w_up=m_rwkv_w_up, rwkv_a0=m_rwkv_a0, rwkv_a_up=m_rwkv_a_up,
                rwkv_k_k=m_rwkv_k_k, rwkv_k_a=m_rwkv_k_a, rwkv_r_k=m_rwkv_r_k, rwkv_ln_w=m_rwkv_ln_w, rwkv_ln_b=m_rwkv_ln_b,
                w_out0=m_w_out0, w_in1=m_w_in1, b_in1=m_b_in1, attn_sinks=m_attn_sinks, w_out1=m_w_out1, b_out1=m_b_out1,
                final_norm_w=m_final_norm_w)
    vars_ = dict(norm_w=v_norm_w, w_in0=v_w_in0, gla_gk_up=v_gla_gk_up, gla_gk_bias=v_gla_gk_bias, gla_norm_w=v_gla_norm_w,
                 rwkv_mu=v_rwkv_mu, rwkv_w0=v_rwkv_w0, rwkv_w_up=v_rwkv_w_up, rwkv_a0=v_rwkv_a0, rwkv_a_up=v_rwkv_a_up,
                 rwkv_k_k=v_rwkv_k_k, rwkv_k_a=v_rwkv_k_a, rwkv_r_k=v_rwkv_r_k, rwkv_ln_w=v_rwkv_ln_w, rwkv_ln_b=v_rwkv_ln_b,
                 w_out0=v_w_out0, w_in1=v_w_in1, b_in1=v_b_in1, attn_sinks=v_attn_sinks, w_out1=v_w_out1, b_out1=v_b_out1,
                 final_norm_w=v_final_norm_w)
    names = list(weights)
    big = ["w_in0", "w_out0", "w_in1", "w_out1"]
    small_sharded = ["gla_gk_up", "rwkv_w_up", "rwkv_a_up", "b_in1", "b_out1"]
    replicated = [n for n in names if n not in big and n not in small_sharded]

    xs = x[0]
    tgt = loss_target[0]
    t = xs.shape[0]

    def view(w):
        shape = tuple(w.shape[-2:]) if w.ndim >= 2 else (1, w.shape[0])
        return shape, ("rows" if shape[0] > 1 and shape[1] <= LANES else "flat")

    def layout(ns, row0=0):
        specs = []
        for n in ns:
            shape, mode = view(weights[n])
            specs.append((shape, mode, row0))
            row0 += _packed_rows(shape, mode)
        return specs, row0

    late = ["norm_w"]
    replicated = [n for n in replicated if n not in late]
    sh_specs, n_shard_rows = layout(small_sharded)
    rep_specs, loss_row = layout(replicated, n_shard_rows)
    late_specs, _ = layout(late, loss_row + _packed_rows((1, 1), "flat"))
    sh_modes, rep_modes, late_modes = [s[1] for s in sh_specs], [s[1] for s in rep_specs], [s[1] for s in late_specs]

    small_shard_pack = _pack_small([weights[n].reshape(view(weights[n])[0]) for n in small_sharded], sh_modes)
    g_in0, g_small = _all_gather([w_in0[0].T.astype(BF16), small_shard_pack])
    w0t = _w0t_to_padded(g_in0.reshape(-1, D_MODEL))
    later_shards = [w_out0[0].astype(BF16), w_in1[0].T.astype(BF16), w_out1[0].astype(BF16)]
    gs = [_take_small(g_small, row0, shape, mode) for shape, mode, row0 in sh_specs]
    join_cols = lambda a: jnp.transpose(a, (1, 0, 2)).reshape(a.shape[1], -1)
    gk_up, w_up, a_up = join_cols(gs[0]), join_cols(gs[1]), join_cols(gs[2])
    b_in, b_out = gs[3].reshape(1, -1), gs[4].reshape(1, -1)

    gk_up_p = _pad_to(gk_up, rows=LOW)
    w3, rank = 3 * RWKV_W, rwkv_w_up.shape[1]
    mu = rwkv_mu
    rwkv_params = [mu[:, 0:RWKV_W], mu[:, RWKV_W:2 * RWKV_W], mu[:, 2 * RWKV_W:w3], _pad_to(mu[:, w3:w3 + rank], cols=LOW),
                   _pad_to(mu[:, w3 + rank:], cols=LOW), rwkv_w0, _pad_to(w_up, rows=LOW), rwkv_a0, _pad_to(a_up, rows=LOW),
                   rwkv_k_k, rwkv_k_a, rwkv_r_k.reshape(1, RWKV_W), rwkv_ln_w, rwkv_ln_b]
    bq, bk, bv = b_in[:, :MIX], b_in[:, MIX:MIX + SWA_KV], b_in[:, MIX + SWA_KV:]
    cos, sin = _rope_tables(t)
    nw0, nw1, fw = norm_w[0:1], norm_w[1:2], final_norm_w.reshape(1, D_MODEL)

    d = D_MODEL
    wide = lambda arr: (arr, d, 0)
    silu = lambda g: g * sigmoid(g)
    hn0, proj0 = _matmul_fused("norm0_proj0", rms, w0t, "nt", [wide(xs)], [nw0], [(N0P, F32)], [], lambda acc, x, w: (acc,))
    o_a, gla_states = _gla_fwd(proj0, gk_up_p, gla_gk_bias, gla_norm_w)
    o_b, rwkv_states, rwkv_prevs, (g_out0, g_in1, g_out1) = _rwkv_fwd(proj0, rwkv_params, later_shards, ["gather"] * 3)
    wo0 = g_out0.reshape(MIX, D_MODEL)
    w1t = _w1t_to_mine(g_in1.reshape(-1, D_MODEL))
    wo1 = g_out1.reshape(MIX, D_MODEL)
    og0, h1, hn1 = _matmul_fused(
        "gate0_out0_norm1", lambda oa, ob, gate, x, w: jnp.concatenate([oa, ob], axis=1) * silu(gate), wo0, "nn",
        [(o_a, GLA_VAL, 0), (o_b, RWKV_W, 0), wide(proj0), wide(xs)], [nw1], [(d, F32), (d, BF16)], [],
        lambda acc, oa, ob, gate, x, w: _resid_norm(acc, x, w))
    proj1 = _matmul("proj1", hn1, w1t, "nt", PROJ_ROWS, N1P // 2)
    o_c, kst, vst = _swa_fwd(proj1, cos, sin, bq, bk, bv, attn_sinks)
    og1, dh2, loss_part, d_b_out, d_fw = _matmul_fused(
        "gate1_out1_loss", lambda oc, gate, h, tg, b, w: oc * silu(gate), wo1, "nn",
        [wide(o_c), wide(proj1), wide(h1), wide(tgt)], [b_out, fw], [(d, F32)], [LANES, d, d],
        lambda acc, oc, gate, h, tg, b, w: _loss_head(acc, h, tg, b, w))

    d_oc, d_gate1 = _matmul_fused("out1_dx_gate1", dh2, wo1, "nt", [wide(o_c), wide(proj1)], [], [(d, F32), (d, BF16)], [], _gate_back)
    d_wo1 = _matmul("out1_dw", og1, dh2, "tn", DW_COLS, DW_COLS, BF16)
    dq, dk, dv, d_bq, d_bk, d_bv, d_sinks = _swa_bwd(proj1, cos, sin, bq, bk, bv, attn_sinks, kst, vst, d_oc)
    dproj1 = jnp.concatenate([d_gate1, dq, dk, dv], axis=1)
    dh1, d_nw1 = _matmul_fused("proj1_dx_norm1", dproj1, w1t, "nn", [wide(h1), wide(dh2)], [nw1], [(d, F32)], [d], _norm_back)
    d_w1t = _matmul("proj1_dw", dproj1, hn1, "tn", DW_COLS, d, BF16)
    d_oa, d_ob, d_gate0 = _matmul_fused("out0_dx_gate0", dh1, wo0, "nt", [(o_a, GLA_VAL, 0), (o_b, RWKV_W, 0), wide(proj0)], [],
                                        [(GLA_VAL, F32), (RWKV_W, F32), (d, BF16)], [], _gate_back)
    d_wo0 = _matmul("out0_dw", og0, dh1, "tn", DW_COLS, DW_COLS, BF16)
    dgq, dgk, dgv, dglow, d_gk_up, d_gk_bias, d_gla_nw = _gla_bwd(proj0, gk_up_p, gla_gk_bias, gla_norm_w, gla_states, d_oa)
    row_blocks = lambda a: a.astype(BF16).reshape(N_DEV, -1, D_MODEL)
    early = [row_blocks(_w1t_from_mine(d_w1t)), row_blocks(d_wo1), row_blocks(d_wo0)]
    def dw_rows(name, pieces):
        a = jnp.concatenate([p for _, p in pieces], axis=1)
        a = _pad_to(a, cols=-(-a.shape[1] // DW_COLS) * DW_COLS)
        rows, at = _matmul(name, a, hn0, "tn", DW_COLS, d, BF16), 0
        out = {}
        for n, p in pieces:
            out[n] = rows[at:at + ORIG0[n][1]]
            at += p.shape[1]
        return out

    dw0 = dw_rows("proj0_dw_early", [("gate", d_gate0), ("gv", dgv), ("gq", dgq), ("gk", dgk), ("glow", dglow)])
    (dr, dkk, dvv, dxw, dxa), d_rp, (r_in1, r_out1, r_out0) = _rwkv_bwd(
        proj0, rwkv_params, rwkv_states, rwkv_prevs, d_ob, early, ["scatter"] * 3)
    dw0.update(dw_rows("proj0_dw_late", [("r", dr), ("k", dkk), ("v", dvv), ("xw", dxw), ("xa", dxa)]))
    d_w0 = row_blocks(jnp.concatenate([dw0[n] for n in ORIG0_ORDER], axis=0))
    pad = jnp.zeros((t, N0P - C0["xa"][0] - C0["xa"][1]), BF16)
    dproj0 = jnp.concatenate([d_gate0, dgv, dr, dkk, dvv, dgq, dgk, dglow, dxw, dxa, pad], axis=1)

    contrib = dict(
        gla_gk_bias=d_gk_bias, gla_norm_w=d_gla_nw,
        rwkv_mu=jnp.concatenate([d_rp[0], d_rp[1], d_rp[2], d_rp[3][:, :rank], d_rp[4][:, :rank]], axis=1),
        rwkv_w0=d_rp[5], rwkv_a0=d_rp[7], rwkv_k_k=d_rp[9], rwkv_k_a=d_rp[10], rwkv_r_k=d_rp[11].reshape(RWKV_HEADS, RWKV_N),
        rwkv_ln_w=d_rp[12], rwkv_ln_b=d_rp[13], attn_sinks=d_sinks, final_norm_w=d_fw)
    rep_pack = _pack_small([contrib[n] for n in replicated] + [loss_part[:, :1]], rep_modes + ["flat"])
    d_b_in = jnp.concatenate([d_bq, d_bk, d_bv], axis=1)
    full_small = [d_gk_up[:gk_up.shape[0]], d_rp[6][:rank], d_rp[8][:rank], d_b_in, d_b_out]
    split_cols = lambda a: jnp.transpose(a.reshape(a.shape[0], N_DEV, -1), (1, 0, 2))
    small_pack = _pack_small([split_cols(a) for a in full_small], sh_modes, lead=True)

    sent, sent_kinds = [d_w0, small_pack, rep_pack], ["scatter", "scatter", "gather"]
    received = _sequencer_exchange("exchange_last_grads", sent, sent_kinds, 0)
    grad_x, d_nw0 = _matmul_fused("proj0_dx_norm0", dproj0, w0t, "nn", [wide(xs), wide(dh1)], [nw0], [(d, F32)], [d], _norm_back)

    res = {}
    res["w_out0"] = tuple(a[None] for a in _adamw("adamw_w_out0", w_out0[0], r_out0, m_w_out0[0], v_w_out0[0], ADAM_COLS))
    res["w_in1"] = tuple(a.T[None] for a in _adamw("adamw_w_in1", w_in1[0].T, r_in1, m_w_in1[0].T, v_w_in1[0].T, ADAM_COLS))
    res["w_out1"] = tuple(a[None] for a in _adamw("adamw_w_out1", w_out1[0], r_out1, m_w_out1[0], v_w_out1[0], ADAM_COLS))
    late_pack = _pack_small([jnp.concatenate([d_nw0, d_nw1], axis=0)], late_modes)
    r_late, = _exchange([late_pack], ["gather"])
    r_in0, r_small, r_rep = [_own_block(r, mine, kind) for r, mine, kind in zip(received, sent, sent_kinds)]
    small_names = small_sharded + replicated + late
    slots = jnp.concatenate([r_small, r_rep, r_late], axis=1)
    as_2d = lambda a: a.reshape(1, -1) if a.ndim == 1 else a
    small_res, loss_row_out = _adamw_small(slots, sh_specs + rep_specs + late_specs, [as_2d(weights[n]) for n in small_names],
                                           [as_2d(moms[n]) for n in small_names], [as_2d(vars_[n]) for n in small_names], loss_row)
    for n, vals in zip(small_names, small_res):
        res[n] = tuple(val.reshape(weights[n].shape) for val in vals)
    loss = loss_row_out[0, 0]
    res["w_in0"] = tuple(a.T[None] for a in _adamw("adamw_w_in0", w_in0[0].T, r_in0, m_w_in0[0].T, v_w_in0[0].T, ADAM_COLS))
    return (loss, grad_x[None], *[res[n][0] for n in names], *[res[n][1] for n in names],
            *[res[n][2] for n in names], *[res[n][3] for n in names])
```

```python
import functools

import jax
import jax.numpy as jnp
from jax import lax
from jax.experimental import pallas as pl
from jax.experimental.pallas import tpu as pltpu
from jax.experimental.pallas import tpu_sc as plsc

F32 = jnp.float32
BF16 = jnp.bfloat16
HI = lax.Precision.HIGHEST

D_MODEL = 1024
NORM_EPS = 1e-5
GLA_HEADS, GLA_DK, GLA_DV = 4, 64, 128
GLA_NORMALIZER = 16.0
GLA_CHUNK = 64
GLA_STEP = 1024
RWKV_HEADS, RWKV_N = 8, 64
RWKV_LN_EPS = 64e-5
RWKV_CHUNK = 128
SWA_Q_HEADS, SWA_KV_HEADS, SWA_GROUP, SWA_HD = 16, 4, 4, 64
WINDOW = 128
SWA_STEP = 512
ROPE_THETA = 500000.0
NEG = -1e30
N_DEV = 8
LANES = 128

ADAM_LR, ADAM_B1, ADAM_B2, ADAM_EPS, ADAM_WD, ADAM_STEP = 0.001, 0.9, 0.999, 1e-08, 0.01, 10

GLA_KEY, GLA_VAL = GLA_HEADS * GLA_DK, GLA_HEADS * GLA_DV
RWKV_W = RWKV_HEADS * RWKV_N
SWA_KV = SWA_KV_HEADS * SWA_HD
MIX = GLA_VAL + RWKV_W
LOW = LANES

N0P = 4096
C0 = dict(gate=(0, MIX), gv=(1024, GLA_VAL), r=(1536, RWKV_W), k=(2048, RWKV_W), v=(2560, RWKV_W), gq=(3072, GLA_KEY),
          gk=(3328, GLA_KEY), glow=(3584, LOW), xw=(3712, LOW), xa=(3840, LOW))
N1P = 2560
C1 = dict(gate=(0, MIX), q=(1024, MIX), k=(2048, SWA_KV), v=(2304, SWA_KV))

VMEM_LIMIT = 56 * 1024 * 1024

P_LORA = 1
P_GLA = 1
P_RWKV_G = 2
P_RWKV = 1
P_SWA = 1


def _cparams(sem=None):
    return pltpu.CompilerParams(dimension_semantics=sem, vmem_limit_bytes=VMEM_LIMIT)


DIMS = dict(nn=(((1,), (0,)), ((), ())), nt=(((1,), (1,)), ((), ())), tn=(((0,), (0,)), ((), ())))


def _split_bf16(a):
    hi = a.astype(BF16)
    return hi, (a - hi.astype(F32)).astype(BF16)


def _dot(a, b, mode, passes):
    dg = lambda p, q: lax.dot_general(p, q, DIMS[mode], preferred_element_type=F32)
    if passes == 1:
        return dg(a.astype(BF16), b.astype(BF16))
    if passes == 2:
        ah, (bh, bl) = a.astype(BF16), _split_bf16(b)
        return dg(ah, bh) + dg(ah, bl)
    if passes == 3:
        (ah, al), (bh, bl) = _split_bf16(a), _split_bf16(b)
        return dg(ah, bh) + dg(al, bh) + dg(ah, bl)
    return lax.dot_general(a, b, DIMS[mode], precision=HI, preferred_element_type=F32)


@functools.partial(jax.custom_vjp, nondiff_argnums=(2, 3))
def mmx(a, b, mode, passes):
    return _dot(a, b, mode, passes)


def _mmx_fwd(a, b, mode, passes):
    return _dot(a, b, mode, passes), (a, b)


def _mmx_bwd(mode, passes, res, g):
    a, b = res
    if mode == "nn":
        return _dot(g, b, "nt", passes), _dot(a, g, "tn", passes)
    if mode == "nt":
        return _dot(g, b, "nn", passes), _dot(g, a, "tn", passes)
    return _dot(b, g, "nt", passes), _dot(a, g, "nn", passes)


mmx.defvjp(_mmx_fwd, _mmx_bwd)


def _tri_dot(tri, x):
    t = tri.astype(BF16)
    x1 = x.astype(BF16)
    r1 = x - x1.astype(F32)
    x2 = r1.astype(BF16)
    x3 = (r1 - x2.astype(F32)).astype(BF16)
    dg = lambda q: jnp.dot(t, q, preferred_element_type=F32)
    return dg(x1) + dg(x2) + dg(x3)


@jax.custom_vjp
def cumsum_rows(x):
    return _tri_dot(tril_ones(x.shape[0]), x)


def _cumsum_fwd(x):
    return cumsum_rows(x), None


def _cumsum_bwd(_, g):
    i, j = _iota2(g.shape[0], g.shape[0])
    return (_tri_dot(jnp.where(i <= j, 1.0, 0.0).astype(F32), g),)


cumsum_rows.defvjp(_cumsum_fwd, _cumsum_bwd)


def _head_dot(x):
    i, j = _iota2(LANES, LANES)
    shift = RWKV_N.bit_length() - 1
    same = jnp.where(jnp.right_shift(i, shift) == jnp.right_shift(j, shift), 1.0, 0.0).astype(F32)
    return jnp.concatenate([_ones_right(x[:, g * LANES:(g + 1) * LANES], same) for g in range(x.shape[1] // LANES)], axis=1)


def _ones_right(x, ones):
    t = ones.astype(BF16)
    x1 = x.astype(BF16)
    x2 = (x - x1.astype(F32)).astype(BF16)
    dg = lambda q: jnp.dot(q, t, preferred_element_type=F32)
    return dg(x1) + dg(x2)


@jax.custom_vjp
def head_sum(x):
    return _head_dot(x)


def _head_sum_fwd(x):
    return head_sum(x), None


def _head_sum_bwd(_, g):
    return (_head_dot(g),)


head_sum.defvjp(_head_sum_fwd, _head_sum_bwd)


def cat_rows(*xs):
    return jnp.concatenate(xs, axis=0)


def _iota2(n, m):
    return lax.broadcasted_iota(jnp.int32, (n, m), 0), lax.broadcasted_iota(jnp.int32, (n, m), 1)


def tril_ones(c, strict=False):
    i, j = _iota2(c, c)
    return jnp.where((i > j) if strict else (i >= j), 1.0, 0.0).astype(F32)


def row_of(x, r):
    i = lax.broadcasted_iota(jnp.int32, x.shape, 0)
    return jnp.sum(jnp.where(i == r, x, 0.0), axis=0, keepdims=True)


@jax.custom_vjp
def shift_rows(x, prev):
    r = lax.broadcasted_iota(jnp.int32, x.shape, 0)
    return jnp.where(r == 0, prev, pltpu.roll(x, 1, 0))


def _shift_fwd(x, prev):
    return shift_rows(x, prev), None


def _shift_bwd(_, g):
    c = g.shape[0]
    r = lax.broadcasted_iota(jnp.int32, g.shape, 0)
    return jnp.where(r == c - 1, 0.0, pltpu.roll(g, c - 1, 0)), row_of(g, 0)


shift_rows.defvjp(_shift_fwd, _shift_bwd)


def log_sigmoid(x):
    return jnp.minimum(x, 0.0) - jnp.log(1.0 + jnp.exp(-jnp.abs(x)))


def softplus(x):
    return jnp.maximum(x, 0.0) + jnp.log(1.0 + jnp.exp(-jnp.abs(x)))


def sigmoid(x):
    return 1.0 / (1.0 + jnp.exp(-x))


def rms(x, w, eps=NORM_EPS):
    return x * lax.rsqrt(jnp.mean(x * x, axis=-1, keepdims=True) + eps) * w


def gla_chunk(state, toks, params):
    q, k, v, glow = toks
    gk_up, bias, norm_w = params
    c = GLA_CHUNK
    subs, heads = range(glow.shape[0] // c), range(GLA_HEADS)
    rows = lambda x, j: x[j * c:(j + 1) * c]
    hk = lambda x, h: x[:, h * GLA_DK:(h + 1) * GLA_DK]
    hv = lambda x, h: x[:, h * GLA_DV:(h + 1) * GLA_DV]
    ltri = tril_ones(c)
    g = log_sigmoid(mmx(glow, gk_up, "nn", P_LORA) + bias) / GLA_NORMALIZER
    b = [cumsum_rows(rows(g, j)) for j in subs]
    ref = [lax.stop_gradient(row_of(b[j], c // 2)) for j in subs]
    last = [row_of(b[j], c - 1) for j in subs]
    ql = [rows(q, j) * (GLA_DK ** -0.5) * jnp.exp(b[j] - ref[j]) for j in subs]
    kr = [rows(k, j) * jnp.exp(ref[j] - b[j]) for j in subs]
    kl = [rows(k, j) * jnp.exp(last[j] - b[j]) for j in subs]
    vj = [rows(v, j) for j in subs]
    e_ref, e_last = [jnp.exp(x) for x in ref], [jnp.exp(x) for x in last]
    att = [[mmx(hk(ql[j], h), hk(kr[j], h), "nt", P_GLA) * ltri for h in heads] for j in subs]
    o_in = [[mmx(att[j][h], hv(vj[j], h), "nn", P_GLA) for h in heads] for j in subs]
    kv = [[mmx(hv(vj[j], h), hk(kl[j], h), "tn", P_GLA) for h in heads] for j in subs]
    o = []
    for j in subs:
        o.append([o_in[j][h] + mmx(hk(ql[j], h), state[h] * hk(e_ref[j], h), "nt", P_GLA) for h in heads])
        state = [state[h] * hk(e_last[j], h) + kv[j][h] for h in heads]
    o = [[x * lax.rsqrt(jnp.mean(x * x, axis=-1, keepdims=True) + NORM_EPS) * norm_w for x in oj] for oj in o]
    return cat_rows(*[jnp.concatenate(oj, axis=1) for oj in o]), state


SOLVE_BLOCK = 128


def solve_unit_lower(ps, ws):
    n = ps[0].shape[0]
    heads = range(len(ps))
    if n > SOLVE_BLOCK:
        half = n // 2
        top = solve_unit_lower([p[:half, :half] for p in ps], [w[:half] for w in ws])
        rest = [ws[h][half:] + mmx(ps[h][half:, :half], top[h], "nn", P_RWKV) for h in heads]
        bottom = solve_unit_lower([p[half:, half:] for p in ps], rest)
        return [cat_rows(top[h], bottom[h]) for h in heads]
    u, p = ws, ps
    levels = max(1, (n - 1).bit_length())
    for it in range(levels):
        if it + 1 < levels:
            y = [mmx(p[h], jnp.concatenate([p[h], u[h]], axis=1), "nn", P_RWKV) for h in heads]
            u = [u[h] + y[h][:, n:] for h in heads]
            p = [y[h][:, :n] for h in heads]
        else:
            u = [u[h] + mmx(p[h], u[h], "nn", P_RWKV) for h in heads]
    return u


def rwkv_chunk(state, toks, params):
    S, pr, pk, pv, pxw, pxa = state
    r_, k_, v_, xw_, xa_ = toks
    mu_r, mu_k, mu_v, mu_xw, mu_xa, w0, w_up, a0, a_up, k_k, k_a, r_k, ln_w, ln_b = params
    c, n = xw_.shape[0], RWKV_N
    heads = range(RWKV_HEADS)
    hs = lambda x, h: x[:, h * n:(h + 1) * n]
    ltri = tril_ones(c)
    stri = tril_ones(c, strict=True)

    def lerp(x, prev, mu):
        return x + (shift_rows(x, prev) - x) * mu

    xw = jnp.tanh(lerp(xw_, pxw, mu_xw))
    xa = lerp(xa_, pxa, mu_xa)
    r = lerp(r_, pr, mu_r)
    k = lerp(k_, pk, mu_k)
    v = lerp(v_, pv, mu_v)
    w = -softplus(-(w0 + mmx(xw, w_up, "nn", P_LORA))) - 0.5
    lw = -jnp.exp(w)
    asig = sigmoid(a0 + mmx(xa, a_up, "nn", P_LORA))
    kk = k * k_k
    kk = kk * lax.rsqrt(jnp.maximum(head_sum(kk * kk), 1e-24))
    k2 = k * (1.0 + (asig - 1.0) * k_a)
    b = kk * asig
    cum = cumsum_rows(lw)
    ref = lax.stop_gradient(row_of(cum, c // 2))
    last = row_of(cum, c - 1)
    at = -kk * jnp.exp(cum - lw - ref)
    rt = r * jnp.exp(cum - ref)
    e_out = jnp.exp(ref - cum)
    bt, kt = b * e_out, k2 * e_out
    e_tail = jnp.exp(last - cum)
    bl, kl = b * e_tail, k2 * e_tail
    e_ref, e_last = jnp.exp(ref), jnp.exp(last)
    g = [mmx(cat_rows(hs(at, h), hs(rt, h)), cat_rows(hs(bt, h), hs(kt, h), S[h] * hs(e_ref, h)), "nt", P_RWKV_G) for h in heads]
    aab = [x[:c, :c] * stri for x in g]
    aak = [x[:c, c:2 * c] * stri for x in g]
    arb = [x[c:, :c] * ltri for x in g]
    ark = [x[c:, c:2 * c] * ltri for x in g]
    av = [mmx(cat_rows(aak[h], ark[h]), hs(v, h), "nn", P_RWKV) for h in heads]
    u = solve_unit_lower(aab, [g[h][:c, 2 * c:] + av[h][:c] for h in heads])
    o = [g[h][c:, 2 * c:] + av[h][c:] + mmx(arb[h], u[h], "nn", P_RWKV) for h in heads]
    s1 = [S[h] * hs(e_last, h) + mmx(cat_rows(u[h], hs(v, h)), cat_rows(hs(bl, h), hs(kl, h)), "tn", P_RWKV) for h in heads]
    o = jnp.concatenate(o, axis=1)
    d = o - head_sum(o) * (1.0 / n)
    var = head_sum(d * d) * (1.0 / n)
    o = d * lax.rsqrt(var + RWKV_LN_EPS) * ln_w + ln_b + head_sum(r * k2 * r_k) * v
    new_state = (s1, row_of(r_, c - 1), row_of(k_, c - 1), row_of(v_, c - 1), row_of(xw_, c - 1), row_of(xa_, c - 1))
    return o, new_state


ROPE_HALF = 8


def _rot_half_raw(x):
    lane = lax.broadcasted_iota(jnp.int32, (x.shape[0], LANES), 1) & (SWA_HD - 1)
    out = []
    for i in range(x.shape[1] // LANES):
        g = x[:, i * LANES:(i + 1) * LANES]
        up, down = pltpu.roll(g, LANES - ROPE_HALF, 1), pltpu.roll(g, ROPE_HALF, 1)
        out.append(jnp.where(lane < ROPE_HALF, -up, jnp.where(lane < 2 * ROPE_HALF, down, 0.0)))
    return out[0] if len(out) == 1 else jnp.concatenate(out, axis=1)


@jax.custom_vjp
def rot_half(x):
    return _rot_half_raw(x)


rot_half.defvjp(lambda x: (_rot_half_raw(x), None), lambda _, g: (-_rot_half_raw(g),))


def rope(x, cos2, sin2):
    reps = x.shape[1] // LANES
    tile = lambda t: t if reps == 1 else jnp.concatenate([t] * reps, axis=1)
    return x * tile(cos2) + rot_half(x) * tile(sin2)


def swa_chunk(state, toks, params, first):
    kprev, vprev = state
    q_, k_, v_, cos, sin = toks
    bq, bk, bv, sinks = params
    c, ng = WINDOW, SWA_GROUP
    n_sub = cos.shape[0] // c
    units = [(j, g) for j in range(n_sub) for g in range(SWA_KV_HEADS)]
    rows = lambda x, j: x[j * c:(j + 1) * c]
    hs = lambda g: range(g * ng, (g + 1) * ng)
    head = lambda x, h: x[:, h * SWA_HD:(h + 1) * SWA_HD]
    qi, kj = _iota2(ng * c, 2 * c)
    qpos = qi & (c - 1)
    cur_ok = (kj >= c) & (qpos >= kj - c)
    prev_ok = (kj < c) & (kj > qpos)
    ok = [cur_ok | (prev_ok & jnp.logical_not(first))] + [cur_ok | prev_ok] * (n_sub - 1)
    q_all = rope(q_ + bq, cos, sin) * (SWA_HD ** -0.5)
    k_all = rope(k_ + bk, cos, sin)
    v_all = v_ + bv
    k = {(j, g): rows(head(k_all, g), j) for j, g in units}
    v = {(j, g): rows(head(v_all, g), j) for j, g in units}
    q = {(j, g): cat_rows(*[rows(head(q_all, h), j) for h in hs(g)]) for j, g in units}
    kp = lambda j, g: kprev[g] if j == 0 else k[(j - 1, g)]
    vp = lambda j, g: vprev[g] if j == 0 else v[(j - 1, g)]
    s = {(j, g): jnp.where(ok[j], mmx(q[(j, g)], cat_rows(kp(j, g), k[(j, g)]), "nt", P_SWA), NEG) for j, g in units}
    sink = [cat_rows(*[jnp.broadcast_to(sinks[h], (c, 1)) for h in hs(g)]) for g in range(SWA_KV_HEADS)]
    m = {(j, g): lax.stop_gradient(jnp.maximum(jnp.max(s[(j, g)], axis=-1, keepdims=True), sink[g])) for j, g in units}
    p = {u: jnp.exp(s[u] - m[u]) for u in units}
    ones = jnp.ones((2 * c, SWA_HD), F32)
    pv = {(j, g): mmx(p[(j, g)], cat_rows(vp(j, g), v[(j, g)]), "nn", P_SWA) for j, g in units}
    den = {u: mmx(p[u], ones, "nn", P_SWA) for u in units}
    o = {(j, g): pv[(j, g)] / (den[(j, g)] + jnp.exp(sink[g] - m[(j, g)])) for j, g in units}
    outs = [cat_rows(*[o[(j, g)][i * c:(i + 1) * c] for j in range(n_sub)]) for g in range(SWA_KV_HEADS) for i in range(ng)]
    last = n_sub - 1
    return outs, ([k[(last, g)] for g in range(SWA_KV_HEADS)], [v[(last, g)] for g in range(SWA_KV_HEADS)])


def _heads(ref, n, w, rows=slice(None)):
    return [ref[rows, h * w:(h + 1) * w] for h in range(n)]


def _put_heads(ref, vals, w, rows=slice(None), add=False):
    for h, val in enumerate(vals):
        if add:
            ref[rows, h * w:(h + 1) * w] += val
        else:
            ref[rows, h * w:(h + 1) * w] = val


def _col(block_w, name, table):
    off, w = table[name]
    assert off % block_w == 0 and w % block_w == 0
    return off // block_w


def _tok_spec(c, w, colblock, n=None):
    if n is None:
        return pl.BlockSpec((c, w), lambda i: (i, colblock))
    return pl.BlockSpec((c, w), lambda i: (n - 1 - i, colblock))


def _full_spec(shape):
    return pl.BlockSpec(shape, lambda i: (0,) * len(shape))


def _matmul(name, a, b, mode, tm, tn, out_dtype=F32):
    (m, kd) = (a.shape[1], a.shape[0]) if mode == "tn" else a.shape
    n = b.shape[0] if mode == "nt" else b.shape[1]
    assert m % tm == 0 and n % tn == 0
    a_spec = pl.BlockSpec((kd, tm), lambda j, i: (0, i)) if mode == "tn" else pl.BlockSpec((tm, kd), lambda j, i: (i, 0))
    b_spec = pl.BlockSpec((tn, kd), lambda j, i: (j, 0)) if mode == "nt" else pl.BlockSpec((kd, tn), lambda j, i: (0, j))

    def body(a_ref, b_ref, o_ref):
        o_ref[...] = lax.dot_general(a_ref[...].astype(BF16), b_ref[...].astype(BF16), DIMS[mode],
                                     preferred_element_type=F32).astype(out_dtype)

    return pl.pallas_call(
        body, name=name, grid=(n // tn, m // tm), in_specs=[a_spec, b_spec],
        out_specs=pl.BlockSpec((tm, tn), lambda j, i: (i, j)), out_shape=jax.ShapeDtypeStruct((m, n), out_dtype),
        compiler_params=_cparams(("arbitrary", "arbitrary")))(a, b)


TOK_TILE = 512
PROJ_ROWS = 1024
DW_COLS = 512
ADAM_COLS = 256


def _matmul_fused(name, a, b, mode, tiles, rows, outs, sums, epilogue, comm=(), kinds=()):
    made = callable(a)
    m = tiles[0][0].shape[0] if made else a.shape[0]
    kd = b.shape[0] if mode == "nn" else b.shape[1]
    n = b.shape[1] if mode == "nn" else b.shape[0]
    tm = TOK_TILE
    steps = m // tm
    if made:
        outs = [(kd, BF16)] + list(outs)
    nt_, nr, no, ns, ncomm = len(tiles), len(rows), len(outs), len(sums), len(comm)

    def body(*refs):
        at = 1 if made else 2
        b_ref = refs[at - 1]
        tile_refs, row_refs, comm_in = refs[at:at + nt_], refs[at + nt_:at + nt_ + nr], refs[at + nt_ + nr:at + nt_ + nr + ncomm]
        at += nt_ + nr + ncomm
        out_refs, sum_refs, comm_out = refs[at:at + no], refs[at + no:at + no + ns], refs[at + no + ns:at + no + ns + ncomm]
        sems = refs[at + no + ns + ncomm:]
        i = pl.program_id(0)

        @pl.when(i == 0)
        def _():
            if ncomm:
                _comm_start(*_comm_copies(comm_in, comm_out, kinds, *sems))
            for ref in sum_refs:
                ref[...] = jnp.zeros_like(ref)

        extras = [r[...] for r in tile_refs] + [r[...] for r in row_refs]
        a_blk = (a(*extras) if made else refs[0][...]).astype(BF16)
        acc = lax.dot_general(a_blk, b_ref[...].astype(BF16), DIMS[mode], preferred_element_type=F32)
        res = epilogue(acc, *extras)
        if made:
            res = (a_blk,) + tuple(res)
        for ref, val in zip(out_refs, res[:no]):
            ref[...] = val.astype(ref.dtype)
        for ref, val in zip(sum_refs, res[no:]):
            ref[...] += val

        if ncomm:
            @pl.when(i == steps - 1)
            def _():
                _comm_wait(*_comm_copies(comm_in, comm_out, kinds, *sems))

    in_specs = ([] if made else [pl.BlockSpec((tm, kd), lambda i: (i, 0))]) + [_full_spec(b.shape)]
    in_specs += [pl.BlockSpec((tm, w), functools.partial(lambda i, cb: (i, cb), cb=cb)) for _, w, cb in tiles]
    in_specs += [_full_spec(r.shape) for r in rows] + [ANY] * ncomm
    out_specs = [pl.BlockSpec((tm, w), lambda i: (i, 0)) for w, _ in outs] + [_full_spec((1, w)) for w in sums] + [ANY] * ncomm
    out_shape = ([jax.ShapeDtypeStruct((m, w), dt) for w, dt in outs] + [jax.ShapeDtypeStruct((1, w), F32) for w in sums]
                 + _comm_out_shapes(comm, kinds))
    return pl.pallas_call(body, name=name, grid=(steps,), in_specs=in_specs, out_specs=out_specs, out_shape=out_shape,
                          scratch_shapes=_comm_scratch(ncomm) if ncomm else [],
                          compiler_params=_cparams(("arbitrary",)))(*([] if made else [a]), b, *[t[0] for t in tiles], *rows, *comm)


def _resid_norm(y, x, w):
    h = x + y
    return h, rms(h, w)


def _norm_back(dhn, h, dres, w):
    _, vjp = jax.vjp(rms, h, w)
    dh, dw = vjp(dhn)
    return dh + dres, dw


def _gate_back(dog, *o_and_gate):
    outs, g = o_and_gate[:-1], o_and_gate[-1]
    s = sigmoid(g)
    silu, dsilu = g * s, s * (1.0 + g * (1.0 - s))
    d_outs, c = [], 0
    for o in outs:
        w = o.shape[1]
        d_outs.append(dog[:, c:c + w] * silu[:, c:c + w])
        c += w
    o_all = outs[0] if len(outs) == 1 else jnp.concatenate(outs, axis=1)
    return (*d_outs, dog * o_all * dsilu)


def _loss_head(y1, h1, target, b_out, fw):
    def f(h2, w):
        err = rms(h2, w) - target
        return 0.5 * jnp.sum(jnp.mean(err * err, axis=-1, keepdims=True), axis=0, keepdims=True)

    loss, vjp = jax.vjp(f, h1 + y1 + b_out, fw)
    dh2, dfw = vjp(jnp.ones((1, 1), F32))
    return dh2, jnp.broadcast_to(loss, (1, LANES)), jnp.sum(dh2, axis=0, keepdims=True), dfw


def _gla_load(q_ref, k_ref, v_ref, gl_ref, up_ref, bias_ref, nw_ref):
    toks = (q_ref[...], k_ref[...], v_ref[...], gl_ref[...])
    params = (up_ref[...], bias_ref[...], nw_ref[...])
    return toks, params


def _gla_specs(c, n=None):
    toks = [_tok_spec(c, GLA_KEY, _col(GLA_KEY, "gq", C0), n), _tok_spec(c, GLA_KEY, _col(GLA_KEY, "gk", C0), n),
            _tok_spec(c, GLA_VAL, _col(GLA_VAL, "gv", C0), n), _tok_spec(c, LOW, _col(LOW, "glow", C0), n)]
    return toks, [_full_spec(s) for s in GLA_PARAM_SHAPES]


GLA_PARAM_SHAPES = [(LOW, GLA_KEY), (1, GLA_KEY), (1, GLA_DV)]
GLA_STATE = (GLA_HEADS * GLA_DV, GLA_DK)


def _gla_fwd(proj0, gk_up, gk_bias, norm_w):
    t = proj0.shape[0]
    c = GLA_STEP
    nc = t // c
    toks_s, params_s = _gla_specs(c)

    def body(q_ref, k_ref, v_ref, gl_ref, up_ref, bias_ref, nw_ref, o_ref, st_ref, s_scr):
        @pl.when(pl.program_id(0) == 0)
        def _():
            s_scr[...] = jnp.zeros_like(s_scr)

        st_ref[...] = s_scr[...]
        toks, params = _gla_load(q_ref, k_ref, v_ref, gl_ref, up_ref, bias_ref, nw_ref)
        state = [s_scr[h * GLA_DV:(h + 1) * GLA_DV, :] for h in range(GLA_HEADS)]
        o_ref[...], new = gla_chunk(state, toks, params)
        for h in range(GLA_HEADS):
            s_scr[h * GLA_DV:(h + 1) * GLA_DV, :] = new[h]

    return pl.pallas_call(
        body, name="gla_fwd", grid=(nc,), in_specs=toks_s + params_s,
        out_specs=(_tok_spec(c, GLA_VAL, 0), pl.BlockSpec(GLA_STATE, lambda i: (i, 0))),
        out_shape=(jax.ShapeDtypeStruct((t, GLA_VAL), F32), jax.ShapeDtypeStruct((nc * GLA_STATE[0], GLA_DK), F32)),
        scratch_shapes=[pltpu.VMEM(GLA_STATE, F32)], compiler_params=_cparams(("arbitrary",)))(
            proj0, proj0, proj0, proj0, gk_up, gk_bias, norm_w)


def _gla_bwd(proj0, gk_up, gk_bias, norm_w, states, do):
    t = proj0.shape[0]
    c = GLA_STEP
    nc = t // c
    toks_s, params_s = _gla_specs(c, nc)

    def body(q_ref, k_ref, v_ref, gl_ref, up_ref, bias_ref, nw_ref, st_ref, do_ref,
             dq_ref, dk_ref, dv_ref, dgl_ref, dup_ref, dbias_ref, dnw_ref, ds_scr):
        @pl.when(pl.program_id(0) == 0)
        def _():
            ds_scr[...] = jnp.zeros_like(ds_scr)
            dup_ref[...] = jnp.zeros_like(dup_ref)
            dbias_ref[...] = jnp.zeros_like(dbias_ref)
            dnw_ref[...] = jnp.zeros_like(dnw_ref)

        toks, params = _gla_load(q_ref, k_ref, v_ref, gl_ref, up_ref, bias_ref, nw_ref)
        rows = lambda h: slice(h * GLA_DV, (h + 1) * GLA_DV)
        state = [st_ref[rows(h), :] for h in range(GLA_HEADS)]
        _, vjp = jax.vjp(gla_chunk, state, toks, params)
        dstate_in = [ds_scr[rows(h), :] for h in range(GLA_HEADS)]
        dstate, dtoks, (dup, dbias, dnw) = vjp((do_ref[...], dstate_in))
        for ref, val in zip((dq_ref, dk_ref, dv_ref, dgl_ref), dtoks):
            ref[...] = val.astype(ref.dtype)
        dup_ref[...] += dup
        dbias_ref[...] += dbias
        dnw_ref[...] += dnw
        for h in range(GLA_HEADS):
            ds_scr[rows(h), :] = dstate[h]

    rev = lambda w: pl.BlockSpec((c, w), lambda i: (nc - 1 - i, 0))
    tok_widths = (GLA_KEY, GLA_KEY, GLA_VAL, LOW)
    return pl.pallas_call(
        body, name="gla_bwd", grid=(nc,),
        in_specs=toks_s + params_s + [pl.BlockSpec(GLA_STATE, lambda i: (nc - 1 - i, 0)), rev(GLA_VAL)],
        out_specs=[rev(w) for w in tok_widths] + params_s,
        out_shape=[jax.ShapeDtypeStruct((t, w), BF16) for w in tok_widths] + [jax.ShapeDtypeStruct(s, F32) for s in GLA_PARAM_SHAPES],
        scratch_shapes=[pltpu.VMEM(GLA_STATE, F32)], compiler_params=_cparams(("arbitrary",)))(
            proj0, proj0, proj0, proj0, gk_up, gk_bias, norm_w, states, do)


RWKV_PARAM_SHAPES = [(1, RWKV_W), (1, RWKV_W), (1, RWKV_W), (1, LOW), (1, LOW), (1, RWKV_W), (LOW, RWKV_W), (1, RWKV_W),
                     (LOW, RWKV_W), (1, RWKV_W), (1, RWKV_W), (1, RWKV_W), (1, RWKV_W), (1, RWKV_W)]
RWKV_STATE = (RWKV_HEADS * RWKV_N, RWKV_N)
RWKV_TOK_WIDTHS = (RWKV_W, RWKV_W, RWKV_W, LOW, LOW)
PREV_W = sum(RWKV_TOK_WIDTHS)
PREV_COLS = [slice(sum(RWKV_TOK_WIDTHS[:i]), sum(RWKV_TOK_WIDTHS[:i + 1])) for i in range(len(RWKV_TOK_WIDTHS))]


def _rwkv_load(r_ref, k_ref, v_ref, xw_ref, xa_ref, p_refs):
    toks = (r_ref[...], k_ref[...], v_ref[...], xw_ref[...], xa_ref[...])
    return toks, tuple(p[...] for p in p_refs)


def _rwkv_state(s_ref, prev_ref):
    n = RWKV_N
    S = [s_ref[h * n:(h + 1) * n, :] for h in range(RWKV_HEADS)]
    return (S,) + tuple(prev_ref[0:1, cols] for cols in PREV_COLS)


def _rwkv_put_state(s_ref, prev_ref, state):
    n = RWKV_N
    for h in range(RWKV_HEADS):
        s_ref[h * n:(h + 1) * n, :] = state[0][h]
    for cols, val in zip(PREV_COLS, state[1:]):
        prev_ref[0:1, cols] = val


def _rwkv_specs(c, n=None):
    toks = [_tok_spec(c, w, _col(w, name, C0), n) for name, w in zip(("r", "k", "v", "xw", "xa"), RWKV_TOK_WIDTHS)]
    return toks, [_full_spec(s) for s in RWKV_PARAM_SHAPES]


def _rwkv_fwd(proj0, params, comm, kinds):
    t = proj0.shape[0]
    c = RWKV_CHUNK
    nc = t // c
    toks_s, params_s = _rwkv_specs(c)
    npar, ncomm = len(params), len(comm)

    def body(*refs):
        tok_refs, p_refs = refs[:5], refs[5:5 + npar]
        comm_in = refs[5 + npar:5 + npar + ncomm]
        o_ref, st_ref, pst_ref = refs[5 + npar + ncomm:8 + npar + ncomm]
        comm_out = refs[8 + npar + ncomm:8 + npar + 2 * ncomm]
        s_scr, prev_scr = refs[8 + npar + 2 * ncomm:10 + npar + 2 * ncomm]
        sems = refs[10 + npar + 2 * ncomm:]
        i = pl.program_id(0)

        @pl.when(i == 0)
        def _():
            _comm_start(*_comm_copies(comm_in, comm_out, kinds, *sems))
            s_scr[...] = jnp.zeros_like(s_scr)
            prev_scr[...] = jnp.zeros_like(prev_scr)

        st_ref[...] = s_scr[...]
        pst_ref[...] = prev_scr[...]
        toks, prm = _rwkv_load(*tok_refs, p_refs)
        o_ref[...], new = rwkv_chunk(_rwkv_state(s_scr, prev_scr), toks, prm)
        _rwkv_put_state(s_scr, prev_scr, new)

        @pl.when(i == nc - 1)
        def _():
            _comm_wait(*_comm_copies(comm_in, comm_out, kinds, *sems))

    outs = pl.pallas_call(
        body, name="rwkv_fwd", grid=(nc,), in_specs=toks_s + params_s + [ANY] * ncomm,
        out_specs=[_tok_spec(c, RWKV_W, 0), pl.BlockSpec(RWKV_STATE, lambda i: (i, 0)), pl.BlockSpec((8, PREV_W), lambda i: (i, 0))]
        + [ANY] * ncomm,
        out_shape=[jax.ShapeDtypeStruct((t, RWKV_W), F32), jax.ShapeDtypeStruct((nc * RWKV_STATE[0], RWKV_N), F32),
                   jax.ShapeDtypeStruct((nc * 8, PREV_W), F32)] + _comm_out_shapes(comm, kinds),
        scratch_shapes=[pltpu.VMEM(RWKV_STATE, F32), pltpu.VMEM((8, PREV_W), F32)] + _comm_scratch(ncomm),
        compiler_params=_cparams(("arbitrary",)))(proj0, proj0, proj0, proj0, proj0, *params, *comm)
    return outs[0], outs[1], outs[2], outs[3:]


def _rwkv_bwd(proj0, params, states, prevs, do, comm, kinds):
    t = proj0.shape[0]
    c = RWKV_CHUNK
    nc = t // c
    toks_s, params_s = _rwkv_specs(c, nc)
    npar, ncomm = len(params), len(comm)

    def body(*refs):
        tok_refs, p_refs = refs[:5], refs[5:5 + npar]
        st_ref, pst_ref, do_ref = refs[5 + npar:8 + npar]
        comm_in = refs[8 + npar:8 + npar + ncomm]
        outs = refs[8 + npar + ncomm:]
        dtok_refs, dp_refs, comm_out = outs[:5], outs[5:5 + npar], outs[5 + npar:5 + npar + ncomm]
        ds_scr, dprev_scr = outs[5 + npar + ncomm:7 + npar + ncomm]
        sems = outs[7 + npar + ncomm:]
        i = pl.program_id(0)

        @pl.when(i == 0)
        def _():
            _comm_start(*_comm_copies(comm_in, comm_out, kinds, *sems))
            ds_scr[...] = jnp.zeros_like(ds_scr)
            dprev_scr[...] = jnp.zeros_like(dprev_scr)
            for dp in dp_refs:
                dp[...] = jnp.zeros_like(dp)

        toks, prm = _rwkv_load(*tok_refs, p_refs)
        _, vjp = jax.vjp(rwkv_chunk, _rwkv_state(st_ref, pst_ref), toks, prm)
        dstate, dtoks, dprm = vjp((do_ref[...], _rwkv_state(ds_scr, dprev_scr)))
        for ref, val in zip(dtok_refs, dtoks):
            ref[...] = val.astype(ref.dtype)
        for ref, val in zip(dp_refs, dprm):
            ref[...] += val
        _rwkv_put_state(ds_scr, dprev_scr, dstate)

        @pl.when(i == nc - 1)
        def _():
            _comm_wait(*_comm_copies(comm_in, comm_out, kinds, *sems))

    rev = lambda w: pl.BlockSpec((c, w), lambda i: (nc - 1 - i, 0))
    outs = pl.pallas_call(
        body, name="rwkv_bwd", grid=(nc,),
        in_specs=toks_s + params_s + [pl.BlockSpec(RWKV_STATE, lambda i: (nc - 1 - i, 0)),
                                      pl.BlockSpec((8, PREV_W), lambda i: (nc - 1 - i, 0)), rev(RWKV_W)] + [ANY] * ncomm,
        out_specs=[rev(w) for w in RWKV_TOK_WIDTHS] + params_s + [ANY] * ncomm,
        out_shape=[jax.ShapeDtypeStruct((t, w), BF16) for w in RWKV_TOK_WIDTHS]
        + [jax.ShapeDtypeStruct(s, F32) for s in RWKV_PARAM_SHAPES] + _comm_out_shapes(comm, kinds),
        scratch_shapes=[pltpu.VMEM(RWKV_STATE, F32), pltpu.VMEM((8, PREV_W), F32)] + _comm_scratch(ncomm),
        compiler_params=_cparams(("arbitrary",)))(proj0, proj0, proj0, proj0, proj0, *params, states, prevs, do, *comm)
    return outs[:5], outs[5:5 + npar], outs[5 + npar:]


def _swa_load(q_ref, k_ref, v_ref, cos_ref, sin_ref, bq_ref, bk_ref, bv_ref, sk_ref):
    toks = (q_ref[...], k_ref[...], v_ref[...], cos_ref[...], sin_ref[...])
    params = (bq_ref[...], bk_ref[...], bv_ref[...], _heads(sk_ref, SWA_Q_HEADS, 1))
    return toks, params


SWA_TOK_WIDTHS = (MIX, SWA_KV, SWA_KV)
SWA_PARAM_SHAPES = [(1, MIX), (1, SWA_KV), (1, SWA_KV), (1, SWA_Q_HEADS)]
SWA_STATE = (WINDOW, SWA_KV)


def _swa_specs(c, n=None):
    toks = [_tok_spec(c, w, _col(w, name, C1), n) for name, w in zip(("q", "k", "v"), SWA_TOK_WIDTHS)]
    toks += [_tok_spec(c, LANES, 0, n), _tok_spec(c, LANES, 0, n)]
    return toks, [_full_spec(s) for s in SWA_PARAM_SHAPES]


def _swa_fwd(proj1, cos, sin, bq, bk, bv, sinks):
    t = proj1.shape[0]
    c = SWA_STEP
    nb = t // c
    toks_s, params_s = _swa_specs(c)
    state_spec = pl.BlockSpec(SWA_STATE, lambda i: (i, 0))
    kv = SWA_KV_HEADS

    def body(q_ref, k_ref, v_ref, cos_ref, sin_ref, bq_ref, bk_ref, bv_ref, sk_ref, o_ref, kst_ref, vst_ref, k_scr, v_scr):
        first = pl.program_id(0) == 0

        @pl.when(first)
        def _():
            k_scr[...] = jnp.zeros_like(k_scr)
            v_scr[...] = jnp.zeros_like(v_scr)

        kst_ref[...] = k_scr[...]
        vst_ref[...] = v_scr[...]
        toks, params = _swa_load(q_ref, k_ref, v_ref, cos_ref, sin_ref, bq_ref, bk_ref, bv_ref, sk_ref)
        outs, (kn, vn) = swa_chunk((_heads(k_scr, kv, SWA_HD), _heads(v_scr, kv, SWA_HD)), toks, params, first)
        _put_heads(o_ref, outs, SWA_HD)
        _put_heads(k_scr, kn, SWA_HD)
        _put_heads(v_scr, vn, SWA_HD)

    saved = jax.ShapeDtypeStruct((nb * WINDOW, SWA_KV), F32)
    return pl.pallas_call(
        body, name="swa_fwd", grid=(nb,), in_specs=toks_s + params_s,
        out_specs=(_tok_spec(c, MIX, 0), state_spec, state_spec),
        out_shape=(jax.ShapeDtypeStruct((t, MIX), F32), saved, saved),
        scratch_shapes=[pltpu.VMEM(SWA_STATE, F32), pltpu.VMEM(SWA_STATE, F32)],
        compiler_params=_cparams(("arbitrary",)))(proj1, proj1, proj1, cos, sin, bq, bk, bv, sinks)


def _swa_bwd(proj1, cos, sin, bq, bk, bv, sinks, kst, vst, do):
    t = proj1.shape[0]
    c = SWA_STEP
    nb = t // c
    toks_s, params_s = _swa_specs(c, nb)
    state_spec = pl.BlockSpec(SWA_STATE, lambda i: (nb - 1 - i, 0))
    kv = SWA_KV_HEADS

    def body(q_ref, k_ref, v_ref, cos_ref, sin_ref, bq_ref, bk_ref, bv_ref, sk_ref, kst_ref, vst_ref, do_ref,
             dq_ref, dk_ref, dv_ref, dbq_ref, dbk_ref, dbv_ref, dsk_ref, dk_scr, dv_scr):
        i = pl.program_id(0)

        @pl.when(i == 0)
        def _():
            dk_scr[...] = jnp.zeros_like(dk_scr)
            dv_scr[...] = jnp.zeros_like(dv_scr)
            for ref in (dbq_ref, dbk_ref, dbv_ref, dsk_ref):
                ref[...] = jnp.zeros_like(ref)

        first = i == nb - 1
        toks, params = _swa_load(q_ref, k_ref, v_ref, cos_ref, sin_ref, bq_ref, bk_ref, bv_ref, sk_ref)
        f = functools.partial(swa_chunk, first=first)
        _, vjp = jax.vjp(f, (_heads(kst_ref, kv, SWA_HD), _heads(vst_ref, kv, SWA_HD)), toks, params)
        dstate_in = (_heads(dk_scr, kv, SWA_HD), _heads(dv_scr, kv, SWA_HD))
        (dkp, dvp), (dq, dk, dv, _, _), (dbq, dbk, dbv, dsk) = vjp((_heads(do_ref, SWA_Q_HEADS, SWA_HD), dstate_in))
        dq_ref[...], dk_ref[...], dv_ref[...] = dq.astype(BF16), dk.astype(BF16), dv.astype(BF16)
        dbq_ref[...] += dbq
        dbk_ref[...] += dbk
        dbv_ref[...] += dbv
        _put_heads(dsk_ref, dsk, 1, add=True)
        _put_heads(dk_scr, dkp, SWA_HD)
        _put_heads(dv_scr, dvp, SWA_HD)

    rev = lambda w: pl.BlockSpec((c, w), lambda i: (nb - 1 - i, 0))
    return pl.pallas_call(
        body, name="swa_bwd", grid=(nb,), in_specs=toks_s + params_s + [state_spec, state_spec, rev(MIX)],
        out_specs=[rev(w) for w in SWA_TOK_WIDTHS] + params_s,
        out_shape=[jax.ShapeDtypeStruct((t, w), BF16) for w in SWA_TOK_WIDTHS] + [jax.ShapeDtypeStruct(s, F32) for s in SWA_PARAM_SHAPES],
        scratch_shapes=[pltpu.VMEM(SWA_STATE, F32), pltpu.VMEM(SWA_STATE, F32)],
        compiler_params=_cparams(("arbitrary",)))(proj1, proj1, proj1, cos, sin, bq, bk, bv, sinks, kst, vst, do)


MESH = pl.DeviceIdType.MESH
ANY = pl.BlockSpec(memory_space=pl.ANY)


def _my_place():
    return lax.axis_index("x"), lax.axis_index("y"), lax.axis_index("c")


def _all_gather(shards):
    n = len(shards)

    def body(*refs):
        in_refs, out_refs = refs[:n], refs[n:2 * n]
        send_sems, recv_sems, local_sems = refs[2 * n:]
        x, y, c = _my_place()
        me, sibling = (x, y, c), (x, y, 1 - c)
        chips = [(1 - x, y), (x, 1 - y), (1 - x, 1 - y)]

        def slot(out_ref, place):
            px, py, pc = place
            return out_ref.at[4 * px + 2 * py + pc]

        def copy(a, k, block, to, src=None):
            return pltpu.make_async_remote_copy(
                src_ref=slot(out_refs[a], block) if src is None else src, dst_ref=slot(out_refs[a], block),
                send_sem=send_sems.at[a, k], recv_sem=recv_sems.at[a, k], device_id=to, device_id_type=MESH)

        mine = [pltpu.make_async_copy(in_refs[a], slot(out_refs[a], me), local_sems.at[a]) for a in range(n)]
        for cp in mine:
            cp.start()
        first = []
        for a in range(n):
            first.append(copy(a, 0, me, sibling, src=in_refs[a]))
            first += [copy(a, 1 + j, me, (*chip, c), src=in_refs[a]) for j, chip in enumerate(chips)]
        for cp in first:
            cp.start()
        passed = []
        for j, chip in enumerate(chips):
            for a in range(n):
                copy(a, 1 + j, (*chip, c), me).wait_recv()
                fwd = copy(a, 4 + j, (*chip, c), sibling)
                fwd.start()
                passed.append(fwd)
        for a in range(n):
            copy(a, 0, sibling, me).wait_recv()
            for j, chip in enumerate(chips):
                copy(a, 4 + j, (*chip, 1 - c), me).wait_recv()
        for cp in first + passed:
            cp.wait_send()
        for cp in mine:
            cp.wait()

    return pl.pallas_call(
        body, name="all_gather_weights", in_specs=[ANY] * n, out_specs=[ANY] * n,
        out_shape=[jax.ShapeDtypeStruct((N_DEV,) + s.shape, s.dtype) for s in shards],
        scratch_shapes=_comm_scratch(n))(*shards)


def _comm_copies(in_refs, out_refs, kinds, send_sems, recv_sems, local_sems):
    x, y, c = _my_place()
    my_idx = 4 * x + 2 * y + c
    src = lambda a, idx: in_refs[a] if kinds[a] == "gather" else in_refs[a].at[idx]
    local = [pltpu.make_async_copy(src(a, my_idx), out_refs[a].at[my_idx], local_sems.at[a]) for a in range(len(kinds))]
    remote = []
    for rel in range(1, N_DEV):
        px, py, pc = x ^ ((rel >> 2) & 1), y ^ ((rel >> 1) & 1), c ^ (rel & 1)
        for a in range(len(kinds)):
            remote.append(pltpu.make_async_remote_copy(
                src_ref=src(a, 4 * px + 2 * py + pc), dst_ref=out_refs[a].at[my_idx], send_sem=send_sems.at[a, rel - 1],
                recv_sem=recv_sems.at[a, rel - 1], device_id=(px, py, pc), device_id_type=MESH))
    return local, remote


def _comm_start(local, remote):
    for cp in local + remote:
        cp.start()


def _comm_wait(local, remote):
    for cp in remote:
        cp.wait_recv()
    for cp in remote:
        cp.wait_send()
    for cp in local:
        cp.wait()


def _comm_out_shapes(arrays, kinds):
    return [jax.ShapeDtypeStruct(((N_DEV,) + a.shape) if k == "gather" else a.shape, a.dtype) for a, k in zip(arrays, kinds)]


def _comm_scratch(n):
    return [pltpu.SemaphoreType.DMA((n, N_DEV - 1)), pltpu.SemaphoreType.DMA((n, N_DEV - 1)), pltpu.SemaphoreType.DMA((n,))]


def _sequencer_scatter(name, parts, collective_id):
    src = jax.new_ref(parts, memory_space=pltpu.MemorySpace.HBM)
    dst = jax.empty_ref(jax.ShapeDtypeStruct(parts.shape, parts.dtype), memory_space=pltpu.MemorySpace.HBM)

    @pl.kernel(mesh=plsc.ScalarSubcoreMesh(axis_name="sequencer", num_cores=1), name=name,
               scratch_types=(pltpu.SemaphoreType.DMA((N_DEV - 1,)), pltpu.SemaphoreType.DMA((N_DEV - 1,))),
               compiler_params=pltpu.CompilerParams(collective_id=collective_id))
    def launch(send_sems, recv_sems):
        x, y, c = _my_place()
        my_idx = 4 * x + 2 * y + c
        peers = [(x ^ ((rel >> 2) & 1), y ^ ((rel >> 1) & 1), c ^ (rel & 1)) for rel in range(1, N_DEV)]
        barrier = pltpu.get_barrier_semaphore()
        for peer in peers:
            pl.semaphore_signal(barrier, inc=1, device_id=peer, device_id_type=MESH)
        pl.semaphore_wait(barrier, N_DEV - 1)
        copies = [pltpu.make_async_remote_copy(
            src_ref=src.at[4 * px + 2 * py + pc], dst_ref=dst.at[my_idx], send_sem=send_sems.at[k], recv_sem=recv_sems.at[k],
            device_id=(px, py, pc), device_id_type=MESH) for k, (px, py, pc) in enumerate(peers)]
        for cp in copies:
            cp.start()
        for cp in copies:
            cp.wait_recv()
        for cp in copies:
            cp.wait_send()

    launch()
    return dst[...]


def _exchange(arrays, kinds):
    n = len(arrays)

    def body(*refs):
        copies = _comm_copies(refs[:n], refs[n:2 * n], kinds, *refs[2 * n:])
        _comm_start(*copies)
        _comm_wait(*copies)

    return pl.pallas_call(body, name="exchange_grads", in_specs=[ANY] * n, out_specs=[ANY] * n,
                          out_shape=_comm_out_shapes(arrays, kinds), scratch_shapes=_comm_scratch(n))(*arrays)


def _adam_math(w, g, m, v):
    m = ADAM_B1 * m + (1.0 - ADAM_B1) * g
    v = ADAM_B2 * v + (1.0 - ADAM_B2) * (g * g)
    m_hat = m / (1.0 - ADAM_B1 ** ADAM_STEP)
    v_hat = v / (1.0 - ADAM_B2 ** ADAM_STEP)
    delta = -ADAM_LR * (m_hat / (jnp.sqrt(v_hat) + ADAM_EPS) + ADAM_WD * w)
    return delta, m, v


def _adamw(name, w, gslots, m, v, tc):
    r, cc = w.shape
    assert cc % tc == 0
    tile = pl.BlockSpec((r, tc), lambda i: (0, i))

    def body(w_ref, g_ref, m_ref, v_ref, go_ref, d_ref, mo_ref, vo_ref):
        g = g_ref[0].astype(F32)
        for s in range(1, N_DEV):
            g = g + g_ref[s].astype(F32)
        d, mn, vn = _adam_math(w_ref[...], g, m_ref[...], v_ref[...])
        go_ref[...] = g
        d_ref[...] = d
        mo_ref[...] = mn
        vo_ref[...] = vn

    shp = jax.ShapeDtypeStruct((r, cc), F32)
    return pl.pallas_call(body, name=name, grid=(cc // tc,),
                          in_specs=[tile, pl.BlockSpec((N_DEV, r, tc), lambda i: (0, 0, i)), tile, tile],
                          out_specs=(tile,) * 4, out_shape=(shp,) * 4, compiler_params=_cparams(("arbitrary",)))(w, gslots, m, v)


PACK_TILE = 8 * LANES


def _packed_rows(shape, mode):
    r, w = shape
    return -(-r // 8) * 8 if mode == "rows" else -(-(r * w) // PACK_TILE) * 8


def _pack_small(arrays, modes, lead=False):
    out = []
    for a, mode in zip(arrays, modes):
        a = a.astype(F32) if lead else a.astype(F32)[None]
        if mode == "rows":
            out.append(jnp.pad(a, ((0, 0), (0, (-a.shape[1]) % 8), (0, LANES - a.shape[2]))))
        else:
            flat = a.reshape(a.shape[0], -1)
            out.append(jnp.pad(flat, ((0, 0), (0, (-flat.shape[1]) % PACK_TILE))).reshape(a.shape[0], -1, LANES))
    out = jnp.concatenate(out, axis=1)
    return out if lead else out[0]


def _take_small(packed, row0, shape, mode):
    r, w = shape
    lead = packed.ndim == 3
    if mode == "rows":
        return packed[:, row0:row0 + r, :w] if lead else packed[row0:row0 + r, :w]
    per_row = -(-w // LANES)
    if lead:
        return packed[:, row0:row0 + r * per_row].reshape(packed.shape[0], r, per_row * LANES)[:, :, :w]
    rows = []
    for i in range(r):
        pieces = [packed[row0 + i * per_row + j:row0 + i * per_row + j + 1, :] for j in range(per_row)]
        rows.append((pieces[0] if per_row == 1 else jnp.concatenate(pieces, axis=1))[:, :w])
    return rows[0] if r == 1 else jnp.concatenate(rows, axis=0)


def _adamw_small(slots, specs, ws, ms, vs, loss_row):
    n = len(specs)

    def body(*refs):
        slots_ref, w_refs, m_refs, v_refs = refs[0], refs[1:1 + n], refs[1 + n:1 + 2 * n], refs[1 + 2 * n:1 + 3 * n]
        out_refs, loss_ref = refs[1 + 3 * n:1 + 7 * n], refs[1 + 7 * n]
        gp = slots_ref[0]
        for s in range(1, N_DEV):
            gp = gp + slots_ref[s]
        read = lambda ref: ref[0] if len(ref.shape) == 3 else ref[...]
        for k, (shape, mode, row0) in enumerate(specs):
            g = _take_small(gp, row0, shape, mode)
            d, mn, vn = _adam_math(read(w_refs[k]), g, read(m_refs[k]), read(v_refs[k]))
            for ref, val in zip(out_refs[4 * k:4 * k + 4], (g, d, mn, vn)):
                if len(ref.shape) == 3:
                    ref[0] = val
                else:
                    ref[...] = val
        loss_ref[...] = gp[loss_row:loss_row + 1, :]

    vmem = pl.BlockSpec(memory_space=pltpu.VMEM)
    out_shape = [jax.ShapeDtypeStruct(w.shape, F32) for w in ws for _ in range(4)] + [jax.ShapeDtypeStruct((1, LANES), F32)]
    outs = pl.pallas_call(body, name="adamw_small", in_specs=[vmem] * (1 + 3 * n), out_specs=[vmem] * (4 * n + 1),
                          out_shape=out_shape)(slots, *ws, *ms, *vs)
    return [outs[4 * k:4 * k + 4] for k in range(n)], outs[4 * n]


def _rope_tables(t):
    dim = jnp.arange(LANES) % SWA_HD
    inv_freq = ROPE_THETA ** (-(dim % ROPE_HALF).astype(F32) / ROPE_HALF)
    ang = jnp.arange(t, dtype=F32)[:, None] * jnp.where(dim < 2 * ROPE_HALF, inv_freq, 0.0)[None, :]
    return jnp.cos(ang), jnp.sin(ang)


def _pad_to(a, rows=None, cols=None):
    r = 0 if rows is None else rows - a.shape[0]
    c = 0 if cols is None else cols - a.shape[1]
    return jnp.pad(a, ((0, r), (0, c)))


ORIG0 = dict(gq=(0, 256), gk=(256, 256), gv=(512, 512), glow=(1024, 16), r=(1040, 512), k=(1552, 512), v=(2064, 512),
             xw=(2576, 64), xa=(2640, 64), gate=(2704, 1024))
ORIG0_ORDER = ["gq", "gk", "gv", "glow", "r", "k", "v", "xw", "xa", "gate"]


def _w0t_to_padded(wt):
    rows, at = [], 0
    for name, (off, width) in sorted(C0.items(), key=lambda kv: kv[1][0]):
        assert off == at
        src, src_w = ORIG0[name]
        rows.append(_pad_to(wt[src:src + src_w], rows=width))
        at += width
    rows.append(jnp.zeros((N0P - at, wt.shape[1]), wt.dtype))
    return jnp.concatenate(rows, axis=0)


def _w0t_from_padded(wpt):
    return jnp.concatenate([wpt[C0[n][0]:C0[n][0] + ORIG0[n][1]] for n in ORIG0_ORDER], axis=0)


def _w1t_to_mine(wt):
    return jnp.concatenate([wt[1536:2560], wt[:1536]], axis=0)


def _w1t_from_mine(wt):
    return jnp.concatenate([wt[1024:2560], wt[:1024]], axis=0)


def kernel(x, norm_w, w_in0, gla_gk_up, gla_gk_bias, gla_norm_w, rwkv_mu, rwkv_w0, rwkv_w_up, rwkv_a0, rwkv_a_up, rwkv_k_k, rwkv_k_a, rwkv_r_k, rwkv_ln_w, rwkv_ln_b, w_out0, w_in1, b_in1, attn_sinks, w_out1, b_out1, final_norm_w, loss_target, m_norm_w, m_w_in0, m_gla_gk_up, m_gla_gk_bias, m_gla_norm_w, m_rwkv_mu, m_rwkv_w0, m_rwkv_w_up, m_rwkv_a0, m_rwkv_a_up, m_rwkv_k_k, m_rwkv_k_a, m_rwkv_r_k, m_rwkv_ln_w, m_rwkv_ln_b, m_w_out0, m_w_in1, m_b_in1, m_attn_sinks, m_w_out1, m_b_out1, m_final_norm_w, v_norm_w, v_w_in0, v_gla_gk_up, v_gla_gk_bias, v_gla_norm_w, v_rwkv_mu, v_rwkv_w0, v_rwkv_w_up, v_rwkv_a0, v_rwkv_a_up, v_rwkv_k_k, v_rwkv_k_a, v_rwkv_r_k, v_rwkv_ln_w, v_rwkv_ln_b, v_w_out0, v_w_in1, v_b_in1, v_attn_sinks, v_w_out1, v_b_out1, v_final_norm_w):
    weights = dict(norm_w=norm_w, w_in0=w_in0, gla_gk_up=gla_gk_up, gla_gk_bias=gla_gk_bias, gla_norm_w=gla_norm_w, rwkv_mu=rwkv_mu,
                   rwkv_w0=rwkv_w0, rwkv_w_up=rwkv_w_up, rwkv_a0=rwkv_a0, rwkv_a_up=rwkv_a_up, rwkv_k_k=rwkv_k_k, rwkv_k_a=rwkv_k_a,
                   rwkv_r_k=rwkv_r_k, rwkv_ln_w=rwkv_ln_w, rwkv_ln_b=rwkv_ln_b, w_out0=w_out0, w_in1=w_in1, b_in1=b_in1,
                   attn_sinks=attn_sinks, w_out1=w_out1, b_out1=b_out1, final_norm_w=final_norm_w)
    moms = dict(norm_w=m_norm_w, w_in0=m_w_in0, gla_gk_up=m_gla_gk_up, gla_gk_bias=m_gla_gk_bias, gla_norm_w=m_gla_norm_w,
                rwkv_mu=m_rwkv_mu, rwkv_w0=m_rwkv_w0, rwkv_w_up=m_rwkv_w_up, rwkv_a0=m_rwkv_a0, rwkv_a_up=m_rwkv_a_up,
                rwkv_k_k=m_rwkv_k_k, rwkv_k_a=m_rwkv_k_a, rwkv_r_k=m_rwkv_r_k, rwkv_ln_w=m_rwkv_ln_w, rwkv_ln_b=m_rwkv_ln_b,
                w_out0=m_w_out0, w_in1=m_w_in1, b_in1=m_b_in1, attn_sinks=m_attn_sinks, w_out1=m_w_out1, b_out1=m_b_out1,
                final_norm_w=m_final_norm_w)
    vars_ = dict(norm_w=v_norm_w, w_in0=v_w_in0, gla_gk_up=v_gla_gk_up, gla_gk_bias=v_gla_gk_bias, gla_norm_w=v_gla_norm_w,
                 rwkv_mu=v_rwkv_mu, rwkv_w0=v_rwkv_w0, rwkv_w_up=v_rwkv_w_up, rwkv_a0=v_rwkv_a0, rwkv_a_up=v_rwkv_a_up,
                 rwkv_k_k=v_rwkv_k_k, rwkv_k_a=v_rwkv_k_a, rwkv_r_k=v_rwkv_r_k, rwkv_ln_w=v_rwkv_ln_w, rwkv_ln_b=v_rwkv_ln_b,
                 w_out0=v_w_out0, w_in1=v_w_in1, b_in1=v_b_in1, attn_sinks=v_attn_sinks, w_out1=v_w_out1, b_out1=v_b_out1,
                 final_norm_w=v_final_norm_w)
    names = list(weights)
    big = ["w_in0", "w_out0", "w_in1", "w_out1"]
    small_sharded = ["gla_gk_up", "rwkv_w_up", "rwkv_a_up", "b_in1", "b_out1"]
    replicated = [n for n in names if n not in big and n not in small_sharded]

    xs = x[0]
    tgt = loss_target[0]
    t = xs.shape[0]

    def view(w):
        shape = tuple(w.shape[-2:]) if w.ndim >= 2 else (1, w.shape[0])
        return shape, ("rows" if shape[0] > 1 and shape[1] <= LANES else "flat")

    def layout(ns, row0=0):
        specs = []
        for n in ns:
            shape, mode = view(weights[n])
            specs.append((shape, mode, row0))
            row0 += _packed_rows(shape, mode)
        return specs, row0

    sh_specs, n_shard_rows = layout(small_sharded)
    rep_specs, loss_row = layout(replicated, n_shard_rows)
    sh_modes, rep_modes = [s[1] for s in sh_specs], [s[1] for s in rep_specs]

    small_shard_pack = _pack_small([weights[n].reshape(view(weights[n])[0]) for n in small_sharded], sh_modes)
    g_in0, g_small = _all_gather([w_in0[0].T.astype(BF16), small_shard_pack])
    w0t = _w0t_to_padded(g_in0.reshape(-1, D_MODEL))
    later_shards = [w_out0[0].astype(BF16), w_in1[0].T.astype(BF16), w_out1[0].astype(BF16)]
    gs = [_take_small(g_small, row0, shape, mode) for shape, mode, row0 in sh_specs]
    join_cols = lambda a: jnp.transpose(a, (1, 0, 2)).reshape(a.shape[1], -1)
    gk_up, w_up, a_up = join_cols(gs[0]), join_cols(gs[1]), join_cols(gs[2])
    b_in, b_out = gs[3].reshape(1, -1), gs[4].reshape(1, -1)

    gk_up_p = _pad_to(gk_up, rows=LOW)
    w3, rank = 3 * RWKV_W, rwkv_w_up.shape[1]
    mu = rwkv_mu
    rwkv_params = [mu[:, 0:RWKV_W], mu[:, RWKV_W:2 * RWKV_W], mu[:, 2 * RWKV_W:w3], _pad_to(mu[:, w3:w3 + rank], cols=LOW),
                   _pad_to(mu[:, w3 + rank:], cols=LOW), rwkv_w0, _pad_to(w_up, rows=LOW), rwkv_a0, _pad_to(a_up, rows=LOW),
                   rwkv_k_k, rwkv_k_a, rwkv_r_k.reshape(1, RWKV_W), rwkv_ln_w, rwkv_ln_b]
    bq, bk, bv = b_in[:, :MIX], b_in[:, MIX:MIX + SWA_KV], b_in[:, MIX + SWA_KV:]
    cos, sin = _rope_tables(t)
    nw0, nw1, fw = norm_w[0:1], norm_w[1:2], final_norm_w.reshape(1, D_MODEL)

    d = D_MODEL
    wide = lambda arr: (arr, d, 0)
    silu = lambda g: g * sigmoid(g)
    hn0, proj0 = _matmul_fused("norm0_proj0", rms, w0t, "nt", [wide(xs)], [nw0], [(N0P, F32)], [], lambda acc, x, w: (acc,))
    o_a, gla_states = _gla_fwd(proj0, gk_up_p, gla_gk_bias, gla_norm_w)
    o_b, rwkv_states, rwkv_prevs, (g_out0, g_in1, g_out1) = _rwkv_fwd(proj0, rwkv_params, later_shards, ["gather"] * 3)
    wo0 = g_out0.reshape(MIX, D_MODEL)
    w1t = _w1t_to_mine(g_in1.reshape(-1, D_MODEL))
    wo1 = g_out1.reshape(MIX, D_MODEL)
    og0, h1, hn1 = _matmul_fused(
        "gate0_out0_norm1", lambda oa, ob, gate, x, w: jnp.concatenate([oa, ob], axis=1) * silu(gate), wo0, "nn",
        [(o_a, GLA_VAL, 0), (o_b, RWKV_W, 0), wide(proj0), wide(xs)], [nw1], [(d, F32), (d, BF16)], [],
        lambda acc, oa, ob, gate, x, w: _resid_norm(acc, x, w))
    proj1 = _matmul("proj1", hn1, w1t, "nt", PROJ_ROWS, N1P // 2)
    o_c, kst, vst = _swa_fwd(proj1, cos, sin, bq, bk, bv, attn_sinks)
    og1, dh2, loss_part, d_b_out, d_fw = _matmul_fused(
        "gate1_out1_loss", lambda oc, gate, h, tg, b, w: oc * silu(gate), wo1, "nn",
        [wide(o_c), wide(proj1), wide(h1), wide(tgt)], [b_out, fw], [(d, F32)], [LANES, d, d],
        lambda acc, oc, gate, h, tg, b, w: _loss_head(acc, h, tg, b, w))

    d_oc, d_gate1 = _matmul_fused("out1_dx_gate1", dh2, wo1, "nt", [wide(o_c), wide(proj1)], [], [(d, F32), (d, BF16)], [], _gate_back)
    d_wo1 = _matmul("out1_dw", og1, dh2, "tn", DW_COLS, DW_COLS, BF16)
    dq, dk, dv, d_bq, d_bk, d_bv, d_sinks = _swa_bwd(proj1, cos, sin, bq, bk, bv, attn_sinks, kst, vst, d_oc)
    dproj1 = jnp.concatenate([d_gate1, dq, dk, dv], axis=1)
    dh1, d_nw1 = _matmul_fused("proj1_dx_norm1", dproj1, w1t, "nn", [wide(h1), wide(dh2)], [nw1], [(d, F32)], [d], _norm_back)
    d_w1t = _matmul("proj1_dw", dproj1, hn1, "tn", DW_COLS, d, BF16)
    d_oa, d_ob, d_gate0 = _matmul_fused("out0_dx_gate0", dh1, wo0, "nt", [(o_a, GLA_VAL, 0), (o_b, RWKV_W, 0), wide(proj0)], [],
                                        [(GLA_VAL, F32), (RWKV_W, F32), (d, BF16)], [], _gate_back)
    d_wo0 = _matmul("out0_dw", og0, dh1, "tn", DW_COLS, DW_COLS, BF16)
    dgq, dgk, dgv, dglow, d_gk_up, d_gk_bias, d_gla_nw = _gla_bwd(proj0, gk_up_p, gla_gk_bias, gla_norm_w, gla_states, d_oa)
    row_blocks = lambda a: a.astype(BF16).reshape(N_DEV, -1, D_MODEL)
    early = [row_blocks(_w1t_from_mine(d_w1t)), row_blocks(d_wo1), row_blocks(d_wo0)]
    (dr, dkk, dvv, dxw, dxa), d_rp, (r_in1, r_out1, r_out0) = _rwkv_bwd(
        proj0, rwkv_params, rwkv_states, rwkv_prevs, d_ob, early, ["scatter"] * 3)
    pad = jnp.zeros((t, N0P - C0["xa"][0] - C0["xa"][1]), BF16)
    dproj0 = jnp.concatenate([d_gate0, dgv, dr, dkk, dvv, dgq, dgk, dglow, dxw, dxa, pad], axis=1)
    d_w0 = row_blocks(_w0t_from_padded(_matmul("proj0_dw", dproj0, hn0, "tn", DW_COLS, d, BF16)))
    r_in0 = _sequencer_scatter("exchange_w_in0_grad", d_w0, 0)
    grad_x, d_nw0 = _matmul_fused("proj0_dx_norm0", dproj0, w0t, "nn", [wide(xs), wide(dh1)], [nw0], [(d, F32)], [d], _norm_back)

    contrib = dict(
        norm_w=jnp.concatenate([d_nw0, d_nw1], axis=0), gla_gk_bias=d_gk_bias, gla_norm_w=d_gla_nw,
        rwkv_mu=jnp.concatenate([d_rp[0], d_rp[1], d_rp[2], d_rp[3][:, :rank], d_rp[4][:, :rank]], axis=1),
        rwkv_w0=d_rp[5], rwkv_a0=d_rp[7], rwkv_k_k=d_rp[9], rwkv_k_a=d_rp[10], rwkv_r_k=d_rp[11].reshape(RWKV_HEADS, RWKV_N),
        rwkv_ln_w=d_rp[12], rwkv_ln_b=d_rp[13], attn_sinks=d_sinks, final_norm_w=d_fw)
    rep_pack = _pack_small([contrib[n] for n in replicated] + [loss_part[:, :1]], rep_modes + ["flat"])

    d_b_in = jnp.concatenate([d_bq, d_bk, d_bv], axis=1)
    full_small = [d_gk_up[:gk_up.shape[0]], d_rp[6][:rank], d_rp[8][:rank], d_b_in, d_b_out]
    split_cols = lambda a: jnp.transpose(a.reshape(a.shape[0], N_DEV, -1), (1, 0, 2))
    small_parts = [split_cols(a) for a in full_small]
    small_pack = _pack_small(small_parts, sh_modes, lead=True)
    r_small, r_rep = _exchange([small_pack, rep_pack], ["scatter", "gather"])

    res = {}
    res["w_out0"] = tuple(a[None] for a in _adamw("adamw_w_out0", w_out0[0], r_out0, m_w_out0[0], v_w_out0[0], ADAM_COLS))
    res["w_in1"] = tuple(a.T[None] for a in _adamw("adamw_w_in1", w_in1[0].T, r_in1, m_w_in1[0].T, v_w_in1[0].T, ADAM_COLS))
    res["w_out1"] = tuple(a[None] for a in _adamw("adamw_w_out1", w_out1[0], r_out1, m_w_out1[0], v_w_out1[0], ADAM_COLS))
    small_names = small_sharded + replicated
    slots = jnp.concatenate([r_small, r_rep], axis=1)
    as_2d = lambda a: a.reshape(1, -1) if a.ndim == 1 else a
    small_res, loss_row_out = _adamw_small(slots, sh_specs + rep_specs, [as_2d(weights[n]) for n in small_names],
                                           [as_2d(moms[n]) for n in small_names], [as_2d(vars_[n]) for n in small_names], loss_row)
    for n, vals in zip(small_names, small_res):
        res[n] = tuple(val.reshape(weights[n].shape) for val in vals)
    loss = loss_row_out[0, 0]
    my_idx = 4 * lax.axis_index("x") + 2 * lax.axis_index("y") + lax.axis_index("c")
    r_in0 = lax.dynamic_update_slice(r_in0, lax.dynamic_slice(d_w0, (my_idx, 0, 0), (1,) + d_w0.shape[1:]), (my_idx, 0, 0))
    res["w_in0"] = tuple(a.T[None] for a in _adamw("adamw_w_in0", w_in0[0].T, r_in0, m_w_in0[0].T, v_w_in0[0].T, ADAM_COLS))
    return (loss, grad_x[None], *[res[n][0] for n in names], *[res[n][1] for n in names],
            *[res[n][2] for n in names], *[res[n][3] for n in names])
```

```python
import functools

import jax
import jax.numpy as jnp
from jax import lax
from jax.experimental import pallas as pl
from jax.experimental.pallas import tpu as pltpu
from jax.experimental.pallas import tpu_sc as plsc

F32 = jnp.float32
BF16 = jnp.bfloat16
HI = lax.Precision.HIGHEST

D_MODEL = 1024
NORM_EPS = 1e-5
GLA_HEADS, GLA_DK, GLA_DV = 4, 64, 128
GLA_NORMALIZER = 16.0
GLA_CHUNK = 64
GLA_STEP = 1024
RWKV_HEADS, RWKV_N = 8, 64
RWKV_LN_EPS = 64e-5
RWKV_CHUNK = 128
SWA_Q_HEADS, SWA_KV_HEADS, SWA_GROUP, SWA_HD = 16, 4, 4, 64
WINDOW = 128
SWA_STEP = 512
ROPE_THETA = 500000.0
NEG = -1e30
N_DEV = 8
LANES = 128

ADAM_LR, ADAM_B1, ADAM_B2, ADAM_EPS, ADAM_WD, ADAM_STEP = 0.001, 0.9, 0.999, 1e-08, 0.01, 10

GLA_KEY, GLA_VAL = GLA_HEADS * GLA_DK, GLA_HEADS * GLA_DV
RWKV_W = RWKV_HEADS * RWKV_N
SWA_KV = SWA_KV_HEADS * SWA_HD
MIX = GLA_VAL + RWKV_W
LOW = LANES

N0P = 4096
C0 = dict(gate=(0, MIX), gv=(1024, GLA_VAL), r=(1536, RWKV_W), k=(2048, RWKV_W), v=(2560, RWKV_W), gq=(3072, GLA_KEY),
          gk=(3328, GLA_KEY), glow=(3584, LOW), xw=(3712, LOW), xa=(3840, LOW))
N1P = 2560
C1 = dict(gate=(0, MIX), q=(1024, MIX), k=(2048, SWA_KV), v=(2304, SWA_KV))

VMEM_LIMIT = 56 * 1024 * 1024

P_LORA = 1
P_GLA = 1
P_RWKV_G = 2
P_RWKV = 1
P_SWA = 1


def _cparams(sem=None):
    return pltpu.CompilerParams(dimension_semantics=sem, vmem_limit_bytes=VMEM_LIMIT)


DIMS = dict(nn=(((1,), (0,)), ((), ())), nt=(((1,), (1,)), ((), ())), tn=(((0,), (0,)), ((), ())))


def _split_bf16(a):
    hi = a.astype(BF16)
    return hi, (a - hi.astype(F32)).astype(BF16)


def _dot(a, b, mode, passes):
    dg = lambda p, q: lax.dot_general(p, q, DIMS[mode], preferred_element_type=F32)
    if passes == 1:
        return dg(a.astype(BF16), b.astype(BF16))
    if passes == 2:
        ah, (bh, bl) = a.astype(BF16), _split_bf16(b)
        return dg(ah, bh) + dg(ah, bl)
    if passes == 3:
        (ah, al), (bh, bl) = _split_bf16(a), _split_bf16(b)
        return dg(ah, bh) + dg(al, bh) + dg(ah, bl)
    return lax.dot_general(a, b, DIMS[mode], precision=HI, preferred_element_type=F32)


@functools.partial(jax.custom_vjp, nondiff_argnums=(2, 3))
def mmx(a, b, mode, passes):
    return _dot(a, b, mode, passes)


def _mmx_fwd(a, b, mode, passes):
    return _dot(a, b, mode, passes), (a, b)


def _mmx_bwd(mode, passes, res, g):
    a, b = res
    if mode == "nn":
        return _dot(g, b, "nt", passes), _dot(a, g, "tn", passes)
    if mode == "nt":
        return _dot(g, b, "nn", passes), _dot(g, a, "tn", passes)
    return _dot(b, g, "nt", passes), _dot(a, g, "nn", passes)


mmx.defvjp(_mmx_fwd, _mmx_bwd)


def _tri_dot(tri, x):
    t = tri.astype(BF16)
    x1 = x.astype(BF16)
    r1 = x - x1.astype(F32)
    x2 = r1.astype(BF16)
    x3 = (r1 - x2.astype(F32)).astype(BF16)
    dg = lambda q: jnp.dot(t, q, preferred_element_type=F32)
    return dg(x1) + dg(x2) + dg(x3)


@jax.custom_vjp
def cumsum_rows(x):
    return _tri_dot(tril_ones(x.shape[0]), x)


def _cumsum_fwd(x):
    return cumsum_rows(x), None


def _cumsum_bwd(_, g):
    i, j = _iota2(g.shape[0], g.shape[0])
    return (_tri_dot(jnp.where(i <= j, 1.0, 0.0).astype(F32), g),)


cumsum_rows.defvjp(_cumsum_fwd, _cumsum_bwd)


def _head_dot(x):
    i, j = _iota2(LANES, LANES)
    shift = RWKV_N.bit_length() - 1
    same = jnp.where(jnp.right_shift(i, shift) == jnp.right_shift(j, shift), 1.0, 0.0).astype(F32)
    return jnp.concatenate([_ones_right(x[:, g * LANES:(g + 1) * LANES], same) for g in range(x.shape[1] // LANES)], axis=1)


def _ones_right(x, ones):
    t = ones.astype(BF16)
    x1 = x.astype(BF16)
    x2 = (x - x1.astype(F32)).astype(BF16)
    dg = lambda q: jnp.dot(q, t, preferred_element_type=F32)
    return dg(x1) + dg(x2)


@jax.custom_vjp
def head_sum(x):
    return _head_dot(x)


def _head_sum_fwd(x):
    return head_sum(x), None


def _head_sum_bwd(_, g):
    return (_head_dot(g),)


head_sum.defvjp(_head_sum_fwd, _head_sum_bwd)


def cat_rows(*xs):
    return jnp.concatenate(xs, axis=0)


def _iota2(n, m):
    return lax.broadcasted_iota(jnp.int32, (n, m), 0), lax.broadcasted_iota(jnp.int32, (n, m), 1)


def tril_ones(c, strict=False):
    i, j = _iota2(c, c)
    return jnp.where((i > j) if strict else (i >= j), 1.0, 0.0).astype(F32)


def row_of(x, r):
    i = lax.broadcasted_iota(jnp.int32, x.shape, 0)
    return jnp.sum(jnp.where(i == r, x, 0.0), axis=0, keepdims=True)


@jax.custom_vjp
def shift_rows(x, prev):
    r = lax.broadcasted_iota(jnp.int32, x.shape, 0)
    return jnp.where(r == 0, prev, pltpu.roll(x, 1, 0))


def _shift_fwd(x, prev):
    return shift_rows(x, prev), None


def _shift_bwd(_, g):
    c = g.shape[0]
    r = lax.broadcasted_iota(jnp.int32, g.shape, 0)
    return jnp.where(r == c - 1, 0.0, pltpu.roll(g, c - 1, 0)), row_of(g, 0)


shift_rows.defvjp(_shift_fwd, _shift_bwd)


def log_sigmoid(x):
    return jnp.minimum(x, 0.0) - jnp.log(1.0 + jnp.exp(-jnp.abs(x)))


def softplus(x):
    return jnp.maximum(x, 0.0) + jnp.log(1.0 + jnp.exp(-jnp.abs(x)))


def sigmoid(x):
    return 1.0 / (1.0 + jnp.exp(-x))


def rms(x, w, eps=NORM_EPS):
    return x * lax.rsqrt(jnp.mean(x * x, axis=-1, keepdims=True) + eps) * w


def gla_chunk(state, toks, params):
    q, k, v, glow = toks
    gk_up, bias, norm_w = params
    c = GLA_CHUNK
    subs, heads = range(glow.shape[0] // c), range(GLA_HEADS)
    rows = lambda x, j: x[j * c:(j + 1) * c]
    hk = lambda x, h: x[:, h * GLA_DK:(h + 1) * GLA_DK]
    hv = lambda x, h: x[:, h * GLA_DV:(h + 1) * GLA_DV]
    ltri = tril_ones(c)
    g = log_sigmoid(mmx(glow, gk_up, "nn", P_LORA) + bias) / GLA_NORMALIZER
    b = [cumsum_rows(rows(g, j)) for j in subs]
    ref = [lax.stop_gradient(row_of(b[j], c // 2)) for j in subs]
    last = [row_of(b[j], c - 1) for j in subs]
    ql = [rows(q, j) * (GLA_DK ** -0.5) * jnp.exp(b[j] - ref[j]) for j in subs]
    kr = [rows(k, j) * jnp.exp(ref[j] - b[j]) for j in subs]
    kl = [rows(k, j) * jnp.exp(last[j] - b[j]) for j in subs]
    vj = [rows(v, j) for j in subs]
    e_ref, e_last = [jnp.exp(x) for x in ref], [jnp.exp(x) for x in last]
    att = [[mmx(hk(ql[j], h), hk(kr[j], h), "nt", P_GLA) * ltri for h in heads] for j in subs]
    o_in = [[mmx(att[j][h], hv(vj[j], h), "nn", P_GLA) for h in heads] for j in subs]
    kv = [[mmx(hv(vj[j], h), hk(kl[j], h), "tn", P_GLA) for h in heads] for j in subs]
    o = []
    for j in subs:
        o.append([o_in[j][h] + mmx(hk(ql[j], h), state[h] * hk(e_ref[j], h), "nt", P_GLA) for h in heads])
        state = [state[h] * hk(e_last[j], h) + kv[j][h] for h in heads]
    o = [[x * lax.rsqrt(jnp.mean(x * x, axis=-1, keepdims=True) + NORM_EPS) * norm_w for x in oj] for oj in o]
    return cat_rows(*[jnp.concatenate(oj, axis=1) for oj in o]), state


SOLVE_BLOCK = 128


def solve_unit_lower(ps, ws):
    n = ps[0].shape[0]
    heads = range(len(ps))
    if n > SOLVE_BLOCK:
        half = n // 2
        top = solve_unit_lower([p[:half, :half] for p in ps], [w[:half] for w in ws])
        rest = [ws[h][half:] + mmx(ps[h][half:, :half], top[h], "nn", P_RWKV) for h in heads]
        bottom = solve_unit_lower([p[half:, half:] for p in ps], rest)
        return [cat_rows(top[h], bottom[h]) for h in heads]
    u, p = ws, ps
    levels = max(1, (n - 1).bit_length())
    for it in range(levels):
        if it + 1 < levels:
            y = [mmx(p[h], jnp.concatenate([p[h], u[h]], axis=1), "nn", P_RWKV) for h in heads]
            u = [u[h] + y[h][:, n:] for h in heads]
            p = [y[h][:, :n] for h in heads]
        else:
            u = [u[h] + mmx(p[h], u[h], "nn", P_RWKV) for h in heads]
    return u


def rwkv_chunk(state, toks, params):
    S, pr, pk, pv, pxw, pxa = state
    r_, k_, v_, xw_, xa_ = toks
    mu_r, mu_k, mu_v, mu_xw, mu_xa, w0, w_up, a0, a_up, k_k, k_a, r_k, ln_w, ln_b = params
    c, n = xw_.shape[0], RWKV_N
    heads = range(RWKV_HEADS)
    hs = lambda x, h: x[:, h * n:(h + 1) * n]
    ltri = tril_ones(c)
    stri = tril_ones(c, strict=True)

    def lerp(x, prev, mu):
        return x + (shift_rows(x, prev) - x) * mu

    xw = jnp.tanh(lerp(xw_, pxw, mu_xw))
    xa = lerp(xa_, pxa, mu_xa)
    r = lerp(r_, pr, mu_r)
    k = lerp(k_, pk, mu_k)
    v = lerp(v_, pv, mu_v)
    w = -softplus(-(w0 + mmx(xw, w_up, "nn", P_LORA))) - 0.5
    lw = -jnp.exp(w)
    asig = sigmoid(a0 + mmx(xa, a_up, "nn", P_LORA))
    kk = k * k_k
    kk = kk * lax.rsqrt(jnp.maximum(head_sum(kk * kk), 1e-24))
    k2 = k * (1.0 + (asig - 1.0) * k_a)
    b = kk * asig
    cum = cumsum_rows(lw)
    ref = lax.stop_gradient(row_of(cum, c // 2))
    last = row_of(cum, c - 1)
    at = -kk * jnp.exp(cum - lw - ref)
    rt = r * jnp.exp(cum - ref)
    e_out = jnp.exp(ref - cum)
    bt, kt = b * e_out, k2 * e_out
    e_tail = jnp.exp(last - cum)
    bl, kl = b * e_tail, k2 * e_tail
    e_ref, e_last = jnp.exp(ref), jnp.exp(last)
    g = [mmx(cat_rows(hs(at, h), hs(rt, h)), cat_rows(hs(bt, h), hs(kt, h), S[h] * hs(e_ref, h)), "nt", P_RWKV_G) for h in heads]
    aab = [x[:c, :c] * stri for x in g]
    aak = [x[:c, c:2 * c] * stri for x in g]
    arb = [x[c:, :c] * ltri for x in g]
    ark = [x[c:, c:2 * c] * ltri for x in g]
    av = [mmx(cat_rows(aak[h], ark[h]), hs(v, h), "nn", P_RWKV) for h in heads]
    u = solve_unit_lower(aab, [g[h][:c, 2 * c:] + av[h][:c] for h in heads])
    o = [g[h][c:, 2 * c:] + av[h][c:] + mmx(arb[h], u[h], "nn", P_RWKV) for h in heads]
    s1 = [S[h] * hs(e_last, h) + mmx(cat_rows(u[h], hs(v, h)), cat_rows(hs(bl, h), hs(kl, h)), "tn", P_RWKV) for h in heads]
    o = jnp.concatenate(o, axis=1)
    d = o - head_sum(o) * (1.0 / n)
    var = head_sum(d * d) * (1.0 / n)
    o = d * lax.rsqrt(var + RWKV_LN_EPS) * ln_w + ln_b + head_sum(r * k2 * r_k) * v
    new_state = (s1, row_of(r_, c - 1), row_of(k_, c - 1), row_of(v_, c - 1), row_of(xw_, c - 1), row_of(xa_, c - 1))
    return o, new_state


RWKV_STEP = 256


def rwkv_chunks(state, toks, params):
    outs = []
    for j in range(toks[3].shape[0] // RWKV_CHUNK):
        rows = slice(j * RWKV_CHUNK, (j + 1) * RWKV_CHUNK)
        o, state = rwkv_chunk(state, tuple(t[rows] for t in toks), params)
        outs.append(o)
    return cat_rows(*outs), state


ROPE_HALF = 8


def _rot_half_raw(x):
    lane = lax.broadcasted_iota(jnp.int32, (x.shape[0], LANES), 1) & (SWA_HD - 1)
    out = []
    for i in range(x.shape[1] // LANES):
        g = x[:, i * LANES:(i + 1) * LANES]
        up, down = pltpu.roll(g, LANES - ROPE_HALF, 1), pltpu.roll(g, ROPE_HALF, 1)
        out.append(jnp.where(lane < ROPE_HALF, -up, jnp.where(lane < 2 * ROPE_HALF, down, 0.0)))
    return out[0] if len(out) == 1 else jnp.concatenate(out, axis=1)


@jax.custom_vjp
def rot_half(x):
    return _rot_half_raw(x)


rot_half.defvjp(lambda x: (_rot_half_raw(x), None), lambda _, g: (-_rot_half_raw(g),))


def rope(x, cos2, sin2):
    reps = x.shape[1] // LANES
    tile = lambda t: t if reps == 1 else jnp.concatenate([t] * reps, axis=1)
    return x * tile(cos2) + rot_half(x) * tile(sin2)


def swa_chunk(state, toks, params, first):
    kprev, vprev = state
    q_, k_, v_, cos, sin = toks
    bq, bk, bv, sinks = params
    c, ng = WINDOW, SWA_GROUP
    n_sub = cos.shape[0] // c
    units = [(j, g) for j in range(n_sub) for g in range(SWA_KV_HEADS)]
    rows = lambda x, j: x[j * c:(j + 1) * c]
    hs = lambda g: range(g * ng, (g + 1) * ng)
    head = lambda x, h: x[:, h * SWA_HD:(h + 1) * SWA_HD]
    qi, kj = _iota2(ng * c, 2 * c)
    qpos = qi & (c - 1)
    cur_ok = (kj >= c) & (qpos >= kj - c)
    prev_ok = (kj < c) & (kj > qpos)
    ok = [cur_ok | (prev_ok & jnp.logical_not(first))] + [cur_ok | prev_ok] * (n_sub - 1)
    q_all = rope(q_ + bq, cos, sin) * (SWA_HD ** -0.5)
    k_all = rope(k_ + bk, cos, sin)
    v_all = v_ + bv
    k = {(j, g): rows(head(k_all, g), j) for j, g in units}
    v = {(j, g): rows(head(v_all, g), j) for j, g in units}
    q = {(j, g): cat_rows(*[rows(head(q_all, h), j) for h in hs(g)]) for j, g in units}
    kp = lambda j, g: kprev[g] if j == 0 else k[(j - 1, g)]
    vp = lambda j, g: vprev[g] if j == 0 else v[(j - 1, g)]
    s = {(j, g): jnp.where(ok[j], mmx(q[(j, g)], cat_rows(kp(j, g), k[(j, g)]), "nt", P_SWA), NEG) for j, g in units}
    sink = [cat_rows(*[jnp.broadcast_to(sinks[h], (c, 1)) for h in hs(g)]) for g in range(SWA_KV_HEADS)]
    m = {(j, g): lax.stop_gradient(jnp.maximum(jnp.max(s[(j, g)], axis=-1, keepdims=True), sink[g])) for j, g in units}
    p = {u: jnp.exp(s[u] - m[u]) for u in units}
    ones = jnp.ones((2 * c, SWA_HD), F32)
    pv = {(j, g): mmx(p[(j, g)], cat_rows(vp(j, g), v[(j, g)]), "nn", P_SWA) for j, g in units}
    den = {u: mmx(p[u], ones, "nn", P_SWA) for u in units}
    o = {(j, g): pv[(j, g)] / (den[(j, g)] + jnp.exp(sink[g] - m[(j, g)])) for j, g in units}
    outs = [cat_rows(*[o[(j, g)][i * c:(i + 1) * c] for j in range(n_sub)]) for g in range(SWA_KV_HEADS) for i in range(ng)]
    last = n_sub - 1
    return outs, ([k[(last, g)] for g in range(SWA_KV_HEADS)], [v[(last, g)] for g in range(SWA_KV_HEADS)])


def _heads(ref, n, w, rows=slice(None)):
    return [ref[rows, h * w:(h + 1) * w] for h in range(n)]


def _put_heads(ref, vals, w, rows=slice(None), add=False):
    for h, val in enumerate(vals):
        if add:
            ref[rows, h * w:(h + 1) * w] += val
        else:
            ref[rows, h * w:(h + 1) * w] = val


def _col(block_w, name, table):
    off, w = table[name]
    assert off % block_w == 0 and w % block_w == 0
    return off // block_w


def _tok_spec(c, w, colblock, n=None):
    if n is None:
        return pl.BlockSpec((c, w), lambda i: (i, colblock))
    return pl.BlockSpec((c, w), lambda i: (n - 1 - i, colblock))


def _full_spec(shape):
    return pl.BlockSpec(shape, lambda i: (0,) * len(shape))


def _matmul(name, a, b, mode, tm, tn, out_dtype=F32):
    (m, kd) = (a.shape[1], a.shape[0]) if mode == "tn" else a.shape
    n = b.shape[0] if mode == "nt" else b.shape[1]
    assert m % tm == 0 and n % tn == 0
    a_spec = pl.BlockSpec((kd, tm), lambda j, i: (0, i)) if mode == "tn" else pl.BlockSpec((tm, kd), lambda j, i: (i, 0))
    b_spec = pl.BlockSpec((tn, kd), lambda j, i: (j, 0)) if mode == "nt" else pl.BlockSpec((kd, tn), lambda j, i: (0, j))

    def body(a_ref, b_ref, o_ref):
        o_ref[...] = lax.dot_general(a_ref[...].astype(BF16), b_ref[...].astype(BF16), DIMS[mode],
                                     preferred_element_type=F32).astype(out_dtype)

    return pl.pallas_call(
        body, name=name, grid=(n // tn, m // tm), in_specs=[a_spec, b_spec],
        out_specs=pl.BlockSpec((tm, tn), lambda j, i: (i, j)), out_shape=jax.ShapeDtypeStruct((m, n), out_dtype),
        compiler_params=_cparams(("arbitrary", "arbitrary")))(a, b)


TOK_TILE = 512
PROJ_ROWS = 1024
DW_COLS = 512
ADAM_COLS = 256


def _matmul_fused(name, a, b, mode, tiles, rows, outs, sums, epilogue, comm=(), kinds=()):
    made = callable(a)
    m = tiles[0][0].shape[0] if made else a.shape[0]
    kd = b.shape[0] if mode == "nn" else b.shape[1]
    n = b.shape[1] if mode == "nn" else b.shape[0]
    tm = TOK_TILE
    steps = m // tm
    if made:
        outs = [(kd, BF16)] + list(outs)
    nt_, nr, no, ns, ncomm = len(tiles), len(rows), len(outs), len(sums), len(comm)

    def body(*refs):
        at = 1 if made else 2
        b_ref = refs[at - 1]
        tile_refs, row_refs, comm_in = refs[at:at + nt_], refs[at + nt_:at + nt_ + nr], refs[at + nt_ + nr:at + nt_ + nr + ncomm]
        at += nt_ + nr + ncomm
        out_refs, sum_refs, comm_out = refs[at:at + no], refs[at + no:at + no + ns], refs[at + no + ns:at + no + ns + ncomm]
        sems = refs[at + no + ns + ncomm:]
        i = pl.program_id(0)

        @pl.when(i == 0)
        def _():
            if ncomm:
                _comm_start(*_comm_copies(comm_in, comm_out, kinds, *sems))
            for ref in sum_refs:
                ref[...] = jnp.zeros_like(ref)

        extras = [r[...] for r in tile_refs] + [r[...] for r in row_refs]
        a_blk = (a(*extras) if made else refs[0][...]).astype(BF16)
        acc = lax.dot_general(a_blk, b_ref[...].astype(BF16), DIMS[mode], preferred_element_type=F32)
        res = epilogue(acc, *extras)
        if made:
            res = (a_blk,) + tuple(res)
        for ref, val in zip(out_refs, res[:no]):
            ref[...] = val.astype(ref.dtype)
        for ref, val in zip(sum_refs, res[no:]):
            ref[...] += val

        if ncomm:
            @pl.when(i == steps - 1)
            def _():
                _comm_wait(*_comm_copies(comm_in, comm_out, kinds, *sems))

    in_specs = ([] if made else [pl.BlockSpec((tm, kd), lambda i: (i, 0))]) + [_full_spec(b.shape)]
    in_specs += [pl.BlockSpec((tm, w), functools.partial(lambda i, cb: (i, cb), cb=cb)) for _, w, cb in tiles]
    in_specs += [_full_spec(r.shape) for r in rows] + [ANY] * ncomm
    out_specs = [pl.BlockSpec((tm, w), lambda i: (i, 0)) for w, _ in outs] + [_full_spec((1, w)) for w in sums] + [ANY] * ncomm
    out_shape = ([jax.ShapeDtypeStruct((m, w), dt) for w, dt in outs] + [jax.ShapeDtypeStruct((1, w), F32) for w in sums]
                 + _comm_out_shapes(comm, kinds))
    return pl.pallas_call(body, name=name, grid=(steps,), in_specs=in_specs, out_specs=out_specs, out_shape=out_shape,
                          scratch_shapes=_comm_scratch(ncomm) if ncomm else [],
                          compiler_params=_cparams(("arbitrary",)))(*([] if made else [a]), b, *[t[0] for t in tiles], *rows, *comm)


def _resid_norm(y, x, w):
    h = x + y
    return h, rms(h, w)


def _norm_back(dhn, h, dres, w):
    _, vjp = jax.vjp(rms, h, w)
    dh, dw = vjp(dhn)
    return dh + dres, dw


def _gate_back(dog, *o_and_gate):
    outs, g = o_and_gate[:-1], o_and_gate[-1]
    s = sigmoid(g)
    silu, dsilu = g * s, s * (1.0 + g * (1.0 - s))
    d_outs, c = [], 0
    for o in outs:
        w = o.shape[1]
        d_outs.append(dog[:, c:c + w] * silu[:, c:c + w])
        c += w
    o_all = outs[0] if len(outs) == 1 else jnp.concatenate(outs, axis=1)
    return (*d_outs, dog * o_all * dsilu)


def _loss_head(y1, h1, target, b_out, fw):
    def f(h2, w):
        err = rms(h2, w) - target
        return 0.5 * jnp.sum(jnp.mean(err * err, axis=-1, keepdims=True), axis=0, keepdims=True)

    loss, vjp = jax.vjp(f, h1 + y1 + b_out, fw)
    dh2, dfw = vjp(jnp.ones((1, 1), F32))
    return dh2, jnp.broadcast_to(loss, (1, LANES)), jnp.sum(dh2, axis=0, keepdims=True), dfw


def _gla_load(q_ref, k_ref, v_ref, gl_ref, up_ref, bias_ref, nw_ref):
    toks = (q_ref[...], k_ref[...], v_ref[...], gl_ref[...])
    params = (up_ref[...], bias_ref[...], nw_ref[...])
    return toks, params


def _gla_specs(c, n=None):
    toks = [_tok_spec(c, GLA_KEY, _col(GLA_KEY, "gq", C0), n), _tok_spec(c, GLA_KEY, _col(GLA_KEY, "gk", C0), n),
            _tok_spec(c, GLA_VAL, _col(GLA_VAL, "gv", C0), n), _tok_spec(c, LOW, _col(LOW, "glow", C0), n)]
    return toks, [_full_spec(s) for s in GLA_PARAM_SHAPES]


GLA_PARAM_SHAPES = [(LOW, GLA_KEY), (1, GLA_KEY), (1, GLA_DV)]
GLA_STATE = (GLA_HEADS * GLA_DV, GLA_DK)


def _gla_fwd(proj0, gk_up, gk_bias, norm_w):
    t = proj0.shape[0]
    c = GLA_STEP
    nc = t // c
    toks_s, params_s = _gla_specs(c)

    def body(q_ref, k_ref, v_ref, gl_ref, up_ref, bias_ref, nw_ref, o_ref, st_ref, s_scr):
        @pl.when(pl.program_id(0) == 0)
        def _():
            s_scr[...] = jnp.zeros_like(s_scr)

        st_ref[...] = s_scr[...]
        toks, params = _gla_load(q_ref, k_ref, v_ref, gl_ref, up_ref, bias_ref, nw_ref)
        state = [s_scr[h * GLA_DV:(h + 1) * GLA_DV, :] for h in range(GLA_HEADS)]
        o_ref[...], new = gla_chunk(state, toks, params)
        for h in range(GLA_HEADS):
            s_scr[h * GLA_DV:(h + 1) * GLA_DV, :] = new[h]

    return pl.pallas_call(
        body, name="gla_fwd", grid=(nc,), in_specs=toks_s + params_s,
        out_specs=(_tok_spec(c, GLA_VAL, 0), pl.BlockSpec(GLA_STATE, lambda i: (i, 0))),
        out_shape=(jax.ShapeDtypeStruct((t, GLA_VAL), F32), jax.ShapeDtypeStruct((nc * GLA_STATE[0], GLA_DK), F32)),
        scratch_shapes=[pltpu.VMEM(GLA_STATE, F32)], compiler_params=_cparams(("arbitrary",)))(
            proj0, proj0, proj0, proj0, gk_up, gk_bias, norm_w)


def _gla_bwd(proj0, gk_up, gk_bias, norm_w, states, do):
    t = proj0.shape[0]
    c = GLA_STEP
    nc = t // c
    toks_s, params_s = _gla_specs(c, nc)

    def body(q_ref, k_ref, v_ref, gl_ref, up_ref, bias_ref, nw_ref, st_ref, do_ref,
             dq_ref, dk_ref, dv_ref, dgl_ref, dup_ref, dbias_ref, dnw_ref, ds_scr):
        @pl.when(pl.program_id(0) == 0)
        def _():
            ds_scr[...] = jnp.zeros_like(ds_scr)
            dup_ref[...] = jnp.zeros_like(dup_ref)
            dbias_ref[...] = jnp.zeros_like(dbias_ref)
            dnw_ref[...] = jnp.zeros_like(dnw_ref)

        toks, params = _gla_load(q_ref, k_ref, v_ref, gl_ref, up_ref, bias_ref, nw_ref)
        rows = lambda h: slice(h * GLA_DV, (h + 1) * GLA_DV)
        state = [st_ref[rows(h), :] for h in range(GLA_HEADS)]
        _, vjp = jax.vjp(gla_chunk, state, toks, params)
        dstate_in = [ds_scr[rows(h), :] for h in range(GLA_HEADS)]
        dstate, dtoks, (dup, dbias, dnw) = vjp((do_ref[...], dstate_in))
        for ref, val in zip((dq_ref, dk_ref, dv_ref, dgl_ref), dtoks):
            ref[...] = val.astype(ref.dtype)
        dup_ref[...] += dup
        dbias_ref[...] += dbias
        dnw_ref[...] += dnw
        for h in range(GLA_HEADS):
            ds_scr[rows(h), :] = dstate[h]

    rev = lambda w: pl.BlockSpec((c, w), lambda i: (nc - 1 - i, 0))
    tok_widths = (GLA_KEY, GLA_KEY, GLA_VAL, LOW)
    return pl.pallas_call(
        body, name="gla_bwd", grid=(nc,),
        in_specs=toks_s + params_s + [pl.BlockSpec(GLA_STATE, lambda i: (nc - 1 - i, 0)), rev(GLA_VAL)],
        out_specs=[rev(w) for w in tok_widths] + params_s,
        out_shape=[jax.ShapeDtypeStruct((t, w), BF16) for w in tok_widths] + [jax.ShapeDtypeStruct(s, F32) for s in GLA_PARAM_SHAPES],
        scratch_shapes=[pltpu.VMEM(GLA_STATE, F32)], compiler_params=_cparams(("arbitrary",)))(
            proj0, proj0, proj0, proj0, gk_up, gk_bias, norm_w, states, do)


RWKV_PARAM_SHAPES = [(1, RWKV_W), (1, RWKV_W), (1, RWKV_W), (1, LOW), (1, LOW), (1, RWKV_W), (LOW, RWKV_W), (1, RWKV_W),
                     (LOW, RWKV_W), (1, RWKV_W), (1, RWKV_W), (1, RWKV_W), (1, RWKV_W), (1, RWKV_W)]
RWKV_STATE = (RWKV_HEADS * RWKV_N, RWKV_N)
RWKV_TOK_WIDTHS = (RWKV_W, RWKV_W, RWKV_W, LOW, LOW)
PREV_W = sum(RWKV_TOK_WIDTHS)
PREV_COLS = [slice(sum(RWKV_TOK_WIDTHS[:i]), sum(RWKV_TOK_WIDTHS[:i + 1])) for i in range(len(RWKV_TOK_WIDTHS))]


def _rwkv_load(r_ref, k_ref, v_ref, xw_ref, xa_ref, p_refs):
    toks = (r_ref[...], k_ref[...], v_ref[...], xw_ref[...], xa_ref[...])
    return toks, tuple(p[...] for p in p_refs)


def _rwkv_state(s_ref, prev_ref):
    n = RWKV_N
    S = [s_ref[h * n:(h + 1) * n, :] for h in range(RWKV_HEADS)]
    return (S,) + tuple(prev_ref[0:1, cols] for cols in PREV_COLS)


def _rwkv_put_state(s_ref, prev_ref, state):
    n = RWKV_N
    for h in range(RWKV_HEADS):
        s_ref[h * n:(h + 1) * n, :] = state[0][h]
    for cols, val in zip(PREV_COLS, state[1:]):
        prev_ref[0:1, cols] = val


def _rwkv_specs(c, n=None):
    toks = [_tok_spec(c, w, _col(w, name, C0), n) for name, w in zip(("r", "k", "v", "xw", "xa"), RWKV_TOK_WIDTHS)]
    return toks, [_full_spec(s) for s in RWKV_PARAM_SHAPES]


def _rwkv_fwd(proj0, params, comm, kinds):
    t = proj0.shape[0]
    c = RWKV_STEP
    nc = t // c
    toks_s, params_s = _rwkv_specs(c)
    npar, ncomm = len(params), len(comm)

    def body(*refs):
        tok_refs, p_refs = refs[:5], refs[5:5 + npar]
        comm_in = refs[5 + npar:5 + npar + ncomm]
        o_ref, st_ref, pst_ref = refs[5 + npar + ncomm:8 + npar + ncomm]
        comm_out = refs[8 + npar + ncomm:8 + npar + 2 * ncomm]
        s_scr, prev_scr = refs[8 + npar + 2 * ncomm:10 + npar + 2 * ncomm]
        sems = refs[10 + npar + 2 * ncomm:]
        i = pl.program_id(0)

        @pl.when(i == 0)
        def _():
            _comm_start(*_comm_copies(comm_in, comm_out, kinds, *sems))
            s_scr[...] = jnp.zeros_like(s_scr)
            prev_scr[...] = jnp.zeros_like(prev_scr)

        st_ref[...] = s_scr[...]
        pst_ref[...] = prev_scr[...]
        toks, prm = _rwkv_load(*tok_refs, p_refs)
        o_ref[...], new = rwkv_chunks(_rwkv_state(s_scr, prev_scr), toks, prm)
        _rwkv_put_state(s_scr, prev_scr, new)

        @pl.when(i == nc - 1)
        def _():
            _comm_wait(*_comm_copies(comm_in, comm_out, kinds, *sems))

    outs = pl.pallas_call(
        body, name="rwkv_fwd", grid=(nc,), in_specs=toks_s + params_s + [ANY] * ncomm,
        out_specs=[_tok_spec(c, RWKV_W, 0), pl.BlockSpec(RWKV_STATE, lambda i: (i, 0)), pl.BlockSpec((8, PREV_W), lambda i: (i, 0))]
        + [ANY] * ncomm,
        out_shape=[jax.ShapeDtypeStruct((t, RWKV_W), F32), jax.ShapeDtypeStruct((nc * RWKV_STATE[0], RWKV_N), F32),
                   jax.ShapeDtypeStruct((nc * 8, PREV_W), F32)] + _comm_out_shapes(comm, kinds),
        scratch_shapes=[pltpu.VMEM(RWKV_STATE, F32), pltpu.VMEM((8, PREV_W), F32)] + _comm_scratch(ncomm),
        compiler_params=_cparams(("arbitrary",)))(proj0, proj0, proj0, proj0, proj0, *params, *comm)
    return outs[0], outs[1], outs[2], outs[3:]


def _rwkv_bwd(proj0, params, states, prevs, do, comm, kinds):
    t = proj0.shape[0]
    c = RWKV_STEP
    nc = t // c
    toks_s, params_s = _rwkv_specs(c, nc)
    npar, ncomm = len(params), len(comm)

    def body(*refs):
        tok_refs, p_refs = refs[:5], refs[5:5 + npar]
        st_ref, pst_ref, do_ref = refs[5 + npar:8 + npar]
        comm_in = refs[8 + npar:8 + npar + ncomm]
        outs = refs[8 + npar + ncomm:]
        dtok_refs, dp_refs, comm_out = outs[:5], outs[5:5 + npar], outs[5 + npar:5 + npar + ncomm]
        ds_scr, dprev_scr = outs[5 + npar + ncomm:7 + npar + ncomm]
        sems = outs[7 + npar + ncomm:]
        i = pl.program_id(0)

        @pl.when(i == 0)
        def _():
            _comm_start(*_comm_copies(comm_in, comm_out, kinds, *sems))
            ds_scr[...] = jnp.zeros_like(ds_scr)
            dprev_scr[...] = jnp.zeros_like(dprev_scr)
            for dp in dp_refs:
                dp[...] = jnp.zeros_like(dp)

        toks, prm = _rwkv_load(*tok_refs, p_refs)
        _, vjp = jax.vjp(rwkv_chunks, _rwkv_state(st_ref, pst_ref), toks, prm)
        dstate, dtoks, dprm = vjp((do_ref[...], _rwkv_state(ds_scr, dprev_scr)))
        for ref, val in zip(dtok_refs, dtoks):
            ref[...] = val.astype(ref.dtype)
        for ref, val in zip(dp_refs, dprm):
            ref[...] += val
        _rwkv_put_state(ds_scr, dprev_scr, dstate)

        @pl.when(i == nc - 1)
        def _():
            _comm_wait(*_comm_copies(comm_in, comm_out, kinds, *sems))

    rev = lambda w: pl.BlockSpec((c, w), lambda i: (nc - 1 - i, 0))
    outs = pl.pallas_call(
        body, name="rwkv_bwd", grid=(nc,),
        in_specs=toks_s + params_s + [pl.BlockSpec(RWKV_STATE, lambda i: (nc - 1 - i, 0)),
                                      pl.BlockSpec((8, PREV_W), lambda i: (nc - 1 - i, 0)), rev(RWKV_W)] + [ANY] * ncomm,
        out_specs=[rev(w) for w in RWKV_TOK_WIDTHS] + params_s + [ANY] * ncomm,
        out_shape=[jax.ShapeDtypeStruct((t, w), BF16) for w in RWKV_TOK_WIDTHS]
        + [jax.ShapeDtypeStruct(s, F32) for s in RWKV_PARAM_SHAPES] + _comm_out_shapes(comm, kinds),
        scratch_shapes=[pltpu.VMEM(RWKV_STATE, F32), pltpu.VMEM((8, PREV_W), F32)] + _comm_scratch(ncomm),
        compiler_params=_cparams(("arbitrary",)))(proj0, proj0, proj0, proj0, proj0, *params, states, prevs, do, *comm)
    return outs[:5], outs[5:5 + npar], outs[5 + npar:]


def _swa_load(q_ref, k_ref, v_ref, cos_ref, sin_ref, bq_ref, bk_ref, bv_ref, sk_ref):
    toks = (q_ref[...], k_ref[...], v_ref[...], cos_ref[...], sin_ref[...])
    params = (bq_ref[...], bk_ref[...], bv_ref[...], _heads(sk_ref, SWA_Q_HEADS, 1))
    return toks, params


SWA_TOK_WIDTHS = (MIX, SWA_KV, SWA_KV)
SWA_PARAM_SHAPES = [(1, MIX), (1, SWA_KV), (1, SWA_KV), (1, SWA_Q_HEADS)]
SWA_STATE = (WINDOW, SWA_KV)


def _swa_specs(c, n=None):
    toks = [_tok_spec(c, w, _col(w, name, C1), n) for name, w in zip(("q", "k", "v"), SWA_TOK_WIDTHS)]
    toks += [_tok_spec(c, LANES, 0, n), _tok_spec(c, LANES, 0, n)]
    return toks, [_full_spec(s) for s in SWA_PARAM_SHAPES]


def _swa_fwd(proj1, cos, sin, bq, bk, bv, sinks):
    t = proj1.shape[0]
    c = SWA_STEP
    nb = t // c
    toks_s, params_s = _swa_specs(c)
    state_spec = pl.BlockSpec(SWA_STATE, lambda i: (i, 0))
    kv = SWA_KV_HEADS

    def body(q_ref, k_ref, v_ref, cos_ref, sin_ref, bq_ref, bk_ref, bv_ref, sk_ref, o_ref, kst_ref, vst_ref, k_scr, v_scr):
        first = pl.program_id(0) == 0

        @pl.when(first)
        def _():
            k_scr[...] = jnp.zeros_like(k_scr)
            v_scr[...] = jnp.zeros_like(v_scr)

        kst_ref[...] = k_scr[...]
        vst_ref[...] = v_scr[...]
        toks, params = _swa_load(q_ref, k_ref, v_ref, cos_ref, sin_ref, bq_ref, bk_ref, bv_ref, sk_ref)
        outs, (kn, vn) = swa_chunk((_heads(k_scr, kv, SWA_HD), _heads(v_scr, kv, SWA_HD)), toks, params, first)
        _put_heads(o_ref, outs, SWA_HD)
        _put_heads(k_scr, kn, SWA_HD)
        _put_heads(v_scr, vn, SWA_HD)

    saved = jax.ShapeDtypeStruct((nb * WINDOW, SWA_KV), F32)
    return pl.pallas_call(
        body, name="swa_fwd", grid=(nb,), in_specs=toks_s + params_s,
        out_specs=(_tok_spec(c, MIX, 0), state_spec, state_spec),
        out_shape=(jax.ShapeDtypeStruct((t, MIX), F32), saved, saved),
        scratch_shapes=[pltpu.VMEM(SWA_STATE, F32), pltpu.VMEM(SWA_STATE, F32)],
        compiler_params=_cparams(("arbitrary",)))(proj1, proj1, proj1, cos, sin, bq, bk, bv, sinks)


def _swa_bwd(proj1, cos, sin, bq, bk, bv, sinks, kst, vst, do):
    t = proj1.shape[0]
    c = SWA_STEP
    nb = t // c
    toks_s, params_s = _swa_specs(c, nb)
    state_spec = pl.BlockSpec(SWA_STATE, lambda i: (nb - 1 - i, 0))
    kv = SWA_KV_HEADS

    def body(q_ref, k_ref, v_ref, cos_ref, sin_ref, bq_ref, bk_ref, bv_ref, sk_ref, kst_ref, vst_ref, do_ref,
             dq_ref, dk_ref, dv_ref, dbq_ref, dbk_ref, dbv_ref, dsk_ref, dk_scr, dv_scr):
        i = pl.program_id(0)

        @pl.when(i == 0)
        def _():
            dk_scr[...] = jnp.zeros_like(dk_scr)
            dv_scr[...] = jnp.zeros_like(dv_scr)
            for ref in (dbq_ref, dbk_ref, dbv_ref, dsk_ref):
                ref[...] = jnp.zeros_like(ref)

        first = i == nb - 1
        toks, params = _swa_load(q_ref, k_ref, v_ref, cos_ref, sin_ref, bq_ref, bk_ref, bv_ref, sk_ref)
        f = functools.partial(swa_chunk, first=first)
        _, vjp = jax.vjp(f, (_heads(kst_ref, kv, SWA_HD), _heads(vst_ref, kv, SWA_HD)), toks, params)
        dstate_in = (_heads(dk_scr, kv, SWA_HD), _heads(dv_scr, kv, SWA_HD))
        (dkp, dvp), (dq, dk, dv, _, _), (dbq, dbk, dbv, dsk) = vjp((_heads(do_ref, SWA_Q_HEADS, SWA_HD), dstate_in))
        dq_ref[...], dk_ref[...], dv_ref[...] = dq.astype(BF16), dk.astype(BF16), dv.astype(BF16)
        dbq_ref[...] += dbq
        dbk_ref[...] += dbk
        dbv_ref[...] += dbv
        _put_heads(dsk_ref, dsk, 1, add=True)
        _put_heads(dk_scr, dkp, SWA_HD)
        _put_heads(dv_scr, dvp, SWA_HD)

    rev = lambda w: pl.BlockSpec((c, w), lambda i: (nb - 1 - i, 0))
    return pl.pallas_call(
        body, name="swa_bwd", grid=(nb,), in_specs=toks_s + params_s + [state_spec, state_spec, rev(MIX)],
        out_specs=[rev(w) for w in SWA_TOK_WIDTHS] + params_s,
        out_shape=[jax.ShapeDtypeStruct((t, w), BF16) for w in SWA_TOK_WIDTHS] + [jax.ShapeDtypeStruct(s, F32) for s in SWA_PARAM_SHAPES],
        scratch_shapes=[pltpu.VMEM(SWA_STATE, F32), pltpu.VMEM(SWA_STATE, F32)],
        compiler_params=_cparams(("arbitrary",)))(proj1, proj1, proj1, cos, sin, bq, bk, bv, sinks, kst, vst, do)


MESH = pl.DeviceIdType.MESH
ANY = pl.BlockSpec(memory_space=pl.ANY)


def _my_place():
    return lax.axis_index("x"), lax.axis_index("y"), lax.axis_index("c")


def _all_gather(shards):
    n = len(shards)

    def body(*refs):
        in_refs, out_refs = refs[:n], refs[n:2 * n]
        send_sems, recv_sems, local_sems = refs[2 * n:]
        x, y, c = _my_place()
        me, sibling = (x, y, c), (x, y, 1 - c)
        chips = [(1 - x, y), (x, 1 - y), (1 - x, 1 - y)]

        def slot(out_ref, place):
            px, py, pc = place
            return out_ref.at[4 * px + 2 * py + pc]

        def copy(a, k, block, to, src=None):
            return pltpu.make_async_remote_copy(
                src_ref=slot(out_refs[a], block) if src is None else src, dst_ref=slot(out_refs[a], block),
                send_sem=send_sems.at[a, k], recv_sem=recv_sems.at[a, k], device_id=to, device_id_type=MESH)

        mine = [pltpu.make_async_copy(in_refs[a], slot(out_refs[a], me), local_sems.at[a]) for a in range(n)]
        for cp in mine:
            cp.start()
        first = []
        for a in range(n):
            first.append(copy(a, 0, me, sibling, src=in_refs[a]))
            first += [copy(a, 1 + j, me, (*chip, c), src=in_refs[a]) for j, chip in enumerate(chips)]
        for cp in first:
            cp.start()
        passed = []
        for j, chip in enumerate(chips):
            for a in range(n):
                copy(a, 1 + j, (*chip, c), me).wait_recv()
                fwd = copy(a, 4 + j, (*chip, c), sibling)
                fwd.start()
                passed.append(fwd)
        for a in range(n):
            copy(a, 0, sibling, me).wait_recv()
            for j, chip in enumerate(chips):
                copy(a, 4 + j, (*chip, 1 - c), me).wait_recv()
        for cp in first + passed:
            cp.wait_send()
        for cp in mine:
            cp.wait()

    return pl.pallas_call(
        body, name="all_gather_weights", in_specs=[ANY] * n, out_specs=[ANY] * n,
        out_shape=[jax.ShapeDtypeStruct((N_DEV,) + s.shape, s.dtype) for s in shards],
        scratch_shapes=_comm_scratch(n))(*shards)


def _comm_copies(in_refs, out_refs, kinds, send_sems, recv_sems, local_sems):
    x, y, c = _my_place()
    my_idx = 4 * x + 2 * y + c
    src = lambda a, idx: in_refs[a] if kinds[a] == "gather" else in_refs[a].at[idx]
    local = [pltpu.make_async_copy(src(a, my_idx), out_refs[a].at[my_idx], local_sems.at[a]) for a in range(len(kinds))]
    remote = []
    for rel in range(1, N_DEV):
        px, py, pc = x ^ ((rel >> 2) & 1), y ^ ((rel >> 1) & 1), c ^ (rel & 1)
        for a in range(len(kinds)):
            remote.append(pltpu.make_async_remote_copy(
                src_ref=src(a, 4 * px + 2 * py + pc), dst_ref=out_refs[a].at[my_idx], send_sem=send_sems.at[a, rel - 1],
                recv_sem=recv_sems.at[a, rel - 1], device_id=(px, py, pc), device_id_type=MESH))
    return local, remote


def _comm_start(local, remote):
    for cp in local + remote:
        cp.start()


def _comm_wait(local, remote):
    for cp in remote:
        cp.wait_recv()
    for cp in remote:
        cp.wait_send()
    for cp in local:
        cp.wait()


def _comm_out_shapes(arrays, kinds):
    return [jax.ShapeDtypeStruct(((N_DEV,) + a.shape) if k == "gather" else a.shape, a.dtype) for a, k in zip(arrays, kinds)]


def _comm_scratch(n):
    return [pltpu.SemaphoreType.DMA((n, N_DEV - 1)), pltpu.SemaphoreType.DMA((n, N_DEV - 1)), pltpu.SemaphoreType.DMA((n,))]


def _sequencer_scatter(name, parts, collective_id):
    src = jax.new_ref(parts, memory_space=pltpu.MemorySpace.HBM)
    dst = jax.empty_ref(jax.ShapeDtypeStruct(parts.shape, parts.dtype), memory_space=pltpu.MemorySpace.HBM)

    @pl.kernel(mesh=plsc.ScalarSubcoreMesh(axis_name="sequencer", num_cores=1), name=name,
               scratch_types=(pltpu.SemaphoreType.DMA((N_DEV - 1,)), pltpu.SemaphoreType.DMA((N_DEV - 1,))),
               compiler_params=pltpu.CompilerParams(collective_id=collective_id))
    def launch(send_sems, recv_sems):
        x, y, c = _my_place()
        my_idx = 4 * x + 2 * y + c
        peers = [(x ^ ((rel >> 2) & 1), y ^ ((rel >> 1) & 1), c ^ (rel & 1)) for rel in range(1, N_DEV)]
        barrier = pltpu.get_barrier_semaphore()
        for peer in peers:
            pl.semaphore_signal(barrier, inc=1, device_id=peer, device_id_type=MESH)
        pl.semaphore_wait(barrier, N_DEV - 1)
        copies = [pltpu.make_async_remote_copy(
            src_ref=src.at[4 * px + 2 * py + pc], dst_ref=dst.at[my_idx], send_sem=send_sems.at[k], recv_sem=recv_sems.at[k],
            device_id=(px, py, pc), device_id_type=MESH) for k, (px, py, pc) in enumerate(peers)]
        for cp in copies:
            cp.start()
        for cp in copies:
            cp.wait_recv()
        for cp in copies:
            cp.wait_send()

    launch()
    return dst[...]


def _exchange(arrays, kinds):
    n = len(arrays)

    def body(*refs):
        copies = _comm_copies(refs[:n], refs[n:2 * n], kinds, *refs[2 * n:])
        _comm_start(*copies)
        _comm_wait(*copies)

    return pl.pallas_call(body, name="exchange_grads", in_specs=[ANY] * n, out_specs=[ANY] * n,
                          out_shape=_comm_out_shapes(arrays, kinds), scratch_shapes=_comm_scratch(n))(*arrays)


def _adam_math(w, g, m, v):
    m = ADAM_B1 * m + (1.0 - ADAM_B1) * g
    v = ADAM_B2 * v + (1.0 - ADAM_B2) * (g * g)
    m_hat = m / (1.0 - ADAM_B1 ** ADAM_STEP)
    v_hat = v / (1.0 - ADAM_B2 ** ADAM_STEP)
    delta = -ADAM_LR * (m_hat / (jnp.sqrt(v_hat) + ADAM_EPS) + ADAM_WD * w)
    return delta, m, v


def _adamw(name, w, gslots, m, v, tc):
    r, cc = w.shape
    assert cc % tc == 0
    tile = pl.BlockSpec((r, tc), lambda i: (0, i))

    def body(w_ref, g_ref, m_ref, v_ref, go_ref, d_ref, mo_ref, vo_ref):
        g = g_ref[0].astype(F32)
        for s in range(1, N_DEV):
            g = g + g_ref[s].astype(F32)
        d, mn, vn = _adam_math(w_ref[...], g, m_ref[...], v_ref[...])
        go_ref[...] = g
        d_ref[...] = d
        mo_ref[...] = mn
        vo_ref[...] = vn

    shp = jax.ShapeDtypeStruct((r, cc), F32)
    return pl.pallas_call(body, name=name, grid=(cc // tc,),
                          in_specs=[tile, pl.BlockSpec((N_DEV, r, tc), lambda i: (0, 0, i)), tile, tile],
                          out_specs=(tile,) * 4, out_shape=(shp,) * 4, compiler_params=_cparams(("arbitrary",)))(w, gslots, m, v)


PACK_TILE = 8 * LANES


def _packed_rows(shape, mode):
    r, w = shape
    return -(-r // 8) * 8 if mode == "rows" else -(-(r * w) // PACK_TILE) * 8


def _pack_small(arrays, modes, lead=False):
    out = []
    for a, mode in zip(arrays, modes):
        a = a.astype(F32) if lead else a.astype(F32)[None]
        if mode == "rows":
            out.append(jnp.pad(a, ((0, 0), (0, (-a.shape[1]) % 8), (0, LANES - a.shape[2]))))
        else:
            flat = a.reshape(a.shape[0], -1)
            out.append(jnp.pad(flat, ((0, 0), (0, (-flat.shape[1]) % PACK_TILE))).reshape(a.shape[0], -1, LANES))
    out = jnp.concatenate(out, axis=1)
    return out if lead else out[0]


def _take_small(packed, row0, shape, mode):
    r, w = shape
    lead = packed.ndim == 3
    if mode == "rows":
        return packed[:, row0:row0 + r, :w] if lead else packed[row0:row0 + r, :w]
    per_row = -(-w // LANES)
    if lead:
        return packed[:, row0:row0 + r * per_row].reshape(packed.shape[0], r, per_row * LANES)[:, :, :w]
    rows = []
    for i in range(r):
        pieces = [packed[row0 + i * per_row + j:row0 + i * per_row + j + 1, :] for j in range(per_row)]
        rows.append((pieces[0] if per_row == 1 else jnp.concatenate(pieces, axis=1))[:, :w])
    return rows[0] if r == 1 else jnp.concatenate(rows, axis=0)


def _adamw_small(slots, specs, ws, ms, vs, loss_row):
    n = len(specs)

    def body(*refs):
        slots_ref, w_refs, m_refs, v_refs = refs[0], refs[1:1 + n], refs[1 + n:1 + 2 * n], refs[1 + 2 * n:1 + 3 * n]
        out_refs, loss_ref = refs[1 + 3 * n:1 + 7 * n], refs[1 + 7 * n]
        gp = slots_ref[0]
        for s in range(1, N_DEV):
            gp = gp + slots_ref[s]
        read = lambda ref: ref[0] if len(ref.shape) == 3 else ref[...]
        for k, (shape, mode, row0) in enumerate(specs):
            g = _take_small(gp, row0, shape, mode)
            d, mn, vn = _adam_math(read(w_refs[k]), g, read(m_refs[k]), read(v_refs[k]))
            for ref, val in zip(out_refs[4 * k:4 * k + 4], (g, d, mn, vn)):
                if len(ref.shape) == 3:
                    ref[0] = val
                else:
                    ref[...] = val
        loss_ref[...] = gp[loss_row:loss_row + 1, :]

    vmem = pl.BlockSpec(memory_space=pltpu.VMEM)
    out_shape = [jax.ShapeDtypeStruct(w.shape, F32) for w in ws for _ in range(4)] + [jax.ShapeDtypeStruct((1, LANES), F32)]
    outs = pl.pallas_call(body, name="adamw_small", in_specs=[vmem] * (1 + 3 * n), out_specs=[vmem] * (4 * n + 1),
                          out_shape=out_shape)(slots, *ws, *ms, *vs)
    return [outs[4 * k:4 * k + 4] for k in range(n)], outs[4 * n]


def _rope_tables(t):
    dim = jnp.arange(LANES) % SWA_HD
    inv_freq = ROPE_THETA ** (-(dim % ROPE_HALF).astype(F32) / ROPE_HALF)
    ang = jnp.arange(t, dtype=F32)[:, None] * jnp.where(dim < 2 * ROPE_HALF, inv_freq, 0.0)[None, :]
    return jnp.cos(ang), jnp.sin(ang)


def _pad_to(a, rows=None, cols=None):
    r = 0 if rows is None else rows - a.shape[0]
    c = 0 if cols is None else cols - a.shape[1]
    return jnp.pad(a, ((0, r), (0, c)))


ORIG0 = dict(gq=(0, 256), gk=(256, 256), gv=(512, 512), glow=(1024, 16), r=(1040, 512), k=(1552, 512), v=(2064, 512),
             xw=(2576, 64), xa=(2640, 64), gate=(2704, 1024))
ORIG0_ORDER = ["gq", "gk", "gv", "glow", "r", "k", "v", "xw", "xa", "gate"]


def _w0t_to_padded(wt):
    rows, at = [], 0
    for name, (off, width) in sorted(C0.items(), key=lambda kv: kv[1][0]):
        assert off == at
        src, src_w = ORIG0[name]
        rows.append(_pad_to(wt[src:src + src_w], rows=width))
        at += width
    rows.append(jnp.zeros((N0P - at, wt.shape[1]), wt.dtype))
    return jnp.concatenate(rows, axis=0)


def _w0t_from_padded(wpt):
    return jnp.concatenate([wpt[C0[n][0]:C0[n][0] + ORIG0[n][1]] for n in ORIG0_ORDER], axis=0)


def _w1t_to_mine(wt):
    return jnp.concatenate([wt[1536:2560], wt[:1536]], axis=0)


def _w1t_from_mine(wt):
    return jnp.concatenate([wt[1024:2560], wt[:1024]], axis=0)


def kernel(x, norm_w, w_in0, gla_gk_up, gla_gk_bias, gla_norm_w, rwkv_mu, rwkv_w0, rwkv_w_up, rwkv_a0, rwkv_a_up, rwkv_k_k, rwkv_k_a, rwkv_r_k, rwkv_ln_w, rwkv_ln_b, w_out0, w_in1, b_in1, attn_sinks, w_out1, b_out1, final_norm_w, loss_target, m_norm_w, m_w_in0, m_gla_gk_up, m_gla_gk_bias, m_gla_norm_w, m_rwkv_mu, m_rwkv_w0, m_rwkv_w_up, m_rwkv_a0, m_rwkv_a_up, m_rwkv_k_k, m_rwkv_k_a, m_rwkv_r_k, m_rwkv_ln_w, m_rwkv_ln_b, m_w_out0, m_w_in1, m_b_in1, m_attn_sinks, m_w_out1, m_b_out1, m_final_norm_w, v_norm_w, v_w_in0, v_gla_gk_up, v_gla_gk_bias, v_gla_norm_w, v_rwkv_mu, v_rwkv_w0, v_rwkv_w_up, v_rwkv_a0, v_rwkv_a_up, v_rwkv_k_k, v_rwkv_k_a, v_rwkv_r_k, v_rwkv_ln_w, v_rwkv_ln_b, v_w_out0, v_w_in1, v_b_in1, v_attn_sinks, v_w_out1, v_b_out1, v_final_norm_w):
    weights = dict(norm_w=norm_w, w_in0=w_in0, gla_gk_up=gla_gk_up, gla_gk_bias=gla_gk_bias, gla_norm_w=gla_norm_w, rwkv_mu=rwkv_mu,
                   rwkv_w0=rwkv_w0, rwkv_w_up=rwkv_w_up, rwkv_a0=rwkv_a0, rwkv_a_up=rwkv_a_up, rwkv_k_k=rwkv_k_k, rwkv_k_a=rwkv_k_a,
                   rwkv_r_k=rwkv_r_k, rwkv_ln_w=rwkv_ln_w, rwkv_ln_b=rwkv_ln_b, w_out0=w_out0, w_in1=w_in1, b_in1=b_in1,
                   attn_sinks=attn_sinks, w_out1=w_out1, b_out1=b_out1, final_norm_w=final_norm_w)
    moms = dict(norm_w=m_norm_w, w_in0=m_w_in0, gla_gk_up=m_gla_gk_up, gla_gk_bias=m_gla_gk_bias, gla_norm_w=m_gla_norm_w,
                rwkv_mu=m_rwkv_mu, rwkv_w0=m_rwkv_w0, rwkv_w_up=m_rwkv_w_up, rwkv_a0=m_rwkv_a0, rwkv_a_up=m_rwkv_a_up,
                rwkv_k_k=m_rwkv_k_k, rwkv_k_a=m_rwkv_k_a, rwkv_r_k=m_rwkv_r_k, rwkv_ln_w=m_rwkv_ln_w, rwkv_ln_b=m_rwkv_ln_b,
                w_out0=m_w_out0, w_in1=m_w_in1, b_in1=m_b_in1, attn_sinks=m_attn_sinks, w_out1=m_w_out1, b_out1=m_b_out1,
                final_norm_w=m_final_norm_w)
    vars_ = dict(norm_w=v_norm_w, w_in0=v_w_in0, gla_gk_up=v_gla_gk_up, gla_gk_bias=v_gla_gk_bias, gla_norm_w=v_gla_norm_w,
                 rwkv_mu=v_rwkv_mu, rwkv_w0=v_rwkv_w0, rwkv_w_up=v_rwkv_w_up, rwkv_a0=v_rwkv_a0, rwkv_a_up=v_rwkv_a_up,
                 rwkv_k_k=v_rwkv_k_k, rwkv_k_a=v_rwkv_k_a, rwkv_r_k=v_rwkv_r_k, rwkv_ln_w=v_rwkv_ln_w, rwkv_ln_b=v_rwkv_ln_b,
                 w_out0=v_w_out0, w_in1=v_w_in1, b_in1=v_b_in1, attn_sinks=v_attn_sinks, w_out1=v_w_out1, b_out1=v_b_out1,
                 final_norm_w=v_final_norm_w)
    names = list(weights)
    big = ["w_in0", "w_out0", "w_in1", "w_out1"]
    small_sharded = ["gla_gk_up", "rwkv_w_up", "rwkv_a_up", "b_in1", "b_out1"]
    replicated = [n for n in names if n not in big and n not in small_sharded]

    xs = x[0]
    tgt = loss_target[0]
    t = xs.shape[0]

    def view(w):
        shape = tuple(w.shape[-2:]) if w.ndim >= 2 else (1, w.shape[0])
        return shape, ("rows" if shape[0] > 1 and shape[1] <= LANES else "flat")

    def layout(ns, row0=0):
        specs = []
        for n in ns:
            shape, mode = view(weights[n])
            specs.append((shape, mode, row0))
            row0 += _packed_rows(shape, mode)
        return specs, row0

    sh_specs, n_shard_rows = layout(small_sharded)
    rep_specs, loss_row = layout(replicated, n_shard_rows)
    sh_modes, rep_modes = [s[1] for s in sh_specs], [s[1] for s in rep_specs]

    small_shard_pack = _pack_small([weights[n].reshape(view(weights[n])[0]) for n in small_sharded], sh_modes)
    g_in0, g_small = _all_gather([w_in0[0].T.astype(BF16), small_shard_pack])
    w0t = _w0t_to_padded(g_in0.reshape(-1, D_MODEL))
    later_shards = [w_out0[0].astype(BF16), w_in1[0].T.astype(BF16), w_out1[0].astype(BF16)]
    gs = [_take_small(g_small, row0, shape, mode) for shape, mode, row0 in sh_specs]
    join_cols = lambda a: jnp.transpose(a, (1, 0, 2)).reshape(a.shape[1], -1)
    gk_up, w_up, a_up = join_cols(gs[0]), join_cols(gs[1]), join_cols(gs[2])
    b_in, b_out = gs[3].reshape(1, -1), gs[4].reshape(1, -1)

    gk_up_p = _pad_to(gk_up, rows=LOW)
    w3, rank = 3 * RWKV_W, rwkv_w_up.shape[1]
    mu = rwkv_mu
    rwkv_params = [mu[:, 0:RWKV_W], mu[:, RWKV_W:2 * RWKV_W], mu[:, 2 * RWKV_W:w3], _pad_to(mu[:, w3:w3 + rank], cols=LOW),
                   _pad_to(mu[:, w3 + rank:], cols=LOW), rwkv_w0, _pad_to(w_up, rows=LOW), rwkv_a0, _pad_to(a_up, rows=LOW),
                   rwkv_k_k, rwkv_k_a, rwkv_r_k.reshape(1, RWKV_W), rwkv_ln_w, rwkv_ln_b]
    bq, bk, bv = b_in[:, :MIX], b_in[:, MIX:MIX + SWA_KV], b_in[:, MIX + SWA_KV:]
    cos, sin = _rope_tables(t)
    nw0, nw1, fw = norm_w[0:1], norm_w[1:2], final_norm_w.reshape(1, D_MODEL)

    d = D_MODEL
    wide = lambda arr: (arr, d, 0)
    silu = lambda g: g * sigmoid(g)
    hn0, proj0 = _matmul_fused("norm0_proj0", rms, w0t, "nt", [wide(xs)], [nw0], [(N0P, F32)], [], lambda acc, x, w: (acc,))
    o_a, gla_states = _gla_fwd(proj0, gk_up_p, gla_gk_bias, gla_norm_w)
    o_b, rwkv_states, rwkv_prevs, (g_out0, g_in1, g_out1) = _rwkv_fwd(proj0, rwkv_params, later_shards, ["gather"] * 3)
    wo0 = g_out0.reshape(MIX, D_MODEL)
    w1t = _w1t_to_mine(g_in1.reshape(-1, D_MODEL))
    wo1 = g_out1.reshape(MIX, D_MODEL)
    og0, h1, hn1 = _matmul_fused(
        "gate0_out0_norm1", lambda oa, ob, gate, x, w: jnp.concatenate([oa, ob], axis=1) * silu(gate), wo0, "nn",
        [(o_a, GLA_VAL, 0), (o_b, RWKV_W, 0), wide(proj0), wide(xs)], [nw1], [(d, F32), (d, BF16)], [],
        lambda acc, oa, ob, gate, x, w: _resid_norm(acc, x, w))
    proj1 = _matmul("proj1", hn1, w1t, "nt", PROJ_ROWS, N1P // 2)
    o_c, kst, vst = _swa_fwd(proj1, cos, sin, bq, bk, bv, attn_sinks)
    og1, dh2, loss_part, d_b_out, d_fw = _matmul_fused(
        "gate1_out1_loss", lambda oc, gate, h, tg, b, w: oc * silu(gate), wo1, "nn",
        [wide(o_c), wide(proj1), wide(h1), wide(tgt)], [b_out, fw], [(d, F32)], [LANES, d, d],
        lambda acc, oc, gate, h, tg, b, w: _loss_head(acc, h, tg, b, w))

    d_oc, d_gate1 = _matmul_fused("out1_dx_gate1", dh2, wo1, "nt", [wide(o_c), wide(proj1)], [], [(d, F32), (d, BF16)], [], _gate_back)
    d_wo1 = _matmul("out1_dw", og1, dh2, "tn", DW_COLS, DW_COLS, BF16)
    dq, dk, dv, d_bq, d_bk, d_bv, d_sinks = _swa_bwd(proj1, cos, sin, bq, bk, bv, attn_sinks, kst, vst, d_oc)
    dproj1 = jnp.concatenate([d_gate1, dq, dk, dv], axis=1)
    dh1, d_nw1 = _matmul_fused("proj1_dx_norm1", dproj1, w1t, "nn", [wide(h1), wide(dh2)], [nw1], [(d, F32)], [d], _norm_back)
    d_w1t = _matmul("proj1_dw", dproj1, hn1, "tn", DW_COLS, d, BF16)
    d_oa, d_ob, d_gate0 = _matmul_fused("out0_dx_gate0", dh1, wo0, "nt", [(o_a, GLA_VAL, 0), (o_b, RWKV_W, 0), wide(proj0)], [],
                                        [(GLA_VAL, F32), (RWKV_W, F32), (d, BF16)], [], _gate_back)
    d_wo0 = _matmul("out0_dw", og0, dh1, "tn", DW_COLS, DW_COLS, BF16)
    dgq, dgk, dgv, dglow, d_gk_up, d_gk_bias, d_gla_nw = _gla_bwd(proj0, gk_up_p, gla_gk_bias, gla_norm_w, gla_states, d_oa)
    row_blocks = lambda a: a.astype(BF16).reshape(N_DEV, -1, D_MODEL)
    early = [row_blocks(_w1t_from_mine(d_w1t)), row_blocks(d_wo1), row_blocks(d_wo0)]
    (dr, dkk, dvv, dxw, dxa), d_rp, (r_in1, r_out1, r_out0) = _rwkv_bwd(
        proj0, rwkv_params, rwkv_states, rwkv_prevs, d_ob, early, ["scatter"] * 3)
    pad = jnp.zeros((t, N0P - C0["xa"][0] - C0["xa"][1]), BF16)
    dproj0 = jnp.concatenate([d_gate0, dgv, dr, dkk, dvv, dgq, dgk, dglow, dxw, dxa, pad], axis=1)
    d_w0 = row_blocks(_w0t_from_padded(_matmul("proj0_dw", dproj0, hn0, "tn", DW_COLS, d, BF16)))
    r_in0 = _sequencer_scatter("exchange_w_in0_grad", d_w0, 0)
    grad_x, d_nw0 = _matmul_fused("proj0_dx_norm0", dproj0, w0t, "nn", [wide(xs), wide(dh1)], [nw0], [(d, F32)], [d], _norm_back)

    contrib = dict(
        norm_w=jnp.concatenate([d_nw0, d_nw1], axis=0), gla_gk_bias=d_gk_bias, gla_norm_w=d_gla_nw,
        rwkv_mu=jnp.concatenate([d_rp[0], d_rp[1], d_rp[2], d_rp[3][:, :rank], d_rp[4][:, :rank]], axis=1),
        rwkv_w0=d_rp[5], rwkv_a0=d_rp[7], rwkv_k_k=d_rp[9], rwkv_k_a=d_rp[10], rwkv_r_k=d_rp[11].reshape(RWKV_HEADS, RWKV_N),
        rwkv_ln_w=d_rp[12], rwkv_ln_b=d_rp[13], attn_sinks=d_sinks, final_norm_w=d_fw)
    rep_pack = _pack_small([contrib[n] for n in replicated] + [loss_part[:, :1]], rep_modes + ["flat"])

    d_b_in = jnp.concatenate([d_bq, d_bk, d_bv], axis=1)
    full_small = [d_gk_up[:gk_up.shape[0]], d_rp[6][:rank], d_rp[8][:rank], d_b_in, d_b_out]
    split_cols = lambda a: jnp.transpose(a.reshape(a.shape[0], N_DEV, -1), (1, 0, 2))
    small_parts = [split_cols(a) for a in full_small]
    small_pack = _pack_small(small_parts, sh_modes, lead=True)
    r_small, r_rep = _exchange([small_pack, rep_pack], ["scatter", "gather"])

    res = {}
    res["w_out0"] = tuple(a[None] for a in _adamw("adamw_w_out0", w_out0[0], r_out0, m_w_out0[0], v_w_out0[0], ADAM_COLS))
    res["w_in1"] = tuple(a.T[None] for a in _adamw("adamw_w_in1", w_in1[0].T, r_in1, m_w_in1[0].T, v_w_in1[0].T, ADAM_COLS))
    res["w_out1"] = tuple(a[None] for a in _adamw("adamw_w_out1", w_out1[0], r_out1, m_w_out1[0], v_w_out1[0], ADAM_COLS))
    small_names = small_sharded + replicated
    slots = jnp.concatenate([r_small, r_rep], axis=1)
    as_2d = lambda a: a.reshape(1, -1) if a.ndim == 1 else a
    small_res, loss_row_out = _adamw_small(slots, sh_specs + rep_specs, [as_2d(weights[n]) for n in small_names],
                                           [as_2d(moms[n]) for n in small_names], [as_2d(vars_[n]) for n in small_names], loss_row)
    for n, vals in zip(small_names, small_res):
        res[n] = tuple(val.reshape(weights[n].shape) for val in vals)
    loss = loss_row_out[0, 0]
    my_idx = 4 * lax.axis_index("x") + 2 * lax.axis_index("y") + lax.axis_index("c")
    r_in0 = lax.dynamic_update_slice(r_in0, lax.dynamic_slice(d_w0, (my_idx, 0, 0), (1,) + d_w0.shape[1:]), (my_idx, 0, 0))
    res["w_in0"] = tuple(a.T[None] for a in _adamw("adamw_w_in0", w_in0[0].T, r_in0, m_w_in0[0].T, v_w_in0[0].T, ADAM_COLS))
    return (loss, grad_x[None], *[res[n][0] for n in names], *[res[n][1] for n in names],
            *[res[n][2] for n in names], *[res[n][3] for n in names])
```

```python
import functools

import jax
import jax.numpy as jnp
from jax import lax
from jax.experimental import pallas as pl
from jax.experimental.pallas import tpu as pltpu
from jax.experimental.pallas import tpu_sc as plsc

F32 = jnp.float32
BF16 = jnp.bfloat16
HI = lax.Precision.HIGHEST

D_MODEL = 1024
NORM_EPS = 1e-5
GLA_HEADS, GLA_DK, GLA_DV = 4, 64, 128
GLA_NORMALIZER = 16.0
GLA_CHUNK = 64
GLA_STEP = 1024
RWKV_HEADS, RWKV_N = 8, 64
RWKV_LN_EPS = 64e-5
RWKV_CHUNK = 128
SWA_Q_HEADS, SWA_KV_HEADS, SWA_GROUP, SWA_HD = 16, 4, 4, 64
WINDOW = 128
SWA_STEP = 512
ROPE_THETA = 500000.0
NEG = -1e30
N_DEV = 8
LANES = 128

ADAM_LR, ADAM_B1, ADAM_B2, ADAM_EPS, ADAM_WD, ADAM_STEP = 0.001, 0.9, 0.999, 1e-08, 0.01, 10

GLA_KEY, GLA_VAL = GLA_HEADS * GLA_DK, GLA_HEADS * GLA_DV
RWKV_W = RWKV_HEADS * RWKV_N
SWA_KV = SWA_KV_HEADS * SWA_HD
MIX = GLA_VAL + RWKV_W
LOW = LANES

N0P = 4096
C0 = dict(gate=(0, MIX), gv=(1024, GLA_VAL), r=(1536, RWKV_W), k=(2048, RWKV_W), v=(2560, RWKV_W), gq=(3072, GLA_KEY),
          gk=(3328, GLA_KEY), glow=(3584, LOW), xw=(3712, LOW), xa=(3840, LOW))
N1P = 2560
C1 = dict(gate=(0, MIX), q=(1024, MIX), k=(2048, SWA_KV), v=(2304, SWA_KV))

VMEM_LIMIT = 56 * 1024 * 1024

P_LORA = 1
P_GLA = 1
P_RWKV_G = 2
P_RWKV = 1
P_SWA = 1


def _cparams(sem=None):
    return pltpu.CompilerParams(dimension_semantics=sem, vmem_limit_bytes=VMEM_LIMIT)


DIMS = dict(nn=(((1,), (0,)), ((), ())), nt=(((1,), (1,)), ((), ())), tn=(((0,), (0,)), ((), ())))


def _split_bf16(a):
    hi = a.astype(BF16)
    return hi, (a - hi.astype(F32)).astype(BF16)


def _dot(a, b, mode, passes):
    dg = lambda p, q: lax.dot_general(p, q, DIMS[mode], preferred_element_type=F32)
    if passes == 1:
        return dg(a.astype(BF16), b.astype(BF16))
    if passes == 2:
        ah, (bh, bl) = a.astype(BF16), _split_bf16(b)
        return dg(ah, bh) + dg(ah, bl)
    if passes == 3:
        (ah, al), (bh, bl) = _split_bf16(a), _split_bf16(b)
        return dg(ah, bh) + dg(al, bh) + dg(ah, bl)
    return lax.dot_general(a, b, DIMS[mode], precision=HI, preferred_element_type=F32)


@functools.partial(jax.custom_vjp, nondiff_argnums=(2, 3))
def mmx(a, b, mode, passes):
    return _dot(a, b, mode, passes)


def _mmx_fwd(a, b, mode, passes):
    return _dot(a, b, mode, passes), (a, b)


def _mmx_bwd(mode, passes, res, g):
    a, b = res
    if mode == "nn":
        return _dot(g, b, "nt", passes), _dot(a, g, "tn", passes)
    if mode == "nt":
        return _dot(g, b, "nn", passes), _dot(g, a, "tn", passes)
    return _dot(b, g, "nt", passes), _dot(a, g, "nn", passes)


mmx.defvjp(_mmx_fwd, _mmx_bwd)


def _tri_dot(tri, x):
    t = tri.astype(BF16)
    x1 = x.astype(BF16)
    r1 = x - x1.astype(F32)
    x2 = r1.astype(BF16)
    x3 = (r1 - x2.astype(F32)).astype(BF16)
    dg = lambda q: jnp.dot(t, q, preferred_element_type=F32)
    return dg(x1) + dg(x2) + dg(x3)


@jax.custom_vjp
def cumsum_rows(x):
    return _tri_dot(tril_ones(x.shape[0]), x)


def _cumsum_fwd(x):
    return cumsum_rows(x), None


def _cumsum_bwd(_, g):
    i, j = _iota2(g.shape[0], g.shape[0])
    return (_tri_dot(jnp.where(i <= j, 1.0, 0.0).astype(F32), g),)


cumsum_rows.defvjp(_cumsum_fwd, _cumsum_bwd)


def _head_dot(x):
    i, j = _iota2(LANES, LANES)
    shift = RWKV_N.bit_length() - 1
    same = jnp.where(jnp.right_shift(i, shift) == jnp.right_shift(j, shift), 1.0, 0.0).astype(F32)
    return jnp.concatenate([_ones_right(x[:, g * LANES:(g + 1) * LANES], same) for g in range(x.shape[1] // LANES)], axis=1)


def _ones_right(x, ones):
    t = ones.astype(BF16)
    x1 = x.astype(BF16)
    x2 = (x - x1.astype(F32)).astype(BF16)
    dg = lambda q: jnp.dot(q, t, preferred_element_type=F32)
    return dg(x1) + dg(x2)


@jax.custom_vjp
def head_sum(x):
    return _head_dot(x)


def _head_sum_fwd(x):
    return head_sum(x), None


def _head_sum_bwd(_, g):
    return (_head_dot(g),)


head_sum.defvjp(_head_sum_fwd, _head_sum_bwd)


def cat_rows(*xs):
    return jnp.concatenate(xs, axis=0)


def _iota2(n, m):
    return lax.broadcasted_iota(jnp.int32, (n, m), 0), lax.broadcasted_iota(jnp.int32, (n, m), 1)


def tril_ones(c, strict=False):
    i, j = _iota2(c, c)
    return jnp.where((i > j) if strict else (i >= j), 1.0, 0.0).astype(F32)


def row_of(x, r):
    i = lax.broadcasted_iota(jnp.int32, x.shape, 0)
    return jnp.sum(jnp.where(i == r, x, 0.0), axis=0, keepdims=True)


@jax.custom_vjp
def shift_rows(x, prev):
    r = lax.broadcasted_iota(jnp.int32, x.shape, 0)
    return jnp.where(r == 0, prev, pltpu.roll(x, 1, 0))


def _shift_fwd(x, prev):
    return shift_rows(x, prev), None


def _shift_bwd(_, g):
    c = g.shape[0]
    r = lax.broadcasted_iota(jnp.int32, g.shape, 0)
    return jnp.where(r == c - 1, 0.0, pltpu.roll(g, c - 1, 0)), row_of(g, 0)


shift_rows.defvjp(_shift_fwd, _shift_bwd)


def log_sigmoid(x):
    return jnp.minimum(x, 0.0) - jnp.log(1.0 + jnp.exp(-jnp.abs(x)))


def softplus(x):
    return jnp.maximum(x, 0.0) + jnp.log(1.0 + jnp.exp(-jnp.abs(x)))


def sigmoid(x):
    return 1.0 / (1.0 + jnp.exp(-x))


def rms(x, w, eps=NORM_EPS):
    return x * lax.rsqrt(jnp.mean(x * x, axis=-1, keepdims=True) + eps) * w


def gla_chunk(state, toks, params):
    q, k, v, glow = toks
    gk_up, bias, norm_w = params
    c = GLA_CHUNK
    subs, heads = range(glow.shape[0] // c), range(GLA_HEADS)
    rows = lambda x, j: x[j * c:(j + 1) * c]
    hk = lambda x, h: x[:, h * GLA_DK:(h + 1) * GLA_DK]
    hv = lambda x, h: x[:, h * GLA_DV:(h + 1) * GLA_DV]
    ltri = tril_ones(c)
    g = log_sigmoid(mmx(glow, gk_up, "nn", P_LORA) + bias) / GLA_NORMALIZER
    b = [cumsum_rows(rows(g, j)) for j in subs]
    ref = [lax.stop_gradient(row_of(b[j], c // 2)) for j in subs]
    last = [row_of(b[j], c - 1) for j in subs]
    ql = [rows(q, j) * (GLA_DK ** -0.5) * jnp.exp(b[j] - ref[j]) for j in subs]
    kr = [rows(k, j) * jnp.exp(ref[j] - b[j]) for j in subs]
    kl = [rows(k, j) * jnp.exp(last[j] - b[j]) for j in subs]
    vj = [rows(v, j) for j in subs]
    e_ref, e_last = [jnp.exp(x) for x in ref], [jnp.exp(x) for x in last]
    att = [[mmx(hk(ql[j], h), hk(kr[j], h), "nt", P_GLA) * ltri for h in heads] for j in subs]
    o_in = [[mmx(att[j][h], hv(vj[j], h), "nn", P_GLA) for h in heads] for j in subs]
    kv = [[mmx(hv(vj[j], h), hk(kl[j], h), "tn", P_GLA) for h in heads] for j in subs]
    o = []
    for j in subs:
        o.append([o_in[j][h] + mmx(hk(ql[j], h), state[h] * hk(e_ref[j], h), "nt", P_GLA) for h in heads])
        state = [state[h] * hk(e_last[j], h) + kv[j][h] for h in heads]
    o = [[x * lax.rsqrt(jnp.mean(x * x, axis=-1, keepdims=True) + NORM_EPS) * norm_w for x in oj] for oj in o]
    return cat_rows(*[jnp.concatenate(oj, axis=1) for oj in o]), state


SOLVE_BLOCK = 128


def solve_unit_lower(ps, ws):
    n = ps[0].shape[0]
    heads = range(len(ps))
    if n > SOLVE_BLOCK:
        half = n // 2
        top = solve_unit_lower([p[:half, :half] for p in ps], [w[:half] for w in ws])
        rest = [ws[h][half:] + mmx(ps[h][half:, :half], top[h], "nn", P_RWKV) for h in heads]
        bottom = solve_unit_lower([p[half:, half:] for p in ps], rest)
        return [cat_rows(top[h], bottom[h]) for h in heads]
    u, p = ws, ps
    levels = max(1, (n - 1).bit_length())
    for it in range(levels):
        if it + 1 < levels:
            y = [mmx(p[h], jnp.concatenate([p[h], u[h]], axis=1), "nn", P_RWKV) for h in heads]
            u = [u[h] + y[h][:, n:] for h in heads]
            p = [y[h][:, :n] for h in heads]
        else:
            u = [u[h] + mmx(p[h], u[h], "nn", P_RWKV) for h in heads]
    return u


def rwkv_chunk(state, toks, params):
    S, pr, pk, pv, pxw, pxa = state
    r_, k_, v_, xw_, xa_ = toks
    mu_r, mu_k, mu_v, mu_xw, mu_xa, w0, w_up, a0, a_up, k_k, k_a, r_k, ln_w, ln_b = params
    c, n = xw_.shape[0], RWKV_N
    heads = range(RWKV_HEADS)
    hs = lambda x, h: x[:, h * n:(h + 1) * n]
    ltri = tril_ones(c)
    stri = tril_ones(c, strict=True)

    def lerp(x, prev, mu):
        return x + (shift_rows(x, prev) - x) * mu

    xw = jnp.tanh(lerp(xw_, pxw, mu_xw))
    xa = lerp(xa_, pxa, mu_xa)
    r = lerp(r_, pr, mu_r)
    k = lerp(k_, pk, mu_k)
    v = lerp(v_, pv, mu_v)
    w = -softplus(-(w0 + mmx(xw, w_up, "nn", P_LORA))) - 0.5
    lw = -jnp.exp(w)
    asig = sigmoid(a0 + mmx(xa, a_up, "nn", P_LORA))
    kk = k * k_k
    kk = kk * lax.rsqrt(jnp.maximum(head_sum(kk * kk), 1e-24))
    k2 = k * (1.0 + (asig - 1.0) * k_a)
    b = kk * asig
    cum = cumsum_rows(lw)
    ref = lax.stop_gradient(row_of(cum, c // 2))
    last = row_of(cum, c - 1)
    at = -kk * jnp.exp(cum - lw - ref)
    rt = r * jnp.exp(cum - ref)
    e_out = jnp.exp(ref - cum)
    bt, kt = b * e_out, k2 * e_out
    e_tail = jnp.exp(last - cum)
    bl, kl = b * e_tail, k2 * e_tail
    e_ref, e_last = jnp.exp(ref), jnp.exp(last)
    g = [mmx(cat_rows(hs(at, h), hs(rt, h)), cat_rows(hs(bt, h), hs(kt, h), S[h] * hs(e_ref, h)), "nt", P_RWKV_G) for h in heads]
    aab = [x[:c, :c] * stri for x in g]
    aak = [x[:c, c:2 * c] * stri for x in g]
    arb = [x[c:, :c] * ltri for x in g]
    ark = [x[c:, c:2 * c] * ltri for x in g]
    av = [mmx(cat_rows(aak[h], ark[h]), hs(v, h), "nn", P_RWKV) for h in heads]
    u = solve_unit_lower(aab, [g[h][:c, 2 * c:] + av[h][:c] for h in heads])
    o = [g[h][c:, 2 * c:] + av[h][c:] + mmx(arb[h], u[h], "nn", P_RWKV) for h in heads]
    s1 = [S[h] * hs(e_last, h) + mmx(cat_rows(u[h], hs(v, h)), cat_rows(hs(bl, h), hs(kl, h)), "tn", P_RWKV) for h in heads]
    o = jnp.concatenate(o, axis=1)
    d = o - head_sum(o) * (1.0 / n)
    var = head_sum(d * d) * (1.0 / n)
    o = d * lax.rsqrt(var + RWKV_LN_EPS) * ln_w + ln_b + head_sum(r * k2 * r_k) * v
    new_state = (s1, row_of(r_, c - 1), row_of(k_, c - 1), row_of(v_, c - 1), row_of(xw_, c - 1), row_of(xa_, c - 1))
    return o, new_state


RWKV_STEP = 256


def rwkv_chunks(state, toks, params):
    outs = []
    for j in range(toks[3].shape[0] // RWKV_CHUNK):
        rows = slice(j * RWKV_CHUNK, (j + 1) * RWKV_CHUNK)
        o, state = rwkv_chunk(state, tuple(t[rows] for t in toks), params)
        outs.append(o)
    return cat_rows(*outs), state


ROPE_HALF = 8


def _rot_half_raw(x):
    lane = lax.broadcasted_iota(jnp.int32, (x.shape[0], LANES), 1) & (SWA_HD - 1)
    out = []
    for i in range(x.shape[1] // LANES):
        g = x[:, i * LANES:(i + 1) * LANES]
        up, down = pltpu.roll(g, LANES - ROPE_HALF, 1), pltpu.roll(g, ROPE_HALF, 1)
        out.append(jnp.where(lane < ROPE_HALF, -up, jnp.where(lane < 2 * ROPE_HALF, down, 0.0)))
    return out[0] if len(out) == 1 else jnp.concatenate(out, axis=1)


@jax.custom_vjp
def rot_half(x):
    return _rot_half_raw(x)


rot_half.defvjp(lambda x: (_rot_half_raw(x), None), lambda _, g: (-_rot_half_raw(g),))


def rope(x, cos2, sin2):
    reps = x.shape[1] // LANES
    tile = lambda t: t if reps == 1 else jnp.concatenate([t] * reps, axis=1)
    return x * tile(cos2) + rot_half(x) * tile(sin2)


def swa_chunk(state, toks, params, first):
    kprev, vprev = state
    q_, k_, v_, cos, sin = toks
    bq, bk, bv, sinks = params
    c, ng = WINDOW, SWA_GROUP
    n_sub = cos.shape[0] // c
    units = [(j, g) for j in range(n_sub) for g in range(SWA_KV_HEADS)]
    rows = lambda x, j: x[j * c:(j + 1) * c]
    hs = lambda g: range(g * ng, (g + 1) * ng)
    head = lambda x, h: x[:, h * SWA_HD:(h + 1) * SWA_HD]
    qi, kj = _iota2(ng * c, 2 * c)
    qpos = qi & (c - 1)
    cur_ok = (kj >= c) & (qpos >= kj - c)
    prev_ok = (kj < c) & (kj > qpos)
    ok = [cur_ok | (prev_ok & jnp.logical_not(first))] + [cur_ok | prev_ok] * (n_sub - 1)
    q_all = rope(q_ + bq, cos, sin) * (SWA_HD ** -0.5)
    k_all = rope(k_ + bk, cos, sin)
    v_all = v_ + bv
    k = {(j, g): rows(head(k_all, g), j) for j, g in units}
    v = {(j, g): rows(head(v_all, g), j) for j, g in units}
    q = {(j, g): cat_rows(*[rows(head(q_all, h), j) for h in hs(g)]) for j, g in units}
    kp = lambda j, g: kprev[g] if j == 0 else k[(j - 1, g)]
    vp = lambda j, g: vprev[g] if j == 0 else v[(j - 1, g)]
    s = {(j, g): jnp.where(ok[j], mmx(q[(j, g)], cat_rows(kp(j, g), k[(j, g)]), "nt", P_SWA), NEG) for j, g in units}
    sink = [cat_rows(*[jnp.broadcast_to(sinks[h], (c, 1)) for h in hs(g)]) for g in range(SWA_KV_HEADS)]
    m = {(j, g): lax.stop_gradient(jnp.maximum(jnp.max(s[(j, g)], axis=-1, keepdims=True), sink[g])) for j, g in units}
    p = {u: jnp.exp(s[u] - m[u]) for u in units}
    ones = jnp.ones((2 * c, SWA_HD), F32)
    pv = {(j, g): mmx(p[(j, g)], cat_rows(vp(j, g), v[(j, g)]), "nn", P_SWA) for j, g in units}
    den = {u: mmx(p[u], ones, "nn", P_SWA) for u in units}
    o = {(j, g): pv[(j, g)] / (den[(j, g)] + jnp.exp(sink[g] - m[(j, g)])) for j, g in units}
    outs = [cat_rows(*[o[(j, g)][i * c:(i + 1) * c] for j in range(n_sub)]) for g in range(SWA_KV_HEADS) for i in range(ng)]
    last = n_sub - 1
    return outs, ([k[(last, g)] for g in range(SWA_KV_HEADS)], [v[(last, g)] for g in range(SWA_KV_HEADS)])


def _heads(ref, n, w, rows=slice(None)):
    return [ref[rows, h * w:(h + 1) * w] for h in range(n)]


def _put_heads(ref, vals, w, rows=slice(None), add=False):
    for h, val in enumerate(vals):
        if add:
            ref[rows, h * w:(h + 1) * w] += val
        else:
            ref[rows, h * w:(h + 1) * w] = val


def _col(block_w, name, table):
    off, w = table[name]
    assert off % block_w == 0 and w % block_w == 0
    return off // block_w


def _tok_spec(c, w, colblock, n=None):
    if n is None:
        return pl.BlockSpec((c, w), lambda i: (i, colblock))
    return pl.BlockSpec((c, w), lambda i: (n - 1 - i, colblock))


def _full_spec(shape):
    return pl.BlockSpec(shape, lambda i: (0,) * len(shape))


def _matmul(name, a, b, mode, tm, tn, out_dtype=F32):
    (m, kd) = (a.shape[1], a.shape[0]) if mode == "tn" else a.shape
    n = b.shape[0] if mode == "nt" else b.shape[1]
    assert m % tm == 0 and n % tn == 0
    a_spec = pl.BlockSpec((kd, tm), lambda j, i: (0, i)) if mode == "tn" else pl.BlockSpec((tm, kd), lambda j, i: (i, 0))
    b_spec = pl.BlockSpec((tn, kd), lambda j, i: (j, 0)) if mode == "nt" else pl.BlockSpec((kd, tn), lambda j, i: (0, j))

    def body(a_ref, b_ref, o_ref):
        o_ref[...] = lax.dot_general(a_ref[...].astype(BF16), b_ref[...].astype(BF16), DIMS[mode],
                                     preferred_element_type=F32).astype(out_dtype)

    return pl.pallas_call(
        body, name=name, grid=(n // tn, m // tm), in_specs=[a_spec, b_spec],
        out_specs=pl.BlockSpec((tm, tn), lambda j, i: (i, j)), out_shape=jax.ShapeDtypeStruct((m, n), out_dtype),
        compiler_params=_cparams(("arbitrary", "arbitrary")))(a, b)


TOK_TILE = 512
PROJ_ROWS = 1024
DW_COLS = 512
ADAM_COLS = 256


def _matmul_fused(name, a, b, mode, tiles, rows, outs, sums, epilogue, comm=(), kinds=()):
    made = callable(a)
    m = tiles[0][0].shape[0] if made else a.shape[0]
    kd = b.shape[0] if mode == "nn" else b.shape[1]
    n = b.shape[1] if mode == "nn" else b.shape[0]
    tm = TOK_TILE
    steps = m // tm
    if made:
        outs = [(kd, BF16)] + list(outs)
    nt_, nr, no, ns, ncomm = len(tiles), len(rows), len(outs), len(sums), len(comm)

    def body(*refs):
        at = 1 if made else 2
        b_ref = refs[at - 1]
        tile_refs, row_refs, comm_in = refs[at:at + nt_], refs[at + nt_:at + nt_ + nr], refs[at + nt_ + nr:at + nt_ + nr + ncomm]
        at += nt_ + nr + ncomm
        out_refs, sum_refs, comm_out = refs[at:at + no], refs[at + no:at + no + ns], refs[at + no + ns:at + no + ns + ncomm]
        sems = refs[at + no + ns + ncomm:]
        i = pl.program_id(0)

        @pl.when(i == 0)
        def _():
            if ncomm:
                _comm_start(*_comm_copies(comm_in, comm_out, kinds, *sems))
            for ref in sum_refs:
                ref[...] = jnp.zeros_like(ref)

        extras = [r[...] for r in tile_refs] + [r[...] for r in row_refs]
        a_blk = (a(*extras) if made else refs[0][...]).astype(BF16)
        acc = lax.dot_general(a_blk, b_ref[...].astype(BF16), DIMS[mode], preferred_element_type=F32)
        res = epilogue(acc, *extras)
        if made:
            res = (a_blk,) + tuple(res)
        for ref, val in zip(out_refs, res[:no]):
            ref[...] = val.astype(ref.dtype)
        for ref, val in zip(sum_refs, res[no:]):
            ref[...] += val

        if ncomm:
            @pl.when(i == steps - 1)
            def _():
                _comm_wait(*_comm_copies(comm_in, comm_out, kinds, *sems))

    in_specs = ([] if made else [pl.BlockSpec((tm, kd), lambda i: (i, 0))]) + [_full_spec(b.shape)]
    in_specs += [pl.BlockSpec((tm, w), functools.partial(lambda i, cb: (i, cb), cb=cb)) for _, w, cb in tiles]
    in_specs += [_full_spec(r.shape) for r in rows] + [ANY] * ncomm
    out_specs = [pl.BlockSpec((tm, w), lambda i: (i, 0)) for w, _ in outs] + [_full_spec((1, w)) for w in sums] + [ANY] * ncomm
    out_shape = ([jax.ShapeDtypeStruct((m, w), dt) for w, dt in outs] + [jax.ShapeDtypeStruct((1, w), F32) for w in sums]
                 + _comm_out_shapes(comm, kinds))
    return pl.pallas_call(body, name=name, grid=(steps,), in_specs=in_specs, out_specs=out_specs, out_shape=out_shape,
                          scratch_shapes=_comm_scratch(ncomm) if ncomm else [],
                          compiler_params=_cparams(("arbitrary",)))(*([] if made else [a]), b, *[t[0] for t in tiles], *rows, *comm)


def _resid_norm(y, x, w):
    h = x + y
    return h, rms(h, w)


def _norm_back(dhn, h, dres, w):
    _, vjp = jax.vjp(rms, h, w)
    dh, dw = vjp(dhn)
    return dh + dres, dw


def _gate_back(dog, *o_and_gate):
    outs, g = o_and_gate[:-1], o_and_gate[-1]
    s = sigmoid(g)
    silu, dsilu = g * s, s * (1.0 + g * (1.0 - s))
    d_outs, c = [], 0
    for o in outs:
        w = o.shape[1]
        d_outs.append(dog[:, c:c + w] * silu[:, c:c + w])
        c += w
    o_all = outs[0] if len(outs) == 1 else jnp.concatenate(outs, axis=1)
    return (*d_outs, dog * o_all * dsilu)


def _loss_head(y1, h1, target, b_out, fw):
    def f(h2, w):
        err = rms(h2, w) - target
        return 0.5 * jnp.sum(jnp.mean(err * err, axis=-1, keepdims=True), axis=0, keepdims=True)

    loss, vjp = jax.vjp(f, h1 + y1 + b_out, fw)
    dh2, dfw = vjp(jnp.ones((1, 1), F32))
    return dh2, jnp.broadcast_to(loss, (1, LANES)), jnp.sum(dh2, axis=0, keepdims=True), dfw


def _gla_load(q_ref, k_ref, v_ref, gl_ref, up_ref, bias_ref, nw_ref):
    toks = (q_ref[...], k_ref[...], v_ref[...], gl_ref[...])
    params = (up_ref[...], bias_ref[...], nw_ref[...])
    return toks, params


def _gla_specs(c, n=None):
    toks = [_tok_spec(c, GLA_KEY, _col(GLA_KEY, "gq", C0), n), _tok_spec(c, GLA_KEY, _col(GLA_KEY, "gk", C0), n),
            _tok_spec(c, GLA_VAL, _col(GLA_VAL, "gv", C0), n), _tok_spec(c, LOW, _col(LOW, "glow", C0), n)]
    return toks, [_full_spec(s) for s in GLA_PARAM_SHAPES]


GLA_PARAM_SHAPES = [(LOW, GLA_KEY), (1, GLA_KEY), (1, GLA_DV)]
GLA_STATE = (GLA_HEADS * GLA_DV, GLA_DK)


def _gla_fwd(proj0, gk_up, gk_bias, norm_w):
    t = proj0.shape[0]
    c = GLA_STEP
    nc = t // c
    toks_s, params_s = _gla_specs(c)

    def body(q_ref, k_ref, v_ref, gl_ref, up_ref, bias_ref, nw_ref, o_ref, st_ref, s_scr):
        @pl.when(pl.program_id(0) == 0)
        def _():
            s_scr[...] = jnp.zeros_like(s_scr)

        st_ref[...] = s_scr[...]
        toks, params = _gla_load(q_ref, k_ref, v_ref, gl_ref, up_ref, bias_ref, nw_ref)
        state = [s_scr[h * GLA_DV:(h + 1) * GLA_DV, :] for h in range(GLA_HEADS)]
        o_ref[...], new = gla_chunk(state, toks, params)
        for h in range(GLA_HEADS):
            s_scr[h * GLA_DV:(h + 1) * GLA_DV, :] = new[h]

    return pl.pallas_call(
        body, name="gla_fwd", grid=(nc,), in_specs=toks_s + params_s,
        out_specs=(_tok_spec(c, GLA_VAL, 0), pl.BlockSpec(GLA_STATE, lambda i: (i, 0))),
        out_shape=(jax.ShapeDtypeStruct((t, GLA_VAL), F32), jax.ShapeDtypeStruct((nc * GLA_STATE[0], GLA_DK), F32)),
        scratch_shapes=[pltpu.VMEM(GLA_STATE, F32)], compiler_params=_cparams(("arbitrary",)))(
            proj0, proj0, proj0, proj0, gk_up, gk_bias, norm_w)


def _gla_bwd(proj0, gk_up, gk_bias, norm_w, states, do):
    t = proj0.shape[0]
    c = GLA_STEP
    nc = t // c
    toks_s, params_s = _gla_specs(c, nc)

    def body(q_ref, k_ref, v_ref, gl_ref, up_ref, bias_ref, nw_ref, st_ref, do_ref,
             dq_ref, dk_ref, dv_ref, dgl_ref, dup_ref, dbias_ref, dnw_ref, ds_scr):
        @pl.when(pl.program_id(0) == 0)
        def _():
            ds_scr[...] = jnp.zeros_like(ds_scr)
            dup_ref[...] = jnp.zeros_like(dup_ref)
            dbias_ref[...] = jnp.zeros_like(dbias_ref)
            dnw_ref[...] = jnp.zeros_like(dnw_ref)

        toks, params = _gla_load(q_ref, k_ref, v_ref, gl_ref, up_ref, bias_ref, nw_ref)
        rows = lambda h: slice(h * GLA_DV, (h + 1) * GLA_DV)
        state = [st_ref[rows(h), :] for h in range(GLA_HEADS)]
        _, vjp = jax.vjp(gla_chunk, state, toks, params)
        dstate_in = [ds_scr[rows(h), :] for h in range(GLA_HEADS)]
        dstate, dtoks, (dup, dbias, dnw) = vjp((do_ref[...], dstate_in))
        for ref, val in zip((dq_ref, dk_ref, dv_ref, dgl_ref), dtoks):
            ref[...] = val.astype(ref.dtype)
        dup_ref[...] += dup
        dbias_ref[...] += dbias
        dnw_ref[...] += dnw
        for h in range(GLA_HEADS):
            ds_scr[rows(h), :] = dstate[h]

    rev = lambda w: pl.BlockSpec((c, w), lambda i: (nc - 1 - i, 0))
    tok_widths = (GLA_KEY, GLA_KEY, GLA_VAL, LOW)
    return pl.pallas_call(
        body, name="gla_bwd", grid=(nc,),
        in_specs=toks_s + params_s + [pl.BlockSpec(GLA_STATE, lambda i: (nc - 1 - i, 0)), rev(GLA_VAL)],
        out_specs=[rev(w) for w in tok_widths] + params_s,
        out_shape=[jax.ShapeDtypeStruct((t, w), BF16) for w in tok_widths] + [jax.ShapeDtypeStruct(s, F32) for s in GLA_PARAM_SHAPES],
        scratch_shapes=[pltpu.VMEM(GLA_STATE, F32)], compiler_params=_cparams(("arbitrary",)))(
            proj0, proj0, proj0, proj0, gk_up, gk_bias, norm_w, states, do)


RWKV_PARAM_SHAPES = [(1, RWKV_W), (1, RWKV_W), (1, RWKV_W), (1, LOW), (1, LOW), (1, RWKV_W), (LOW, RWKV_W), (1, RWKV_W),
                     (LOW, RWKV_W), (1, RWKV_W), (1, RWKV_W), (1, RWKV_W), (1, RWKV_W), (1, RWKV_W)]
RWKV_STATE = (RWKV_HEADS * RWKV_N, RWKV_N)
RWKV_TOK_WIDTHS = (RWKV_W, RWKV_W, RWKV_W, LOW, LOW)
PREV_W = sum(RWKV_TOK_WIDTHS)
PREV_COLS = [slice(sum(RWKV_TOK_WIDTHS[:i]), sum(RWKV_TOK_WIDTHS[:i + 1])) for i in range(len(RWKV_TOK_WIDTHS))]


def _rwkv_load(r_ref, k_ref, v_ref, xw_ref, xa_ref, p_refs):
    toks = (r_ref[...], k_ref[...], v_ref[...], xw_ref[...], xa_ref[...])
    return toks, tuple(p[...] for p in p_refs)


def _rwkv_state(s_ref, prev_ref):
    n = RWKV_N
    S = [s_ref[h * n:(h + 1) * n, :] for h in range(RWKV_HEADS)]
    return (S,) + tuple(prev_ref[0:1, cols] for cols in PREV_COLS)


def _rwkv_put_state(s_ref, prev_ref, state):
    n = RWKV_N
    for h in range(RWKV_HEADS):
        s_ref[h * n:(h + 1) * n, :] = state[0][h]
    for cols, val in zip(PREV_COLS, state[1:]):
        prev_ref[0:1, cols] = val


def _rwkv_specs(c, n=None):
    toks = [_tok_spec(c, w, _col(w, name, C0), n) for name, w in zip(("r", "k", "v", "xw", "xa"), RWKV_TOK_WIDTHS)]
    return toks, [_full_spec(s) for s in RWKV_PARAM_SHAPES]


def _rwkv_fwd(proj0, params, comm, kinds):
    t = proj0.shape[0]
    c = RWKV_STEP
    nc = t // c
    toks_s, params_s = _rwkv_specs(c)
    npar, ncomm = len(params), len(comm)

    def body(*refs):
        tok_refs, p_refs = refs[:5], refs[5:5 + npar]
        comm_in = refs[5 + npar:5 + npar + ncomm]
        o_ref, st_ref, pst_ref = refs[5 + npar + ncomm:8 + npar + ncomm]
        comm_out = refs[8 + npar + ncomm:8 + npar + 2 * ncomm]
        s_scr, prev_scr = refs[8 + npar + 2 * ncomm:10 + npar + 2 * ncomm]
        sems = refs[10 + npar + 2 * ncomm:]
        i = pl.program_id(0)

        @pl.when(i == 0)
        def _():
            _comm_start(*_comm_copies(comm_in, comm_out, kinds, *sems))
            s_scr[...] = jnp.zeros_like(s_scr)
            prev_scr[...] = jnp.zeros_like(prev_scr)

        st_ref[...] = s_scr[...]
        pst_ref[...] = prev_scr[...]
        toks, prm = _rwkv_load(*tok_refs, p_refs)
        o_ref[...], new = rwkv_chunks(_rwkv_state(s_scr, prev_scr), toks, prm)
        _rwkv_put_state(s_scr, prev_scr, new)

        @pl.when(i == nc - 1)
        def _():
            _comm_wait(*_comm_copies(comm_in, comm_out, kinds, *sems))

    outs = pl.pallas_call(
        body, name="rwkv_fwd", grid=(nc,), in_specs=toks_s + params_s + [ANY] * ncomm,
        out_specs=[_tok_spec(c, RWKV_W, 0), pl.BlockSpec(RWKV_STATE, lambda i: (i, 0)), pl.BlockSpec((8, PREV_W), lambda i: (i, 0))]
        + [ANY] * ncomm,
        out_shape=[jax.ShapeDtypeStruct((t, RWKV_W), F32), jax.ShapeDtypeStruct((nc * RWKV_STATE[0], RWKV_N), F32),
                   jax.ShapeDtypeStruct((nc * 8, PREV_W), F32)] + _comm_out_shapes(comm, kinds),
        scratch_shapes=[pltpu.VMEM(RWKV_STATE, F32), pltpu.VMEM((8, PREV_W), F32)] + _comm_scratch(ncomm),
        compiler_params=_cparams(("arbitrary",)))(proj0, proj0, proj0, proj0, proj0, *params, *comm)
    return outs[0], outs[1], outs[2], outs[3:]


def _rwkv_bwd(proj0, params, states, prevs, do, comm, kinds):
    t = proj0.shape[0]
    c = RWKV_STEP
    nc = t // c
    toks_s, params_s = _rwkv_specs(c, nc)
    npar, ncomm = len(params), len(comm)

    def body(*refs):
        tok_refs, p_refs = refs[:5], refs[5:5 + npar]
        st_ref, pst_ref, do_ref = refs[5 + npar:8 + npar]
        comm_in = refs[8 + npar:8 + npar + ncomm]
        outs = refs[8 + npar + ncomm:]
        dtok_refs, dp_refs, comm_out = outs[:5], outs[5:5 + npar], outs[5 + npar:5 + npar + ncomm]
        ds_scr, dprev_scr = outs[5 + npar + ncomm:7 + npar + ncomm]
        sems = outs[7 + npar + ncomm:]
        i = pl.program_id(0)

        @pl.when(i == 0)
        def _():
            _comm_start(*_comm_copies(comm_in, comm_out, kinds, *sems))
            ds_scr[...] = jnp.zeros_like(ds_scr)
            dprev_scr[...] = jnp.zeros_like(dprev_scr)
            for dp in dp_refs:
                dp[...] = jnp.zeros_like(dp)

        toks, prm = _rwkv_load(*tok_refs, p_refs)
        _, vjp = jax.vjp(rwkv_chunks, _rwkv_state(st_ref, pst_ref), toks, prm)
        dstate, dtoks, dprm = vjp((do_ref[...], _rwkv_state(ds_scr, dprev_scr)))
        for ref, val in zip(dtok_refs, dtoks):
            ref[...] = val.astype(ref.dtype)
        for ref, val in zip(dp_refs, dprm):
            ref[...] += val
        _rwkv_put_state(ds_scr, dprev_scr, dstate)

        @pl.when(i == nc - 1)
        def _():
            _comm_wait(*_comm_copies(comm_in, comm_out, kinds, *sems))

    rev = lambda w: pl.BlockSpec((c, w), lambda i: (nc - 1 - i, 0))
    outs = pl.pallas_call(
        body, name="rwkv_bwd", grid=(nc,),
        in_specs=toks_s + params_s + [pl.BlockSpec(RWKV_STATE, lambda i: (nc - 1 - i, 0)),
                                      pl.BlockSpec((8, PREV_W), lambda i: (nc - 1 - i, 0)), rev(RWKV_W)] + [ANY] * ncomm,
        out_specs=[rev(w) for w in RWKV_TOK_WIDTHS] + params_s + [ANY] * ncomm,
        out_shape=[jax.ShapeDtypeStruct((t, w), BF16) for w in RWKV_TOK_WIDTHS]
        + [jax.ShapeDtypeStruct(s, F32) for s in RWKV_PARAM_SHAPES] + _comm_out_shapes(comm, kinds),
        scratch_shapes=[pltpu.VMEM(RWKV_STATE, F32), pltpu.VMEM((8, PREV_W), F32)] + _comm_scratch(ncomm),
        compiler_params=_cparams(("arbitrary",)))(proj0, proj0, proj0, proj0, proj0, *params, states, prevs, do, *comm)
    return outs[:5], outs[5:5 + npar], outs[5 + npar:]


def _swa_load(q_ref, k_ref, v_ref, cos_ref, sin_ref, bq_ref, bk_ref, bv_ref, sk_ref):
    toks = (q_ref[...], k_ref[...], v_ref[...], cos_ref[...], sin_ref[...])
    params = (bq_ref[...], bk_ref[...], bv_ref[...], _heads(sk_ref, SWA_Q_HEADS, 1))
    return toks, params


SWA_TOK_WIDTHS = (MIX, SWA_KV, SWA_KV)
SWA_PARAM_SHAPES = [(1, MIX), (1, SWA_KV), (1, SWA_KV), (1, SWA_Q_HEADS)]
SWA_STATE = (WINDOW, SWA_KV)


def _swa_specs(c, n=None):
    toks = [_tok_spec(c, w, _col(w, name, C1), n) for name, w in zip(("q", "k", "v"), SWA_TOK_WIDTHS)]
    toks += [_tok_spec(c, LANES, 0, n), _tok_spec(c, LANES, 0, n)]
    return toks, [_full_spec(s) for s in SWA_PARAM_SHAPES]


def _swa_fwd(proj1, cos, sin, bq, bk, bv, sinks):
    t = proj1.shape[0]
    c = SWA_STEP
    nb = t // c
    toks_s, params_s = _swa_specs(c)
    state_spec = pl.BlockSpec(SWA_STATE, lambda i: (i, 0))
    kv = SWA_KV_HEADS

    def body(q_ref, k_ref, v_ref, cos_ref, sin_ref, bq_ref, bk_ref, bv_ref, sk_ref, o_ref, kst_ref, vst_ref, k_scr, v_scr):
        first = pl.program_id(0) == 0

        @pl.when(first)
        def _():
            k_scr[...] = jnp.zeros_like(k_scr)
            v_scr[...] = jnp.zeros_like(v_scr)

        kst_ref[...] = k_scr[...]
        vst_ref[...] = v_scr[...]
        toks, params = _swa_load(q_ref, k_ref, v_ref, cos_ref, sin_ref, bq_ref, bk_ref, bv_ref, sk_ref)
        outs, (kn, vn) = swa_chunk((_heads(k_scr, kv, SWA_HD), _heads(v_scr, kv, SWA_HD)), toks, params, first)
        _put_heads(o_ref, outs, SWA_HD)
        _put_heads(k_scr, kn, SWA_HD)
        _put_heads(v_scr, vn, SWA_HD)

    saved = jax.ShapeDtypeStruct((nb * WINDOW, SWA_KV), F32)
    return pl.pallas_call(
        body, name="swa_fwd", grid=(nb,), in_specs=toks_s + params_s,
        out_specs=(_tok_spec(c, MIX, 0), state_spec, state_spec),
        out_shape=(jax.ShapeDtypeStruct((t, MIX), F32), saved, saved),
        scratch_shapes=[pltpu.VMEM(SWA_STATE, F32), pltpu.VMEM(SWA_STATE, F32)],
        compiler_params=_cparams(("arbitrary",)))(proj1, proj1, proj1, cos, sin, bq, bk, bv, sinks)


def _swa_bwd(proj1, cos, sin, bq, bk, bv, sinks, kst, vst, do):
    t = proj1.shape[0]
    c = SWA_STEP
    nb = t // c
    toks_s, params_s = _swa_specs(c, nb)
    state_spec = pl.BlockSpec(SWA_STATE, lambda i: (nb - 1 - i, 0))
    kv = SWA_KV_HEADS

    def body(q_ref, k_ref, v_ref, cos_ref, sin_ref, bq_ref, bk_ref, bv_ref, sk_ref, kst_ref, vst_ref, do_ref,
             dq_ref, dk_ref, dv_ref, dbq_ref, dbk_ref, dbv_ref, dsk_ref, dk_scr, dv_scr):
        i = pl.program_id(0)

        @pl.when(i == 0)
        def _():
            dk_scr[...] = jnp.zeros_like(dk_scr)
            dv_scr[...] = jnp.zeros_like(dv_scr)
            for ref in (dbq_ref, dbk_ref, dbv_ref, dsk_ref):
                ref[...] = jnp.zeros_like(ref)

        first = i == nb - 1
        toks, params = _swa_load(q_ref, k_ref, v_ref, cos_ref, sin_ref, bq_ref, bk_ref, bv_ref, sk_ref)
        f = functools.partial(swa_chunk, first=first)
        _, vjp = jax.vjp(f, (_heads(kst_ref, kv, SWA_HD), _heads(vst_ref, kv, SWA_HD)), toks, params)
        dstate_in = (_heads(dk_scr, kv, SWA_HD), _heads(dv_scr, kv, SWA_HD))
        (dkp, dvp), (dq, dk, dv, _, _), (dbq, dbk, dbv, dsk) = vjp((_heads(do_ref, SWA_Q_HEADS, SWA_HD), dstate_in))
        dq_ref[...], dk_ref[...], dv_ref[...] = dq.astype(BF16), dk.astype(BF16), dv.astype(BF16)
        dbq_ref[...] += dbq
        dbk_ref[...] += dbk
        dbv_ref[...] += dbv
        _put_heads(dsk_ref, dsk, 1, add=True)
        _put_heads(dk_scr, dkp, SWA_HD)
        _put_heads(dv_scr, dvp, SWA_HD)

    rev = lambda w: pl.BlockSpec((c, w), lambda i: (nb - 1 - i, 0))
    return pl.pallas_call(
        body, name="swa_bwd", grid=(nb,), in_specs=toks_s + params_s + [state_spec, state_spec, rev(MIX)],
        out_specs=[rev(w) for w in SWA_TOK_WIDTHS] + params_s,
        out_shape=[jax.ShapeDtypeStruct((t, w), BF16) for w in SWA_TOK_WIDTHS] + [jax.ShapeDtypeStruct(s, F32) for s in SWA_PARAM_SHAPES],
        scratch_shapes=[pltpu.VMEM(SWA_STATE, F32), pltpu.VMEM(SWA_STATE, F32)],
        compiler_params=_cparams(("arbitrary",)))(proj1, proj1, proj1, cos, sin, bq, bk, bv, sinks, kst, vst, do)


MESH = pl.DeviceIdType.MESH
ANY = pl.BlockSpec(memory_space=pl.ANY)


def _my_place():
    return lax.axis_index("x"), lax.axis_index("y"), lax.axis_index("c")


def _all_gather(shards):
    n = len(shards)

    def body(*refs):
        in_refs, out_refs = refs[:n], refs[n:2 * n]
        send_sems, recv_sems, local_sems = refs[2 * n:]
        x, y, c = _my_place()
        me, sibling = (x, y, c), (x, y, 1 - c)
        chips = [(1 - x, y), (x, 1 - y), (1 - x, 1 - y)]

        def slot(out_ref, place):
            px, py, pc = place
            return out_ref.at[4 * px + 2 * py + pc]

        def copy(a, k, block, to, src=None):
            return pltpu.make_async_remote_copy(
                src_ref=slot(out_refs[a], block) if src is None else src, dst_ref=slot(out_refs[a], block),
                send_sem=send_sems.at[a, k], recv_sem=recv_sems.at[a, k], device_id=to, device_id_type=MESH)

        mine = [pltpu.make_async_copy(in_refs[a], slot(out_refs[a], me), local_sems.at[a]) for a in range(n)]
        for cp in mine:
            cp.start()
        first = []
        for a in range(n):
            first.append(copy(a, 0, me, sibling, src=in_refs[a]))
            first += [copy(a, 1 + j, me, (*chip, c), src=in_refs[a]) for j, chip in enumerate(chips)]
        for cp in first:
            cp.start()
        passed = []
        for j, chip in enumerate(chips):
            for a in range(n):
                copy(a, 1 + j, (*chip, c), me).wait_recv()
                fwd = copy(a, 4 + j, (*chip, c), sibling)
                fwd.start()
                passed.append(fwd)
        for a in range(n):
            copy(a, 0, sibling, me).wait_recv()
            for j, chip in enumerate(chips):
                copy(a, 4 + j, (*chip, 1 - c), me).wait_recv()
        for cp in first + passed:
            cp.wait_send()
        for cp in mine:
            cp.wait()

    return pl.pallas_call(
        body, name="all_gather_weights", in_specs=[ANY] * n, out_specs=[ANY] * n,
        out_shape=[jax.ShapeDtypeStruct((N_DEV,) + s.shape, s.dtype) for s in shards],
        scratch_shapes=_comm_scratch(n))(*shards)


def _comm_copies(in_refs, out_refs, kinds, send_sems, recv_sems, local_sems):
    x, y, c = _my_place()
    my_idx = 4 * x + 2 * y + c
    src = lambda a, idx: in_refs[a] if kinds[a] == "gather" else in_refs[a].at[idx]
    local = [pltpu.make_async_copy(src(a, my_idx), out_refs[a].at[my_idx], local_sems.at[a]) for a in range(len(kinds))]
    remote = []
    for rel in range(1, N_DEV):
        px, py, pc = x ^ ((rel >> 2) & 1), y ^ ((rel >> 1) & 1), c ^ (rel & 1)
        for a in range(len(kinds)):
            remote.append(pltpu.make_async_remote_copy(
                src_ref=src(a, 4 * px + 2 * py + pc), dst_ref=out_refs[a].at[my_idx], send_sem=send_sems.at[a, rel - 1],
                recv_sem=recv_sems.at[a, rel - 1], device_id=(px, py, pc), device_id_type=MESH))
    return local, remote


def _comm_start(local, remote):
    for cp in local + remote:
        cp.start()


def _comm_wait(local, remote):
    for cp in remote:
        cp.wait_recv()
    for cp in remote:
        cp.wait_send()
    for cp in local:
        cp.wait()


def _comm_out_shapes(arrays, kinds):
    return [jax.ShapeDtypeStruct(((N_DEV,) + a.shape) if k == "gather" else a.shape, a.dtype) for a, k in zip(arrays, kinds)]


def _comm_scratch(n):
    return [pltpu.SemaphoreType.DMA((n, N_DEV - 1)), pltpu.SemaphoreType.DMA((n, N_DEV - 1)), pltpu.SemaphoreType.DMA((n,))]


def _sequencer_scatter(name, parts, collective_id):
    src = jax.new_ref(parts, memory_space=pltpu.MemorySpace.HBM)
    dst = jax.empty_ref(jax.ShapeDtypeStruct(parts.shape, parts.dtype), memory_space=pltpu.MemorySpace.HBM)

    @pl.kernel(mesh=plsc.ScalarSubcoreMesh(axis_name="sequencer", num_cores=1), name=name,
               scratch_types=(pltpu.SemaphoreType.DMA((N_DEV - 1,)), pltpu.SemaphoreType.DMA((N_DEV - 1,))),
               compiler_params=pltpu.CompilerParams(collective_id=collective_id))
    def launch(send_sems, recv_sems):
        x, y, c = _my_place()
        my_idx = 4 * x + 2 * y + c
        peers = [(x ^ ((rel >> 2) & 1), y ^ ((rel >> 1) & 1), c ^ (rel & 1)) for rel in range(1, N_DEV)]
        barrier = pltpu.get_barrier_semaphore()
        for peer in peers:
            pl.semaphore_signal(barrier, inc=1, device_id=peer, device_id_type=MESH)
        pl.semaphore_wait(barrier, N_DEV - 1)
        copies = [pltpu.make_async_remote_copy(
            src_ref=src.at[4 * px + 2 * py + pc], dst_ref=dst.at[my_idx], send_sem=send_sems.at[k], recv_sem=recv_sems.at[k],
            device_id=(px, py, pc), device_id_type=MESH) for k, (px, py, pc) in enumerate(peers)]
        for cp in copies:
            cp.start()
        for cp in copies:
            cp.wait_recv()
        for cp in copies:
            cp.wait_send()

    launch()
    return dst[...]


def _exchange(arrays, kinds):
    n = len(arrays)

    def body(*refs):
        copies = _comm_copies(refs[:n], refs[n:2 * n], kinds, *refs[2 * n:])
        _comm_start(*copies)
        _comm_wait(*copies)

    return pl.pallas_call(body, name="exchange_grads", in_specs=[ANY] * n, out_specs=[ANY] * n,
                          out_shape=_comm_out_shapes(arrays, kinds), scratch_shapes=_comm_scratch(n))(*arrays)


def _adam_math(w, g, m, v):
    m = ADAM_B1 * m + (1.0 - ADAM_B1) * g
    v = ADAM_B2 * v + (1.0 - ADAM_B2) * (g * g)
    m_hat = m / (1.0 - ADAM_B1 ** ADAM_STEP)
    v_hat = v / (1.0 - ADAM_B2 ** ADAM_STEP)
    delta = -ADAM_LR * (m_hat / (jnp.sqrt(v_hat) + ADAM_EPS) + ADAM_WD * w)
    return delta, m, v


def _adamw(name, w, gslots, m, v, tc):
    r, cc = w.shape
    assert cc % tc == 0
    tile = pl.BlockSpec((r, tc), lambda i: (0, i))

    def body(w_ref, g_ref, m_ref, v_ref, go_ref, d_ref, mo_ref, vo_ref):
        g = g_ref[0].astype(F32)
        for s in range(1, N_DEV):
            g = g + g_ref[s].astype(F32)
        d, mn, vn = _adam_math(w_ref[...], g, m_ref[...], v_ref[...])
        go_ref[...] = g
        d_ref[...] = d
        mo_ref[...] = mn
        vo_ref[...] = vn

    shp = jax.ShapeDtypeStruct((r, cc), F32)
    return pl.pallas_call(body, name=name, grid=(cc // tc,),
                          in_specs=[tile, pl.BlockSpec((N_DEV, r, tc), lambda i: (0, 0, i)), tile, tile],
                          out_specs=(tile,) * 4, out_shape=(shp,) * 4, compiler_params=_cparams(("arbitrary",)))(w, gslots, m, v)


PACK_TILE = 8 * LANES


def _packed_rows(shape, mode):
    r, w = shape
    return -(-r // 8) * 8 if mode == "rows" else -(-(r * w) // PACK_TILE) * 8


def _pack_small(arrays, modes, lead=False):
    out = []
    for a, mode in zip(arrays, modes):
        a = a.astype(F32) if lead else a.astype(F32)[None]
        if mode == "rows":
            out.append(jnp.pad(a, ((0, 0), (0, (-a.shape[1]) % 8), (0, LANES - a.shape[2]))))
        else:
            flat = a.reshape(a.shape[0], -1)
            out.append(jnp.pad(flat, ((0, 0), (0, (-flat.shape[1]) % PACK_TILE))).reshape(a.shape[0], -1, LANES))
    out = jnp.concatenate(out, axis=1)
    return out if lead else out[0]


def _take_small(packed, row0, shape, mode):
    r, w = shape
    lead = packed.ndim == 3
    if mode == "rows":
        return packed[:, row0:row0 + r, :w] if lead else packed[row0:row0 + r, :w]
    per_row = -(-w // LANES)
    if lead:
        return packed[:, row0:row0 + r * per_row].reshape(packed.shape[0], r, per_row * LANES)[:, :, :w]
    rows = []
    for i in range(r):
        pieces = [packed[row0 + i * per_row + j:row0 + i * per_row + j + 1, :] for j in range(per_row)]
        rows.append((pieces[0] if per_row == 1 else jnp.concatenate(pieces, axis=1))[:, :w])
    return rows[0] if r == 1 else jnp.concatenate(rows, axis=0)


def _adamw_small(slots, specs, ws, ms, vs, loss_row):
    n = len(specs)

    def body(*refs):
        slots_ref, w_refs, m_refs, v_refs = refs[0], refs[1:1 + n], refs[1 + n:1 + 2 * n], refs[1 + 2 * n:1 + 3 * n]
        out_refs, loss_ref = refs[1 + 3 * n:1 + 7 * n], refs[1 + 7 * n]
        gp = slots_ref[0]
        for s in range(1, N_DEV):
            gp = gp + slots_ref[s]
        read = lambda ref: ref[0] if len(ref.shape) == 3 else ref[...]
        for k, (shape, mode, row0) in enumerate(specs):
            g = _take_small(gp, row0, shape, mode)
            d, mn, vn = _adam_math(read(w_refs[k]), g, read(m_refs[k]), read(v_refs[k]))
            for ref, val in zip(out_refs[4 * k:4 * k + 4], (g, d, mn, vn)):
                if len(ref.shape) == 3:
                    ref[0] = val
                else:
                    ref[...] = val
        loss_ref[...] = gp[loss_row:loss_row + 1, :]

    vmem = pl.BlockSpec(memory_space=pltpu.VMEM)
    out_shape = [jax.ShapeDtypeStruct(w.shape, F32) for w in ws for _ in range(4)] + [jax.ShapeDtypeStruct((1, LANES), F32)]
    outs = pl.pallas_call(body, name="adamw_small", in_specs=[vmem] * (1 + 3 * n), out_specs=[vmem] * (4 * n + 1),
                          out_shape=out_shape)(slots, *ws, *ms, *vs)
    return [outs[4 * k:4 * k + 4] for k in range(n)], outs[4 * n]


def _rope_tables(t):
    dim = jnp.arange(LANES) % SWA_HD
    inv_freq = ROPE_THETA ** (-(dim % ROPE_HALF).astype(F32) / ROPE_HALF)
    ang = jnp.arange(t, dtype=F32)[:, None] * jnp.where(dim < 2 * ROPE_HALF, inv_freq, 0.0)[None, :]
    return jnp.cos(ang), jnp.sin(ang)


def _pad_to(a, rows=None, cols=None):
    r = 0 if rows is None else rows - a.shape[0]
    c = 0 if cols is None else cols - a.shape[1]
    return jnp.pad(a, ((0, r), (0, c)))


ORIG0 = dict(gq=(0, 256), gk=(256, 256), gv=(512, 512), glow=(1024, 16), r=(1040, 512), k=(1552, 512), v=(2064, 512),
             xw=(2576, 64), xa=(2640, 64), gate=(2704, 1024))
ORIG0_ORDER = ["gq", "gk", "gv", "glow", "r", "k", "v", "xw", "xa", "gate"]


def _w0t_to_padded(wt):
    rows, at = [], 0
    for name, (off, width) in sorted(C0.items(), key=lambda kv: kv[1][0]):
        assert off == at
        src, src_w = ORIG0[name]
        rows.append(_pad_to(wt[src:src + src_w], rows=width))
        at += width
    rows.append(jnp.zeros((N0P - at, wt.shape[1]), wt.dtype))
    return jnp.concatenate(rows, axis=0)


def _w0t_from_padded(wpt):
    return jnp.concatenate([wpt[C0[n][0]:C0[n][0] + ORIG0[n][1]] for n in ORIG0_ORDER], axis=0)


def _w1t_to_mine(wt):
    return jnp.concatenate([wt[1536:2560], wt[:1536]], axis=0)


def _w1t_from_mine(wt):
    return jnp.concatenate([wt[1024:2560], wt[:1024]], axis=0)


def kernel(x, norm_w, w_in0, gla_gk_up, gla_gk_bias, gla_norm_w, rwkv_mu, rwkv_w0, rwkv_w_up, rwkv_a0, rwkv_a_up, rwkv_k_k, rwkv_k_a, rwkv_r_k, rwkv_ln_w, rwkv_ln_b, w_out0, w_in1, b_in1, attn_sinks, w_out1, b_out1, final_norm_w, loss_target, m_norm_w, m_w_in0, m_gla_gk_up, m_gla_gk_bias, m_gla_norm_w, m_rwkv_mu, m_rwkv_w0, m_rwkv_w_up, m_rwkv_a0, m_rwkv_a_up, m_rwkv_k_k, m_rwkv_k_a, m_rwkv_r_k, m_rwkv_ln_w, m_rwkv_ln_b, m_w_out0, m_w_in1, m_b_in1, m_attn_sinks, m_w_out1, m_b_out1, m_final_norm_w, v_norm_w, v_w_in0, v_gla_gk_up, v_gla_gk_bias, v_gla_norm_w, v_rwkv_mu, v_rwkv_w0, v_rwkv_w_up, v_rwkv_a0, v_rwkv_a_up, v_rwkv_k_k, v_rwkv_k_a, v_rwkv_r_k, v_rwkv_ln_w, v_rwkv_ln_b, v_w_out0, v_w_in1, v_b_in1, v_attn_sinks, v_w_out1, v_b_out1, v_final_norm_w):
    weights = dict(norm_w=norm_w, w_in0=w_in0, gla_gk_up=gla_gk_up, gla_gk_bias=gla_gk_bias, gla_norm_w=gla_norm_w, rwkv_mu=rwkv_mu,
                   rwkv_w0=rwkv_w0, rwkv_w_up=rwkv_w_up, rwkv_a0=rwkv_a0, rwkv_a_up=rwkv_a_up, rwkv_k_k=rwkv_k_k, rwkv_k_a=rwkv_k_a,
                   rwkv_r_k=rwkv_r_k, rwkv_ln_w=rwkv_ln_w, rwkv_ln_b=rwkv_ln_b, w_out0=w_out0, w_in1=w_in1, b_in1=b_in1,
                   attn_sinks=attn_sinks, w_out1=w_out1, b_out1=b_out1, final_norm_w=final_norm_w)
    moms = dict(norm_w=m_norm_w, w_in0=m_w_in0, gla_gk_up=m_gla_gk_up, gla_gk_bias=m_gla_gk_bias, gla_norm_w=m_gla_norm_w,
                rwkv_mu=m_rwkv_mu, rwkv_w0=m_rwkv_w0, rwkv_w_up=m_rwkv_w_up, rwkv_a0=m_rwkv_a0, rwkv_a_up=m_rwkv_a_up,
                rwkv_k_k=m_rwkv_k_k, rwkv_k_a=m_rwkv_k_a, rwkv_r_k=m_rwkv_r_k, rwkv_ln_w=m_rwkv_ln_w, rwkv_ln_b=m_rwkv_ln_b,
                w_out0=m_w_out0, w_in1=m_w_in1, b_in1=m_b_in1, attn_sinks=m_attn_sinks, w_out1=m_w_out1, b_out1=m_b_out1,
                final_norm_w=m_final_norm_w)
    vars_ = dict(norm_w=v_norm_w, w_in0=v_w_in0, gla_gk_up=v_gla_gk_up, gla_gk_bias=v_gla_gk_bias, gla_norm_w=v_gla_norm_w,
                 rwkv_mu=v_rwkv_mu, rwkv_w0=v_rwkv_w0, rwkv_w_up=v_rwkv_w_up, rwkv_a0=v_rwkv_a0, rwkv_a_up=v_rwkv_a_up,
                 rwkv_k_k=v_rwkv_k_k, rwkv_k_a=v_rwkv_k_a, rwkv_r_k=v_rwkv_r_k, rwkv_ln_w=v_rwkv_ln_w, rwkv_ln_b=v_rwkv_ln_b,
                 w_out0=v_w_out0, w_in1=v_w_in1, b_in1=v_b_in1, attn_sinks=v_attn_sinks, w_out1=v_w_out1, b_out1=v_b_out1,
                 final_norm_w=v_final_norm_w)
    names = list(weights)
    big = ["w_in0", "w_out0", "w_in1", "w_out1"]
    small_sharded = ["gla_gk_up", "rwkv_w_up", "rwkv_a_up", "b_in1", "b_out1"]
    replicated = [n for n in names if n not in big and n not in small_sharded]

    xs = x[0]
    tgt = loss_target[0]
    t = xs.shape[0]

    def view(w):
        shape = tuple(w.shape[-2:]) if w.ndim >= 2 else (1, w.shape[0])
        return shape, ("rows" if shape[0] > 1 and shape[1] <= LANES else "flat")

    def layout(ns, row0=0):
        specs = []
        for n in ns:
            shape, mode = view(weights[n])
            specs.append((shape, mode, row0))
            row0 += _packed_rows(shape, mode)
        return specs, row0

    sh_specs, n_shard_rows = layout(small_sharded)
    rep_specs, loss_row = layout(replicated, n_shard_rows)
    sh_modes, rep_modes = [s[1] for s in sh_specs], [s[1] for s in rep_specs]

    small_shard_pack = _pack_small([weights[n].reshape(view(weights[n])[0]) for n in small_sharded], sh_modes)
    g_in0, g_small = _all_gather([w_in0[0].T.astype(BF16), small_shard_pack])
    w0t = _w0t_to_padded(g_in0.reshape(-1, D_MODEL))
    later_shards = [w_out0[0].astype(BF16), w_in1[0].T.astype(BF16), w_out1[0].astype(BF16)]
    gs = [_take_small(g_small, row0, shape, mode) for shape, mode, row0 in sh_specs]
    join_cols = lambda a: jnp.transpose(a, (1, 0, 2)).reshape(a.shape[1], -1)
    gk_up, w_up, a_up = join_cols(gs[0]), join_cols(gs[1]), join_cols(gs[2])
    b_in, b_out = gs[3].reshape(1, -1), gs[4].reshape(1, -1)

    gk_up_p = _pad_to(gk_up, rows=LOW)
    w3, rank = 3 * RWKV_W, rwkv_w_up.shape[1]
    mu = rwkv_mu
    rwkv_params = [mu[:, 0:RWKV_W], mu[:, RWKV_W:2 * RWKV_W], mu[:, 2 * RWKV_W:w3], _pad_to(mu[:, w3:w3 + rank], cols=LOW),
                   _pad_to(mu[:, w3 + rank:], cols=LOW), rwkv_w0, _pad_to(w_up, rows=LOW), rwkv_a0, _pad_to(a_up, rows=LOW),
                   rwkv_k_k, rwkv_k_a, rwkv_r_k.reshape(1, RWKV_W), rwkv_ln_w, rwkv_ln_b]
    bq, bk, bv = b_in[:, :MIX], b_in[:, MIX:MIX + SWA_KV], b_in[:, MIX + SWA_KV:]
    cos, sin = _rope_tables(t)
    nw0, nw1, fw = norm_w[0:1], norm_w[1:2], final_norm_w.reshape(1, D_MODEL)

    d = D_MODEL
    wide = lambda arr: (arr, d, 0)
    silu = lambda g: g * sigmoid(g)
    hn0, proj0 = _matmul_fused("norm0_proj0", rms, w0t, "nt", [wide(xs)], [nw0], [(N0P, F32)], [], lambda acc, x, w: (acc,))
    o_a, gla_states = _gla_fwd(proj0, gk_up_p, gla_gk_bias, gla_norm_w)
    o_b, rwkv_states, rwkv_prevs, (g_out0, g_in1, g_out1) = _rwkv_fwd(proj0, rwkv_params, later_shards, ["gather"] * 3)
    wo0 = g_out0.reshape(MIX, D_MODEL)
    w1t = _w1t_to_mine(g_in1.reshape(-1, D_MODEL))
    wo1 = g_out1.reshape(MIX, D_MODEL)
    og0, h1, hn1 = _matmul_fused(
        "gate0_out0_norm1", lambda oa, ob, gate, x, w: jnp.concatenate([oa, ob], axis=1) * silu(gate), wo0, "nn",
        [(o_a, GLA_VAL, 0), (o_b, RWKV_W, 0), wide(proj0), wide(xs)], [nw1], [(d, F32), (d, BF16)], [],
        lambda acc, oa, ob, gate, x, w: _resid_norm(acc, x, w))
    proj1 = _matmul("proj1", hn1, w1t, "nt", PROJ_ROWS, N1P // 2)
    o_c, kst, vst = _swa_fwd(proj1, cos, sin, bq, bk, bv, attn_sinks)
    og1, dh2, loss_part, d_b_out, d_fw = _matmul_fused(
        "gate1_out1_loss", lambda oc, gate, h, tg, b, w: oc * silu(gate), wo1, "nn",
        [wide(o_c), wide(proj1), wide(h1), wide(tgt)], [b_out, fw], [(d, F32)], [LANES, d, d],
        lambda acc, oc, gate, h, tg, b, w: _loss_head(acc, h, tg, b, w))

    d_oc, d_gate1 = _matmul_fused("out1_dx_gate1", dh2, wo1, "nt", [wide(o_c), wide(proj1)], [], [(d, F32), (d, BF16)], [], _gate_back)
    d_wo1 = _matmul("out1_dw", og1, dh2, "tn", DW_COLS, DW_COLS, BF16)
    dq, dk, dv, d_bq, d_bk, d_bv, d_sinks = _swa_bwd(proj1, cos, sin, bq, bk, bv, attn_sinks, kst, vst, d_oc)
    dproj1 = jnp.concatenate([d_gate1, dq, dk, dv], axis=1)
    dh1, d_nw1 = _matmul_fused("proj1_dx_norm1", dproj1, w1t, "nn", [wide(h1), wide(dh2)], [nw1], [(d, F32)], [d], _norm_back)
    d_w1t = _matmul("proj1_dw", dproj1, hn1, "tn", DW_COLS, d, BF16)
    d_oa, d_ob, d_gate0 = _matmul_fused("out0_dx_gate0", dh1, wo0, "nt", [(o_a, GLA_VAL, 0), (o_b, RWKV_W, 0), wide(proj0)], [],
                                        [(GLA_VAL, F32), (RWKV_W, F32), (d, BF16)], [], _gate_back)
    d_wo0 = _matmul("out0_dw", og0, dh1, "tn", DW_COLS, DW_COLS, BF16)
    dgq, dgk, dgv, dglow, d_gk_up, d_gk_bias, d_gla_nw = _gla_bwd(proj0, gk_up_p, gla_gk_bias, gla_norm_w, gla_states, d_oa)
    row_blocks = lambda a: a.astype(BF16).reshape(N_DEV, -1, D_MODEL)
    early = [row_blocks(_w1t_from_mine(d_w1t)), row_blocks(d_wo1), row_blocks(d_wo0)]
    (dr, dkk, dvv, dxw, dxa), d_rp, (r_in1, r_out1, r_out0) = _rwkv_bwd(
        proj0, rwkv_params, rwkv_states, rwkv_prevs, d_ob, early, ["scatter"] * 3)
    pad = jnp.zeros((t, N0P - C0["xa"][0] - C0["xa"][1]), BF16)
    dproj0 = jnp.concatenate([d_gate0, dgv, dr, dkk, dvv, dgq, dgk, dglow, dxw, dxa, pad], axis=1)
    d_w0 = row_blocks(_w0t_from_padded(_matmul("proj0_dw", dproj0, hn0, "tn", DW_COLS, d, BF16)))
    r_in0 = _sequencer_scatter("exchange_w_in0_grad", d_w0, 0)
    res = {}
    res["w_out0"] = tuple(a[None] for a in _adamw("adamw_w_out0", w_out0[0], r_out0, m_w_out0[0], v_w_out0[0], ADAM_COLS))
    res["w_in1"] = tuple(a.T[None] for a in _adamw("adamw_w_in1", w_in1[0].T, r_in1, m_w_in1[0].T, v_w_in1[0].T, ADAM_COLS))
    res["w_out1"] = tuple(a[None] for a in _adamw("adamw_w_out1", w_out1[0], r_out1, m_w_out1[0], v_w_out1[0], ADAM_COLS))
    grad_x, d_nw0 = _matmul_fused("proj0_dx_norm0", dproj0, w0t, "nn", [wide(xs), wide(dh1)], [nw0], [(d, F32)], [d], _norm_back)

    contrib = dict(
        norm_w=jnp.concatenate([d_nw0, d_nw1], axis=0), gla_gk_bias=d_gk_bias, gla_norm_w=d_gla_nw,
        rwkv_mu=jnp.concatenate([d_rp[0], d_rp[1], d_rp[2], d_rp[3][:, :rank], d_rp[4][:, :rank]], axis=1),
        rwkv_w0=d_rp[5], rwkv_a0=d_rp[7], rwkv_k_k=d_rp[9], rwkv_k_a=d_rp[10], rwkv_r_k=d_rp[11].reshape(RWKV_HEADS, RWKV_N),
        rwkv_ln_w=d_rp[12], rwkv_ln_b=d_rp[13], attn_sinks=d_sinks, final_norm_w=d_fw)
    rep_pack = _pack_small([contrib[n] for n in replicated] + [loss_part[:, :1]], rep_modes + ["flat"])

    d_b_in = jnp.concatenate([d_bq, d_bk, d_bv], axis=1)
    full_small = [d_gk_up[:gk_up.shape[0]], d_rp[6][:rank], d_rp[8][:rank], d_b_in, d_b_out]
    split_cols = lambda a: jnp.transpose(a.reshape(a.shape[0], N_DEV, -1), (1, 0, 2))
    small_parts = [split_cols(a) for a in full_small]
    small_pack = _pack_small(small_parts, sh_modes, lead=True)
    r_small, r_rep = _exchange([small_pack, rep_pack], ["scatter", "gather"])

    small_names = small_sharded + replicated
    slots = jnp.concatenate([r_small, r_rep], axis=1)
    as_2d = lambda a: a.reshape(1, -1) if a.ndim == 1 else a
    small_res, loss_row_out = _adamw_small(slots, sh_specs + rep_specs, [as_2d(weights[n]) for n in small_names],
                                           [as_2d(moms[n]) for n in small_names], [as_2d(vars_[n]) for n in small_names], loss_row)
    for n, vals in zip(small_names, small_res):
        res[n] = tuple(val.reshape(weights[n].shape) for val in vals)
    loss = loss_row_out[0, 0]
    my_idx = 4 * lax.axis_index("x") + 2 * lax.axis_index("y") + lax.axis_index("c")
    r_in0 = lax.dynamic_update_slice(r_in0, lax.dynamic_slice(d_w0, (my_idx, 0, 0), (1,) + d_w0.shape[1:]), (my_idx, 0, 0))
    res["w_in0"] = tuple(a.T[None] for a in _adamw("adamw_w_in0", w_in0[0].T, r_in0, m_w_in0[0].T, v_w_in0[0].T, ADAM_COLS))
    return (loss, grad_x[None], *[res[n][0] for n in names], *[res[n][1] for n in names],
            *[res[n][2] for n in names], *[res[n][3] for n in names])
```

```python
import functools

import jax
import jax.numpy as jnp
from jax import lax
from jax.experimental import pallas as pl
from jax.experimental.pallas import tpu as pltpu
from jax.experimental.pallas import tpu_sc as plsc

F32 = jnp.float32
BF16 = jnp.bfloat16
HI = lax.Precision.HIGHEST

D_MODEL = 1024
NORM_EPS = 1e-5
GLA_HEADS, GLA_DK, GLA_DV = 4, 64, 128
GLA_NORMALIZER = 16.0
GLA_CHUNK = 64
GLA_STEP = 1024
RWKV_HEADS, RWKV_N = 8, 64
RWKV_LN_EPS = 64e-5
RWKV_CHUNK = 128
SWA_Q_HEADS, SWA_KV_HEADS, SWA_GROUP, SWA_HD = 16, 4, 4, 64
WINDOW = 128
SWA_STEP = 512
ROPE_THETA = 500000.0
NEG = -1e30
N_DEV = 8
LANES = 128

ADAM_LR, ADAM_B1, ADAM_B2, ADAM_EPS, ADAM_WD, ADAM_STEP = 0.001, 0.9, 0.999, 1e-08, 0.01, 10

GLA_KEY, GLA_VAL = GLA_HEADS * GLA_DK, GLA_HEADS * GLA_DV
RWKV_W = RWKV_HEADS * RWKV_N
SWA_KV = SWA_KV_HEADS * SWA_HD
MIX = GLA_VAL + RWKV_W
LOW = LANES

N0P = 4096
C0 = dict(gate=(0, MIX), gv=(1024, GLA_VAL), r=(1536, RWKV_W), k=(2048, RWKV_W), v=(2560, RWKV_W), gq=(3072, GLA_KEY),
          gk=(3328, GLA_KEY), glow=(3584, LOW), xw=(3712, LOW), xa=(3840, LOW))
N1P = 2560
C1 = dict(gate=(0, MIX), q=(1024, MIX), k=(2048, SWA_KV), v=(2304, SWA_KV))

VMEM_LIMIT = 56 * 1024 * 1024

P_LORA = 1
P_GLA = 1
P_RWKV_G = 2
P_RWKV = 1
P_SWA = 1


def _cparams(sem=None):
    return pltpu.CompilerParams(dimension_semantics=sem, vmem_limit_bytes=VMEM_LIMIT)


DIMS = dict(nn=(((1,), (0,)), ((), ())), nt=(((1,), (1,)), ((), ())), tn=(((0,), (0,)), ((), ())))


def _split_bf16(a):
    hi = a.astype(BF16)
    return hi, (a - hi.astype(F32)).astype(BF16)


def _dot(a, b, mode, passes):
    dg = lambda p, q: lax.dot_general(p, q, DIMS[mode], preferred_element_type=F32)
    if passes == 1:
        return dg(a.astype(BF16), b.astype(BF16))
    if passes == 2:
        ah, (bh, bl) = a.astype(BF16), _split_bf16(b)
        return dg(ah, bh) + dg(ah, bl)
    if passes == 3:
        (ah, al), (bh, bl) = _split_bf16(a), _split_bf16(b)
        return dg(ah, bh) + dg(al, bh) + dg(ah, bl)
    return lax.dot_general(a, b, DIMS[mode], precision=HI, preferred_element_type=F32)


@functools.partial(jax.custom_vjp, nondiff_argnums=(2, 3))
def mmx(a, b, mode, passes):
    return _dot(a, b, mode, passes)


def _mmx_fwd(a, b, mode, passes):
    return _dot(a, b, mode, passes), (a, b)


def _mmx_bwd(mode, passes, res, g):
    a, b = res
    if mode == "nn":
        return _dot(g, b, "nt", passes), _dot(a, g, "tn", passes)
    if mode == "nt":
        return _dot(g, b, "nn", passes), _dot(g, a, "tn", passes)
    return _dot(b, g, "nt", passes), _dot(a, g, "nn", passes)


mmx.defvjp(_mmx_fwd, _mmx_bwd)


def _tri_dot(tri, x):
    t = tri.astype(BF16)
    x1 = x.astype(BF16)
    r1 = x - x1.astype(F32)
    x2 = r1.astype(BF16)
    x3 = (r1 - x2.astype(F32)).astype(BF16)
    dg = lambda q: jnp.dot(t, q, preferred_element_type=F32)
    return dg(x1) + dg(x2) + dg(x3)


@jax.custom_vjp
def cumsum_rows(x):
    return _tri_dot(tril_ones(x.shape[0]), x)


def _cumsum_fwd(x):
    return cumsum_rows(x), None


def _cumsum_bwd(_, g):
    i, j = _iota2(g.shape[0], g.shape[0])
    return (_tri_dot(jnp.where(i <= j, 1.0, 0.0).astype(F32), g),)


cumsum_rows.defvjp(_cumsum_fwd, _cumsum_bwd)


def _head_dot(x):
    i, j = _iota2(LANES, LANES)
    shift = RWKV_N.bit_length() - 1
    same = jnp.where(jnp.right_shift(i, shift) == jnp.right_shift(j, shift), 1.0, 0.0).astype(F32)
    return jnp.concatenate([_ones_right(x[:, g * LANES:(g + 1) * LANES], same) for g in range(x.shape[1] // LANES)], axis=1)


def _ones_right(x, ones):
    t = ones.astype(BF16)
    x1 = x.astype(BF16)
    x2 = (x - x1.astype(F32)).astype(BF16)
    dg = lambda q: jnp.dot(q, t, preferred_element_type=F32)
    return dg(x1) + dg(x2)


@jax.custom_vjp
def head_sum(x):
    return _head_dot(x)


def _head_sum_fwd(x):
    return head_sum(x), None


def _head_sum_bwd(_, g):
    return (_head_dot(g),)


head_sum.defvjp(_head_sum_fwd, _head_sum_bwd)


def cat_rows(*xs):
    return jnp.concatenate(xs, axis=0)


def _iota2(n, m):
    return lax.broadcasted_iota(jnp.int32, (n, m), 0), lax.broadcasted_iota(jnp.int32, (n, m), 1)


def tril_ones(c, strict=False):
    i, j = _iota2(c, c)
    return jnp.where((i > j) if strict else (i >= j), 1.0, 0.0).astype(F32)


def row_of(x, r):
    i = lax.broadcasted_iota(jnp.int32, x.shape, 0)
    return jnp.sum(jnp.where(i == r, x, 0.0), axis=0, keepdims=True)


@jax.custom_vjp
def shift_rows(x, prev):
    r = lax.broadcasted_iota(jnp.int32, x.shape, 0)
    return jnp.where(r == 0, prev, pltpu.roll(x, 1, 0))


def _shift_fwd(x, prev):
    return shift_rows(x, prev), None


def _shift_bwd(_, g):
    c = g.shape[0]
    r = lax.broadcasted_iota(jnp.int32, g.shape, 0)
    return jnp.where(r == c - 1, 0.0, pltpu.roll(g, c - 1, 0)), row_of(g, 0)


shift_rows.defvjp(_shift_fwd, _shift_bwd)


def log_sigmoid(x):
    return jnp.minimum(x, 0.0) - jnp.log(1.0 + jnp.exp(-jnp.abs(x)))


def softplus(x):
    return jnp.maximum(x, 0.0) + jnp.log(1.0 + jnp.exp(-jnp.abs(x)))


def sigmoid(x):
    return 1.0 / (1.0 + jnp.exp(-x))


def rms(x, w, eps=NORM_EPS):
    return x * lax.rsqrt(jnp.mean(x * x, axis=-1, keepdims=True) + eps) * w


def gla_chunk(state, toks, params):
    q, k, v, glow = toks
    gk_up, bias, norm_w = params
    c = GLA_CHUNK
    subs, heads = range(glow.shape[0] // c), range(GLA_HEADS)
    rows = lambda x, j: x[j * c:(j + 1) * c]
    hk = lambda x, h: x[:, h * GLA_DK:(h + 1) * GLA_DK]
    hv = lambda x, h: x[:, h * GLA_DV:(h + 1) * GLA_DV]
    ltri = tril_ones(c)
    g = log_sigmoid(mmx(glow, gk_up, "nn", P_LORA) + bias) / GLA_NORMALIZER
    b = [cumsum_rows(rows(g, j)) for j in subs]
    ref = [lax.stop_gradient(row_of(b[j], c // 2)) for j in subs]
    last = [row_of(b[j], c - 1) for j in subs]
    ql = [rows(q, j) * (GLA_DK ** -0.5) * jnp.exp(b[j] - ref[j]) for j in subs]
    kr = [rows(k, j) * jnp.exp(ref[j] - b[j]) for j in subs]
    kl = [rows(k, j) * jnp.exp(last[j] - b[j]) for j in subs]
    vj = [rows(v, j) for j in subs]
    e_ref, e_last = [jnp.exp(x) for x in ref], [jnp.exp(x) for x in last]
    att = [[mmx(hk(ql[j], h), hk(kr[j], h), "nt", P_GLA) * ltri for h in heads] for j in subs]
    o_in = [[mmx(att[j][h], hv(vj[j], h), "nn", P_GLA) for h in heads] for j in subs]
    kv = [[mmx(hv(vj[j], h), hk(kl[j], h), "tn", P_GLA) for h in heads] for j in subs]
    o = []
    for j in subs:
        o.append([o_in[j][h] + mmx(hk(ql[j], h), state[h] * hk(e_ref[j], h), "nt", P_GLA) for h in heads])
        state = [state[h] * hk(e_last[j], h) + kv[j][h] for h in heads]
    o = [[x * lax.rsqrt(jnp.mean(x * x, axis=-1, keepdims=True) + NORM_EPS) * norm_w for x in oj] for oj in o]
    return cat_rows(*[jnp.concatenate(oj, axis=1) for oj in o]), state


SOLVE_BLOCK = 128


def solve_unit_lower(ps, ws):
    n = ps[0].shape[0]
    heads = range(len(ps))
    if n > SOLVE_BLOCK:
        half = n // 2
        top = solve_unit_lower([p[:half, :half] for p in ps], [w[:half] for w in ws])
        rest = [ws[h][half:] + mmx(ps[h][half:, :half], top[h], "nn", P_RWKV) for h in heads]
        bottom = solve_unit_lower([p[half:, half:] for p in ps], rest)
        return [cat_rows(top[h], bottom[h]) for h in heads]
    u, p = ws, ps
    levels = max(1, (n - 1).bit_length())
    for it in range(levels):
        if it + 1 < levels:
            y = [mmx(p[h], jnp.concatenate([p[h], u[h]], axis=1), "nn", P_RWKV) for h in heads]
            u = [u[h] + y[h][:, n:] for h in heads]
            p = [y[h][:, :n] for h in heads]
        else:
            u = [u[h] + mmx(p[h], u[h], "nn", P_RWKV) for h in heads]
    return u


def rwkv_chunk(state, toks, params):
    S, pr, pk, pv, pxw, pxa = state
    r_, k_, v_, xw_, xa_ = toks
    mu_r, mu_k, mu_v, mu_xw, mu_xa, w0, w_up, a0, a_up, k_k, k_a, r_k, ln_w, ln_b = params
    c, n = xw_.shape[0], RWKV_N
    heads = range(RWKV_HEADS)
    hs = lambda x, h: x[:, h * n:(h + 1) * n]
    ltri = tril_ones(c)
    stri = tril_ones(c, strict=True)

    def lerp(x, prev, mu):
        return x + (shift_rows(x, prev) - x) * mu

    xw = jnp.tanh(lerp(xw_, pxw, mu_xw))
    xa = lerp(xa_, pxa, mu_xa)
    r = lerp(r_, pr, mu_r)
    k = lerp(k_, pk, mu_k)
    v = lerp(v_, pv, mu_v)
    w = -softplus(-(w0 + mmx(xw, w_up, "nn", P_LORA))) - 0.5
    lw = -jnp.exp(w)
    asig = sigmoid(a0 + mmx(xa, a_up, "nn", P_LORA))
    kk = k * k_k
    kk = kk * lax.rsqrt(jnp.maximum(head_sum(kk * kk), 1e-24))
    k2 = k * (1.0 + (asig - 1.0) * k_a)
    b = kk * asig
    cum = cumsum_rows(lw)
    ref = lax.stop_gradient(row_of(cum, c // 2))
    last = row_of(cum, c - 1)
    at = -kk * jnp.exp(cum - lw - ref)
    rt = r * jnp.exp(cum - ref)
    e_out = jnp.exp(ref - cum)
    bt, kt = b * e_out, k2 * e_out
    e_tail = jnp.exp(last - cum)
    bl, kl = b * e_tail, k2 * e_tail
    e_ref, e_last = jnp.exp(ref), jnp.exp(last)
    g = [mmx(cat_rows(hs(at, h), hs(rt, h)), cat_rows(hs(bt, h), hs(kt, h), S[h] * hs(e_ref, h)), "nt", P_RWKV_G) for h in heads]
    aab = [x[:c, :c] * stri for x in g]
    aak = [x[:c, c:2 * c] * stri for x in g]
    arb = [x[c:, :c] * ltri for x in g]
    ark = [x[c:, c:2 * c] * ltri for x in g]
    av = [mmx(cat_rows(aak[h], ark[h]), hs(v, h), "nn", P_RWKV) for h in heads]
    u = solve_unit_lower(aab, [g[h][:c, 2 * c:] + av[h][:c] for h in heads])
    o = [g[h][c:, 2 * c:] + av[h][c:] + mmx(arb[h], u[h], "nn", P_RWKV) for h in heads]
    s1 = [S[h] * hs(e_last, h) + mmx(cat_rows(u[h], hs(v, h)), cat_rows(hs(bl, h), hs(kl, h)), "tn", P_RWKV) for h in heads]
    o = jnp.concatenate(o, axis=1)
    d = o - head_sum(o) * (1.0 / n)
    var = head_sum(d * d) * (1.0 / n)
    o = d * lax.rsqrt(var + RWKV_LN_EPS) * ln_w + ln_b + head_sum(r * k2 * r_k) * v
    new_state = (s1, row_of(r_, c - 1), row_of(k_, c - 1), row_of(v_, c - 1), row_of(xw_, c - 1), row_of(xa_, c - 1))
    return o, new_state


RWKV_STEP = 256


def rwkv_chunks(state, toks, params):
    outs = []
    for j in range(toks[3].shape[0] // RWKV_CHUNK):
        rows = slice(j * RWKV_CHUNK, (j + 1) * RWKV_CHUNK)
        o, state = rwkv_chunk(state, tuple(t[rows] for t in toks), params)
        outs.append(o)
    return cat_rows(*outs), state


ROPE_HALF = 8


def _rot_half_raw(x):
    lane = lax.broadcasted_iota(jnp.int32, (x.shape[0], LANES), 1) & (SWA_HD - 1)
    out = []
    for i in range(x.shape[1] // LANES):
        g = x[:, i * LANES:(i + 1) * LANES]
        up, down = pltpu.roll(g, LANES - ROPE_HALF, 1), pltpu.roll(g, ROPE_HALF, 1)
        out.append(jnp.where(lane < ROPE_HALF, -up, jnp.where(lane < 2 * ROPE_HALF, down, 0.0)))
    return out[0] if len(out) == 1 else jnp.concatenate(out, axis=1)


@jax.custom_vjp
def rot_half(x):
    return _rot_half_raw(x)


rot_half.defvjp(lambda x: (_rot_half_raw(x), None), lambda _, g: (-_rot_half_raw(g),))


def rope(x, cos2, sin2):
    reps = x.shape[1] // LANES
    tile = lambda t: t if reps == 1 else jnp.concatenate([t] * reps, axis=1)
    return x * tile(cos2) + rot_half(x) * tile(sin2)


def swa_chunk(state, toks, params, first):
    kprev, vprev = state
    q_, k_, v_, cos, sin = toks
    bq, bk, bv, sinks = params
    c, ng = WINDOW, SWA_GROUP
    n_sub = cos.shape[0] // c
    units = [(j, g) for j in range(n_sub) for g in range(SWA_KV_HEADS)]
    rows = lambda x, j: x[j * c:(j + 1) * c]
    hs = lambda g: range(g * ng, (g + 1) * ng)
    head = lambda x, h: x[:, h * SWA_HD:(h + 1) * SWA_HD]
    qi, kj = _iota2(ng * c, 2 * c)
    qpos = qi & (c - 1)
    cur_ok = (kj >= c) & (qpos >= kj - c)
    prev_ok = (kj < c) & (kj > qpos)
    ok = [cur_ok | (prev_ok & jnp.logical_not(first))] + [cur_ok | prev_ok] * (n_sub - 1)
    q_all = rope(q_ + bq, cos, sin) * (SWA_HD ** -0.5)
    k_all = rope(k_ + bk, cos, sin)
    v_all = v_ + bv
    k = {(j, g): rows(head(k_all, g), j) for j, g in units}
    v = {(j, g): rows(head(v_all, g), j) for j, g in units}
    q = {(j, g): cat_rows(*[rows(head(q_all, h), j) for h in hs(g)]) for j, g in units}
    kp = lambda j, g: kprev[g] if j == 0 else k[(j - 1, g)]
    vp = lambda j, g: vprev[g] if j == 0 else v[(j - 1, g)]
    s = {(j, g): jnp.where(ok[j], mmx(q[(j, g)], cat_rows(kp(j, g), k[(j, g)]), "nt", P_SWA), NEG) for j, g in units}
    sink = [cat_rows(*[jnp.broadcast_to(sinks[h], (c, 1)) for h in hs(g)]) for g in range(SWA_KV_HEADS)]
    m = {(j, g): lax.stop_gradient(jnp.maximum(jnp.max(s[(j, g)], axis=-1, keepdims=True), sink[g])) for j, g in units}
    p = {u: jnp.exp(s[u] - m[u]) for u in units}
    ones = jnp.ones((2 * c, SWA_HD), F32)
    pv = {(j, g): mmx(p[(j, g)], cat_rows(vp(j, g), v[(j, g)]), "nn", P_SWA) for j, g in units}
    den = {u: mmx(p[u], ones, "nn", P_SWA) for u in units}
    o = {(j, g): pv[(j, g)] / (den[(j, g)] + jnp.exp(sink[g] - m[(j, g)])) for j, g in units}
    outs = [cat_rows(*[o[(j, g)][i * c:(i + 1) * c] for j in range(n_sub)]) for g in range(SWA_KV_HEADS) for i in range(ng)]
    last = n_sub - 1
    return outs, ([k[(last, g)] for g in range(SWA_KV_HEADS)], [v[(last, g)] for g in range(SWA_KV_HEADS)])


def _heads(ref, n, w, rows=slice(None)):
    return [ref[rows, h * w:(h + 1) * w] for h in range(n)]


def _put_heads(ref, vals, w, rows=slice(None), add=False):
    for h, val in enumerate(vals):
        if add:
            ref[rows, h * w:(h + 1) * w] += val
        else:
            ref[rows, h * w:(h + 1) * w] = val


def _col(block_w, name, table):
    off, w = table[name]
    assert off % block_w == 0 and w % block_w == 0
    return off // block_w


def _tok_spec(c, w, colblock, n=None):
    if n is None:
        return pl.BlockSpec((c, w), lambda i: (i, colblock))
    return pl.BlockSpec((c, w), lambda i: (n - 1 - i, colblock))


def _full_spec(shape):
    return pl.BlockSpec(shape, lambda i: (0,) * len(shape))


def _matmul(name, a, b, mode, tm, tn, out_dtype=F32):
    (m, kd) = (a.shape[1], a.shape[0]) if mode == "tn" else a.shape
    n = b.shape[0] if mode == "nt" else b.shape[1]
    assert m % tm == 0 and n % tn == 0
    a_spec = pl.BlockSpec((kd, tm), lambda j, i: (0, i)) if mode == "tn" else pl.BlockSpec((tm, kd), lambda j, i: (i, 0))
    b_spec = pl.BlockSpec((tn, kd), lambda j, i: (j, 0)) if mode == "nt" else pl.BlockSpec((kd, tn), lambda j, i: (0, j))

    def body(a_ref, b_ref, o_ref):
        o_ref[...] = lax.dot_general(a_ref[...].astype(BF16), b_ref[...].astype(BF16), DIMS[mode],
                                     preferred_element_type=F32).astype(out_dtype)

    return pl.pallas_call(
        body, name=name, grid=(n // tn, m // tm), in_specs=[a_spec, b_spec],
        out_specs=pl.BlockSpec((tm, tn), lambda j, i: (i, j)), out_shape=jax.ShapeDtypeStruct((m, n), out_dtype),
        compiler_params=_cparams(("arbitrary", "arbitrary")))(a, b)


TOK_TILE = 512
PROJ_ROWS = 1024
DW_COLS = 512
ADAM_COLS = 512


def _matmul_fused(name, a, b, mode, tiles, rows, outs, sums, epilogue, comm=(), kinds=()):
    made = callable(a)
    m = tiles[0][0].shape[0] if made else a.shape[0]
    kd = b.shape[0] if mode == "nn" else b.shape[1]
    n = b.shape[1] if mode == "nn" else b.shape[0]
    tm = TOK_TILE
    steps = m // tm
    if made:
        outs = [(kd, BF16)] + list(outs)
    nt_, nr, no, ns, ncomm = len(tiles), len(rows), len(outs), len(sums), len(comm)

    def body(*refs):
        at = 1 if made else 2
        b_ref = refs[at - 1]
        tile_refs, row_refs, comm_in = refs[at:at + nt_], refs[at + nt_:at + nt_ + nr], refs[at + nt_ + nr:at + nt_ + nr + ncomm]
        at += nt_ + nr + ncomm
        out_refs, sum_refs, comm_out = refs[at:at + no], refs[at + no:at + no + ns], refs[at + no + ns:at + no + ns + ncomm]
        sems = refs[at + no + ns + ncomm:]
        i = pl.program_id(0)

        @pl.when(i == 0)
        def _():
            if ncomm:
                _comm_start(*_comm_copies(comm_in, comm_out, kinds, *sems))
            for ref in sum_refs:
                ref[...] = jnp.zeros_like(ref)

        extras = [r[...] for r in tile_refs] + [r[...] for r in row_refs]
        a_blk = (a(*extras) if made else refs[0][...]).astype(BF16)
        acc = lax.dot_general(a_blk, b_ref[...].astype(BF16), DIMS[mode], preferred_element_type=F32)
        res = epilogue(acc, *extras)
        if made:
            res = (a_blk,) + tuple(res)
        for ref, val in zip(out_refs, res[:no]):
            ref[...] = val.astype(ref.dtype)
        for ref, val in zip(sum_refs, res[no:]):
            ref[...] += val

        if ncomm:
            @pl.when(i == steps - 1)
            def _():
                _comm_wait(*_comm_copies(comm_in, comm_out, kinds, *sems))

    in_specs = ([] if made else [pl.BlockSpec((tm, kd), lambda i: (i, 0))]) + [_full_spec(b.shape)]
    in_specs += [pl.BlockSpec((tm, w), functools.partial(lambda i, cb: (i, cb), cb=cb)) for _, w, cb in tiles]
    in_specs += [_full_spec(r.shape) for r in rows] + [ANY] * ncomm
    out_specs = [pl.BlockSpec((tm, w), lambda i: (i, 0)) for w, _ in outs] + [_full_spec((1, w)) for w in sums] + [ANY] * ncomm
    out_shape = ([jax.ShapeDtypeStruct((m, w), dt) for w, dt in outs] + [jax.ShapeDtypeStruct((1, w), F32) for w in sums]
                 + _comm_out_shapes(comm, kinds))
    return pl.pallas_call(body, name=name, grid=(steps,), in_specs=in_specs, out_specs=out_specs, out_shape=out_shape,
                          scratch_shapes=_comm_scratch(ncomm) if ncomm else [],
                          compiler_params=_cparams(("arbitrary",)))(*([] if made else [a]), b, *[t[0] for t in tiles], *rows, *comm)


def _resid_norm(y, x, w):
    h = x + y
    return h, rms(h, w)


def _norm_back(dhn, h, dres, w):
    _, vjp = jax.vjp(rms, h, w)
    dh, dw = vjp(dhn)
    return dh + dres, dw


def _gate_back(dog, *o_and_gate):
    outs, g = o_and_gate[:-1], o_and_gate[-1]
    s = sigmoid(g)
    silu, dsilu = g * s, s * (1.0 + g * (1.0 - s))
    d_outs, c = [], 0
    for o in outs:
        w = o.shape[1]
        d_outs.append(dog[:, c:c + w] * silu[:, c:c + w])
        c += w
    o_all = outs[0] if len(outs) == 1 else jnp.concatenate(outs, axis=1)
    return (*d_outs, dog * o_all * dsilu)


def _loss_head(y1, h1, target, b_out, fw):
    def f(h2, w):
        err = rms(h2, w) - target
        return 0.5 * jnp.sum(jnp.mean(err * err, axis=-1, keepdims=True), axis=0, keepdims=True)

    loss, vjp = jax.vjp(f, h1 + y1 + b_out, fw)
    dh2, dfw = vjp(jnp.ones((1, 1), F32))
    return dh2, jnp.broadcast_to(loss, (1, LANES)), jnp.sum(dh2, axis=0, keepdims=True), dfw


def _gla_load(q_ref, k_ref, v_ref, gl_ref, up_ref, bias_ref, nw_ref):
    toks = (q_ref[...], k_ref[...], v_ref[...], gl_ref[...])
    params = (up_ref[...], bias_ref[...], nw_ref[...])
    return toks, params


def _gla_specs(c, n=None):
    toks = [_tok_spec(c, GLA_KEY, _col(GLA_KEY, "gq", C0), n), _tok_spec(c, GLA_KEY, _col(GLA_KEY, "gk", C0), n),
            _tok_spec(c, GLA_VAL, _col(GLA_VAL, "gv", C0), n), _tok_spec(c, LOW, _col(LOW, "glow", C0), n)]
    return toks, [_full_spec(s) for s in GLA_PARAM_SHAPES]


GLA_PARAM_SHAPES = [(LOW, GLA_KEY), (1, GLA_KEY), (1, GLA_DV)]
GLA_STATE = (GLA_HEADS * GLA_DV, GLA_DK)


def _gla_fwd(proj0, gk_up, gk_bias, norm_w):
    t = proj0.shape[0]
    c = GLA_STEP
    nc = t // c
    toks_s, params_s = _gla_specs(c)

    def body(q_ref, k_ref, v_ref, gl_ref, up_ref, bias_ref, nw_ref, o_ref, st_ref, s_scr):
        @pl.when(pl.program_id(0) == 0)
        def _():
            s_scr[...] = jnp.zeros_like(s_scr)

        st_ref[...] = s_scr[...]
        toks, params = _gla_load(q_ref, k_ref, v_ref, gl_ref, up_ref, bias_ref, nw_ref)
        state = [s_scr[h * GLA_DV:(h + 1) * GLA_DV, :] for h in range(GLA_HEADS)]
        o_ref[...], new = gla_chunk(state, toks, params)
        for h in range(GLA_HEADS):
            s_scr[h * GLA_DV:(h + 1) * GLA_DV, :] = new[h]

    return pl.pallas_call(
        body, name="gla_fwd", grid=(nc,), in_specs=toks_s + params_s,
        out_specs=(_tok_spec(c, GLA_VAL, 0), pl.BlockSpec(GLA_STATE, lambda i: (i, 0))),
        out_shape=(jax.ShapeDtypeStruct((t, GLA_VAL), F32), jax.ShapeDtypeStruct((nc * GLA_STATE[0], GLA_DK), F32)),
        scratch_shapes=[pltpu.VMEM(GLA_STATE, F32)], compiler_params=_cparams(("arbitrary",)))(
            proj0, proj0, proj0, proj0, gk_up, gk_bias, norm_w)


def _gla_bwd(proj0, gk_up, gk_bias, norm_w, states, do):
    t = proj0.shape[0]
    c = GLA_STEP
    nc = t // c
    toks_s, params_s = _gla_specs(c, nc)

    def body(q_ref, k_ref, v_ref, gl_ref, up_ref, bias_ref, nw_ref, st_ref, do_ref,
             dq_ref, dk_ref, dv_ref, dgl_ref, dup_ref, dbias_ref, dnw_ref, ds_scr):
        @pl.when(pl.program_id(0) == 0)
        def _():
            ds_scr[...] = jnp.zeros_like(ds_scr)
            dup_ref[...] = jnp.zeros_like(dup_ref)
            dbias_ref[...] = jnp.zeros_like(dbias_ref)
            dnw_ref[...] = jnp.zeros_like(dnw_ref)

        toks, params = _gla_load(q_ref, k_ref, v_ref, gl_ref, up_ref, bias_ref, nw_ref)
        rows = lambda h: slice(h * GLA_DV, (h + 1) * GLA_DV)
        state = [st_ref[rows(h), :] for h in range(GLA_HEADS)]
        _, vjp = jax.vjp(gla_chunk, state, toks, params)
        dstate_in = [ds_scr[rows(h), :] for h in range(GLA_HEADS)]
        dstate, dtoks, (dup, dbias, dnw) = vjp((do_ref[...], dstate_in))
        for ref, val in zip((dq_ref, dk_ref, dv_ref, dgl_ref), dtoks):
            ref[...] = val.astype(ref.dtype)
        dup_ref[...] += dup
        dbias_ref[...] += dbias
        dnw_ref[...] += dnw
        for h in range(GLA_HEADS):
            ds_scr[rows(h), :] = dstate[h]

    rev = lambda w: pl.BlockSpec((c, w), lambda i: (nc - 1 - i, 0))
    tok_widths = (GLA_KEY, GLA_KEY, GLA_VAL, LOW)
    return pl.pallas_call(
        body, name="gla_bwd", grid=(nc,),
        in_specs=toks_s + params_s + [pl.BlockSpec(GLA_STATE, lambda i: (nc - 1 - i, 0)), rev(GLA_VAL)],
        out_specs=[rev(w) for w in tok_widths] + params_s,
        out_shape=[jax.ShapeDtypeStruct((t, w), BF16) for w in tok_widths] + [jax.ShapeDtypeStruct(s, F32) for s in GLA_PARAM_SHAPES],
        scratch_shapes=[pltpu.VMEM(GLA_STATE, F32)], compiler_params=_cparams(("arbitrary",)))(
            proj0, proj0, proj0, proj0, gk_up, gk_bias, norm_w, states, do)


RWKV_PARAM_SHAPES = [(1, RWKV_W), (1, RWKV_W), (1, RWKV_W), (1, LOW), (1, LOW), (1, RWKV_W), (LOW, RWKV_W), (1, RWKV_W),
                     (LOW, RWKV_W), (1, RWKV_W), (1, RWKV_W), (1, RWKV_W), (1, RWKV_W), (1, RWKV_W)]
RWKV_STATE = (RWKV_HEADS * RWKV_N, RWKV_N)
RWKV_TOK_WIDTHS = (RWKV_W, RWKV_W, RWKV_W, LOW, LOW)
PREV_W = sum(RWKV_TOK_WIDTHS)
PREV_COLS = [slice(sum(RWKV_TOK_WIDTHS[:i]), sum(RWKV_TOK_WIDTHS[:i + 1])) for i in range(len(RWKV_TOK_WIDTHS))]


def _rwkv_load(r_ref, k_ref, v_ref, xw_ref, xa_ref, p_refs):
    toks = (r_ref[...], k_ref[...], v_ref[...], xw_ref[...], xa_ref[...])
    return toks, tuple(p[...] for p in p_refs)


def _rwkv_state(s_ref, prev_ref):
    n = RWKV_N
    S = [s_ref[h * n:(h + 1) * n, :] for h in range(RWKV_HEADS)]
    return (S,) + tuple(prev_ref[0:1, cols] for cols in PREV_COLS)


def _rwkv_put_state(s_ref, prev_ref, state):
    n = RWKV_N
    for h in range(RWKV_HEADS):
        s_ref[h * n:(h + 1) * n, :] = state[0][h]
    for cols, val in zip(PREV_COLS, state[1:]):
        prev_ref[0:1, cols] = val


def _rwkv_specs(c, n=None):
    toks = [_tok_spec(c, w, _col(w, name, C0), n) for name, w in zip(("r", "k", "v", "xw", "xa"), RWKV_TOK_WIDTHS)]
    return toks, [_full_spec(s) for s in RWKV_PARAM_SHAPES]


def _rwkv_fwd(proj0, params, comm, kinds):
    t = proj0.shape[0]
    c = RWKV_STEP
    nc = t // c
    toks_s, params_s = _rwkv_specs(c)
    npar, ncomm = len(params), len(comm)

    def body(*refs):
        tok_refs, p_refs = refs[:5], refs[5:5 + npar]
        comm_in = refs[5 + npar:5 + npar + ncomm]
        o_ref, st_ref, pst_ref = refs[5 + npar + ncomm:8 + npar + ncomm]
        comm_out = refs[8 + npar + ncomm:8 + npar + 2 * ncomm]
        s_scr, prev_scr = refs[8 + npar + 2 * ncomm:10 + npar + 2 * ncomm]
        sems = refs[10 + npar + 2 * ncomm:]
        i = pl.program_id(0)

        @pl.when(i == 0)
        def _():
            _comm_start(*_comm_copies(comm_in, comm_out, kinds, *sems))
            s_scr[...] = jnp.zeros_like(s_scr)
            prev_scr[...] = jnp.zeros_like(prev_scr)

        st_ref[...] = s_scr[...]
        pst_ref[...] = prev_scr[...]
        toks, prm = _rwkv_load(*tok_refs, p_refs)
        o_ref[...], new = rwkv_chunks(_rwkv_state(s_scr, prev_scr), toks, prm)
        _rwkv_put_state(s_scr, prev_scr, new)

        @pl.when(i == nc - 1)
        def _():
            _comm_wait(*_comm_copies(comm_in, comm_out, kinds, *sems))

    outs = pl.pallas_call(
        body, name="rwkv_fwd", grid=(nc,), in_specs=toks_s + params_s + [ANY] * ncomm,
        out_specs=[_tok_spec(c, RWKV_W, 0), pl.BlockSpec(RWKV_STATE, lambda i: (i, 0)), pl.BlockSpec((8, PREV_W), lambda i: (i, 0))]
        + [ANY] * ncomm,
        out_shape=[jax.ShapeDtypeStruct((t, RWKV_W), F32), jax.ShapeDtypeStruct((nc * RWKV_STATE[0], RWKV_N), F32),
                   jax.ShapeDtypeStruct((nc * 8, PREV_W), F32)] + _comm_out_shapes(comm, kinds),
        scratch_shapes=[pltpu.VMEM(RWKV_STATE, F32), pltpu.VMEM((8, PREV_W), F32)] + _comm_scratch(ncomm),
        compiler_params=_cparams(("arbitrary",)))(proj0, proj0, proj0, proj0, proj0, *params, *comm)
    return outs[0], outs[1], outs[2], outs[3:]


def _rwkv_bwd(proj0, params, states, prevs, do, comm, kinds):
    t = proj0.shape[0]
    c = RWKV_STEP
    nc = t // c
    toks_s, params_s = _rwkv_specs(c, nc)
    npar, ncomm = len(params), len(comm)

    def body(*refs):
        tok_refs, p_refs = refs[:5], refs[5:5 + npar]
        st_ref, pst_ref, do_ref = refs[5 + npar:8 + npar]
        comm_in = refs[8 + npar:8 + npar + ncomm]
        outs = refs[8 + npar + ncomm:]
        dtok_refs, dp_refs, comm_out = outs[:5], outs[5:5 + npar], outs[5 + npar:5 + npar + ncomm]
        ds_scr, dprev_scr = outs[5 + npar + ncomm:7 + npar + ncomm]
        sems = outs[7 + npar + ncomm:]
        i = pl.program_id(0)

        @pl.when(i == 0)
        def _():
            _comm_start(*_comm_copies(comm_in, comm_out, kinds, *sems))
            ds_scr[...] = jnp.zeros_like(ds_scr)
            dprev_scr[...] = jnp.zeros_like(dprev_scr)
            for dp in dp_refs:
                dp[...] = jnp.zeros_like(dp)

        toks, prm = _rwkv_load(*tok_refs, p_refs)
        _, vjp = jax.vjp(rwkv_chunks, _rwkv_state(st_ref, pst_ref), toks, prm)
        dstate, dtoks, dprm = vjp((do_ref[...], _rwkv_state(ds_scr, dprev_scr)))
        for ref, val in zip(dtok_refs, dtoks):
            ref[...] = val.astype(ref.dtype)
        for ref, val in zip(dp_refs, dprm):
            ref[...] += val
        _rwkv_put_state(ds_scr, dprev_scr, dstate)

        @pl.when(i == nc - 1)
        def _():
            _comm_wait(*_comm_copies(comm_in, comm_out, kinds, *sems))

    rev = lambda w: pl.BlockSpec((c, w), lambda i: (nc - 1 - i, 0))
    outs = pl.pallas_call(
        body, name="rwkv_bwd", grid=(nc,),
        in_specs=toks_s + params_s + [pl.BlockSpec(RWKV_STATE, lambda i: (nc - 1 - i, 0)),
                                      pl.BlockSpec((8, PREV_W), lambda i: (nc - 1 - i, 0)), rev(RWKV_W)] + [ANY] * ncomm,
        out_specs=[rev(w) for w in RWKV_TOK_WIDTHS] + params_s + [ANY] * ncomm,
        out_shape=[jax.ShapeDtypeStruct((t, w), BF16) for w in RWKV_TOK_WIDTHS]
        + [jax.ShapeDtypeStruct(s, F32) for s in RWKV_PARAM_SHAPES] + _comm_out_shapes(comm, kinds),
        scratch_shapes=[pltpu.VMEM(RWKV_STATE, F32), pltpu.VMEM((8, PREV_W), F32)] + _comm_scratch(ncomm),
        compiler_params=_cparams(("arbitrary",)))(proj0, proj0, proj0, proj0, proj0, *params, states, prevs, do, *comm)
    return outs[:5], outs[5:5 + npar], outs[5 + npar:]


def _swa_load(q_ref, k_ref, v_ref, cos_ref, sin_ref, bq_ref, bk_ref, bv_ref, sk_ref):
    toks = (q_ref[...], k_ref[...], v_ref[...], cos_ref[...], sin_ref[...])
    params = (bq_ref[...], bk_ref[...], bv_ref[...], _heads(sk_ref, SWA_Q_HEADS, 1))
    return toks, params


SWA_TOK_WIDTHS = (MIX, SWA_KV, SWA_KV)
SWA_PARAM_SHAPES = [(1, MIX), (1, SWA_KV), (1, SWA_KV), (1, SWA_Q_HEADS)]
SWA_STATE = (WINDOW, SWA_KV)


def _swa_specs(c, n=None):
    toks = [_tok_spec(c, w, _col(w, name, C1), n) for name, w in zip(("q", "k", "v"), SWA_TOK_WIDTHS)]
    toks += [_tok_spec(c, LANES, 0, n), _tok_spec(c, LANES, 0, n)]
    return toks, [_full_spec(s) for s in SWA_PARAM_SHAPES]


def _swa_fwd(proj1, cos, sin, bq, bk, bv, sinks):
    t = proj1.shape[0]
    c = SWA_STEP
    nb = t // c
    toks_s, params_s = _swa_specs(c)
    state_spec = pl.BlockSpec(SWA_STATE, lambda i: (i, 0))
    kv = SWA_KV_HEADS

    def body(q_ref, k_ref, v_ref, cos_ref, sin_ref, bq_ref, bk_ref, bv_ref, sk_ref, o_ref, kst_ref, vst_ref, k_scr, v_scr):
        first = pl.program_id(0) == 0

        @pl.when(first)
        def _():
            k_scr[...] = jnp.zeros_like(k_scr)
            v_scr[...] = jnp.zeros_like(v_scr)

        kst_ref[...] = k_scr[...]
        vst_ref[...] = v_scr[...]
        toks, params = _swa_load(q_ref, k_ref, v_ref, cos_ref, sin_ref, bq_ref, bk_ref, bv_ref, sk_ref)
        outs, (kn, vn) = swa_chunk((_heads(k_scr, kv, SWA_HD), _heads(v_scr, kv, SWA_HD)), toks, params, first)
        _put_heads(o_ref, outs, SWA_HD)
        _put_heads(k_scr, kn, SWA_HD)
        _put_heads(v_scr, vn, SWA_HD)

    saved = jax.ShapeDtypeStruct((nb * WINDOW, SWA_KV), F32)
    return pl.pallas_call(
        body, name="swa_fwd", grid=(nb,), in_specs=toks_s + params_s,
        out_specs=(_tok_spec(c, MIX, 0), state_spec, state_spec),
        out_shape=(jax.ShapeDtypeStruct((t, MIX), F32), saved, saved),
        scratch_shapes=[pltpu.VMEM(SWA_STATE, F32), pltpu.VMEM(SWA_STATE, F32)],
        compiler_params=_cparams(("arbitrary",)))(proj1, proj1, proj1, cos, sin, bq, bk, bv, sinks)


def _swa_bwd(proj1, cos, sin, bq, bk, bv, sinks, kst, vst, do):
    t = proj1.shape[0]
    c = SWA_STEP
    nb = t // c
    toks_s, params_s = _swa_specs(c, nb)
    state_spec = pl.BlockSpec(SWA_STATE, lambda i: (nb - 1 - i, 0))
    kv = SWA_KV_HEADS

    def body(q_ref, k_ref, v_ref, cos_ref, sin_ref, bq_ref, bk_ref, bv_ref, sk_ref, kst_ref, vst_ref, do_ref,
             dq_ref, dk_ref, dv_ref, dbq_ref, dbk_ref, dbv_ref, dsk_ref, dk_scr, dv_scr):
        i = pl.program_id(0)

        @pl.when(i == 0)
        def _():
            dk_scr[...] = jnp.zeros_like(dk_scr)
            dv_scr[...] = jnp.zeros_like(dv_scr)
            for ref in (dbq_ref, dbk_ref, dbv_ref, dsk_ref):
                ref[...] = jnp.zeros_like(ref)

        first = i == nb - 1
        toks, params = _swa_load(q_ref, k_ref, v_ref, cos_ref, sin_ref, bq_ref, bk_ref, bv_ref, sk_ref)
        f = functools.partial(swa_chunk, first=first)
        _, vjp = jax.vjp(f, (_heads(kst_ref, kv, SWA_HD), _heads(vst_ref, kv, SWA_HD)), toks, params)
        dstate_in = (_heads(dk_scr, kv, SWA_HD), _heads(dv_scr, kv, SWA_HD))
        (dkp, dvp), (dq, dk, dv, _, _), (dbq, dbk, dbv, dsk) = vjp((_heads(do_ref, SWA_Q_HEADS, SWA_HD), dstate_in))
        dq_ref[...], dk_ref[...], dv_ref[...] = dq.astype(BF16), dk.astype(BF16), dv.astype(BF16)
        dbq_ref[...] += dbq
        dbk_ref[...] += dbk
        dbv_ref[...] += dbv
        _put_heads(dsk_ref, dsk, 1, add=True)
        _put_heads(dk_scr, dkp, SWA_HD)
        _put_heads(dv_scr, dvp, SWA_HD)

    rev = lambda w: pl.BlockSpec((c, w), lambda i: (nb - 1 - i, 0))
    return pl.pallas_call(
        body, name="swa_bwd", grid=(nb,), in_specs=toks_s + params_s + [state_spec, state_spec, rev(MIX)],
        out_specs=[rev(w) for w in SWA_TOK_WIDTHS] + params_s,
        out_shape=[jax.ShapeDtypeStruct((t, w), BF16) for w in SWA_TOK_WIDTHS] + [jax.ShapeDtypeStruct(s, F32) for s in SWA_PARAM_SHAPES],
        scratch_shapes=[pltpu.VMEM(SWA_STATE, F32), pltpu.VMEM(SWA_STATE, F32)],
        compiler_params=_cparams(("arbitrary",)))(proj1, proj1, proj1, cos, sin, bq, bk, bv, sinks, kst, vst, do)


MESH = pl.DeviceIdType.MESH
ANY = pl.BlockSpec(memory_space=pl.ANY)


def _my_place():
    return lax.axis_index("x"), lax.axis_index("y"), lax.axis_index("c")


def _all_gather(shards):
    n = len(shards)

    def body(*refs):
        in_refs, out_refs = refs[:n], refs[n:2 * n]
        send_sems, recv_sems, local_sems = refs[2 * n:]
        x, y, c = _my_place()
        me, sibling = (x, y, c), (x, y, 1 - c)
        chips = [(1 - x, y), (x, 1 - y), (1 - x, 1 - y)]

        def slot(out_ref, place):
            px, py, pc = place
            return out_ref.at[4 * px + 2 * py + pc]

        def copy(a, k, block, to, src=None):
            return pltpu.make_async_remote_copy(
                src_ref=slot(out_refs[a], block) if src is None else src, dst_ref=slot(out_refs[a], block),
                send_sem=send_sems.at[a, k], recv_sem=recv_sems.at[a, k], device_id=to, device_id_type=MESH)

        mine = [pltpu.make_async_copy(in_refs[a], slot(out_refs[a], me), local_sems.at[a]) for a in range(n)]
        for cp in mine:
            cp.start()
        first = []
        for a in range(n):
            first.append(copy(a, 0, me, sibling, src=in_refs[a]))
            first += [copy(a, 1 + j, me, (*chip, c), src=in_refs[a]) for j, chip in enumerate(chips)]
        for cp in first:
            cp.start()
        passed = []
        for j, chip in enumerate(chips):
            for a in range(n):
                copy(a, 1 + j, (*chip, c), me).wait_recv()
                fwd = copy(a, 4 + j, (*chip, c), sibling)
                fwd.start()
                passed.append(fwd)
        for a in range(n):
            copy(a, 0, sibling, me).wait_recv()
            for j, chip in enumerate(chips):
                copy(a, 4 + j, (*chip, 1 - c), me).wait_recv()
        for cp in first + passed:
            cp.wait_send()
        for cp in mine:
            cp.wait()

    return pl.pallas_call(
        body, name="all_gather_weights", in_specs=[ANY] * n, out_specs=[ANY] * n,
        out_shape=[jax.ShapeDtypeStruct((N_DEV,) + s.shape, s.dtype) for s in shards],
        scratch_shapes=_comm_scratch(n))(*shards)


def _comm_copies(in_refs, out_refs, kinds, send_sems, recv_sems, local_sems):
    x, y, c = _my_place()
    my_idx = 4 * x + 2 * y + c
    src = lambda a, idx: in_refs[a] if kinds[a] == "gather" else in_refs[a].at[idx]
    local = [pltpu.make_async_copy(src(a, my_idx), out_refs[a].at[my_idx], local_sems.at[a]) for a in range(len(kinds))]
    remote = []
    for rel in range(1, N_DEV):
        px, py, pc = x ^ ((rel >> 2) & 1), y ^ ((rel >> 1) & 1), c ^ (rel & 1)
        for a in range(len(kinds)):
            remote.append(pltpu.make_async_remote_copy(
                src_ref=src(a, 4 * px + 2 * py + pc), dst_ref=out_refs[a].at[my_idx], send_sem=send_sems.at[a, rel - 1],
                recv_sem=recv_sems.at[a, rel - 1], device_id=(px, py, pc), device_id_type=MESH))
    return local, remote


def _comm_start(local, remote):
    for cp in local + remote:
        cp.start()


def _comm_wait(local, remote):
    for cp in remote:
        cp.wait_recv()
    for cp in remote:
        cp.wait_send()
    for cp in local:
        cp.wait()


def _comm_out_shapes(arrays, kinds):
    return [jax.ShapeDtypeStruct(((N_DEV,) + a.shape) if k == "gather" else a.shape, a.dtype) for a, k in zip(arrays, kinds)]


def _comm_scratch(n):
    return [pltpu.SemaphoreType.DMA((n, N_DEV - 1)), pltpu.SemaphoreType.DMA((n, N_DEV - 1)), pltpu.SemaphoreType.DMA((n,))]


def _sequencer_scatter(name, parts, collective_id):
    src = jax.new_ref(parts, memory_space=pltpu.MemorySpace.HBM)
    dst = jax.empty_ref(jax.ShapeDtypeStruct(parts.shape, parts.dtype), memory_space=pltpu.MemorySpace.HBM)

    @pl.kernel(mesh=plsc.ScalarSubcoreMesh(axis_name="sequencer", num_cores=1), name=name,
               scratch_types=(pltpu.SemaphoreType.DMA((N_DEV - 1,)), pltpu.SemaphoreType.DMA((N_DEV - 1,))),
               compiler_params=pltpu.CompilerParams(collective_id=collective_id))
    def launch(send_sems, recv_sems):
        x, y, c = _my_place()
        my_idx = 4 * x + 2 * y + c
        peers = [(x ^ ((rel >> 2) & 1), y ^ ((rel >> 1) & 1), c ^ (rel & 1)) for rel in range(1, N_DEV)]
        barrier = pltpu.get_barrier_semaphore()
        for peer in peers:
            pl.semaphore_signal(barrier, inc=1, device_id=peer, device_id_type=MESH)
        pl.semaphore_wait(barrier, N_DEV - 1)
        copies = [pltpu.make_async_remote_copy(
            src_ref=src.at[4 * px + 2 * py + pc], dst_ref=dst.at[my_idx], send_sem=send_sems.at[k], recv_sem=recv_sems.at[k],
            device_id=(px, py, pc), device_id_type=MESH) for k, (px, py, pc) in enumerate(peers)]
        for cp in copies:
            cp.start()
        for cp in copies:
            cp.wait_recv()
        for cp in copies:
            cp.wait_send()

    launch()
    return dst[...]


def _exchange(arrays, kinds):
    n = len(arrays)

    def body(*refs):
        copies = _comm_copies(refs[:n], refs[n:2 * n], kinds, *refs[2 * n:])
        _comm_start(*copies)
        _comm_wait(*copies)

    return pl.pallas_call(body, name="exchange_grads", in_specs=[ANY] * n, out_specs=[ANY] * n,
                          out_shape=_comm_out_shapes(arrays, kinds), scratch_shapes=_comm_scratch(n))(*arrays)


def _adam_math(w, g, m, v):
    m = ADAM_B1 * m + (1.0 - ADAM_B1) * g
    v = ADAM_B2 * v + (1.0 - ADAM_B2) * (g * g)
    m_hat = m / (1.0 - ADAM_B1 ** ADAM_STEP)
    v_hat = v / (1.0 - ADAM_B2 ** ADAM_STEP)
    delta = -ADAM_LR * (m_hat / (jnp.sqrt(v_hat) + ADAM_EPS) + ADAM_WD * w)
    return delta, m, v


def _adamw(name, w, gslots, m, v, tc):
    r, cc = w.shape
    assert cc % tc == 0
    tile = pl.BlockSpec((r, tc), lambda i: (0, i))

    def body(w_ref, g_ref, m_ref, v_ref, go_ref, d_ref, mo_ref, vo_ref):
        g = g_ref[0].astype(F32)
        for s in range(1, N_DEV):
            g = g + g_ref[s].astype(F32)
        d, mn, vn = _adam_math(w_ref[...], g, m_ref[...], v_ref[...])
        go_ref[...] = g
        d_ref[...] = d
        mo_ref[...] = mn
        vo_ref[...] = vn

    shp = jax.ShapeDtypeStruct((r, cc), F32)
    return pl.pallas_call(body, name=name, grid=(cc // tc,),
                          in_specs=[tile, pl.BlockSpec((N_DEV, r, tc), lambda i: (0, 0, i)), tile, tile],
                          out_specs=(tile,) * 4, out_shape=(shp,) * 4, compiler_params=_cparams(("arbitrary",)))(w, gslots, m, v)


PACK_TILE = 8 * LANES


def _packed_rows(shape, mode):
    r, w = shape
    return -(-r // 8) * 8 if mode == "rows" else -(-(r * w) // PACK_TILE) * 8


def _pack_small(arrays, modes, lead=False):
    out = []
    for a, mode in zip(arrays, modes):
        a = a.astype(F32) if lead else a.astype(F32)[None]
        if mode == "rows":
            out.append(jnp.pad(a, ((0, 0), (0, (-a.shape[1]) % 8), (0, LANES - a.shape[2]))))
        else:
            flat = a.reshape(a.shape[0], -1)
            out.append(jnp.pad(flat, ((0, 0), (0, (-flat.shape[1]) % PACK_TILE))).reshape(a.shape[0], -1, LANES))
    out = jnp.concatenate(out, axis=1)
    return out if lead else out[0]


def _take_small(packed, row0, shape, mode):
    r, w = shape
    lead = packed.ndim == 3
    if mode == "rows":
        return packed[:, row0:row0 + r, :w] if lead else packed[row0:row0 + r, :w]
    per_row = -(-w // LANES)
    if lead:
        return packed[:, row0:row0 + r * per_row].reshape(packed.shape[0], r, per_row * LANES)[:, :, :w]
    rows = []
    for i in range(r):
        pieces = [packed[row0 + i * per_row + j:row0 + i * per_row + j + 1, :] for j in range(per_row)]
        rows.append((pieces[0] if per_row == 1 else jnp.concatenate(pieces, axis=1))[:, :w])
    return rows[0] if r == 1 else jnp.concatenate(rows, axis=0)


def _adamw_small(slots, specs, ws, ms, vs, loss_row):
    n = len(specs)

    def body(*refs):
        slots_ref, w_refs, m_refs, v_refs = refs[0], refs[1:1 + n], refs[1 + n:1 + 2 * n], refs[1 + 2 * n:1 + 3 * n]
        out_refs, loss_ref = refs[1 + 3 * n:1 + 7 * n], refs[1 + 7 * n]
        gp = slots_ref[0]
        for s in range(1, N_DEV):
            gp = gp + slots_ref[s]
        read = lambda ref: ref[0] if len(ref.shape) == 3 else ref[...]
        for k, (shape, mode, row0) in enumerate(specs):
            g = _take_small(gp, row0, shape, mode)
            d, mn, vn = _adam_math(read(w_refs[k]), g, read(m_refs[k]), read(v_refs[k]))
            for ref, val in zip(out_refs[4 * k:4 * k + 4], (g, d, mn, vn)):
                if len(ref.shape) == 3:
                    ref[0] = val
                else:
                    ref[...] = val
        loss_ref[...] = gp[loss_row:loss_row + 1, :]

    vmem = pl.BlockSpec(memory_space=pltpu.VMEM)
    out_shape = [jax.ShapeDtypeStruct(w.shape, F32) for w in ws for _ in range(4)] + [jax.ShapeDtypeStruct((1, LANES), F32)]
    outs = pl.pallas_call(body, name="adamw_small", in_specs=[vmem] * (1 + 3 * n), out_specs=[vmem] * (4 * n + 1),
                          out_shape=out_shape)(slots, *ws, *ms, *vs)
    return [outs[4 * k:4 * k + 4] for k in range(n)], outs[4 * n]


def _rope_tables(t):
    dim = jnp.arange(LANES) % SWA_HD
    inv_freq = ROPE_THETA ** (-(dim % ROPE_HALF).astype(F32) / ROPE_HALF)
    ang = jnp.arange(t, dtype=F32)[:, None] * jnp.where(dim < 2 * ROPE_HALF, inv_freq, 0.0)[None, :]
    return jnp.cos(ang), jnp.sin(ang)


def _pad_to(a, rows=None, cols=None):
    r = 0 if rows is None else rows - a.shape[0]
    c = 0 if cols is None else cols - a.shape[1]
    return jnp.pad(a, ((0, r), (0, c)))


ORIG0 = dict(gq=(0, 256), gk=(256, 256), gv=(512, 512), glow=(1024, 16), r=(1040, 512), k=(1552, 512), v=(2064, 512),
             xw=(2576, 64), xa=(2640, 64), gate=(2704, 1024))
ORIG0_ORDER = ["gq", "gk", "gv", "glow", "r", "k", "v", "xw", "xa", "gate"]


def _w0t_to_padded(wt):
    rows, at = [], 0
    for name, (off, width) in sorted(C0.items(), key=lambda kv: kv[1][0]):
        assert off == at
        src, src_w = ORIG0[name]
        rows.append(_pad_to(wt[src:src + src_w], rows=width))
        at += width
    rows.append(jnp.zeros((N0P - at, wt.shape[1]), wt.dtype))
    return jnp.concatenate(rows, axis=0)


def _w0t_from_padded(wpt):
    return jnp.concatenate([wpt[C0[n][0]:C0[n][0] + ORIG0[n][1]] for n in ORIG0_ORDER], axis=0)


def _w1t_to_mine(wt):
    return jnp.concatenate([wt[1536:2560], wt[:1536]], axis=0)


def _w1t_from_mine(wt):
    return jnp.concatenate([wt[1024:2560], wt[:1024]], axis=0)


def kernel(x, norm_w, w_in0, gla_gk_up, gla_gk_bias, gla_norm_w, rwkv_mu, rwkv_w0, rwkv_w_up, rwkv_a0, rwkv_a_up, rwkv_k_k, rwkv_k_a, rwkv_r_k, rwkv_ln_w, rwkv_ln_b, w_out0, w_in1, b_in1, attn_sinks, w_out1, b_out1, final_norm_w, loss_target, m_norm_w, m_w_in0, m_gla_gk_up, m_gla_gk_bias, m_gla_norm_w, m_rwkv_mu, m_rwkv_w0, m_rwkv_w_up, m_rwkv_a0, m_rwkv_a_up, m_rwkv_k_k, m_rwkv_k_a, m_rwkv_r_k, m_rwkv_ln_w, m_rwkv_ln_b, m_w_out0, m_w_in1, m_b_in1, m_attn_sinks, m_w_out1, m_b_out1, m_final_norm_w, v_norm_w, v_w_in0, v_gla_gk_up, v_gla_gk_bias, v_gla_norm_w, v_rwkv_mu, v_rwkv_w0, v_rwkv_w_up, v_rwkv_a0, v_rwkv_a_up, v_rwkv_k_k, v_rwkv_k_a, v_rwkv_r_k, v_rwkv_ln_w, v_rwkv_ln_b, v_w_out0, v_w_in1, v_b_in1, v_attn_sinks, v_w_out1, v_b_out1, v_final_norm_w):
    weights = dict(norm_w=norm_w, w_in0=w_in0, gla_gk_up=gla_gk_up, gla_gk_bias=gla_gk_bias, gla_norm_w=gla_norm_w, rwkv_mu=rwkv_mu,
                   rwkv_w0=rwkv_w0, rwkv_w_up=rwkv_w_up, rwkv_a0=rwkv_a0, rwkv_a_up=rwkv_a_up, rwkv_k_k=rwkv_k_k, rwkv_k_a=rwkv_k_a,
                   rwkv_r_k=rwkv_r_k, rwkv_ln_w=rwkv_ln_w, rwkv_ln_b=rwkv_ln_b, w_out0=w_out0, w_in1=w_in1, b_in1=b_in1,
                   attn_sinks=attn_sinks, w_out1=w_out1, b_out1=b_out1, final_norm_w=final_norm_w)
    moms = dict(norm_w=m_norm_w, w_in0=m_w_in0, gla_gk_up=m_gla_gk_up, gla_gk_bias=m_gla_gk_bias, gla_norm_w=m_gla_norm_w,
                rwkv_mu=m_rwkv_mu, rwkv_w0=m_rwkv_w0, rwkv_w_up=m_rwkv_w_up, rwkv_a0=m_rwkv_a0, rwkv_a_up=m_rwkv_a_up,
                rwkv_k_k=m_rwkv_k_k, rwkv_k_a=m_rwkv_k_a, rwkv_r_k=m_rwkv_r_k, rwkv_ln_w=m_rwkv_ln_w, rwkv_ln_b=m_rwkv_ln_b,
                w_out0=m_w_out0, w_in1=m_w_in1, b_in1=m_b_in1, attn_sinks=m_attn_sinks, w_out1=m_w_out1, b_out1=m_b_out1,
                final_norm_w=m_final_norm_w)
    vars_ = dict(norm_w=v_norm_w, w_in0=v_w_in0, gla_gk_up=v_gla_gk_up, gla_gk_bias=v_gla_gk_bias, gla_norm_w=v_gla_norm_w,
                 rwkv_mu=v_rwkv_mu, rwkv_w0=v_rwkv_w0, rwkv_w_up=v_rwkv_w_up, rwkv_a0=v_rwkv_a0, rwkv_a_up=v_rwkv_a_up,
                 rwkv_k_k=v_rwkv_k_k, rwkv_k_a=v_rwkv_k_a, rwkv_r_k=v_rwkv_r_k, rwkv_ln_w=v_rwkv_ln_w, rwkv_ln_b=v_rwkv_ln_b,
                 w_out0=v_w_out0, w_in1=v_w_in1, b_in1=v_b_in1, attn_sinks=v_attn_sinks, w_out1=v_w_out1, b_out1=v_b_out1,
                 final_norm_w=v_final_norm_w)
    names = list(weights)
    big = ["w_in0", "w_out0", "w_in1", "w_out1"]
    small_sharded = ["gla_gk_up", "rwkv_w_up", "rwkv_a_up", "b_in1", "b_out1"]
    replicated = [n for n in names if n not in big and n not in small_sharded]

    xs = x[0]
    tgt = loss_target[0]
    t = xs.shape[0]

    def view(w):
        shape = tuple(w.shape[-2:]) if w.ndim >= 2 else (1, w.shape[0])
        return shape, ("rows" if shape[0] > 1 and shape[1] <= LANES else "flat")

    def layout(ns, row0=0):
        specs = []
        for n in ns:
            shape, mode = view(weights[n])
            specs.append((shape, mode, row0))
            row0 += _packed_rows(shape, mode)
        return specs, row0

    sh_specs, n_shard_rows = layout(small_sharded)
    rep_specs, loss_row = layout(replicated, n_shard_rows)
    sh_modes, rep_modes = [s[1] for s in sh_specs], [s[1] for s in rep_specs]

    small_shard_pack = _pack_small([weights[n].reshape(view(weights[n])[0]) for n in small_sharded], sh_modes)
    g_in0, g_small = _all_gather([w_in0[0].T.astype(BF16), small_shard_pack])
    w0t = _w0t_to_padded(g_in0.reshape(-1, D_MODEL))
    later_shards = [w_out0[0].astype(BF16), w_in1[0].T.astype(BF16), w_out1[0].astype(BF16)]
    gs = [_take_small(g_small, row0, shape, mode) for shape, mode, row0 in sh_specs]
    join_cols = lambda a: jnp.transpose(a, (1, 0, 2)).reshape(a.shape[1], -1)
    gk_up, w_up, a_up = join_cols(gs[0]), join_cols(gs[1]), join_cols(gs[2])
    b_in, b_out = gs[3].reshape(1, -1), gs[4].reshape(1, -1)

    gk_up_p = _pad_to(gk_up, rows=LOW)
    w3, rank = 3 * RWKV_W, rwkv_w_up.shape[1]
    mu = rwkv_mu
    rwkv_params = [mu[:, 0:RWKV_W], mu[:, RWKV_W:2 * RWKV_W], mu[:, 2 * RWKV_W:w3], _pad_to(mu[:, w3:w3 + rank], cols=LOW),
                   _pad_to(mu[:, w3 + rank:], cols=LOW), rwkv_w0, _pad_to(w_up, rows=LOW), rwkv_a0, _pad_to(a_up, rows=LOW),
                   rwkv_k_k, rwkv_k_a, rwkv_r_k.reshape(1, RWKV_W), rwkv_ln_w, rwkv_ln_b]
    bq, bk, bv = b_in[:, :MIX], b_in[:, MIX:MIX + SWA_KV], b_in[:, MIX + SWA_KV:]
    cos, sin = _rope_tables(t)
    nw0, nw1, fw = norm_w[0:1], norm_w[1:2], final_norm_w.reshape(1, D_MODEL)

    d = D_MODEL
    wide = lambda arr: (arr, d, 0)
    silu = lambda g: g * sigmoid(g)
    hn0, proj0 = _matmul_fused("norm0_proj0", rms, w0t, "nt", [wide(xs)], [nw0], [(N0P, F32)], [], lambda acc, x, w: (acc,))
    o_a, gla_states = _gla_fwd(proj0, gk_up_p, gla_gk_bias, gla_norm_w)
    o_b, rwkv_states, rwkv_prevs, (g_out0, g_in1, g_out1) = _rwkv_fwd(proj0, rwkv_params, later_shards, ["gather"] * 3)
    wo0 = g_out0.reshape(MIX, D_MODEL)
    w1t = _w1t_to_mine(g_in1.reshape(-1, D_MODEL))
    wo1 = g_out1.reshape(MIX, D_MODEL)
    og0, h1, hn1 = _matmul_fused(
        "gate0_out0_norm1", lambda oa, ob, gate, x, w: jnp.concatenate([oa, ob], axis=1) * silu(gate), wo0, "nn",
        [(o_a, GLA_VAL, 0), (o_b, RWKV_W, 0), wide(proj0), wide(xs)], [nw1], [(d, F32), (d, BF16)], [],
        lambda acc, oa, ob, gate, x, w: _resid_norm(acc, x, w))
    proj1 = _matmul("proj1", hn1, w1t, "nt", PROJ_ROWS, N1P)
    o_c, kst, vst = _swa_fwd(proj1, cos, sin, bq, bk, bv, attn_sinks)
    og1, dh2, loss_part, d_b_out, d_fw = _matmul_fused(
        "gate1_out1_loss", lambda oc, gate, h, tg, b, w: oc * silu(gate), wo1, "nn",
        [wide(o_c), wide(proj1), wide(h1), wide(tgt)], [b_out, fw], [(d, F32)], [LANES, d, d],
        lambda acc, oc, gate, h, tg, b, w: _loss_head(acc, h, tg, b, w))

    d_oc, d_gate1 = _matmul_fused("out1_dx_gate1", dh2, wo1, "nt", [wide(o_c), wide(proj1)], [], [(d, F32), (d, BF16)], [], _gate_back)
    d_wo1 = _matmul("out1_dw", og1, dh2, "tn", DW_COLS, DW_COLS, BF16)
    dq, dk, dv, d_bq, d_bk, d_bv, d_sinks = _swa_bwd(proj1, cos, sin, bq, bk, bv, attn_sinks, kst, vst, d_oc)
    dproj1 = jnp.concatenate([d_gate1, dq, dk, dv], axis=1)
    dh1, d_nw1 = _matmul_fused("proj1_dx_norm1", dproj1, w1t, "nn", [wide(h1), wide(dh2)], [nw1], [(d, F32)], [d], _norm_back)
    d_w1t = _matmul("proj1_dw", dproj1, hn1, "tn", DW_COLS, d, BF16)
    d_oa, d_ob, d_gate0 = _matmul_fused("out0_dx_gate0", dh1, wo0, "nt", [(o_a, GLA_VAL, 0), (o_b, RWKV_W, 0), wide(proj0)], [],
                                        [(GLA_VAL, F32), (RWKV_W, F32), (d, BF16)], [], _gate_back)
    d_wo0 = _matmul("out0_dw", og0, dh1, "tn", DW_COLS, DW_COLS, BF16)
    dgq, dgk, dgv, dglow, d_gk_up, d_gk_bias, d_gla_nw = _gla_bwd(proj0, gk_up_p, gla_gk_bias, gla_norm_w, gla_states, d_oa)
    row_blocks = lambda a: a.astype(BF16).reshape(N_DEV, -1, D_MODEL)
    early = [row_blocks(_w1t_from_mine(d_w1t)), row_blocks(d_wo1), row_blocks(d_wo0)]
    (dr, dkk, dvv, dxw, dxa), d_rp, (r_in1, r_out1, r_out0) = _rwkv_bwd(
        proj0, rwkv_params, rwkv_states, rwkv_prevs, d_ob, early, ["scatter"] * 3)
    pad = jnp.zeros((t, N0P - C0["xa"][0] - C0["xa"][1]), BF16)
    dproj0 = jnp.concatenate([d_gate0, dgv, dr, dkk, dvv, dgq, dgk, dglow, dxw, dxa, pad], axis=1)
    d_w0 = row_blocks(_w0t_from_padded(_matmul("proj0_dw", dproj0, hn0, "tn", DW_COLS, d, BF16)))
    r_in0 = _sequencer_scatter("exchange_w_in0_grad", d_w0, 0)
    res = {}
    res["w_out0"] = tuple(a[None] for a in _adamw("adamw_w_out0", w_out0[0], r_out0, m_w_out0[0], v_w_out0[0], ADAM_COLS))
    res["w_in1"] = tuple(a.T[None] for a in _adamw("adamw_w_in1", w_in1[0].T, r_in1, m_w_in1[0].T, v_w_in1[0].T, ADAM_COLS))
    res["w_out1"] = tuple(a[None] for a in _adamw("adamw_w_out1", w_out1[0], r_out1, m_w_out1[0], v_w_out1[0], ADAM_COLS))
    grad_x, d_nw0 = _matmul_fused("proj0_dx_norm0", dproj0, w0t, "nn", [wide(xs), wide(dh1)], [nw0], [(d, F32)], [d], _norm_back)

    contrib = dict(
        norm_w=jnp.concatenate([d_nw0, d_nw1], axis=0), gla_gk_bias=d_gk_bias, gla_norm_w=d_gla_nw,
        rwkv_mu=jnp.concatenate([d_rp[0], d_rp[1], d_rp[2], d_rp[3][:, :rank], d_rp[4][:, :rank]], axis=1),
        rwkv_w0=d_rp[5], rwkv_a0=d_rp[7], rwkv_k_k=d_rp[9], rwkv_k_a=d_rp[10], rwkv_r_k=d_rp[11].reshape(RWKV_HEADS, RWKV_N),
        rwkv_ln_w=d_rp[12], rwkv_ln_b=d_rp[13], attn_sinks=d_sinks, final_norm_w=d_fw)
    rep_pack = _pack_small([contrib[n] for n in replicated] + [loss_part[:, :1]], rep_modes + ["flat"])

    d_b_in = jnp.concatenate([d_bq, d_bk, d_bv], axis=1)
    full_small = [d_gk_up[:gk_up.shape[0]], d_rp[6][:rank], d_rp[8][:rank], d_b_in, d_b_out]
    split_cols = lambda a: jnp.transpose(a.reshape(a.shape[0], N_DEV, -1), (1, 0, 2))
    small_parts = [split_cols(a) for a in full_small]
    small_pack = _pack_small(small_parts, sh_modes, lead=True)
    r_small, r_rep = _exchange([small_pack, rep_pack], ["scatter", "gather"])

    small_names = small_sharded + replicated
    slots = jnp.concatenate([r_small, r_rep], axis=1)
    as_2d = lambda a: a.reshape(1, -1) if a.ndim == 1 else a
    small_res, loss_row_out = _adamw_small(slots, sh_specs + rep_specs, [as_2d(weights[n]) for n in small_names],
                                           [as_2d(moms[n]) for n in small_names], [as_2d(vars_[n]) for n in small_names], loss_row)
    for n, vals in zip(small_names, small_res):
        res[n] = tuple(val.reshape(weights[n].shape) for val in vals)
    loss = loss_row_out[0, 0]
    my_idx = 4 * lax.axis_index("x") + 2 * lax.axis_index("y") + lax.axis_index("c")
    r_in0 = lax.dynamic_update_slice(r_in0, lax.dynamic_slice(d_w0, (my_idx, 0, 0), (1,) + d_w0.shape[1:]), (my_idx, 0, 0))
    res["w_in0"] = tuple(a.T[None] for a in _adamw("adamw_w_in0", w_in0[0].T, r_in0, m_w_in0[0].T, v_w_in0[0].T, ADAM_COLS))
    return (loss, grad_x[None], *[res[n][0] for n in names], *[res[n][1] for n in names],
            *[res[n][2] for n in names], *[res[n][3] for n in names])
```

```python
import functools

import jax
import jax.numpy as jnp
from jax import lax
from jax.experimental import pallas as pl
from jax.experimental.pallas import tpu as pltpu
from jax.experimental.pallas import tpu_sc as plsc

F32 = jnp.float32
BF16 = jnp.bfloat16
HI = lax.Precision.HIGHEST

D_MODEL = 1024
NORM_EPS = 1e-5
GLA_HEADS, GLA_DK, GLA_DV = 4, 64, 128
GLA_NORMALIZER = 16.0
GLA_CHUNK = 64
GLA_STEP = 1024
RWKV_HEADS, RWKV_N = 8, 64
RWKV_LN_EPS = 64e-5
RWKV_CHUNK = 128
SWA_Q_HEADS, SWA_KV_HEADS, SWA_GROUP, SWA_HD = 16, 4, 4, 64
WINDOW = 128
SWA_STEP = 512
ROPE_THETA = 500000.0
NEG = -1e30
N_DEV = 8
LANES = 128

ADAM_LR, ADAM_B1, ADAM_B2, ADAM_EPS, ADAM_WD, ADAM_STEP = 0.001, 0.9, 0.999, 1e-08, 0.01, 10

GLA_KEY, GLA_VAL = GLA_HEADS * GLA_DK, GLA_HEADS * GLA_DV
RWKV_W = RWKV_HEADS * RWKV_N
SWA_KV = SWA_KV_HEADS * SWA_HD
MIX = GLA_VAL + RWKV_W
LOW = LANES

N0P = 4096
C0 = dict(gate=(0, MIX), gv=(1024, GLA_VAL), r=(1536, RWKV_W), k=(2048, RWKV_W), v=(2560, RWKV_W), gq=(3072, GLA_KEY),
          gk=(3328, GLA_KEY), glow=(3584, LOW), xw=(3712, LOW), xa=(3840, LOW))
N1P = 2560
C1 = dict(gate=(0, MIX), q=(1024, MIX), k=(2048, SWA_KV), v=(2304, SWA_KV))

VMEM_LIMIT = 56 * 1024 * 1024

P_LORA = 1
P_GLA = 1
P_RWKV_G = 2
P_RWKV = 1
P_SWA = 1


def _cparams(sem=None):
    return pltpu.CompilerParams(dimension_semantics=sem, vmem_limit_bytes=VMEM_LIMIT)


DIMS = dict(nn=(((1,), (0,)), ((), ())), nt=(((1,), (1,)), ((), ())), tn=(((0,), (0,)), ((), ())))


def _split_bf16(a):
    hi = a.astype(BF16)
    return hi, (a - hi.astype(F32)).astype(BF16)


def _dot(a, b, mode, passes):
    dg = lambda p, q: lax.dot_general(p, q, DIMS[mode], preferred_element_type=F32)
    if passes == 1:
        return dg(a.astype(BF16), b.astype(BF16))
    if passes == 2:
        ah, (bh, bl) = a.astype(BF16), _split_bf16(b)
        return dg(ah, bh) + dg(ah, bl)
    if passes == 3:
        (ah, al), (bh, bl) = _split_bf16(a), _split_bf16(b)
        return dg(ah, bh) + dg(al, bh) + dg(ah, bl)
    return lax.dot_general(a, b, DIMS[mode], precision=HI, preferred_element_type=F32)


@functools.partial(jax.custom_vjp, nondiff_argnums=(2, 3))
def mmx(a, b, mode, passes):
    return _dot(a, b, mode, passes)


def _mmx_fwd(a, b, mode, passes):
    return _dot(a, b, mode, passes), (a, b)


def _mmx_bwd(mode, passes, res, g):
    a, b = res
    if mode == "nn":
        return _dot(g, b, "nt", passes), _dot(a, g, "tn", passes)
    if mode == "nt":
        return _dot(g, b, "nn", passes), _dot(g, a, "tn", passes)
    return _dot(b, g, "nt", passes), _dot(a, g, "nn", passes)


mmx.defvjp(_mmx_fwd, _mmx_bwd)


def _tri_dot(tri, x):
    t = tri.astype(BF16)
    x1 = x.astype(BF16)
    r1 = x - x1.astype(F32)
    x2 = r1.astype(BF16)
    x3 = (r1 - x2.astype(F32)).astype(BF16)
    dg = lambda q: jnp.dot(t, q, preferred_element_type=F32)
    return dg(x1) + dg(x2) + dg(x3)


@jax.custom_vjp
def cumsum_rows(x):
    return _tri_dot(tril_ones(x.shape[0]), x)


def _cumsum_fwd(x):
    return cumsum_rows(x), None


def _cumsum_bwd(_, g):
    i, j = _iota2(g.shape[0], g.shape[0])
    return (_tri_dot(jnp.where(i <= j, 1.0, 0.0).astype(F32), g),)


cumsum_rows.defvjp(_cumsum_fwd, _cumsum_bwd)


def _head_dot(x):
    i, j = _iota2(LANES, LANES)
    shift = RWKV_N.bit_length() - 1
    same = jnp.where(jnp.right_shift(i, shift) == jnp.right_shift(j, shift), 1.0, 0.0).astype(F32)
    return jnp.concatenate([_ones_right(x[:, g * LANES:(g + 1) * LANES], same) for g in range(x.shape[1] // LANES)], axis=1)


def _ones_right(x, ones):
    t = ones.astype(BF16)
    x1 = x.astype(BF16)
    x2 = (x - x1.astype(F32)).astype(BF16)
    dg = lambda q: jnp.dot(q, t, preferred_element_type=F32)
    return dg(x1) + dg(x2)


@jax.custom_vjp
def head_sum(x):
    return _head_dot(x)


def _head_sum_fwd(x):
    return head_sum(x), None


def _head_sum_bwd(_, g):
    return (_head_dot(g),)


head_sum.defvjp(_head_sum_fwd, _head_sum_bwd)


def cat_rows(*xs):
    return jnp.concatenate(xs, axis=0)


def _iota2(n, m):
    return lax.broadcasted_iota(jnp.int32, (n, m), 0), lax.broadcasted_iota(jnp.int32, (n, m), 1)


def tril_ones(c, strict=False):
    i, j = _iota2(c, c)
    return jnp.where((i > j) if strict else (i >= j), 1.0, 0.0).astype(F32)


def row_of(x, r):
    i = lax.broadcasted_iota(jnp.int32, x.shape, 0)
    return jnp.sum(jnp.where(i == r, x, 0.0), axis=0, keepdims=True)


@jax.custom_vjp
def shift_rows(x, prev):
    r = lax.broadcasted_iota(jnp.int32, x.shape, 0)
    return jnp.where(r == 0, prev, pltpu.roll(x, 1, 0))


def _shift_fwd(x, prev):
    return shift_rows(x, prev), None


def _shift_bwd(_, g):
    c = g.shape[0]
    r = lax.broadcasted_iota(jnp.int32, g.shape, 0)
    return jnp.where(r == c - 1, 0.0, pltpu.roll(g, c - 1, 0)), row_of(g, 0)


shift_rows.defvjp(_shift_fwd, _shift_bwd)


def log_sigmoid(x):
    return jnp.minimum(x, 0.0) - jnp.log(1.0 + jnp.exp(-jnp.abs(x)))


def softplus(x):
    return jnp.maximum(x, 0.0) + jnp.log(1.0 + jnp.exp(-jnp.abs(x)))


def sigmoid(x):
    return 1.0 / (1.0 + jnp.exp(-x))


def rms(x, w, eps=NORM_EPS):
    return x * lax.rsqrt(jnp.mean(x * x, axis=-1, keepdims=True) + eps) * w


def gla_chunk(state, toks, params):
    q, k, v, glow = toks
    gk_up, bias, norm_w = params
    c = GLA_CHUNK
    subs, heads = range(glow.shape[0] // c), range(GLA_HEADS)
    rows = lambda x, j: x[j * c:(j + 1) * c]
    hk = lambda x, h: x[:, h * GLA_DK:(h + 1) * GLA_DK]
    hv = lambda x, h: x[:, h * GLA_DV:(h + 1) * GLA_DV]
    ltri = tril_ones(c)
    g = log_sigmoid(mmx(glow, gk_up, "nn", P_LORA) + bias) / GLA_NORMALIZER
    b = [cumsum_rows(rows(g, j)) for j in subs]
    ref = [lax.stop_gradient(row_of(b[j], c // 2)) for j in subs]
    last = [row_of(b[j], c - 1) for j in subs]
    ql = [rows(q, j) * (GLA_DK ** -0.5) * jnp.exp(b[j] - ref[j]) for j in subs]
    kr = [rows(k, j) * jnp.exp(ref[j] - b[j]) for j in subs]
    kl = [rows(k, j) * jnp.exp(last[j] - b[j]) for j in subs]
    vj = [rows(v, j) for j in subs]
    e_ref, e_last = [jnp.exp(x) for x in ref], [jnp.exp(x) for x in last]
    att = [[mmx(hk(ql[j], h), hk(kr[j], h), "nt", P_GLA) * ltri for h in heads] for j in subs]
    o_in = [[mmx(att[j][h], hv(vj[j], h), "nn", P_GLA) for h in heads] for j in subs]
    kv = [[mmx(hv(vj[j], h), hk(kl[j], h), "tn", P_GLA) for h in heads] for j in subs]
    o = []
    for j in subs:
        o.append([o_in[j][h] + mmx(hk(ql[j], h), state[h] * hk(e_ref[j], h), "nt", P_GLA) for h in heads])
        state = [state[h] * hk(e_last[j], h) + kv[j][h] for h in heads]
    o = [[x * lax.rsqrt(jnp.mean(x * x, axis=-1, keepdims=True) + NORM_EPS) * norm_w for x in oj] for oj in o]
    return cat_rows(*[jnp.concatenate(oj, axis=1) for oj in o]), state


SOLVE_BLOCK = 128


def solve_unit_lower(ps, ws):
    n = ps[0].shape[0]
    heads = range(len(ps))
    if n > SOLVE_BLOCK:
        half = n // 2
        top = solve_unit_lower([p[:half, :half] for p in ps], [w[:half] for w in ws])
        rest = [ws[h][half:] + mmx(ps[h][half:, :half], top[h], "nn", P_RWKV) for h in heads]
        bottom = solve_unit_lower([p[half:, half:] for p in ps], rest)
        return [cat_rows(top[h], bottom[h]) for h in heads]
    u, p = ws, ps
    levels = max(1, (n - 1).bit_length())
    for it in range(levels):
        if it + 1 < levels:
            y = [mmx(p[h], jnp.concatenate([p[h], u[h]], axis=1), "nn", P_RWKV) for h in heads]
            u = [u[h] + y[h][:, n:] for h in heads]
            p = [y[h][:, :n] for h in heads]
        else:
            u = [u[h] + mmx(p[h], u[h], "nn", P_RWKV) for h in heads]
    return u


def rwkv_chunk(state, toks, params):
    S, pr, pk, pv, pxw, pxa = state
    r_, k_, v_, xw_, xa_ = toks
    mu_r, mu_k, mu_v, mu_xw, mu_xa, w0, w_up, a0, a_up, k_k, k_a, r_k, ln_w, ln_b = params
    c, n = xw_.shape[0], RWKV_N
    heads = range(RWKV_HEADS)
    hs = lambda x, h: x[:, h * n:(h + 1) * n]
    ltri = tril_ones(c)
    stri = tril_ones(c, strict=True)

    def lerp(x, prev, mu):
        return x + (shift_rows(x, prev) - x) * mu

    xw = jnp.tanh(lerp(xw_, pxw, mu_xw))
    xa = lerp(xa_, pxa, mu_xa)
    r = lerp(r_, pr, mu_r)
    k = lerp(k_, pk, mu_k)
    v = lerp(v_, pv, mu_v)
    w = -softplus(-(w0 + mmx(xw, w_up, "nn", P_LORA))) - 0.5
    lw = -jnp.exp(w)
    asig = sigmoid(a0 + mmx(xa, a_up, "nn", P_LORA))
    kk = k * k_k
    kk = kk * lax.rsqrt(jnp.maximum(head_sum(kk * kk), 1e-24))
    k2 = k * (1.0 + (asig - 1.0) * k_a)
    b = kk * asig
    cum = cumsum_rows(lw)
    ref = lax.stop_gradient(row_of(cum, c // 2))
    last = row_of(cum, c - 1)
    at = -kk * jnp.exp(cum - lw - ref)
    rt = r * jnp.exp(cum - ref)
    e_out = jnp.exp(ref - cum)
    bt, kt = b * e_out, k2 * e_out
    e_tail = jnp.exp(last - cum)
    bl, kl = b * e_tail, k2 * e_tail
    e_ref, e_last = jnp.exp(ref), jnp.exp(last)
    g = [mmx(cat_rows(hs(at, h), hs(rt, h)), cat_rows(hs(bt, h), hs(kt, h), S[h] * hs(e_ref, h)), "nt", P_RWKV_G) for h in heads]
    aab = [x[:c, :c] * stri for x in g]
    aak = [x[:c, c:2 * c] * stri for x in g]
    arb = [x[c:, :c] * ltri for x in g]
    ark = [x[c:, c:2 * c] * ltri for x in g]
    av = [mmx(cat_rows(aak[h], ark[h]), hs(v, h), "nn", P_RWKV) for h in heads]
    u = solve_unit_lower(aab, [g[h][:c, 2 * c:] + av[h][:c] for h in heads])
    o = [g[h][c:, 2 * c:] + av[h][c:] + mmx(arb[h], u[h], "nn", P_RWKV) for h in heads]
    s1 = [S[h] * hs(e_last, h) + mmx(cat_rows(u[h], hs(v, h)), cat_rows(hs(bl, h), hs(kl, h)), "tn", P_RWKV) for h in heads]
    o = jnp.concatenate(o, axis=1)
    d = o - head_sum(o) * (1.0 / n)
    var = head_sum(d * d) * (1.0 / n)
    o = d * lax.rsqrt(var + RWKV_LN_EPS) * ln_w + ln_b + head_sum(r * k2 * r_k) * v
    new_state = (s1, row_of(r_, c - 1), row_of(k_, c - 1), row_of(v_, c - 1), row_of(xw_, c - 1), row_of(xa_, c - 1))
    return o, new_state


RWKV_STEP = 256


def rwkv_chunks(state, toks, params):
    outs = []
    for j in range(toks[3].shape[0] // RWKV_CHUNK):
        rows = slice(j * RWKV_CHUNK, (j + 1) * RWKV_CHUNK)
        o, state = rwkv_chunk(state, tuple(t[rows] for t in toks), params)
        outs.append(o)
    return cat_rows(*outs), state


ROPE_HALF = 8


def _rot_half_raw(x):
    lane = lax.broadcasted_iota(jnp.int32, (x.shape[0], LANES), 1) & (SWA_HD - 1)
    out = []
    for i in range(x.shape[1] // LANES):
        g = x[:, i * LANES:(i + 1) * LANES]
        up, down = pltpu.roll(g, LANES - ROPE_HALF, 1), pltpu.roll(g, ROPE_HALF, 1)
        out.append(jnp.where(lane < ROPE_HALF, -up, jnp.where(lane < 2 * ROPE_HALF, down, 0.0)))
    return out[0] if len(out) == 1 else jnp.concatenate(out, axis=1)


@jax.custom_vjp
def rot_half(x):
    return _rot_half_raw(x)


rot_half.defvjp(lambda x: (_rot_half_raw(x), None), lambda _, g: (-_rot_half_raw(g),))


def rope(x, cos2, sin2):
    reps = x.shape[1] // LANES
    tile = lambda t: t if reps == 1 else jnp.concatenate([t] * reps, axis=1)
    return x * tile(cos2) + rot_half(x) * tile(sin2)


def swa_chunk(state, toks, params, first):
    kprev, vprev = state
    q_, k_, v_, cos, sin = toks
    bq, bk, bv, sinks = params
    c, ng = WINDOW, SWA_GROUP
    n_sub = cos.shape[0] // c
    units = [(j, g) for j in range(n_sub) for g in range(SWA_KV_HEADS)]
    rows = lambda x, j: x[j * c:(j + 1) * c]
    hs = lambda g: range(g * ng, (g + 1) * ng)
    head = lambda x, h: x[:, h * SWA_HD:(h + 1) * SWA_HD]
    qi, kj = _iota2(ng * c, 2 * c)
    qpos = qi & (c - 1)
    cur_ok = (kj >= c) & (qpos >= kj - c)
    prev_ok = (kj < c) & (kj > qpos)
    ok = [cur_ok | (prev_ok & jnp.logical_not(first))] + [cur_ok | prev_ok] * (n_sub - 1)
    q_all = rope(q_ + bq, cos, sin) * (SWA_HD ** -0.5)
    k_all = rope(k_ + bk, cos, sin)
    v_all = v_ + bv
    k = {(j, g): rows(head(k_all, g), j) for j, g in units}
    v = {(j, g): rows(head(v_all, g), j) for j, g in units}
    q = {(j, g): cat_rows(*[rows(head(q_all, h), j) for h in hs(g)]) for j, g in units}
    kp = lambda j, g: kprev[g] if j == 0 else k[(j - 1, g)]
    vp = lambda j, g: vprev[g] if j == 0 else v[(j - 1, g)]
    s = {(j, g): jnp.where(ok[j], mmx(q[(j, g)], cat_rows(kp(j, g), k[(j, g)]), "nt", P_SWA), NEG) for j, g in units}
    sink = [cat_rows(*[jnp.broadcast_to(sinks[h], (c, 1)) for h in hs(g)]) for g in range(SWA_KV_HEADS)]
    m = {(j, g): lax.stop_gradient(jnp.maximum(jnp.max(s[(j, g)], axis=-1, keepdims=True), sink[g])) for j, g in units}
    p = {u: jnp.exp(s[u] - m[u]) for u in units}
    ones = jnp.ones((2 * c, SWA_HD), F32)
    pv = {(j, g): mmx(p[(j, g)], cat_rows(vp(j, g), v[(j, g)]), "nn", P_SWA) for j, g in units}
    den = {u: mmx(p[u], ones, "nn", P_SWA) for u in units}
    o = {(j, g): pv[(j, g)] / (den[(j, g)] + jnp.exp(sink[g] - m[(j, g)])) for j, g in units}
    outs = [cat_rows(*[o[(j, g)][i * c:(i + 1) * c] for j in range(n_sub)]) for g in range(SWA_KV_HEADS) for i in range(ng)]
    last = n_sub - 1
    return outs, ([k[(last, g)] for g in range(SWA_KV_HEADS)], [v[(last, g)] for g in range(SWA_KV_HEADS)])


def _heads(ref, n, w, rows=slice(None)):
    return [ref[rows, h * w:(h + 1) * w] for h in range(n)]


def _put_heads(ref, vals, w, rows=slice(None), add=False):
    for h, val in enumerate(vals):
        if add:
            ref[rows, h * w:(h + 1) * w] += val
        else:
            ref[rows, h * w:(h + 1) * w] = val


def _col(block_w, name, table):
    off, w = table[name]
    assert off % block_w == 0 and w % block_w == 0
    return off // block_w


def _tok_spec(c, w, colblock, n=None):
    if n is None:
        return pl.BlockSpec((c, w), lambda i: (i, colblock))
    return pl.BlockSpec((c, w), lambda i: (n - 1 - i, colblock))


def _full_spec(shape):
    return pl.BlockSpec(shape, lambda i: (0,) * len(shape))


def _matmul(name, a, b, mode, tm, tn, out_dtype=F32):
    (m, kd) = (a.shape[1], a.shape[0]) if mode == "tn" else a.shape
    n = b.shape[0] if mode == "nt" else b.shape[1]
    assert m % tm == 0 and n % tn == 0
    a_spec = pl.BlockSpec((kd, tm), lambda j, i: (0, i)) if mode == "tn" else pl.BlockSpec((tm, kd), lambda j, i: (i, 0))
    b_spec = pl.BlockSpec((tn, kd), lambda j, i: (j, 0)) if mode == "nt" else pl.BlockSpec((kd, tn), lambda j, i: (0, j))

    def body(a_ref, b_ref, o_ref):
        o_ref[...] = lax.dot_general(a_ref[...].astype(BF16), b_ref[...].astype(BF16), DIMS[mode],
                                     preferred_element_type=F32).astype(out_dtype)

    return pl.pallas_call(
        body, name=name, grid=(n // tn, m // tm), in_specs=[a_spec, b_spec],
        out_specs=pl.BlockSpec((tm, tn), lambda j, i: (i, j)), out_shape=jax.ShapeDtypeStruct((m, n), out_dtype),
        compiler_params=_cparams(("arbitrary", "arbitrary")))(a, b)


TOK_TILE = 512
PROJ_ROWS = 1024
DW_COLS = 512
ADAM_COLS = 256


def _matmul_fused(name, a, b, mode, tiles, rows, outs, sums, epilogue, comm=(), kinds=()):
    made = callable(a)
    m = tiles[0][0].shape[0] if made else a.shape[0]
    kd = b.shape[0] if mode == "nn" else b.shape[1]
    n = b.shape[1] if mode == "nn" else b.shape[0]
    tm = TOK_TILE
    steps = m // tm
    if made:
        outs = [(kd, BF16)] + list(outs)
    nt_, nr, no, ns, ncomm = len(tiles), len(rows), len(outs), len(sums), len(comm)

    def body(*refs):
        at = 1 if made else 2
        b_ref = refs[at - 1]
        tile_refs, row_refs, comm_in = refs[at:at + nt_], refs[at + nt_:at + nt_ + nr], refs[at + nt_ + nr:at + nt_ + nr + ncomm]
        at += nt_ + nr + ncomm
        out_refs, sum_refs, comm_out = refs[at:at + no], refs[at + no:at + no + ns], refs[at + no + ns:at + no + ns + ncomm]
        sems = refs[at + no + ns + ncomm:]
        i = pl.program_id(0)

        @pl.when(i == 0)
        def _():
            if ncomm:
                _comm_start(*_comm_copies(comm_in, comm_out, kinds, *sems))
            for ref in sum_refs:
                ref[...] = jnp.zeros_like(ref)

        extras = [r[...] for r in tile_refs] + [r[...] for r in row_refs]
        a_blk = (a(*extras) if made else refs[0][...]).astype(BF16)
        acc = lax.dot_general(a_blk, b_ref[...].astype(BF16), DIMS[mode], preferred_element_type=F32)
        res = epilogue(acc, *extras)
        if made:
            res = (a_blk,) + tuple(res)
        for ref, val in zip(out_refs, res[:no]):
            ref[...] = val.astype(ref.dtype)
        for ref, val in zip(sum_refs, res[no:]):
            ref[...] += val

        if ncomm:
            @pl.when(i == steps - 1)
            def _():
                _comm_wait(*_comm_copies(comm_in, comm_out, kinds, *sems))

    in_specs = ([] if made else [pl.BlockSpec((tm, kd), lambda i: (i, 0))]) + [_full_spec(b.shape)]
    in_specs += [pl.BlockSpec((tm, w), functools.partial(lambda i, cb: (i, cb), cb=cb)) for _, w, cb in tiles]
    in_specs += [_full_spec(r.shape) for r in rows] + [ANY] * ncomm
    out_specs = [pl.BlockSpec((tm, w), lambda i: (i, 0)) for w, _ in outs] + [_full_spec((1, w)) for w in sums] + [ANY] * ncomm
    out_shape = ([jax.ShapeDtypeStruct((m, w), dt) for w, dt in outs] + [jax.ShapeDtypeStruct((1, w), F32) for w in sums]
                 + _comm_out_shapes(comm, kinds))
    return pl.pallas_call(body, name=name, grid=(steps,), in_specs=in_specs, out_specs=out_specs, out_shape=out_shape,
                          scratch_shapes=_comm_scratch(ncomm) if ncomm else [],
                          compiler_params=_cparams(("arbitrary",)))(*([] if made else [a]), b, *[t[0] for t in tiles], *rows, *comm)


def _resid_norm(y, x, w):
    h = x + y
    return h, rms(h, w)


def _norm_back(dhn, h, dres, w):
    _, vjp = jax.vjp(rms, h, w)
    dh, dw = vjp(dhn)
    return dh + dres, dw


def _gate_back(dog, *o_and_gate):
    outs, g = o_and_gate[:-1], o_and_gate[-1]
    s = sigmoid(g)
    silu, dsilu = g * s, s * (1.0 + g * (1.0 - s))
    d_outs, c = [], 0
    for o in outs:
        w = o.shape[1]
        d_outs.append(dog[:, c:c + w] * silu[:, c:c + w])
        c += w
    o_all = outs[0] if len(outs) == 1 else jnp.concatenate(outs, axis=1)
    return (*d_outs, dog * o_all * dsilu)


def _loss_head(y1, h1, target, b_out, fw):
    def f(h2, w):
        err = rms(h2, w) - target
        return 0.5 * jnp.sum(jnp.mean(err * err, axis=-1, keepdims=True), axis=0, keepdims=True)

    loss, vjp = jax.vjp(f, h1 + y1 + b_out, fw)
    dh2, dfw = vjp(jnp.ones((1, 1), F32))
    return dh2, jnp.broadcast_to(loss, (1, LANES)), jnp.sum(dh2, axis=0, keepdims=True), dfw


def _gla_load(q_ref, k_ref, v_ref, gl_ref, up_ref, bias_ref, nw_ref):
    toks = (q_ref[...], k_ref[...], v_ref[...], gl_ref[...])
    params = (up_ref[...], bias_ref[...], nw_ref[...])
    return toks, params


def _gla_specs(c, n=None):
    toks = [_tok_spec(c, GLA_KEY, _col(GLA_KEY, "gq", C0), n), _tok_spec(c, GLA_KEY, _col(GLA_KEY, "gk", C0), n),
            _tok_spec(c, GLA_VAL, _col(GLA_VAL, "gv", C0), n), _tok_spec(c, LOW, _col(LOW, "glow", C0), n)]
    return toks, [_full_spec(s) for s in GLA_PARAM_SHAPES]


GLA_PARAM_SHAPES = [(LOW, GLA_KEY), (1, GLA_KEY), (1, GLA_DV)]
GLA_STATE = (GLA_HEADS * GLA_DV, GLA_DK)


def _gla_fwd(proj0, gk_up, gk_bias, norm_w):
    t = proj0.shape[0]
    c = GLA_STEP
    nc = t // c
    toks_s, params_s = _gla_specs(c)

    def body(q_ref, k_ref, v_ref, gl_ref, up_ref, bias_ref, nw_ref, o_ref, st_ref, s_scr):
        @pl.when(pl.program_id(0) == 0)
        def _():
            s_scr[...] = jnp.zeros_like(s_scr)

        st_ref[...] = s_scr[...]
        toks, params = _gla_load(q_ref, k_ref, v_ref, gl_ref, up_ref, bias_ref, nw_ref)
        state = [s_scr[h * GLA_DV:(h + 1) * GLA_DV, :] for h in range(GLA_HEADS)]
        o_ref[...], new = gla_chunk(state, toks, params)
        for h in range(GLA_HEADS):
            s_scr[h * GLA_DV:(h + 1) * GLA_DV, :] = new[h]

    return pl.pallas_call(
        body, name="gla_fwd", grid=(nc,), in_specs=toks_s + params_s,
        out_specs=(_tok_spec(c, GLA_VAL, 0), pl.BlockSpec(GLA_STATE, lambda i: (i, 0))),
        out_shape=(jax.ShapeDtypeStruct((t, GLA_VAL), F32), jax.ShapeDtypeStruct((nc * GLA_STATE[0], GLA_DK), F32)),
        scratch_shapes=[pltpu.VMEM(GLA_STATE, F32)], compiler_params=_cparams(("arbitrary",)))(
            proj0, proj0, proj0, proj0, gk_up, gk_bias, norm_w)


def _gla_bwd(proj0, gk_up, gk_bias, norm_w, states, do):
    t = proj0.shape[0]
    c = GLA_STEP
    nc = t // c
    toks_s, params_s = _gla_specs(c, nc)

    def body(q_ref, k_ref, v_ref, gl_ref, up_ref, bias_ref, nw_ref, st_ref, do_ref,
             dq_ref, dk_ref, dv_ref, dgl_ref, dup_ref, dbias_ref, dnw_ref, ds_scr):
        @pl.when(pl.program_id(0) == 0)
        def _():
            ds_scr[...] = jnp.zeros_like(ds_scr)
            dup_ref[...] = jnp.zeros_like(dup_ref)
            dbias_ref[...] = jnp.zeros_like(dbias_ref)
            dnw_ref[...] = jnp.zeros_like(dnw_ref)

        toks, params = _gla_load(q_ref, k_ref, v_ref, gl_ref, up_ref, bias_ref, nw_ref)
        rows = lambda h: slice(h * GLA_DV, (h + 1) * GLA_DV)
        state = [st_ref[rows(h), :] for h in range(GLA_HEADS)]
        _, vjp = jax.vjp(gla_chunk, state, toks, params)
        dstate_in = [ds_scr[rows(h), :] for h in range(GLA_HEADS)]
        dstate, dtoks, (dup, dbias, dnw) = vjp((do_ref[...], dstate_in))
        for ref, val in zip((dq_ref, dk_ref, dv_ref, dgl_ref), dtoks):
            ref[...] = val.astype(ref.dtype)
        dup_ref[...] += dup
        dbias_ref[...] += dbias
        dnw_ref[...] += dnw
        for h in range(GLA_HEADS):
            ds_scr[rows(h), :] = dstate[h]

    rev = lambda w: pl.BlockSpec((c, w), lambda i: (nc - 1 - i, 0))
    tok_widths = (GLA_KEY, GLA_KEY, GLA_VAL, LOW)
    return pl.pallas_call(
        body, name="gla_bwd", grid=(nc,),
        in_specs=toks_s + params_s + [pl.BlockSpec(GLA_STATE, lambda i: (nc - 1 - i, 0)), rev(GLA_VAL)],
        out_specs=[rev(w) for w in tok_widths] + params_s,
        out_shape=[jax.ShapeDtypeStruct((t, w), BF16) for w in tok_widths] + [jax.ShapeDtypeStruct(s, F32) for s in GLA_PARAM_SHAPES],
        scratch_shapes=[pltpu.VMEM(GLA_STATE, F32)], compiler_params=_cparams(("arbitrary",)))(
            proj0, proj0, proj0, proj0, gk_up, gk_bias, norm_w, states, do)


RWKV_PARAM_SHAPES = [(1, RWKV_W), (1, RWKV_W), (1, RWKV_W), (1, LOW), (1, LOW), (1, RWKV_W), (LOW, RWKV_W), (1, RWKV_W),
                     (LOW, RWKV_W), (1, RWKV_W), (1, RWKV_W), (1, RWKV_W), (1, RWKV_W), (1, RWKV_W)]
RWKV_STATE = (RWKV_HEADS * RWKV_N, RWKV_N)
RWKV_TOK_WIDTHS = (RWKV_W, RWKV_W, RWKV_W, LOW, LOW)
PREV_W = sum(RWKV_TOK_WIDTHS)
PREV_COLS = [slice(sum(RWKV_TOK_WIDTHS[:i]), sum(RWKV_TOK_WIDTHS[:i + 1])) for i in range(len(RWKV_TOK_WIDTHS))]


def _rwkv_load(r_ref, k_ref, v_ref, xw_ref, xa_ref, p_refs):
    toks = (r_ref[...], k_ref[...], v_ref[...], xw_ref[...], xa_ref[...])
    return toks, tuple(p[...] for p in p_refs)


def _rwkv_state(s_ref, prev_ref):
    n = RWKV_N
    S = [s_ref[h * n:(h + 1) * n, :] for h in range(RWKV_HEADS)]
    return (S,) + tuple(prev_ref[0:1, cols] for cols in PREV_COLS)


def _rwkv_put_state(s_ref, prev_ref, state):
    n = RWKV_N
    for h in range(RWKV_HEADS):
        s_ref[h * n:(h + 1) * n, :] = state[0][h]
    for cols, val in zip(PREV_COLS, state[1:]):
        prev_ref[0:1, cols] = val


def _rwkv_specs(c, n=None):
    toks = [_tok_spec(c, w, _col(w, name, C0), n) for name, w in zip(("r", "k", "v", "xw", "xa"), RWKV_TOK_WIDTHS)]
    return toks, [_full_spec(s) for s in RWKV_PARAM_SHAPES]


def _rwkv_fwd(proj0, params, comm, kinds):
    t = proj0.shape[0]
    c = RWKV_STEP
    nc = t // c
    toks_s, params_s = _rwkv_specs(c)
    npar, ncomm = len(params), len(comm)

    def body(*refs):
        tok_refs, p_refs = refs[:5], refs[5:5 + npar]
        comm_in = refs[5 + npar:5 + npar + ncomm]
        o_ref, st_ref, pst_ref = refs[5 + npar + ncomm:8 + npar + ncomm]
        comm_out = refs[8 + npar + ncomm:8 + npar + 2 * ncomm]
        s_scr, prev_scr = refs[8 + npar + 2 * ncomm:10 + npar + 2 * ncomm]
        sems = refs[10 + npar + 2 * ncomm:]
        i = pl.program_id(0)

        @pl.when(i == 0)
        def _():
            _comm_start(*_comm_copies(comm_in, comm_out, kinds, *sems))
            s_scr[...] = jnp.zeros_like(s_scr)
            prev_scr[...] = jnp.zeros_like(prev_scr)

        st_ref[...] = s_scr[...]
        pst_ref[...] = prev_scr[...]
        toks, prm = _rwkv_load(*tok_refs, p_refs)
        o_ref[...], new = rwkv_chunks(_rwkv_state(s_scr, prev_scr), toks, prm)
        _rwkv_put_state(s_scr, prev_scr, new)

        @pl.when(i == nc - 1)
        def _():
            _comm_wait(*_comm_copies(comm_in, comm_out, kinds, *sems))

    outs = pl.pallas_call(
        body, name="rwkv_fwd", grid=(nc,), in_specs=toks_s + params_s + [ANY] * ncomm,
        out_specs=[_tok_spec(c, RWKV_W, 0), pl.BlockSpec(RWKV_STATE, lambda i: (i, 0)), pl.BlockSpec((8, PREV_W), lambda i: (i, 0))]
        + [ANY] * ncomm,
        out_shape=[jax.ShapeDtypeStruct((t, RWKV_W), F32), jax.ShapeDtypeStruct((nc * RWKV_STATE[0], RWKV_N), F32),
                   jax.ShapeDtypeStruct((nc * 8, PREV_W), F32)] + _comm_out_shapes(comm, kinds),
        scratch_shapes=[pltpu.VMEM(RWKV_STATE, F32), pltpu.VMEM((8, PREV_W), F32)] + _comm_scratch(ncomm),
        compiler_params=_cparams(("arbitrary",)))(proj0, proj0, proj0, proj0, proj0, *params, *comm)
    return outs[0], outs[1], outs[2], outs[3:]


def _rwkv_bwd(proj0, params, states, prevs, do, comm, kinds):
    t = proj0.shape[0]
    c = RWKV_STEP
    nc = t // c
    toks_s, params_s = _rwkv_specs(c, nc)
    npar, ncomm = len(params), len(comm)

    def body(*refs):
        tok_refs, p_refs = refs[:5], refs[5:5 + npar]
        st_ref, pst_ref, do_ref = refs[5 + npar:8 + npar]
        comm_in = refs[8 + npar:8 + npar + ncomm]
        outs = refs[8 + npar + ncomm:]
        dtok_refs, dp_refs, comm_out = outs[:5], outs[5:5 + npar], outs[5 + npar:5 + npar + ncomm]
        ds_scr, dprev_scr = outs[5 + npar + ncomm:7 + npar + ncomm]
        sems = outs[7 + npar + ncomm:]
        i = pl.program_id(0)

        @pl.when(i == 0)
        def _():
            _comm_start(*_comm_copies(comm_in, comm_out, kinds, *sems))
            ds_scr[...] = jnp.zeros_like(ds_scr)
            dprev_scr[...] = jnp.zeros_like(dprev_scr)
            for dp in dp_refs:
                dp[...] = jnp.zeros_like(dp)

        toks, prm = _rwkv_load(*tok_refs, p_refs)
        _, vjp = jax.vjp(rwkv_chunks, _rwkv_state(st_ref, pst_ref), toks, prm)
        dstate, dtoks, dprm = vjp((do_ref[...], _rwkv_state(ds_scr, dprev_scr)))
        for ref, val in zip(dtok_refs, dtoks):
            ref[...] = val.astype(ref.dtype)
        for ref, val in zip(dp_refs, dprm):
            ref[...] += val
        _rwkv_put_state(ds_scr, dprev_scr, dstate)

        @pl.when(i == nc - 1)
        def _():
            _comm_wait(*_comm_copies(comm_in, comm_out, kinds, *sems))

    rev = lambda w: pl.BlockSpec((c, w), lambda i: (nc - 1 - i, 0))
    outs = pl.pallas_call(
        body, name="rwkv_bwd", grid=(nc,),
        in_specs=toks_s + params_s + [pl.BlockSpec(RWKV_STATE, lambda i: (nc - 1 - i, 0)),
                                      pl.BlockSpec((8, PREV_W), lambda i: (nc - 1 - i, 0)), rev(RWKV_W)] + [ANY] * ncomm,
        out_specs=[rev(w) for w in RWKV_TOK_WIDTHS] + params_s + [ANY] * ncomm,
        out_shape=[jax.ShapeDtypeStruct((t, w), BF16) for w in RWKV_TOK_WIDTHS]
        + [jax.ShapeDtypeStruct(s, F32) for s in RWKV_PARAM_SHAPES] + _comm_out_shapes(comm, kinds),
        scratch_shapes=[pltpu.VMEM(RWKV_STATE, F32), pltpu.VMEM((8, PREV_W), F32)] + _comm_scratch(ncomm),
        compiler_params=_cparams(("arbitrary",)))(proj0, proj0, proj0, proj0, proj0, *params, states, prevs, do, *comm)
    return outs[:5], outs[5:5 + npar], outs[5 + npar:]


def _swa_load(q_ref, k_ref, v_ref, cos_ref, sin_ref, bq_ref, bk_ref, bv_ref, sk_ref):
    toks = (q_ref[...], k_ref[...], v_ref[...], cos_ref[...], sin_ref[...])
    params = (bq_ref[...], bk_ref[...], bv_ref[...], _heads(sk_ref, SWA_Q_HEADS, 1))
    return toks, params


SWA_TOK_WIDTHS = (MIX, SWA_KV, SWA_KV)
SWA_PARAM_SHAPES = [(1, MIX), (1, SWA_KV), (1, SWA_KV), (1, SWA_Q_HEADS)]
SWA_STATE = (WINDOW, SWA_KV)


def _swa_specs(c, n=None):
    toks = [_tok_spec(c, w, _col(w, name, C1), n) for name, w in zip(("q", "k", "v"), SWA_TOK_WIDTHS)]
    toks += [_tok_spec(c, LANES, 0, n), _tok_spec(c, LANES, 0, n)]
    return toks, [_full_spec(s) for s in SWA_PARAM_SHAPES]


def _swa_fwd(proj1, cos, sin, bq, bk, bv, sinks):
    t = proj1.shape[0]
    c = SWA_STEP
    nb = t // c
    toks_s, params_s = _swa_specs(c)
    state_spec = pl.BlockSpec(SWA_STATE, lambda i: (i, 0))
    kv = SWA_KV_HEADS

    def body(q_ref, k_ref, v_ref, cos_ref, sin_ref, bq_ref, bk_ref, bv_ref, sk_ref, o_ref, kst_ref, vst_ref, k_scr, v_scr):
        first = pl.program_id(0) == 0

        @pl.when(first)
        def _():
            k_scr[...] = jnp.zeros_like(k_scr)
            v_scr[...] = jnp.zeros_like(v_scr)

        kst_ref[...] = k_scr[...]
        vst_ref[...] = v_scr[...]
        toks, params = _swa_load(q_ref, k_ref, v_ref, cos_ref, sin_ref, bq_ref, bk_ref, bv_ref, sk_ref)
        outs, (kn, vn) = swa_chunk((_heads(k_scr, kv, SWA_HD), _heads(v_scr, kv, SWA_HD)), toks, params, first)
        _put_heads(o_ref, outs, SWA_HD)
        _put_heads(k_scr, kn, SWA_HD)
        _put_heads(v_scr, vn, SWA_HD)

    saved = jax.ShapeDtypeStruct((nb * WINDOW, SWA_KV), F32)
    return pl.pallas_call(
        body, name="swa_fwd", grid=(nb,), in_specs=toks_s + params_s,
        out_specs=(_tok_spec(c, MIX, 0), state_spec, state_spec),
        out_shape=(jax.ShapeDtypeStruct((t, MIX), F32), saved, saved),
        scratch_shapes=[pltpu.VMEM(SWA_STATE, F32), pltpu.VMEM(SWA_STATE, F32)],
        compiler_params=_cparams(("arbitrary",)))(proj1, proj1, proj1, cos, sin, bq, bk, bv, sinks)


def _swa_bwd(proj1, cos, sin, bq, bk, bv, sinks, kst, vst, do):
    t = proj1.shape[0]
    c = SWA_STEP
    nb = t // c
    toks_s, params_s = _swa_specs(c, nb)
    state_spec = pl.BlockSpec(SWA_STATE, lambda i: (nb - 1 - i, 0))
    kv = SWA_KV_HEADS

    def body(q_ref, k_ref, v_ref, cos_ref, sin_ref, bq_ref, bk_ref, bv_ref, sk_ref, kst_ref, vst_ref, do_ref,
             dq_ref, dk_ref, dv_ref, dbq_ref, dbk_ref, dbv_ref, dsk_ref, dk_scr, dv_scr):
        i = pl.program_id(0)

        @pl.when(i == 0)
        def _():
            dk_scr[...] = jnp.zeros_like(dk_scr)
            dv_scr[...] = jnp.zeros_like(dv_scr)
            for ref in (dbq_ref, dbk_ref, dbv_ref, dsk_ref):
                ref[...] = jnp.zeros_like(ref)

        first = i == nb - 1
        toks, params = _swa_load(q_ref, k_ref, v_ref, cos_ref, sin_ref, bq_ref, bk_ref, bv_ref, sk_ref)
        f = functools.partial(swa_chunk, first=first)
        _, vjp = jax.vjp(f, (_heads(kst_ref, kv, SWA_HD), _heads(vst_ref, kv, SWA_HD)), toks, params)
        dstate_in = (_heads(dk_scr, kv, SWA_HD), _heads(dv_scr, kv, SWA_HD))
        (dkp, dvp), (dq, dk, dv, _, _), (dbq, dbk, dbv, dsk) = vjp((_heads(do_ref, SWA_Q_HEADS, SWA_HD), dstate_in))
        dq_ref[...], dk_ref[...], dv_ref[...] = dq.astype(BF16), dk.astype(BF16), dv.astype(BF16)
        dbq_ref[...] += dbq
        dbk_ref[...] += dbk
        dbv_ref[...] += dbv
        _put_heads(dsk_ref, dsk, 1, add=True)
        _put_heads(dk_scr, dkp, SWA_HD)
        _put_heads(dv_scr, dvp, SWA_HD)

    rev = lambda w: pl.BlockSpec((c, w), lambda i: (nb - 1 - i, 0))
    return pl.pallas_call(
        body, name="swa_bwd", grid=(nb,), in_specs=toks_s + params_s + [state_spec, state_spec, rev(MIX)],
        out_specs=[rev(w) for w in SWA_TOK_WIDTHS] + params_s,
        out_shape=[jax.ShapeDtypeStruct((t, w), BF16) for w in SWA_TOK_WIDTHS] + [jax.ShapeDtypeStruct(s, F32) for s in SWA_PARAM_SHAPES],
        scratch_shapes=[pltpu.VMEM(SWA_STATE, F32), pltpu.VMEM(SWA_STATE, F32)],
        compiler_params=_cparams(("arbitrary",)))(proj1, proj1, proj1, cos, sin, bq, bk, bv, sinks, kst, vst, do)


MESH = pl.DeviceIdType.MESH
ANY = pl.BlockSpec(memory_space=pl.ANY)


def _my_place():
    return lax.axis_index("x"), lax.axis_index("y"), lax.axis_index("c")


def _all_gather(shards):
    n = len(shards)

    def body(*refs):
        in_refs, out_refs = refs[:n], refs[n:2 * n]
        send_sems, recv_sems, local_sems = refs[2 * n:]
        x, y, c = _my_place()
        me, sibling = (x, y, c), (x, y, 1 - c)
        chips = [(1 - x, y), (x, 1 - y), (1 - x, 1 - y)]

        def slot(out_ref, place):
            px, py, pc = place
            return out_ref.at[4 * px + 2 * py + pc]

        def copy(a, k, block, to, src=None):
            return pltpu.make_async_remote_copy(
                src_ref=slot(out_refs[a], block) if src is None else src, dst_ref=slot(out_refs[a], block),
                send_sem=send_sems.at[a, k], recv_sem=recv_sems.at[a, k], device_id=to, device_id_type=MESH)

        mine = [pltpu.make_async_copy(in_refs[a], slot(out_refs[a], me), local_sems.at[a]) for a in range(n)]
        for cp in mine:
            cp.start()
        first = []
        for a in range(n):
            first.append(copy(a, 0, me, sibling, src=in_refs[a]))
            first += [copy(a, 1 + j, me, (*chip, c), src=in_refs[a]) for j, chip in enumerate(chips)]
        for cp in first:
            cp.start()
        passed = []
        for j, chip in enumerate(chips):
            for a in range(n):
                copy(a, 1 + j, (*chip, c), me).wait_recv()
                fwd = copy(a, 4 + j, (*chip, c), sibling)
                fwd.start()
                passed.append(fwd)
        for a in range(n):
            copy(a, 0, sibling, me).wait_recv()
            for j, chip in enumerate(chips):
                copy(a, 4 + j, (*chip, 1 - c), me).wait_recv()
        for cp in first + passed:
            cp.wait_send()
        for cp in mine:
            cp.wait()

    return pl.pallas_call(
        body, name="all_gather_weights", in_specs=[ANY] * n, out_specs=[ANY] * n,
        out_shape=[jax.ShapeDtypeStruct((N_DEV,) + s.shape, s.dtype) for s in shards],
        scratch_shapes=_comm_scratch(n))(*shards)


def _comm_copies(in_refs, out_refs, kinds, send_sems=None, recv_sems=None, local_sems=None):
    x, y, c = _my_place()
    my_idx = 4 * x + 2 * y + c
    src = lambda a, idx: in_refs[a] if kinds[a] == "gather" else in_refs[a].at[idx]
    local = [pltpu.make_async_copy(src(a, my_idx), out_refs[a].at[my_idx], local_sems.at[a]) for a in range(len(kinds))]
    remote = []
    for rel in range(1, N_DEV):
        px, py, pc = x ^ ((rel >> 2) & 1), y ^ ((rel >> 1) & 1), c ^ (rel & 1)
        for a in range(len(kinds)):
            remote.append(pltpu.make_async_remote_copy(
                src_ref=src(a, 4 * px + 2 * py + pc), dst_ref=out_refs[a].at[my_idx], send_sem=send_sems.at[a, rel - 1],
                recv_sem=recv_sems.at[a, rel - 1], device_id=(px, py, pc), device_id_type=MESH))
    return local, remote


def _comm_start(local, remote):
    for cp in local + remote:
        cp.start()


def _comm_wait(local, remote):
    for cp in remote:
        cp.wait_recv()
    for cp in remote:
        cp.wait_send()
    for cp in local:
        cp.wait()


def _comm_out_shapes(arrays, kinds):
    return [jax.ShapeDtypeStruct(((N_DEV,) + a.shape) if k == "gather" else a.shape, a.dtype) for a, k in zip(arrays, kinds)]


def _comm_scratch(n):
    return [] if n == 0 else [pltpu.SemaphoreType.DMA((n, N_DEV - 1)), pltpu.SemaphoreType.DMA((n, N_DEV - 1)), pltpu.SemaphoreType.DMA((n,))]


def _sequencer_scatter(name, parts, collective_id):
    src = jax.new_ref(parts, memory_space=pltpu.MemorySpace.HBM)
    dst = jax.empty_ref(jax.ShapeDtypeStruct(parts.shape, parts.dtype), memory_space=pltpu.MemorySpace.HBM)

    @pl.kernel(mesh=plsc.ScalarSubcoreMesh(axis_name="sequencer", num_cores=1), name=name,
               scratch_types=(pltpu.SemaphoreType.DMA((N_DEV - 1,)), pltpu.SemaphoreType.DMA((N_DEV - 1,))),
               compiler_params=pltpu.CompilerParams(collective_id=collective_id))
    def launch(send_sems, recv_sems):
        x, y, c = _my_place()
        my_idx = 4 * x + 2 * y + c
        peers = [(x ^ ((rel >> 2) & 1), y ^ ((rel >> 1) & 1), c ^ (rel & 1)) for rel in range(1, N_DEV)]
        barrier = pltpu.get_barrier_semaphore()
        for peer in peers:
            pl.semaphore_signal(barrier, inc=1, device_id=peer, device_id_type=MESH)
        pl.semaphore_wait(barrier, N_DEV - 1)
        copies = [pltpu.make_async_remote_copy(
            src_ref=src.at[4 * px + 2 * py + pc], dst_ref=dst.at[my_idx], send_sem=send_sems.at[k], recv_sem=recv_sems.at[k],
            device_id=(px, py, pc), device_id_type=MESH) for k, (px, py, pc) in enumerate(peers)]
        for cp in copies:
            cp.start()
        for cp in copies:
            cp.wait_recv()
        for cp in copies:
            cp.wait_send()

    launch()
    return dst[...]


def _sequencer_exchange(name, arrays, kinds, collective_id):
    n = len(arrays)
    srcs = [jax.new_ref(a, memory_space=pltpu.MemorySpace.HBM) for a in arrays]
    dsts = [jax.empty_ref(s, memory_space=pltpu.MemorySpace.HBM) for s in _comm_out_shapes(arrays, kinds)]

    @pl.kernel(mesh=plsc.ScalarSubcoreMesh(axis_name="sequencer", num_cores=1), name=name,
               scratch_types=(pltpu.SemaphoreType.DMA((n, N_DEV - 1)), pltpu.SemaphoreType.DMA((n, N_DEV - 1))),
               compiler_params=pltpu.CompilerParams(collective_id=collective_id))
    def launch(send_sems, recv_sems):
        x, y, c = _my_place()
        my_idx = 4 * x + 2 * y + c
        peers = [(x ^ ((rel >> 2) & 1), y ^ ((rel >> 1) & 1), c ^ (rel & 1)) for rel in range(1, N_DEV)]
        barrier = pltpu.get_barrier_semaphore()
        for peer in peers:
            pl.semaphore_signal(barrier, inc=1, device_id=peer, device_id_type=MESH)
        pl.semaphore_wait(barrier, N_DEV - 1)
        copies = [pltpu.make_async_remote_copy(
            src_ref=srcs[a] if kinds[a] == "gather" else srcs[a].at[4 * px + 2 * py + pc], dst_ref=dsts[a].at[my_idx],
            send_sem=send_sems.at[a, k], recv_sem=recv_sems.at[a, k], device_id=(px, py, pc), device_id_type=MESH)
            for k, (px, py, pc) in enumerate(peers) for a in range(n)]
        for cp in copies:
            cp.start()
        for cp in copies:
            cp.wait_recv()
        for cp in copies:
            cp.wait_send()

    launch()
    return [dst[...] for dst in dsts]


def _own_block(received, mine, kind):
    my_idx = 4 * lax.axis_index("x") + 2 * lax.axis_index("y") + lax.axis_index("c")
    block = mine[None] if kind == "gather" else lax.dynamic_slice(mine, (my_idx,) + (0,) * (mine.ndim - 1), (1,) + mine.shape[1:])
    return lax.dynamic_update_slice(received, block, (my_idx,) + (0,) * (received.ndim - 1))


def _exchange(arrays, kinds):
    n = len(arrays)

    def body(*refs):
        copies = _comm_copies(refs[:n], refs[n:2 * n], kinds, *refs[2 * n:])
        _comm_start(*copies)
        _comm_wait(*copies)

    return pl.pallas_call(body, name="exchange_grads", in_specs=[ANY] * n, out_specs=[ANY] * n,
                          out_shape=_comm_out_shapes(arrays, kinds), scratch_shapes=_comm_scratch(n))(*arrays)


def _adam_math(w, g, m, v):
    m = ADAM_B1 * m + (1.0 - ADAM_B1) * g
    v = ADAM_B2 * v + (1.0 - ADAM_B2) * (g * g)
    m_hat = m / (1.0 - ADAM_B1 ** ADAM_STEP)
    v_hat = v / (1.0 - ADAM_B2 ** ADAM_STEP)
    delta = -ADAM_LR * (m_hat / (jnp.sqrt(v_hat) + ADAM_EPS) + ADAM_WD * w)
    return delta, m, v


def _adamw(name, w, gslots, m, v, tc):
    r, cc = w.shape
    assert cc % tc == 0
    tile = pl.BlockSpec((r, tc), lambda i: (0, i))

    def body(w_ref, g_ref, m_ref, v_ref, go_ref, d_ref, mo_ref, vo_ref):
        g = g_ref[0].astype(F32)
        for s in range(1, N_DEV):
            g = g + g_ref[s].astype(F32)
        d, mn, vn = _adam_math(w_ref[...], g, m_ref[...], v_ref[...])
        go_ref[...] = g
        d_ref[...] = d
        mo_ref[...] = mn
        vo_ref[...] = vn

    shp = jax.ShapeDtypeStruct((r, cc), F32)
    return pl.pallas_call(body, name=name, grid=(cc // tc,),
                          in_specs=[tile, pl.BlockSpec((N_DEV, r, tc), lambda i: (0, 0, i)), tile, tile],
                          out_specs=(tile,) * 4, out_shape=(shp,) * 4, compiler_params=_cparams(("arbitrary",)))(w, gslots, m, v)


PACK_TILE = 8 * LANES


def _packed_rows(shape, mode):
    r, w = shape
    return -(-r // 8) * 8 if mode == "rows" else -(-(r * w) // PACK_TILE) * 8


def _pack_small(arrays, modes, lead=False):
    out = []
    for a, mode in zip(arrays, modes):
        a = a.astype(F32) if lead else a.astype(F32)[None]
        if mode == "rows":
            out.append(jnp.pad(a, ((0, 0), (0, (-a.shape[1]) % 8), (0, LANES - a.shape[2]))))
        else:
            flat = a.reshape(a.shape[0], -1)
            out.append(jnp.pad(flat, ((0, 0), (0, (-flat.shape[1]) % PACK_TILE))).reshape(a.shape[0], -1, LANES))
    out = jnp.concatenate(out, axis=1)
    return out if lead else out[0]


def _take_small(packed, row0, shape, mode):
    r, w = shape
    lead = packed.ndim == 3
    if mode == "rows":
        return packed[:, row0:row0 + r, :w] if lead else packed[row0:row0 + r, :w]
    per_row = -(-w // LANES)
    if lead:
        return packed[:, row0:row0 + r * per_row].reshape(packed.shape[0], r, per_row * LANES)[:, :, :w]
    rows = []
    for i in range(r):
        pieces = [packed[row0 + i * per_row + j:row0 + i * per_row + j + 1, :] for j in range(per_row)]
        rows.append((pieces[0] if per_row == 1 else jnp.concatenate(pieces, axis=1))[:, :w])
    return rows[0] if r == 1 else jnp.concatenate(rows, axis=0)


def _adamw_small(slots, specs, ws, ms, vs, loss_row):
    n = len(specs)

    def body(*refs):
        slots_ref, w_refs, m_refs, v_refs = refs[0], refs[1:1 + n], refs[1 + n:1 + 2 * n], refs[1 + 2 * n:1 + 3 * n]
        out_refs, loss_ref = refs[1 + 3 * n:1 + 7 * n], refs[1 + 7 * n]
        gp = slots_ref[0]
        for s in range(1, N_DEV):
            gp = gp + slots_ref[s]
        read = lambda ref: ref[0] if len(ref.shape) == 3 else ref[...]
        for k, (shape, mode, row0) in enumerate(specs):
            g = _take_small(gp, row0, shape, mode)
            d, mn, vn = _adam_math(read(w_refs[k]), g, read(m_refs[k]), read(v_refs[k]))
            for ref, val in zip(out_refs[4 * k:4 * k + 4], (g, d, mn, vn)):
                if len(ref.shape) == 3:
                    ref[0] = val
                else:
                    ref[...] = val
        loss_ref[...] = gp[loss_row:loss_row + 1, :]

    vmem = pl.BlockSpec(memory_space=pltpu.VMEM)
    out_shape = [jax.ShapeDtypeStruct(w.shape, F32) for w in ws for _ in range(4)] + [jax.ShapeDtypeStruct((1, LANES), F32)]
    outs = pl.pallas_call(body, name="adamw_small", in_specs=[vmem] * (1 + 3 * n), out_specs=[vmem] * (4 * n + 1),
                          out_shape=out_shape)(slots, *ws, *ms, *vs)
    return [outs[4 * k:4 * k + 4] for k in range(n)], outs[4 * n]


def _rope_tables(t):
    dim = jnp.arange(LANES) % SWA_HD
    inv_freq = ROPE_THETA ** (-(dim % ROPE_HALF).astype(F32) / ROPE_HALF)
    ang = jnp.arange(t, dtype=F32)[:, None] * jnp.where(dim < 2 * ROPE_HALF, inv_freq, 0.0)[None, :]
    return jnp.cos(ang), jnp.sin(ang)


def _pad_to(a, rows=None, cols=None):
    r = 0 if rows is None else rows - a.shape[0]
    c = 0 if cols is None else cols - a.shape[1]
    return jnp.pad(a, ((0, r), (0, c)))


ORIG0 = dict(gq=(0, 256), gk=(256, 256), gv=(512, 512), glow=(1024, 16), r=(1040, 512), k=(1552, 512), v=(2064, 512),
             xw=(2576, 64), xa=(2640, 64), gate=(2704, 1024))
ORIG0_ORDER = ["gq", "gk", "gv", "glow", "r", "k", "v", "xw", "xa", "gate"]


def _w0t_to_padded(wt):
    rows, at = [], 0
    for name, (off, width) in sorted(C0.items(), key=lambda kv: kv[1][0]):
        assert off == at
        src, src_w = ORIG0[name]
        rows.append(_pad_to(wt[src:src + src_w], rows=width))
        at += width
    rows.append(jnp.zeros((N0P - at, wt.shape[1]), wt.dtype))
    return jnp.concatenate(rows, axis=0)


def _w0t_from_padded(wpt):
    return jnp.concatenate([wpt[C0[n][0]:C0[n][0] + ORIG0[n][1]] for n in ORIG0_ORDER], axis=0)


def _w1t_to_mine(wt):
    return jnp.concatenate([wt[1536:2560], wt[:1536]], axis=0)


def _w1t_from_mine(wt):
    return jnp.concatenate([wt[1024:2560], wt[:1024]], axis=0)


def kernel(x, norm_w, w_in0, gla_gk_up, gla_gk_bias, gla_norm_w, rwkv_mu, rwkv_w0, rwkv_w_up, rwkv_a0, rwkv_a_up, rwkv_k_k, rwkv_k_a, rwkv_r_k, rwkv_ln_w, rwkv_ln_b, w_out0, w_in1, b_in1, attn_sinks, w_out1, b_out1, final_norm_w, loss_target, m_norm_w, m_w_in0, m_gla_gk_up, m_gla_gk_bias, m_gla_norm_w, m_rwkv_mu, m_rwkv_w0, m_rwkv_w_up, m_rwkv_a0, m_rwkv_a_up, m_rwkv_k_k, m_rwkv_k_a, m_rwkv_r_k, m_rwkv_ln_w, m_rwkv_ln_b, m_w_out0, m_w_in1, m_b_in1, m_attn_sinks, m_w_out1, m_b_out1, m_final_norm_w, v_norm_w, v_w_in0, v_gla_gk_up, v_gla_gk_bias, v_gla_norm_w, v_rwkv_mu, v_rwkv_w0, v_rwkv_w_up, v_rwkv_a0, v_rwkv_a_up, v_rwkv_k_k, v_rwkv_k_a, v_rwkv_r_k, v_rwkv_ln_w, v_rwkv_ln_b, v_w_out0, v_w_in1, v_b_in1, v_attn_sinks, v_w_out1, v_b_out1, v_final_norm_w):
    weights = dict(norm_w=norm_w, w_in0=w_in0, gla_gk_up=gla_gk_up, gla_gk_bias=gla_gk_bias, gla_norm_w=gla_norm_w, rwkv_mu=rwkv_mu,
                   rwkv_w0=rwkv_w0, rwkv_w_up=rwkv_w_up, rwkv_a0=rwkv_a0, rwkv_a_up=rwkv_a_up, rwkv_k_k=rwkv_k_k, rwkv_k_a=rwkv_k_a,
                   rwkv_r_k=rwkv_r_k, rwkv_ln_w=rwkv_ln_w, rwkv_ln_b=rwkv_ln_b, w_out0=w_out0, w_in1=w_in1, b_in1=b_in1,
                   attn_sinks=attn_sinks, w_out1=w_out1, b_out1=b_out1, final_norm_w=final_norm_w)
    moms = dict(norm_w=m_norm_w, w_in0=m_w_in0, gla_gk_up=m_gla_gk_up, gla_gk_bias=m_gla_gk_bias, gla_norm_w=m_gla_norm_w,
                rwkv_mu=m_rwkv_mu, rwkv_w0=m_rwkv_w0, rwkv_w_up=m_rwkv_w_up, rwkv_a0=m_rwkv_a0, rwkv_a_up=m_rwkv_a_up,
                rwkv_k_k=m_rwkv_k_k, rwkv_k_a=m_rwkv_k_a, rwkv_r_k=m_rwkv_r_k, rwkv_ln_w=m_rwkv_ln_w, rwkv_ln_b=m_rwkv_ln_b,
                w_out0=m_w_out0, w_in1=m_w_in1, b_in1=m_b_in1, attn_sinks=m_attn_sinks, w_out1=m_w_out1, b_out1=m_b_out1,
                final_norm_w=m_final_norm_w)
    vars_ = dict(norm_w=v_norm_w, w_in0=v_w_in0, gla_gk_up=v_gla_gk_up, gla_gk_bias=v_gla_gk_bias, gla_norm_w=v_gla_norm_w,
                 rwkv_mu=v_rwkv_mu, rwkv_w0=v_rwkv_w0, rwkv_w_up=v_rwkv_w_up, rwkv_a0=v_rwkv_a0, rwkv_a_up=v_rwkv_a_up,
                 rwkv_k_k=v_rwkv_k_k, rwkv_k_a=v_rwkv_k_a, rwkv_r_k=v_rwkv_r_k, rwkv_ln_w=v_rwkv_ln_w, rwkv_ln_b=v_rwkv_ln_b,
                 w_out0=v_w_out0, w_in1=v_w_in1, b_in1=v_b_in1, attn_sinks=v_attn_sinks, w_out1=v_w_out1, b_out1=v_b_out1,
                 final_norm_w=v_final_norm_w)
    names = list(weights)
    big = ["w_in0", "w_out0", "w_in1", "w_out1"]
    small_sharded = ["gla_gk_up", "rwkv_w_up", "rwkv_a_up", "b_in1", "b_out1"]
    replicated = [n for n in names if n not in big and n not in small_sharded]

    xs = x[0]
    tgt = loss_target[0]
    t = xs.shape[0]

    def view(w):
        shape = tuple(w.shape[-2:]) if w.ndim >= 2 else (1, w.shape[0])
        return shape, ("rows" if shape[0] > 1 and shape[1] <= LANES else "flat")

    def layout(ns, row0=0):
        specs = []
        for n in ns:
            shape, mode = view(weights[n])
            specs.append((shape, mode, row0))
            row0 += _packed_rows(shape, mode)
        return specs, row0

    sh_specs, n_shard_rows = layout(small_sharded)
    rep_specs, loss_row = layout(replicated, n_shard_rows)
    sh_modes, rep_modes = [s[1] for s in sh_specs], [s[1] for s in rep_specs]

    small_shard_pack = _pack_small([weights[n].reshape(view(weights[n])[0]) for n in small_sharded], sh_modes)
    g_in0, g_small = _all_gather([w_in0[0].T.astype(BF16), small_shard_pack])
    w0t = _w0t_to_padded(g_in0.reshape(-1, D_MODEL))
    later_shards = [w_out0[0].astype(BF16), w_in1[0].T.astype(BF16), w_out1[0].astype(BF16)]
    gs = [_take_small(g_small, row0, shape, mode) for shape, mode, row0 in sh_specs]
    join_cols = lambda a: jnp.transpose(a, (1, 0, 2)).reshape(a.shape[1], -1)
    gk_up, w_up, a_up = join_cols(gs[0]), join_cols(gs[1]), join_cols(gs[2])
    b_in, b_out = gs[3].reshape(1, -1), gs[4].reshape(1, -1)

    gk_up_p = _pad_to(gk_up, rows=LOW)
    w3, rank = 3 * RWKV_W, rwkv_w_up.shape[1]
    mu = rwkv_mu
    rwkv_params = [mu[:, 0:RWKV_W], mu[:, RWKV_W:2 * RWKV_W], mu[:, 2 * RWKV_W:w3], _pad_to(mu[:, w3:w3 + rank], cols=LOW),
                   _pad_to(mu[:, w3 + rank:], cols=LOW), rwkv_w0, _pad_to(w_up, rows=LOW), rwkv_a0, _pad_to(a_up, rows=LOW),
                   rwkv_k_k, rwkv_k_a, rwkv_r_k.reshape(1, RWKV_W), rwkv_ln_w, rwkv_ln_b]
    bq, bk, bv = b_in[:, :MIX], b_in[:, MIX:MIX + SWA_KV], b_in[:, MIX + SWA_KV:]
    cos, sin = _rope_tables(t)
    nw0, nw1, fw = norm_w[0:1], norm_w[1:2], final_norm_w.reshape(1, D_MODEL)

    d = D_MODEL
    wide = lambda arr: (arr, d, 0)
    silu = lambda g: g * sigmoid(g)
    hn0, proj0 = _matmul_fused("norm0_proj0", rms, w0t, "nt", [wide(xs)], [nw0], [(N0P, F32)], [], lambda acc, x, w: (acc,))
    o_a, gla_states = _gla_fwd(proj0, gk_up_p, gla_gk_bias, gla_norm_w)
    got = _sequencer_exchange("gather_later_weights", later_shards, ["gather"] * 3, 1)
    o_b, rwkv_states, rwkv_prevs, _ = _rwkv_fwd(proj0, rwkv_params, [], [])
    g_out0, g_in1, g_out1 = [_own_block(r, mine, "gather") for r, mine in zip(got, later_shards)]
    wo0 = g_out0.reshape(MIX, D_MODEL)
    w1t = _w1t_to_mine(g_in1.reshape(-1, D_MODEL))
    wo1 = g_out1.reshape(MIX, D_MODEL)
    og0, h1, hn1 = _matmul_fused(
        "gate0_out0_norm1", lambda oa, ob, gate, x, w: jnp.concatenate([oa, ob], axis=1) * silu(gate), wo0, "nn",
        [(o_a, GLA_VAL, 0), (o_b, RWKV_W, 0), wide(proj0), wide(xs)], [nw1], [(d, F32), (d, BF16)], [],
        lambda acc, oa, ob, gate, x, w: _resid_norm(acc, x, w))
    proj1 = _matmul("proj1", hn1, w1t, "nt", PROJ_ROWS, N1P // 2)
    o_c, kst, vst = _swa_fwd(proj1, cos, sin, bq, bk, bv, attn_sinks)
    og1, dh2, loss_part, d_b_out, d_fw = _matmul_fused(
        "gate1_out1_loss", lambda oc, gate, h, tg, b, w: oc * silu(gate), wo1, "nn",
        [wide(o_c), wide(proj1), wide(h1), wide(tgt)], [b_out, fw], [(d, F32)], [LANES, d, d],
        lambda acc, oc, gate, h, tg, b, w: _loss_head(acc, h, tg, b, w))

    d_oc, d_gate1 = _matmul_fused("out1_dx_gate1", dh2, wo1, "nt", [wide(o_c), wide(proj1)], [], [(d, F32), (d, BF16)], [], _gate_back)
    d_wo1 = _matmul("out1_dw", og1, dh2, "tn", DW_COLS, DW_COLS, BF16)
    dq, dk, dv, d_bq, d_bk, d_bv, d_sinks = _swa_bwd(proj1, cos, sin, bq, bk, bv, attn_sinks, kst, vst, d_oc)
    dproj1 = jnp.concatenate([d_gate1, dq, dk, dv], axis=1)
    dh1, d_nw1 = _matmul_fused("proj1_dx_norm1", dproj1, w1t, "nn", [wide(h1), wide(dh2)], [nw1], [(d, F32)], [d], _norm_back)
    d_w1t = _matmul("proj1_dw", dproj1, hn1, "tn", DW_COLS, d, BF16)
    d_oa, d_ob, d_gate0 = _matmul_fused("out0_dx_gate0", dh1, wo0, "nt", [(o_a, GLA_VAL, 0), (o_b, RWKV_W, 0), wide(proj0)], [],
                                        [(GLA_VAL, F32), (RWKV_W, F32), (d, BF16)], [], _gate_back)
    d_wo0 = _matmul("out0_dw", og0, dh1, "tn", DW_COLS, DW_COLS, BF16)
    dgq, dgk, dgv, dglow, d_gk_up, d_gk_bias, d_gla_nw = _gla_bwd(proj0, gk_up_p, gla_gk_bias, gla_norm_w, gla_states, d_oa)
    row_blocks = lambda a: a.astype(BF16).reshape(N_DEV, -1, D_MODEL)
    early = [row_blocks(_w1t_from_mine(d_w1t)), row_blocks(d_wo1), row_blocks(d_wo0)]
    got = _sequencer_exchange("exchange_early_grads", early, ["scatter"] * 3, 2)
    (dr, dkk, dvv, dxw, dxa), d_rp, _ = _rwkv_bwd(proj0, rwkv_params, rwkv_states, rwkv_prevs, d_ob, [], [])
    r_in1, r_out1, r_out0 = [_own_block(r, mine, "scatter") for r, mine in zip(got, early)]
    pad = jnp.zeros((t, N0P - C0["xa"][0] - C0["xa"][1]), BF16)
    dproj0 = jnp.concatenate([d_gate0, dgv, dr, dkk, dvv, dgq, dgk, dglow, dxw, dxa, pad], axis=1)
    d_w0 = row_blocks(_w0t_from_padded(_matmul("proj0_dw", dproj0, hn0, "tn", DW_COLS, d, BF16)))
    r_in0 = _sequencer_scatter("exchange_w_in0_grad", d_w0, 0)
    res = {}
    res["w_out0"] = tuple(a[None] for a in _adamw("adamw_w_out0", w_out0[0], r_out0, m_w_out0[0], v_w_out0[0], ADAM_COLS))
    res["w_in1"] = tuple(a.T[None] for a in _adamw("adamw_w_in1", w_in1[0].T, r_in1, m_w_in1[0].T, v_w_in1[0].T, ADAM_COLS))
    res["w_out1"] = tuple(a[None] for a in _adamw("adamw_w_out1", w_out1[0], r_out1, m_w_out1[0], v_w_out1[0], ADAM_COLS))
    grad_x, d_nw0 = _matmul_fused("proj0_dx_norm0", dproj0, w0t, "nn", [wide(xs), wide(dh1)], [nw0], [(d, F32)], [d], _norm_back)

    contrib = dict(
        norm_w=jnp.concatenate([d_nw0, d_nw1], axis=0), gla_gk_bias=d_gk_bias, gla_norm_w=d_gla_nw,
        rwkv_mu=jnp.concatenate([d_rp[0], d_rp[1], d_rp[2], d_rp[3][:, :rank], d_rp[4][:, :rank]], axis=1),
        rwkv_w0=d_rp[5], rwkv_a0=d_rp[7], rwkv_k_k=d_rp[9], rwkv_k_a=d_rp[10], rwkv_r_k=d_rp[11].reshape(RWKV_HEADS, RWKV_N),
        rwkv_ln_w=d_rp[12], rwkv_ln_b=d_rp[13], attn_sinks=d_sinks, final_norm_w=d_fw)
    rep_pack = _pack_small([contrib[n] for n in replicated] + [loss_part[:, :1]], rep_modes + ["flat"])

    d_b_in = jnp.concatenate([d_bq, d_bk, d_bv], axis=1)
    full_small = [d_gk_up[:gk_up.shape[0]], d_rp[6][:rank], d_rp[8][:rank], d_b_in, d_b_out]
    split_cols = lambda a: jnp.transpose(a.reshape(a.shape[0], N_DEV, -1), (1, 0, 2))
    small_parts = [split_cols(a) for a in full_small]
    small_pack = _pack_small(small_parts, sh_modes, lead=True)
    r_small, r_rep = _exchange([small_pack, rep_pack], ["scatter", "gather"])

    small_names = small_sharded + replicated
    slots = jnp.concatenate([r_small, r_rep], axis=1)
    as_2d = lambda a: a.reshape(1, -1) if a.ndim == 1 else a
    small_res, loss_row_out = _adamw_small(slots, sh_specs + rep_specs, [as_2d(weights[n]) for n in small_names],
                                           [as_2d(moms[n]) for n in small_names], [as_2d(vars_[n]) for n in small_names], loss_row)
    for n, vals in zip(small_names, small_res):
        res[n] = tuple(val.reshape(weights[n].shape) for val in vals)
    loss = loss_row_out[0, 0]
    my_idx = 4 * lax.axis_index("x") + 2 * lax.axis_index("y") + lax.axis_index("c")
    r_in0 = lax.dynamic_update_slice(r_in0, lax.dynamic_slice(d_w0, (my_idx, 0, 0), (1,) + d_w0.shape[1:]), (my_idx, 0, 0))
    res["w_in0"] = tuple(a.T[None] for a in _adamw("adamw_w_in0", w_in0[0].T, r_in0, m_w_in0[0].T, v_w_in0[0].T, ADAM_COLS))
    return (loss, grad_x[None], *[res[n][0] for n in names], *[res[n][1] for n in names],
            *[res[n][2] for n in names], *[res[n][3] for n in names])
```

```python
import functools

import jax
import jax.numpy as jnp
from jax import lax
from jax.experimental import pallas as pl
from jax.experimental.pallas import tpu as pltpu
from jax.experimental.pallas import tpu_sc as plsc

F32 = jnp.float32
BF16 = jnp.bfloat16
HI = lax.Precision.HIGHEST

D_MODEL = 1024
NORM_EPS = 1e-5
GLA_HEADS, GLA_DK, GLA_DV = 4, 64, 128
GLA_NORMALIZER = 16.0
GLA_CHUNK = 64
GLA_STEP = 1024
RWKV_HEADS, RWKV_N = 8, 64
RWKV_LN_EPS = 64e-5
RWKV_CHUNK = 128
SWA_Q_HEADS, SWA_KV_HEADS, SWA_GROUP, SWA_HD = 16, 4, 4, 64
WINDOW = 128
SWA_STEP = 512
ROPE_THETA = 500000.0
NEG = -1e30
N_DEV = 8
LANES = 128

ADAM_LR, ADAM_B1, ADAM_B2, ADAM_EPS, ADAM_WD, ADAM_STEP = 0.001, 0.9, 0.999, 1e-08, 0.01, 10

GLA_KEY, GLA_VAL = GLA_HEADS * GLA_DK, GLA_HEADS * GLA_DV
RWKV_W = RWKV_HEADS * RWKV_N
SWA_KV = SWA_KV_HEADS * SWA_HD
MIX = GLA_VAL + RWKV_W
LOW = LANES

N0P = 4096
C0 = dict(gate=(0, MIX), gv=(1024, GLA_VAL), r=(1536, RWKV_W), k=(2048, RWKV_W), v=(2560, RWKV_W), gq=(3072, GLA_KEY),
          gk=(3328, GLA_KEY), glow=(3584, LOW), xw=(3712, LOW), xa=(3840, LOW))
N1P = 2560
C1 = dict(gate=(0, MIX), q=(1024, MIX), k=(2048, SWA_KV), v=(2304, SWA_KV))

VMEM_LIMIT = 56 * 1024 * 1024

P_LORA = 1
P_GLA = 1
P_RWKV_G = 2
P_RWKV = 1
P_SWA = 1


def _cparams(sem=None):
    return pltpu.CompilerParams(dimension_semantics=sem, vmem_limit_bytes=VMEM_LIMIT)


DIMS = dict(nn=(((1,), (0,)), ((), ())), nt=(((1,), (1,)), ((), ())), tn=(((0,), (0,)), ((), ())))


def _split_bf16(a):
    hi = a.astype(BF16)
    return hi, (a - hi.astype(F32)).astype(BF16)


def _dot(a, b, mode, passes):
    dg = lambda p, q: lax.dot_general(p, q, DIMS[mode], preferred_element_type=F32)
    if passes == 1:
        return dg(a.astype(BF16), b.astype(BF16))
    if passes == 2:
        ah, (bh, bl) = a.astype(BF16), _split_bf16(b)
        return dg(ah, bh) + dg(ah, bl)
    if passes == 3:
        (ah, al), (bh, bl) = _split_bf16(a), _split_bf16(b)
        return dg(ah, bh) + dg(al, bh) + dg(ah, bl)
    return lax.dot_general(a, b, DIMS[mode], precision=HI, preferred_element_type=F32)


@functools.partial(jax.custom_vjp, nondiff_argnums=(2, 3))
def mmx(a, b, mode, passes):
    return _dot(a, b, mode, passes)


def _mmx_fwd(a, b, mode, passes):
    return _dot(a, b, mode, passes), (a, b)


def _mmx_bwd(mode, passes, res, g):
    a, b = res
    if mode == "nn":
        return _dot(g, b, "nt", passes), _dot(a, g, "tn", passes)
    if mode == "nt":
        return _dot(g, b, "nn", passes), _dot(g, a, "tn", passes)
    return _dot(b, g, "nt", passes), _dot(a, g, "nn", passes)


mmx.defvjp(_mmx_fwd, _mmx_bwd)


def _tri_dot(tri, x):
    t = tri.astype(BF16)
    x1 = x.astype(BF16)
    r1 = x - x1.astype(F32)
    x2 = r1.astype(BF16)
    x3 = (r1 - x2.astype(F32)).astype(BF16)
    dg = lambda q: jnp.dot(t, q, preferred_element_type=F32)
    return dg(x1) + dg(x2) + dg(x3)


@jax.custom_vjp
def cumsum_rows(x):
    return _tri_dot(tril_ones(x.shape[0]), x)


def _cumsum_fwd(x):
    return cumsum_rows(x), None


def _cumsum_bwd(_, g):
    i, j = _iota2(g.shape[0], g.shape[0])
    return (_tri_dot(jnp.where(i <= j, 1.0, 0.0).astype(F32), g),)


cumsum_rows.defvjp(_cumsum_fwd, _cumsum_bwd)


def _head_dot(x):
    i, j = _iota2(LANES, LANES)
    shift = RWKV_N.bit_length() - 1
    same = jnp.where(jnp.right_shift(i, shift) == jnp.right_shift(j, shift), 1.0, 0.0).astype(F32)
    return jnp.concatenate([_ones_right(x[:, g * LANES:(g + 1) * LANES], same) for g in range(x.shape[1] // LANES)], axis=1)


def _ones_right(x, ones):
    t = ones.astype(BF16)
    x1 = x.astype(BF16)
    x2 = (x - x1.astype(F32)).astype(BF16)
    dg = lambda q: jnp.dot(q, t, preferred_element_type=F32)
    return dg(x1) + dg(x2)


@jax.custom_vjp
def head_sum(x):
    return _head_dot(x)


def _head_sum_fwd(x):
    return head_sum(x), None


def _head_sum_bwd(_, g):
    return (_head_dot(g),)


head_sum.defvjp(_head_sum_fwd, _head_sum_bwd)


def cat_rows(*xs):
    return jnp.concatenate(xs, axis=0)


def _iota2(n, m):
    return lax.broadcasted_iota(jnp.int32, (n, m), 0), lax.broadcasted_iota(jnp.int32, (n, m), 1)


def tril_ones(c, strict=False):
    i, j = _iota2(c, c)
    return jnp.where((i > j) if strict else (i >= j), 1.0, 0.0).astype(F32)


def row_of(x, r):
    i = lax.broadcasted_iota(jnp.int32, x.shape, 0)
    return jnp.sum(jnp.where(i == r, x, 0.0), axis=0, keepdims=True)


@jax.custom_vjp
def shift_rows(x, prev):
    r = lax.broadcasted_iota(jnp.int32, x.shape, 0)
    return jnp.where(r == 0, prev, pltpu.roll(x, 1, 0))


def _shift_fwd(x, prev):
    return shift_rows(x, prev), None


def _shift_bwd(_, g):
    c = g.shape[0]
    r = lax.broadcasted_iota(jnp.int32, g.shape, 0)
    return jnp.where(r == c - 1, 0.0, pltpu.roll(g, c - 1, 0)), row_of(g, 0)


shift_rows.defvjp(_shift_fwd, _shift_bwd)


def log_sigmoid(x):
    return jnp.minimum(x, 0.0) - jnp.log(1.0 + jnp.exp(-jnp.abs(x)))


def softplus(x):
    return jnp.maximum(x, 0.0) + jnp.log(1.0 + jnp.exp(-jnp.abs(x)))


def sigmoid(x):
    return 1.0 / (1.0 + jnp.exp(-x))


def rms(x, w, eps=NORM_EPS):
    return x * lax.rsqrt(jnp.mean(x * x, axis=-1, keepdims=True) + eps) * w


def gla_chunk(state, toks, params):
    q, k, v, glow = toks
    gk_up, bias, norm_w = params
    c = GLA_CHUNK
    subs, heads = range(glow.shape[0] // c), range(GLA_HEADS)
    rows = lambda x, j: x[j * c:(j + 1) * c]
    hk = lambda x, h: x[:, h * GLA_DK:(h + 1) * GLA_DK]
    hv = lambda x, h: x[:, h * GLA_DV:(h + 1) * GLA_DV]
    ltri = tril_ones(c)
    g = log_sigmoid(mmx(glow, gk_up, "nn", P_LORA) + bias) / GLA_NORMALIZER
    b = [cumsum_rows(rows(g, j)) for j in subs]
    ref = [lax.stop_gradient(row_of(b[j], c // 2)) for j in subs]
    last = [row_of(b[j], c - 1) for j in subs]
    ql = [rows(q, j) * (GLA_DK ** -0.5) * jnp.exp(b[j] - ref[j]) for j in subs]
    kr = [rows(k, j) * jnp.exp(ref[j] - b[j]) for j in subs]
    kl = [rows(k, j) * jnp.exp(last[j] - b[j]) for j in subs]
    vj = [rows(v, j) for j in subs]
    e_ref, e_last = [jnp.exp(x) for x in ref], [jnp.exp(x) for x in last]
    att = [[mmx(hk(ql[j], h), hk(kr[j], h), "nt", P_GLA) * ltri for h in heads] for j in subs]
    o_in = [[mmx(att[j][h], hv(vj[j], h), "nn", P_GLA) for h in heads] for j in subs]
    kv = [[mmx(hv(vj[j], h), hk(kl[j], h), "tn", P_GLA) for h in heads] for j in subs]
    o = []
    for j in subs:
        o.append([o_in[j][h] + mmx(hk(ql[j], h), state[h] * hk(e_ref[j], h), "nt", P_GLA) for h in heads])
        state = [state[h] * hk(e_last[j], h) + kv[j][h] for h in heads]
    o = [[x * lax.rsqrt(jnp.mean(x * x, axis=-1, keepdims=True) + NORM_EPS) * norm_w for x in oj] for oj in o]
    return cat_rows(*[jnp.concatenate(oj, axis=1) for oj in o]), state


SOLVE_BLOCK = 128


def solve_unit_lower(ps, ws):
    n = ps[0].shape[0]
    heads = range(len(ps))
    if n > SOLVE_BLOCK:
        half = n // 2
        top = solve_unit_lower([p[:half, :half] for p in ps], [w[:half] for w in ws])
        rest = [ws[h][half:] + mmx(ps[h][half:, :half], top[h], "nn", P_RWKV) for h in heads]
        bottom = solve_unit_lower([p[half:, half:] for p in ps], rest)
        return [cat_rows(top[h], bottom[h]) for h in heads]
    u, p = ws, ps
    levels = max(1, (n - 1).bit_length())
    for it in range(levels):
        if it + 1 < levels:
            y = [mmx(p[h], jnp.concatenate([p[h], u[h]], axis=1), "nn", P_RWKV) for h in heads]
            u = [u[h] + y[h][:, n:] for h in heads]
            p = [y[h][:, :n] for h in heads]
        else:
            u = [u[h] + mmx(p[h], u[h], "nn", P_RWKV) for h in heads]
    return u


def rwkv_chunk(state, toks, params):
    S, pr, pk, pv, pxw, pxa = state
    r_, k_, v_, xw_, xa_ = toks
    mu_r, mu_k, mu_v, mu_xw, mu_xa, w0, w_up, a0, a_up, k_k, k_a, r_k, ln_w, ln_b = params
    c, n = xw_.shape[0], RWKV_N
    heads = range(RWKV_HEADS)
    hs = lambda x, h: x[:, h * n:(h + 1) * n]
    ltri = tril_ones(c)
    stri = tril_ones(c, strict=True)

    def lerp(x, prev, mu):
        return x + (shift_rows(x, prev) - x) * mu

    xw = jnp.tanh(lerp(xw_, pxw, mu_xw))
    xa = lerp(xa_, pxa, mu_xa)
    r = lerp(r_, pr, mu_r)
    k = lerp(k_, pk, mu_k)
    v = lerp(v_, pv, mu_v)
    w = -softplus(-(w0 + mmx(xw, w_up, "nn", P_LORA))) - 0.5
    lw = -jnp.exp(w)
    asig = sigmoid(a0 + mmx(xa, a_up, "nn", P_LORA))
    kk = k * k_k
    kk = kk * lax.rsqrt(jnp.maximum(head_sum(kk * kk), 1e-24))
    k2 = k * (1.0 + (asig - 1.0) * k_a)
    b = kk * asig
    cum = cumsum_rows(lw)
    ref = lax.stop_gradient(row_of(cum, c // 2))
    last = row_of(cum, c - 1)
    at = -kk * jnp.exp(cum - lw - ref)
    rt = r * jnp.exp(cum - ref)
    e_out = jnp.exp(ref - cum)
    bt, kt = b * e_out, k2 * e_out
    e_tail = jnp.exp(last - cum)
    bl, kl = b * e_tail, k2 * e_tail
    e_ref, e_last = jnp.exp(ref), jnp.exp(last)
    g = [mmx(cat_rows(hs(at, h), hs(rt, h)), cat_rows(hs(bt, h), hs(kt, h), S[h] * hs(e_ref, h)), "nt", P_RWKV_G) for h in heads]
    aab = [x[:c, :c] * stri for x in g]
    aak = [x[:c, c:2 * c] * stri for x in g]
    arb = [x[c:, :c] * ltri for x in g]
    ark = [x[c:, c:2 * c] * ltri for x in g]
    av = [mmx(cat_rows(aak[h], ark[h]), hs(v, h), "nn", P_RWKV) for h in heads]
    u = solve_unit_lower(aab, [g[h][:c, 2 * c:] + av[h][:c] for h in heads])
    o = [g[h][c:, 2 * c:] + av[h][c:] + mmx(arb[h], u[h], "nn", P_RWKV) for h in heads]
    s1 = [S[h] * hs(e_last, h) + mmx(cat_rows(u[h], hs(v, h)), cat_rows(hs(bl, h), hs(kl, h)), "tn", P_RWKV) for h in heads]
    o = jnp.concatenate(o, axis=1)
    d = o - head_sum(o) * (1.0 / n)
    var = head_sum(d * d) * (1.0 / n)
    o = d * lax.rsqrt(var + RWKV_LN_EPS) * ln_w + ln_b + head_sum(r * k2 * r_k) * v
    new_state = (s1, row_of(r_, c - 1), row_of(k_, c - 1), row_of(v_, c - 1), row_of(xw_, c - 1), row_of(xa_, c - 1))
    return o, new_state


RWKV_STEP = 256


def rwkv_chunks(state, toks, params):
    outs = []
    for j in range(toks[3].shape[0] // RWKV_CHUNK):
        rows = slice(j * RWKV_CHUNK, (j + 1) * RWKV_CHUNK)
        o, state = rwkv_chunk(state, tuple(t[rows] for t in toks), params)
        outs.append(o)
    return cat_rows(*outs), state


ROPE_HALF = 8


def _rot_half_raw(x):
    lane = lax.broadcasted_iota(jnp.int32, (x.shape[0], LANES), 1) & (SWA_HD - 1)
    out = []
    for i in range(x.shape[1] // LANES):
        g = x[:, i * LANES:(i + 1) * LANES]
        up, down = pltpu.roll(g, LANES - ROPE_HALF, 1), pltpu.roll(g, ROPE_HALF, 1)
        out.append(jnp.where(lane < ROPE_HALF, -up, jnp.where(lane < 2 * ROPE_HALF, down, 0.0)))
    return out[0] if len(out) == 1 else jnp.concatenate(out, axis=1)


@jax.custom_vjp
def rot_half(x):
    return _rot_half_raw(x)


rot_half.defvjp(lambda x: (_rot_half_raw(x), None), lambda _, g: (-_rot_half_raw(g),))


def rope(x, cos2, sin2):
    reps = x.shape[1] // LANES
    tile = lambda t: t if reps == 1 else jnp.concatenate([t] * reps, axis=1)
    return x * tile(cos2) + rot_half(x) * tile(sin2)


def swa_chunk(state, toks, params, first):
    kprev, vprev = state
    q_, k_, v_, cos, sin = toks
    bq, bk, bv, sinks = params
    c, ng = WINDOW, SWA_GROUP
    n_sub = cos.shape[0] // c
    units = [(j, g) for j in range(n_sub) for g in range(SWA_KV_HEADS)]
    rows = lambda x, j: x[j * c:(j + 1) * c]
    hs = lambda g: range(g * ng, (g + 1) * ng)
    head = lambda x, h: x[:, h * SWA_HD:(h + 1) * SWA_HD]
    qi, kj = _iota2(ng * c, 2 * c)
    qpos = qi & (c - 1)
    cur_ok = (kj >= c) & (qpos >= kj - c)
    prev_ok = (kj < c) & (kj > qpos)
    ok = [cur_ok | (prev_ok & jnp.logical_not(first))] + [cur_ok | prev_ok] * (n_sub - 1)
    q_all = rope(q_ + bq, cos, sin) * (SWA_HD ** -0.5)
    k_all = rope(k_ + bk, cos, sin)
    v_all = v_ + bv
    k = {(j, g): rows(head(k_all, g), j) for j, g in units}
    v = {(j, g): rows(head(v_all, g), j) for j, g in units}
    q = {(j, g): cat_rows(*[rows(head(q_all, h), j) for h in hs(g)]) for j, g in units}
    kp = lambda j, g: kprev[g] if j == 0 else k[(j - 1, g)]
    vp = lambda j, g: vprev[g] if j == 0 else v[(j - 1, g)]
    s = {(j, g): jnp.where(ok[j], mmx(q[(j, g)], cat_rows(kp(j, g), k[(j, g)]), "nt", P_SWA), NEG) for j, g in units}
    sink = [cat_rows(*[jnp.broadcast_to(sinks[h], (c, 1)) for h in hs(g)]) for g in range(SWA_KV_HEADS)]
    m = {(j, g): lax.stop_gradient(jnp.maximum(jnp.max(s[(j, g)], axis=-1, keepdims=True), sink[g])) for j, g in units}
    p = {u: jnp.exp(s[u] - m[u]) for u in units}
    ones = jnp.ones((2 * c, SWA_HD), F32)
    pv = {(j, g): mmx(p[(j, g)], cat_rows(vp(j, g), v[(j, g)]), "nn", P_SWA) for j, g in units}
    den = {u: mmx(p[u], ones, "nn", P_SWA) for u in units}
    o = {(j, g): pv[(j, g)] / (den[(j, g)] + jnp.exp(sink[g] - m[(j, g)])) for j, g in units}
    outs = [cat_rows(*[o[(j, g)][i * c:(i + 1) * c] for j in range(n_sub)]) for g in range(SWA_KV_HEADS) for i in range(ng)]
    last = n_sub - 1
    return outs, ([k[(last, g)] for g in range(SWA_KV_HEADS)], [v[(last, g)] for g in range(SWA_KV_HEADS)])


def _heads(ref, n, w, rows=slice(None)):
    return [ref[rows, h * w:(h + 1) * w] for h in range(n)]


def _put_heads(ref, vals, w, rows=slice(None), add=False):
    for h, val in enumerate(vals):
        if add:
            ref[rows, h * w:(h + 1) * w] += val
        else:
            ref[rows, h * w:(h + 1) * w] = val


def _col(block_w, name, table):
    off, w = table[name]
    assert off % block_w == 0 and w % block_w == 0
    return off // block_w


def _tok_spec(c, w, colblock, n=None):
    if n is None:
        return pl.BlockSpec((c, w), lambda i: (i, colblock))
    return pl.BlockSpec((c, w), lambda i: (n - 1 - i, colblock))


def _full_spec(shape):
    return pl.BlockSpec(shape, lambda i: (0,) * len(shape))


def _matmul(name, a, b, mode, tm, tn, out_dtype=F32):
    (m, kd) = (a.shape[1], a.shape[0]) if mode == "tn" else a.shape
    n = b.shape[0] if mode == "nt" else b.shape[1]
    assert m % tm == 0 and n % tn == 0
    a_spec = pl.BlockSpec((kd, tm), lambda j, i: (0, i)) if mode == "tn" else pl.BlockSpec((tm, kd), lambda j, i: (i, 0))
    b_spec = pl.BlockSpec((tn, kd), lambda j, i: (j, 0)) if mode == "nt" else pl.BlockSpec((kd, tn), lambda j, i: (0, j))

    def body(a_ref, b_ref, o_ref):
        o_ref[...] = lax.dot_general(a_ref[...].astype(BF16), b_ref[...].astype(BF16), DIMS[mode],
                                     preferred_element_type=F32).astype(out_dtype)

    return pl.pallas_call(
        body, name=name, grid=(n // tn, m // tm), in_specs=[a_spec, b_spec],
        out_specs=pl.BlockSpec((tm, tn), lambda j, i: (i, j)), out_shape=jax.ShapeDtypeStruct((m, n), out_dtype),
        compiler_params=_cparams(("arbitrary", "arbitrary")))(a, b)


TOK_TILE = 512
PROJ_ROWS = 1024
DW_COLS = 512
ADAM_COLS = 256


def _matmul_fused(name, a, b, mode, tiles, rows, outs, sums, epilogue, comm=(), kinds=()):
    made = callable(a)
    m = tiles[0][0].shape[0] if made else a.shape[0]
    kd = b.shape[0] if mode == "nn" else b.shape[1]
    n = b.shape[1] if mode == "nn" else b.shape[0]
    tm = TOK_TILE
    steps = m // tm
    if made:
        outs = [(kd, BF16)] + list(outs)
    nt_, nr, no, ns, ncomm = len(tiles), len(rows), len(outs), len(sums), len(comm)

    def body(*refs):
        at = 1 if made else 2
        b_ref = refs[at - 1]
        tile_refs, row_refs, comm_in = refs[at:at + nt_], refs[at + nt_:at + nt_ + nr], refs[at + nt_ + nr:at + nt_ + nr + ncomm]
        at += nt_ + nr + ncomm
        out_refs, sum_refs, comm_out = refs[at:at + no], refs[at + no:at + no + ns], refs[at + no + ns:at + no + ns + ncomm]
        sems = refs[at + no + ns + ncomm:]
        i = pl.program_id(0)

        @pl.when(i == 0)
        def _():
            if ncomm:
                _comm_start(*_comm_copies(comm_in, comm_out, kinds, *sems))
            for ref in sum_refs:
                ref[...] = jnp.zeros_like(ref)

        extras = [r[...] for r in tile_refs] + [r[...] for r in row_refs]
        a_blk = (a(*extras) if made else refs[0][...]).astype(BF16)
        acc = lax.dot_general(a_blk, b_ref[...].astype(BF16), DIMS[mode], preferred_element_type=F32)
        res = epilogue(acc, *extras)
        if made:
            res = (a_blk,) + tuple(res)
        for ref, val in zip(out_refs, res[:no]):
            ref[...] = val.astype(ref.dtype)
        for ref, val in zip(sum_refs, res[no:]):
            ref[...] += val

        if ncomm:
            @pl.when(i == steps - 1)
            def _():
                _comm_wait(*_comm_copies(comm_in, comm_out, kinds, *sems))

    in_specs = ([] if made else [pl.BlockSpec((tm, kd), lambda i: (i, 0))]) + [_full_spec(b.shape)]
    in_specs += [pl.BlockSpec((tm, w), functools.partial(lambda i, cb: (i, cb), cb=cb)) for _, w, cb in tiles]
    in_specs += [_full_spec(r.shape) for r in rows] + [ANY] * ncomm
    out_specs = [pl.BlockSpec((tm, w), lambda i: (i, 0)) for w, _ in outs] + [_full_spec((1, w)) for w in sums] + [ANY] * ncomm
    out_shape = ([jax.ShapeDtypeStruct((m, w), dt) for w, dt in outs] + [jax.ShapeDtypeStruct((1, w), F32) for w in sums]
                 + _comm_out_shapes(comm, kinds))
    return pl.pallas_call(body, name=name, grid=(steps,), in_specs=in_specs, out_specs=out_specs, out_shape=out_shape,
                          scratch_shapes=_comm_scratch(ncomm) if ncomm else [],
                          compiler_params=_cparams(("arbitrary",)))(*([] if made else [a]), b, *[t[0] for t in tiles], *rows, *comm)


def _resid_norm(y, x, w):
    h = x + y
    return h, rms(h, w)


def _norm_back(dhn, h, dres, w):
    _, vjp = jax.vjp(rms, h, w)
    dh, dw = vjp(dhn)
    return dh + dres, dw


def _gate_back(dog, *o_and_gate):
    outs, g = o_and_gate[:-1], o_and_gate[-1]
    s = sigmoid(g)
    silu, dsilu = g * s, s * (1.0 + g * (1.0 - s))
    d_outs, c = [], 0
    for o in outs:
        w = o.shape[1]
        d_outs.append(dog[:, c:c + w] * silu[:, c:c + w])
        c += w
    o_all = outs[0] if len(outs) == 1 else jnp.concatenate(outs, axis=1)
    return (*d_outs, dog * o_all * dsilu)


def _loss_head(y1, h1, target, b_out, fw):
    def f(h2, w):
        err = rms(h2, w) - target
        return 0.5 * jnp.sum(jnp.mean(err * err, axis=-1, keepdims=True), axis=0, keepdims=True)

    loss, vjp = jax.vjp(f, h1 + y1 + b_out, fw)
    dh2, dfw = vjp(jnp.ones((1, 1), F32))
    return dh2, jnp.broadcast_to(loss, (1, LANES)), jnp.sum(dh2, axis=0, keepdims=True), dfw


def _gla_load(q_ref, k_ref, v_ref, gl_ref, up_ref, bias_ref, nw_ref):
    toks = (q_ref[...], k_ref[...], v_ref[...], gl_ref[...])
    params = (up_ref[...], bias_ref[...], nw_ref[...])
    return toks, params


def _gla_specs(c, n=None):
    toks = [_tok_spec(c, GLA_KEY, _col(GLA_KEY, "gq", C0), n), _tok_spec(c, GLA_KEY, _col(GLA_KEY, "gk", C0), n),
            _tok_spec(c, GLA_VAL, _col(GLA_VAL, "gv", C0), n), _tok_spec(c, LOW, _col(LOW, "glow", C0), n)]
    return toks, [_full_spec(s) for s in GLA_PARAM_SHAPES]


GLA_PARAM_SHAPES = [(LOW, GLA_KEY), (1, GLA_KEY), (1, GLA_DV)]
GLA_STATE = (GLA_HEADS * GLA_DV, GLA_DK)


def _gla_fwd(proj0, gk_up, gk_bias, norm_w):
    t = proj0.shape[0]
    c = GLA_STEP
    nc = t // c
    toks_s, params_s = _gla_specs(c)

    def body(q_ref, k_ref, v_ref, gl_ref, up_ref, bias_ref, nw_ref, o_ref, st_ref, s_scr):
        @pl.when(pl.program_id(0) == 0)
        def _():
            s_scr[...] = jnp.zeros_like(s_scr)

        st_ref[...] = s_scr[...]
        toks, params = _gla_load(q_ref, k_ref, v_ref, gl_ref, up_ref, bias_ref, nw_ref)
        state = [s_scr[h * GLA_DV:(h + 1) * GLA_DV, :] for h in range(GLA_HEADS)]
        o_ref[...], new = gla_chunk(state, toks, params)
        for h in range(GLA_HEADS):
            s_scr[h * GLA_DV:(h + 1) * GLA_DV, :] = new[h]

    return pl.pallas_call(
        body, name="gla_fwd", grid=(nc,), in_specs=toks_s + params_s,
        out_specs=(_tok_spec(c, GLA_VAL, 0), pl.BlockSpec(GLA_STATE, lambda i: (i, 0))),
        out_shape=(jax.ShapeDtypeStruct((t, GLA_VAL), F32), jax.ShapeDtypeStruct((nc * GLA_STATE[0], GLA_DK), F32)),
        scratch_shapes=[pltpu.VMEM(GLA_STATE, F32)], compiler_params=_cparams(("arbitrary",)))(
            proj0, proj0, proj0, proj0, gk_up, gk_bias, norm_w)


def _gla_bwd(proj0, gk_up, gk_bias, norm_w, states, do):
    t = proj0.shape[0]
    c = GLA_STEP
    nc = t // c
    toks_s, params_s = _gla_specs(c, nc)

    def body(q_ref, k_ref, v_ref, gl_ref, up_ref, bias_ref, nw_ref, st_ref, do_ref,
             dq_ref, dk_ref, dv_ref, dgl_ref, dup_ref, dbias_ref, dnw_ref, ds_scr):
        @pl.when(pl.program_id(0) == 0)
        def _():
            ds_scr[...] = jnp.zeros_like(ds_scr)
            dup_ref[...] = jnp.zeros_like(dup_ref)
            dbias_ref[...] = jnp.zeros_like(dbias_ref)
            dnw_ref[...] = jnp.zeros_like(dnw_ref)

        toks, params = _gla_load(q_ref, k_ref, v_ref, gl_ref, up_ref, bias_ref, nw_ref)
        rows = lambda h: slice(h * GLA_DV, (h + 1) * GLA_DV)
        state = [st_ref[rows(h), :] for h in range(GLA_HEADS)]
        _, vjp = jax.vjp(gla_chunk, state, toks, params)
        dstate_in = [ds_scr[rows(h), :] for h in range(GLA_HEADS)]
        dstate, dtoks, (dup, dbias, dnw) = vjp((do_ref[...], dstate_in))
        for ref, val in zip((dq_ref, dk_ref, dv_ref, dgl_ref), dtoks):
            ref[...] = val.astype(ref.dtype)
        dup_ref[...] += dup
        dbias_ref[...] += dbias
        dnw_ref[...] += dnw
        for h in range(GLA_HEADS):
            ds_scr[rows(h), :] = dstate[h]

    rev = lambda w: pl.BlockSpec((c, w), lambda i: (nc - 1 - i, 0))
    tok_widths = (GLA_KEY, GLA_KEY, GLA_VAL, LOW)
    return pl.pallas_call(
        body, name="gla_bwd", grid=(nc,),
        in_specs=toks_s + params_s + [pl.BlockSpec(GLA_STATE, lambda i: (nc - 1 - i, 0)), rev(GLA_VAL)],
        out_specs=[rev(w) for w in tok_widths] + params_s,
        out_shape=[jax.ShapeDtypeStruct((t, w), BF16) for w in tok_widths] + [jax.ShapeDtypeStruct(s, F32) for s in GLA_PARAM_SHAPES],
        scratch_shapes=[pltpu.VMEM(GLA_STATE, F32)], compiler_params=_cparams(("arbitrary",)))(
            proj0, proj0, proj0, proj0, gk_up, gk_bias, norm_w, states, do)


RWKV_PARAM_SHAPES = [(1, RWKV_W), (1, RWKV_W), (1, RWKV_W), (1, LOW), (1, LOW), (1, RWKV_W), (LOW, RWKV_W), (1, RWKV_W),
                     (LOW, RWKV_W), (1, RWKV_W), (1, RWKV_W), (1, RWKV_W), (1, RWKV_W), (1, RWKV_W)]
RWKV_STATE = (RWKV_HEADS * RWKV_N, RWKV_N)
RWKV_TOK_WIDTHS = (RWKV_W, RWKV_W, RWKV_W, LOW, LOW)
PREV_W = sum(RWKV_TOK_WIDTHS)
PREV_COLS = [slice(sum(RWKV_TOK_WIDTHS[:i]), sum(RWKV_TOK_WIDTHS[:i + 1])) for i in range(len(RWKV_TOK_WIDTHS))]


def _rwkv_load(r_ref, k_ref, v_ref, xw_ref, xa_ref, p_refs):
    toks = (r_ref[...], k_ref[...], v_ref[...], xw_ref[...], xa_ref[...])
    return toks, tuple(p[...] for p in p_refs)


def _rwkv_state(s_ref, prev_ref):
    n = RWKV_N
    S = [s_ref[h * n:(h + 1) * n, :] for h in range(RWKV_HEADS)]
    return (S,) + tuple(prev_ref[0:1, cols] for cols in PREV_COLS)


def _rwkv_put_state(s_ref, prev_ref, state):
    n = RWKV_N
    for h in range(RWKV_HEADS):
        s_ref[h * n:(h + 1) * n, :] = state[0][h]
    for cols, val in zip(PREV_COLS, state[1:]):
        prev_ref[0:1, cols] = val


def _rwkv_specs(c, n=None):
    toks = [_tok_spec(c, w, _col(w, name, C0), n) for name, w in zip(("r", "k", "v", "xw", "xa"), RWKV_TOK_WIDTHS)]
    return toks, [_full_spec(s) for s in RWKV_PARAM_SHAPES]


def _rwkv_fwd(proj0, params, comm, kinds):
    t = proj0.shape[0]
    c = RWKV_STEP
    nc = t // c
    toks_s, params_s = _rwkv_specs(c)
    npar, ncomm = len(params), len(comm)

    def body(*refs):
        tok_refs, p_refs = refs[:5], refs[5:5 + npar]
        comm_in = refs[5 + npar:5 + npar + ncomm]
        o_ref, st_ref, pst_ref = refs[5 + npar + ncomm:8 + npar + ncomm]
        comm_out = refs[8 + npar + ncomm:8 + npar + 2 * ncomm]
        s_scr, prev_scr = refs[8 + npar + 2 * ncomm:10 + npar + 2 * ncomm]
        sems = refs[10 + npar + 2 * ncomm:]
        i = pl.program_id(0)

        @pl.when(i == 0)
        def _():
            _comm_start(*_comm_copies(comm_in, comm_out, kinds, *sems))
            s_scr[...] = jnp.zeros_like(s_scr)
            prev_scr[...] = jnp.zeros_like(prev_scr)

        st_ref[...] = s_scr[...]
        pst_ref[...] = prev_scr[...]
        toks, prm = _rwkv_load(*tok_refs, p_refs)
        o_ref[...], new = rwkv_chunks(_rwkv_state(s_scr, prev_scr), toks, prm)
        _rwkv_put_state(s_scr, prev_scr, new)

        @pl.when(i == nc - 1)
        def _():
            _comm_wait(*_comm_copies(comm_in, comm_out, kinds, *sems))

    outs = pl.pallas_call(
        body, name="rwkv_fwd", grid=(nc,), in_specs=toks_s + params_s + [ANY] * ncomm,
        out_specs=[_tok_spec(c, RWKV_W, 0), pl.BlockSpec(RWKV_STATE, lambda i: (i, 0)), pl.BlockSpec((8, PREV_W), lambda i: (i, 0))]
        + [ANY] * ncomm,
        out_shape=[jax.ShapeDtypeStruct((t, RWKV_W), F32), jax.ShapeDtypeStruct((nc * RWKV_STATE[0], RWKV_N), F32),
                   jax.ShapeDtypeStruct((nc * 8, PREV_W), F32)] + _comm_out_shapes(comm, kinds),
        scratch_shapes=[pltpu.VMEM(RWKV_STATE, F32), pltpu.VMEM((8, PREV_W), F32)] + _comm_scratch(ncomm),
        compiler_params=_cparams(("arbitrary",)))(proj0, proj0, proj0, proj0, proj0, *params, *comm)
    return outs[0], outs[1], outs[2], outs[3:]


def _rwkv_bwd(proj0, params, states, prevs, do, comm, kinds):
    t = proj0.shape[0]
    c = RWKV_STEP
    nc = t // c
    toks_s, params_s = _rwkv_specs(c, nc)
    npar, ncomm = len(params), len(comm)

    def body(*refs):
        tok_refs, p_refs = refs[:5], refs[5:5 + npar]
        st_ref, pst_ref, do_ref = refs[5 + npar:8 + npar]
        comm_in = refs[8 + npar:8 + npar + ncomm]
        outs = refs[8 + npar + ncomm:]
        dtok_refs, dp_refs, comm_out = outs[:5], outs[5:5 + npar], outs[5 + npar:5 + npar + ncomm]
        ds_scr, dprev_scr = outs[5 + npar + ncomm:7 + npar + ncomm]
        sems = outs[7 + npar + ncomm:]
        i = pl.program_id(0)

        @pl.when(i == 0)
        def _():
            _comm_start(*_comm_copies(comm_in, comm_out, kinds, *sems))
            ds_scr[...] = jnp.zeros_like(ds_scr)
            dprev_scr[...] = jnp.zeros_like(dprev_scr)
            for dp in dp_refs:
                dp[...] = jnp.zeros_like(dp)

        toks, prm = _rwkv_load(*tok_refs, p_refs)
        _, vjp = jax.vjp(rwkv_chunks, _rwkv_state(st_ref, pst_ref), toks, prm)
        dstate, dtoks, dprm = vjp((do_ref[...], _rwkv_state(ds_scr, dprev_scr)))
        for ref, val in zip(dtok_refs, dtoks):
            ref[...] = val.astype(ref.dtype)
        for ref, val in zip(dp_refs, dprm):
            ref[...] += val
        _rwkv_put_state(ds_scr, dprev_scr, dstate)

        @pl.when(i == nc - 1)
        def _():
            _comm_wait(*_comm_copies(comm_in, comm_out, kinds, *sems))

    rev = lambda w: pl.BlockSpec((c, w), lambda i: (nc - 1 - i, 0))
    outs = pl.pallas_call(
        body, name="rwkv_bwd", grid=(nc,),
        in_specs=toks_s + params_s + [pl.BlockSpec(RWKV_STATE, lambda i: (nc - 1 - i, 0)),
                                      pl.BlockSpec((8, PREV_W), lambda i: (nc - 1 - i, 0)), rev(RWKV_W)] + [ANY] * ncomm,
        out_specs=[rev(w) for w in RWKV_TOK_WIDTHS] + params_s + [ANY] * ncomm,
        out_shape=[jax.ShapeDtypeStruct((t, w), BF16) for w in RWKV_TOK_WIDTHS]
        + [jax.ShapeDtypeStruct(s, F32) for s in RWKV_PARAM_SHAPES] + _comm_out_shapes(comm, kinds),
        scratch_shapes=[pltpu.VMEM(RWKV_STATE, F32), pltpu.VMEM((8, PREV_W), F32)] + _comm_scratch(ncomm),
        compiler_params=_cparams(("arbitrary",)))(proj0, proj0, proj0, proj0, proj0, *params, states, prevs, do, *comm)
    return outs[:5], outs[5:5 + npar], outs[5 + npar:]


def _swa_load(q_ref, k_ref, v_ref, cos_ref, sin_ref, bq_ref, bk_ref, bv_ref, sk_ref):
    toks = (q_ref[...], k_ref[...], v_ref[...], cos_ref[...], sin_ref[...])
    params = (bq_ref[...], bk_ref[...], bv_ref[...], _heads(sk_ref, SWA_Q_HEADS, 1))
    return toks, params


SWA_TOK_WIDTHS = (MIX, SWA_KV, SWA_KV)
SWA_PARAM_SHAPES = [(1, MIX), (1, SWA_KV), (1, SWA_KV), (1, SWA_Q_HEADS)]
SWA_STATE = (WINDOW, SWA_KV)


def _swa_specs(c, n=None):
    toks = [_tok_spec(c, w, _col(w, name, C1), n) for name, w in zip(("q", "k", "v"), SWA_TOK_WIDTHS)]
    toks += [_tok_spec(c, LANES, 0, n), _tok_spec(c, LANES, 0, n)]
    return toks, [_full_spec(s) for s in SWA_PARAM_SHAPES]


def _swa_fwd(proj1, cos, sin, bq, bk, bv, sinks):
    t = proj1.shape[0]
    c = SWA_STEP
    nb = t // c
    toks_s, params_s = _swa_specs(c)
    state_spec = pl.BlockSpec(SWA_STATE, lambda i: (i, 0))
    kv = SWA_KV_HEADS

    def body(q_ref, k_ref, v_ref, cos_ref, sin_ref, bq_ref, bk_ref, bv_ref, sk_ref, o_ref, kst_ref, vst_ref, k_scr, v_scr):
        first = pl.program_id(0) == 0

        @pl.when(first)
        def _():
            k_scr[...] = jnp.zeros_like(k_scr)
            v_scr[...] = jnp.zeros_like(v_scr)

        kst_ref[...] = k_scr[...]
        vst_ref[...] = v_scr[...]
        toks, params = _swa_load(q_ref, k_ref, v_ref, cos_ref, sin_ref, bq_ref, bk_ref, bv_ref, sk_ref)
        outs, (kn, vn) = swa_chunk((_heads(k_scr, kv, SWA_HD), _heads(v_scr, kv, SWA_HD)), toks, params, first)
        _put_heads(o_ref, outs, SWA_HD)
        _put_heads(k_scr, kn, SWA_HD)
        _put_heads(v_scr, vn, SWA_HD)

    saved = jax.ShapeDtypeStruct((nb * WINDOW, SWA_KV), F32)
    return pl.pallas_call(
        body, name="swa_fwd", grid=(nb,), in_specs=toks_s + params_s,
        out_specs=(_tok_spec(c, MIX, 0), state_spec, state_spec),
        out_shape=(jax.ShapeDtypeStruct((t, MIX), F32), saved, saved),
        scratch_shapes=[pltpu.VMEM(SWA_STATE, F32), pltpu.VMEM(SWA_STATE, F32)],
        compiler_params=_cparams(("arbitrary",)))(proj1, proj1, proj1, cos, sin, bq, bk, bv, sinks)


def _swa_bwd(proj1, cos, sin, bq, bk, bv, sinks, kst, vst, do):
    t = proj1.shape[0]
    c = SWA_STEP
    nb = t // c
    toks_s, params_s = _swa_specs(c, nb)
    state_spec = pl.BlockSpec(SWA_STATE, lambda i: (nb - 1 - i, 0))
    kv = SWA_KV_HEADS

    def body(q_ref, k_ref, v_ref, cos_ref, sin_ref, bq_ref, bk_ref, bv_ref, sk_ref, kst_ref, vst_ref, do_ref,
             dq_ref, dk_ref, dv_ref, dbq_ref, dbk_ref, dbv_ref, dsk_ref, dk_scr, dv_scr):
        i = pl.program_id(0)

        @pl.when(i == 0)
        def _():
            dk_scr[...] = jnp.zeros_like(dk_scr)
            dv_scr[...] = jnp.zeros_like(dv_scr)
            for ref in (dbq_ref, dbk_ref, dbv_ref, dsk_ref):
                ref[...] = jnp.zeros_like(ref)

        first = i == nb - 1
        toks, params = _swa_load(q_ref, k_ref, v_ref, cos_ref, sin_ref, bq_ref, bk_ref, bv_ref, sk_ref)
        f = functools.partial(swa_chunk, first=first)
        _, vjp = jax.vjp(f, (_heads(kst_ref, kv, SWA_HD), _heads(vst_ref, kv, SWA_HD)), toks, params)
        dstate_in = (_heads(dk_scr, kv, SWA_HD), _heads(dv_scr, kv, SWA_HD))
        (dkp, dvp), (dq, dk, dv, _, _), (dbq, dbk, dbv, dsk) = vjp((_heads(do_ref, SWA_Q_HEADS, SWA_HD), dstate_in))
        dq_ref[...], dk_ref[...], dv_ref[...] = dq.astype(BF16), dk.astype(BF16), dv.astype(BF16)
        dbq_ref[...] += dbq
        dbk_ref[...] += dbk
        dbv_ref[...] += dbv
        _put_heads(dsk_ref, dsk, 1, add=True)
        _put_heads(dk_scr, dkp, SWA_HD)
        _put_heads(dv_scr, dvp, SWA_HD)

    rev = lambda w: pl.BlockSpec((c, w), lambda i: (nb - 1 - i, 0))
    return pl.pallas_call(
        body, name="swa_bwd", grid=(nb,), in_specs=toks_s + params_s + [state_spec, state_spec, rev(MIX)],
        out_specs=[rev(w) for w in SWA_TOK_WIDTHS] + params_s,
        out_shape=[jax.ShapeDtypeStruct((t, w), BF16) for w in SWA_TOK_WIDTHS] + [jax.ShapeDtypeStruct(s, F32) for s in SWA_PARAM_SHAPES],
        scratch_shapes=[pltpu.VMEM(SWA_STATE, F32), pltpu.VMEM(SWA_STATE, F32)],
        compiler_params=_cparams(("arbitrary",)))(proj1, proj1, proj1, cos, sin, bq, bk, bv, sinks, kst, vst, do)


MESH = pl.DeviceIdType.MESH
ANY = pl.BlockSpec(memory_space=pl.ANY)


def _my_place():
    return lax.axis_index("x"), lax.axis_index("y"), lax.axis_index("c")


def _all_gather(shards):
    n = len(shards)

    def body(*refs):
        in_refs, out_refs = refs[:n], refs[n:2 * n]
        send_sems, recv_sems, local_sems = refs[2 * n:]
        x, y, c = _my_place()
        me, sibling = (x, y, c), (x, y, 1 - c)
        chips = [(1 - x, y), (x, 1 - y), (1 - x, 1 - y)]

        def slot(out_ref, place):
            px, py, pc = place
            return out_ref.at[4 * px + 2 * py + pc]

        def copy(a, k, block, to, src=None):
            return pltpu.make_async_remote_copy(
                src_ref=slot(out_refs[a], block) if src is None else src, dst_ref=slot(out_refs[a], block),
                send_sem=send_sems.at[a, k], recv_sem=recv_sems.at[a, k], device_id=to, device_id_type=MESH)

        mine = [pltpu.make_async_copy(in_refs[a], slot(out_refs[a], me), local_sems.at[a]) for a in range(n)]
        for cp in mine:
            cp.start()
        first = []
        for a in range(n):
            first.append(copy(a, 0, me, sibling, src=in_refs[a]))
            first += [copy(a, 1 + j, me, (*chip, c), src=in_refs[a]) for j, chip in enumerate(chips)]
        for cp in first:
            cp.start()
        passed = []
        for j, chip in enumerate(chips):
            for a in range(n):
                copy(a, 1 + j, (*chip, c), me).wait_recv()
                fwd = copy(a, 4 + j, (*chip, c), sibling)
                fwd.start()
                passed.append(fwd)
        for a in range(n):
            copy(a, 0, sibling, me).wait_recv()
            for j, chip in enumerate(chips):
                copy(a, 4 + j, (*chip, 1 - c), me).wait_recv()
        for cp in first + passed:
            cp.wait_send()
        for cp in mine:
            cp.wait()

    return pl.pallas_call(
        body, name="all_gather_weights", in_specs=[ANY] * n, out_specs=[ANY] * n,
        out_shape=[jax.ShapeDtypeStruct((N_DEV,) + s.shape, s.dtype) for s in shards],
        scratch_shapes=_comm_scratch(n))(*shards)


def _comm_copies(in_refs, out_refs, kinds, send_sems=None, recv_sems=None, local_sems=None):
    x, y, c = _my_place()
    my_idx = 4 * x + 2 * y + c
    src = lambda a, idx: in_refs[a] if kinds[a] == "gather" else in_refs[a].at[idx]
    local = [pltpu.make_async_copy(src(a, my_idx), out_refs[a].at[my_idx], local_sems.at[a]) for a in range(len(kinds))]
    remote = []
    for rel in range(1, N_DEV):
        px, py, pc = x ^ ((rel >> 2) & 1), y ^ ((rel >> 1) & 1), c ^ (rel & 1)
        for a in range(len(kinds)):
            remote.append(pltpu.make_async_remote_copy(
                src_ref=src(a, 4 * px + 2 * py + pc), dst_ref=out_refs[a].at[my_idx], send_sem=send_sems.at[a, rel - 1],
                recv_sem=recv_sems.at[a, rel - 1], device_id=(px, py, pc), device_id_type=MESH))
    return local, remote


def _comm_start(local, remote):
    for cp in local + remote:
        cp.start()


def _comm_wait(local, remote):
    for cp in remote:
        cp.wait_recv()
    for cp in remote:
        cp.wait_send()
    for cp in local:
        cp.wait()


def _comm_out_shapes(arrays, kinds):
    return [jax.ShapeDtypeStruct(((N_DEV,) + a.shape) if k == "gather" else a.shape, a.dtype) for a, k in zip(arrays, kinds)]


def _comm_scratch(n):
    return [] if n == 0 else [pltpu.SemaphoreType.DMA((n, N_DEV - 1)), pltpu.SemaphoreType.DMA((n, N_DEV - 1)), pltpu.SemaphoreType.DMA((n,))]


def _sequencer_scatter(name, parts, collective_id):
    src = jax.new_ref(parts, memory_space=pltpu.MemorySpace.HBM)
    dst = jax.empty_ref(jax.ShapeDtypeStruct(parts.shape, parts.dtype), memory_space=pltpu.MemorySpace.HBM)

    @pl.kernel(mesh=plsc.ScalarSubcoreMesh(axis_name="sequencer", num_cores=1), name=name,
               scratch_types=(pltpu.SemaphoreType.DMA((N_DEV - 1,)), pltpu.SemaphoreType.DMA((N_DEV - 1,))),
               compiler_params=pltpu.CompilerParams(collective_id=collective_id))
    def launch(send_sems, recv_sems):
        x, y, c = _my_place()
        my_idx = 4 * x + 2 * y + c
        peers = [(x ^ ((rel >> 2) & 1), y ^ ((rel >> 1) & 1), c ^ (rel & 1)) for rel in range(1, N_DEV)]
        barrier = pltpu.get_barrier_semaphore()
        for peer in peers:
            pl.semaphore_signal(barrier, inc=1, device_id=peer, device_id_type=MESH)
        pl.semaphore_wait(barrier, N_DEV - 1)
        copies = [pltpu.make_async_remote_copy(
            src_ref=src.at[4 * px + 2 * py + pc], dst_ref=dst.at[my_idx], send_sem=send_sems.at[k], recv_sem=recv_sems.at[k],
            device_id=(px, py, pc), device_id_type=MESH) for k, (px, py, pc) in enumerate(peers)]
        for cp in copies:
            cp.start()
        for cp in copies:
            cp.wait_recv()
        for cp in copies:
            cp.wait_send()

    launch()
    return dst[...]


def _sequencer_exchange(name, arrays, kinds, collective_id):
    n = len(arrays)
    srcs = [jax.new_ref(a, memory_space=pltpu.MemorySpace.HBM) for a in arrays]
    dsts = [jax.empty_ref(s, memory_space=pltpu.MemorySpace.HBM) for s in _comm_out_shapes(arrays, kinds)]

    @pl.kernel(mesh=plsc.ScalarSubcoreMesh(axis_name="sequencer", num_cores=1), name=name,
               scratch_types=(pltpu.SemaphoreType.DMA((n, N_DEV - 1)), pltpu.SemaphoreType.DMA((n, N_DEV - 1))),
               compiler_params=pltpu.CompilerParams(collective_id=collective_id))
    def launch(send_sems, recv_sems):
        x, y, c = _my_place()
        my_idx = 4 * x + 2 * y + c
        peers = [(x ^ ((rel >> 2) & 1), y ^ ((rel >> 1) & 1), c ^ (rel & 1)) for rel in range(1, N_DEV)]
        barrier = pltpu.get_barrier_semaphore()
        for peer in peers:
            pl.semaphore_signal(barrier, inc=1, device_id=peer, device_id_type=MESH)
        pl.semaphore_wait(barrier, N_DEV - 1)
        copies = [pltpu.make_async_remote_copy(
            src_ref=srcs[a] if kinds[a] == "gather" else srcs[a].at[4 * px + 2 * py + pc], dst_ref=dsts[a].at[my_idx],
            send_sem=send_sems.at[a, k], recv_sem=recv_sems.at[a, k], device_id=(px, py, pc), device_id_type=MESH)
            for k, (px, py, pc) in enumerate(peers) for a in range(n)]
        for cp in copies:
            cp.start()
        for cp in copies:
            cp.wait_recv()
        for cp in copies:
            cp.wait_send()

    launch()
    return [dst[...] for dst in dsts]


def _own_block(received, mine, kind):
    my_idx = 4 * lax.axis_index("x") + 2 * lax.axis_index("y") + lax.axis_index("c")
    block = mine[None] if kind == "gather" else lax.dynamic_slice(mine, (my_idx,) + (0,) * (mine.ndim - 1), (1,) + mine.shape[1:])
    return lax.dynamic_update_slice(received, block, (my_idx,) + (0,) * (received.ndim - 1))


def _exchange(arrays, kinds):
    n = len(arrays)

    def body(*refs):
        copies = _comm_copies(refs[:n], refs[n:2 * n], kinds, *refs[2 * n:])
        _comm_start(*copies)
        _comm_wait(*copies)

    return pl.pallas_call(body, name="exchange_grads", in_specs=[ANY] * n, out_specs=[ANY] * n,
                          out_shape=_comm_out_shapes(arrays, kinds), scratch_shapes=_comm_scratch(n))(*arrays)


def _adam_math(w, g, m, v):
    m = ADAM_B1 * m + (1.0 - ADAM_B1) * g
    v = ADAM_B2 * v + (1.0 - ADAM_B2) * (g * g)
    m_hat = m / (1.0 - ADAM_B1 ** ADAM_STEP)
    v_hat = v / (1.0 - ADAM_B2 ** ADAM_STEP)
    delta = -ADAM_LR * (m_hat / (jnp.sqrt(v_hat) + ADAM_EPS) + ADAM_WD * w)
    return delta, m, v


def _adamw(name, w, gslots, m, v, tc):
    r, cc = w.shape
    assert cc % tc == 0
    tile = pl.BlockSpec((r, tc), lambda i: (0, i))

    def body(w_ref, g_ref, m_ref, v_ref, go_ref, d_ref, mo_ref, vo_ref):
        g = g_ref[0].astype(F32)
        for s in range(1, N_DEV):
            g = g + g_ref[s].astype(F32)
        d, mn, vn = _adam_math(w_ref[...], g, m_ref[...], v_ref[...])
        go_ref[...] = g
        d_ref[...] = d
        mo_ref[...] = mn
        vo_ref[...] = vn

    shp = jax.ShapeDtypeStruct((r, cc), F32)
    return pl.pallas_call(body, name=name, grid=(cc // tc,),
                          in_specs=[tile, pl.BlockSpec((N_DEV, r, tc), lambda i: (0, 0, i)), tile, tile],
                          out_specs=(tile,) * 4, out_shape=(shp,) * 4, compiler_params=_cparams(("arbitrary",)))(w, gslots, m, v)


PACK_TILE = 8 * LANES


def _packed_rows(shape, mode):
    r, w = shape
    return -(-r // 8) * 8 if mode == "rows" else -(-(r * w) // PACK_TILE) * 8


def _pack_small(arrays, modes, lead=False):
    out = []
    for a, mode in zip(arrays, modes):
        a = a.astype(F32) if lead else a.astype(F32)[None]
        if mode == "rows":
            out.append(jnp.pad(a, ((0, 0), (0, (-a.shape[1]) % 8), (0, LANES - a.shape[2]))))
        else:
            flat = a.reshape(a.shape[0], -1)
            out.append(jnp.pad(flat, ((0, 0), (0, (-flat.shape[1]) % PACK_TILE))).reshape(a.shape[0], -1, LANES))
    out = jnp.concatenate(out, axis=1)
    return out if lead else out[0]


def _take_small(packed, row0, shape, mode):
    r, w = shape
    lead = packed.ndim == 3
    if mode == "rows":
        return packed[:, row0:row0 + r, :w] if lead else packed[row0:row0 + r, :w]
    per_row = -(-w // LANES)
    if lead:
        return packed[:, row0:row0 + r * per_row].reshape(packed.shape[0], r, per_row * LANES)[:, :, :w]
    rows = []
    for i in range(r):
        pieces = [packed[row0 + i * per_row + j:row0 + i * per_row + j + 1, :] for j in range(per_row)]
        rows.append((pieces[0] if per_row == 1 else jnp.concatenate(pieces, axis=1))[:, :w])
    return rows[0] if r == 1 else jnp.concatenate(rows, axis=0)


def _adamw_small(slots, specs, ws, ms, vs, loss_row):
    n = len(specs)

    def body(*refs):
        slots_ref, w_refs, m_refs, v_refs = refs[0], refs[1:1 + n], refs[1 + n:1 + 2 * n], refs[1 + 2 * n:1 + 3 * n]
        out_refs, loss_ref = refs[1 + 3 * n:1 + 7 * n], refs[1 + 7 * n]
        gp = slots_ref[0]
        for s in range(1, N_DEV):
            gp = gp + slots_ref[s]
        read = lambda ref: ref[0] if len(ref.shape) == 3 else ref[...]
        for k, (shape, mode, row0) in enumerate(specs):
            g = _take_small(gp, row0, shape, mode)
            d, mn, vn = _adam_math(read(w_refs[k]), g, read(m_refs[k]), read(v_refs[k]))
            for ref, val in zip(out_refs[4 * k:4 * k + 4], (g, d, mn, vn)):
                if len(ref.shape) == 3:
                    ref[0] = val
                else:
                    ref[...] = val
        loss_ref[...] = gp[loss_row:loss_row + 1, :]

    vmem = pl.BlockSpec(memory_space=pltpu.VMEM)
    out_shape = [jax.ShapeDtypeStruct(w.shape, F32) for w in ws for _ in range(4)] + [jax.ShapeDtypeStruct((1, LANES), F32)]
    outs = pl.pallas_call(body, name="adamw_small", in_specs=[vmem] * (1 + 3 * n), out_specs=[vmem] * (4 * n + 1),
                          out_shape=out_shape)(slots, *ws, *ms, *vs)
    return [outs[4 * k:4 * k + 4] for k in range(n)], outs[4 * n]


def _rope_tables(t):
    dim = jnp.arange(LANES) % SWA_HD
    inv_freq = ROPE_THETA ** (-(dim % ROPE_HALF).astype(F32) / ROPE_HALF)
    ang = jnp.arange(t, dtype=F32)[:, None] * jnp.where(dim < 2 * ROPE_HALF, inv_freq, 0.0)[None, :]
    return jnp.cos(ang), jnp.sin(ang)


def _pad_to(a, rows=None, cols=None):
    r = 0 if rows is None else rows - a.shape[0]
    c = 0 if cols is None else cols - a.shape[1]
    return jnp.pad(a, ((0, r), (0, c)))


ORIG0 = dict(gq=(0, 256), gk=(256, 256), gv=(512, 512), glow=(1024, 16), r=(1040, 512), k=(1552, 512), v=(2064, 512),
             xw=(2576, 64), xa=(2640, 64), gate=(2704, 1024))
ORIG0_ORDER = ["gq", "gk", "gv", "glow", "r", "k", "v", "xw", "xa", "gate"]


def _w0t_to_padded(wt):
    rows, at = [], 0
    for name, (off, width) in sorted(C0.items(), key=lambda kv: kv[1][0]):
        assert off == at
        src, src_w = ORIG0[name]
        rows.append(_pad_to(wt[src:src + src_w], rows=width))
        at += width
    rows.append(jnp.zeros((N0P - at, wt.shape[1]), wt.dtype))
    return jnp.concatenate(rows, axis=0)


def _w0t_from_padded(wpt):
    return jnp.concatenate([wpt[C0[n][0]:C0[n][0] + ORIG0[n][1]] for n in ORIG0_ORDER], axis=0)


def _w1t_to_mine(wt):
    return jnp.concatenate([wt[1536:2560], wt[:1536]], axis=0)


def _w1t_from_mine(wt):
    return jnp.concatenate([wt[1024:2560], wt[:1024]], axis=0)


def kernel(x, norm_w, w_in0, gla_gk_up, gla_gk_bias, gla_norm_w, rwkv_mu, rwkv_w0, rwkv_w_up, rwkv_a0, rwkv_a_up, rwkv_k_k, rwkv_k_a, rwkv_r_k, rwkv_ln_w, rwkv_ln_b, w_out0, w_in1, b_in1, attn_sinks, w_out1, b_out1, final_norm_w, loss_target, m_norm_w, m_w_in0, m_gla_gk_up, m_gla_gk_bias, m_gla_norm_w, m_rwkv_mu, m_rwkv_w0, m_rwkv_w_up, m_rwkv_a0, m_rwkv_a_up, m_rwkv_k_k, m_rwkv_k_a, m_rwkv_r_k, m_rwkv_ln_w, m_rwkv_ln_b, m_w_out0, m_w_in1, m_b_in1, m_attn_sinks, m_w_out1, m_b_out1, m_final_norm_w, v_norm_w, v_w_in0, v_gla_gk_up, v_gla_gk_bias, v_gla_norm_w, v_rwkv_mu, v_rwkv_w0, v_rwkv_w_up, v_rwkv_a0, v_rwkv_a_up, v_rwkv_k_k, v_rwkv_k_a, v_rwkv_r_k, v_rwkv_ln_w, v_rwkv_ln_b, v_w_out0, v_w_in1, v_b_in1, v_attn_sinks, v_w_out1, v_b_out1, v_final_norm_w):
    weights = dict(norm_w=norm_w, w_in0=w_in0, gla_gk_up=gla_gk_up, gla_gk_bias=gla_gk_bias, gla_norm_w=gla_norm_w, rwkv_mu=rwkv_mu,
                   rwkv_w0=rwkv_w0, rwkv_w_up=rwkv_w_up, rwkv_a0=rwkv_a0, rwkv_a_up=rwkv_a_up, rwkv_k_k=rwkv_k_k, rwkv_k_a=rwkv_k_a,
                   rwkv_r_k=rwkv_r_k, rwkv_ln_w=rwkv_ln_w, rwkv_ln_b=rwkv_ln_b, w_out0=w_out0, w_in1=w_in1, b_in1=b_in1,
                   attn_sinks=attn_sinks, w_out1=w_out1, b_out1=b_out1, final_norm_w=final_norm_w)
    moms = dict(norm_w=m_norm_w, w_in0=m_w_in0, gla_gk_up=m_gla_gk_up, gla_gk_bias=m_gla_gk_bias, gla_norm_w=m_gla_norm_w,
                rwkv_mu=m_rwkv_mu, rwkv_w0=m_rwkv_w0, rwkv_w_up=m_rwkv_w_up, rwkv_a0=m_rwkv_a0, rwkv_a_up=m_rwkv_a_up,
                rwkv_k_k=m_rwkv_k_k, rwkv_k_a=m_rwkv_k_a, rwkv_r_k=m_rwkv_r_k, rwkv_ln_w=m_rwkv_ln_w, rwkv_ln_b=m_rwkv_ln_b,
                w_out0=m_w_out0, w_in1=m_w_in1, b_in1=m_b_in1, attn_sinks=m_attn_sinks, w_out1=m_w_out1, b_out1=m_b_out1,
                final_norm_w=m_final_norm_w)
    vars_ = dict(norm_w=v_norm_w, w_in0=v_w_in0, gla_gk_up=v_gla_gk_up, gla_gk_bias=v_gla_gk_bias, gla_norm_w=v_gla_norm_w,
                 rwkv_mu=v_rwkv_mu, rwkv_w0=v_rwkv_w0, rwkv_w_up=v_rwkv_w_up, rwkv_a0=v_rwkv_a0, rwkv_a_up=v_rwkv_a_up,
                 rwkv_k_k=v_rwkv_k_k, rwkv_k_a=v_rwkv_k_a, rwkv_r_k=v_rwkv_r_k, rwkv_ln_w=v_rwkv_ln_w, rwkv_ln_b=v_rwkv_ln_b,
                 w_out0=v_w_out0, w_in1=v_w_in1, b_in1=v_b_in1, attn_sinks=v_attn_sinks, w_out1=v_w_out1, b_out1=v_b_out1,
                 final_norm_w=v_final_norm_w)
    names = list(weights)
    big = ["w_in0", "w_out0", "w_in1", "w_out1"]
    small_sharded = ["gla_gk_up", "rwkv_w_up", "rwkv_a_up", "b_in1", "b_out1"]
    replicated = [n for n in names if n not in big and n not in small_sharded]

    xs = x[0]
    tgt = loss_target[0]
    t = xs.shape[0]

    def view(w):
        shape = tuple(w.shape[-2:]) if w.ndim >= 2 else (1, w.shape[0])
        return shape, ("rows" if shape[0] > 1 and shape[1] <= LANES else "flat")

    def layout(ns, row0=0):
        specs = []
        for n in ns:
            shape, mode = view(weights[n])
            specs.append((shape, mode, row0))
            row0 += _packed_rows(shape, mode)
        return specs, row0

    sh_specs, n_shard_rows = layout(small_sharded)
    rep_specs, loss_row = layout(replicated, n_shard_rows)
    sh_modes, rep_modes = [s[1] for s in sh_specs], [s[1] for s in rep_specs]

    small_shard_pack = _pack_small([weights[n].reshape(view(weights[n])[0]) for n in small_sharded], sh_modes)
    g_in0, g_small = _all_gather([w_in0[0].T.astype(BF16), small_shard_pack])
    w0t = _w0t_to_padded(g_in0.reshape(-1, D_MODEL))
    later_shards = [w_out0[0].astype(BF16), w_in1[0].T.astype(BF16), w_out1[0].astype(BF16)]
    gs = [_take_small(g_small, row0, shape, mode) for shape, mode, row0 in sh_specs]
    join_cols = lambda a: jnp.transpose(a, (1, 0, 2)).reshape(a.shape[1], -1)
    gk_up, w_up, a_up = join_cols(gs[0]), join_cols(gs[1]), join_cols(gs[2])
    b_in, b_out = gs[3].reshape(1, -1), gs[4].reshape(1, -1)

    gk_up_p = _pad_to(gk_up, rows=LOW)
    w3, rank = 3 * RWKV_W, rwkv_w_up.shape[1]
    mu = rwkv_mu
    rwkv_params = [mu[:, 0:RWKV_W], mu[:, RWKV_W:2 * RWKV_W], mu[:, 2 * RWKV_W:w3], _pad_to(mu[:, w3:w3 + rank], cols=LOW),
                   _pad_to(mu[:, w3 + rank:], cols=LOW), rwkv_w0, _pad_to(w_up, rows=LOW), rwkv_a0, _pad_to(a_up, rows=LOW),
                   rwkv_k_k, rwkv_k_a, rwkv_r_k.reshape(1, RWKV_W), rwkv_ln_w, rwkv_ln_b]
    bq, bk, bv = b_in[:, :MIX], b_in[:, MIX:MIX + SWA_KV], b_in[:, MIX + SWA_KV:]
    cos, sin = _rope_tables(t)
    nw0, nw1, fw = norm_w[0:1], norm_w[1:2], final_norm_w.reshape(1, D_MODEL)

    d = D_MODEL
    wide = lambda arr: (arr, d, 0)
    silu = lambda g: g * sigmoid(g)
    hn0, proj0 = _matmul_fused("norm0_proj0", rms, w0t, "nt", [wide(xs)], [nw0], [(N0P, F32)], [], lambda acc, x, w: (acc,))
    o_a, gla_states = _gla_fwd(proj0, gk_up_p, gla_gk_bias, gla_norm_w)
    later_shards, _ = lax.optimization_barrier((later_shards, g_in0))
    got = _sequencer_exchange("gather_later_weights", later_shards, ["gather"] * 3, 1)
    o_b, rwkv_states, rwkv_prevs, _ = _rwkv_fwd(proj0, rwkv_params, [], [])
    g_out0, g_in1, g_out1 = [_own_block(r, mine, "gather") for r, mine in zip(got, later_shards)]
    wo0 = g_out0.reshape(MIX, D_MODEL)
    w1t = _w1t_to_mine(g_in1.reshape(-1, D_MODEL))
    wo1 = g_out1.reshape(MIX, D_MODEL)
    og0, h1, hn1 = _matmul_fused(
        "gate0_out0_norm1", lambda oa, ob, gate, x, w: jnp.concatenate([oa, ob], axis=1) * silu(gate), wo0, "nn",
        [(o_a, GLA_VAL, 0), (o_b, RWKV_W, 0), wide(proj0), wide(xs)], [nw1], [(d, F32), (d, BF16)], [],
        lambda acc, oa, ob, gate, x, w: _resid_norm(acc, x, w))
    proj1 = _matmul("proj1", hn1, w1t, "nt", PROJ_ROWS, N1P // 2)
    o_c, kst, vst = _swa_fwd(proj1, cos, sin, bq, bk, bv, attn_sinks)
    og1, dh2, loss_part, d_b_out, d_fw = _matmul_fused(
        "gate1_out1_loss", lambda oc, gate, h, tg, b, w: oc * silu(gate), wo1, "nn",
        [wide(o_c), wide(proj1), wide(h1), wide(tgt)], [b_out, fw], [(d, F32)], [LANES, d, d],
        lambda acc, oc, gate, h, tg, b, w: _loss_head(acc, h, tg, b, w))

    d_oc, d_gate1 = _matmul_fused("out1_dx_gate1", dh2, wo1, "nt", [wide(o_c), wide(proj1)], [], [(d, F32), (d, BF16)], [], _gate_back)
    d_wo1 = _matmul("out1_dw", og1, dh2, "tn", DW_COLS, DW_COLS, BF16)
    dq, dk, dv, d_bq, d_bk, d_bv, d_sinks = _swa_bwd(proj1, cos, sin, bq, bk, bv, attn_sinks, kst, vst, d_oc)
    dproj1 = jnp.concatenate([d_gate1, dq, dk, dv], axis=1)
    dh1, d_nw1 = _matmul_fused("proj1_dx_norm1", dproj1, w1t, "nn", [wide(h1), wide(dh2)], [nw1], [(d, F32)], [d], _norm_back)
    d_w1t = _matmul("proj1_dw", dproj1, hn1, "tn", DW_COLS, d, BF16)
    d_oa, d_ob, d_gate0 = _matmul_fused("out0_dx_gate0", dh1, wo0, "nt", [(o_a, GLA_VAL, 0), (o_b, RWKV_W, 0), wide(proj0)], [],
                                        [(GLA_VAL, F32), (RWKV_W, F32), (d, BF16)], [], _gate_back)
    d_wo0 = _matmul("out0_dw", og0, dh1, "tn", DW_COLS, DW_COLS, BF16)
    dgq, dgk, dgv, dglow, d_gk_up, d_gk_bias, d_gla_nw = _gla_bwd(proj0, gk_up_p, gla_gk_bias, gla_norm_w, gla_states, d_oa)
    row_blocks = lambda a: a.astype(BF16).reshape(N_DEV, -1, D_MODEL)
    early = [row_blocks(_w1t_from_mine(d_w1t)), row_blocks(d_wo1), row_blocks(d_wo0)]
    got = _sequencer_exchange("exchange_early_grads", early, ["scatter"] * 3, 2)
    (dr, dkk, dvv, dxw, dxa), d_rp, _ = _rwkv_bwd(proj0, rwkv_params, rwkv_states, rwkv_prevs, d_ob, [], [])
    r_in1, r_out1, r_out0 = [_own_block(r, mine, "scatter") for r, mine in zip(got, early)]
    pad = jnp.zeros((t, N0P - C0["xa"][0] - C0["xa"][1]), BF16)
    dproj0 = jnp.concatenate([d_gate0, dgv, dr, dkk, dvv, dgq, dgk, dglow, dxw, dxa, pad], axis=1)
    d_w0 = row_blocks(_w0t_from_padded(_matmul("proj0_dw", dproj0, hn0, "tn", DW_COLS, d, BF16)))
    r_in0 = _sequencer_scatter("exchange_w_in0_grad", d_w0, 0)
    res = {}
    res["w_out0"] = tuple(a[None] for a in _adamw("adamw_w_out0", w_out0[0], r_out0, m_w_out0[0], v_w_out0[0], ADAM_COLS))
    res["w_in1"] = tuple(a.T[None] for a in _adamw("adamw_w_in1", w_in1[0].T, r_in1, m_w_in1[0].T, v_w_in1[0].T, ADAM_COLS))
    res["w_out1"] = tuple(a[None] for a in _adamw("adamw_w_out1", w_out1[0], r_out1, m_w_out1[0], v_w_out1[0], ADAM_COLS))
    grad_x, d_nw0 = _matmul_fused("proj0_dx_norm0", dproj0, w0t, "nn", [wide(xs), wide(dh1)], [nw0], [(d, F32)], [d], _norm_back)

    contrib = dict(
        norm_w=jnp.concatenate([d_nw0, d_nw1], axis=0), gla_gk_bias=d_gk_bias, gla_norm_w=d_gla_nw,
        rwkv_mu=jnp.concatenate([d_rp[0], d_rp[1], d_rp[2], d_rp[3][:, :rank], d_rp[4][:, :rank]], axis=1),
        rwkv_w0=d_rp[5], rwkv_a0=d_rp[7], rwkv_k_k=d_rp[9], rwkv_k_a=d_rp[10], rwkv_r_k=d_rp[11].reshape(RWKV_HEADS, RWKV_N),
        rwkv_ln_w=d_rp[12], rwkv_ln_b=d_rp[13], attn_sinks=d_sinks, final_norm_w=d_fw)
    rep_pack = _pack_small([contrib[n] for n in replicated] + [loss_part[:, :1]], rep_modes + ["flat"])

    d_b_in = jnp.concatenate([d_bq, d_bk, d_bv], axis=1)
    full_small = [d_gk_up[:gk_up.shape[0]], d_rp[6][:rank], d_rp[8][:rank], d_b_in, d_b_out]
    split_cols = lambda a: jnp.transpose(a.reshape(a.shape[0], N_DEV, -1), (1, 0, 2))
    small_parts = [split_cols(a) for a in full_small]
    small_pack = _pack_small(small_parts, sh_modes, lead=True)
    r_small, r_rep = _exchange([small_pack, rep_pack], ["scatter", "gather"])

    small_names = small_sharded + replicated
    slots = jnp.concatenate([r_small, r_rep], axis=1)
    as_2d = lambda a: a.reshape(1, -1) if a.ndim == 1 else a
    small_res, loss_row_out = _adamw_small(slots, sh_specs + rep_specs, [as_2d(weights[n]) for n in small_names],
                                           [as_2d(moms[n]) for n in small_names], [as_2d(vars_[n]) for n in small_names], loss_row)
    for n, vals in zip(small_names, small_res):
        res[n] = tuple(val.reshape(weights[n].shape) for val in vals)
    loss = loss_row_out[0, 0]
    my_idx = 4 * lax.axis_index("x") + 2 * lax.axis_index("y") + lax.axis_index("c")
    r_in0 = lax.dynamic_update_slice(r_in0, lax.dynamic_slice(d_w0, (my_idx, 0, 0), (1,) + d_w0.shape[1:]), (my_idx, 0, 0))
    res["w_in0"] = tuple(a.T[None] for a in _adamw("adamw_w_in0", w_in0[0].T, r_in0, m_w_in0[0].T, v_w_in0[0].T, ADAM_COLS))
    return (loss, grad_x[None], *[res[n][0] for n in names], *[res[n][1] for n in names],
            *[res[n][2] for n in names], *[res[n][3] for n in names])
```
